```python
import math
import jax, jax.numpy as jnp
from jax import lax
import numpy as np

D_MODEL = 1024
BATCH = 8
SEQ = 4096
DEPTH = 4

D_MIX = D_MODEL
LRU_WIDTH = D_MIX // 2
LRU_BLOCKS = 8
LRU_BLOCK_W = LRU_WIDTH // LRU_BLOCKS
LRU_C = 8.0
CONV_WIDTH = 4
CONV_PAD = (CONV_WIDTH // 2, CONV_WIDTH - 1 - CONV_WIDTH // 2)
N_HEADS = 8
N_KV_HEADS = 2
KV_GROUP = N_HEADS // N_KV_HEADS
HEAD_DIM = (D_MIX - LRU_WIDTH) // N_HEADS
ATT_WIDTH = N_HEADS * HEAD_DIM
KV_WIDTH = N_KV_HEADS * HEAD_DIM
WINDOW = 128
BLOCK = 128
N_BUCKETS = 32
MAX_DISTANCE = 128
D_FF = ((8 * D_MODEL // 3 + 255) // 256) * 256
FFN_RES = 0.5
EPS = 1e-6
NEG_INF = -1e30
D_IN = 2 * LRU_WIDTH + ATT_WIDTH + 2 * KV_WIDTH
SPLITS = (LRU_WIDTH, 2 * LRU_WIDTH, 2 * LRU_WIDTH + ATT_WIDTH, 2 * LRU_WIDTH + ATT_WIDTH + KV_WIDTH)

kernel_name = 'hymba_style_rglru_swa_macaron_encoder'


def rms_norm(x, g):
    x32 = x.astype(jnp.float32)
    y = x32 * lax.rsqrt(jnp.mean(x32 * x32, axis=-1, keepdims=True) + EPS)
    return (y * g.astype(jnp.float32)).astype(x.dtype)


def swiglu(x, w_gate, w_up, w_down):
    return (jax.nn.silu(x @ w_gate) * (x @ w_up)) @ w_down


def t5_buckets(rel):
    half = N_BUCKETS // 2
    max_exact = half // 2
    ret = (rel > 0).astype(jnp.int32) * half
    n = jnp.abs(rel)
    n_f = jnp.maximum(n, 1).astype(jnp.float32)
    large = max_exact + (jnp.log(n_f / max_exact) / math.log(MAX_DISTANCE / max_exact) * (half - max_exact)).astype(jnp.int32)
    large = jnp.minimum(large, half - 1)
    return ret + jnp.where(n < max_exact, n, large)


def band_layout(seq):
    nb = seq // BLOCK
    n_idx = jnp.arange(nb)[:, None, None]
    t = jnp.arange(BLOCK)[None, :, None]
    j = jnp.arange(3 * BLOCK)[None, None, :]
    rel = j - BLOCK - t
    key_pos = (n_idx - 1) * BLOCK + j
    mask = (jnp.abs(rel) <= WINDOW) & (key_pos >= 0) & (key_pos < seq)
    return t5_buckets(rel[0]), mask


def band_windows(t, nb):
    b = t.shape[0]
    tp = jnp.pad(t, ((0, 0), (BLOCK, BLOCK), (0, 0), (0, 0)))
    tb = tp.reshape(b, nb + 2, BLOCK, N_KV_HEADS, HEAD_DIM)
    return jnp.concatenate([tb[:, :-2], tb[:, 1:-1], tb[:, 2:]], axis=2)


def windowed_gqa(q, k, v, sink, rel_bias):
    b, s = q.shape[0], q.shape[1]
    nb = s // BLOCK
    qb = q.reshape(b, nb, BLOCK, N_KV_HEADS, KV_GROUP, HEAD_DIM) * (HEAD_DIM ** -0.5)
    kw = band_windows(k.reshape(b, s, N_KV_HEADS, HEAD_DIM), nb)
    vw = band_windows(v.reshape(b, s, N_KV_HEADS, HEAD_DIM), nb)
    buckets, mask = band_layout(s)
    bias = rel_bias[buckets].astype(jnp.float32)
    bias = jnp.transpose(bias, (2, 0, 1)).reshape(N_KV_HEADS, KV_GROUP, BLOCK, 3 * BLOCK)
    logits = jnp.einsum('bnqkgd,bnjkd->bnkgqj', qb, kw).astype(jnp.float32) + bias
    logits = jnp.where(mask[None, :, None, None], logits, NEG_INF)
    sink32 = sink.astype(jnp.float32).reshape(1, 1, N_KV_HEADS, KV_GROUP, 1, 1)
    m = jnp.maximum(jnp.max(logits, axis=-1, keepdims=True), sink32)
    p = jnp.exp(logits - m)
    p = p / (jnp.sum(p, axis=-1, keepdims=True) + jnp.exp(sink32 - m))
    o = jnp.einsum('bnkgqj,bnjkd->bnqkgd', p.astype(vw.dtype), vw)
    return o.reshape(b, s, ATT_WIDTH)


def _linear_combine(left, right):
    a_l, b_l = left
    a_r, b_r = right
    return a_l * a_r, a_r * b_l + b_r


def rg_lru_direction(xc, w_a, b_a, w_x, b_x, lam, reverse):
    b, s = xc.shape[0], xc.shape[1]
    xb = xc.reshape(b, s, LRU_BLOCKS, LRU_BLOCK_W)
    r = jax.nn.sigmoid(jnp.einsum('bsnc,ncd->bsnd', xb, w_a).reshape(b, s, LRU_WIDTH).astype(jnp.float32) + b_a.astype(jnp.float32))
    i = jax.nn.sigmoid(jnp.einsum('bsnc,ncd->bsnd', xb, w_x).reshape(b, s, LRU_WIDTH).astype(jnp.float32) + b_x.astype(jnp.float32))
    log_a = -LRU_C * jax.nn.softplus(-lam.astype(jnp.float32)) * r
    a = jnp.exp(log_a)
    u = jnp.sqrt(-jnp.expm1(2.0 * log_a)) * (i * xc.astype(jnp.float32))
    _, h = lax.associative_scan(_linear_combine, (a, u), axis=1, reverse=reverse)
    return h


def recurrent_group(xr, gate, conv_w, conv_b, w_a, b_a, w_x, b_x, lam):
    xc = lax.conv_general_dilated(xr, conv_w[:, None, :], window_strides=(1,), padding=[CONV_PAD],
                                  dimension_numbers=('NWC', 'WIO', 'NWC'), feature_group_count=LRU_WIDTH) + conv_b
    h = (rg_lru_direction(xc, w_a[0], b_a[0], w_x[0], b_x[0], lam[0], False)
         + rg_lru_direction(xc, w_a[1], b_a[1], w_x[1], b_x[1], lam[1], True))
    return (jax.nn.gelu(gate.astype(jnp.float32)) * h).astype(xr.dtype)


def _fwd_setup_inputs(seed: int = 0) -> dict:
    key = jax.random.key(seed)
    ks = jax.random.split(key, 32)
    f32 = jnp.float32

    def nrm(k, shape, fan_in):
        return jax.random.normal(k, shape, f32) * (fan_in ** -0.5)

    def gain(k, shape):
        return 1.0 + 0.02 * jax.random.normal(k, shape, f32)

    def small(k, shape, scale=0.01):
        return scale * jax.random.normal(k, shape, f32)

    u = jax.random.uniform(ks[14], (DEPTH, 2, LRU_WIDTH), f32, minval=0.9, maxval=0.999)
    a0 = u ** (1.0 / LRU_C)
    lru_lambda = jnp.log(a0) - jnp.log1p(-a0)
    return {
        'x': jax.random.normal(ks[0], (BATCH, SEQ, D_MODEL), f32),
        'ffn1_norm': gain(ks[1], (DEPTH, D_MODEL)),
        'ffn1_w_gate': nrm(ks[2], (DEPTH, D_MODEL, D_FF), D_MODEL),
        'ffn1_w_up': nrm(ks[3], (DEPTH, D_MODEL, D_FF), D_MODEL),
        'ffn1_w_down': nrm(ks[4], (DEPTH, D_FF, D_MODEL), D_FF),
        'mix_norm': gain(ks[5], (DEPTH, D_MODEL)),
        'w_in': nrm(ks[6], (DEPTH, D_MODEL, D_IN), D_MODEL),
        'conv_w': nrm(ks[7], (DEPTH, CONV_WIDTH, LRU_WIDTH), CONV_WIDTH),
        'conv_b': small(ks[8], (DEPTH, LRU_WIDTH)),
        'lru_w_a': nrm(ks[9], (DEPTH, 2, LRU_BLOCKS, LRU_BLOCK_W, LRU_BLOCK_W), LRU_BLOCK_W),
        'lru_b_a': small(ks[10], (DEPTH, 2, LRU_WIDTH), 0.1),
        'lru_w_x': nrm(ks[11], (DEPTH, 2, LRU_BLOCKS, LRU_BLOCK_W, LRU_BLOCK_W), LRU_BLOCK_W),
        'lru_b_x': small(ks[12], (DEPTH, 2, LRU_WIDTH), 0.1),
        'lru_lambda': lru_lambda,
        'attn_sink': 0.5 * jax.random.normal(ks[15], (DEPTH, N_HEADS), f32),
        'rel_bias': 0.2 * jax.random.normal(ks[16], (N_BUCKETS, N_HEADS), f32),
        'lru_out_norm': gain(ks[17], (DEPTH, LRU_WIDTH)),
        'attn_out_norm': gain(ks[18], (DEPTH, ATT_WIDTH)),
        'w_out': nrm(ks[19], (DEPTH, D_MIX, D_MODEL), D_MIX),
        'ffn2_norm': gain(ks[20], (DEPTH, D_MODEL)),
        'ffn2_w_gate': nrm(ks[21], (DEPTH, D_MODEL, D_FF), D_MODEL),
        'ffn2_w_up': nrm(ks[22], (DEPTH, D_MODEL, D_FF), D_MODEL),
        'ffn2_w_down': nrm(ks[23], (DEPTH, D_FF, D_MODEL), D_FF),
        'final_norm': gain(ks[24], (D_MODEL,)),
    }


def _fwd_reference(x, ffn1_norm, ffn1_w_gate, ffn1_w_up, ffn1_w_down, mix_norm, w_in, conv_w, conv_b,
              lru_w_a, lru_b_a, lru_w_x, lru_b_x, lru_lambda, attn_sink, rel_bias,
              lru_out_norm, attn_out_norm, w_out, ffn2_norm, ffn2_w_gate, ffn2_w_up, ffn2_w_down,
              final_norm):
    for l in range(DEPTH):
        x = x + FFN_RES * swiglu(rms_norm(x, ffn1_norm[l]), ffn1_w_gate[l], ffn1_w_up[l], ffn1_w_down[l])
        h = rms_norm(x, mix_norm[l])
        proj = h @ w_in[l]
        xr, gate, q, k, v = jnp.split(proj, SPLITS, axis=-1)
        y_rec = recurrent_group(xr, gate, conv_w[l], conv_b[l], lru_w_a[l], lru_b_a[l],
                                lru_w_x[l], lru_b_x[l], lru_lambda[l])
        y_att = windowed_gqa(q, k, v, attn_sink[l], rel_bias)
        y = jnp.concatenate([rms_norm(y_rec, lru_out_norm[l]), rms_norm(y_att, attn_out_norm[l])], axis=-1)
        x = x + y @ w_out[l]
        x = x + FFN_RES * swiglu(rms_norm(x, ffn2_norm[l]), ffn2_w_gate[l], ffn2_w_up[l], ffn2_w_down[l])
    return rms_norm(x, final_norm)


import jax as _jax
import jax.numpy as _jnp

TWIN_FORMAT = 'train_step'
FWD_PARAMS = ['x', 'ffn1_norm', 'ffn1_w_gate', 'ffn1_w_up', 'ffn1_w_down', 'mix_norm', 'w_in', 'conv_w', 'conv_b', 'lru_w_a', 'lru_b_a', 'lru_w_x', 'lru_b_x', 'lru_lambda', 'attn_sink', 'rel_bias', 'lru_out_norm', 'attn_out_norm', 'w_out', 'ffn2_norm', 'ffn2_w_gate', 'ffn2_w_up', 'ffn2_w_down', 'final_norm']
TWIN_WEIGHTS = ['ffn1_norm', 'ffn1_w_gate', 'ffn1_w_up', 'ffn1_w_down', 'mix_norm', 'w_in', 'conv_w', 'conv_b', 'lru_w_a', 'lru_b_a', 'lru_w_x', 'lru_b_x', 'lru_lambda', 'attn_sink', 'rel_bias', 'lru_out_norm', 'attn_out_norm', 'w_out', 'ffn2_norm', 'ffn2_w_gate', 'ffn2_w_up', 'ffn2_w_down', 'final_norm']
TWIN_DIFF_INPUT = 'x'
TWIN_INPUTS = ['x', 'ffn1_norm', 'ffn1_w_gate', 'ffn1_w_up', 'ffn1_w_down', 'mix_norm', 'w_in', 'conv_w', 'conv_b', 'lru_w_a', 'lru_b_a', 'lru_w_x', 'lru_b_x', 'lru_lambda', 'attn_sink', 'rel_bias', 'lru_out_norm', 'attn_out_norm', 'w_out', 'ffn2_norm', 'ffn2_w_gate', 'ffn2_w_up', 'ffn2_w_down', 'final_norm', 'loss_target', 'm_ffn1_norm', 'm_ffn1_w_gate', 'm_ffn1_w_up', 'm_ffn1_w_down', 'm_mix_norm', 'm_w_in', 'm_conv_w', 'm_conv_b', 'm_lru_w_a', 'm_lru_b_a', 'm_lru_w_x', 'm_lru_b_x', 'm_lru_lambda', 'm_attn_sink', 'm_rel_bias', 'm_lru_out_norm', 'm_attn_out_norm', 'm_w_out', 'm_ffn2_norm', 'm_ffn2_w_gate', 'm_ffn2_w_up', 'm_ffn2_w_down', 'm_final_norm', 'v_ffn1_norm', 'v_ffn1_w_gate', 'v_ffn1_w_up', 'v_ffn1_w_down', 'v_mix_norm', 'v_w_in', 'v_conv_w', 'v_conv_b', 'v_lru_w_a', 'v_lru_b_a', 'v_lru_w_x', 'v_lru_b_x', 'v_lru_lambda', 'v_attn_sink', 'v_rel_bias', 'v_lru_out_norm', 'v_attn_out_norm', 'v_w_out', 'v_ffn2_norm', 'v_ffn2_w_gate', 'v_ffn2_w_up', 'v_ffn2_w_down', 'v_final_norm']
TWIN_OUTPUTS = ['loss', 'grad_x', 'grad_ffn1_norm', 'grad_ffn1_w_gate', 'grad_ffn1_w_up', 'grad_ffn1_w_down', 'grad_mix_norm', 'grad_w_in', 'grad_conv_w', 'grad_conv_b', 'grad_lru_w_a', 'grad_lru_b_a', 'grad_lru_w_x', 'grad_lru_b_x', 'grad_lru_lambda', 'grad_attn_sink', 'grad_rel_bias', 'grad_lru_out_norm', 'grad_attn_out_norm', 'grad_w_out', 'grad_ffn2_norm', 'grad_ffn2_w_gate', 'grad_ffn2_w_up', 'grad_ffn2_w_down', 'grad_final_norm', 'delta_ffn1_norm', 'delta_ffn1_w_gate', 'delta_ffn1_w_up', 'delta_ffn1_w_down', 'delta_mix_norm', 'delta_w_in', 'delta_conv_w', 'delta_conv_b', 'delta_lru_w_a', 'delta_lru_b_a', 'delta_lru_w_x', 'delta_lru_b_x', 'delta_lru_lambda', 'delta_attn_sink', 'delta_rel_bias', 'delta_lru_out_norm', 'delta_attn_out_norm', 'delta_w_out', 'delta_ffn2_norm', 'delta_ffn2_w_gate', 'delta_ffn2_w_up', 'delta_ffn2_w_down', 'delta_final_norm', 'new_m_ffn1_norm', 'new_m_ffn1_w_gate', 'new_m_ffn1_w_up', 'new_m_ffn1_w_down', 'new_m_mix_norm', 'new_m_w_in', 'new_m_conv_w', 'new_m_conv_b', 'new_m_lru_w_a', 'new_m_lru_b_a', 'new_m_lru_w_x', 'new_m_lru_b_x', 'new_m_lru_lambda', 'new_m_attn_sink', 'new_m_rel_bias', 'new_m_lru_out_norm', 'new_m_attn_out_norm', 'new_m_w_out', 'new_m_ffn2_norm', 'new_m_ffn2_w_gate', 'new_m_ffn2_w_up', 'new_m_ffn2_w_down', 'new_m_final_norm', 'new_v_ffn1_norm', 'new_v_ffn1_w_gate', 'new_v_ffn1_w_up', 'new_v_ffn1_w_down', 'new_v_mix_norm', 'new_v_w_in', 'new_v_conv_w', 'new_v_conv_b', 'new_v_lru_w_a', 'new_v_lru_b_a', 'new_v_lru_w_x', 'new_v_lru_b_x', 'new_v_lru_lambda', 'new_v_attn_sink', 'new_v_rel_bias', 'new_v_lru_out_norm', 'new_v_attn_out_norm', 'new_v_w_out', 'new_v_ffn2_norm', 'new_v_ffn2_w_gate', 'new_v_ffn2_w_up', 'new_v_ffn2_w_down', 'new_v_final_norm']
TWIN_LEAF_KINDS = {'loss': 'loss', 'grad_x': 'grad_x', 'grad_ffn1_norm': 'grad_w', 'grad_ffn1_w_gate': 'grad_w', 'grad_ffn1_w_up': 'grad_w', 'grad_ffn1_w_down': 'grad_w', 'grad_mix_norm': 'grad_w', 'grad_w_in': 'grad_w', 'grad_conv_w': 'grad_w', 'grad_conv_b': 'grad_w', 'grad_lru_w_a': 'grad_w', 'grad_lru_b_a': 'grad_w', 'grad_lru_w_x': 'grad_w', 'grad_lru_b_x': 'grad_w', 'grad_lru_lambda': 'grad_w', 'grad_attn_sink': 'grad_w', 'grad_rel_bias': 'grad_w', 'grad_lru_out_norm': 'grad_w', 'grad_attn_out_norm': 'grad_w', 'grad_w_out': 'grad_w', 'grad_ffn2_norm': 'grad_w', 'grad_ffn2_w_gate': 'grad_w', 'grad_ffn2_w_up': 'grad_w', 'grad_ffn2_w_down': 'grad_w', 'grad_final_norm': 'grad_w', 'delta_ffn1_norm': 'delta_w', 'delta_ffn1_w_gate': 'delta_w', 'delta_ffn1_w_up': 'delta_w', 'delta_ffn1_w_down': 'delta_w', 'delta_mix_norm': 'delta_w', 'delta_w_in': 'delta_w', 'delta_conv_w': 'delta_w', 'delta_conv_b': 'delta_w', 'delta_lru_w_a': 'delta_w', 'delta_lru_b_a': 'delta_w', 'delta_lru_w_x': 'delta_w', 'delta_lru_b_x': 'delta_w', 'delta_lru_lambda': 'delta_w', 'delta_attn_sink': 'delta_w', 'delta_rel_bias': 'delta_w', 'delta_lru_out_norm': 'delta_w', 'delta_attn_out_norm': 'delta_w', 'delta_w_out': 'delta_w', 'delta_ffn2_norm': 'delta_w', 'delta_ffn2_w_gate': 'delta_w', 'delta_ffn2_w_up': 'delta_w', 'delta_ffn2_w_down': 'delta_w', 'delta_final_norm': 'delta_w', 'new_m_ffn1_norm': 'new_m', 'new_m_ffn1_w_gate': 'new_m', 'new_m_ffn1_w_up': 'new_m', 'new_m_ffn1_w_down': 'new_m', 'new_m_mix_norm': 'new_m', 'new_m_w_in': 'new_m', 'new_m_conv_w': 'new_m', 'new_m_conv_b': 'new_m', 'new_m_lru_w_a': 'new_m', 'new_m_lru_b_a': 'new_m', 'new_m_lru_w_x': 'new_m', 'new_m_lru_b_x': 'new_m', 'new_m_lru_lambda': 'new_m', 'new_m_attn_sink': 'new_m', 'new_m_rel_bias': 'new_m', 'new_m_lru_out_norm': 'new_m', 'new_m_attn_out_norm': 'new_m', 'new_m_w_out': 'new_m', 'new_m_ffn2_norm': 'new_m', 'new_m_ffn2_w_gate': 'new_m', 'new_m_ffn2_w_up': 'new_m', 'new_m_ffn2_w_down': 'new_m', 'new_m_final_norm': 'new_m', 'new_v_ffn1_norm': 'new_v', 'new_v_ffn1_w_gate': 'new_v', 'new_v_ffn1_w_up': 'new_v', 'new_v_ffn1_w_down': 'new_v', 'new_v_mix_norm': 'new_v', 'new_v_w_in': 'new_v', 'new_v_conv_w': 'new_v', 'new_v_conv_b': 'new_v', 'new_v_lru_w_a': 'new_v', 'new_v_lru_b_a': 'new_v', 'new_v_lru_w_x': 'new_v', 'new_v_lru_b_x': 'new_v', 'new_v_lru_lambda': 'new_v', 'new_v_attn_sink': 'new_v', 'new_v_rel_bias': 'new_v', 'new_v_lru_out_norm': 'new_v', 'new_v_attn_out_norm': 'new_v', 'new_v_w_out': 'new_v', 'new_v_ffn2_norm': 'new_v', 'new_v_ffn2_w_gate': 'new_v', 'new_v_ffn2_w_up': 'new_v', 'new_v_ffn2_w_down': 'new_v', 'new_v_final_norm': 'new_v'}


def _forward(args):
    return _fwd_reference(*[args[k] for k in FWD_PARAMS])


def _output_shape():
    def fwd():
        inp = _fwd_setup_inputs(0)
        return _fwd_reference(*[inp[k] for k in FWD_PARAMS])
    out = _jax.eval_shape(fwd)
    return out.shape, out.dtype

N_MICROBATCH = 1
ADAM_LR = 0.001
ADAM_B1 = 0.9
ADAM_B2 = 0.999
ADAM_EPS = 1e-08
ADAM_WD = 0.01
ADAM_STEP = 10
PER_EXAMPLE_BATCH_AXIS = {'x': 0, 'loss_target': 0}
SHARED_INPUTS = []
_WEIGHT_DTYPES = {'ffn1_norm': _jnp.float32, 'ffn1_w_gate': _jnp.float32, 'ffn1_w_up': _jnp.float32, 'ffn1_w_down': _jnp.float32, 'mix_norm': _jnp.float32, 'w_in': _jnp.float32, 'conv_w': _jnp.float32, 'conv_b': _jnp.float32, 'lru_w_a': _jnp.float32, 'lru_b_a': _jnp.float32, 'lru_w_x': _jnp.float32, 'lru_b_x': _jnp.float32, 'lru_lambda': _jnp.float32, 'attn_sink': _jnp.float32, 'rel_bias': _jnp.float32, 'lru_out_norm': _jnp.float32, 'attn_out_norm': _jnp.float32, 'w_out': _jnp.float32, 'ffn2_norm': _jnp.float32, 'ffn2_w_gate': _jnp.float32, 'ffn2_w_up': _jnp.float32, 'ffn2_w_down': _jnp.float32, 'final_norm': _jnp.float32}
MOMENT_SCALE = {'ffn1_norm': 7.140252e-02, 'ffn1_w_gate': 2.998319e-02, 'ffn1_w_up': 2.944777e-02, 'ffn1_w_down': 4.899691e-02, 'mix_norm': 1.717538e-01, 'w_in': 1.285496e-01, 'conv_w': 1.355834e-01, 'conv_b': 1.855877e+00, 'lru_w_a': 3.212456e-02, 'lru_b_a': 2.489678e-02, 'lru_w_x': 5.866937e-02, 'lru_b_x': 2.611877e-02, 'lru_lambda': 4.509076e-02, 'attn_sink': 4.395992e-03, 'rel_bias': 2.153356e-01, 'lru_out_norm': 1.384435e-01, 'attn_out_norm': 1.568188e-01, 'w_out': 1.468994e-01, 'ffn2_norm': 4.162982e-02, 'ffn2_w_gate': 1.787369e-02, 'ffn2_w_up': 1.794502e-02, 'ffn2_w_down': 2.970342e-02, 'final_norm': 3.196083e+01}


def _to_microbatches(a, axis):
    t = _jnp.moveaxis(a, axis, 0)
    t = t.reshape((N_MICROBATCH, t.shape[0] // N_MICROBATCH) + t.shape[1:])
    return _jnp.moveaxis(t, 1, axis + 1)


def setup_inputs(seed: int = 0) -> dict:
    inp = _fwd_setup_inputs(seed)
    key = _jax.random.fold_in(_jax.random.key(seed), 7919)
    shape, _ = _output_shape()
    out = dict(inp)
    out["loss_target"] = _jax.random.normal(_jax.random.fold_in(key, 0), shape, _jnp.float32)
    for i, name in enumerate(TWIN_WEIGHTS):
        w = inp[name].astype(_jnp.float32)
        if MOMENT_SCALE is None:
            s = _jnp.sqrt(_jnp.mean(_jnp.square(w)) + 1e-30)
        else:
            s = MOMENT_SCALE[name]
        km, kv = _jax.random.split(_jax.random.fold_in(key, i + 1))
        out[name] = w
        out["m_" + name] = s * _jax.random.normal(km, w.shape, _jnp.float32)
        out["v_" + name] = (s * s) * _jax.random.uniform(kv, w.shape, _jnp.float32, 0.5, 1.5)
    if N_MICROBATCH > 1:
        for name, axis in PER_EXAMPLE_BATCH_AXIS.items():
            out[name] = _to_microbatches(out[name], axis)
    return {'x': out['x'], 'ffn1_norm': out['ffn1_norm'], 'ffn1_w_gate': out['ffn1_w_gate'], 'ffn1_w_up': out['ffn1_w_up'], 'ffn1_w_down': out['ffn1_w_down'], 'mix_norm': out['mix_norm'], 'w_in': out['w_in'], 'conv_w': out['conv_w'], 'conv_b': out['conv_b'], 'lru_w_a': out['lru_w_a'], 'lru_b_a': out['lru_b_a'], 'lru_w_x': out['lru_w_x'], 'lru_b_x': out['lru_b_x'], 'lru_lambda': out['lru_lambda'], 'attn_sink': out['attn_sink'], 'rel_bias': out['rel_bias'], 'lru_out_norm': out['lru_out_norm'], 'attn_out_norm': out['attn_out_norm'], 'w_out': out['w_out'], 'ffn2_norm': out['ffn2_norm'], 'ffn2_w_gate': out['ffn2_w_gate'], 'ffn2_w_up': out['ffn2_w_up'], 'ffn2_w_down': out['ffn2_w_down'], 'final_norm': out['final_norm'], 'loss_target': out['loss_target'], 'm_ffn1_norm': out['m_ffn1_norm'], 'm_ffn1_w_gate': out['m_ffn1_w_gate'], 'm_ffn1_w_up': out['m_ffn1_w_up'], 'm_ffn1_w_down': out['m_ffn1_w_down'], 'm_mix_norm': out['m_mix_norm'], 'm_w_in': out['m_w_in'], 'm_conv_w': out['m_conv_w'], 'm_conv_b': out['m_conv_b'], 'm_lru_w_a': out['m_lru_w_a'], 'm_lru_b_a': out['m_lru_b_a'], 'm_lru_w_x': out['m_lru_w_x'], 'm_lru_b_x': out['m_lru_b_x'], 'm_lru_lambda': out['m_lru_lambda'], 'm_attn_sink': out['m_attn_sink'], 'm_rel_bias': out['m_rel_bias'], 'm_lru_out_norm': out['m_lru_out_norm'], 'm_attn_out_norm': out['m_attn_out_norm'], 'm_w_out': out['m_w_out'], 'm_ffn2_norm': out['m_ffn2_norm'], 'm_ffn2_w_gate': out['m_ffn2_w_gate'], 'm_ffn2_w_up': out['m_ffn2_w_up'], 'm_ffn2_w_down': out['m_ffn2_w_down'], 'm_final_norm': out['m_final_norm'], 'v_ffn1_norm': out['v_ffn1_norm'], 'v_ffn1_w_gate': out['v_ffn1_w_gate'], 'v_ffn1_w_up': out['v_ffn1_w_up'], 'v_ffn1_w_down': out['v_ffn1_w_down'], 'v_mix_norm': out['v_mix_norm'], 'v_w_in': out['v_w_in'], 'v_conv_w': out['v_conv_w'], 'v_conv_b': out['v_conv_b'], 'v_lru_w_a': out['v_lru_w_a'], 'v_lru_b_a': out['v_lru_b_a'], 'v_lru_w_x': out['v_lru_w_x'], 'v_lru_b_x': out['v_lru_b_x'], 'v_lru_lambda': out['v_lru_lambda'], 'v_attn_sink': out['v_attn_sink'], 'v_rel_bias': out['v_rel_bias'], 'v_lru_out_norm': out['v_lru_out_norm'], 'v_attn_out_norm': out['v_attn_out_norm'], 'v_w_out': out['v_w_out'], 'v_ffn2_norm': out['v_ffn2_norm'], 'v_ffn2_w_gate': out['v_ffn2_w_gate'], 'v_ffn2_w_up': out['v_ffn2_w_up'], 'v_ffn2_w_down': out['v_ffn2_w_down'], 'v_final_norm': out['v_final_norm']}


def _loss(weights, diff, rest, loss_target):
    with _jax.named_scope("forward"):
        args = {**rest, TWIN_DIFF_INPUT: diff, **{k: w.astype(_WEIGHT_DTYPES[k]) for k, w in weights.items()}}
        y = _forward(args)
    with _jax.named_scope("loss_head"):
        err = _jnp.square(y.astype(_jnp.float32) - loss_target)
        return 0.5 * _jnp.sum(_jnp.mean(err, axis=-1)) if err.ndim else 0.5 * err


def _adamw(w, g, m, v):
    m = ADAM_B1 * m + (1.0 - ADAM_B1) * g
    v = ADAM_B2 * v + (1.0 - ADAM_B2) * _jnp.square(g)
    m_hat = m / (1.0 - ADAM_B1 ** ADAM_STEP)
    v_hat = v / (1.0 - ADAM_B2 ** ADAM_STEP)
    delta = -ADAM_LR * (m_hat / (_jnp.sqrt(v_hat) + ADAM_EPS) + ADAM_WD * w)
    return delta, m, v


def reference(x, ffn1_norm, ffn1_w_gate, ffn1_w_up, ffn1_w_down, mix_norm, w_in, conv_w, conv_b, lru_w_a, lru_b_a, lru_w_x, lru_b_x, lru_lambda, attn_sink, rel_bias, lru_out_norm, attn_out_norm, w_out, ffn2_norm, ffn2_w_gate, ffn2_w_up, ffn2_w_down, final_norm, loss_target, m_ffn1_norm, m_ffn1_w_gate, m_ffn1_w_up, m_ffn1_w_down, m_mix_norm, m_w_in, m_conv_w, m_conv_b, m_lru_w_a, m_lru_b_a, m_lru_w_x, m_lru_b_x, m_lru_lambda, m_attn_sink, m_rel_bias, m_lru_out_norm, m_attn_out_norm, m_w_out, m_ffn2_norm, m_ffn2_w_gate, m_ffn2_w_up, m_ffn2_w_down, m_final_norm, v_ffn1_norm, v_ffn1_w_gate, v_ffn1_w_up, v_ffn1_w_down, v_mix_norm, v_w_in, v_conv_w, v_conv_b, v_lru_w_a, v_lru_b_a, v_lru_w_x, v_lru_b_x, v_lru_lambda, v_attn_sink, v_rel_bias, v_lru_out_norm, v_attn_out_norm, v_w_out, v_ffn2_norm, v_ffn2_w_gate, v_ffn2_w_up, v_ffn2_w_down, v_final_norm):
    given = dict(x=x, ffn1_norm=ffn1_norm, ffn1_w_gate=ffn1_w_gate, ffn1_w_up=ffn1_w_up, ffn1_w_down=ffn1_w_down, mix_norm=mix_norm, w_in=w_in, conv_w=conv_w, conv_b=conv_b, lru_w_a=lru_w_a, lru_b_a=lru_b_a, lru_w_x=lru_w_x, lru_b_x=lru_b_x, lru_lambda=lru_lambda, attn_sink=attn_sink, rel_bias=rel_bias, lru_out_norm=lru_out_norm, attn_out_norm=attn_out_norm, w_out=w_out, ffn2_norm=ffn2_norm, ffn2_w_gate=ffn2_w_gate, ffn2_w_up=ffn2_w_up, ffn2_w_down=ffn2_w_down, final_norm=final_norm, loss_target=loss_target, m_ffn1_norm=m_ffn1_norm, m_ffn1_w_gate=m_ffn1_w_gate, m_ffn1_w_up=m_ffn1_w_up, m_ffn1_w_down=m_ffn1_w_down, m_mix_norm=m_mix_norm, m_w_in=m_w_in, m_conv_w=m_conv_w, m_conv_b=m_conv_b, m_lru_w_a=m_lru_w_a, m_lru_b_a=m_lru_b_a, m_lru_w_x=m_lru_w_x, m_lru_b_x=m_lru_b_x, m_lru_lambda=m_lru_lambda, m_attn_sink=m_attn_sink, m_rel_bias=m_rel_bias, m_lru_out_norm=m_lru_out_norm, m_attn_out_norm=m_attn_out_norm, m_w_out=m_w_out, m_ffn2_norm=m_ffn2_norm, m_ffn2_w_gate=m_ffn2_w_gate, m_ffn2_w_up=m_ffn2_w_up, m_ffn2_w_down=m_ffn2_w_down, m_final_norm=m_final_norm, v_ffn1_norm=v_ffn1_norm, v_ffn1_w_gate=v_ffn1_w_gate, v_ffn1_w_up=v_ffn1_w_up, v_ffn1_w_down=v_ffn1_w_down, v_mix_norm=v_mix_norm, v_w_in=v_w_in, v_conv_w=v_conv_w, v_conv_b=v_conv_b, v_lru_w_a=v_lru_w_a, v_lru_b_a=v_lru_b_a, v_lru_w_x=v_lru_w_x, v_lru_b_x=v_lru_b_x, v_lru_lambda=v_lru_lambda, v_attn_sink=v_attn_sink, v_rel_bias=v_rel_bias, v_lru_out_norm=v_lru_out_norm, v_attn_out_norm=v_attn_out_norm, v_w_out=v_w_out, v_ffn2_norm=v_ffn2_norm, v_ffn2_w_gate=v_ffn2_w_gate, v_ffn2_w_up=v_ffn2_w_up, v_ffn2_w_down=v_ffn2_w_down, v_final_norm=v_final_norm)
    weights = {n: given[n] for n in TWIN_WEIGHTS}
    shared = {n: given[n] for n in SHARED_INPUTS}
    per_example = {n: given[n] for n in ['x']}
    grad_fn = _jax.value_and_grad(_loss, argnums=(0, 1))

    def one_microbatch(ex, loss_target):
        ex = dict(ex)
        diff = ex.pop(TWIN_DIFF_INPUT)
        return grad_fn(weights, diff, {**shared, **ex}, loss_target)

    if N_MICROBATCH == 1:
        loss, (grad_w, grad_x) = one_microbatch(per_example, given["loss_target"])
    else:
        def body(carry, xs):
            loss_sum, grad_sum = carry
            l_k, (gw_k, gx_k) = one_microbatch(xs[0], xs[1])
            with _jax.named_scope("update"):
                return (loss_sum + l_k, _jax.tree.map(_jnp.add, grad_sum, gw_k)), gx_k

        init = (_jnp.zeros((), _jnp.float32), _jax.tree.map(_jnp.zeros_like, weights))
        (loss, grad_w), grad_x = _jax.lax.scan(body, init, (per_example, given["loss_target"]))
    with _jax.named_scope("update"):
        delta_w, new_m, new_v = {}, {}, {}
        for n in TWIN_WEIGHTS:
            delta_w[n], new_m[n], new_v[n] = _adamw(weights[n], grad_w[n], given["m_" + n], given["v_" + n])
    return (loss, grad_x, *[grad_w[n] for n in TWIN_WEIGHTS], *[delta_w[n] for n in TWIN_WEIGHTS],
            *[new_m[n] for n in TWIN_WEIGHTS], *[new_v[n] for n in TWIN_WEIGHTS])
```

```python
import functools
import math

import jax
import jax.numpy as jnp
import numpy as np
from jax import lax
from jax.experimental import pallas as pl
from jax.experimental.pallas import tpu as pltpu

BF = jnp.bfloat16
F32 = jnp.float32
SDS = jax.ShapeDtypeStruct
MESH = pl.DeviceIdType.MESH
ANY = pl.BlockSpec(memory_space=pl.ANY)

N_CHIPS = 4
N_HEADS = 8
N_KV_HEADS = 2
KV_GROUP = N_HEADS // N_KV_HEADS
HEAD_DIM = 64
BLOCK = 128
WINDOW = 128
N_BUCKETS = 32
MAX_DISTANCE = 128
LRU_C = 8.0
CONV_WIDTH = 4
LANE = 128
SCAN_SEGMENTS = 8
EPS = 1e-6
FFN_RES = 0.5
NEG_INF = -1e30
ADAM_LR = 0.001
ADAM_B1 = 0.9
ADAM_B2 = 0.999
ADAM_EPS = 1e-08
ADAM_WD = 0.01
ADAM_STEP = 10
VMEM_LIMIT = 60000 * 1024
GELU_C = math.sqrt(2.0 / math.pi)


def dot_nn(a, b):
    return lax.dot_general(a, b, (((1,), (0,)), ((), ())), preferred_element_type=F32)


def dot_nt(a, b):
    return lax.dot_general(a, b, (((1,), (1,)), ((), ())), preferred_element_type=F32)


def dot_tn(a, b):
    return lax.dot_general(a, b, (((0,), (0,)), ((), ())), preferred_element_type=F32)


def _cparams(**kw):
    return pltpu.CompilerParams(vmem_limit_bytes=VMEM_LIMIT, **kw)


class Layout:
    def __init__(self, depth, d_model, d_ff, d_in):
        self.depth = depth
        self.fh = d_ff // (2 * N_CHIPS)
        self.ih = d_in // (2 * N_CHIPS)
        self.oh = d_model // (2 * N_CHIPS)
        self.win0 = 0
        self.wout0 = -(-(self.ih * depth) // self.oh) * self.oh
        self.ffn0 = -(-(self.wout0 + self.oh * depth) // self.fh) * self.fh
        self.rows = self.ffn0 + self.fh * 6 * depth

    def win_row(self, l):
        return self.win0 + self.ih * l

    def wout_row(self, l):
        return self.wout0 + self.oh * l

    def ffn_row(self, l, m):
        return self.ffn0 + self.fh * (6 * l + m)

    def win_blk(self, l):
        return self.win_row(l) // self.ih

    def wout_blk(self, l):
        return self.wout_row(l) // self.oh

    def ffn_blk(self, l, m):
        return self.ffn_row(l, m) // self.fh


def _row_chunk(rows, target, step=16):
    best = rows
    for c in range(step, min(rows, target) + 1, step):
        if rows % c == 0:
            best = c
    return best


def _mesh_pos():
    return lax.axis_index("x"), lax.axis_index("y"), lax.axis_index("c")


def _rcopy(src, dst, ssem, rsem, dev):
    return pltpu.make_async_remote_copy(src_ref=src, dst_ref=dst, send_sem=ssem, recv_sem=rsem,
                                        device_id=dev, device_id_type=MESH)


def gather_weights(wshard, sshard):
    _, rh, d = wshard.shape
    nch = 4 if rh % 64 == 0 else 1
    cr = rh // nch
    n_ici = 3 * nch

    def body(w_ref, s_ref, wf_ref, sf_ref, lsem, ssem_i, rsem_i, ssem_f, rsem_f, ssem_s, rsem_s):
        x, y, c = _mesh_pos()
        k = 2 * x + y
        peers = [(1 - x, y), (x, 1 - y), (1 - x, 1 - y)]
        own = pltpu.make_async_copy(w_ref, wf_ref.at[k], lsem.at[0])
        own_s = pltpu.make_async_copy(s_ref, sf_ref.at[k], lsem.at[1])
        own.start()
        own_s.start()

        def ici(j, q):
            rows = pl.ds(q * cr, cr)
            px, py = peers[j]
            return _rcopy(w_ref.at[c, rows], wf_ref.at[k, c, rows], ssem_i.at[j * nch + q],
                          rsem_i.at[j * nch + q], (px, py, c))

        def landed(j, q):
            rows = pl.ds(q * cr, cr)
            px, py = peers[j]
            kp = 2 * px + py
            return _rcopy(wf_ref.at[kp, c, rows], wf_ref.at[kp, c, rows], ssem_f.at[j * nch + q],
                          rsem_f.at[j * nch + q], (x, y, 1 - c))

        def from_peer(j, q):
            rows = pl.ds(q * cr, cr)
            px, py = peers[j]
            kp = 2 * px + py
            return _rcopy(w_ref.at[c, rows], wf_ref.at[kp, c, rows], ssem_i.at[j * nch + q],
                          rsem_i.at[j * nch + q], (px, py, c))

        def from_sibling(j, q):
            rows = pl.ds(q * cr, cr)
            px, py = peers[j]
            kp = 2 * px + py
            return _rcopy(wf_ref.at[kp, 1 - c, rows], wf_ref.at[kp, 1 - c, rows], ssem_f.at[j * nch + q],
                          rsem_f.at[j * nch + q], (x, y, 1 - c))

        def small(j):
            px, py = peers[j]
            return _rcopy(s_ref, sf_ref.at[k], ssem_s.at[j], rsem_s.at[j], (px, py, c))

        def small_in(j):
            px, py = peers[j]
            return _rcopy(s_ref, sf_ref.at[2 * px + py], ssem_s.at[j], rsem_s.at[j], (px, py, c))

        for q in range(nch):
            for j in range(3):
                ici(j, q).start()
        for j in range(3):
            small(j).start()
        for q in range(nch):
            for j in range(3):
                from_peer(j, q).wait_recv()
                landed(j, q).start()
        for q in range(nch):
            for j in range(3):
                from_sibling(j, q).wait_recv()
        for j in range(3):
            small_in(j).wait_recv()
        for q in range(nch):
            for j in range(3):
                ici(j, q).wait_send()
                landed(j, q).wait_send()
        for j in range(3):
            small(j).wait_send()
        own.wait()
        own_s.wait()

    return pl.pallas_call(
        body, name="gather_weights",
        out_shape=(SDS((N_CHIPS,) + wshard.shape, wshard.dtype), SDS((N_CHIPS,) + sshard.shape, sshard.dtype)),
        in_specs=[ANY, ANY], out_specs=(ANY, ANY),
        scratch_shapes=[pltpu.SemaphoreType.DMA((2,)),
                        pltpu.SemaphoreType.DMA((n_ici,)), pltpu.SemaphoreType.DMA((n_ici,)),
                        pltpu.SemaphoreType.DMA((n_ici,)), pltpu.SemaphoreType.DMA((n_ici,)),
                        pltpu.SemaphoreType.DMA((3,)), pltpu.SemaphoreType.DMA((3,))],
    )(wshard, sshard)


def exchange_pair(gb, sb):
    n, _, rh, d = gb.shape

    def body(gb_ref, sb_ref, p_ref, sp_ref, ssem, rsem):
        x, y, c = _mesh_pos()
        sib = (x, y, 1 - c)
        sends = [_rcopy(gb_ref.at[kk, 1 - c], p_ref.at[kk], ssem.at[kk], rsem.at[kk], sib) for kk in range(n)]
        sends.append(_rcopy(sb_ref, sp_ref, ssem.at[n], rsem.at[n], sib))
        for cp in sends:
            cp.start()
        for cp in sends:
            cp.wait_recv()
        for cp in sends:
            cp.wait_send()

    return pl.pallas_call(
        body, name="exchange_pair",
        out_shape=(SDS((n, rh, d), gb.dtype), SDS(sb.shape, sb.dtype)),
        in_specs=[ANY, ANY], out_specs=(ANY, ANY),
        scratch_shapes=[pltpu.SemaphoreType.DMA((n + 1,)), pltpu.SemaphoreType.DMA((n + 1,))],
    )(gb, sb)


def exchange_chips(cs, ss):
    n, rh, d = cs.shape

    def body(cs_ref, ss_ref, p_ref, sp_ref, lsem, ssem, rsem, ssem_s, rsem_s):
        x, y, c = _mesh_pos()
        k = 2 * x + y
        peers = [(1 - x, y), (x, 1 - y), (1 - x, 1 - y)]
        own = pltpu.make_async_copy(ss_ref, sp_ref.at[k], lsem)
        own.start()
        big, small, small_in = [], [], []
        for j, (px, py) in enumerate(peers):
            kp = 2 * px + py
            big.append(_rcopy(cs_ref.at[kp], p_ref.at[j], ssem.at[j], rsem.at[j], (px, py, c)))
            small.append(_rcopy(ss_ref, sp_ref.at[k], ssem_s.at[j], rsem_s.at[j], (px, py, c)))
            small_in.append(_rcopy(ss_ref, sp_ref.at[kp], ssem_s.at[j], rsem_s.at[j], (px, py, c)))
        for cp in big + small:
            cp.start()
        for cp in big + small_in:
            cp.wait_recv()
        for cp in big + small:
            cp.wait_send()
        own.wait()

    return pl.pallas_call(
        body, name="exchange_chips",
        out_shape=(SDS((3, rh, d), cs.dtype), SDS((N_CHIPS,) + ss.shape, ss.dtype)),
        in_specs=[ANY, ANY], out_specs=(ANY, ANY),
        scratch_shapes=[pltpu.SemaphoreType.DMA, pltpu.SemaphoreType.DMA((3,)), pltpu.SemaphoreType.DMA((3,)),
                        pltpu.SemaphoreType.DMA((3,)), pltpu.SemaphoreType.DMA((3,))],
    )(cs, ss)


def exchange_final(gf):
    _, rh, d = gf.shape
    nch = 4 if rh % 32 == 0 else 1
    cr = rh // nch

    def body(gf_ref, out_ref, ssem, rsem):
        x, y, c = _mesh_pos()
        sib = (x, y, 1 - c)
        sends = [_rcopy(out_ref.at[c, pl.ds(q * cr, cr)], out_ref.at[c, pl.ds(q * cr, cr)], ssem.at[q], rsem.at[q], sib)
                 for q in range(nch)]
        recvs = [_rcopy(out_ref.at[1 - c, pl.ds(q * cr, cr)], out_ref.at[1 - c, pl.ds(q * cr, cr)], ssem.at[q], rsem.at[q], sib)
                 for q in range(nch)]
        for cp in sends:
            cp.start()
        for cp in recvs:
            cp.wait_recv()
        for cp in sends:
            cp.wait_send()

    return pl.pallas_call(
        body, name="exchange_final",
        out_shape=SDS(gf.shape, gf.dtype),
        in_specs=[ANY], out_specs=ANY, input_output_aliases={0: 0},
        scratch_shapes=[pltpu.SemaphoreType.DMA((nch,)), pltpu.SemaphoreType.DMA((nch,))],
    )(gf)


def pair_sum(pos, gb, p1):
    n, _, rh, d = gb.shape
    cr = _row_chunk(rh, 1024)

    def body(pos_ref, a_ref, b_ref, o_ref):
        o_ref[...] = (a_ref[...].astype(F32) + b_ref[...].astype(F32)).astype(o_ref.dtype)

    return pl.pallas_call(
        body, name="pair_sum", out_shape=SDS((n, rh, d), gb.dtype),
        grid_spec=pltpu.PrefetchScalarGridSpec(
            num_scalar_prefetch=1, grid=(n, rh // cr),
            in_specs=[pl.BlockSpec((None, None, cr, d), lambda kk, r, pos: (kk, pos[1], r, 0)),
                      pl.BlockSpec((None, cr, d), lambda kk, r, pos: (kk, r, 0))],
            out_specs=pl.BlockSpec((None, cr, d), lambda kk, r, pos: (kk, r, 0))),
        compiler_params=_cparams(),
    )(pos, gb, p1)


def chip_sum(pos, cs, p3):
    n, rh, d = cs.shape
    cr = _row_chunk(rh, 512)

    def body(pos_ref, a_ref, b_ref, o_ref):
        acc = a_ref[...].astype(F32)
        for j in range(3):
            acc = acc + b_ref[j].astype(F32)
        o_ref[...] = acc

    return pl.pallas_call(
        body, name="chip_sum", out_shape=SDS((2, rh, d), F32),
        grid_spec=pltpu.PrefetchScalarGridSpec(
            num_scalar_prefetch=1, grid=(rh // cr,),
            in_specs=[pl.BlockSpec((None, cr, d), lambda r, pos: (pos[0], r, 0)),
                      pl.BlockSpec((3, cr, d), lambda r, pos: (0, r, 0))],
            out_specs=pl.BlockSpec((None, cr, d), lambda r, pos: (pos[1], r, 0))),
        compiler_params=_cparams(),
    )(pos, cs, p3)


def small_pair_sum(a, b):
    def body(a_ref, b_ref, o_ref):
        o_ref[...] = a_ref[...] + b_ref[...]

    return pl.pallas_call(body, name="small_pair_sum", out_shape=SDS(a.shape, a.dtype),
                          compiler_params=_cparams())(a, b)


def small_chip_sum(p):
    def body(p_ref, o_ref):
        o_ref[...] = ((p_ref[0] + p_ref[1]) + p_ref[2]) + p_ref[3]

    return pl.pallas_call(body, name="small_chip_sum", out_shape=SDS(p.shape[1:], p.dtype),
                          compiler_params=_cparams())(p)


def _rms(x, g):
    rs = lax.rsqrt(jnp.mean(x * x, axis=-1, keepdims=True) + EPS)
    xh = x * rs
    return xh, rs, xh * g


def _rms_bwd(dy, xh, rs, g):
    dxh = dy * g
    dx = rs * (dxh - xh * jnp.mean(dxh * xh, axis=-1, keepdims=True))
    return dx, dy * xh


def _gelu(x):
    t = jnp.tanh(GELU_C * (x + 0.044715 * x * x * x))
    return 0.5 * x * (1.0 + t), t


def _gelu_grad(x, t):
    return 0.5 * (1.0 + t) + 0.5 * x * (1.0 - t * t) * GELU_C * (1.0 + 3.0 * 0.044715 * x * x)


def _shift_rows(v, s, n):
    if s == 0:
        return v
    t = lax.broadcasted_iota(jnp.int32, v.shape, 0)
    rolled = pltpu.roll(v, (-s) % n, 0)
    return jnp.where((t + s >= 0) & (t + s < n), rolled, 0.0)


def _scan_rows(a_ref, u_ref, h_ref, acum_ref, reverse):
    s_len, w = a_ref.shape
    seg = s_len // SCAN_SEGMENTS

    def step(j, carry):
        h, acc = carry
        jj = (seg - 1 - j) if reverse else j
        idx = pl.ds(jj, SCAN_SEGMENTS, stride=seg)
        a = a_ref[idx, :]
        h = a * h + u_ref[idx, :]
        acc = a * acc
        h_ref[idx, :] = h
        acum_ref[idx, :] = acc
        return h, acc

    h, acc = lax.fori_loop(0, seg, step, (jnp.zeros((SCAN_SEGMENTS, w), F32), jnp.ones((SCAN_SEGMENTS, w), F32)),
                           unroll=8)
    order = range(SCAN_SEGMENTS - 2, -1, -1) if reverse else range(1, SCAN_SEGMENTS)
    inflow = jnp.zeros((1, w), F32)
    for s in order:
        src = s + 1 if reverse else s - 1
        inflow = h[src:src + 1, :] + acc[src:src + 1, :] * inflow
        rows = pl.ds(s * seg, seg)
        h_ref[rows, :] = h_ref[rows, :] + acum_ref[rows, :] * inflow


def _w_spec(rows_half, d, blk):
    return pl.BlockSpec((N_CHIPS, 2, rows_half, d), lambda *_: (0, 0, blk, 0), pipeline_mode=pl.Buffered(1))


def ffn_forward(x, gain, wfull, lay, l, which, tm=512):
    s_len, d = x.shape
    tm = min(tm, s_len)
    f = 8 * lay.fh
    fc = f // 2

    def body(x_ref, g_ref, wg_ref, wu_ref, wd_ref, o_ref):
        x = x_ref[...]
        _, _, hn = _rms(x, g_ref[...])
        h = hn.astype(BF)
        y = jnp.zeros((tm, d), F32)
        for part in range(2):
            cols = slice(part * fc, (part + 1) * fc)
            gate = dot_nt(h, wg_ref[...].reshape(f, d)[cols])
            up = dot_nt(h, wu_ref[...].reshape(f, d)[cols])
            act = (gate * jax.nn.sigmoid(gate) * up).astype(BF)
            y = y + dot_nn(act, wd_ref[...].reshape(f, d)[cols])
        o_ref[...] = x + FFN_RES * y

    row = pl.BlockSpec((tm, d), lambda i: (i, 0))
    return pl.pallas_call(
        body, name="ffn_forward", grid=(s_len // tm,), out_shape=SDS((s_len, d), F32),
        in_specs=[row, pl.BlockSpec((1, d), lambda i: (0, 0))]
        + [_w_spec(lay.fh, d, lay.ffn_blk(l, 3 * which + m)) for m in range(3)],
        out_specs=row, compiler_params=_cparams(),
    )(x, gain, wfull, wfull, wfull)


def ffn_backward_dx(x, gain, dout, wfull, lay, l, which, tm=256):
    s_len, d = x.shape
    tm = min(tm, s_len)
    f = 8 * lay.fh
    fc = f // 2
    nt = s_len // tm

    def body(x_ref, g_ref, do_ref, wg_ref, wu_ref, wd_ref, dx_ref, dg_ref, dgate_ref, dup_ref, act_ref, h_ref, df_ref):
        x = x_ref[...]
        g = g_ref[...]
        xh, rs, hn = _rms(x, g)
        h = hn.astype(BF)
        do = do_ref[...]
        df = (FFN_RES * do).astype(BF)
        dh = jnp.zeros((tm, d), F32)
        for part in range(2):
            cols = slice(part * fc, (part + 1) * fc)
            wg = wg_ref[...].reshape(f, d)[cols]
            wu = wu_ref[...].reshape(f, d)[cols]
            gate = dot_nt(h, wg)
            up = dot_nt(h, wu)
            sg = jax.nn.sigmoid(gate)
            silu = gate * sg
            dact = dot_nt(df, wd_ref[...].reshape(f, d)[cols])
            dup = (dact * silu).astype(BF)
            dgate = (dact * up * (sg * (1.0 + gate * (1.0 - sg)))).astype(BF)
            dh = dh + dot_nn(dgate, wg) + dot_nn(dup, wu)
            dgate_ref[:, cols] = dgate
            dup_ref[:, cols] = dup
            act_ref[:, cols] = (silu * up).astype(BF)
        dxn, dgrow = _rms_bwd(dh, xh, rs, g)
        dx_ref[...] = do + dxn

        @pl.when(pl.program_id(0) == 0)
        def _():
            dg_ref[...] = jnp.zeros_like(dg_ref)

        dg_ref[...] += jnp.sum(dgrow, axis=0, keepdims=True)
        h_ref[...] = h
        df_ref[...] = df

    row = pl.BlockSpec((tm, d), lambda i: (i, 0))
    wide = pl.BlockSpec((tm, f), lambda i: (i, 0))
    vec = pl.BlockSpec((1, d), lambda i: (0, 0))
    return pl.pallas_call(
        body, name="ffn_backward_dx", grid=(nt,),
        out_shape=(SDS((s_len, d), F32), SDS((1, d), F32), SDS((s_len, f), BF), SDS((s_len, f), BF),
                   SDS((s_len, f), BF), SDS((s_len, d), BF), SDS((s_len, d), BF)),
        in_specs=[row, vec, row] + [_w_spec(lay.fh, d, lay.ffn_blk(l, 3 * which + m)) for m in range(3)],
        out_specs=(row, vec, wide, wide, wide, row, row), compiler_params=_cparams(),
    )(x, gain, dout, wfull, wfull, wfull)


def weight_grad_tn(a, b, gb, lay, blk, tk=512):
    s_len, f = a.shape
    tk = min(tk, s_len)
    d = b.shape[1]
    fc = f // 2
    nk = s_len // tk

    def body(a_ref, b_ref, gb_ref, o_ref, acc):
        kt = pl.program_id(1)

        @pl.when(kt == 0)
        def _():
            acc[...] = jnp.zeros_like(acc)

        acc[...] += dot_tn(a_ref[...], b_ref[...])

        @pl.when(kt == nk - 1)
        def _():
            for p in range(2):
                for q in range(2):
                    o_ref[p, q] = acc[pl.ds((2 * p + q) * lay.fh, lay.fh), :].astype(o_ref.dtype)

    return pl.pallas_call(
        body, name="weight_grad_tn", grid=(2, nk), out_shape=SDS(gb.shape, gb.dtype),
        in_specs=[pl.BlockSpec((tk, fc), lambda j, kt: (kt, j)), pl.BlockSpec((tk, d), lambda j, kt: (kt, 0)), ANY],
        out_specs=pl.BlockSpec((2, 2, lay.fh, d), lambda j, kt: (j, 0, blk, 0)),
        scratch_shapes=[pltpu.VMEM((fc, d), F32)],
        input_output_aliases={2: 0}, compiler_params=_cparams(),
    )(a, b, gb)


def mix_project(x, gain, wfull, lay, l, tm=512):
    s_len, d = x.shape
    tm = min(tm, s_len)
    d_in = 8 * lay.ih

    def body(x_ref, g_ref, w_ref, o_ref):
        _, _, hn = _rms(x_ref[...], g_ref[...])
        o_ref[...] = dot_nt(hn.astype(BF), w_ref[...].reshape(d_in, d))

    return pl.pallas_call(
        body, name="mix_project", grid=(s_len // tm,), out_shape=SDS((s_len, d_in), F32),
        in_specs=[pl.BlockSpec((tm, d), lambda i: (i, 0)), pl.BlockSpec((1, d), lambda i: (0, 0)),
                  _w_spec(lay.ih, d, lay.win_blk(l))],
        out_specs=pl.BlockSpec((tm, d_in), lambda i: (i, 0)), compiler_params=_cparams(),
    )(x, gain, wfull)


def mix_project_backward(x, gain, dout, dxr, dgt, dq, dkv, wfull, gb, lay, l, tm=512):
    s_len, d = x.shape
    tm = min(tm, s_len)
    d_in = 8 * lay.ih
    nt = s_len // tm
    lw = dxr.shape[1]
    kvw = dkv.shape[1]

    def body(x_ref, g_ref, do_ref, dxr_ref, dgt_ref, dq_ref, dkv_ref, w_ref, gb_ref, dx_ref, dg_ref, o_ref, acc):
        i = pl.program_id(0)
        g = g_ref[...]
        xh, rs, hn = _rms(x_ref[...], g)
        h = hn.astype(BF)
        dp = jnp.concatenate([dxr_ref[...], dgt_ref[...], dq_ref[...], dkv_ref[...]], axis=1).astype(BF)
        dh = dot_nn(dp, w_ref[...].reshape(d_in, d))
        dxn, dgrow = _rms_bwd(dh, xh, rs, g)
        dx_ref[...] = do_ref[...] + dxn

        @pl.when(i == 0)
        def _():
            dg_ref[...] = jnp.zeros_like(dg_ref)
            acc[...] = jnp.zeros_like(acc)

        dg_ref[...] += jnp.sum(dgrow, axis=0, keepdims=True)
        acc[...] += dot_tn(dp, h)

        @pl.when(i == nt - 1)
        def _():
            for p in range(N_CHIPS):
                for q in range(2):
                    o_ref[p, q] = acc[pl.ds((2 * p + q) * lay.ih, lay.ih), :].astype(o_ref.dtype)

    row = pl.BlockSpec((tm, d), lambda i: (i, 0))
    vec = pl.BlockSpec((1, d), lambda i: (0, 0))
    return pl.pallas_call(
        body, name="mix_project_backward", grid=(nt,),
        out_shape=(SDS((s_len, d), F32), SDS((1, d), F32), SDS(gb.shape, gb.dtype)),
        in_specs=[row, vec, row, pl.BlockSpec((tm, lw), lambda i: (i, 0)), pl.BlockSpec((tm, lw), lambda i: (i, 0)),
                  pl.BlockSpec((tm, dq.shape[1]), lambda i: (i, 0)), pl.BlockSpec((tm, kvw), lambda i: (i, 0)),
                  _w_spec(lay.ih, d, lay.win_blk(l)), ANY],
        out_specs=(row, vec, pl.BlockSpec((N_CHIPS, 2, lay.ih, d), lambda i: (0, 0, lay.win_blk(l), 0))),
        scratch_shapes=[pltpu.VMEM((d_in, d), F32)],
        input_output_aliases={8: 2}, compiler_params=_cparams(),
    )(x, gain, dout, dxr, dgt, dq, dkv, wfull, gb)


def _lru_gates(xc, wb_ref, pv_ref, direction):
    xcb = xc.astype(BF)
    r = jax.nn.sigmoid(dot_nn(xcb, wb_ref[2 * direction]) + pv_ref[1 + direction:2 + direction, :])
    i = jax.nn.sigmoid(dot_nn(xcb, wb_ref[2 * direction + 1]) + pv_ref[3 + direction:4 + direction, :])
    lam = pv_ref[5 + direction:6 + direction, :]
    sp = jnp.maximum(-lam, 0.0) + jnp.log(1.0 + jnp.exp(-jnp.abs(lam)))
    a = jnp.exp(-LRU_C * sp * r)
    mult = jnp.sqrt(1.0 - a * a)
    return xcb, r, i, a, mult, sp


def _conv_rows(xr, cv_ref, bias, n):
    acc = bias + cv_ref[0:1, :] * _shift_rows(xr, -2, n)
    for j in range(1, CONV_WIDTH):
        acc = acc + cv_ref[j:j + 1, :] * _shift_rows(xr, j - 2, n)
    return acc


def lru_forward(proj, cvec, pvec, wblk, lw, ch=512):
    s_len = proj.shape[0]
    ncb = lw // LANE
    ch = min(ch, s_len)
    nchunk = s_len // ch

    def body(xr_ref, gt_ref, cv_ref, pv_ref, wb_ref, y_ref, hs_ref, xc_s, a_s, u_s, acum_s):
        xc_s[...] = _conv_rows(xr_ref[...], cv_ref, pv_ref[0:1, :], s_len)
        for direction in range(2):
            def fill(ci, _):
                rows = pl.ds(pl.multiple_of(ci * ch, ch), ch)
                xc = xc_s[rows, :]
                _, _, i, a, mult, _ = _lru_gates(xc, wb_ref, pv_ref, direction)
                a_s[rows, :] = a
                u_s[rows, :] = mult * (i * xc)
                return 0

            lax.fori_loop(0, nchunk, fill, 0)
            _scan_rows(a_s, u_s, hs_ref.at[direction], acum_s, reverse=direction == 1)

        def out(ci, _):
            rows = pl.ds(pl.multiple_of(ci * ch, ch), ch)
            gl, _ = _gelu(gt_ref[rows, :])
            y_ref[rows, :] = gl * (hs_ref[0, rows, :] + hs_ref[1, rows, :])
            return 0

        lax.fori_loop(0, nchunk, out, 0)

    col = lambda off: pl.BlockSpec((s_len, LANE), lambda cb: (0, off + cb))
    return pl.pallas_call(
        body, name="lru_forward", grid=(ncb,),
        out_shape=(SDS((s_len, lw), F32), SDS((2, s_len, lw), F32)),
        in_specs=[col(0), col(ncb), pl.BlockSpec((8, LANE), lambda cb: (0, cb)), pl.BlockSpec((8, LANE), lambda cb: (0, cb)),
                  pl.BlockSpec((4, None, LANE, LANE), lambda cb: (0, cb, 0, 0))],
        out_specs=(pl.BlockSpec((s_len, LANE), lambda cb: (0, cb)), pl.BlockSpec((2, s_len, LANE), lambda cb: (0, 0, cb))),
        scratch_shapes=[pltpu.VMEM((s_len, LANE), F32)] * 4, compiler_params=_cparams(),
    )(proj, proj, cvec, pvec, wblk)


def lru_backward(proj, hs, dy, cvec, pvec, wblk, lw, ch=512):
    s_len = proj.shape[0]
    ncb = lw // LANE
    ch = min(ch, s_len)
    nchunk = s_len // ch

    def body(xr_ref, gt_ref, hs_ref, dy_ref, cv_ref, pv_ref, wb_ref, dxr_ref, dgt_ref, dcv_ref, dpv_ref, dwb_ref,
             xc_s, a_s, dh_s, lam_s, hp_s, dxc_s, acum_s):
        xr = xr_ref[...]
        xc_s[...] = _conv_rows(xr, cv_ref, pv_ref[0:1, :], s_len)
        dxc_s[...] = jnp.zeros_like(dxc_s)
        dpv_ref[...] = jnp.zeros_like(dpv_ref)
        dwb_ref[...] = jnp.zeros_like(dwb_ref)

        def head(ci, _):
            rows = pl.ds(pl.multiple_of(ci * ch, ch), ch)
            gt = gt_ref[rows, :]
            gl, t = _gelu(gt)
            dy = dy_ref[rows, :]
            dh_s[rows, :] = dy * gl
            dgt_ref[rows, :] = dy * (hs_ref[0, rows, :] + hs_ref[1, rows, :]) * _gelu_grad(gt, t)
            return 0

        lax.fori_loop(0, nchunk, head, 0)

        for direction in range(2):
            def fill(ci, _):
                rows = pl.ds(pl.multiple_of(ci * ch, ch), ch)
                _, _, _, a, _, _ = _lru_gates(xc_s[rows, :], wb_ref, pv_ref, direction)
                a_s[rows, :] = a
                return 0

            lax.fori_loop(0, nchunk, fill, 0)
            toward = 1 if direction == 0 else -1
            hp_s[...] = _shift_rows(a_s[...], toward, s_len)
            _scan_rows(hp_s, dh_s, lam_s, acum_s, reverse=direction == 0)
            hp_s[...] = _shift_rows(hs_ref[direction], -toward, s_len)

            def grads(ci, _):
                rows = pl.ds(pl.multiple_of(ci * ch, ch), ch)
                xc = xc_s[rows, :]
                xcb, r, i, a, mult, sp = _lru_gates(xc, wb_ref, pv_ref, direction)
                du = lam_s[rows, :]
                da = du * hp_s[rows, :]
                dmult = du * i * xc
                di = du * mult * xc
                dlog_a = (da - dmult * a / mult) * a
                dr = dlog_a * (-LRU_C * sp)
                dza = dr * r * (1.0 - r)
                dzx = di * i * (1.0 - i)
                dzab = dza.astype(BF)
                dzxb = dzx.astype(BF)
                dxc_s[rows, :] += (du * mult * i + dot_nt(dzab, wb_ref[2 * direction])
                                   + dot_nt(dzxb, wb_ref[2 * direction + 1]))
                dwb_ref[2 * direction] += dot_tn(xcb, dzab)
                dwb_ref[2 * direction + 1] += dot_tn(xcb, dzxb)
                dpv_ref[1 + direction:2 + direction, :] += jnp.sum(dza, axis=0, keepdims=True)
                dpv_ref[3 + direction:4 + direction, :] += jnp.sum(dzx, axis=0, keepdims=True)
                dpv_ref[5 + direction:6 + direction, :] += jnp.sum(dlog_a * (-LRU_C * r), axis=0, keepdims=True)
                return 0

            lax.fori_loop(0, nchunk, grads, 0)

        for direction in range(2):
            lam = pv_ref[5 + direction:6 + direction, :]
            dpv_ref[5 + direction:6 + direction, :] = dpv_ref[5 + direction:6 + direction, :] * (-jax.nn.sigmoid(-lam))
        dxc = dxc_s[...]
        dpv_ref[0:1, :] = jnp.sum(dxc, axis=0, keepdims=True)
        dxr = cv_ref[0:1, :] * _shift_rows(dxc, 2, s_len)
        for j in range(1, CONV_WIDTH):
            dxr = dxr + cv_ref[j:j + 1, :] * _shift_rows(dxc, 2 - j, s_len)
        dxr_ref[...] = dxr
        dcv_ref[...] = jnp.zeros_like(dcv_ref)
        for j in range(CONV_WIDTH):
            dcv_ref[j:j + 1, :] = jnp.sum(dxc * _shift_rows(xr, j - 2, s_len), axis=0, keepdims=True)

    col = lambda off: pl.BlockSpec((s_len, LANE), lambda cb: (0, off + cb))
    own = pl.BlockSpec((s_len, LANE), lambda cb: (0, cb))
    small = pl.BlockSpec((8, LANE), lambda cb: (0, cb))
    wspec = pl.BlockSpec((4, None, LANE, LANE), lambda cb: (0, cb, 0, 0))
    return pl.pallas_call(
        body, name="lru_backward", grid=(ncb,),
        out_shape=(SDS((s_len, lw), F32), SDS((s_len, lw), F32), SDS((8, lw), F32), SDS((8, lw), F32),
                   SDS(wblk.shape, F32)),
        in_specs=[col(0), col(ncb), pl.BlockSpec((2, s_len, LANE), lambda cb: (0, 0, cb)), own, small, small, wspec],
        out_specs=(own, own, small, small, wspec),
        scratch_shapes=[pltpu.VMEM((s_len, LANE), F32)] * 7, compiler_params=_cparams(),
    )(proj, proj, hs, dy, cvec, pvec, wblk)


def _attn_specs(s_len, lw, att):
    nb = s_len // BLOCK
    qcol = 2 * lw // att
    kcol = (2 * lw + att) // BLOCK
    prev = lambda n: jnp.maximum(n - 1, 0)
    nxt = lambda n: jnp.minimum(n + 1, nb - 1)
    q = pl.BlockSpec((BLOCK, att), lambda n: (n, qcol))
    ks = [pl.BlockSpec((BLOCK, BLOCK), lambda n, f=f: (f(n), kcol)) for f in (prev, lambda n: n, nxt)]
    vs = [pl.BlockSpec((BLOCK, BLOCK), lambda n, f=f: (f(n), kcol + 1)) for f in (prev, lambda n: n, nxt)]
    return q, ks, vs


def _attn_probs(n, s_len, qh, kcat, bias_h, sink_h):
    logits = dot_nt(qh, kcat) + bias_h
    t = lax.broadcasted_iota(jnp.int32, logits.shape, 0)
    j = lax.broadcasted_iota(jnp.int32, logits.shape, 1)
    key_pos = (n - 1) * BLOCK + j
    ok = (jnp.abs(j - BLOCK - t) <= WINDOW) & (key_pos >= 0) & (key_pos < s_len)
    logits = jnp.where(ok, logits, NEG_INF)
    m = jnp.maximum(jnp.max(logits, axis=-1, keepdims=True), sink_h)
    p = jnp.exp(logits - m)
    es = jnp.exp(sink_h - m)
    den = jnp.sum(p, axis=-1, keepdims=True) + es
    return p / den, es / den


def attention_forward(proj, bias, sink, lw, att):
    s_len = proj.shape[0]
    nb = s_len // BLOCK
    q_spec, k_specs, v_specs = _attn_specs(s_len, lw, att)

    def body(q_ref, kp_ref, kc_ref, kn_ref, vp_ref, vc_ref, vn_ref, b_ref, s_ref, o_ref):
        n = pl.program_id(0)
        q = q_ref[...]
        kall = jnp.concatenate([kp_ref[...], kc_ref[...], kn_ref[...]], axis=0).astype(BF)
        vall = jnp.concatenate([vp_ref[...], vc_ref[...], vn_ref[...]], axis=0).astype(BF)
        outs = []
        for h in range(N_HEADS):
            kh = h // KV_GROUP
            qh = (q[:, h * HEAD_DIM:(h + 1) * HEAD_DIM] * (HEAD_DIM ** -0.5)).astype(BF)
            p, _ = _attn_probs(n, s_len, qh, kall[:, kh * HEAD_DIM:(kh + 1) * HEAD_DIM], b_ref[h], s_ref[h:h + 1, 0:1])
            outs.append(dot_nn(p.astype(BF), vall[:, kh * HEAD_DIM:(kh + 1) * HEAD_DIM]))
        o_ref[...] = jnp.concatenate(outs, axis=1)

    return pl.pallas_call(
        body, name="attention_forward", grid=(nb,), out_shape=SDS((s_len, att), F32),
        in_specs=[q_spec] + k_specs + v_specs
        + [pl.BlockSpec(bias.shape, lambda n: (0, 0, 0)), pl.BlockSpec(sink.shape, lambda n: (0, 0))],
        out_specs=pl.BlockSpec((BLOCK, att), lambda n: (n, 0)), compiler_params=_cparams(),
    )(proj, proj, proj, proj, proj, proj, proj, bias, sink)


def attention_backward(proj, y_att, dy, bias, sink, lw, att):
    s_len = proj.shape[0]
    nb = s_len // BLOCK
    kvw = N_KV_HEADS * HEAD_DIM
    q_spec, k_specs, v_specs = _attn_specs(s_len, lw, att)

    def body(q_ref, kp_ref, kc_ref, kn_ref, vp_ref, vc_ref, vn_ref, o_ref, do_ref, b_ref, s_ref,
             dq_ref, dkv_ref, db_ref, ds_ref):
        n = pl.program_id(0)

        @pl.when(n == 0)
        def _():
            dkv_ref[...] = jnp.zeros_like(dkv_ref)
            db_ref[...] = jnp.zeros_like(db_ref)
            ds_ref[...] = jnp.zeros_like(ds_ref)

        q = q_ref[...]
        o = o_ref[...]
        do = do_ref[...]
        kall = jnp.concatenate([kp_ref[...], kc_ref[...], kn_ref[...]], axis=0).astype(BF)
        vall = jnp.concatenate([vp_ref[...], vc_ref[...], vn_ref[...]], axis=0).astype(BF)
        dqs, dks, dvs = [], [], []
        for kh in range(N_KV_HEADS):
            kcat = kall[:, kh * HEAD_DIM:(kh + 1) * HEAD_DIM]
            vcat = vall[:, kh * HEAD_DIM:(kh + 1) * HEAD_DIM]
            dk = jnp.zeros((3 * BLOCK, HEAD_DIM), F32)
            dv = jnp.zeros((3 * BLOCK, HEAD_DIM), F32)
            for g in range(KV_GROUP):
                h = kh * KV_GROUP + g
                cols = slice(h * HEAD_DIM, (h + 1) * HEAD_DIM)
                qh = (q[:, cols] * (HEAD_DIM ** -0.5)).astype(BF)
                p, ps = _attn_probs(n, s_len, qh, kcat, b_ref[h], s_ref[h:h + 1, 0:1])
                doh = do[:, cols]
                dohb = doh.astype(BF)
                delta = jnp.sum(doh * o[:, cols], axis=-1, keepdims=True)
                dlog = p * (dot_nt(dohb, vcat) - delta)
                dlogb = dlog.astype(BF)
                db_ref[h] += dlog
                ds_ref[h:h + 1, :] += jnp.broadcast_to(jnp.sum(-ps * delta, axis=0, keepdims=True), (1, LANE))
                dqs.append(dot_nn(dlogb, kcat) * (HEAD_DIM ** -0.5))
                dk = dk + dot_tn(dlogb, qh)
                dv = dv + dot_tn(p.astype(BF), dohb)
            dks.append(dk)
            dvs.append(dv)
        dq_ref[...] = jnp.concatenate(dqs, axis=1)
        dkv = jnp.concatenate(dks + dvs, axis=1)
        starts = [jnp.maximum(n - 1, 0), n, jnp.minimum(n + 1, nb - 1)]
        for b, st in enumerate(starts):
            rows = pl.ds(pl.multiple_of(st * BLOCK, BLOCK), BLOCK)
            dkv_ref[rows, :] += dkv[b * BLOCK:(b + 1) * BLOCK, :]

    blk = pl.BlockSpec((BLOCK, att), lambda n: (n, 0))
    return pl.pallas_call(
        body, name="attention_backward", grid=(nb,),
        out_shape=(SDS((s_len, att), F32), SDS((s_len, 2 * kvw), F32), SDS(bias.shape, F32), SDS(sink.shape, F32)),
        in_specs=[q_spec] + k_specs + v_specs
        + [blk, blk, pl.BlockSpec(bias.shape, lambda n: (0, 0, 0)), pl.BlockSpec(sink.shape, lambda n: (0, 0))],
        out_specs=(blk, pl.BlockSpec((s_len, 2 * kvw), lambda n: (0, 0)),
                   pl.BlockSpec(bias.shape, lambda n: (0, 0, 0)), pl.BlockSpec(sink.shape, lambda n: (0, 0))),
        compiler_params=_cparams(),
    )(proj, proj, proj, proj, proj, proj, proj, y_att, dy, bias, sink)


def mix_output(x, y_rec, y_att, g_rec, g_att, wfull, lay, l, tm=512):
    s_len, d = x.shape
    tm = min(tm, s_len)
    lw = y_rec.shape[1]
    att = y_att.shape[1]

    def body(x_ref, yr_ref, ya_ref, gr_ref, ga_ref, w_ref, o_ref):
        _, _, nr = _rms(yr_ref[...], gr_ref[...])
        _, _, na = _rms(ya_ref[...], ga_ref[...])
        y = jnp.concatenate([nr, na], axis=1).astype(BF)
        o_ref[...] = x_ref[...] + dot_nn(y, w_ref[...].reshape(d, d))

    row = pl.BlockSpec((tm, d), lambda i: (i, 0))
    return pl.pallas_call(
        body, name="mix_output", grid=(s_len // tm,), out_shape=SDS((s_len, d), F32),
        in_specs=[row, pl.BlockSpec((tm, lw), lambda i: (i, 0)), pl.BlockSpec((tm, att), lambda i: (i, 0)),
                  pl.BlockSpec((1, lw), lambda i: (0, 0)), pl.BlockSpec((1, att), lambda i: (0, 0)),
                  _w_spec(lay.oh, d, lay.wout_blk(l))],
        out_specs=row, compiler_params=_cparams(),
    )(x, y_rec, y_att, g_rec, g_att, wfull)


def mix_output_backward(dout, y_rec, y_att, g_rec, g_att, wfull, gb, lay, l, tm=512):
    s_len, d = dout.shape
    tm = min(tm, s_len)
    lw = y_rec.shape[1]
    att = y_att.shape[1]
    nt = s_len // tm

    def body(do_ref, yr_ref, ya_ref, gr_ref, ga_ref, w_ref, gb_ref, dyr_ref, dya_ref, dgr_ref, dga_ref, o_ref, acc):
        i = pl.program_id(0)
        gr = gr_ref[...]
        ga = ga_ref[...]
        xhr, rsr, nr = _rms(yr_ref[...], gr)
        xha, rsa, na = _rms(ya_ref[...], ga)
        y = jnp.concatenate([nr, na], axis=1).astype(BF)
        dob = do_ref[...].astype(BF)
        dy = dot_nt(dob, w_ref[...].reshape(d, d))
        dyr, dgr_row = _rms_bwd(dy[:, :lw], xhr, rsr, gr)
        dya, dga_row = _rms_bwd(dy[:, lw:], xha, rsa, ga)
        dyr_ref[...] = dyr
        dya_ref[...] = dya

        @pl.when(i == 0)
        def _():
            dgr_ref[...] = jnp.zeros_like(dgr_ref)
            dga_ref[...] = jnp.zeros_like(dga_ref)
            acc[...] = jnp.zeros_like(acc)

        dgr_ref[...] += jnp.sum(dgr_row, axis=0, keepdims=True)
        dga_ref[...] += jnp.sum(dga_row, axis=0, keepdims=True)
        acc[...] += dot_tn(y, dob)

        @pl.when(i == nt - 1)
        def _():
            for p in range(N_CHIPS):
                for q in range(2):
                    o_ref[p, q] = acc[pl.ds((2 * p + q) * lay.oh, lay.oh), :].astype(o_ref.dtype)

    row = pl.BlockSpec((tm, d), lambda i: (i, 0))
    return pl.pallas_call(
        body, name="mix_output_backward", grid=(nt,),
        out_shape=(SDS((s_len, lw), F32), SDS((s_len, att), F32), SDS((1, lw), F32), SDS((1, att), F32),
                   SDS(gb.shape, gb.dtype)),
        in_specs=[row, pl.BlockSpec((tm, lw), lambda i: (i, 0)), pl.BlockSpec((tm, att), lambda i: (i, 0)),
                  pl.BlockSpec((1, lw), lambda i: (0, 0)), pl.BlockSpec((1, att), lambda i: (0, 0)),
                  _w_spec(lay.oh, d, lay.wout_blk(l)), ANY],
        out_specs=(pl.BlockSpec((tm, lw), lambda i: (i, 0)), pl.BlockSpec((tm, att), lambda i: (i, 0)),
                   pl.BlockSpec((1, lw), lambda i: (0, 0)), pl.BlockSpec((1, att), lambda i: (0, 0)),
                   pl.BlockSpec((N_CHIPS, 2, lay.oh, d), lambda i: (0, 0, lay.wout_blk(l), 0))),
        scratch_shapes=[pltpu.VMEM((d, d), F32)],
        input_output_aliases={6: 4}, compiler_params=_cparams(),
    )(dout, y_rec, y_att, g_rec, g_att, wfull, gb)


def loss_head(x, gain, target, tm=512):
    s_len, d = x.shape
    tm = min(tm, s_len)

    def body(x_ref, g_ref, t_ref, dx_ref, dg_ref, loss_ref):
        g = g_ref[...]
        xh, rs, y = _rms(x_ref[...], g)
        err = y - t_ref[...]

        @pl.when(pl.program_id(0) == 0)
        def _():
            dg_ref[...] = jnp.zeros_like(dg_ref)
            loss_ref[...] = jnp.zeros_like(loss_ref)

        part = 0.5 * jnp.sum(jnp.mean(err * err, axis=-1, keepdims=True), axis=0, keepdims=True)
        loss_ref[...] += jnp.broadcast_to(part, loss_ref.shape)
        dx, dgrow = _rms_bwd(err * (1.0 / d), xh, rs, g)
        dx_ref[...] = dx
        dg_ref[...] += jnp.sum(dgrow, axis=0, keepdims=True)

    row = pl.BlockSpec((tm, d), lambda i: (i, 0))
    vec = pl.BlockSpec((1, d), lambda i: (0, 0))
    return pl.pallas_call(
        body, name="loss_head", grid=(s_len // tm,),
        out_shape=(SDS((s_len, d), F32), SDS((1, d), F32), SDS((8, LANE), F32)),
        in_specs=[row, vec, row], out_specs=(row, vec, pl.BlockSpec((8, LANE), lambda i: (0, 0))),
        compiler_params=_cparams(),
    )(x, gain, target)


def adamw(w, g, m, v, tr=512):
    rows, cols = w.shape
    tr = _row_chunk(rows, tr, 8)

    def body(w_ref, g_ref, m_ref, v_ref, d_ref, nm_ref, nv_ref):
        g = g_ref[...]
        m = ADAM_B1 * m_ref[...] + (1.0 - ADAM_B1) * g
        v = ADAM_B2 * v_ref[...] + (1.0 - ADAM_B2) * (g * g)
        m_hat = m / (1.0 - ADAM_B1 ** ADAM_STEP)
        v_hat = v / (1.0 - ADAM_B2 ** ADAM_STEP)
        d_ref[...] = -ADAM_LR * (m_hat / (jnp.sqrt(v_hat) + ADAM_EPS) + ADAM_WD * w_ref[...])
        nm_ref[...] = m
        nv_ref[...] = v

    blk = pl.BlockSpec((tr, cols), lambda i: (i, 0))
    return pl.pallas_call(
        body, name="adamw", grid=(rows // tr,), out_shape=(SDS(w.shape, F32),) * 3,
        in_specs=[blk] * 4, out_specs=(blk,) * 3, compiler_params=_cparams(),
    )(w, g, m, v)


def _pack_rows(arrays, width):
    flat = jnp.concatenate([a.reshape(-1).astype(F32) for a in arrays])
    rows = -(-flat.shape[0] // (8 * width)) * 8
    return jnp.pad(flat, (0, rows * width - flat.shape[0])).reshape(rows, width)


def _unpack_rows(buf, shapes):
    flat = buf.reshape(-1)
    out, off = [], 0
    for shp in shapes:
        n = int(np.prod(shp))
        out.append(flat[off:off + n].reshape(shp))
        off += n
    return out


def _t5_buckets(rel):
    half = N_BUCKETS // 2
    max_exact = half // 2
    ret = (rel > 0).astype(jnp.int32) * half
    n = jnp.abs(rel)
    n_f = jnp.maximum(n, 1).astype(F32)
    large = max_exact + (jnp.log(n_f / max_exact) / math.log(MAX_DISTANCE / max_exact) * (half - max_exact)).astype(jnp.int32)
    large = jnp.minimum(large, half - 1)
    return ret + jnp.where(n < max_exact, n, large)


def _band_buckets():
    t = jnp.arange(BLOCK)[:, None]
    j = jnp.arange(3 * BLOCK)[None, :]
    return _t5_buckets(j - BLOCK - t)


def _block_diag_pairs(w):
    depth, two, nblk, bw, _ = w.shape
    pairs = w.reshape(depth, two, nblk // 2, 2, bw, bw)
    z = jnp.zeros_like(pairs[:, :, :, 0])
    top = jnp.concatenate([pairs[:, :, :, 0], z], axis=-1)
    bot = jnp.concatenate([z, pairs[:, :, :, 1]], axis=-1)
    return jnp.concatenate([top, bot], axis=-2)


def _diag_blocks(dw):
    bw = dw.shape[-1] // 2
    a = dw[:, :, :bw, :bw]
    b = dw[:, :, bw:, bw:]
    return jnp.stack([a, b], axis=2).reshape(dw.shape[0], 2 * dw.shape[1], bw, bw)


def _pack_weight_shard(lay, d, w_in, w_out, mats):
    halves = []
    for hf in range(2):
        parts = []

        def pad_to(n):
            have = sum(p.shape[0] for p in parts)
            if n > have:
                parts.append(jnp.zeros((n - have, d), F32))

        for l in range(lay.depth):
            parts.append(w_in[l].T[hf * lay.ih:(hf + 1) * lay.ih])
        pad_to(lay.wout0)
        for l in range(lay.depth):
            parts.append(w_out[l][hf * lay.oh:(hf + 1) * lay.oh])
        pad_to(lay.ffn0)
        for l in range(lay.depth):
            for m in range(6):
                a = mats[m][l]
                a = a if m % 3 == 2 else a.T
                parts.append(a[hf * lay.fh:(hf + 1) * lay.fh])
        halves.append(jnp.concatenate(parts, axis=0))
    return jnp.stack(halves).astype(BF)


def kernel(x, ffn1_norm, ffn1_w_gate, ffn1_w_up, ffn1_w_down, mix_norm, w_in, conv_w, conv_b, lru_w_a, lru_b_a, lru_w_x, lru_b_x, lru_lambda, attn_sink, rel_bias, lru_out_norm, attn_out_norm, w_out, ffn2_norm, ffn2_w_gate, ffn2_w_up, ffn2_w_down, final_norm, loss_target, m_ffn1_norm, m_ffn1_w_gate, m_ffn1_w_up, m_ffn1_w_down, m_mix_norm, m_w_in, m_conv_w, m_conv_b, m_lru_w_a, m_lru_b_a, m_lru_w_x, m_lru_b_x, m_lru_lambda, m_attn_sink, m_rel_bias, m_lru_out_norm, m_attn_out_norm, m_w_out, m_ffn2_norm, m_ffn2_w_gate, m_ffn2_w_up, m_ffn2_w_down, m_final_norm, v_ffn1_norm, v_ffn1_w_gate, v_ffn1_w_up, v_ffn1_w_down, v_mix_norm, v_w_in, v_conv_w, v_conv_b, v_lru_w_a, v_lru_b_a, v_lru_w_x, v_lru_b_x, v_lru_lambda, v_attn_sink, v_rel_bias, v_lru_out_norm, v_attn_out_norm, v_w_out, v_ffn2_norm, v_ffn2_w_gate, v_ffn2_w_up, v_ffn2_w_down, v_final_norm):
    depth, d = ffn1_norm.shape
    s_len = x.shape[1]
    d_ff = N_CHIPS * ffn1_w_gate.shape[2]
    d_in = N_CHIPS * w_in.shape[2]
    lw = conv_b.shape[1]
    att = N_HEADS * HEAD_DIM
    ncb = lw // LANE
    lay = Layout(depth, d, d_ff, d_in)
    k_chip = 2 * lax.axis_index("x") + lax.axis_index("y")
    pos = jnp.stack([k_chip, lax.axis_index("c")]).astype(jnp.int32)

    mats = (ffn1_w_gate, ffn1_w_up, ffn1_w_down, ffn2_w_gate, ffn2_w_up, ffn2_w_down)
    wshard = _pack_weight_shard(lay, d, w_in, w_out, mats)
    sharded_small = (conv_w, lru_b_a, lru_b_x, lru_lambda)
    sshard = jnp.concatenate([a.reshape(-1, LANE) for a in sharded_small], axis=0)
    wfull, sfull = gather_weights(wshard, sshard)
    small_full, off = [], 0
    for a in sharded_small:
        r = a.shape[0] * a.shape[1]
        piece = sfull[:, off:off + r].reshape((N_CHIPS,) + a.shape)
        small_full.append(jnp.moveaxis(piece, 0, 2).reshape(a.shape[0], a.shape[1], N_CHIPS * LANE))
        off += r
    conv_w_f, b_a_f, b_x_f, lam_f = small_full

    zrow = jnp.zeros((1, lw), F32)
    wblk_a = _block_diag_pairs(lru_w_a)
    wblk_x = _block_diag_pairs(lru_w_x)
    buckets = _band_buckets()
    bias = jnp.transpose(rel_bias[buckets], (2, 0, 1))

    def layer_small(l):
        cvec = jnp.concatenate([conv_w_f[l], jnp.zeros((8 - CONV_WIDTH, lw), F32)], axis=0)
        pvec = jnp.concatenate([conv_b[l][None], b_a_f[l], b_x_f[l], lam_f[l], zrow], axis=0)
        wblk = jnp.stack([wblk_a[l, 0], wblk_x[l, 0], wblk_a[l, 1], wblk_x[l, 1]]).astype(BF)
        sink = jnp.broadcast_to(attn_sink[l][:, None], (N_HEADS, LANE))
        return cvec, pvec, wblk, sink

    xs = x[0]
    saved = []
    for l in range(depth):
        cvec, pvec, wblk, sink = layer_small(l)
        x1 = ffn_forward(xs, ffn1_norm[l][None], wfull, lay, l, 0)
        proj = mix_project(x1, mix_norm[l][None], wfull, lay, l)
        y_rec, hs = lru_forward(proj, cvec, pvec, wblk, lw)
        y_att = attention_forward(proj, bias, sink, lw, att)
        x2 = mix_output(x1, y_rec, y_att, lru_out_norm[l][None], attn_out_norm[l][None], wfull, lay, l)
        x3 = ffn_forward(x2, ffn2_norm[l][None], wfull, lay, l, 1)
        saved.append((xs, x1, x2, proj, y_rec, hs, y_att))
        xs = x3

    dx, d_final, loss_tile = loss_head(xs, final_norm[None], loss_target[0])
    loss = lax.psum(loss_tile[0, 0], ("x", "y", "c"))

    gb = lax.empty((N_CHIPS, 2, lay.rows, d), BF)
    g_small = {k: [None] * depth for k in ("ffn1_norm", "mix_norm", "ffn2_norm", "conv_w", "conv_b", "lru_w_a", "lru_b_a",
                                           "lru_w_x", "lru_b_x", "lru_lambda", "attn_sink", "lru_out_norm", "attn_out_norm")}
    dbias_total = jnp.zeros(bias.shape, F32)

    def ffn_back(xin, gain, dout, gb, l, which):
        dxo, dg, dgate, dup, act, h, df = ffn_backward_dx(xin, gain, dout, wfull, lay, l, which)
        gb = weight_grad_tn(dgate, h, gb, lay, lay.ffn_blk(l, 3 * which + 0))
        gb = weight_grad_tn(dup, h, gb, lay, lay.ffn_blk(l, 3 * which + 1))
        gb = weight_grad_tn(act, df, gb, lay, lay.ffn_blk(l, 3 * which + 2))
        return dxo, dg[0], gb

    for l in reversed(range(depth)):
        x0, x1, x2, proj, y_rec, hs, y_att = saved[l]
        cvec, pvec, wblk, sink = layer_small(l)
        dx, g_small["ffn2_norm"][l], gb = ffn_back(x2, ffn2_norm[l][None], dx, gb, l, 1)
        dyr, dya, dgr, dga, gb = mix_output_backward(dx, y_rec, y_att, lru_out_norm[l][None], attn_out_norm[l][None],
                                                     wfull, gb, lay, l)
        g_small["lru_out_norm"][l] = dgr[0]
        g_small["attn_out_norm"][l] = dga[0]
        dq, dkv, dbias, dsink = attention_backward(proj, y_att, dya, bias, sink, lw, att)
        dbias_total = dbias_total + dbias
        g_small["attn_sink"][l] = dsink[:, 0]
        dxr, dgt, dcv, dpv, dwb = lru_backward(proj, hs, dyr, cvec, pvec, wblk, lw)
        g_small["conv_w"][l] = dcv[:CONV_WIDTH]
        g_small["conv_b"][l] = dpv[0]
        g_small["lru_b_a"][l] = dpv[1:3]
        g_small["lru_b_x"][l] = dpv[3:5]
        g_small["lru_lambda"][l] = dpv[5:7]
        g_small["lru_w_a"][l] = _diag_blocks(jnp.stack([dwb[0], dwb[2]]))
        g_small["lru_w_x"][l] = _diag_blocks(jnp.stack([dwb[1], dwb[3]]))
        dx, dgm, gb = mix_project_backward(x1, mix_norm[l][None], dx, dxr, dgt, dq, dkv, wfull, gb, lay, l)
        g_small["mix_norm"][l] = dgm[0]
        dx, g_small["ffn1_norm"][l], gb = ffn_back(x0, ffn1_norm[l][None], dx, gb, l, 0)
    grad_x = dx[None]

    onehot = (buckets.reshape(-1)[:, None] == jnp.arange(N_BUCKETS)[None, :]).astype(F32)
    d_rel_bias = jnp.dot(dbias_total.reshape(N_HEADS, -1), onehot, precision=lax.Precision.HIGHEST).T

    small_names = ["ffn1_norm", "mix_norm", "conv_w", "conv_b", "lru_w_a", "lru_b_a", "lru_w_x", "lru_b_x", "lru_lambda",
                   "attn_sink", "rel_bias", "lru_out_norm", "attn_out_norm", "ffn2_norm", "final_norm"]
    small_part = {k: jnp.stack(v) for k, v in g_small.items()}
    small_part["rel_bias"] = d_rel_bias
    small_part["final_norm"] = d_final[0]
    sb = _pack_rows([small_part[k] for k in small_names], 1024)
    p1, sp1 = exchange_pair(gb, sb)
    cs = pair_sum(pos, gb, p1)
    ss = small_pair_sum(sb, sp1)
    p3, sp3 = exchange_chips(cs, ss)
    gf = exchange_final(chip_sum(pos, cs, p3))
    small_sum = _unpack_rows(small_chip_sum(sp3), [small_part[k].shape for k in small_names])
    grads = dict(zip(small_names, small_sum))
    for name in ("conv_w", "lru_b_a", "lru_b_x", "lru_lambda"):
        grads[name] = lax.dynamic_slice_in_dim(grads[name], k_chip * LANE, LANE, axis=2)

    def from_halves(row0, n, transpose):
        a = jnp.concatenate([gf[0, row0:row0 + n], gf[1, row0:row0 + n]], axis=0)
        return a.T if transpose else a

    grads["w_in"] = jnp.stack([from_halves(lay.win_row(l), lay.ih, True) for l in range(depth)])
    grads["w_out"] = jnp.stack([from_halves(lay.wout_row(l), lay.oh, False) for l in range(depth)])
    for m, name in enumerate(("ffn1_w_gate", "ffn1_w_up", "ffn1_w_down", "ffn2_w_gate", "ffn2_w_up", "ffn2_w_down")):
        grads[name] = jnp.stack([from_halves(lay.ffn_row(l, m), lay.fh, m % 3 != 2) for l in range(depth)])

    weights = dict(ffn1_norm=ffn1_norm, ffn1_w_gate=ffn1_w_gate, ffn1_w_up=ffn1_w_up, ffn1_w_down=ffn1_w_down, mix_norm=mix_norm, w_in=w_in, conv_w=conv_w, conv_b=conv_b, lru_w_a=lru_w_a, lru_b_a=lru_b_a, lru_w_x=lru_w_x, lru_b_x=lru_b_x, lru_lambda=lru_lambda, attn_sink=attn_sink, rel_bias=rel_bias, lru_out_norm=lru_out_norm, attn_out_norm=attn_out_norm, w_out=w_out, ffn2_norm=ffn2_norm, ffn2_w_gate=ffn2_w_gate, ffn2_w_up=ffn2_w_up, ffn2_w_down=ffn2_w_down, final_norm=final_norm)
    m_in = dict(ffn1_norm=m_ffn1_norm, ffn1_w_gate=m_ffn1_w_gate, ffn1_w_up=m_ffn1_w_up, ffn1_w_down=m_ffn1_w_down, mix_norm=m_mix_norm, w_in=m_w_in, conv_w=m_conv_w, conv_b=m_conv_b, lru_w_a=m_lru_w_a, lru_b_a=m_lru_b_a, lru_w_x=m_lru_w_x, lru_b_x=m_lru_b_x, lru_lambda=m_lru_lambda, attn_sink=m_attn_sink, rel_bias=m_rel_bias, lru_out_norm=m_lru_out_norm, attn_out_norm=m_attn_out_norm, w_out=m_w_out, ffn2_norm=m_ffn2_norm, ffn2_w_gate=m_ffn2_w_gate, ffn2_w_up=m_ffn2_w_up, ffn2_w_down=m_ffn2_w_down, final_norm=m_final_norm)
    v_in = dict(ffn1_norm=v_ffn1_norm, ffn1_w_gate=v_ffn1_w_gate, ffn1_w_up=v_ffn1_w_up, ffn1_w_down=v_ffn1_w_down, mix_norm=v_mix_norm, w_in=v_w_in, conv_w=v_conv_w, conv_b=v_conv_b, lru_w_a=v_lru_w_a, lru_b_a=v_lru_b_a, lru_w_x=v_lru_w_x, lru_b_x=v_lru_b_x, lru_lambda=v_lru_lambda, attn_sink=v_attn_sink, rel_bias=v_rel_bias, lru_out_norm=v_lru_out_norm, attn_out_norm=v_attn_out_norm, w_out=v_w_out, ffn2_norm=v_ffn2_norm, ffn2_w_gate=v_ffn2_w_gate, ffn2_w_up=v_ffn2_w_up, ffn2_w_down=v_ffn2_w_down, final_norm=v_final_norm)
    order = list(weights)
    large = ("ffn1_w_gate", "ffn1_w_up", "ffn1_w_down", "w_in", "w_out", "ffn2_w_gate", "ffn2_w_up", "ffn2_w_down")
    delta, new_m, new_v = {}, {}, {}
    for name in large:
        shp = weights[name].shape
        two_d = lambda a: a.reshape(-1, shp[-1])
        dl, nm, nv = adamw(two_d(weights[name]), two_d(grads[name]), two_d(m_in[name]), two_d(v_in[name]))
        delta[name], new_m[name], new_v[name] = dl.reshape(shp), nm.reshape(shp), nv.reshape(shp)
    small = [n for n in order if n not in large]
    packed = [_pack_rows([src[n] for n in small], 1024) for src in (weights, grads, m_in, v_in)]
    outs = adamw(*packed)
    shapes = [weights[n].shape for n in small]
    for dst, buf in zip((delta, new_m, new_v), outs):
        dst.update(zip(small, _unpack_rows(buf, shapes)))

    return (loss, grad_x, *[grads[n] for n in order], *[delta[n] for n in order],
            *[new_m[n] for n in order], *[new_v[n] for n in order])
```

```python
import functools
import math

import jax
import jax.numpy as jnp
import numpy as np
from jax import lax
from jax.experimental import pallas as pl
from jax.experimental.pallas import tpu as pltpu

BF = jnp.bfloat16
F32 = jnp.float32
SDS = jax.ShapeDtypeStruct
MESH = pl.DeviceIdType.MESH
ANY = pl.BlockSpec(memory_space=pl.ANY)

N_CHIPS = 4
N_HEADS = 8
N_KV_HEADS = 2
KV_GROUP = N_HEADS // N_KV_HEADS
HEAD_DIM = 64
BLOCK = 128
WINDOW = 128
N_BUCKETS = 32
MAX_DISTANCE = 128
LRU_C = 8.0
CONV_WIDTH = 4
LANE = 128
SCAN_SEGMENTS = 8
SCAN_CHAINS = 8
EPS = 1e-6
FFN_RES = 0.5
NEG_INF = -1e30
ADAM_LR = 0.001
ADAM_B1 = 0.9
ADAM_B2 = 0.999
ADAM_EPS = 1e-08
ADAM_WD = 0.01
ADAM_STEP = 10
VMEM_LIMIT = 60000 * 1024
GELU_C = math.sqrt(2.0 / math.pi)


def dot_nn(a, b):
    return lax.dot_general(a, b, (((1,), (0,)), ((), ())), preferred_element_type=F32)


def dot_nt(a, b):
    return lax.dot_general(a, b, (((1,), (1,)), ((), ())), preferred_element_type=F32)


def dot_tn(a, b):
    return lax.dot_general(a, b, (((0,), (0,)), ((), ())), preferred_element_type=F32)


def _cparams(**kw):
    return pltpu.CompilerParams(vmem_limit_bytes=VMEM_LIMIT, **kw)


class Layout:
    MIX_BLK = 6

    def __init__(self, d_model, d_ff, d_in):
        self.fh = d_ff // (2 * N_CHIPS)
        self.ih = d_in // (2 * N_CHIPS)
        self.oh = d_model // (2 * N_CHIPS)
        assert self.ih + self.oh == self.fh, "w_in^T and w_out rows must fill one ffn-sized block"
        self.rows = 7 * self.fh


def _row_chunk(rows, target, step=16):
    best = rows
    for c in range(step, min(rows, target) + 1, step):
        if rows % c == 0:
            best = c
    return best


def _mesh_pos():
    return lax.axis_index("x"), lax.axis_index("y"), lax.axis_index("c")


def _rcopy(src, dst, ssem, rsem, dev):
    return pltpu.make_async_remote_copy(src_ref=src, dst_ref=dst, send_sem=ssem, recv_sem=rsem,
                                        device_id=dev, device_id_type=MESH)


HBM = pl.BlockSpec(memory_space=pltpu.HBM)
SEM = pl.BlockSpec(memory_space=pltpu.SEMAPHORE)
DATAFLOW = pltpu.SideEffectType.DATAFLOW_SIDE_EFFECTING


def _chip_peers():
    x, y, c = _mesh_pos()
    peers = [(1 - x, y), (x, 1 - y), (1 - x, 1 - y)]
    return x, y, c, 2 * x + y, [(px, py, 2 * px + py) for px, py in peers]


def split_start(name, bufs, n, plan):
    nb = len(bufs)

    def body(*refs):
        sends, _ = plan(refs[:nb], refs[nb], refs[nb + 1])
        for cp in sends:
            cp.start()
        refs[-1][...] = jnp.zeros_like(refs[-1])

    out = pl.pallas_call(
        body, name=name,
        out_shape=(pltpu.SemaphoreType.DMA((n,)), pltpu.SemaphoreType.DMA((n,)),
                   *[pltpu.HBM(b.shape, b.dtype) for b in bufs], SDS((8, LANE), F32)),
        in_specs=[HBM] * nb, out_specs=(SEM, SEM, *([HBM] * nb), pl.BlockSpec(memory_space=pltpu.VMEM)),
        input_output_aliases={i: 2 + i for i in range(nb)},
        compiler_params=pltpu.CompilerParams(has_side_effects=DATAFLOW),
    )(*[pltpu.with_memory_space_constraint(b, pltpu.HBM) for b in bufs])
    return out[0], out[1], list(out[2:2 + nb]), out[-1]


def split_wait(name, ssem, rsem, bufs, after, plan):
    nb = len(bufs)

    def body(*refs):
        sends, recvs = plan(refs[:nb], refs[nb], refs[nb + 1])
        for cp in recvs:
            cp.wait_recv()
        for cp in sends:
            cp.wait_send()

    out = pl.pallas_call(
        body, name=name, out_shape=tuple(pltpu.HBM(b.shape, b.dtype) for b in bufs),
        in_specs=[HBM] * nb + [SEM, SEM, ANY], out_specs=tuple([HBM] * nb),
        input_output_aliases={i: i for i in range(nb)},
        compiler_params=pltpu.CompilerParams(has_side_effects=DATAFLOW),
    )(*bufs, ssem, rsem, after)
    return list(out)


def gather_plan(refs, ssem, rsem):
    w_ref, land_ref = refs
    _, _, c, k, peers = _chip_peers()
    sends = [_rcopy(w_ref.at[c], land_ref.at[k, c], ssem.at[j], rsem.at[j], (px, py, c))
             for j, (px, py, _) in enumerate(peers)]
    recvs = [_rcopy(w_ref.at[c], land_ref.at[kp, c], ssem.at[j], rsem.at[j], (px, py, c))
             for j, (px, py, kp) in enumerate(peers)]
    return sends, recvs


def reduce_plan(refs, ssem, rsem):
    cs_ref, ss_ref, p3_ref, sp3_ref = refs
    _, _, c, k, peers = _chip_peers()
    sends, recvs = [], []
    for j, (px, py, kp) in enumerate(peers):
        sends.append(_rcopy(cs_ref.at[kp], p3_ref.at[j], ssem.at[j], rsem.at[j], (px, py, c)))
        recvs.append(_rcopy(cs_ref.at[kp], p3_ref.at[j], ssem.at[j], rsem.at[j], (px, py, c)))
        sends.append(_rcopy(ss_ref, sp3_ref.at[k], ssem.at[3 + j], rsem.at[3 + j], (px, py, c)))
        recvs.append(_rcopy(ss_ref, sp3_ref.at[kp], ssem.at[3 + j], rsem.at[3 + j], (px, py, c)))
    return sends, recvs


def gather_small(sshard):
    def body(s_ref, sf_ref, lsem, ssem, rsem):
        _, _, c, k, peers = _chip_peers()
        own = pltpu.make_async_copy(s_ref, sf_ref.at[k], lsem)
        own.start()
        sends = [_rcopy(s_ref, sf_ref.at[k], ssem.at[j], rsem.at[j], (px, py, c)) for j, (px, py, _) in enumerate(peers)]
        recvs = [_rcopy(s_ref, sf_ref.at[kp], ssem.at[j], rsem.at[j], (px, py, c)) for j, (px, py, kp) in enumerate(peers)]
        for cp in sends:
            cp.start()
        for cp in recvs:
            cp.wait_recv()
        for cp in sends:
            cp.wait_send()
        own.wait()

    return pl.pallas_call(
        body, name="gather_small", out_shape=SDS((N_CHIPS,) + sshard.shape, sshard.dtype),
        in_specs=[ANY], out_specs=ANY,
        scratch_shapes=[pltpu.SemaphoreType.DMA, pltpu.SemaphoreType.DMA((3,)), pltpu.SemaphoreType.DMA((3,))],
    )(sshard)


def gather_pair(wshard, land):
    def body(w_ref, land_in, land_ref, lsem, ssem, rsem):
        x, y, c, k, peers = _chip_peers()
        sib = (x, y, 1 - c)
        own = pltpu.make_async_copy(w_ref, land_ref.at[k], lsem)
        own.start()
        sends = [_rcopy(land_ref.at[kp, c], land_ref.at[kp, c], ssem.at[j], rsem.at[j], sib) for j, (_, _, kp) in enumerate(peers)]
        recvs = [_rcopy(land_ref.at[kp, 1 - c], land_ref.at[kp, 1 - c], ssem.at[j], rsem.at[j], sib)
                 for j, (_, _, kp) in enumerate(peers)]
        for cp in sends:
            cp.start()
        for cp in recvs:
            cp.wait_recv()
        for cp in sends:
            cp.wait_send()
        own.wait()

    return pl.pallas_call(
        body, name="gather_pair", out_shape=SDS(land.shape, land.dtype),
        in_specs=[ANY, ANY], out_specs=ANY, input_output_aliases={1: 0},
        scratch_shapes=[pltpu.SemaphoreType.DMA, pltpu.SemaphoreType.DMA((3,)), pltpu.SemaphoreType.DMA((3,))],
    )(wshard, land)


def exchange_pair(gb, sb):
    n, _, rh, d = gb.shape

    def body(gb_ref, sb_ref, p_ref, sp_ref, ssem, rsem):
        x, y, c = _mesh_pos()
        sib = (x, y, 1 - c)
        sends = [_rcopy(gb_ref.at[kk, 1 - c], p_ref.at[kk], ssem.at[kk], rsem.at[kk], sib) for kk in range(n)]
        sends.append(_rcopy(sb_ref, sp_ref, ssem.at[n], rsem.at[n], sib))
        for cp in sends:
            cp.start()
        for cp in sends:
            cp.wait_recv()
        for cp in sends:
            cp.wait_send()

    return pl.pallas_call(
        body, name="exchange_pair",
        out_shape=(SDS((n, rh, d), gb.dtype), SDS(sb.shape, sb.dtype)),
        in_specs=[ANY, ANY], out_specs=(ANY, ANY),
        scratch_shapes=[pltpu.SemaphoreType.DMA((n + 1,)), pltpu.SemaphoreType.DMA((n + 1,))],
    )(gb, sb)


def exchange_final(gf):
    _, rh, d = gf.shape
    nch = 4 if rh % 32 == 0 else 1
    cr = rh // nch

    def body(gf_ref, out_ref, ssem, rsem):
        x, y, c = _mesh_pos()
        sib = (x, y, 1 - c)
        sends = [_rcopy(out_ref.at[c, pl.ds(q * cr, cr)], out_ref.at[c, pl.ds(q * cr, cr)], ssem.at[q], rsem.at[q], sib)
                 for q in range(nch)]
        recvs = [_rcopy(out_ref.at[1 - c, pl.ds(q * cr, cr)], out_ref.at[1 - c, pl.ds(q * cr, cr)], ssem.at[q], rsem.at[q], sib)
                 for q in range(nch)]
        for cp in sends:
            cp.start()
        for cp in recvs:
            cp.wait_recv()
        for cp in sends:
            cp.wait_send()

    return pl.pallas_call(
        body, name="exchange_final",
        out_shape=SDS(gf.shape, gf.dtype),
        in_specs=[ANY], out_specs=ANY, input_output_aliases={0: 0},
        scratch_shapes=[pltpu.SemaphoreType.DMA((nch,)), pltpu.SemaphoreType.DMA((nch,))],
    )(gf)


def pair_sum(pos, gb, p1):
    n, _, rh, d = gb.shape
    cr = _row_chunk(rh, 1024)

    def body(pos_ref, a_ref, b_ref, o_ref):
        o_ref[...] = (a_ref[...].astype(F32) + b_ref[...].astype(F32)).astype(o_ref.dtype)

    return pl.pallas_call(
        body, name="pair_sum", out_shape=SDS((n, rh, d), gb.dtype),
        grid_spec=pltpu.PrefetchScalarGridSpec(
            num_scalar_prefetch=1, grid=(n, rh // cr),
            in_specs=[pl.BlockSpec((None, None, cr, d), lambda kk, r, pos: (kk, pos[1], r, 0)),
                      pl.BlockSpec((None, cr, d), lambda kk, r, pos: (kk, r, 0))],
            out_specs=pl.BlockSpec((None, cr, d), lambda kk, r, pos: (kk, r, 0))),
        compiler_params=_cparams(),
    )(pos, gb, p1)


def chip_sum(pos, cs, p3):
    n, rh, d = cs.shape
    cr = _row_chunk(rh, 512)

    def body(pos_ref, a_ref, b_ref, o_ref):
        acc = a_ref[...].astype(F32)
        for j in range(3):
            acc = acc + b_ref[j].astype(F32)
        o_ref[...] = acc

    return pl.pallas_call(
        body, name="chip_sum", out_shape=SDS((2, rh, d), F32),
        grid_spec=pltpu.PrefetchScalarGridSpec(
            num_scalar_prefetch=1, grid=(rh // cr,),
            in_specs=[pl.BlockSpec((None, cr, d), lambda r, pos: (pos[0], r, 0)),
                      pl.BlockSpec((3, cr, d), lambda r, pos: (0, r, 0))],
            out_specs=pl.BlockSpec((None, cr, d), lambda r, pos: (pos[1], r, 0))),
        compiler_params=_cparams(),
    )(pos, cs, p3)


def small_pair_sum(a, b):
    def body(a_ref, b_ref, o_ref):
        o_ref[...] = a_ref[...] + b_ref[...]

    return pl.pallas_call(body, name="small_pair_sum", out_shape=SDS(a.shape, a.dtype),
                          compiler_params=_cparams())(a, b)


def small_chip_sum(pos, own, p):
    ns, w = own.shape

    def body(pos_ref, own_ref, p0, p1, p2, p3, o_ref):
        k = pos_ref[0]
        acc = None
        for chip, ref in enumerate((p0, p1, p2, p3)):
            term = jnp.where(k == chip, own_ref[...], ref[...])
            acc = term if acc is None else acc + term
        o_ref[...] = acc

    def slot(chip):
        return pl.BlockSpec((None, ns, w), lambda i, pos: (jnp.where(pos[0] == chip, (chip + 1) % N_CHIPS, chip), 0, 0))

    return pl.pallas_call(
        body, name="small_chip_sum", out_shape=SDS(own.shape, own.dtype),
        grid_spec=pltpu.PrefetchScalarGridSpec(
            num_scalar_prefetch=1, grid=(1,),
            in_specs=[pl.BlockSpec((ns, w), lambda i, pos: (0, 0))] + [slot(chip) for chip in range(N_CHIPS)],
            out_specs=pl.BlockSpec((ns, w), lambda i, pos: (0, 0))),
        compiler_params=_cparams(),
    )(pos, own, p, p, p, p)


def _rms(x, g):
    rs = lax.rsqrt(jnp.mean(x * x, axis=-1, keepdims=True) + EPS)
    xh = x * rs
    return xh, rs, xh * g


def _rms_bwd(dy, xh, rs, g):
    dxh = dy * g
    dx = rs * (dxh - xh * jnp.mean(dxh * xh, axis=-1, keepdims=True))
    return dx, dy * xh


def _gelu(x):
    t = jnp.tanh(GELU_C * (x + 0.044715 * x * x * x))
    return 0.5 * x * (1.0 + t), t


def _gelu_grad(x, t):
    return 0.5 * (1.0 + t) + 0.5 * x * (1.0 - t * t) * GELU_C * (1.0 + 3.0 * 0.044715 * x * x)


def _shift_rows(v, s, n):
    if s == 0:
        return v
    t = lax.broadcasted_iota(jnp.int32, v.shape, 0)
    rolled = pltpu.roll(v, (-s) % n, 0)
    return jnp.where((t + s >= 0) & (t + s < n), rolled, 0.0)


def _scan_rows(a_ref, u_ref, h_ref, acum_ref, reverse):
    s_len, w = a_ref.shape
    chains = max(1, min(SCAN_CHAINS, s_len // (8 * SCAN_SEGMENTS)))
    nseg = SCAN_SEGMENTS * chains
    seg = s_len // nseg

    def step(j, carry):
        jj = (seg - 1 - j) if reverse else j
        out = []
        for c, (h, acc) in enumerate(carry):
            idx = pl.ds(c * SCAN_SEGMENTS * seg + jj, SCAN_SEGMENTS, stride=seg)
            a = a_ref[idx, :]
            h = a * h + u_ref[idx, :]
            acc = a * acc
            h_ref[idx, :] = h
            acum_ref[idx, :] = acc
            out.append((h, acc))
        return tuple(out)

    init = tuple((jnp.zeros((SCAN_SEGMENTS, w), F32), jnp.ones((SCAN_SEGMENTS, w), F32)) for _ in range(chains))
    ends = lax.fori_loop(0, seg, step, init, unroll=min(8, seg))
    order = range(nseg - 2, -1, -1) if reverse else range(1, nseg)
    inflow = jnp.zeros((1, w), F32)
    for s in order:
        src = s + 1 if reverse else s - 1
        h, acc = ends[src // SCAN_SEGMENTS]
        r = src % SCAN_SEGMENTS
        inflow = h[r:r + 1, :] + acc[r:r + 1, :] * inflow
        rows = pl.ds(s * seg, seg)
        h_ref[rows, :] = h_ref[rows, :] + acum_ref[rows, :] * inflow


def _w_spec(rows_half, d, blk):
    return pl.BlockSpec((N_CHIPS, 2, rows_half, d), lambda *_: (0, 0, blk, 0), pipeline_mode=pl.Buffered(1))


def ffn_forward(x, gain, wfull, lay, which, deps=(), tm=512):
    s_len, d = x.shape
    tm = min(tm, s_len)
    f = 8 * lay.fh
    fc = f // 2

    def body(x_ref, g_ref, wg_ref, wu_ref, wd_ref, *rest):
        o_ref = rest[-1]
        x = x_ref[...]
        _, _, hn = _rms(x, g_ref[...])
        h = hn.astype(BF)
        y = jnp.zeros((tm, d), F32)
        for part in range(2):
            cols = slice(part * fc, (part + 1) * fc)
            gate = dot_nt(h, wg_ref[...].reshape(f, d)[cols])
            up = dot_nt(h, wu_ref[...].reshape(f, d)[cols])
            act = (gate * jax.nn.sigmoid(gate) * up).astype(BF)
            y = y + dot_nn(act, wd_ref[...].reshape(f, d)[cols])
        o_ref[...] = x + FFN_RES * y

    row = pl.BlockSpec((tm, d), lambda i: (i, 0))
    return pl.pallas_call(
        body, name="ffn_forward", grid=(s_len // tm,), out_shape=SDS((s_len, d), F32),
        in_specs=[row, pl.BlockSpec((1, d), lambda i: (0, 0))]
        + [_w_spec(lay.fh, d, 3 * which + m) for m in range(3)] + [ANY] * len(deps),
        out_specs=row, compiler_params=_cparams(),
    )(x, gain, wfull, wfull, wfull, *deps)


def ffn_backward_dx(x, gain, dout, wfull, lay, which, deps=(), tm=256):
    s_len, d = x.shape
    tm = min(tm, s_len)
    f = 8 * lay.fh
    fc = f // 2
    nt = s_len // tm

    def body(x_ref, g_ref, do_ref, wg_ref, wu_ref, wd_ref, *rest):
        dx_ref, dg_ref, dgate_ref, dup_ref, act_ref, h_ref, df_ref = rest[len(deps):]
        x = x_ref[...]
        g = g_ref[...]
        xh, rs, hn = _rms(x, g)
        h = hn.astype(BF)
        do = do_ref[...]
        df = (FFN_RES * do).astype(BF)
        dh = jnp.zeros((tm, d), F32)
        for part in range(2):
            cols = slice(part * fc, (part + 1) * fc)
            wg = wg_ref[...].reshape(f, d)[cols]
            wu = wu_ref[...].reshape(f, d)[cols]
            gate = dot_nt(h, wg)
            up = dot_nt(h, wu)
            sg = jax.nn.sigmoid(gate)
            silu = gate * sg
            dact = dot_nt(df, wd_ref[...].reshape(f, d)[cols])
            dup = (dact * silu).astype(BF)
            dgate = (dact * up * (sg * (1.0 + gate * (1.0 - sg)))).astype(BF)
            dh = dh + dot_nn(dgate, wg) + dot_nn(dup, wu)
            dgate_ref[:, cols] = dgate
            dup_ref[:, cols] = dup
            act_ref[:, cols] = (silu * up).astype(BF)
        dxn, dgrow = _rms_bwd(dh, xh, rs, g)
        dx_ref[...] = do + dxn

        @pl.when(pl.program_id(0) == 0)
        def _():
            dg_ref[...] = jnp.zeros_like(dg_ref)

        dg_ref[...] += jnp.sum(dgrow, axis=0, keepdims=True)
        h_ref[...] = h
        df_ref[...] = df

    row = pl.BlockSpec((tm, d), lambda i: (i, 0))
    wide = pl.BlockSpec((tm, f), lambda i: (i, 0))
    vec = pl.BlockSpec((1, d), lambda i: (0, 0))
    return pl.pallas_call(
        body, name="ffn_backward_dx", grid=(nt,),
        out_shape=(SDS((s_len, d), F32), SDS((1, d), F32), SDS((s_len, f), BF), SDS((s_len, f), BF),
                   SDS((s_len, f), BF), SDS((s_len, d), BF), SDS((s_len, d), BF)),
        in_specs=[row, vec, row] + [_w_spec(lay.fh, d, 3 * which + m) for m in range(3)] + [ANY] * len(deps),
        out_specs=(row, vec, wide, wide, wide, row, row), compiler_params=_cparams(),
    )(x, gain, dout, wfull, wfull, wfull, *deps)


def weight_grad_tn(a, b, gb, lay, blk, tk=512):
    s_len, f = a.shape
    tk = min(tk, s_len)
    d = b.shape[1]
    fc = f // 2
    nk = s_len // tk

    def body(a_ref, b_ref, gb_ref, o_ref, acc):
        kt = pl.program_id(1)

        @pl.when(kt == 0)
        def _():
            acc[...] = jnp.zeros_like(acc)

        acc[...] += dot_tn(a_ref[...], b_ref[...])

        @pl.when(kt == nk - 1)
        def _():
            for p in range(2):
                for q in range(2):
                    o_ref[p, q] = acc[pl.ds((2 * p + q) * lay.fh, lay.fh), :].astype(o_ref.dtype)

    return pl.pallas_call(
        body, name="weight_grad_tn", grid=(2, nk), out_shape=SDS(gb.shape, gb.dtype),
        in_specs=[pl.BlockSpec((tk, fc), lambda j, kt: (kt, j)), pl.BlockSpec((tk, d), lambda j, kt: (kt, 0)), ANY],
        out_specs=pl.BlockSpec((2, 2, lay.fh, d), lambda j, kt: (j, 0, blk, 0)),
        scratch_shapes=[pltpu.VMEM((fc, d), F32)],
        input_output_aliases={2: 0}, compiler_params=_cparams(),
    )(a, b, gb)


def mix_project(x, gain, wfull, lay, tm=512):
    s_len, d = x.shape
    tm = min(tm, s_len)
    d_in = 8 * lay.ih

    def body(x_ref, g_ref, w_ref, o_ref):
        _, _, hn = _rms(x_ref[...], g_ref[...])
        o_ref[...] = dot_nt(hn.astype(BF), w_ref[:, :, :lay.ih, :].reshape(d_in, d))

    return pl.pallas_call(
        body, name="mix_project", grid=(s_len // tm,), out_shape=SDS((s_len, d_in), F32),
        in_specs=[pl.BlockSpec((tm, d), lambda i: (i, 0)), pl.BlockSpec((1, d), lambda i: (0, 0)),
                  _w_spec(lay.fh, d, lay.MIX_BLK)],
        out_specs=pl.BlockSpec((tm, d_in), lambda i: (i, 0)), compiler_params=_cparams(),
    )(x, gain, wfull)


def mix_project_backward(x, gain, dout, dxr, dgt, dq, dkv, dwout, wfull, gb, lay, tm=512):
    s_len, d = x.shape
    tm = min(tm, s_len)
    d_in = 8 * lay.ih
    nt = s_len // tm
    lw = dxr.shape[1]
    kvw = dkv.shape[1]

    def body(x_ref, g_ref, do_ref, dxr_ref, dgt_ref, dq_ref, dkv_ref, dwo_ref, w_ref, gb_ref, dx_ref, dg_ref, o_ref, acc):
        i = pl.program_id(0)
        g = g_ref[...]
        xh, rs, hn = _rms(x_ref[...], g)
        h = hn.astype(BF)
        dp = jnp.concatenate([dxr_ref[...], dgt_ref[...], dq_ref[...], dkv_ref[...]], axis=1).astype(BF)
        dh = dot_nn(dp, w_ref[:, :, :lay.ih, :].reshape(d_in, d))
        dxn, dgrow = _rms_bwd(dh, xh, rs, g)
        dx_ref[...] = do_ref[...] + dxn

        @pl.when(i == 0)
        def _():
            dg_ref[...] = jnp.zeros_like(dg_ref)
            acc[...] = jnp.zeros_like(acc)

        dg_ref[...] += jnp.sum(dgrow, axis=0, keepdims=True)
        acc[...] += dot_tn(dp, h)

        @pl.when(i == nt - 1)
        def _():
            for p in range(N_CHIPS):
                for q in range(2):
                    o_ref[p, q, :lay.ih, :] = acc[pl.ds((2 * p + q) * lay.ih, lay.ih), :].astype(o_ref.dtype)
            o_ref[:, :, lay.ih:, :] = dwo_ref[...]

    row = pl.BlockSpec((tm, d), lambda i: (i, 0))
    vec = pl.BlockSpec((1, d), lambda i: (0, 0))
    return pl.pallas_call(
        body, name="mix_project_backward", grid=(nt,),
        out_shape=(SDS((s_len, d), F32), SDS((1, d), F32), SDS(gb.shape, gb.dtype)),
        in_specs=[row, vec, row, pl.BlockSpec((tm, lw), lambda i: (i, 0)), pl.BlockSpec((tm, lw), lambda i: (i, 0)),
                  pl.BlockSpec((tm, dq.shape[1]), lambda i: (i, 0)), pl.BlockSpec((tm, kvw), lambda i: (i, 0)),
                  pl.BlockSpec(dwout.shape, lambda i: (0, 0, 0, 0)), _w_spec(lay.fh, d, lay.MIX_BLK), ANY],
        out_specs=(row, vec, pl.BlockSpec((N_CHIPS, 2, lay.fh, d), lambda i: (0, 0, lay.MIX_BLK, 0))),
        scratch_shapes=[pltpu.VMEM((d_in, d), F32)],
        input_output_aliases={9: 2}, compiler_params=_cparams(),
    )(x, gain, dout, dxr, dgt, dq, dkv, dwout, wfull, gb)


def _lru_gates(xc, wb_ref, pv_ref, direction):
    xcb = xc.astype(BF)
    r = jax.nn.sigmoid(dot_nn(xcb, wb_ref[2 * direction]) + pv_ref[1 + direction:2 + direction, :])
    i = jax.nn.sigmoid(dot_nn(xcb, wb_ref[2 * direction + 1]) + pv_ref[3 + direction:4 + direction, :])
    lam = pv_ref[5 + direction:6 + direction, :]
    sp = jnp.maximum(-lam, 0.0) + jnp.log(1.0 + jnp.exp(-jnp.abs(lam)))
    a = jnp.exp(-LRU_C * sp * r)
    mult = jnp.sqrt(1.0 - a * a)
    return xcb, r, i, a, mult, sp


def _conv_rows(xr, cv_ref, bias, n):
    acc = bias + cv_ref[0:1, :] * _shift_rows(xr, -2, n)
    for j in range(1, CONV_WIDTH):
        acc = acc + cv_ref[j:j + 1, :] * _shift_rows(xr, j - 2, n)
    return acc


def lru_forward(proj, cvec, pvec, wblk, lw, deps=(), ch=512):
    s_len = proj.shape[0]
    ncb = lw // LANE
    ch = min(ch, s_len)
    nchunk = s_len // ch

    def body(xr_ref, gt_ref, cv_ref, pv_ref, wb_ref, *rest):
        y_ref, hs_ref, xc_s, a_s, u_s, acum_s = rest[len(deps):]
        xc_s[...] = _conv_rows(xr_ref[...], cv_ref, pv_ref[0:1, :], s_len)
        for direction in range(2):
            def fill(ci, _):
                rows = pl.ds(pl.multiple_of(ci * ch, ch), ch)
                xc = xc_s[rows, :]
                _, _, i, a, mult, _ = _lru_gates(xc, wb_ref, pv_ref, direction)
                a_s[rows, :] = a
                u_s[rows, :] = mult * (i * xc)
                return 0

            lax.fori_loop(0, nchunk, fill, 0)
            _scan_rows(a_s, u_s, hs_ref.at[direction], acum_s, reverse=direction == 1)

        def out(ci, _):
            rows = pl.ds(pl.multiple_of(ci * ch, ch), ch)
            gl, _ = _gelu(gt_ref[rows, :])
            y_ref[rows, :] = gl * (hs_ref[0, rows, :] + hs_ref[1, rows, :])
            return 0

        lax.fori_loop(0, nchunk, out, 0)

    col = lambda off: pl.BlockSpec((s_len, LANE), lambda cb: (0, off + cb))
    return pl.pallas_call(
        body, name="lru_forward", grid=(ncb,),
        out_shape=(SDS((s_len, lw), F32), SDS((2, s_len, lw), F32)),
        in_specs=[col(0), col(ncb), pl.BlockSpec((8, LANE), lambda cb: (0, cb)), pl.BlockSpec((8, LANE), lambda cb: (0, cb)),
                  pl.BlockSpec((4, None, LANE, LANE), lambda cb: (0, cb, 0, 0))] + [ANY] * len(deps),
        out_specs=(pl.BlockSpec((s_len, LANE), lambda cb: (0, cb)), pl.BlockSpec((2, s_len, LANE), lambda cb: (0, 0, cb))),
        scratch_shapes=[pltpu.VMEM((s_len, LANE), F32)] * 4, compiler_params=_cparams(),
    )(proj, proj, cvec, pvec, wblk, *deps)


def lru_backward(proj, hs, dy, cvec, pvec, wblk, lw, ch=512):
    s_len = proj.shape[0]
    ncb = lw // LANE
    ch = min(ch, s_len)
    nchunk = s_len // ch

    def body(xr_ref, gt_ref, hs_ref, dy_ref, cv_ref, pv_ref, wb_ref, dxr_ref, dgt_ref, dcv_ref, dpv_ref, dwb_ref,
             xc_s, a_s, dh_s, lam_s, hp_s, dxc_s, acum_s):
        xr = xr_ref[...]
        xc_s[...] = _conv_rows(xr, cv_ref, pv_ref[0:1, :], s_len)
        dxc_s[...] = jnp.zeros_like(dxc_s)
        dpv_ref[...] = jnp.zeros_like(dpv_ref)
        dwb_ref[...] = jnp.zeros_like(dwb_ref)

        def head(ci, _):
            rows = pl.ds(pl.multiple_of(ci * ch, ch), ch)
            gt = gt_ref[rows, :]
            gl, t = _gelu(gt)
            dy = dy_ref[rows, :]
            dh_s[rows, :] = dy * gl
            dgt_ref[rows, :] = dy * (hs_ref[0, rows, :] + hs_ref[1, rows, :]) * _gelu_grad(gt, t)
            return 0

        lax.fori_loop(0, nchunk, head, 0)

        for direction in range(2):
            def fill(ci, _):
                rows = pl.ds(pl.multiple_of(ci * ch, ch), ch)
                _, _, _, a, _, _ = _lru_gates(xc_s[rows, :], wb_ref, pv_ref, direction)
                a_s[rows, :] = a
                return 0

            lax.fori_loop(0, nchunk, fill, 0)
            toward = 1 if direction == 0 else -1
            hp_s[...] = _shift_rows(a_s[...], toward, s_len)
            _scan_rows(hp_s, dh_s, lam_s, acum_s, reverse=direction == 0)
            hp_s[...] = _shift_rows(hs_ref[direction], -toward, s_len)

            def grads(ci, _):
                rows = pl.ds(pl.multiple_of(ci * ch, ch), ch)
                xc = xc_s[rows, :]
                xcb, r, i, a, mult, sp = _lru_gates(xc, wb_ref, pv_ref, direction)
                du = lam_s[rows, :]
                da = du * hp_s[rows, :]
                dmult = du * i * xc
                di = du * mult * xc
                dlog_a = (da - dmult * a / mult) * a
                dr = dlog_a * (-LRU_C * sp)
                dza = dr * r * (1.0 - r)
                dzx = di * i * (1.0 - i)
                dzab = dza.astype(BF)
                dzxb = dzx.astype(BF)
                dxc_s[rows, :] += (du * mult * i + dot_nt(dzab, wb_ref[2 * direction])
                                   + dot_nt(dzxb, wb_ref[2 * direction + 1]))
                dwb_ref[2 * direction] += dot_tn(xcb, dzab)
                dwb_ref[2 * direction + 1] += dot_tn(xcb, dzxb)
                dpv_ref[1 + direction:2 + direction, :] += jnp.sum(dza, axis=0, keepdims=True)
                dpv_ref[3 + direction:4 + direction, :] += jnp.sum(dzx, axis=0, keepdims=True)
                dpv_ref[5 + direction:6 + direction, :] += jnp.sum(dlog_a * (-LRU_C * r), axis=0, keepdims=True)
                return 0

            lax.fori_loop(0, nchunk, grads, 0)

        for direction in range(2):
            lam = pv_ref[5 + direction:6 + direction, :]
            dpv_ref[5 + direction:6 + direction, :] = dpv_ref[5 + direction:6 + direction, :] * (-jax.nn.sigmoid(-lam))
        dxc = dxc_s[...]
        dpv_ref[0:1, :] = jnp.sum(dxc, axis=0, keepdims=True)
        dxr = cv_ref[0:1, :] * _shift_rows(dxc, 2, s_len)
        for j in range(1, CONV_WIDTH):
            dxr = dxr + cv_ref[j:j + 1, :] * _shift_rows(dxc, 2 - j, s_len)
        dxr_ref[...] = dxr
        dcv_ref[...] = jnp.zeros_like(dcv_ref)
        for j in range(CONV_WIDTH):
            dcv_ref[j:j + 1, :] = jnp.sum(dxc * _shift_rows(xr, j - 2, s_len), axis=0, keepdims=True)

    col = lambda off: pl.BlockSpec((s_len, LANE), lambda cb: (0, off + cb))
    own = pl.BlockSpec((s_len, LANE), lambda cb: (0, cb))
    small = pl.BlockSpec((8, LANE), lambda cb: (0, cb))
    wspec = pl.BlockSpec((4, None, LANE, LANE), lambda cb: (0, cb, 0, 0))
    return pl.pallas_call(
        body, name="lru_backward", grid=(ncb,),
        out_shape=(SDS((s_len, lw), F32), SDS((s_len, lw), F32), SDS((8, lw), F32), SDS((8, lw), F32),
                   SDS(wblk.shape, F32)),
        in_specs=[col(0), col(ncb), pl.BlockSpec((2, s_len, LANE), lambda cb: (0, 0, cb)), own, small, small, wspec],
        out_specs=(own, own, small, small, wspec),
        scratch_shapes=[pltpu.VMEM((s_len, LANE), F32)] * 7, compiler_params=_cparams(),
    )(proj, proj, hs, dy, cvec, pvec, wblk)


def _attn_specs(s_len, lw, att):
    nb = s_len // BLOCK
    qcol = 2 * lw // att
    kcol = (2 * lw + att) // BLOCK
    prev = lambda n: jnp.maximum(n - 1, 0)
    nxt = lambda n: jnp.minimum(n + 1, nb - 1)
    q = pl.BlockSpec((BLOCK, att), lambda n: (n, qcol))
    ks = [pl.BlockSpec((BLOCK, BLOCK), lambda n, f=f: (f(n), kcol)) for f in (prev, lambda n: n, nxt)]
    vs = [pl.BlockSpec((BLOCK, BLOCK), lambda n, f=f: (f(n), kcol + 1)) for f in (prev, lambda n: n, nxt)]
    return q, ks, vs


def _stack_heads(v, kh):
    return jnp.concatenate([v[:, (kh * KV_GROUP + g) * HEAD_DIM:(kh * KV_GROUP + g + 1) * HEAD_DIM]
                            for g in range(KV_GROUP)], axis=0)


def _key_exists(n, nb):
    j = lax.broadcasted_iota(jnp.int32, (1, 3 * BLOCK), 1)
    return ((n > 0) | (j >= BLOCK)) & ((n < nb - 1) | (j < 2 * BLOCK))


def _attn_probs(qs, kcat, bias_g, sink_g, key_ok):
    logits = jnp.where(key_ok, dot_nt(qs, kcat) + bias_g, NEG_INF)
    m = jnp.maximum(jnp.max(logits, axis=-1, keepdims=True), sink_g)
    p = jnp.exp(logits - m)
    es = jnp.exp(sink_g - m)
    inv = 1.0 / (jnp.sum(p, axis=-1, keepdims=True) + es)
    return p * inv, es * inv


def attention_forward(proj, bias, sink, lw, att):
    s_len = proj.shape[0]
    nb = s_len // BLOCK
    q_spec, k_specs, v_specs = _attn_specs(s_len, lw, att)

    def body(q_ref, kp_ref, kc_ref, kn_ref, vp_ref, vc_ref, vn_ref, b_ref, s_ref, o_ref):
        n = pl.program_id(0)
        q = q_ref[...]
        key_ok = _key_exists(n, nb)
        kall = jnp.concatenate([kp_ref[...], kc_ref[...], kn_ref[...]], axis=0).astype(BF)
        vall = jnp.concatenate([vp_ref[...], vc_ref[...], vn_ref[...]], axis=0).astype(BF)
        outs = []
        for kh in range(N_KV_HEADS):
            grp = slice(kh * KV_GROUP * BLOCK, (kh + 1) * KV_GROUP * BLOCK)
            qs = (_stack_heads(q, kh) * (HEAD_DIM ** -0.5)).astype(BF)
            bias_g = b_ref[kh * KV_GROUP:(kh + 1) * KV_GROUP].reshape(KV_GROUP * BLOCK, 3 * BLOCK)
            p, _ = _attn_probs(qs, kall[:, kh * HEAD_DIM:(kh + 1) * HEAD_DIM], bias_g, s_ref[grp, 0:1], key_ok)
            o = dot_nn(p.astype(BF), vall[:, kh * HEAD_DIM:(kh + 1) * HEAD_DIM])
            outs += [o[g * BLOCK:(g + 1) * BLOCK] for g in range(KV_GROUP)]
        o_ref[...] = jnp.concatenate(outs, axis=1)

    return pl.pallas_call(
        body, name="attention_forward", grid=(nb,), out_shape=SDS((s_len, att), F32),
        in_specs=[q_spec] + k_specs + v_specs
        + [pl.BlockSpec(bias.shape, lambda n: (0, 0, 0)), pl.BlockSpec(sink.shape, lambda n: (0, 0))],
        out_specs=pl.BlockSpec((BLOCK, att), lambda n: (n, 0)), compiler_params=_cparams(),
    )(proj, proj, proj, proj, proj, proj, proj, bias, sink)


def attention_backward(proj, y_att, dy, bias, sink, lw, att):
    s_len = proj.shape[0]
    nb = s_len // BLOCK
    kvw = N_KV_HEADS * HEAD_DIM
    q_spec, k_specs, v_specs = _attn_specs(s_len, lw, att)

    def body(q_ref, kp_ref, kc_ref, kn_ref, vp_ref, vc_ref, vn_ref, o_ref, do_ref, b_ref, s_ref,
             dq_ref, dkv_ref, db_ref, ds_ref):
        n = pl.program_id(0)

        @pl.when(n == 0)
        def _():
            dkv_ref[...] = jnp.zeros_like(dkv_ref)
            db_ref[...] = jnp.zeros_like(db_ref)
            ds_ref[...] = jnp.zeros_like(ds_ref)

        q = q_ref[...]
        o = o_ref[...]
        do = do_ref[...]
        kall = jnp.concatenate([kp_ref[...], kc_ref[...], kn_ref[...]], axis=0).astype(BF)
        vall = jnp.concatenate([vp_ref[...], vc_ref[...], vn_ref[...]], axis=0).astype(BF)
        key_ok = _key_exists(n, nb)
        dqs, dks, dvs = [], [], []
        for kh in range(N_KV_HEADS):
            heads = slice(kh * KV_GROUP, (kh + 1) * KV_GROUP)
            grp = slice(kh * KV_GROUP * BLOCK, (kh + 1) * KV_GROUP * BLOCK)
            kcat = kall[:, kh * HEAD_DIM:(kh + 1) * HEAD_DIM]
            vcat = vall[:, kh * HEAD_DIM:(kh + 1) * HEAD_DIM]
            qs = (_stack_heads(q, kh) * (HEAD_DIM ** -0.5)).astype(BF)
            bias_g = b_ref[heads].reshape(KV_GROUP * BLOCK, 3 * BLOCK)
            p, ps = _attn_probs(qs, kcat, bias_g, s_ref[grp, 0:1], key_ok)
            dos = _stack_heads(do, kh)
            dosb = dos.astype(BF)
            delta = jnp.sum(dos * _stack_heads(o, kh), axis=-1, keepdims=True)
            dlog = p * (dot_nt(dosb, vcat) - delta)
            dlogb = dlog.astype(BF)
            db_ref[heads] += dlog.reshape(KV_GROUP, BLOCK, 3 * BLOCK)
            dsink = -ps * delta
            for g in range(KV_GROUP):
                h = kh * KV_GROUP + g
                part = jnp.sum(dsink[g * BLOCK:(g + 1) * BLOCK], axis=0, keepdims=True)
                ds_ref[h:h + 1, :] += jnp.broadcast_to(part, (1, LANE))
            dqg = dot_nn(dlogb, kcat) * (HEAD_DIM ** -0.5)
            dqs += [dqg[g * BLOCK:(g + 1) * BLOCK] for g in range(KV_GROUP)]
            dks.append(dot_tn(dlogb, qs))
            dvs.append(dot_tn(p.astype(BF), dosb))
        dq_ref[...] = jnp.concatenate(dqs, axis=1)
        dkv = jnp.concatenate(dks + dvs, axis=1)
        starts = [jnp.maximum(n - 1, 0), n, jnp.minimum(n + 1, nb - 1)]
        for b, st in enumerate(starts):
            rows = pl.ds(pl.multiple_of(st * BLOCK, BLOCK), BLOCK)
            dkv_ref[rows, :] += dkv[b * BLOCK:(b + 1) * BLOCK, :]

    blk = pl.BlockSpec((BLOCK, att), lambda n: (n, 0))
    return pl.pallas_call(
        body, name="attention_backward", grid=(nb,),
        out_shape=(SDS((s_len, att), F32), SDS((s_len, 2 * kvw), F32), SDS(bias.shape, F32), SDS((N_HEADS, LANE), F32)),
        in_specs=[q_spec] + k_specs + v_specs
        + [blk, blk, pl.BlockSpec(bias.shape, lambda n: (0, 0, 0)), pl.BlockSpec(sink.shape, lambda n: (0, 0))],
        out_specs=(blk, pl.BlockSpec((s_len, 2 * kvw), lambda n: (0, 0)),
                   pl.BlockSpec(bias.shape, lambda n: (0, 0, 0)), pl.BlockSpec((N_HEADS, LANE), lambda n: (0, 0))),
        compiler_params=_cparams(),
    )(proj, proj, proj, proj, proj, proj, proj, y_att, dy, bias, sink)


def mix_output(x, y_rec, y_att, g_rec, g_att, wfull, lay, tm=512):
    s_len, d = x.shape
    tm = min(tm, s_len)
    lw = y_rec.shape[1]
    att = y_att.shape[1]

    def body(x_ref, yr_ref, ya_ref, gr_ref, ga_ref, w_ref, o_ref):
        _, _, nr = _rms(yr_ref[...], gr_ref[...])
        _, _, na = _rms(ya_ref[...], ga_ref[...])
        y = jnp.concatenate([nr, na], axis=1).astype(BF)
        o_ref[...] = x_ref[...] + dot_nn(y, w_ref[:, :, lay.ih:, :].reshape(d, d))

    row = pl.BlockSpec((tm, d), lambda i: (i, 0))
    return pl.pallas_call(
        body, name="mix_output", grid=(s_len // tm,), out_shape=SDS((s_len, d), F32),
        in_specs=[row, pl.BlockSpec((tm, lw), lambda i: (i, 0)), pl.BlockSpec((tm, att), lambda i: (i, 0)),
                  pl.BlockSpec((1, lw), lambda i: (0, 0)), pl.BlockSpec((1, att), lambda i: (0, 0)),
                  _w_spec(lay.fh, d, lay.MIX_BLK)],
        out_specs=row, compiler_params=_cparams(),
    )(x, y_rec, y_att, g_rec, g_att, wfull)


def mix_output_backward(dout, y_rec, y_att, g_rec, g_att, wfull, lay, tm=512):
    s_len, d = dout.shape
    tm = min(tm, s_len)
    lw = y_rec.shape[1]
    att = y_att.shape[1]
    nt = s_len // tm

    def body(do_ref, yr_ref, ya_ref, gr_ref, ga_ref, w_ref, dyr_ref, dya_ref, dgr_ref, dga_ref, o_ref, acc):
        i = pl.program_id(0)
        gr = gr_ref[...]
        ga = ga_ref[...]
        xhr, rsr, nr = _rms(yr_ref[...], gr)
        xha, rsa, na = _rms(ya_ref[...], ga)
        y = jnp.concatenate([nr, na], axis=1).astype(BF)
        dob = do_ref[...].astype(BF)
        dy = dot_nt(dob, w_ref[:, :, lay.ih:, :].reshape(d, d))
        dyr, dgr_row = _rms_bwd(dy[:, :lw], xhr, rsr, gr)
        dya, dga_row = _rms_bwd(dy[:, lw:], xha, rsa, ga)
        dyr_ref[...] = dyr
        dya_ref[...] = dya

        @pl.when(i == 0)
        def _():
            dgr_ref[...] = jnp.zeros_like(dgr_ref)
            dga_ref[...] = jnp.zeros_like(dga_ref)
            acc[...] = jnp.zeros_like(acc)

        dgr_ref[...] += jnp.sum(dgr_row, axis=0, keepdims=True)
        dga_ref[...] += jnp.sum(dga_row, axis=0, keepdims=True)
        acc[...] += dot_tn(y, dob)

        @pl.when(i == nt - 1)
        def _():
            for p in range(N_CHIPS):
                for q in range(2):
                    o_ref[p, q] = acc[pl.ds((2 * p + q) * lay.oh, lay.oh), :].astype(o_ref.dtype)

    row = pl.BlockSpec((tm, d), lambda i: (i, 0))
    return pl.pallas_call(
        body, name="mix_output_backward", grid=(nt,),
        out_shape=(SDS((s_len, lw), F32), SDS((s_len, att), F32), SDS((1, lw), F32), SDS((1, att), F32),
                   SDS((N_CHIPS, 2, lay.oh, d), BF)),
        in_specs=[row, pl.BlockSpec((tm, lw), lambda i: (i, 0)), pl.BlockSpec((tm, att), lambda i: (i, 0)),
                  pl.BlockSpec((1, lw), lambda i: (0, 0)), pl.BlockSpec((1, att), lambda i: (0, 0)),
                  _w_spec(lay.fh, d, lay.MIX_BLK)],
        out_specs=(pl.BlockSpec((tm, lw), lambda i: (i, 0)), pl.BlockSpec((tm, att), lambda i: (i, 0)),
                   pl.BlockSpec((1, lw), lambda i: (0, 0)), pl.BlockSpec((1, att), lambda i: (0, 0)),
                   pl.BlockSpec((N_CHIPS, 2, lay.oh, d), lambda i: (0, 0, 0, 0))),
        scratch_shapes=[pltpu.VMEM((d, d), F32)], compiler_params=_cparams(),
    )(dout, y_rec, y_att, g_rec, g_att, wfull)


def loss_head(x, gain, target, tm=512):
    s_len, d = x.shape
    tm = min(tm, s_len)

    def body(x_ref, g_ref, t_ref, dx_ref, dg_ref, loss_ref):
        g = g_ref[...]
        xh, rs, y = _rms(x_ref[...], g)
        err = y - t_ref[...]

        @pl.when(pl.program_id(0) == 0)
        def _():
            dg_ref[...] = jnp.zeros_like(dg_ref)
            loss_ref[...] = jnp.zeros_like(loss_ref)

        part = 0.5 * jnp.sum(jnp.mean(err * err, axis=-1, keepdims=True), axis=0, keepdims=True)
        loss_ref[...] += jnp.broadcast_to(part, loss_ref.shape)
        dx, dgrow = _rms_bwd(err * (1.0 / d), xh, rs, g)
        dx_ref[...] = dx
        dg_ref[...] += jnp.sum(dgrow, axis=0, keepdims=True)

    row = pl.BlockSpec((tm, d), lambda i: (i, 0))
    vec = pl.BlockSpec((1, d), lambda i: (0, 0))
    return pl.pallas_call(
        body, name="loss_head", grid=(s_len // tm,),
        out_shape=(SDS((s_len, d), F32), SDS((1, d), F32), SDS((8, LANE), F32)),
        in_specs=[row, vec, row], out_specs=(row, vec, pl.BlockSpec((8, LANE), lambda i: (0, 0))),
        compiler_params=_cparams(),
    )(x, gain, target)


def adamw(w, g, m, v, tr=512):
    rows, cols = w.shape
    tr = _row_chunk(rows, tr, 8)

    def body(w_ref, g_ref, m_ref, v_ref, d_ref, nm_ref, nv_ref):
        g = g_ref[...]
        m = ADAM_B1 * m_ref[...] + (1.0 - ADAM_B1) * g
        v = ADAM_B2 * v_ref[...] + (1.0 - ADAM_B2) * (g * g)
        m_hat = m / (1.0 - ADAM_B1 ** ADAM_STEP)
        v_hat = v / (1.0 - ADAM_B2 ** ADAM_STEP)
        d_ref[...] = -ADAM_LR * (m_hat / (jnp.sqrt(v_hat) + ADAM_EPS) + ADAM_WD * w_ref[...])
        nm_ref[...] = m
        nv_ref[...] = v

    blk = pl.BlockSpec((tr, cols), lambda i: (i, 0))
    return pl.pallas_call(
        body, name="adamw", grid=(rows // tr,), out_shape=(SDS(w.shape, F32),) * 3,
        in_specs=[blk] * 4, out_specs=(blk,) * 3, compiler_params=_cparams(),
    )(w, g, m, v)


def _pack_rows(arrays, width):
    flat = jnp.concatenate([a.reshape(-1).astype(F32) for a in arrays])
    rows = -(-flat.shape[0] // (8 * width)) * 8
    return jnp.pad(flat, (0, rows * width - flat.shape[0])).reshape(rows, width)


def _unpack_rows(buf, shapes):
    flat = buf.reshape(-1)
    out, off = [], 0
    for shp in shapes:
        n = int(np.prod(shp))
        out.append(flat[off:off + n].reshape(shp))
        off += n
    return out


def _t5_buckets(rel):
    half = N_BUCKETS // 2
    max_exact = half // 2
    ret = (rel > 0).astype(jnp.int32) * half
    n = jnp.abs(rel)
    n_f = jnp.maximum(n, 1).astype(F32)
    large = max_exact + (jnp.log(n_f / max_exact) / math.log(MAX_DISTANCE / max_exact) * (half - max_exact)).astype(jnp.int32)
    large = jnp.minimum(large, half - 1)
    return ret + jnp.where(n < max_exact, n, large)


def _band_buckets():
    t = jnp.arange(BLOCK)[:, None]
    j = jnp.arange(3 * BLOCK)[None, :]
    rel = j - BLOCK - t
    return _t5_buckets(rel), jnp.abs(rel) <= WINDOW


def _block_diag_pairs(w):
    depth, two, nblk, bw, _ = w.shape
    pairs = w.reshape(depth, two, nblk // 2, 2, bw, bw)
    z = jnp.zeros_like(pairs[:, :, :, 0])
    top = jnp.concatenate([pairs[:, :, :, 0], z], axis=-1)
    bot = jnp.concatenate([z, pairs[:, :, :, 1]], axis=-1)
    return jnp.concatenate([top, bot], axis=-2)


def _diag_blocks(dw):
    bw = dw.shape[-1] // 2
    a = dw[:, :, :bw, :bw]
    b = dw[:, :, bw:, bw:]
    return jnp.stack([a, b], axis=2).reshape(dw.shape[0], 2 * dw.shape[1], bw, bw)


def _pack_layer_shard(lay, w_in_l, w_out_l, mats_l):
    halves = []
    for hf in range(2):
        parts = []
        for m, a in enumerate(mats_l):
            a = a if m % 3 == 2 else a.T
            parts.append(a[hf * lay.fh:(hf + 1) * lay.fh])
        parts.append(w_in_l.T[hf * lay.ih:(hf + 1) * lay.ih])
        parts.append(w_out_l[hf * lay.oh:(hf + 1) * lay.oh])
        halves.append(jnp.concatenate(parts, axis=0))
    return jnp.stack(halves).astype(BF)


def kernel(x, ffn1_norm, ffn1_w_gate, ffn1_w_up, ffn1_w_down, mix_norm, w_in, conv_w, conv_b, lru_w_a, lru_b_a, lru_w_x, lru_b_x, lru_lambda, attn_sink, rel_bias, lru_out_norm, attn_out_norm, w_out, ffn2_norm, ffn2_w_gate, ffn2_w_up, ffn2_w_down, final_norm, loss_target, m_ffn1_norm, m_ffn1_w_gate, m_ffn1_w_up, m_ffn1_w_down, m_mix_norm, m_w_in, m_conv_w, m_conv_b, m_lru_w_a, m_lru_b_a, m_lru_w_x, m_lru_b_x, m_lru_lambda, m_attn_sink, m_rel_bias, m_lru_out_norm, m_attn_out_norm, m_w_out, m_ffn2_norm, m_ffn2_w_gate, m_ffn2_w_up, m_ffn2_w_down, m_final_norm, v_ffn1_norm, v_ffn1_w_gate, v_ffn1_w_up, v_ffn1_w_down, v_mix_norm, v_w_in, v_conv_w, v_conv_b, v_lru_w_a, v_lru_b_a, v_lru_w_x, v_lru_b_x, v_lru_lambda, v_attn_sink, v_rel_bias, v_lru_out_norm, v_attn_out_norm, v_w_out, v_ffn2_norm, v_ffn2_w_gate, v_ffn2_w_up, v_ffn2_w_down, v_final_norm):
    depth, d = ffn1_norm.shape
    d_ff = N_CHIPS * ffn1_w_gate.shape[2]
    d_in = N_CHIPS * w_in.shape[2]
    lw = conv_b.shape[1]
    att = N_HEADS * HEAD_DIM
    lay = Layout(d, d_ff, d_in)
    k_chip = 2 * lax.axis_index("x") + lax.axis_index("y")
    pos = jnp.stack([k_chip, lax.axis_index("c")]).astype(jnp.int32)

    mats = (ffn1_w_gate, ffn1_w_up, ffn1_w_down, ffn2_w_gate, ffn2_w_up, ffn2_w_down)
    wsh = [_pack_layer_shard(lay, w_in[l], w_out[l], [m[l] for m in mats]) for l in range(depth)]

    def gather_start(l):
        land = lax.empty((N_CHIPS, 2, lay.rows, d), BF)
        return split_start(f"gather_start_{l}", [wsh[l], land], 3, gather_plan)

    def gather_finish(l, started, after):
        ssem, rsem, bufs, _ = started
        shard, land = split_wait(f"gather_wait_{l}", ssem, rsem, bufs, after, gather_plan)
        return gather_pair(shard, land)

    sharded_small = (conv_w, lru_b_a, lru_b_x, lru_lambda)
    sshard = jnp.concatenate([a.reshape(-1, LANE) for a in sharded_small], axis=0)
    sfull = gather_small(sshard)
    small_full, off = [], 0
    for a in sharded_small:
        r = a.shape[0] * a.shape[1]
        piece = sfull[:, off:off + r].reshape((N_CHIPS,) + a.shape)
        small_full.append(jnp.moveaxis(piece, 0, 2).reshape(a.shape[0], a.shape[1], N_CHIPS * LANE))
        off += r
    conv_w_f, b_a_f, b_x_f, lam_f = small_full

    zrow = jnp.zeros((1, lw), F32)
    wblk_a = _block_diag_pairs(lru_w_a)
    wblk_x = _block_diag_pairs(lru_w_x)
    buckets, in_band = _band_buckets()
    onehot = (buckets.reshape(-1)[:, None] == jnp.arange(N_BUCKETS)[None, :]).astype(F32)
    bias = jnp.dot(rel_bias.T, onehot.T, precision=lax.Precision.HIGHEST).reshape(N_HEADS, BLOCK, 3 * BLOCK)
    bias = jnp.where(in_band[None], bias, NEG_INF)

    def layer_small(l):
        cvec = jnp.concatenate([conv_w_f[l], jnp.zeros((8 - CONV_WIDTH, lw), F32)], axis=0)
        pvec = jnp.concatenate([conv_b[l][None], b_a_f[l], b_x_f[l], lam_f[l], zrow], axis=0)
        wblk = jnp.stack([wblk_a[l, 0], wblk_x[l, 0], wblk_a[l, 1], wblk_x[l, 1]]).astype(BF)
        sink = jnp.broadcast_to(attn_sink[l][:, None, None], (N_HEADS, BLOCK, LANE)).reshape(N_HEADS * BLOCK, LANE)
        return cvec, pvec, wblk, sink

    xs = x[0]
    wfull = [None] * depth
    wfull[0] = gather_finish(0, gather_start(0), xs)
    started = gather_start(1) if depth > 1 else None
    saved = []
    for l in range(depth):
        cvec, pvec, wblk, sink = layer_small(l)
        deps = (started[3],) if started is not None else ()
        x1 = ffn_forward(xs, ffn1_norm[l][None], wfull[l], lay, 0, deps=deps)
        proj = mix_project(x1, mix_norm[l][None], wfull[l], lay)
        y_rec, hs = lru_forward(proj, cvec, pvec, wblk, lw)
        y_att = attention_forward(proj, bias, sink, lw, att)
        x2 = mix_output(x1, y_rec, y_att, lru_out_norm[l][None], attn_out_norm[l][None], wfull[l], lay)
        x3 = ffn_forward(x2, ffn2_norm[l][None], wfull[l], lay, 1)
        saved.append((xs, x1, x2, proj, y_rec, hs, y_att))
        xs = x3
        if l + 1 < depth:
            wfull[l + 1] = gather_finish(l + 1, started, x3)
            started = gather_start(l + 2) if l + 2 < depth else None

    dx, d_final, loss_tile = loss_head(xs, final_norm[None], loss_target[0])
    loss = lax.psum(loss_tile[0, 0], ("x", "y", "c"))

    layer_names = ["ffn1_norm", "mix_norm", "conv_w", "conv_b", "lru_w_a", "lru_b_a", "lru_w_x", "lru_b_x", "lru_lambda",
                   "attn_sink", "lru_out_norm", "attn_out_norm", "ffn2_norm"]
    dbias_total = jnp.zeros(bias.shape, F32)

    def ffn_back(xin, gain, dout, gb, l, which, deps=()):
        dxo, dg, dgate, dup, act, h, df = ffn_backward_dx(xin, gain, dout, wfull[l], lay, which, deps=deps)
        gb = weight_grad_tn(dgate, h, gb, lay, 3 * which + 0)
        gb = weight_grad_tn(dup, h, gb, lay, 3 * which + 1)
        gb = weight_grad_tn(act, df, gb, lay, 3 * which + 2)
        return dxo, dg[0], gb

    def reduce_start(l, gb, sb):
        p1, sp1 = exchange_pair(gb, sb)
        cs = pair_sum(pos, gb, p1)
        ss = small_pair_sum(sb, sp1)
        lands = [lax.empty((3,) + cs.shape[1:], cs.dtype), lax.empty((N_CHIPS,) + ss.shape, ss.dtype)]
        return split_start(f"reduce_start_{l}", [cs, ss] + lands, 6, reduce_plan)

    def reduce_finish(l, started, after):
        ssem, rsem, bufs, _ = started
        cs, ss, p3, sp3 = split_wait(f"reduce_wait_{l}", ssem, rsem, bufs, after, reduce_plan)
        return exchange_final(chip_sum(pos, cs, p3)), small_chip_sum(pos, ss, sp3)

    gf = [None] * depth
    small_sums = [None] * depth
    small_shapes = [None] * depth
    in_flight = None
    for l in reversed(range(depth)):
        x0, x1, x2, proj, y_rec, hs, y_att = saved[l]
        cvec, pvec, wblk, sink = layer_small(l)
        gb = lax.empty((N_CHIPS, 2, lay.rows, d), BF)
        part = {}
        deps = (in_flight[1][3],) if in_flight is not None else ()
        dx, part["ffn2_norm"], gb = ffn_back(x2, ffn2_norm[l][None], dx, gb, l, 1, deps=deps)
        dyr, dya, dgr, dga, dwout = mix_output_backward(dx, y_rec, y_att, lru_out_norm[l][None], attn_out_norm[l][None],
                                                        wfull[l], lay)
        part["lru_out_norm"] = dgr[0]
        part["attn_out_norm"] = dga[0]
        dq, dkv, dbias, dsink = attention_backward(proj, y_att, dya, bias, sink, lw, att)
        dbias_total = dbias_total + dbias
        part["attn_sink"] = dsink[:, 0]
        dxr, dgt, dcv, dpv, dwb = lru_backward(proj, hs, dyr, cvec, pvec, wblk, lw)
        part["conv_w"] = dcv[:CONV_WIDTH]
        part["conv_b"] = dpv[0]
        part["lru_b_a"] = dpv[1:3]
        part["lru_b_x"] = dpv[3:5]
        part["lru_lambda"] = dpv[5:7]
        part["lru_w_a"] = _diag_blocks(jnp.stack([dwb[0], dwb[2]]))
        part["lru_w_x"] = _diag_blocks(jnp.stack([dwb[1], dwb[3]]))
        dx, dgm, gb = mix_project_backward(x1, mix_norm[l][None], dx, dxr, dgt, dq, dkv, dwout, wfull[l], gb, lay)
        part["mix_norm"] = dgm[0]
        dx, part["ffn1_norm"], gb = ffn_back(x0, ffn1_norm[l][None], dx, gb, l, 0)
        pieces = [part[n] for n in layer_names]
        if l == 0:
            d_rel_bias = jnp.dot(dbias_total.reshape(N_HEADS, -1), onehot, precision=lax.Precision.HIGHEST).T
            pieces += [d_rel_bias, d_final[0]]
        small_shapes[l] = [p.shape for p in pieces]
        if in_flight is not None:
            gf[in_flight[0]], small_sums[in_flight[0]] = reduce_finish(in_flight[0], in_flight[1], dx)
        in_flight = (l, reduce_start(l, gb, _pack_rows(pieces, 1024)))
    gf[in_flight[0]], small_sums[in_flight[0]] = reduce_finish(in_flight[0], in_flight[1], dx)
    grad_x = dx[None]

    per_layer = [_unpack_rows(small_sums[l], small_shapes[l]) for l in range(depth)]
    grads = {n: jnp.stack([per_layer[l][i] for l in range(depth)]) for i, n in enumerate(layer_names)}
    grads["rel_bias"], grads["final_norm"] = per_layer[0][len(layer_names):]
    for name in ("conv_w", "lru_b_a", "lru_b_x", "lru_lambda"):
        grads[name] = lax.dynamic_slice_in_dim(grads[name], k_chip * LANE, LANE, axis=2)

    def from_halves(l, row0, n, transpose):
        a = jnp.concatenate([gf[l][0, row0:row0 + n], gf[l][1, row0:row0 + n]], axis=0)
        return a.T if transpose else a

    grads["w_in"] = jnp.stack([from_halves(l, 6 * lay.fh, lay.ih, True) for l in range(depth)])
    grads["w_out"] = jnp.stack([from_halves(l, 6 * lay.fh + lay.ih, lay.oh, False) for l in range(depth)])
    for m, name in enumerate(("ffn1_w_gate", "ffn1_w_up", "ffn1_w_down", "ffn2_w_gate", "ffn2_w_up", "ffn2_w_down")):
        grads[name] = jnp.stack([from_halves(l, m * lay.fh, lay.fh, m % 3 != 2) for l in range(depth)])

    weights = dict(ffn1_norm=ffn1_norm, ffn1_w_gate=ffn1_w_gate, ffn1_w_up=ffn1_w_up, ffn1_w_down=ffn1_w_down, mix_norm=mix_norm, w_in=w_in, conv_w=conv_w, conv_b=conv_b, lru_w_a=lru_w_a, lru_b_a=lru_b_a, lru_w_x=lru_w_x, lru_b_x=lru_b_x, lru_lambda=lru_lambda, attn_sink=attn_sink, rel_bias=rel_bias, lru_out_norm=lru_out_norm, attn_out_norm=attn_out_norm, w_out=w_out, ffn2_norm=ffn2_norm, ffn2_w_gate=ffn2_w_gate, ffn2_w_up=ffn2_w_up, ffn2_w_down=ffn2_w_down, final_norm=final_norm)
    m_in = dict(ffn1_norm=m_ffn1_norm, ffn1_w_gate=m_ffn1_w_gate, ffn1_w_up=m_ffn1_w_up, ffn1_w_down=m_ffn1_w_down, mix_norm=m_mix_norm, w_in=m_w_in, conv_w=m_conv_w, conv_b=m_conv_b, lru_w_a=m_lru_w_a, lru_b_a=m_lru_b_a, lru_w_x=m_lru_w_x, lru_b_x=m_lru_b_x, lru_lambda=m_lru_lambda, attn_sink=m_attn_sink, rel_bias=m_rel_bias, lru_out_norm=m_lru_out_norm, attn_out_norm=m_attn_out_norm, w_out=m_w_out, ffn2_norm=m_ffn2_norm, ffn2_w_gate=m_ffn2_w_gate, ffn2_w_up=m_ffn2_w_up, ffn2_w_down=m_ffn2_w_down, final_norm=m_final_norm)
    v_in = dict(ffn1_norm=v_ffn1_norm, ffn1_w_gate=v_ffn1_w_gate, ffn1_w_up=v_ffn1_w_up, ffn1_w_down=v_ffn1_w_down, mix_norm=v_mix_norm, w_in=v_w_in, conv_w=v_conv_w, conv_b=v_conv_b, lru_w_a=v_lru_w_a, lru_b_a=v_lru_b_a, lru_w_x=v_lru_w_x, lru_b_x=v_lru_b_x, lru_lambda=v_lru_lambda, attn_sink=v_attn_sink, rel_bias=v_rel_bias, lru_out_norm=v_lru_out_norm, attn_out_norm=v_attn_out_norm, w_out=v_w_out, ffn2_norm=v_ffn2_norm, ffn2_w_gate=v_ffn2_w_gate, ffn2_w_up=v_ffn2_w_up, ffn2_w_down=v_ffn2_w_down, final_norm=v_final_norm)
    order = list(weights)
    large = ("ffn1_w_gate", "ffn1_w_up", "ffn1_w_down", "w_in", "w_out", "ffn2_w_gate", "ffn2_w_up", "ffn2_w_down")
    delta, new_m, new_v = {}, {}, {}
    for name in large:
        shp = weights[name].shape
        two_d = lambda a: a.reshape(-1, shp[-1])
        dl, nm, nv = adamw(two_d(weights[name]), two_d(grads[name]), two_d(m_in[name]), two_d(v_in[name]))
        delta[name], new_m[name], new_v[name] = dl.reshape(shp), nm.reshape(shp), nv.reshape(shp)
    small = [n for n in order if n not in large]
    packed = [_pack_rows([src[n] for n in small], 1024) for src in (weights, grads, m_in, v_in)]
    outs = adamw(*packed)
    shapes = [weights[n].shape for n in small]
    for dst, buf in zip((delta, new_m, new_v), outs):
        dst.update(zip(small, _unpack_rows(buf, shapes)))

    return (loss, grad_x, *[grads[n] for n in order], *[delta[n] for n in order],
            *[new_m[n] for n in order], *[new_v[n] for n in order])
```

```python
import functools
import math

import jax
import jax.numpy as jnp
import numpy as np
from jax import lax
from jax.experimental import pallas as pl
from jax.experimental.pallas import tpu as pltpu

BF = jnp.bfloat16
F32 = jnp.float32
SDS = jax.ShapeDtypeStruct
MESH = pl.DeviceIdType.MESH
ANY = pl.BlockSpec(memory_space=pl.ANY)

N_CHIPS = 4
N_HEADS = 8
N_KV_HEADS = 2
KV_GROUP = N_HEADS // N_KV_HEADS
HEAD_DIM = 64
BLOCK = 128
WINDOW = 128
N_BUCKETS = 32
MAX_DISTANCE = 128
LRU_C = 8.0
CONV_WIDTH = 4
LANE = 128
SCAN_SEGMENTS = 8
SCAN_CHAINS = 8
EPS = 1e-6
FFN_RES = 0.5
NEG_INF = -1e30
ADAM_LR = 0.001
ADAM_B1 = 0.9
ADAM_B2 = 0.999
ADAM_EPS = 1e-08
ADAM_WD = 0.01
ADAM_STEP = 10
VMEM_LIMIT = 60000 * 1024
GELU_C = math.sqrt(2.0 / math.pi)


def dot_nn(a, b):
    return lax.dot_general(a, b, (((1,), (0,)), ((), ())), preferred_element_type=F32)


def dot_nt(a, b):
    return lax.dot_general(a, b, (((1,), (1,)), ((), ())), preferred_element_type=F32)


def dot_tn(a, b):
    return lax.dot_general(a, b, (((0,), (0,)), ((), ())), preferred_element_type=F32)


def _cparams(**kw):
    return pltpu.CompilerParams(vmem_limit_bytes=VMEM_LIMIT, **kw)


class Layout:
    MIX_BLK = 6

    def __init__(self, d_model, d_ff, d_in):
        self.fh = d_ff // (2 * N_CHIPS)
        self.ih = d_in // (2 * N_CHIPS)
        self.oh = d_model // (2 * N_CHIPS)
        assert self.ih + self.oh == self.fh, "w_in^T and w_out rows must fill one ffn-sized block"
        self.rows = 7 * self.fh


def _row_chunk(rows, target, step=16):
    best = rows
    for c in range(step, min(rows, target) + 1, step):
        if rows % c == 0:
            best = c
    return best


def _mesh_pos():
    return lax.axis_index("x"), lax.axis_index("y"), lax.axis_index("c")


def _rcopy(src, dst, ssem, rsem, dev):
    return pltpu.make_async_remote_copy(src_ref=src, dst_ref=dst, send_sem=ssem, recv_sem=rsem,
                                        device_id=dev, device_id_type=MESH)


HBM = pl.BlockSpec(memory_space=pltpu.HBM)
SEM = pl.BlockSpec(memory_space=pltpu.SEMAPHORE)
DATAFLOW = pltpu.SideEffectType.DATAFLOW_SIDE_EFFECTING


def _chip_peers():
    x, y, c = _mesh_pos()
    peers = [(1 - x, y), (x, 1 - y), (1 - x, 1 - y)]
    return x, y, c, 2 * x + y, [(px, py, 2 * px + py) for px, py in peers]


def split_start(name, bufs, n, plan):
    nb = len(bufs)

    def body(*refs):
        sends, _ = plan(refs[:nb], refs[nb], refs[nb + 1])
        for cp in sends:
            cp.start()
        refs[-1][...] = jnp.zeros_like(refs[-1])

    out = pl.pallas_call(
        body, name=name,
        out_shape=(pltpu.SemaphoreType.DMA((n,)), pltpu.SemaphoreType.DMA((n,)),
                   *[pltpu.HBM(b.shape, b.dtype) for b in bufs], SDS((8, LANE), F32)),
        in_specs=[HBM] * nb, out_specs=(SEM, SEM, *([HBM] * nb), pl.BlockSpec(memory_space=pltpu.VMEM)),
        input_output_aliases={i: 2 + i for i in range(nb)},
        compiler_params=pltpu.CompilerParams(has_side_effects=DATAFLOW),
    )(*[pltpu.with_memory_space_constraint(b, pltpu.HBM) for b in bufs])
    return out[0], out[1], list(out[2:2 + nb]), out[-1]


def split_wait(name, ssem, rsem, bufs, after, plan):
    nb = len(bufs)

    def body(*refs):
        sends, recvs = plan(refs[:nb], refs[nb], refs[nb + 1])
        for cp in recvs:
            cp.wait_recv()
        for cp in sends:
            cp.wait_send()

    out = pl.pallas_call(
        body, name=name, out_shape=tuple(pltpu.HBM(b.shape, b.dtype) for b in bufs),
        in_specs=[HBM] * nb + [SEM, SEM, ANY], out_specs=tuple([HBM] * nb),
        input_output_aliases={i: i for i in range(nb)},
        compiler_params=pltpu.CompilerParams(has_side_effects=DATAFLOW),
    )(*bufs, ssem, rsem, after)
    return list(out)


def gather_plan(refs, ssem, rsem):
    land_ref, = refs
    _, _, c, k, peers = _chip_peers()
    sends = [_rcopy(land_ref.at[k, c], land_ref.at[k, c], ssem.at[j], rsem.at[j], (px, py, c))
             for j, (px, py, _) in enumerate(peers)]
    recvs = [_rcopy(land_ref.at[kp, c], land_ref.at[kp, c], ssem.at[j], rsem.at[j], (px, py, c))
             for j, (px, py, kp) in enumerate(peers)]
    return sends, recvs


def reduce_plan(refs, ssem, rsem):
    cs_ref, ss_ref, p3_ref, sp3_ref = refs
    _, _, c, k, peers = _chip_peers()
    sends, recvs = [], []
    for j, (px, py, kp) in enumerate(peers):
        sends.append(_rcopy(cs_ref.at[kp], p3_ref.at[j], ssem.at[j], rsem.at[j], (px, py, c)))
        recvs.append(_rcopy(cs_ref.at[kp], p3_ref.at[j], ssem.at[j], rsem.at[j], (px, py, c)))
        sends.append(_rcopy(ss_ref, sp3_ref.at[k], ssem.at[3 + j], rsem.at[3 + j], (px, py, c)))
        recvs.append(_rcopy(ss_ref, sp3_ref.at[kp], ssem.at[3 + j], rsem.at[3 + j], (px, py, c)))
    return sends, recvs


def gather_small(sshard):
    def body(s_ref, sf_ref, lsem, ssem, rsem):
        _, _, c, k, peers = _chip_peers()
        own = pltpu.make_async_copy(s_ref, sf_ref.at[k], lsem)
        own.start()
        sends = [_rcopy(s_ref, sf_ref.at[k], ssem.at[j], rsem.at[j], (px, py, c)) for j, (px, py, _) in enumerate(peers)]
        recvs = [_rcopy(s_ref, sf_ref.at[kp], ssem.at[j], rsem.at[j], (px, py, c)) for j, (px, py, kp) in enumerate(peers)]
        for cp in sends:
            cp.start()
        for cp in recvs:
            cp.wait_recv()
        for cp in sends:
            cp.wait_send()
        own.wait()

    return pl.pallas_call(
        body, name="gather_small", out_shape=SDS((N_CHIPS,) + sshard.shape, sshard.dtype),
        in_specs=[ANY], out_specs=ANY,
        scratch_shapes=[pltpu.SemaphoreType.DMA, pltpu.SemaphoreType.DMA((3,)), pltpu.SemaphoreType.DMA((3,))],
    )(sshard)


def gather_pair(land):
    def body(land_in, land_ref, ssem, rsem):
        x, y, c, k, peers = _chip_peers()
        sib = (x, y, 1 - c)
        sends = [_rcopy(land_ref.at[kp, c], land_ref.at[kp, c], ssem.at[j], rsem.at[j], sib) for j, (_, _, kp) in enumerate(peers)]
        recvs = [_rcopy(land_ref.at[kp, 1 - c], land_ref.at[kp, 1 - c], ssem.at[j], rsem.at[j], sib)
                 for j, (_, _, kp) in enumerate(peers)]
        for cp in sends:
            cp.start()
        for cp in recvs:
            cp.wait_recv()
        for cp in sends:
            cp.wait_send()

    return pl.pallas_call(
        body, name="gather_pair", out_shape=SDS(land.shape, land.dtype),
        in_specs=[ANY], out_specs=ANY, input_output_aliases={0: 0},
        scratch_shapes=[pltpu.SemaphoreType.DMA((3,)), pltpu.SemaphoreType.DMA((3,))],
    )(land)


def exchange_pair(gb, sb):
    n, _, rh, d = gb.shape

    def body(gb_ref, sb_ref, p_ref, sp_ref, ssem, rsem):
        x, y, c = _mesh_pos()
        sib = (x, y, 1 - c)
        sends = [_rcopy(gb_ref.at[kk, 1 - c], p_ref.at[kk], ssem.at[kk], rsem.at[kk], sib) for kk in range(n)]
        sends.append(_rcopy(sb_ref, sp_ref, ssem.at[n], rsem.at[n], sib))
        for cp in sends:
            cp.start()
        for cp in sends:
            cp.wait_recv()
        for cp in sends:
            cp.wait_send()

    return pl.pallas_call(
        body, name="exchange_pair",
        out_shape=(SDS((n, rh, d), gb.dtype), SDS(sb.shape, sb.dtype)),
        in_specs=[ANY, ANY], out_specs=(ANY, ANY),
        scratch_shapes=[pltpu.SemaphoreType.DMA((n + 1,)), pltpu.SemaphoreType.DMA((n + 1,))],
    )(gb, sb)


def exchange_final(gf):
    _, rh, d = gf.shape
    nch = 4 if rh % 32 == 0 else 1
    cr = rh // nch

    def body(gf_ref, out_ref, ssem, rsem):
        x, y, c = _mesh_pos()
        sib = (x, y, 1 - c)
        sends = [_rcopy(out_ref.at[c, pl.ds(q * cr, cr)], out_ref.at[c, pl.ds(q * cr, cr)], ssem.at[q], rsem.at[q], sib)
                 for q in range(nch)]
        recvs = [_rcopy(out_ref.at[1 - c, pl.ds(q * cr, cr)], out_ref.at[1 - c, pl.ds(q * cr, cr)], ssem.at[q], rsem.at[q], sib)
                 for q in range(nch)]
        for cp in sends:
            cp.start()
        for cp in recvs:
            cp.wait_recv()
        for cp in sends:
            cp.wait_send()

    return pl.pallas_call(
        body, name="exchange_final",
        out_shape=SDS(gf.shape, gf.dtype),
        in_specs=[ANY], out_specs=ANY, input_output_aliases={0: 0},
        scratch_shapes=[pltpu.SemaphoreType.DMA((nch,)), pltpu.SemaphoreType.DMA((nch,))],
    )(gf)


def pair_sum(pos, gb, p1):
    n, _, rh, d = gb.shape
    cr = _row_chunk(rh, 1024)

    def body(pos_ref, a_ref, b_ref, o_ref):
        o_ref[...] = (a_ref[...].astype(F32) + b_ref[...].astype(F32)).astype(o_ref.dtype)

    return pl.pallas_call(
        body, name="pair_sum", out_shape=SDS((n, rh, d), gb.dtype),
        grid_spec=pltpu.PrefetchScalarGridSpec(
            num_scalar_prefetch=1, grid=(n, rh // cr),
            in_specs=[pl.BlockSpec((None, None, cr, d), lambda kk, r, pos: (kk, pos[1], r, 0)),
                      pl.BlockSpec((None, cr, d), lambda kk, r, pos: (kk, r, 0))],
            out_specs=pl.BlockSpec((None, cr, d), lambda kk, r, pos: (kk, r, 0))),
        compiler_params=_cparams(),
    )(pos, gb, p1)


def chip_sum(pos, cs, p3):
    n, rh, d = cs.shape
    cr = _row_chunk(rh, 512)

    def body(pos_ref, a_ref, b_ref, o_ref):
        acc = a_ref[...].astype(F32)
        for j in range(3):
            acc = acc + b_ref[j].astype(F32)
        o_ref[...] = acc

    return pl.pallas_call(
        body, name="chip_sum", out_shape=SDS((2, rh, d), F32),
        grid_spec=pltpu.PrefetchScalarGridSpec(
            num_scalar_prefetch=1, grid=(rh // cr,),
            in_specs=[pl.BlockSpec((None, cr, d), lambda r, pos: (pos[0], r, 0)),
                      pl.BlockSpec((3, cr, d), lambda r, pos: (0, r, 0))],
            out_specs=pl.BlockSpec((None, cr, d), lambda r, pos: (pos[1], r, 0))),
        compiler_params=_cparams(),
    )(pos, cs, p3)


def small_pair_sum(a, b):
    def body(a_ref, b_ref, o_ref):
        o_ref[...] = a_ref[...] + b_ref[...]

    return pl.pallas_call(body, name="small_pair_sum", out_shape=SDS(a.shape, a.dtype),
                          compiler_params=_cparams())(a, b)


def small_chip_sum(pos, own, p):
    ns, w = own.shape

    def body(pos_ref, own_ref, p0, p1, p2, p3, o_ref):
        k = pos_ref[0]
        acc = None
        for chip, ref in enumerate((p0, p1, p2, p3)):
            term = jnp.where(k == chip, own_ref[...], ref[...])
            acc = term if acc is None else acc + term
        o_ref[...] = acc

    def slot(chip):
        return pl.BlockSpec((None, ns, w), lambda i, pos: (jnp.where(pos[0] == chip, (chip + 1) % N_CHIPS, chip), 0, 0))

    return pl.pallas_call(
        body, name="small_chip_sum", out_shape=SDS(own.shape, own.dtype),
        grid_spec=pltpu.PrefetchScalarGridSpec(
            num_scalar_prefetch=1, grid=(1,),
            in_specs=[pl.BlockSpec((ns, w), lambda i, pos: (0, 0))] + [slot(chip) for chip in range(N_CHIPS)],
            out_specs=pl.BlockSpec((ns, w), lambda i, pos: (0, 0))),
        compiler_params=_cparams(),
    )(pos, own, p, p, p, p)


def _rms(x, g):
    rs = lax.rsqrt(jnp.mean(x * x, axis=-1, keepdims=True) + EPS)
    xh = x * rs
    return xh, rs, xh * g


def _rms_bwd(dy, xh, rs, g):
    dxh = dy * g
    dx = rs * (dxh - xh * jnp.mean(dxh * xh, axis=-1, keepdims=True))
    return dx, dy * xh


def _gelu(x):
    t = jnp.tanh(GELU_C * (x + 0.044715 * x * x * x))
    return 0.5 * x * (1.0 + t), t


def _gelu_grad(x, t):
    return 0.5 * (1.0 + t) + 0.5 * x * (1.0 - t * t) * GELU_C * (1.0 + 3.0 * 0.044715 * x * x)


def _shift_rows(v, s, n):
    if s == 0:
        return v
    t = lax.broadcasted_iota(jnp.int32, v.shape, 0)
    rolled = pltpu.roll(v, (-s) % n, 0)
    return jnp.where((t + s >= 0) & (t + s < n), rolled, 0.0)


def _scan_rows(a_ref, u_ref, h_ref, acum_ref, reverse):
    s_len, w = a_ref.shape
    chains = max(1, min(SCAN_CHAINS, s_len // (8 * SCAN_SEGMENTS)))
    nseg = SCAN_SEGMENTS * chains
    seg = s_len // nseg

    def step(j, carry):
        jj = (seg - 1 - j) if reverse else j
        out = []
        for c, (h, acc) in enumerate(carry):
            idx = pl.ds(c * SCAN_SEGMENTS * seg + jj, SCAN_SEGMENTS, stride=seg)
            a = a_ref[idx, :]
            h = a * h + u_ref[idx, :]
            acc = a * acc
            h_ref[idx, :] = h
            acum_ref[idx, :] = acc
            out.append((h, acc))
        return tuple(out)

    init = tuple((jnp.zeros((SCAN_SEGMENTS, w), F32), jnp.ones((SCAN_SEGMENTS, w), F32)) for _ in range(chains))
    ends = lax.fori_loop(0, seg, step, init, unroll=min(8, seg))
    order = range(nseg - 2, -1, -1) if reverse else range(1, nseg)
    inflow = jnp.zeros((1, w), F32)
    for s in order:
        src = s + 1 if reverse else s - 1
        h, acc = ends[src // SCAN_SEGMENTS]
        r = src % SCAN_SEGMENTS
        inflow = h[r:r + 1, :] + acc[r:r + 1, :] * inflow
        rows = pl.ds(s * seg, seg)
        h_ref[rows, :] = h_ref[rows, :] + acum_ref[rows, :] * inflow


def _w_spec(rows_half, d, blk):
    return pl.BlockSpec((N_CHIPS, 2, rows_half, d), lambda *_: (0, 0, blk, 0), pipeline_mode=pl.Buffered(1))


def ffn_forward(x, gain, wfull, lay, which, deps=(), tm=512):
    s_len, d = x.shape
    tm = min(tm, s_len)
    f = 8 * lay.fh
    fc = f // 2

    def body(x_ref, g_ref, wg_ref, wu_ref, wd_ref, *rest):
        o_ref = rest[-1]
        x = x_ref[...]
        _, _, hn = _rms(x, g_ref[...])
        h = hn.astype(BF)
        y = jnp.zeros((tm, d), F32)
        for part in range(2):
            cols = slice(part * fc, (part + 1) * fc)
            gate = dot_nt(h, wg_ref[...].reshape(f, d)[cols])
            up = dot_nt(h, wu_ref[...].reshape(f, d)[cols])
            act = (gate * jax.nn.sigmoid(gate) * up).astype(BF)
            y = y + dot_nn(act, wd_ref[...].reshape(f, d)[cols])
        o_ref[...] = x + FFN_RES * y

    row = pl.BlockSpec((tm, d), lambda i: (i, 0))
    return pl.pallas_call(
        body, name="ffn_forward", grid=(s_len // tm,), out_shape=SDS((s_len, d), F32),
        in_specs=[row, pl.BlockSpec((1, d), lambda i: (0, 0))]
        + [_w_spec(lay.fh, d, 3 * which + m) for m in range(3)] + [ANY] * len(deps),
        out_specs=row, compiler_params=_cparams(),
    )(x, gain, wfull, wfull, wfull, *deps)


def ffn_backward_dx(x, gain, dout, wfull, lay, which, deps=(), tm=256):
    s_len, d = x.shape
    tm = min(tm, s_len)
    f = 8 * lay.fh
    fc = f // 2
    nt = s_len // tm

    def body(x_ref, g_ref, do_ref, wg_ref, wu_ref, wd_ref, *rest):
        dx_ref, dg_ref, dgate_ref, dup_ref, act_ref, h_ref, df_ref = rest[len(deps):]
        x = x_ref[...]
        g = g_ref[...]
        xh, rs, hn = _rms(x, g)
        h = hn.astype(BF)
        do = do_ref[...]
        df = (FFN_RES * do).astype(BF)
        dh = jnp.zeros((tm, d), F32)
        for part in range(2):
            cols = slice(part * fc, (part + 1) * fc)
            wg = wg_ref[...].reshape(f, d)[cols]
            wu = wu_ref[...].reshape(f, d)[cols]
            gate = dot_nt(h, wg)
            up = dot_nt(h, wu)
            sg = jax.nn.sigmoid(gate)
            silu = gate * sg
            dact = dot_nt(df, wd_ref[...].reshape(f, d)[cols])
            dup = (dact * silu).astype(BF)
            dgate = (dact * up * (sg * (1.0 + gate * (1.0 - sg)))).astype(BF)
            dh = dh + dot_nn(dgate, wg) + dot_nn(dup, wu)
            dgate_ref[:, cols] = dgate
            dup_ref[:, cols] = dup
            act_ref[:, cols] = (silu * up).astype(BF)
        dxn, dgrow = _rms_bwd(dh, xh, rs, g)
        dx_ref[...] = do + dxn

        @pl.when(pl.program_id(0) == 0)
        def _():
            dg_ref[...] = jnp.zeros_like(dg_ref)

        dg_ref[...] += jnp.sum(dgrow, axis=0, keepdims=True)
        h_ref[...] = h
        df_ref[...] = df

    row = pl.BlockSpec((tm, d), lambda i: (i, 0))
    wide = pl.BlockSpec((tm, f), lambda i: (i, 0))
    vec = pl.BlockSpec((1, d), lambda i: (0, 0))
    return pl.pallas_call(
        body, name="ffn_backward_dx", grid=(nt,),
        out_shape=(SDS((s_len, d), F32), SDS((1, d), F32), SDS((s_len, f), BF), SDS((s_len, f), BF),
                   SDS((s_len, f), BF), SDS((s_len, d), BF), SDS((s_len, d), BF)),
        in_specs=[row, vec, row] + [_w_spec(lay.fh, d, 3 * which + m) for m in range(3)] + [ANY] * len(deps),
        out_specs=(row, vec, wide, wide, wide, row, row), compiler_params=_cparams(),
    )(x, gain, dout, wfull, wfull, wfull, *deps)


def weight_grad_tn(a, b, gb, lay, blk, tk=512):
    s_len, f = a.shape
    tk = min(tk, s_len)
    d = b.shape[1]
    fc = f // 2
    nk = s_len // tk

    def body(a_ref, b_ref, gb_ref, o_ref, acc):
        kt = pl.program_id(1)

        @pl.when(kt == 0)
        def _():
            acc[...] = jnp.zeros_like(acc)

        acc[...] += dot_tn(a_ref[...], b_ref[...])

        @pl.when(kt == nk - 1)
        def _():
            for p in range(2):
                for q in range(2):
                    o_ref[p, q] = acc[pl.ds((2 * p + q) * lay.fh, lay.fh), :].astype(o_ref.dtype)

    return pl.pallas_call(
        body, name="weight_grad_tn", grid=(2, nk), out_shape=SDS(gb.shape, gb.dtype),
        in_specs=[pl.BlockSpec((tk, fc), lambda j, kt: (kt, j)), pl.BlockSpec((tk, d), lambda j, kt: (kt, 0)), ANY],
        out_specs=pl.BlockSpec((2, 2, lay.fh, d), lambda j, kt: (j, 0, blk, 0)),
        scratch_shapes=[pltpu.VMEM((fc, d), F32)],
        input_output_aliases={2: 0}, compiler_params=_cparams(),
    )(a, b, gb)


def mix_project(x, gain, wfull, lay, tm=512):
    s_len, d = x.shape
    tm = min(tm, s_len)
    d_in = 8 * lay.ih

    def body(x_ref, g_ref, w_ref, o_ref):
        _, _, hn = _rms(x_ref[...], g_ref[...])
        o_ref[...] = dot_nt(hn.astype(BF), w_ref[:, :, :lay.ih, :].reshape(d_in, d))

    return pl.pallas_call(
        body, name="mix_project", grid=(s_len // tm,), out_shape=SDS((s_len, d_in), F32),
        in_specs=[pl.BlockSpec((tm, d), lambda i: (i, 0)), pl.BlockSpec((1, d), lambda i: (0, 0)),
                  _w_spec(lay.fh, d, lay.MIX_BLK)],
        out_specs=pl.BlockSpec((tm, d_in), lambda i: (i, 0)), compiler_params=_cparams(),
    )(x, gain, wfull)


def mix_project_backward(x, gain, dout, dxr, dgt, dq, dkv, dwout, wfull, gb, lay, tm=512):
    s_len, d = x.shape
    tm = min(tm, s_len)
    d_in = 8 * lay.ih
    nt = s_len // tm
    lw = dxr.shape[1]
    kvw = dkv.shape[1]

    def body(x_ref, g_ref, do_ref, dxr_ref, dgt_ref, dq_ref, dkv_ref, dwo_ref, w_ref, gb_ref, dx_ref, dg_ref, o_ref, acc):
        i = pl.program_id(0)
        g = g_ref[...]
        xh, rs, hn = _rms(x_ref[...], g)
        h = hn.astype(BF)
        dp = jnp.concatenate([dxr_ref[...], dgt_ref[...], dq_ref[...], dkv_ref[...]], axis=1).astype(BF)
        dh = dot_nn(dp, w_ref[:, :, :lay.ih, :].reshape(d_in, d))
        dxn, dgrow = _rms_bwd(dh, xh, rs, g)
        dx_ref[...] = do_ref[...] + dxn

        @pl.when(i == 0)
        def _():
            dg_ref[...] = jnp.zeros_like(dg_ref)
            acc[...] = jnp.zeros_like(acc)

        dg_ref[...] += jnp.sum(dgrow, axis=0, keepdims=True)
        acc[...] += dot_tn(dp, h)

        @pl.when(i == nt - 1)
        def _():
            for p in range(N_CHIPS):
                for q in range(2):
                    o_ref[p, q, :lay.ih, :] = acc[pl.ds((2 * p + q) * lay.ih, lay.ih), :].astype(o_ref.dtype)
            o_ref[:, :, lay.ih:, :] = dwo_ref[...]

    row = pl.BlockSpec((tm, d), lambda i: (i, 0))
    vec = pl.BlockSpec((1, d), lambda i: (0, 0))
    return pl.pallas_call(
        body, name="mix_project_backward", grid=(nt,),
        out_shape=(SDS((s_len, d), F32), SDS((1, d), F32), SDS(gb.shape, gb.dtype)),
        in_specs=[row, vec, row, pl.BlockSpec((tm, lw), lambda i: (i, 0)), pl.BlockSpec((tm, lw), lambda i: (i, 0)),
                  pl.BlockSpec((tm, dq.shape[1]), lambda i: (i, 0)), pl.BlockSpec((tm, kvw), lambda i: (i, 0)),
                  pl.BlockSpec(dwout.shape, lambda i: (0, 0, 0, 0)), _w_spec(lay.fh, d, lay.MIX_BLK), ANY],
        out_specs=(row, vec, pl.BlockSpec((N_CHIPS, 2, lay.fh, d), lambda i: (0, 0, lay.MIX_BLK, 0))),
        scratch_shapes=[pltpu.VMEM((d_in, d), F32)],
        input_output_aliases={9: 2}, compiler_params=_cparams(),
    )(x, gain, dout, dxr, dgt, dq, dkv, dwout, wfull, gb)


def _lru_gates(xc, wb_ref, pv_ref, direction):
    xcb = xc.astype(BF)
    r = jax.nn.sigmoid(dot_nn(xcb, wb_ref[2 * direction]) + pv_ref[1 + direction:2 + direction, :])
    i = jax.nn.sigmoid(dot_nn(xcb, wb_ref[2 * direction + 1]) + pv_ref[3 + direction:4 + direction, :])
    lam = pv_ref[5 + direction:6 + direction, :]
    sp = jnp.maximum(-lam, 0.0) + jnp.log(1.0 + jnp.exp(-jnp.abs(lam)))
    a = jnp.exp(-LRU_C * sp * r)
    mult = jnp.sqrt(1.0 - a * a)
    return xcb, r, i, a, mult, sp


def _conv_rows(xr, cv_ref, bias, n):
    acc = bias + cv_ref[0:1, :] * _shift_rows(xr, -2, n)
    for j in range(1, CONV_WIDTH):
        acc = acc + cv_ref[j:j + 1, :] * _shift_rows(xr, j - 2, n)
    return acc


def lru_forward(proj, cvec, pvec, wblk, lw, deps=(), ch=512):
    s_len = proj.shape[0]
    ncb = lw // LANE
    ch = min(ch, s_len)
    nchunk = s_len // ch

    def body(xr_ref, gt_ref, cv_ref, pv_ref, wb_ref, *rest):
        y_ref, hs_ref, xc_s, a_s, u_s, acum_s = rest[len(deps):]
        xc_s[...] = _conv_rows(xr_ref[...], cv_ref, pv_ref[0:1, :], s_len)
        for direction in range(2):
            def fill(ci, _):
                rows = pl.ds(pl.multiple_of(ci * ch, ch), ch)
                xc = xc_s[rows, :]
                _, _, i, a, mult, _ = _lru_gates(xc, wb_ref, pv_ref, direction)
                a_s[rows, :] = a
                u_s[rows, :] = mult * (i * xc)
                return 0

            lax.fori_loop(0, nchunk, fill, 0)
            _scan_rows(a_s, u_s, hs_ref.at[direction], acum_s, reverse=direction == 1)

        def out(ci, _):
            rows = pl.ds(pl.multiple_of(ci * ch, ch), ch)
            gl, _ = _gelu(gt_ref[rows, :])
            y_ref[rows, :] = gl * (hs_ref[0, rows, :] + hs_ref[1, rows, :])
            return 0

        lax.fori_loop(0, nchunk, out, 0)

    col = lambda off: pl.BlockSpec((s_len, LANE), lambda cb: (0, off + cb))
    return pl.pallas_call(
        body, name="lru_forward", grid=(ncb,),
        out_shape=(SDS((s_len, lw), F32), SDS((2, s_len, lw), F32)),
        in_specs=[col(0), col(ncb), pl.BlockSpec((8, LANE), lambda cb: (0, cb)), pl.BlockSpec((8, LANE), lambda cb: (0, cb)),
                  pl.BlockSpec((4, None, LANE, LANE), lambda cb: (0, cb, 0, 0))] + [ANY] * len(deps),
        out_specs=(pl.BlockSpec((s_len, LANE), lambda cb: (0, cb)), pl.BlockSpec((2, s_len, LANE), lambda cb: (0, 0, cb))),
        scratch_shapes=[pltpu.VMEM((s_len, LANE), F32)] * 4, compiler_params=_cparams(),
    )(proj, proj, cvec, pvec, wblk, *deps)


def lru_backward(proj, hs, dy, cvec, pvec, wblk, lw, ch=512):
    s_len = proj.shape[0]
    ncb = lw // LANE
    ch = min(ch, s_len)
    nchunk = s_len // ch

    def body(xr_ref, gt_ref, hs_ref, dy_ref, cv_ref, pv_ref, wb_ref, dxr_ref, dgt_ref, dcv_ref, dpv_ref, dwb_ref,
             xc_s, a_s, dh_s, lam_s, hp_s, dxc_s, acum_s):
        xr = xr_ref[...]
        xc_s[...] = _conv_rows(xr, cv_ref, pv_ref[0:1, :], s_len)
        dxc_s[...] = jnp.zeros_like(dxc_s)
        dpv_ref[...] = jnp.zeros_like(dpv_ref)
        dwb_ref[...] = jnp.zeros_like(dwb_ref)

        def head(ci, _):
            rows = pl.ds(pl.multiple_of(ci * ch, ch), ch)
            gt = gt_ref[rows, :]
            gl, t = _gelu(gt)
            dy = dy_ref[rows, :]
            dh_s[rows, :] = dy * gl
            dgt_ref[rows, :] = dy * (hs_ref[0, rows, :] + hs_ref[1, rows, :]) * _gelu_grad(gt, t)
            return 0

        lax.fori_loop(0, nchunk, head, 0)

        for direction in range(2):
            def fill(ci, _):
                rows = pl.ds(pl.multiple_of(ci * ch, ch), ch)
                _, _, _, a, _, _ = _lru_gates(xc_s[rows, :], wb_ref, pv_ref, direction)
                a_s[rows, :] = a
                return 0

            lax.fori_loop(0, nchunk, fill, 0)
            toward = 1 if direction == 0 else -1
            hp_s[...] = _shift_rows(a_s[...], toward, s_len)
            _scan_rows(hp_s, dh_s, lam_s, acum_s, reverse=direction == 0)
            hp_s[...] = _shift_rows(hs_ref[direction], -toward, s_len)

            def grads(ci, _):
                rows = pl.ds(pl.multiple_of(ci * ch, ch), ch)
                xc = xc_s[rows, :]
                xcb, r, i, a, mult, sp = _lru_gates(xc, wb_ref, pv_ref, direction)
                du = lam_s[rows, :]
                da = du * hp_s[rows, :]
                dmult = du * i * xc
                di = du * mult * xc
                dlog_a = (da - dmult * a / mult) * a
                dr = dlog_a * (-LRU_C * sp)
                dza = dr * r * (1.0 - r)
                dzx = di * i * (1.0 - i)
                dzab = dza.astype(BF)
                dzxb = dzx.astype(BF)
                dxc_s[rows, :] += (du * mult * i + dot_nt(dzab, wb_ref[2 * direction])
                                   + dot_nt(dzxb, wb_ref[2 * direction + 1]))
                dwb_ref[2 * direction] += dot_tn(xcb, dzab)
                dwb_ref[2 * direction + 1] += dot_tn(xcb, dzxb)
                dpv_ref[1 + direction:2 + direction, :] += jnp.sum(dza, axis=0, keepdims=True)
                dpv_ref[3 + direction:4 + direction, :] += jnp.sum(dzx, axis=0, keepdims=True)
                dpv_ref[5 + direction:6 + direction, :] += jnp.sum(dlog_a * (-LRU_C * r), axis=0, keepdims=True)
                return 0

            lax.fori_loop(0, nchunk, grads, 0)

        for direction in range(2):
            lam = pv_ref[5 + direction:6 + direction, :]
            dpv_ref[5 + direction:6 + direction, :] = dpv_ref[5 + direction:6 + direction, :] * (-jax.nn.sigmoid(-lam))
        dxc = dxc_s[...]
        dpv_ref[0:1, :] = jnp.sum(dxc, axis=0, keepdims=True)
        dxr = cv_ref[0:1, :] * _shift_rows(dxc, 2, s_len)
        for j in range(1, CONV_WIDTH):
            dxr = dxr + cv_ref[j:j + 1, :] * _shift_rows(dxc, 2 - j, s_len)
        dxr_ref[...] = dxr
        dcv_ref[...] = jnp.zeros_like(dcv_ref)
        for j in range(CONV_WIDTH):
            dcv_ref[j:j + 1, :] = jnp.sum(dxc * _shift_rows(xr, j - 2, s_len), axis=0, keepdims=True)

    col = lambda off: pl.BlockSpec((s_len, LANE), lambda cb: (0, off + cb))
    own = pl.BlockSpec((s_len, LANE), lambda cb: (0, cb))
    small = pl.BlockSpec((8, LANE), lambda cb: (0, cb))
    wspec = pl.BlockSpec((4, None, LANE, LANE), lambda cb: (0, cb, 0, 0))
    return pl.pallas_call(
        body, name="lru_backward", grid=(ncb,),
        out_shape=(SDS((s_len, lw), F32), SDS((s_len, lw), F32), SDS((8, lw), F32), SDS((8, lw), F32),
                   SDS(wblk.shape, F32)),
        in_specs=[col(0), col(ncb), pl.BlockSpec((2, s_len, LANE), lambda cb: (0, 0, cb)), own, small, small, wspec],
        out_specs=(own, own, small, small, wspec),
        scratch_shapes=[pltpu.VMEM((s_len, LANE), F32)] * 7, compiler_params=_cparams(),
    )(proj, proj, hs, dy, cvec, pvec, wblk)


def _attn_specs(s_len, lw, att):
    nb = s_len // BLOCK
    qcol = 2 * lw // att
    kcol = (2 * lw + att) // BLOCK
    prev = lambda n: jnp.maximum(n - 1, 0)
    nxt = lambda n: jnp.minimum(n + 1, nb - 1)
    q = pl.BlockSpec((BLOCK, att), lambda n: (n, qcol))
    ks = [pl.BlockSpec((BLOCK, BLOCK), lambda n, f=f: (f(n), kcol)) for f in (prev, lambda n: n, nxt)]
    vs = [pl.BlockSpec((BLOCK, BLOCK), lambda n, f=f: (f(n), kcol + 1)) for f in (prev, lambda n: n, nxt)]
    return q, ks, vs


def _stack_heads(v, kh):
    return jnp.concatenate([v[:, (kh * KV_GROUP + g) * HEAD_DIM:(kh * KV_GROUP + g + 1) * HEAD_DIM]
                            for g in range(KV_GROUP)], axis=0)


def _key_exists(n, nb):
    j = lax.broadcasted_iota(jnp.int32, (1, 3 * BLOCK), 1)
    return ((n > 0) | (j >= BLOCK)) & ((n < nb - 1) | (j < 2 * BLOCK))


def _attn_probs(qs, kcat, bias_g, sink_g, key_ok):
    logits = jnp.where(key_ok, dot_nt(qs, kcat) + bias_g, NEG_INF)
    m = jnp.maximum(jnp.max(logits, axis=-1, keepdims=True), sink_g)
    p = jnp.exp(logits - m)
    es = jnp.exp(sink_g - m)
    inv = 1.0 / (jnp.sum(p, axis=-1, keepdims=True) + es)
    return p * inv, es * inv


def attention_forward(proj, bias, sink, lw, att):
    s_len = proj.shape[0]
    nb = s_len // BLOCK
    q_spec, k_specs, v_specs = _attn_specs(s_len, lw, att)

    def body(q_ref, kp_ref, kc_ref, kn_ref, vp_ref, vc_ref, vn_ref, b_ref, s_ref, o_ref):
        n = pl.program_id(0)
        q = q_ref[...]
        key_ok = _key_exists(n, nb)
        kall = jnp.concatenate([kp_ref[...], kc_ref[...], kn_ref[...]], axis=0).astype(BF)
        vall = jnp.concatenate([vp_ref[...], vc_ref[...], vn_ref[...]], axis=0).astype(BF)
        outs = []
        for kh in range(N_KV_HEADS):
            grp = slice(kh * KV_GROUP * BLOCK, (kh + 1) * KV_GROUP * BLOCK)
            qs = (_stack_heads(q, kh) * (HEAD_DIM ** -0.5)).astype(BF)
            bias_g = b_ref[kh * KV_GROUP:(kh + 1) * KV_GROUP].reshape(KV_GROUP * BLOCK, 3 * BLOCK)
            p, _ = _attn_probs(qs, kall[:, kh * HEAD_DIM:(kh + 1) * HEAD_DIM], bias_g, s_ref[grp, 0:1], key_ok)
            o = dot_nn(p.astype(BF), vall[:, kh * HEAD_DIM:(kh + 1) * HEAD_DIM])
            outs += [o[g * BLOCK:(g + 1) * BLOCK] for g in range(KV_GROUP)]
        o_ref[...] = jnp.concatenate(outs, axis=1)

    return pl.pallas_call(
        body, name="attention_forward", grid=(nb,), out_shape=SDS((s_len, att), F32),
        in_specs=[q_spec] + k_specs + v_specs
        + [pl.BlockSpec(bias.shape, lambda n: (0, 0, 0)), pl.BlockSpec(sink.shape, lambda n: (0, 0))],
        out_specs=pl.BlockSpec((BLOCK, att), lambda n: (n, 0)), compiler_params=_cparams(),
    )(proj, proj, proj, proj, proj, proj, proj, bias, sink)


def attention_backward(proj, y_att, dy, bias, sink, lw, att):
    s_len = proj.shape[0]
    nb = s_len // BLOCK
    kvw = N_KV_HEADS * HEAD_DIM
    q_spec, k_specs, v_specs = _attn_specs(s_len, lw, att)

    def body(q_ref, kp_ref, kc_ref, kn_ref, vp_ref, vc_ref, vn_ref, o_ref, do_ref, b_ref, s_ref,
             dq_ref, dkv_ref, db_ref, ds_ref):
        n = pl.program_id(0)

        @pl.when(n == 0)
        def _():
            dkv_ref[...] = jnp.zeros_like(dkv_ref)
            db_ref[...] = jnp.zeros_like(db_ref)
            ds_ref[...] = jnp.zeros_like(ds_ref)

        q = q_ref[...]
        o = o_ref[...]
        do = do_ref[...]
        kall = jnp.concatenate([kp_ref[...], kc_ref[...], kn_ref[...]], axis=0).astype(BF)
        vall = jnp.concatenate([vp_ref[...], vc_ref[...], vn_ref[...]], axis=0).astype(BF)
        key_ok = _key_exists(n, nb)
        dqs, dks, dvs = [], [], []
        for kh in range(N_KV_HEADS):
            heads = slice(kh * KV_GROUP, (kh + 1) * KV_GROUP)
            grp = slice(kh * KV_GROUP * BLOCK, (kh + 1) * KV_GROUP * BLOCK)
            kcat = kall[:, kh * HEAD_DIM:(kh + 1) * HEAD_DIM]
            vcat = vall[:, kh * HEAD_DIM:(kh + 1) * HEAD_DIM]
            qs = (_stack_heads(q, kh) * (HEAD_DIM ** -0.5)).astype(BF)
            bias_g = b_ref[heads].reshape(KV_GROUP * BLOCK, 3 * BLOCK)
            p, ps = _attn_probs(qs, kcat, bias_g, s_ref[grp, 0:1], key_ok)
            dos = _stack_heads(do, kh)
            dosb = dos.astype(BF)
            delta = jnp.sum(dos * _stack_heads(o, kh), axis=-1, keepdims=True)
            dlog = p * (dot_nt(dosb, vcat) - delta)
            dlogb = dlog.astype(BF)
            db_ref[heads] += dlog.reshape(KV_GROUP, BLOCK, 3 * BLOCK)
            dsink = -ps * delta
            for g in range(KV_GROUP):
                h = kh * KV_GROUP + g
                part = jnp.sum(dsink[g * BLOCK:(g + 1) * BLOCK], axis=0, keepdims=True)
                ds_ref[h:h + 1, :] += jnp.broadcast_to(part, (1, LANE))
            dqg = dot_nn(dlogb, kcat) * (HEAD_DIM ** -0.5)
            dqs += [dqg[g * BLOCK:(g + 1) * BLOCK] for g in range(KV_GROUP)]
            dks.append(dot_tn(dlogb, qs))
            dvs.append(dot_tn(p.astype(BF), dosb))
        dq_ref[...] = jnp.concatenate(dqs, axis=1)
        dkv = jnp.concatenate(dks + dvs, axis=1)
        starts = [jnp.maximum(n - 1, 0), n, jnp.minimum(n + 1, nb - 1)]
        for b, st in enumerate(starts):
            rows = pl.ds(pl.multiple_of(st * BLOCK, BLOCK), BLOCK)
            dkv_ref[rows, :] += dkv[b * BLOCK:(b + 1) * BLOCK, :]

    blk = pl.BlockSpec((BLOCK, att), lambda n: (n, 0))
    return pl.pallas_call(
        body, name="attention_backward", grid=(nb,),
        out_shape=(SDS((s_len, att), F32), SDS((s_len, 2 * kvw), F32), SDS(bias.shape, F32), SDS((N_HEADS, LANE), F32)),
        in_specs=[q_spec] + k_specs + v_specs
        + [blk, blk, pl.BlockSpec(bias.shape, lambda n: (0, 0, 0)), pl.BlockSpec(sink.shape, lambda n: (0, 0))],
        out_specs=(blk, pl.BlockSpec((s_len, 2 * kvw), lambda n: (0, 0)),
                   pl.BlockSpec(bias.shape, lambda n: (0, 0, 0)), pl.BlockSpec((N_HEADS, LANE), lambda n: (0, 0))),
        compiler_params=_cparams(),
    )(proj, proj, proj, proj, proj, proj, proj, y_att, dy, bias, sink)


def mix_output(x, y_rec, y_att, g_rec, g_att, wfull, lay, tm=512):
    s_len, d = x.shape
    tm = min(tm, s_len)
    lw = y_rec.shape[1]
    att = y_att.shape[1]

    def body(x_ref, yr_ref, ya_ref, gr_ref, ga_ref, w_ref, o_ref):
        _, _, nr = _rms(yr_ref[...], gr_ref[...])
        _, _, na = _rms(ya_ref[...], ga_ref[...])
        y = jnp.concatenate([nr, na], axis=1).astype(BF)
        o_ref[...] = x_ref[...] + dot_nn(y, w_ref[:, :, lay.ih:, :].reshape(d, d))

    row = pl.BlockSpec((tm, d), lambda i: (i, 0))
    return pl.pallas_call(
        body, name="mix_output", grid=(s_len // tm,), out_shape=SDS((s_len, d), F32),
        in_specs=[row, pl.BlockSpec((tm, lw), lambda i: (i, 0)), pl.BlockSpec((tm, att), lambda i: (i, 0)),
                  pl.BlockSpec((1, lw), lambda i: (0, 0)), pl.BlockSpec((1, att), lambda i: (0, 0)),
                  _w_spec(lay.fh, d, lay.MIX_BLK)],
        out_specs=row, compiler_params=_cparams(),
    )(x, y_rec, y_att, g_rec, g_att, wfull)


def mix_output_backward(dout, y_rec, y_att, g_rec, g_att, wfull, lay, tm=512):
    s_len, d = dout.shape
    tm = min(tm, s_len)
    lw = y_rec.shape[1]
    att = y_att.shape[1]
    nt = s_len // tm

    def body(do_ref, yr_ref, ya_ref, gr_ref, ga_ref, w_ref, dyr_ref, dya_ref, dgr_ref, dga_ref, o_ref, acc):
        i = pl.program_id(0)
        gr = gr_ref[...]
        ga = ga_ref[...]
        xhr, rsr, nr = _rms(yr_ref[...], gr)
        xha, rsa, na = _rms(ya_ref[...], ga)
        y = jnp.concatenate([nr, na], axis=1).astype(BF)
        dob = do_ref[...].astype(BF)
        dy = dot_nt(dob, w_ref[:, :, lay.ih:, :].reshape(d, d))
        dyr, dgr_row = _rms_bwd(dy[:, :lw], xhr, rsr, gr)
        dya, dga_row = _rms_bwd(dy[:, lw:], xha, rsa, ga)
        dyr_ref[...] = dyr
        dya_ref[...] = dya

        @pl.when(i == 0)
        def _():
            dgr_ref[...] = jnp.zeros_like(dgr_ref)
            dga_ref[...] = jnp.zeros_like(dga_ref)
            acc[...] = jnp.zeros_like(acc)

        dgr_ref[...] += jnp.sum(dgr_row, axis=0, keepdims=True)
        dga_ref[...] += jnp.sum(dga_row, axis=0, keepdims=True)
        acc[...] += dot_tn(y, dob)

        @pl.when(i == nt - 1)
        def _():
            for p in range(N_CHIPS):
                for q in range(2):
                    o_ref[p, q] = acc[pl.ds((2 * p + q) * lay.oh, lay.oh), :].astype(o_ref.dtype)

    row = pl.BlockSpec((tm, d), lambda i: (i, 0))
    return pl.pallas_call(
        body, name="mix_output_backward", grid=(nt,),
        out_shape=(SDS((s_len, lw), F32), SDS((s_len, att), F32), SDS((1, lw), F32), SDS((1, att), F32),
                   SDS((N_CHIPS, 2, lay.oh, d), BF)),
        in_specs=[row, pl.BlockSpec((tm, lw), lambda i: (i, 0)), pl.BlockSpec((tm, att), lambda i: (i, 0)),
                  pl.BlockSpec((1, lw), lambda i: (0, 0)), pl.BlockSpec((1, att), lambda i: (0, 0)),
                  _w_spec(lay.fh, d, lay.MIX_BLK)],
        out_specs=(pl.BlockSpec((tm, lw), lambda i: (i, 0)), pl.BlockSpec((tm, att), lambda i: (i, 0)),
                   pl.BlockSpec((1, lw), lambda i: (0, 0)), pl.BlockSpec((1, att), lambda i: (0, 0)),
                   pl.BlockSpec((N_CHIPS, 2, lay.oh, d), lambda i: (0, 0, 0, 0))),
        scratch_shapes=[pltpu.VMEM((d, d), F32)], compiler_params=_cparams(),
    )(dout, y_rec, y_att, g_rec, g_att, wfull)


def loss_head(x, gain, target, tm=512):
    s_len, d = x.shape
    tm = min(tm, s_len)

    def body(x_ref, g_ref, t_ref, dx_ref, dg_ref, loss_ref):
        g = g_ref[...]
        xh, rs, y = _rms(x_ref[...], g)
        err = y - t_ref[...]

        @pl.when(pl.program_id(0) == 0)
        def _():
            dg_ref[...] = jnp.zeros_like(dg_ref)
            loss_ref[...] = jnp.zeros_like(loss_ref)

        part = 0.5 * jnp.sum(jnp.mean(err * err, axis=-1, keepdims=True), axis=0, keepdims=True)
        loss_ref[...] += jnp.broadcast_to(part, loss_ref.shape)
        dx, dgrow = _rms_bwd(err * (1.0 / d), xh, rs, g)
        dx_ref[...] = dx
        dg_ref[...] += jnp.sum(dgrow, axis=0, keepdims=True)

    row = pl.BlockSpec((tm, d), lambda i: (i, 0))
    vec = pl.BlockSpec((1, d), lambda i: (0, 0))
    return pl.pallas_call(
        body, name="loss_head", grid=(s_len // tm,),
        out_shape=(SDS((s_len, d), F32), SDS((1, d), F32), SDS((8, LANE), F32)),
        in_specs=[row, vec, row], out_specs=(row, vec, pl.BlockSpec((8, LANE), lambda i: (0, 0))),
        compiler_params=_cparams(),
    )(x, gain, target)


def adamw(w, g, m, v, tr=512):
    rows, cols = w.shape
    tr = _row_chunk(rows, tr, 8)

    def body(w_ref, g_ref, m_ref, v_ref, d_ref, nm_ref, nv_ref):
        g = g_ref[...]
        m = ADAM_B1 * m_ref[...] + (1.0 - ADAM_B1) * g
        v = ADAM_B2 * v_ref[...] + (1.0 - ADAM_B2) * (g * g)
        m_hat = m / (1.0 - ADAM_B1 ** ADAM_STEP)
        v_hat = v / (1.0 - ADAM_B2 ** ADAM_STEP)
        d_ref[...] = -ADAM_LR * (m_hat / (jnp.sqrt(v_hat) + ADAM_EPS) + ADAM_WD * w_ref[...])
        nm_ref[...] = m
        nv_ref[...] = v

    blk = pl.BlockSpec((tr, cols), lambda i: (i, 0))
    return pl.pallas_call(
        body, name="adamw", grid=(rows // tr,), out_shape=(SDS(w.shape, F32),) * 3,
        in_specs=[blk] * 4, out_specs=(blk,) * 3, compiler_params=_cparams(),
    )(w, g, m, v)


def _pack_rows(arrays, width):
    flat = jnp.concatenate([a.reshape(-1).astype(F32) for a in arrays])
    rows = -(-flat.shape[0] // (8 * width)) * 8
    return jnp.pad(flat, (0, rows * width - flat.shape[0])).reshape(rows, width)


def _unpack_rows(buf, shapes):
    flat = buf.reshape(-1)
    out, off = [], 0
    for shp in shapes:
        n = int(np.prod(shp))
        out.append(flat[off:off + n].reshape(shp))
        off += n
    return out


def _t5_buckets(rel):
    half = N_BUCKETS // 2
    max_exact = half // 2
    ret = (rel > 0).astype(jnp.int32) * half
    n = jnp.abs(rel)
    n_f = jnp.maximum(n, 1).astype(F32)
    large = max_exact + (jnp.log(n_f / max_exact) / math.log(MAX_DISTANCE / max_exact) * (half - max_exact)).astype(jnp.int32)
    large = jnp.minimum(large, half - 1)
    return ret + jnp.where(n < max_exact, n, large)


def _band_buckets():
    t = jnp.arange(BLOCK)[:, None]
    j = jnp.arange(3 * BLOCK)[None, :]
    rel = j - BLOCK - t
    return _t5_buckets(rel), jnp.abs(rel) <= WINDOW


def _block_diag_pairs(w):
    depth, two, nblk, bw, _ = w.shape
    pairs = w.reshape(depth, two, nblk // 2, 2, bw, bw)
    z = jnp.zeros_like(pairs[:, :, :, 0])
    top = jnp.concatenate([pairs[:, :, :, 0], z], axis=-1)
    bot = jnp.concatenate([z, pairs[:, :, :, 1]], axis=-1)
    return jnp.concatenate([top, bot], axis=-2)


def _diag_blocks(dw):
    bw = dw.shape[-1] // 2
    a = dw[:, :, :bw, :bw]
    b = dw[:, :, bw:, bw:]
    return jnp.stack([a, b], axis=2).reshape(dw.shape[0], 2 * dw.shape[1], bw, bw)


def _pack_layer_shard(lay, w_in_l, w_out_l, mats_l):
    halves = []
    for hf in range(2):
        parts = []
        for m, a in enumerate(mats_l):
            a = a if m % 3 == 2 else a.T
            parts.append(a[hf * lay.fh:(hf + 1) * lay.fh])
        parts.append(w_in_l.T[hf * lay.ih:(hf + 1) * lay.ih])
        parts.append(w_out_l[hf * lay.oh:(hf + 1) * lay.oh])
        halves.append(jnp.concatenate(parts, axis=0))
    return jnp.stack(halves).astype(BF)


def kernel(x, ffn1_norm, ffn1_w_gate, ffn1_w_up, ffn1_w_down, mix_norm, w_in, conv_w, conv_b, lru_w_a, lru_b_a, lru_w_x, lru_b_x, lru_lambda, attn_sink, rel_bias, lru_out_norm, attn_out_norm, w_out, ffn2_norm, ffn2_w_gate, ffn2_w_up, ffn2_w_down, final_norm, loss_target, m_ffn1_norm, m_ffn1_w_gate, m_ffn1_w_up, m_ffn1_w_down, m_mix_norm, m_w_in, m_conv_w, m_conv_b, m_lru_w_a, m_lru_b_a, m_lru_w_x, m_lru_b_x, m_lru_lambda, m_attn_sink, m_rel_bias, m_lru_out_norm, m_attn_out_norm, m_w_out, m_ffn2_norm, m_ffn2_w_gate, m_ffn2_w_up, m_ffn2_w_down, m_final_norm, v_ffn1_norm, v_ffn1_w_gate, v_ffn1_w_up, v_ffn1_w_down, v_mix_norm, v_w_in, v_conv_w, v_conv_b, v_lru_w_a, v_lru_b_a, v_lru_w_x, v_lru_b_x, v_lru_lambda, v_attn_sink, v_rel_bias, v_lru_out_norm, v_attn_out_norm, v_w_out, v_ffn2_norm, v_ffn2_w_gate, v_ffn2_w_up, v_ffn2_w_down, v_final_norm):
    depth, d = ffn1_norm.shape
    d_ff = N_CHIPS * ffn1_w_gate.shape[2]
    d_in = N_CHIPS * w_in.shape[2]
    lw = conv_b.shape[1]
    att = N_HEADS * HEAD_DIM
    lay = Layout(d, d_ff, d_in)
    k_chip = 2 * lax.axis_index("x") + lax.axis_index("y")
    pos = jnp.stack([k_chip, lax.axis_index("c")]).astype(jnp.int32)

    mats = (ffn1_w_gate, ffn1_w_up, ffn1_w_down, ffn2_w_gate, ffn2_w_up, ffn2_w_down)
    wsh = [_pack_layer_shard(lay, w_in[l], w_out[l], [m[l] for m in mats]) for l in range(depth)]

    def gather_start(l):
        land = lax.dynamic_update_slice(lax.empty((N_CHIPS, 2, lay.rows, d), BF), wsh[l][None], (k_chip, 0, 0, 0))
        return split_start(f"gather_start_{l}", [land], 3, gather_plan)

    def gather_finish(l, started, after):
        ssem, rsem, bufs, _ = started
        land, = split_wait(f"gather_wait_{l}", ssem, rsem, bufs, after, gather_plan)
        return gather_pair(land)

    sharded_small = (conv_w, lru_b_a, lru_b_x, lru_lambda)
    sshard = jnp.concatenate([a.reshape(-1, LANE) for a in sharded_small], axis=0)
    sfull = gather_small(sshard)
    small_full, off = [], 0
    for a in sharded_small:
        r = a.shape[0] * a.shape[1]
        piece = sfull[:, off:off + r].reshape((N_CHIPS,) + a.shape)
        small_full.append(jnp.moveaxis(piece, 0, 2).reshape(a.shape[0], a.shape[1], N_CHIPS * LANE))
        off += r
    conv_w_f, b_a_f, b_x_f, lam_f = small_full

    zrow = jnp.zeros((1, lw), F32)
    wblk_a = _block_diag_pairs(lru_w_a)
    wblk_x = _block_diag_pairs(lru_w_x)
    buckets, in_band = _band_buckets()
    onehot = (buckets.reshape(-1)[:, None] == jnp.arange(N_BUCKETS)[None, :]).astype(F32)
    bias = jnp.dot(rel_bias.T, onehot.T, precision=lax.Precision.HIGHEST).reshape(N_HEADS, BLOCK, 3 * BLOCK)
    bias = jnp.where(in_band[None], bias, NEG_INF)

    def layer_small(l):
        cvec = jnp.concatenate([conv_w_f[l], jnp.zeros((8 - CONV_WIDTH, lw), F32)], axis=0)
        pvec = jnp.concatenate([conv_b[l][None], b_a_f[l], b_x_f[l], lam_f[l], zrow], axis=0)
        wblk = jnp.stack([wblk_a[l, 0], wblk_x[l, 0], wblk_a[l, 1], wblk_x[l, 1]]).astype(BF)
        sink = jnp.broadcast_to(attn_sink[l][:, None, None], (N_HEADS, BLOCK, LANE)).reshape(N_HEADS * BLOCK, LANE)
        return cvec, pvec, wblk, sink

    xs = x[0]
    wfull = [None] * depth
    wfull[0] = gather_finish(0, gather_start(0), xs)
    started = gather_start(1) if depth > 1 else None
    saved = []
    for l in range(depth):
        cvec, pvec, wblk, sink = layer_small(l)
        deps = (started[3],) if started is not None else ()
        x1 = ffn_forward(xs, ffn1_norm[l][None], wfull[l], lay, 0, deps=deps)
        proj = mix_project(x1, mix_norm[l][None], wfull[l], lay)
        y_rec, hs = lru_forward(proj, cvec, pvec, wblk, lw)
        y_att = attention_forward(proj, bias, sink, lw, att)
        x2 = mix_output(x1, y_rec, y_att, lru_out_norm[l][None], attn_out_norm[l][None], wfull[l], lay)
        x3 = ffn_forward(x2, ffn2_norm[l][None], wfull[l], lay, 1)
        saved.append((xs, x1, x2, proj, y_rec, hs, y_att))
        xs = x3
        if l + 1 < depth:
            wfull[l + 1] = gather_finish(l + 1, started, x3)
            started = gather_start(l + 2) if l + 2 < depth else None

    dx, d_final, loss_tile = loss_head(xs, final_norm[None], loss_target[0])
    loss = lax.psum(loss_tile[0, 0], ("x", "y", "c"))

    layer_names = ["ffn1_norm", "mix_norm", "conv_w", "conv_b", "lru_w_a", "lru_b_a", "lru_w_x", "lru_b_x", "lru_lambda",
                   "attn_sink", "lru_out_norm", "attn_out_norm", "ffn2_norm"]
    dbias_total = jnp.zeros(bias.shape, F32)

    def ffn_back(xin, gain, dout, gb, l, which, deps=()):
        dxo, dg, dgate, dup, act, h, df = ffn_backward_dx(xin, gain, dout, wfull[l], lay, which, deps=deps)
        gb = weight_grad_tn(dgate, h, gb, lay, 3 * which + 0)
        gb = weight_grad_tn(dup, h, gb, lay, 3 * which + 1)
        gb = weight_grad_tn(act, df, gb, lay, 3 * which + 2)
        return dxo, dg[0], gb

    def reduce_start(l, gb, sb):
        p1, sp1 = exchange_pair(gb, sb)
        cs = pair_sum(pos, gb, p1)
        ss = small_pair_sum(sb, sp1)
        lands = [lax.empty((3,) + cs.shape[1:], cs.dtype), lax.empty((N_CHIPS,) + ss.shape, ss.dtype)]
        return split_start(f"reduce_start_{l}", [cs, ss] + lands, 6, reduce_plan)

    def reduce_finish(l, started, after):
        ssem, rsem, bufs, _ = started
        cs, ss, p3, sp3 = split_wait(f"reduce_wait_{l}", ssem, rsem, bufs, after, reduce_plan)
        return exchange_final(chip_sum(pos, cs, p3)), small_chip_sum(pos, ss, sp3)

    gf = [None] * depth
    small_sums = [None] * depth
    small_shapes = [None] * depth
    in_flight = None
    for l in reversed(range(depth)):
        x0, x1, x2, proj, y_rec, hs, y_att = saved[l]
        cvec, pvec, wblk, sink = layer_small(l)
        gb = lax.empty((N_CHIPS, 2, lay.rows, d), BF)
        part = {}
        deps = (in_flight[1][3],) if in_flight is not None else ()
        dx, part["ffn2_norm"], gb = ffn_back(x2, ffn2_norm[l][None], dx, gb, l, 1, deps=deps)
        dyr, dya, dgr, dga, dwout = mix_output_backward(dx, y_rec, y_att, lru_out_norm[l][None], attn_out_norm[l][None],
                                                        wfull[l], lay)
        part["lru_out_norm"] = dgr[0]
        part["attn_out_norm"] = dga[0]
        dq, dkv, dbias, dsink = attention_backward(proj, y_att, dya, bias, sink, lw, att)
        dbias_total = dbias_total + dbias
        part["attn_sink"] = dsink[:, 0]
        dxr, dgt, dcv, dpv, dwb = lru_backward(proj, hs, dyr, cvec, pvec, wblk, lw)
        part["conv_w"] = dcv[:CONV_WIDTH]
        part["conv_b"] = dpv[0]
        part["lru_b_a"] = dpv[1:3]
        part["lru_b_x"] = dpv[3:5]
        part["lru_lambda"] = dpv[5:7]
        part["lru_w_a"] = _diag_blocks(jnp.stack([dwb[0], dwb[2]]))
        part["lru_w_x"] = _diag_blocks(jnp.stack([dwb[1], dwb[3]]))
        dx, dgm, gb = mix_project_backward(x1, mix_norm[l][None], dx, dxr, dgt, dq, dkv, dwout, wfull[l], gb, lay)
        part["mix_norm"] = dgm[0]
        dx, part["ffn1_norm"], gb = ffn_back(x0, ffn1_norm[l][None], dx, gb, l, 0)
        pieces = [part[n] for n in layer_names]
        if l == 0:
            d_rel_bias = jnp.dot(dbias_total.reshape(N_HEADS, -1), onehot, precision=lax.Precision.HIGHEST).T
            pieces += [d_rel_bias, d_final[0]]
        small_shapes[l] = [p.shape for p in pieces]
        if in_flight is not None:
            gf[in_flight[0]], small_sums[in_flight[0]] = reduce_finish(in_flight[0], in_flight[1], dx)
        in_flight = (l, reduce_start(l, gb, _pack_rows(pieces, 1024)))
    gf[in_flight[0]], small_sums[in_flight[0]] = reduce_finish(in_flight[0], in_flight[1], dx)
    grad_x = dx[None]

    per_layer = [_unpack_rows(small_sums[l], small_shapes[l]) for l in range(depth)]
    grads = {n: jnp.stack([per_layer[l][i] for l in range(depth)]) for i, n in enumerate(layer_names)}
    grads["rel_bias"], grads["final_norm"] = per_layer[0][len(layer_names):]
    for name in ("conv_w", "lru_b_a", "lru_b_x", "lru_lambda"):
        grads[name] = lax.dynamic_slice_in_dim(grads[name], k_chip * LANE, LANE, axis=2)

    def from_halves(l, row0, n, transpose):
        a = jnp.concatenate([gf[l][0, row0:row0 + n], gf[l][1, row0:row0 + n]], axis=0)
        return a.T if transpose else a

    grads["w_in"] = jnp.stack([from_halves(l, 6 * lay.fh, lay.ih, True) for l in range(depth)])
    grads["w_out"] = jnp.stack([from_halves(l, 6 * lay.fh + lay.ih, lay.oh, False) for l in range(depth)])
    for m, name in enumerate(("ffn1_w_gate", "ffn1_w_up", "ffn1_w_down", "ffn2_w_gate", "ffn2_w_up", "ffn2_w_down")):
        grads[name] = jnp.stack([from_halves(l, m * lay.fh, lay.fh, m % 3 != 2) for l in range(depth)])

    weights = dict(ffn1_norm=ffn1_norm, ffn1_w_gate=ffn1_w_gate, ffn1_w_up=ffn1_w_up, ffn1_w_down=ffn1_w_down, mix_norm=mix_norm, w_in=w_in, conv_w=conv_w, conv_b=conv_b, lru_w_a=lru_w_a, lru_b_a=lru_b_a, lru_w_x=lru_w_x, lru_b_x=lru_b_x, lru_lambda=lru_lambda, attn_sink=attn_sink, rel_bias=rel_bias, lru_out_norm=lru_out_norm, attn_out_norm=attn_out_norm, w_out=w_out, ffn2_norm=ffn2_norm, ffn2_w_gate=ffn2_w_gate, ffn2_w_up=ffn2_w_up, ffn2_w_down=ffn2_w_down, final_norm=final_norm)
    m_in = dict(ffn1_norm=m_ffn1_norm, ffn1_w_gate=m_ffn1_w_gate, ffn1_w_up=m_ffn1_w_up, ffn1_w_down=m_ffn1_w_down, mix_norm=m_mix_norm, w_in=m_w_in, conv_w=m_conv_w, conv_b=m_conv_b, lru_w_a=m_lru_w_a, lru_b_a=m_lru_b_a, lru_w_x=m_lru_w_x, lru_b_x=m_lru_b_x, lru_lambda=m_lru_lambda, attn_sink=m_attn_sink, rel_bias=m_rel_bias, lru_out_norm=m_lru_out_norm, attn_out_norm=m_attn_out_norm, w_out=m_w_out, ffn2_norm=m_ffn2_norm, ffn2_w_gate=m_ffn2_w_gate, ffn2_w_up=m_ffn2_w_up, ffn2_w_down=m_ffn2_w_down, final_norm=m_final_norm)
    v_in = dict(ffn1_norm=v_ffn1_norm, ffn1_w_gate=v_ffn1_w_gate, ffn1_w_up=v_ffn1_w_up, ffn1_w_down=v_ffn1_w_down, mix_norm=v_mix_norm, w_in=v_w_in, conv_w=v_conv_w, conv_b=v_conv_b, lru_w_a=v_lru_w_a, lru_b_a=v_lru_b_a, lru_w_x=v_lru_w_x, lru_b_x=v_lru_b_x, lru_lambda=v_lru_lambda, attn_sink=v_attn_sink, rel_bias=v_rel_bias, lru_out_norm=v_lru_out_norm, attn_out_norm=v_attn_out_norm, w_out=v_w_out, ffn2_norm=v_ffn2_norm, ffn2_w_gate=v_ffn2_w_gate, ffn2_w_up=v_ffn2_w_up, ffn2_w_down=v_ffn2_w_down, final_norm=v_final_norm)
    order = list(weights)
    large = ("ffn1_w_gate", "ffn1_w_up", "ffn1_w_down", "w_in", "w_out", "ffn2_w_gate", "ffn2_w_up", "ffn2_w_down")
    delta, new_m, new_v = {}, {}, {}
    for name in large:
        shp = weights[name].shape
        two_d = lambda a: a.reshape(-1, shp[-1])
        dl, nm, nv = adamw(two_d(weights[name]), two_d(grads[name]), two_d(m_in[name]), two_d(v_in[name]))
        delta[name], new_m[name], new_v[name] = dl.reshape(shp), nm.reshape(shp), nv.reshape(shp)
    small = [n for n in order if n not in large]
    packed = [_pack_rows([src[n] for n in small], 1024) for src in (weights, grads, m_in, v_in)]
    outs = adamw(*packed)
    shapes = [weights[n].shape for n in small]
    for dst, buf in zip((delta, new_m, new_v), outs):
        dst.update(zip(small, _unpack_rows(buf, shapes)))

    return (loss, grad_x, *[grads[n] for n in order], *[delta[n] for n in order],
            *[new_m[n] for n in order], *[new_v[n] for n in order])
```

```python
import functools
import math

import jax
import jax.numpy as jnp
import numpy as np
from jax import lax
from jax.experimental import pallas as pl
from jax.experimental.pallas import tpu as pltpu

BF = jnp.bfloat16
F32 = jnp.float32
SDS = jax.ShapeDtypeStruct
MESH = pl.DeviceIdType.MESH
ANY = pl.BlockSpec(memory_space=pl.ANY)

N_CHIPS = 4
N_HEADS = 8
N_KV_HEADS = 2
KV_GROUP = N_HEADS // N_KV_HEADS
HEAD_DIM = 64
BLOCK = 128
WINDOW = 128
N_BUCKETS = 32
MAX_DISTANCE = 128
LRU_C = 8.0
CONV_WIDTH = 4
LANE = 128
SCAN_SEGMENTS = 8
SCAN_CHAINS = 8
EPS = 1e-6
FFN_RES = 0.5
NEG_INF = -1e30
ADAM_LR = 0.001
ADAM_B1 = 0.9
ADAM_B2 = 0.999
ADAM_EPS = 1e-08
ADAM_WD = 0.01
ADAM_STEP = 10
VMEM_LIMIT = 60000 * 1024
GELU_C = math.sqrt(2.0 / math.pi)


def dot_nn(a, b):
    return lax.dot_general(a, b, (((1,), (0,)), ((), ())), preferred_element_type=F32)


def dot_nt(a, b):
    return lax.dot_general(a, b, (((1,), (1,)), ((), ())), preferred_element_type=F32)


def dot_tn(a, b):
    return lax.dot_general(a, b, (((0,), (0,)), ((), ())), preferred_element_type=F32)


def _cparams(**kw):
    return pltpu.CompilerParams(vmem_limit_bytes=VMEM_LIMIT, **kw)


class Layout:
    MIX_BLK = 6

    def __init__(self, d_model, d_ff, d_in):
        self.fh = d_ff // (2 * N_CHIPS)
        self.ih = d_in // (2 * N_CHIPS)
        self.oh = d_model // (2 * N_CHIPS)
        assert self.ih + self.oh == self.fh, "w_in^T and w_out rows must fill one ffn-sized block"
        self.rows = 7 * self.fh


def _row_chunk(rows, target, step=16):
    best = rows
    for c in range(step, min(rows, target) + 1, step):
        if rows % c == 0:
            best = c
    return best


def _mesh_pos():
    return lax.axis_index("x"), lax.axis_index("y"), lax.axis_index("c")


def _rcopy(src, dst, ssem, rsem, dev):
    return pltpu.make_async_remote_copy(src_ref=src, dst_ref=dst, send_sem=ssem, recv_sem=rsem,
                                        device_id=dev, device_id_type=MESH)


HBM = pl.BlockSpec(memory_space=pltpu.HBM)
SEM = pl.BlockSpec(memory_space=pltpu.SEMAPHORE)
DATAFLOW = pltpu.SideEffectType.DATAFLOW_SIDE_EFFECTING


def _chip_peers():
    x, y, c = _mesh_pos()
    peers = [(1 - x, y), (x, 1 - y), (1 - x, 1 - y)]
    return x, y, c, 2 * x + y, [(px, py, 2 * px + py) for px, py in peers]


def split_start(name, bufs, n, plan):
    nb = len(bufs)

    def body(*refs):
        sends, _ = plan(refs[:nb], refs[nb], refs[nb + 1])
        for cp in sends:
            cp.start()
        refs[-1][...] = jnp.zeros_like(refs[-1])

    out = pl.pallas_call(
        body, name=name,
        out_shape=(pltpu.SemaphoreType.DMA((n,)), pltpu.SemaphoreType.DMA((n,)),
                   *[pltpu.HBM(b.shape, b.dtype) for b in bufs], SDS((8, LANE), F32)),
        in_specs=[HBM] * nb, out_specs=(SEM, SEM, *([HBM] * nb), pl.BlockSpec(memory_space=pltpu.VMEM)),
        input_output_aliases={i: 2 + i for i in range(nb)},
        compiler_params=pltpu.CompilerParams(has_side_effects=DATAFLOW),
    )(*[pltpu.with_memory_space_constraint(b, pltpu.HBM) for b in bufs])
    return out[0], out[1], list(out[2:2 + nb]), out[-1]


def split_wait(name, ssem, rsem, bufs, after, plan):
    nb = len(bufs)

    def body(*refs):
        sends, recvs = plan(refs[:nb], refs[nb], refs[nb + 1])
        for cp in recvs:
            cp.wait_recv()
        for cp in sends:
            cp.wait_send()

    out = pl.pallas_call(
        body, name=name, out_shape=tuple(pltpu.HBM(b.shape, b.dtype) for b in bufs),
        in_specs=[HBM] * nb + [SEM, SEM, ANY], out_specs=tuple([HBM] * nb),
        input_output_aliases={i: i for i in range(nb)},
        compiler_params=pltpu.CompilerParams(has_side_effects=DATAFLOW),
    )(*bufs, ssem, rsem, after)
    return list(out)


def gather_plan(refs, ssem, rsem):
    land_ref, = refs
    _, _, c, k, peers = _chip_peers()
    sends = [_rcopy(land_ref.at[k, c], land_ref.at[k, c], ssem.at[j], rsem.at[j], (px, py, c))
             for j, (px, py, _) in enumerate(peers)]
    recvs = [_rcopy(land_ref.at[kp, c], land_ref.at[kp, c], ssem.at[j], rsem.at[j], (px, py, c))
             for j, (px, py, kp) in enumerate(peers)]
    return sends, recvs


def reduce_plan(refs, ssem, rsem):
    cs_ref, ss_ref, p3_ref, sp3_ref = refs
    _, _, c, k, peers = _chip_peers()
    sends, recvs = [], []
    for j, (px, py, kp) in enumerate(peers):
        sends.append(_rcopy(cs_ref.at[kp], p3_ref.at[j], ssem.at[j], rsem.at[j], (px, py, c)))
        recvs.append(_rcopy(cs_ref.at[kp], p3_ref.at[j], ssem.at[j], rsem.at[j], (px, py, c)))
        sends.append(_rcopy(ss_ref, sp3_ref.at[k], ssem.at[3 + j], rsem.at[3 + j], (px, py, c)))
        recvs.append(_rcopy(ss_ref, sp3_ref.at[kp], ssem.at[3 + j], rsem.at[3 + j], (px, py, c)))
    return sends, recvs


def gather_small(sshard):
    def body(s_ref, sf_ref, lsem, ssem, rsem):
        _, _, c, k, peers = _chip_peers()
        own = pltpu.make_async_copy(s_ref, sf_ref.at[k], lsem)
        own.start()
        sends = [_rcopy(s_ref, sf_ref.at[k], ssem.at[j], rsem.at[j], (px, py, c)) for j, (px, py, _) in enumerate(peers)]
        recvs = [_rcopy(s_ref, sf_ref.at[kp], ssem.at[j], rsem.at[j], (px, py, c)) for j, (px, py, kp) in enumerate(peers)]
        for cp in sends:
            cp.start()
        for cp in recvs:
            cp.wait_recv()
        for cp in sends:
            cp.wait_send()
        own.wait()

    return pl.pallas_call(
        body, name="gather_small", out_shape=SDS((N_CHIPS,) + sshard.shape, sshard.dtype),
        in_specs=[ANY], out_specs=ANY,
        scratch_shapes=[pltpu.SemaphoreType.DMA, pltpu.SemaphoreType.DMA((3,)), pltpu.SemaphoreType.DMA((3,))],
    )(sshard)


def gather_pair(land):
    def body(land_in, land_ref, ssem, rsem):
        x, y, c, k, peers = _chip_peers()
        sib = (x, y, 1 - c)
        sends = [_rcopy(land_ref.at[kp, c], land_ref.at[kp, c], ssem.at[j], rsem.at[j], sib) for j, (_, _, kp) in enumerate(peers)]
        recvs = [_rcopy(land_ref.at[kp, 1 - c], land_ref.at[kp, 1 - c], ssem.at[j], rsem.at[j], sib)
                 for j, (_, _, kp) in enumerate(peers)]
        for cp in sends:
            cp.start()
        for cp in recvs:
            cp.wait_recv()
        for cp in sends:
            cp.wait_send()

    return pl.pallas_call(
        body, name="gather_pair", out_shape=SDS(land.shape, land.dtype),
        in_specs=[ANY], out_specs=ANY, input_output_aliases={0: 0},
        scratch_shapes=[pltpu.SemaphoreType.DMA((3,)), pltpu.SemaphoreType.DMA((3,))],
    )(land)


def exchange_pair(gb, sb):
    n, _, rh, d = gb.shape

    def body(gb_ref, sb_ref, p_ref, sp_ref, ssem, rsem):
        x, y, c = _mesh_pos()
        sib = (x, y, 1 - c)
        sends = [_rcopy(gb_ref.at[kk, 1 - c], p_ref.at[kk], ssem.at[kk], rsem.at[kk], sib) for kk in range(n)]
        sends.append(_rcopy(sb_ref, sp_ref, ssem.at[n], rsem.at[n], sib))
        for cp in sends:
            cp.start()
        for cp in sends:
            cp.wait_recv()
        for cp in sends:
            cp.wait_send()

    return pl.pallas_call(
        body, name="exchange_pair",
        out_shape=(SDS((n, rh, d), gb.dtype), SDS(sb.shape, sb.dtype)),
        in_specs=[ANY, ANY], out_specs=(ANY, ANY),
        scratch_shapes=[pltpu.SemaphoreType.DMA((n + 1,)), pltpu.SemaphoreType.DMA((n + 1,))],
    )(gb, sb)


def exchange_final(gf):
    _, rh, d = gf.shape
    nch = 4 if rh % 32 == 0 else 1
    cr = rh // nch

    def body(gf_ref, out_ref, ssem, rsem):
        x, y, c = _mesh_pos()
        sib = (x, y, 1 - c)
        sends = [_rcopy(out_ref.at[c, pl.ds(q * cr, cr)], out_ref.at[c, pl.ds(q * cr, cr)], ssem.at[q], rsem.at[q], sib)
                 for q in range(nch)]
        recvs = [_rcopy(out_ref.at[1 - c, pl.ds(q * cr, cr)], out_ref.at[1 - c, pl.ds(q * cr, cr)], ssem.at[q], rsem.at[q], sib)
                 for q in range(nch)]
        for cp in sends:
            cp.start()
        for cp in recvs:
            cp.wait_recv()
        for cp in sends:
            cp.wait_send()

    return pl.pallas_call(
        body, name="exchange_final",
        out_shape=SDS(gf.shape, gf.dtype),
        in_specs=[ANY], out_specs=ANY, input_output_aliases={0: 0},
        scratch_shapes=[pltpu.SemaphoreType.DMA((nch,)), pltpu.SemaphoreType.DMA((nch,))],
    )(gf)


def pair_sum(pos, gb, p1):
    n, _, rh, d = gb.shape
    cr = _row_chunk(rh, 1024)

    def body(pos_ref, a_ref, b_ref, o_ref):
        o_ref[...] = (a_ref[...].astype(F32) + b_ref[...].astype(F32)).astype(o_ref.dtype)

    return pl.pallas_call(
        body, name="pair_sum", out_shape=SDS((n, rh, d), gb.dtype),
        grid_spec=pltpu.PrefetchScalarGridSpec(
            num_scalar_prefetch=1, grid=(n, rh // cr),
            in_specs=[pl.BlockSpec((None, None, cr, d), lambda kk, r, pos: (kk, pos[1], r, 0)),
                      pl.BlockSpec((None, cr, d), lambda kk, r, pos: (kk, r, 0))],
            out_specs=pl.BlockSpec((None, cr, d), lambda kk, r, pos: (kk, r, 0))),
        compiler_params=_cparams(),
    )(pos, gb, p1)


def chip_sum(pos, cs, p3):
    n, rh, d = cs.shape
    cr = _row_chunk(rh, 512)

    def body(pos_ref, a_ref, b_ref, o_ref):
        acc = a_ref[...].astype(F32)
        for j in range(3):
            acc = acc + b_ref[j].astype(F32)
        o_ref[...] = acc

    return pl.pallas_call(
        body, name="chip_sum", out_shape=SDS((2, rh, d), F32),
        grid_spec=pltpu.PrefetchScalarGridSpec(
            num_scalar_prefetch=1, grid=(rh // cr,),
            in_specs=[pl.BlockSpec((None, cr, d), lambda r, pos: (pos[0], r, 0)),
                      pl.BlockSpec((3, cr, d), lambda r, pos: (0, r, 0))],
            out_specs=pl.BlockSpec((None, cr, d), lambda r, pos: (pos[1], r, 0))),
        compiler_params=_cparams(),
    )(pos, cs, p3)


def small_pair_sum(a, b):
    def body(a_ref, b_ref, o_ref):
        o_ref[...] = a_ref[...] + b_ref[...]

    return pl.pallas_call(body, name="small_pair_sum", out_shape=SDS(a.shape, a.dtype),
                          compiler_params=_cparams())(a, b)


def small_chip_sum(pos, own, p):
    ns, w = own.shape

    def body(pos_ref, own_ref, p0, p1, p2, p3, o_ref):
        k = pos_ref[0]
        acc = None
        for chip, ref in enumerate((p0, p1, p2, p3)):
            term = jnp.where(k == chip, own_ref[...], ref[...])
            acc = term if acc is None else acc + term
        o_ref[...] = acc

    def slot(chip):
        return pl.BlockSpec((None, ns, w), lambda i, pos: (jnp.where(pos[0] == chip, (chip + 1) % N_CHIPS, chip), 0, 0))

    return pl.pallas_call(
        body, name="small_chip_sum", out_shape=SDS(own.shape, own.dtype),
        grid_spec=pltpu.PrefetchScalarGridSpec(
            num_scalar_prefetch=1, grid=(1,),
            in_specs=[pl.BlockSpec((ns, w), lambda i, pos: (0, 0))] + [slot(chip) for chip in range(N_CHIPS)],
            out_specs=pl.BlockSpec((ns, w), lambda i, pos: (0, 0))),
        compiler_params=_cparams(),
    )(pos, own, p, p, p, p)


def _rms(x, g):
    rs = lax.rsqrt(jnp.mean(x * x, axis=-1, keepdims=True) + EPS)
    xh = x * rs
    return xh, rs, xh * g


def _rms_bwd(dy, xh, rs, g):
    dxh = dy * g
    dx = rs * (dxh - xh * jnp.mean(dxh * xh, axis=-1, keepdims=True))
    return dx, dy * xh


def _gelu(x):
    t = jnp.tanh(GELU_C * (x + 0.044715 * x * x * x))
    return 0.5 * x * (1.0 + t), t


def _gelu_grad(x, t):
    return 0.5 * (1.0 + t) + 0.5 * x * (1.0 - t * t) * GELU_C * (1.0 + 3.0 * 0.044715 * x * x)


def _shift_rows(v, s, n):
    if s == 0:
        return v
    t = lax.broadcasted_iota(jnp.int32, v.shape, 0)
    rolled = pltpu.roll(v, (-s) % n, 0)
    return jnp.where((t + s >= 0) & (t + s < n), rolled, 0.0)


def _scan_rows(a_ref, u_ref, h_ref, acum_ref, reverse):
    s_len, w = a_ref.shape
    chains = max(1, min(SCAN_CHAINS, s_len // (8 * SCAN_SEGMENTS)))
    nseg = SCAN_SEGMENTS * chains
    seg = s_len // nseg

    def step(j, carry):
        jj = (seg - 1 - j) if reverse else j
        out = []
        for c, (h, acc) in enumerate(carry):
            idx = pl.ds(c * SCAN_SEGMENTS * seg + jj, SCAN_SEGMENTS, stride=seg)
            a = a_ref[idx, :]
            h = a * h + u_ref[idx, :]
            acc = a * acc
            h_ref[idx, :] = h
            acum_ref[idx, :] = acc
            out.append((h, acc))
        return tuple(out)

    init = tuple((jnp.zeros((SCAN_SEGMENTS, w), F32), jnp.ones((SCAN_SEGMENTS, w), F32)) for _ in range(chains))
    ends = lax.fori_loop(0, seg, step, init, unroll=min(8, seg))
    order = range(nseg - 2, -1, -1) if reverse else range(1, nseg)
    inflow = jnp.zeros((1, w), F32)
    for s in order:
        src = s + 1 if reverse else s - 1
        h, acc = ends[src // SCAN_SEGMENTS]
        r = src % SCAN_SEGMENTS
        inflow = h[r:r + 1, :] + acc[r:r + 1, :] * inflow
        rows = pl.ds(s * seg, seg)
        h_ref[rows, :] = h_ref[rows, :] + acum_ref[rows, :] * inflow


def _w_spec(rows_half, d, blk):
    return pl.BlockSpec((N_CHIPS, 2, rows_half, d), lambda *_: (0, 0, blk, 0), pipeline_mode=pl.Buffered(1))


def ffn_forward(x, gain, wfull, lay, which, deps=(), tm=512):
    s_len, d = x.shape
    tm = min(tm, s_len)
    f = 8 * lay.fh
    fc = f // 2

    def body(x_ref, g_ref, wg_ref, wu_ref, wd_ref, *rest):
        o_ref, gate_ref, up_ref = rest[len(deps):]
        x = x_ref[...]
        _, _, hn = _rms(x, g_ref[...])
        h = hn.astype(BF)
        y = jnp.zeros((tm, d), F32)
        for part in range(2):
            cols = slice(part * fc, (part + 1) * fc)
            gate = dot_nt(h, wg_ref[...].reshape(f, d)[cols])
            up = dot_nt(h, wu_ref[...].reshape(f, d)[cols])
            act = (gate * jax.nn.sigmoid(gate) * up).astype(BF)
            y = y + dot_nn(act, wd_ref[...].reshape(f, d)[cols])
            gate_ref[:, cols] = gate.astype(BF)
            up_ref[:, cols] = up.astype(BF)
        o_ref[...] = x + FFN_RES * y

    row = pl.BlockSpec((tm, d), lambda i: (i, 0))
    wide = pl.BlockSpec((tm, f), lambda i: (i, 0))
    return pl.pallas_call(
        body, name="ffn_forward", grid=(s_len // tm,),
        out_shape=(SDS((s_len, d), F32), SDS((s_len, f), BF), SDS((s_len, f), BF)),
        in_specs=[row, pl.BlockSpec((1, d), lambda i: (0, 0))]
        + [_w_spec(lay.fh, d, 3 * which + m) for m in range(3)] + [ANY] * len(deps),
        out_specs=(row, wide, wide), compiler_params=_cparams(),
    )(x, gain, wfull, wfull, wfull, *deps)


def ffn_backward_dx(x, gain, dout, gate_bf, up_bf, wfull, lay, which, deps=(), tm=256):
    s_len, d = x.shape
    tm = min(tm, s_len)
    f = 8 * lay.fh
    fc = f // 2
    nt = s_len // tm

    def body(x_ref, g_ref, do_ref, gate_ref, up_ref, wg_ref, wu_ref, wd_ref, *rest):
        dx_ref, dg_ref, dgate_ref, dup_ref, act_ref, h_ref, df_ref = rest[len(deps):]
        x = x_ref[...]
        g = g_ref[...]
        xh, rs, hn = _rms(x, g)
        h = hn.astype(BF)
        do = do_ref[...]
        df = (FFN_RES * do).astype(BF)
        dh = jnp.zeros((tm, d), F32)
        for part in range(2):
            cols = slice(part * fc, (part + 1) * fc)
            wg = wg_ref[...].reshape(f, d)[cols]
            wu = wu_ref[...].reshape(f, d)[cols]
            gate = gate_ref[:, cols].astype(F32)
            up = up_ref[:, cols].astype(F32)
            sg = jax.nn.sigmoid(gate)
            silu = gate * sg
            dact = dot_nt(df, wd_ref[...].reshape(f, d)[cols])
            dup = (dact * silu).astype(BF)
            dgate = (dact * up * (sg * (1.0 + gate * (1.0 - sg)))).astype(BF)
            dh = dh + dot_nn(dgate, wg) + dot_nn(dup, wu)
            dgate_ref[:, cols] = dgate
            dup_ref[:, cols] = dup
            act_ref[:, cols] = (silu * up).astype(BF)
        dxn, dgrow = _rms_bwd(dh, xh, rs, g)
        dx_ref[...] = do + dxn

        @pl.when(pl.program_id(0) == 0)
        def _():
            dg_ref[...] = jnp.zeros_like(dg_ref)

        dg_ref[...] += jnp.sum(dgrow, axis=0, keepdims=True)
        h_ref[...] = h
        df_ref[...] = df

    row = pl.BlockSpec((tm, d), lambda i: (i, 0))
    wide = pl.BlockSpec((tm, f), lambda i: (i, 0))
    vec = pl.BlockSpec((1, d), lambda i: (0, 0))
    return pl.pallas_call(
        body, name="ffn_backward_dx", grid=(nt,),
        out_shape=(SDS((s_len, d), F32), SDS((1, d), F32), SDS((s_len, f), BF), SDS((s_len, f), BF),
                   SDS((s_len, f), BF), SDS((s_len, d), BF), SDS((s_len, d), BF)),
        in_specs=[row, vec, row, wide, wide] + [_w_spec(lay.fh, d, 3 * which + m) for m in range(3)] + [ANY] * len(deps),
        out_specs=(row, vec, wide, wide, wide, row, row), compiler_params=_cparams(),
    )(x, gain, dout, gate_bf, up_bf, wfull, wfull, wfull, *deps)


def weight_grad_tn(a, b, gb, lay, blk, tk=512):
    s_len, f = a.shape
    tk = min(tk, s_len)
    d = b.shape[1]
    fc = f // 2
    nk = s_len // tk

    def body(a_ref, b_ref, gb_ref, o_ref, acc):
        kt = pl.program_id(1)

        @pl.when(kt == 0)
        def _():
            acc[...] = jnp.zeros_like(acc)

        acc[...] += dot_tn(a_ref[...], b_ref[...])

        @pl.when(kt == nk - 1)
        def _():
            for p in range(2):
                for q in range(2):
                    o_ref[p, q] = acc[pl.ds((2 * p + q) * lay.fh, lay.fh), :].astype(o_ref.dtype)

    return pl.pallas_call(
        body, name="weight_grad_tn", grid=(2, nk), out_shape=SDS(gb.shape, gb.dtype),
        in_specs=[pl.BlockSpec((tk, fc), lambda j, kt: (kt, j)), pl.BlockSpec((tk, d), lambda j, kt: (kt, 0)), ANY],
        out_specs=pl.BlockSpec((2, 2, lay.fh, d), lambda j, kt: (j, 0, blk, 0)),
        scratch_shapes=[pltpu.VMEM((fc, d), F32)],
        input_output_aliases={2: 0}, compiler_params=_cparams(),
    )(a, b, gb)


def _lane_blocks(v):
    return [v[:, j * LANE:(j + 1) * LANE] for j in range(v.shape[1] // LANE)]


def _join_lane_blocks(ref):
    return jnp.concatenate([ref[j] for j in range(ref.shape[0])], axis=1)


def _cbm_spec(nblk, rows, first=0):
    return pl.BlockSpec((nblk, rows, LANE), lambda i: (first // nblk, i, 0))


def mix_project(x, gain, wfull, lay, tm=512):
    s_len, d = x.shape
    tm = min(tm, s_len)
    d_in = 8 * lay.ih
    ncol = d_in // LANE

    def body(x_ref, g_ref, w_ref, o_ref):
        _, _, hn = _rms(x_ref[...], g_ref[...])
        res = dot_nt(hn.astype(BF), w_ref[:, :, :lay.ih, :].reshape(d_in, d))
        for j, piece in enumerate(_lane_blocks(res)):
            o_ref[j] = piece

    return pl.pallas_call(
        body, name="mix_project", grid=(s_len // tm,), out_shape=SDS((ncol, s_len, LANE), F32),
        in_specs=[pl.BlockSpec((tm, d), lambda i: (i, 0)), pl.BlockSpec((1, d), lambda i: (0, 0)),
                  _w_spec(lay.fh, d, lay.MIX_BLK)],
        out_specs=_cbm_spec(ncol, tm), compiler_params=_cparams(),
    )(x, gain, wfull)


def mix_project_backward(x, gain, dout, dxr, dgt, dq, dkv, dwout, wfull, gb, lay, tm=512):
    s_len, d = x.shape
    tm = min(tm, s_len)
    d_in = 8 * lay.ih
    nt = s_len // tm
    kvw = dkv.shape[1]

    def body(x_ref, g_ref, do_ref, dxr_ref, dgt_ref, dq_ref, dkv_ref, dwo_ref, w_ref, gb_ref, dx_ref, dg_ref, o_ref, acc):
        i = pl.program_id(0)
        g = g_ref[...]
        xh, rs, hn = _rms(x_ref[...], g)
        h = hn.astype(BF)
        dp = jnp.concatenate([_join_lane_blocks(dxr_ref), _join_lane_blocks(dgt_ref), _join_lane_blocks(dq_ref),
                              dkv_ref[...]], axis=1).astype(BF)
        dh = dot_nn(dp, w_ref[:, :, :lay.ih, :].reshape(d_in, d))
        dxn, dgrow = _rms_bwd(dh, xh, rs, g)
        dx_ref[...] = do_ref[...] + dxn

        @pl.when(i == 0)
        def _():
            dg_ref[...] = jnp.zeros_like(dg_ref)
            acc[...] = jnp.zeros_like(acc)

        dg_ref[...] += jnp.sum(dgrow, axis=0, keepdims=True)
        acc[...] += dot_tn(dp, h)

        @pl.when(i == nt - 1)
        def _():
            for p in range(N_CHIPS):
                for q in range(2):
                    o_ref[p, q, :lay.ih, :] = acc[pl.ds((2 * p + q) * lay.ih, lay.ih), :].astype(o_ref.dtype)
            o_ref[:, :, lay.ih:, :] = dwo_ref[...]

    row = pl.BlockSpec((tm, d), lambda i: (i, 0))
    vec = pl.BlockSpec((1, d), lambda i: (0, 0))
    return pl.pallas_call(
        body, name="mix_project_backward", grid=(nt,),
        out_shape=(SDS((s_len, d), F32), SDS((1, d), F32), SDS(gb.shape, gb.dtype)),
        in_specs=[row, vec, row, _cbm_spec(dxr.shape[0], tm), _cbm_spec(dgt.shape[0], tm),
                  _cbm_spec(dq.shape[0], tm), pl.BlockSpec((tm, kvw), lambda i: (i, 0)),
                  pl.BlockSpec(dwout.shape, lambda i: (0, 0, 0, 0)), _w_spec(lay.fh, d, lay.MIX_BLK), ANY],
        out_specs=(row, vec, pl.BlockSpec((N_CHIPS, 2, lay.fh, d), lambda i: (0, 0, lay.MIX_BLK, 0))),
        scratch_shapes=[pltpu.VMEM((d_in, d), F32)],
        input_output_aliases={9: 2}, compiler_params=_cparams(),
    )(x, gain, dout, dxr, dgt, dq, dkv, dwout, wfull, gb)


def _lru_gates(xc, wb_ref, pv_ref, direction):
    xcb = xc.astype(BF)
    r = jax.nn.sigmoid(dot_nn(xcb, wb_ref[2 * direction]) + pv_ref[1 + direction:2 + direction, :])
    i = jax.nn.sigmoid(dot_nn(xcb, wb_ref[2 * direction + 1]) + pv_ref[3 + direction:4 + direction, :])
    lam = pv_ref[5 + direction:6 + direction, :]
    sp = jnp.maximum(-lam, 0.0) + jnp.log(1.0 + jnp.exp(-jnp.abs(lam)))
    a = jnp.exp(-LRU_C * sp * r)
    mult = jnp.sqrt(1.0 - a * a)
    return xcb, r, i, a, mult, sp


def _conv_rows(xr, cv_ref, bias, n):
    acc = bias + cv_ref[0:1, :] * _shift_rows(xr, -2, n)
    for j in range(1, CONV_WIDTH):
        acc = acc + cv_ref[j:j + 1, :] * _shift_rows(xr, j - 2, n)
    return acc


def lru_forward(proj, cvec, pvec, wblk, lw, deps=(), ch=512):
    s_len = proj.shape[1]
    ncb = lw // LANE
    ch = min(ch, s_len)
    nchunk = s_len // ch

    def body(xr_ref, gt_ref, cv_ref, pv_ref, wb_ref, *rest):
        y_ref, hs_ref, xc_s, a_s, u_s, acum_s = rest[len(deps):]
        xc_s[...] = _conv_rows(xr_ref[...], cv_ref, pv_ref[0:1, :], s_len)
        for direction in range(2):
            def fill(ci, _):
                rows = pl.ds(pl.multiple_of(ci * ch, ch), ch)
                xc = xc_s[rows, :]
                _, _, i, a, mult, _ = _lru_gates(xc, wb_ref, pv_ref, direction)
                a_s[rows, :] = a
                u_s[rows, :] = mult * (i * xc)
                return 0

            lax.fori_loop(0, nchunk, fill, 0)
            _scan_rows(a_s, u_s, hs_ref.at[direction], acum_s, reverse=direction == 1)

        def out(ci, _):
            rows = pl.ds(pl.multiple_of(ci * ch, ch), ch)
            gl, _ = _gelu(gt_ref[rows, :])
            y_ref[rows, :] = gl * (hs_ref[0, rows, :] + hs_ref[1, rows, :])
            return 0

        lax.fori_loop(0, nchunk, out, 0)

    col = lambda off: pl.BlockSpec((None, s_len, LANE), lambda cb: (off + cb, 0, 0))
    return pl.pallas_call(
        body, name="lru_forward", grid=(ncb,),
        out_shape=(SDS((ncb, s_len, LANE), F32), SDS((2, ncb, s_len, LANE), F32)),
        in_specs=[col(0), col(ncb), pl.BlockSpec((8, LANE), lambda cb: (0, cb)), pl.BlockSpec((8, LANE), lambda cb: (0, cb)),
                  pl.BlockSpec((4, None, LANE, LANE), lambda cb: (0, cb, 0, 0))] + [ANY] * len(deps),
        out_specs=(col(0), pl.BlockSpec((2, None, s_len, LANE), lambda cb: (0, cb, 0, 0))),
        scratch_shapes=[pltpu.VMEM((s_len, LANE), F32)] * 4, compiler_params=_cparams(),
    )(proj, proj, cvec, pvec, wblk, *deps)


def lru_backward(proj, hs, dy, cvec, pvec, wblk, lw, ch=512):
    s_len = proj.shape[1]
    ncb = lw // LANE
    ch = min(ch, s_len)
    nchunk = s_len // ch

    def body(xr_ref, gt_ref, hs_ref, dy_ref, cv_ref, pv_ref, wb_ref, dxr_ref, dgt_ref, dcv_ref, dpv_ref, dwb_ref,
             xc_s, a_s, dh_s, lam_s, hp_s, dxc_s, acum_s):
        xr = xr_ref[...]
        xc_s[...] = _conv_rows(xr, cv_ref, pv_ref[0:1, :], s_len)
        dxc_s[...] = jnp.zeros_like(dxc_s)
        dpv_ref[...] = jnp.zeros_like(dpv_ref)
        dwb_ref[...] = jnp.zeros_like(dwb_ref)

        def head(ci, _):
            rows = pl.ds(pl.multiple_of(ci * ch, ch), ch)
            gt = gt_ref[rows, :]
            gl, t = _gelu(gt)
            dy = dy_ref[rows, :]
            dh_s[rows, :] = dy * gl
            dgt_ref[rows, :] = dy * (hs_ref[0, rows, :] + hs_ref[1, rows, :]) * _gelu_grad(gt, t)
            return 0

        lax.fori_loop(0, nchunk, head, 0)

        for direction in range(2):
            def fill(ci, _):
                rows = pl.ds(pl.multiple_of(ci * ch, ch), ch)
                _, _, _, a, _, _ = _lru_gates(xc_s[rows, :], wb_ref, pv_ref, direction)
                a_s[rows, :] = a
                return 0

            lax.fori_loop(0, nchunk, fill, 0)
            toward = 1 if direction == 0 else -1
            hp_s[...] = _shift_rows(a_s[...], toward, s_len)
            _scan_rows(hp_s, dh_s, lam_s, acum_s, reverse=direction == 0)
            hp_s[...] = _shift_rows(hs_ref[direction], -toward, s_len)

            def grads(ci, _):
                rows = pl.ds(pl.multiple_of(ci * ch, ch), ch)
                xc = xc_s[rows, :]
                xcb, r, i, a, mult, sp = _lru_gates(xc, wb_ref, pv_ref, direction)
                du = lam_s[rows, :]
                da = du * hp_s[rows, :]
                dmult = du * i * xc
                di = du * mult * xc
                dlog_a = (da - dmult * a / mult) * a
                dr = dlog_a * (-LRU_C * sp)
                dza = dr * r * (1.0 - r)
                dzx = di * i * (1.0 - i)
                dzab = dza.astype(BF)
                dzxb = dzx.astype(BF)
                dxc_s[rows, :] += (du * mult * i + dot_nt(dzab, wb_ref[2 * direction])
                                   + dot_nt(dzxb, wb_ref[2 * direction + 1]))
                dwb_ref[2 * direction] += dot_tn(xcb, dzab)
                dwb_ref[2 * direction + 1] += dot_tn(xcb, dzxb)
                dpv_ref[1 + direction:2 + direction, :] += jnp.sum(dza, axis=0, keepdims=True)
                dpv_ref[3 + direction:4 + direction, :] += jnp.sum(dzx, axis=0, keepdims=True)
                dpv_ref[5 + direction:6 + direction, :] += jnp.sum(dlog_a * (-LRU_C * r), axis=0, keepdims=True)
                return 0

            lax.fori_loop(0, nchunk, grads, 0)

        for direction in range(2):
            lam = pv_ref[5 + direction:6 + direction, :]
            dpv_ref[5 + direction:6 + direction, :] = dpv_ref[5 + direction:6 + direction, :] * (-jax.nn.sigmoid(-lam))
        dxc = dxc_s[...]
        dpv_ref[0:1, :] = jnp.sum(dxc, axis=0, keepdims=True)
        dxr = cv_ref[0:1, :] * _shift_rows(dxc, 2, s_len)
        for j in range(1, CONV_WIDTH):
            dxr = dxr + cv_ref[j:j + 1, :] * _shift_rows(dxc, 2 - j, s_len)
        dxr_ref[...] = dxr
        dcv_ref[...] = jnp.zeros_like(dcv_ref)
        for j in range(CONV_WIDTH):
            dcv_ref[j:j + 1, :] = jnp.sum(dxc * _shift_rows(xr, j - 2, s_len), axis=0, keepdims=True)

    col = lambda off: pl.BlockSpec((None, s_len, LANE), lambda cb: (off + cb, 0, 0))
    own = col(0)
    small = pl.BlockSpec((8, LANE), lambda cb: (0, cb))
    wspec = pl.BlockSpec((4, None, LANE, LANE), lambda cb: (0, cb, 0, 0))
    return pl.pallas_call(
        body, name="lru_backward", grid=(ncb,),
        out_shape=(SDS((ncb, s_len, LANE), F32), SDS((ncb, s_len, LANE), F32), SDS((8, lw), F32), SDS((8, lw), F32),
                   SDS(wblk.shape, F32)),
        in_specs=[col(0), col(ncb), pl.BlockSpec((2, None, s_len, LANE), lambda cb: (0, cb, 0, 0)), own, small, small, wspec],
        out_specs=(own, own, small, small, wspec),
        scratch_shapes=[pltpu.VMEM((s_len, LANE), F32)] * 7, compiler_params=_cparams(),
    )(proj, proj, hs, dy, cvec, pvec, wblk)


def _attn_specs(s_len, lw, att):
    nb = s_len // BLOCK
    kcol = (2 * lw + att) // LANE
    prev = lambda n: jnp.maximum(n - 1, 0)
    nxt = lambda n: jnp.minimum(n + 1, nb - 1)
    q = _cbm_spec(att // LANE, BLOCK, first=2 * lw // LANE)
    ks = [pl.BlockSpec((None, BLOCK, LANE), lambda n, f=f: (kcol, f(n), 0)) for f in (prev, lambda n: n, nxt)]
    vs = [pl.BlockSpec((None, BLOCK, LANE), lambda n, f=f: (kcol + 1, f(n), 0)) for f in (prev, lambda n: n, nxt)]
    return q, ks, vs


HEADS_PER_LANE_BLOCK = LANE // HEAD_DIM


def _stack_heads(v, kh):
    pieces = []
    for g in range(KV_GROUP):
        blk, sub = divmod(kh * KV_GROUP + g, HEADS_PER_LANE_BLOCK)
        pieces.append(v[blk][:, sub * HEAD_DIM:(sub + 1) * HEAD_DIM])
    return jnp.concatenate(pieces, axis=0)


def _unstack_heads(groups):
    heads = [grp[g * BLOCK:(g + 1) * BLOCK] for grp in groups for g in range(KV_GROUP)]
    return [jnp.concatenate(heads[b * HEADS_PER_LANE_BLOCK:(b + 1) * HEADS_PER_LANE_BLOCK], axis=1)
            for b in range(len(heads) // HEADS_PER_LANE_BLOCK)]


def _key_exists(n, nb):
    j = lax.broadcasted_iota(jnp.int32, (1, 3 * BLOCK), 1)
    return ((n > 0) | (j >= BLOCK)) & ((n < nb - 1) | (j < 2 * BLOCK))


def _attn_probs(qs, kcat, bias_g, sink_g, key_ok):
    logits = jnp.where(key_ok, dot_nt(qs, kcat) + bias_g, NEG_INF)
    m = jnp.maximum(jnp.max(logits, axis=-1, keepdims=True), sink_g)
    p = jnp.exp(logits - m)
    es = jnp.exp(sink_g - m)
    inv = 1.0 / (jnp.sum(p, axis=-1, keepdims=True) + es)
    return p * inv, es * inv


def attention_forward(proj, bias, sink, lw, att):
    s_len = proj.shape[1]
    nb = s_len // BLOCK
    q_spec, k_specs, v_specs = _attn_specs(s_len, lw, att)

    def body(q_ref, kp_ref, kc_ref, kn_ref, vp_ref, vc_ref, vn_ref, b_ref, s_ref, o_ref):
        n = pl.program_id(0)
        q = q_ref[...]
        key_ok = _key_exists(n, nb)
        kall = jnp.concatenate([kp_ref[...], kc_ref[...], kn_ref[...]], axis=0).astype(BF)
        vall = jnp.concatenate([vp_ref[...], vc_ref[...], vn_ref[...]], axis=0).astype(BF)
        outs = []
        for kh in range(N_KV_HEADS):
            grp = slice(kh * KV_GROUP * BLOCK, (kh + 1) * KV_GROUP * BLOCK)
            qs = (_stack_heads(q, kh) * (HEAD_DIM ** -0.5)).astype(BF)
            bias_g = b_ref[kh * KV_GROUP:(kh + 1) * KV_GROUP].reshape(KV_GROUP * BLOCK, 3 * BLOCK)
            p, _ = _attn_probs(qs, kall[:, kh * HEAD_DIM:(kh + 1) * HEAD_DIM], bias_g, s_ref[grp, 0:1], key_ok)
            outs.append(dot_nn(p.astype(BF), vall[:, kh * HEAD_DIM:(kh + 1) * HEAD_DIM]))
        for b, piece in enumerate(_unstack_heads(outs)):
            o_ref[b] = piece

    return pl.pallas_call(
        body, name="attention_forward", grid=(nb,), out_shape=SDS((att // LANE, s_len, LANE), F32),
        in_specs=[q_spec] + k_specs + v_specs
        + [pl.BlockSpec(bias.shape, lambda n: (0, 0, 0)), pl.BlockSpec(sink.shape, lambda n: (0, 0))],
        out_specs=_cbm_spec(att // LANE, BLOCK), compiler_params=_cparams(),
    )(proj, proj, proj, proj, proj, proj, proj, bias, sink)


def attention_backward(proj, y_att, dy, bias, sink, lw, att):
    s_len = proj.shape[1]
    nb = s_len // BLOCK
    kvw = N_KV_HEADS * HEAD_DIM
    q_spec, k_specs, v_specs = _attn_specs(s_len, lw, att)

    def body(q_ref, kp_ref, kc_ref, kn_ref, vp_ref, vc_ref, vn_ref, o_ref, do_ref, b_ref, s_ref,
             dq_ref, dkv_ref, db_ref, ds_ref):
        n = pl.program_id(0)

        @pl.when(n == 0)
        def _():
            dkv_ref[...] = jnp.zeros_like(dkv_ref)
            db_ref[...] = jnp.zeros_like(db_ref)
            ds_ref[...] = jnp.zeros_like(ds_ref)

        q = q_ref[...]
        o = o_ref[...]
        do = do_ref[...]
        kall = jnp.concatenate([kp_ref[...], kc_ref[...], kn_ref[...]], axis=0).astype(BF)
        vall = jnp.concatenate([vp_ref[...], vc_ref[...], vn_ref[...]], axis=0).astype(BF)
        key_ok = _key_exists(n, nb)
        dqs, dks, dvs = [], [], []
        for kh in range(N_KV_HEADS):
            heads = slice(kh * KV_GROUP, (kh + 1) * KV_GROUP)
            grp = slice(kh * KV_GROUP * BLOCK, (kh + 1) * KV_GROUP * BLOCK)
            kcat = kall[:, kh * HEAD_DIM:(kh + 1) * HEAD_DIM]
            vcat = vall[:, kh * HEAD_DIM:(kh + 1) * HEAD_DIM]
            qs = (_stack_heads(q, kh) * (HEAD_DIM ** -0.5)).astype(BF)
            bias_g = b_ref[heads].reshape(KV_GROUP * BLOCK, 3 * BLOCK)
            p, ps = _attn_probs(qs, kcat, bias_g, s_ref[grp, 0:1], key_ok)
            dos = _stack_heads(do, kh)
            dosb = dos.astype(BF)
            delta = jnp.sum(dos * _stack_heads(o, kh), axis=-1, keepdims=True)
            dlog = p * (dot_nt(dosb, vcat) - delta)
            dlogb = dlog.astype(BF)
            db_ref[heads] += dlog.reshape(KV_GROUP, BLOCK, 3 * BLOCK)
            dsink = -ps * delta
            for g in range(KV_GROUP):
                h = kh * KV_GROUP + g
                part = jnp.sum(dsink[g * BLOCK:(g + 1) * BLOCK], axis=0, keepdims=True)
                ds_ref[h:h + 1, :] += jnp.broadcast_to(part, (1, LANE))
            dqs.append(dot_nn(dlogb, kcat) * (HEAD_DIM ** -0.5))
            dks.append(dot_tn(dlogb, qs))
            dvs.append(dot_tn(p.astype(BF), dosb))
        for b, piece in enumerate(_unstack_heads(dqs)):
            dq_ref[b] = piece
        dkv = jnp.concatenate(dks + dvs, axis=1)
        starts = [jnp.maximum(n - 1, 0), n, jnp.minimum(n + 1, nb - 1)]
        for b, st in enumerate(starts):
            rows = pl.ds(pl.multiple_of(st * BLOCK, BLOCK), BLOCK)
            dkv_ref[rows, :] += dkv[b * BLOCK:(b + 1) * BLOCK, :]

    blk = _cbm_spec(att // LANE, BLOCK)
    return pl.pallas_call(
        body, name="attention_backward", grid=(nb,),
        out_shape=(SDS((att // LANE, s_len, LANE), F32), SDS((s_len, 2 * kvw), F32), SDS(bias.shape, F32),
                   SDS((N_HEADS, LANE), F32)),
        in_specs=[q_spec] + k_specs + v_specs
        + [blk, blk, pl.BlockSpec(bias.shape, lambda n: (0, 0, 0)), pl.BlockSpec(sink.shape, lambda n: (0, 0))],
        out_specs=(blk, pl.BlockSpec((s_len, 2 * kvw), lambda n: (0, 0)),
                   pl.BlockSpec(bias.shape, lambda n: (0, 0, 0)), pl.BlockSpec((N_HEADS, LANE), lambda n: (0, 0))),
        compiler_params=_cparams(),
    )(proj, proj, proj, proj, proj, proj, proj, y_att, dy, bias, sink)


def mix_output(x, y_rec, y_att, g_rec, g_att, wfull, lay, tm=512):
    s_len, d = x.shape
    tm = min(tm, s_len)
    lw = y_rec.shape[0] * LANE
    att = y_att.shape[0] * LANE

    def body(x_ref, yr_ref, ya_ref, gr_ref, ga_ref, w_ref, o_ref):
        _, _, nr = _rms(_join_lane_blocks(yr_ref), gr_ref[...])
        _, _, na = _rms(_join_lane_blocks(ya_ref), ga_ref[...])
        y = jnp.concatenate([nr, na], axis=1).astype(BF)
        o_ref[...] = x_ref[...] + dot_nn(y, w_ref[:, :, lay.ih:, :].reshape(d, d))

    row = pl.BlockSpec((tm, d), lambda i: (i, 0))
    return pl.pallas_call(
        body, name="mix_output", grid=(s_len // tm,), out_shape=SDS((s_len, d), F32),
        in_specs=[row, _cbm_spec(lw // LANE, tm), _cbm_spec(att // LANE, tm),
                  pl.BlockSpec((1, lw), lambda i: (0, 0)), pl.BlockSpec((1, att), lambda i: (0, 0)),
                  _w_spec(lay.fh, d, lay.MIX_BLK)],
        out_specs=row, compiler_params=_cparams(),
    )(x, y_rec, y_att, g_rec, g_att, wfull)


def mix_output_backward(dout, y_rec, y_att, g_rec, g_att, wfull, lay, tm=512):
    s_len, d = dout.shape
    tm = min(tm, s_len)
    lw = y_rec.shape[0] * LANE
    att = y_att.shape[0] * LANE
    nt = s_len // tm

    def body(do_ref, yr_ref, ya_ref, gr_ref, ga_ref, w_ref, dyr_ref, dya_ref, dgr_ref, dga_ref, o_ref, acc):
        i = pl.program_id(0)
        gr = gr_ref[...]
        ga = ga_ref[...]
        xhr, rsr, nr = _rms(_join_lane_blocks(yr_ref), gr)
        xha, rsa, na = _rms(_join_lane_blocks(ya_ref), ga)
        y = jnp.concatenate([nr, na], axis=1).astype(BF)
        dob = do_ref[...].astype(BF)
        dy = dot_nt(dob, w_ref[:, :, lay.ih:, :].reshape(d, d))
        dyr, dgr_row = _rms_bwd(dy[:, :lw], xhr, rsr, gr)
        dya, dga_row = _rms_bwd(dy[:, lw:], xha, rsa, ga)
        for j, piece in enumerate(_lane_blocks(dyr)):
            dyr_ref[j] = piece
        for j, piece in enumerate(_lane_blocks(dya)):
            dya_ref[j] = piece

        @pl.when(i == 0)
        def _():
            dgr_ref[...] = jnp.zeros_like(dgr_ref)
            dga_ref[...] = jnp.zeros_like(dga_ref)
            acc[...] = jnp.zeros_like(acc)

        dgr_ref[...] += jnp.sum(dgr_row, axis=0, keepdims=True)
        dga_ref[...] += jnp.sum(dga_row, axis=0, keepdims=True)
        acc[...] += dot_tn(y, dob)

        @pl.when(i == nt - 1)
        def _():
            for p in range(N_CHIPS):
                for q in range(2):
                    o_ref[p, q] = acc[pl.ds((2 * p + q) * lay.oh, lay.oh), :].astype(o_ref.dtype)

    row = pl.BlockSpec((tm, d), lambda i: (i, 0))
    return pl.pallas_call(
        body, name="mix_output_backward", grid=(nt,),
        out_shape=(SDS(y_rec.shape, F32), SDS(y_att.shape, F32), SDS((1, lw), F32), SDS((1, att), F32),
                   SDS((N_CHIPS, 2, lay.oh, d), BF)),
        in_specs=[row, _cbm_spec(lw // LANE, tm), _cbm_spec(att // LANE, tm),
                  pl.BlockSpec((1, lw), lambda i: (0, 0)), pl.BlockSpec((1, att), lambda i: (0, 0)),
                  _w_spec(lay.fh, d, lay.MIX_BLK)],
        out_specs=(_cbm_spec(lw // LANE, tm), _cbm_spec(att // LANE, tm),
                   pl.BlockSpec((1, lw), lambda i: (0, 0)), pl.BlockSpec((1, att), lambda i: (0, 0)),
                   pl.BlockSpec((N_CHIPS, 2, lay.oh, d), lambda i: (0, 0, 0, 0))),
        scratch_shapes=[pltpu.VMEM((d, d), F32)], compiler_params=_cparams(),
    )(dout, y_rec, y_att, g_rec, g_att, wfull)


def loss_head(x, gain, target, tm=512):
    s_len, d = x.shape
    tm = min(tm, s_len)

    def body(x_ref, g_ref, t_ref, dx_ref, dg_ref, loss_ref):
        g = g_ref[...]
        xh, rs, y = _rms(x_ref[...], g)
        err = y - t_ref[...]

        @pl.when(pl.program_id(0) == 0)
        def _():
            dg_ref[...] = jnp.zeros_like(dg_ref)
            loss_ref[...] = jnp.zeros_like(loss_ref)

        part = 0.5 * jnp.sum(jnp.mean(err * err, axis=-1, keepdims=True), axis=0, keepdims=True)
        loss_ref[...] += jnp.broadcast_to(part, loss_ref.shape)
        dx, dgrow = _rms_bwd(err * (1.0 / d), xh, rs, g)
        dx_ref[...] = dx
        dg_ref[...] += jnp.sum(dgrow, axis=0, keepdims=True)

    row = pl.BlockSpec((tm, d), lambda i: (i, 0))
    vec = pl.BlockSpec((1, d), lambda i: (0, 0))
    return pl.pallas_call(
        body, name="loss_head", grid=(s_len // tm,),
        out_shape=(SDS((s_len, d), F32), SDS((1, d), F32), SDS((8, LANE), F32)),
        in_specs=[row, vec, row], out_specs=(row, vec, pl.BlockSpec((8, LANE), lambda i: (0, 0))),
        compiler_params=_cparams(),
    )(x, gain, target)


def adamw(w, g, m, v, tr=512):
    rows, cols = w.shape
    tr = _row_chunk(rows, tr, 8)

    def body(w_ref, g_ref, m_ref, v_ref, d_ref, nm_ref, nv_ref):
        g = g_ref[...]
        m = ADAM_B1 * m_ref[...] + (1.0 - ADAM_B1) * g
        v = ADAM_B2 * v_ref[...] + (1.0 - ADAM_B2) * (g * g)
        m_hat = m / (1.0 - ADAM_B1 ** ADAM_STEP)
        v_hat = v / (1.0 - ADAM_B2 ** ADAM_STEP)
        d_ref[...] = -ADAM_LR * (m_hat / (jnp.sqrt(v_hat) + ADAM_EPS) + ADAM_WD * w_ref[...])
        nm_ref[...] = m
        nv_ref[...] = v

    blk = pl.BlockSpec((tr, cols), lambda i: (i, 0))
    return pl.pallas_call(
        body, name="adamw", grid=(rows // tr,), out_shape=(SDS(w.shape, F32),) * 3,
        in_specs=[blk] * 4, out_specs=(blk,) * 3, compiler_params=_cparams(),
    )(w, g, m, v)


def _pack_rows(arrays, width):
    flat = jnp.concatenate([a.reshape(-1).astype(F32) for a in arrays])
    rows = -(-flat.shape[0] // (8 * width)) * 8
    return jnp.pad(flat, (0, rows * width - flat.shape[0])).reshape(rows, width)


def _unpack_rows(buf, shapes):
    flat = buf.reshape(-1)
    out, off = [], 0
    for shp in shapes:
        n = int(np.prod(shp))
        out.append(flat[off:off + n].reshape(shp))
        off += n
    return out


def _t5_buckets(rel):
    half = N_BUCKETS // 2
    max_exact = half // 2
    ret = (rel > 0).astype(jnp.int32) * half
    n = jnp.abs(rel)
    n_f = jnp.maximum(n, 1).astype(F32)
    large = max_exact + (jnp.log(n_f / max_exact) / math.log(MAX_DISTANCE / max_exact) * (half - max_exact)).astype(jnp.int32)
    large = jnp.minimum(large, half - 1)
    return ret + jnp.where(n < max_exact, n, large)


def _band_buckets():
    t = jnp.arange(BLOCK)[:, None]
    j = jnp.arange(3 * BLOCK)[None, :]
    rel = j - BLOCK - t
    return _t5_buckets(rel), jnp.abs(rel) <= WINDOW


def _block_diag_pairs(w):
    depth, two, nblk, bw, _ = w.shape
    pairs = w.reshape(depth, two, nblk // 2, 2, bw, bw)
    z = jnp.zeros_like(pairs[:, :, :, 0])
    top = jnp.concatenate([pairs[:, :, :, 0], z], axis=-1)
    bot = jnp.concatenate([z, pairs[:, :, :, 1]], axis=-1)
    return jnp.concatenate([top, bot], axis=-2)


def _diag_blocks(dw):
    bw = dw.shape[-1] // 2
    a = dw[:, :, :bw, :bw]
    b = dw[:, :, bw:, bw:]
    return jnp.stack([a, b], axis=2).reshape(dw.shape[0], 2 * dw.shape[1], bw, bw)


def _pack_layer_shard(lay, w_in_l, w_out_l, mats_l):
    halves = []
    for hf in range(2):
        parts = []
        for m, a in enumerate(mats_l):
            a = a if m % 3 == 2 else a.T
            parts.append(a[hf * lay.fh:(hf + 1) * lay.fh])
        parts.append(w_in_l.T[hf * lay.ih:(hf + 1) * lay.ih])
        parts.append(w_out_l[hf * lay.oh:(hf + 1) * lay.oh])
        halves.append(jnp.concatenate(parts, axis=0))
    return jnp.stack(halves).astype(BF)


def kernel(x, ffn1_norm, ffn1_w_gate, ffn1_w_up, ffn1_w_down, mix_norm, w_in, conv_w, conv_b, lru_w_a, lru_b_a, lru_w_x, lru_b_x, lru_lambda, attn_sink, rel_bias, lru_out_norm, attn_out_norm, w_out, ffn2_norm, ffn2_w_gate, ffn2_w_up, ffn2_w_down, final_norm, loss_target, m_ffn1_norm, m_ffn1_w_gate, m_ffn1_w_up, m_ffn1_w_down, m_mix_norm, m_w_in, m_conv_w, m_conv_b, m_lru_w_a, m_lru_b_a, m_lru_w_x, m_lru_b_x, m_lru_lambda, m_attn_sink, m_rel_bias, m_lru_out_norm, m_attn_out_norm, m_w_out, m_ffn2_norm, m_ffn2_w_gate, m_ffn2_w_up, m_ffn2_w_down, m_final_norm, v_ffn1_norm, v_ffn1_w_gate, v_ffn1_w_up, v_ffn1_w_down, v_mix_norm, v_w_in, v_conv_w, v_conv_b, v_lru_w_a, v_lru_b_a, v_lru_w_x, v_lru_b_x, v_lru_lambda, v_attn_sink, v_rel_bias, v_lru_out_norm, v_attn_out_norm, v_w_out, v_ffn2_norm, v_ffn2_w_gate, v_ffn2_w_up, v_ffn2_w_down, v_final_norm):
    depth, d = ffn1_norm.shape
    d_ff = N_CHIPS * ffn1_w_gate.shape[2]
    d_in = N_CHIPS * w_in.shape[2]
    lw = conv_b.shape[1]
    att = N_HEADS * HEAD_DIM
    lay = Layout(d, d_ff, d_in)
    k_chip = 2 * lax.axis_index("x") + lax.axis_index("y")
    pos = jnp.stack([k_chip, lax.axis_index("c")]).astype(jnp.int32)

    mats = (ffn1_w_gate, ffn1_w_up, ffn1_w_down, ffn2_w_gate, ffn2_w_up, ffn2_w_down)
    wsh = [_pack_layer_shard(lay, w_in[l], w_out[l], [m[l] for m in mats]) for l in range(depth)]

    def gather_start(l):
        land = lax.dynamic_update_slice(lax.empty((N_CHIPS, 2, lay.rows, d), BF), wsh[l][None], (k_chip, 0, 0, 0))
        return split_start(f"gather_start_{l}", [land], 3, gather_plan)

    def gather_finish(l, started, after):
        ssem, rsem, bufs, _ = started
        land, = split_wait(f"gather_wait_{l}", ssem, rsem, bufs, after, gather_plan)
        return gather_pair(land)

    sharded_small = (conv_w, lru_b_a, lru_b_x, lru_lambda)
    sshard = jnp.concatenate([a.reshape(-1, LANE) for a in sharded_small], axis=0)
    sfull = gather_small(sshard)
    small_full, off = [], 0
    for a in sharded_small:
        r = a.shape[0] * a.shape[1]
        piece = sfull[:, off:off + r].reshape((N_CHIPS,) + a.shape)
        small_full.append(jnp.moveaxis(piece, 0, 2).reshape(a.shape[0], a.shape[1], N_CHIPS * LANE))
        off += r
    conv_w_f, b_a_f, b_x_f, lam_f = small_full

    zrow = jnp.zeros((1, lw), F32)
    wblk_a = _block_diag_pairs(lru_w_a)
    wblk_x = _block_diag_pairs(lru_w_x)
    buckets, in_band = _band_buckets()
    onehot = (buckets.reshape(-1)[:, None] == jnp.arange(N_BUCKETS)[None, :]).astype(F32)
    bias = jnp.dot(rel_bias.T, onehot.T, precision=lax.Precision.HIGHEST).reshape(N_HEADS, BLOCK, 3 * BLOCK)
    bias = jnp.where(in_band[None], bias, NEG_INF)

    def layer_small(l):
        cvec = jnp.concatenate([conv_w_f[l], jnp.zeros((8 - CONV_WIDTH, lw), F32)], axis=0)
        pvec = jnp.concatenate([conv_b[l][None], b_a_f[l], b_x_f[l], lam_f[l], zrow], axis=0)
        wblk = jnp.stack([wblk_a[l, 0], wblk_x[l, 0], wblk_a[l, 1], wblk_x[l, 1]]).astype(BF)
        sink = jnp.broadcast_to(attn_sink[l][:, None, None], (N_HEADS, BLOCK, LANE)).reshape(N_HEADS * BLOCK, LANE)
        return cvec, pvec, wblk, sink

    xs = x[0]
    wfull = [None] * depth
    wfull[0] = gather_finish(0, gather_start(0), xs)
    started = gather_start(1) if depth > 1 else None
    saved = []
    for l in range(depth):
        cvec, pvec, wblk, sink = layer_small(l)
        deps = (started[3],) if started is not None else ()
        x1, gate1, up1 = ffn_forward(xs, ffn1_norm[l][None], wfull[l], lay, 0, deps=deps)
        proj = mix_project(x1, mix_norm[l][None], wfull[l], lay)
        y_rec, hs = lru_forward(proj, cvec, pvec, wblk, lw)
        y_att = attention_forward(proj, bias, sink, lw, att)
        x2 = mix_output(x1, y_rec, y_att, lru_out_norm[l][None], attn_out_norm[l][None], wfull[l], lay)
        x3, gate2, up2 = ffn_forward(x2, ffn2_norm[l][None], wfull[l], lay, 1)
        saved.append((xs, x1, x2, proj, y_rec, hs, y_att, (gate1, up1), (gate2, up2)))
        xs = x3
        if l + 1 < depth:
            wfull[l + 1] = gather_finish(l + 1, started, x3)
            started = gather_start(l + 2) if l + 2 < depth else None

    dx, d_final, loss_tile = loss_head(xs, final_norm[None], loss_target[0])
    loss = lax.psum(loss_tile[0, 0], ("x", "y", "c"))

    layer_names = ["ffn1_norm", "mix_norm", "conv_w", "conv_b", "lru_w_a", "lru_b_a", "lru_w_x", "lru_b_x", "lru_lambda",
                   "attn_sink", "lru_out_norm", "attn_out_norm", "ffn2_norm"]
    dbias_total = jnp.zeros(bias.shape, F32)

    def ffn_back(xin, gain, dout, pre, gb, l, which, deps=()):
        dxo, dg, dgate, dup, act, h, df = ffn_backward_dx(xin, gain, dout, *pre, wfull[l], lay, which, deps=deps)
        gb = weight_grad_tn(dgate, h, gb, lay, 3 * which + 0)
        gb = weight_grad_tn(dup, h, gb, lay, 3 * which + 1)
        gb = weight_grad_tn(act, df, gb, lay, 3 * which + 2)
        return dxo, dg[0], gb

    def reduce_start(l, gb, sb):
        p1, sp1 = exchange_pair(gb, sb)
        cs = pair_sum(pos, gb, p1)
        ss = small_pair_sum(sb, sp1)
        lands = [lax.empty((3,) + cs.shape[1:], cs.dtype), lax.empty((N_CHIPS,) + ss.shape, ss.dtype)]
        return split_start(f"reduce_start_{l}", [cs, ss] + lands, 6, reduce_plan)

    def reduce_finish(l, started, after):
        ssem, rsem, bufs, _ = started
        cs, ss, p3, sp3 = split_wait(f"reduce_wait_{l}", ssem, rsem, bufs, after, reduce_plan)
        return exchange_final(chip_sum(pos, cs, p3)), small_chip_sum(pos, ss, sp3)

    gf = [None] * depth
    small_sums = [None] * depth
    small_shapes = [None] * depth
    in_flight = None
    for l in reversed(range(depth)):
        x0, x1, x2, proj, y_rec, hs, y_att, pre1, pre2 = saved[l]
        cvec, pvec, wblk, sink = layer_small(l)
        gb = lax.empty((N_CHIPS, 2, lay.rows, d), BF)
        part = {}
        deps = (in_flight[1][3],) if in_flight is not None else ()
        dx, part["ffn2_norm"], gb = ffn_back(x2, ffn2_norm[l][None], dx, pre2, gb, l, 1, deps=deps)
        dyr, dya, dgr, dga, dwout = mix_output_backward(dx, y_rec, y_att, lru_out_norm[l][None], attn_out_norm[l][None],
                                                        wfull[l], lay)
        part["lru_out_norm"] = dgr[0]
        part["attn_out_norm"] = dga[0]
        dq, dkv, dbias, dsink = attention_backward(proj, y_att, dya, bias, sink, lw, att)
        dbias_total = dbias_total + dbias
        part["attn_sink"] = dsink[:, 0]
        dxr, dgt, dcv, dpv, dwb = lru_backward(proj, hs, dyr, cvec, pvec, wblk, lw)
        part["conv_w"] = dcv[:CONV_WIDTH]
        part["conv_b"] = dpv[0]
        part["lru_b_a"] = dpv[1:3]
        part["lru_b_x"] = dpv[3:5]
        part["lru_lambda"] = dpv[5:7]
        part["lru_w_a"] = _diag_blocks(jnp.stack([dwb[0], dwb[2]]))
        part["lru_w_x"] = _diag_blocks(jnp.stack([dwb[1], dwb[3]]))
        dx, dgm, gb = mix_project_backward(x1, mix_norm[l][None], dx, dxr, dgt, dq, dkv, dwout, wfull[l], gb, lay)
        part["mix_norm"] = dgm[0]
        dx, part["ffn1_norm"], gb = ffn_back(x0, ffn1_norm[l][None], dx, pre1, gb, l, 0)
        pieces = [part[n] for n in layer_names]
        if l == 0:
            d_rel_bias = jnp.dot(dbias_total.reshape(N_HEADS, -1), onehot, precision=lax.Precision.HIGHEST).T
            pieces += [d_rel_bias, d_final[0]]
        small_shapes[l] = [p.shape for p in pieces]
        if in_flight is not None:
            gf[in_flight[0]], small_sums[in_flight[0]] = reduce_finish(in_flight[0], in_flight[1], dx)
        in_flight = (l, reduce_start(l, gb, _pack_rows(pieces, 1024)))
    gf[in_flight[0]], small_sums[in_flight[0]] = reduce_finish(in_flight[0], in_flight[1], dx)
    grad_x = dx[None]

    per_layer = [_unpack_rows(small_sums[l], small_shapes[l]) for l in range(depth)]
    grads = {n: jnp.stack([per_layer[l][i] for l in range(depth)]) for i, n in enumerate(layer_names)}
    grads["rel_bias"], grads["final_norm"] = per_layer[0][len(layer_names):]
    for name in ("conv_w", "lru_b_a", "lru_b_x", "lru_lambda"):
        grads[name] = lax.dynamic_slice_in_dim(grads[name], k_chip * LANE, LANE, axis=2)

    def from_halves(l, row0, n, transpose):
        a = jnp.concatenate([gf[l][0, row0:row0 + n], gf[l][1, row0:row0 + n]], axis=0)
        return a.T if transpose else a

    grads["w_in"] = jnp.stack([from_halves(l, 6 * lay.fh, lay.ih, True) for l in range(depth)])
    grads["w_out"] = jnp.stack([from_halves(l, 6 * lay.fh + lay.ih, lay.oh, False) for l in range(depth)])
    for m, name in enumerate(("ffn1_w_gate", "ffn1_w_up", "ffn1_w_down", "ffn2_w_gate", "ffn2_w_up", "ffn2_w_down")):
        grads[name] = jnp.stack([from_halves(l, m * lay.fh, lay.fh, m % 3 != 2) for l in range(depth)])

    weights = dict(ffn1_norm=ffn1_norm, ffn1_w_gate=ffn1_w_gate, ffn1_w_up=ffn1_w_up, ffn1_w_down=ffn1_w_down, mix_norm=mix_norm, w_in=w_in, conv_w=conv_w, conv_b=conv_b, lru_w_a=lru_w_a, lru_b_a=lru_b_a, lru_w_x=lru_w_x, lru_b_x=lru_b_x, lru_lambda=lru_lambda, attn_sink=attn_sink, rel_bias=rel_bias, lru_out_norm=lru_out_norm, attn_out_norm=attn_out_norm, w_out=w_out, ffn2_norm=ffn2_norm, ffn2_w_gate=ffn2_w_gate, ffn2_w_up=ffn2_w_up, ffn2_w_down=ffn2_w_down, final_norm=final_norm)
    m_in = dict(ffn1_norm=m_ffn1_norm, ffn1_w_gate=m_ffn1_w_gate, ffn1_w_up=m_ffn1_w_up, ffn1_w_down=m_ffn1_w_down, mix_norm=m_mix_norm, w_in=m_w_in, conv_w=m_conv_w, conv_b=m_conv_b, lru_w_a=m_lru_w_a, lru_b_a=m_lru_b_a, lru_w_x=m_lru_w_x, lru_b_x=m_lru_b_x, lru_lambda=m_lru_lambda, attn_sink=m_attn_sink, rel_bias=m_rel_bias, lru_out_norm=m_lru_out_norm, attn_out_norm=m_attn_out_norm, w_out=m_w_out, ffn2_norm=m_ffn2_norm, ffn2_w_gate=m_ffn2_w_gate, ffn2_w_up=m_ffn2_w_up, ffn2_w_down=m_ffn2_w_down, final_norm=m_final_norm)
    v_in = dict(ffn1_norm=v_ffn1_norm, ffn1_w_gate=v_ffn1_w_gate, ffn1_w_up=v_ffn1_w_up, ffn1_w_down=v_ffn1_w_down, mix_norm=v_mix_norm, w_in=v_w_in, conv_w=v_conv_w, conv_b=v_conv_b, lru_w_a=v_lru_w_a, lru_b_a=v_lru_b_a, lru_w_x=v_lru_w_x, lru_b_x=v_lru_b_x, lru_lambda=v_lru_lambda, attn_sink=v_attn_sink, rel_bias=v_rel_bias, lru_out_norm=v_lru_out_norm, attn_out_norm=v_attn_out_norm, w_out=v_w_out, ffn2_norm=v_ffn2_norm, ffn2_w_gate=v_ffn2_w_gate, ffn2_w_up=v_ffn2_w_up, ffn2_w_down=v_ffn2_w_down, final_norm=v_final_norm)
    order = list(weights)
    large = ("ffn1_w_gate", "ffn1_w_up", "ffn1_w_down", "w_in", "w_out", "ffn2_w_gate", "ffn2_w_up", "ffn2_w_down")
    delta, new_m, new_v = {}, {}, {}
    for name in large:
        shp = weights[name].shape
        two_d = lambda a: a.reshape(-1, shp[-1])
        dl, nm, nv = adamw(two_d(weights[name]), two_d(grads[name]), two_d(m_in[name]), two_d(v_in[name]))
        delta[name], new_m[name], new_v[name] = dl.reshape(shp), nm.reshape(shp), nv.reshape(shp)
    small = [n for n in order if n not in large]
    packed = [_pack_rows([src[n] for n in small], 1024) for src in (weights, grads, m_in, v_in)]
    outs = adamw(*packed)
    shapes = [weights[n].shape for n in small]
    for dst, buf in zip((delta, new_m, new_v), outs):
        dst.update(zip(small, _unpack_rows(buf, shapes)))

    return (loss, grad_x, *[grads[n] for n in order], *[delta[n] for n in order],
            *[new_m[n] for n in order], *[new_v[n] for n in order])
```

```python
import functools
import math

import jax
import jax.numpy as jnp
import numpy as np
from jax import lax
from jax.experimental import pallas as pl
from jax.experimental.pallas import tpu as pltpu

BF = jnp.bfloat16
F32 = jnp.float32
SDS = jax.ShapeDtypeStruct
MESH = pl.DeviceIdType.MESH
ANY = pl.BlockSpec(memory_space=pl.ANY)

N_CHIPS = 4
N_HEADS = 8
N_KV_HEADS = 2
KV_GROUP = N_HEADS // N_KV_HEADS
HEAD_DIM = 64
BLOCK = 128
WINDOW = 128
N_BUCKETS = 32
MAX_DISTANCE = 128
LRU_C = 8.0
CONV_WIDTH = 4
LANE = 128
SCAN_SEGMENTS = 8
SCAN_CHAINS = 8
EPS = 1e-6
FFN_RES = 0.5
NEG_INF = -1e30
ADAM_LR = 0.001
ADAM_B1 = 0.9
ADAM_B2 = 0.999
ADAM_EPS = 1e-08
ADAM_WD = 0.01
ADAM_STEP = 10
VMEM_LIMIT = 60000 * 1024
GELU_C = math.sqrt(2.0 / math.pi)


def dot_nn(a, b):
    return lax.dot_general(a, b, (((1,), (0,)), ((), ())), preferred_element_type=F32)


def dot_nt(a, b):
    return lax.dot_general(a, b, (((1,), (1,)), ((), ())), preferred_element_type=F32)


def dot_tn(a, b):
    return lax.dot_general(a, b, (((0,), (0,)), ((), ())), preferred_element_type=F32)


def _cparams(**kw):
    return pltpu.CompilerParams(vmem_limit_bytes=VMEM_LIMIT, **kw)


class Layout:
    MIX_BLK = 6

    def __init__(self, d_model, d_ff, d_in):
        self.fh = d_ff // (2 * N_CHIPS)
        self.ih = d_in // (2 * N_CHIPS)
        self.oh = d_model // (2 * N_CHIPS)
        assert self.ih + self.oh == self.fh, "w_in^T and w_out rows must fill one ffn-sized block"
        self.rows = 7 * self.fh


def _row_chunk(rows, target, step=16):
    best = rows
    for c in range(step, min(rows, target) + 1, step):
        if rows % c == 0:
            best = c
    return best


def _mesh_pos():
    return lax.axis_index("x"), lax.axis_index("y"), lax.axis_index("c")


def _rcopy(src, dst, ssem, rsem, dev):
    return pltpu.make_async_remote_copy(src_ref=src, dst_ref=dst, send_sem=ssem, recv_sem=rsem,
                                        device_id=dev, device_id_type=MESH)


HBM = pl.BlockSpec(memory_space=pltpu.HBM)
SEM = pl.BlockSpec(memory_space=pltpu.SEMAPHORE)
DATAFLOW = pltpu.SideEffectType.DATAFLOW_SIDE_EFFECTING


def _chip_peers():
    x, y, c = _mesh_pos()
    peers = [(1 - x, y), (x, 1 - y), (1 - x, 1 - y)]
    return x, y, c, 2 * x + y, [(px, py, 2 * px + py) for px, py in peers]


def split_start(name, bufs, n, plan):
    nb = len(bufs)

    def body(*refs):
        sends, _ = plan(refs[:nb], refs[nb], refs[nb + 1])
        for cp in sends:
            cp.start()
        refs[-1][...] = jnp.zeros_like(refs[-1])

    out = pl.pallas_call(
        body, name=name,
        out_shape=(pltpu.SemaphoreType.DMA((n,)), pltpu.SemaphoreType.DMA((n,)),
                   *[pltpu.HBM(b.shape, b.dtype) for b in bufs], SDS((8, LANE), F32)),
        in_specs=[HBM] * nb, out_specs=(SEM, SEM, *([HBM] * nb), pl.BlockSpec(memory_space=pltpu.VMEM)),
        input_output_aliases={i: 2 + i for i in range(nb)},
        compiler_params=pltpu.CompilerParams(has_side_effects=DATAFLOW),
    )(*[pltpu.with_memory_space_constraint(b, pltpu.HBM) for b in bufs])
    return out[0], out[1], list(out[2:2 + nb]), out[-1]


def split_wait(name, ssem, rsem, bufs, after, plan):
    nb = len(bufs)

    def body(*refs):
        sends, recvs = plan(refs[:nb], refs[nb], refs[nb + 1])
        for cp in recvs:
            cp.wait_recv()
        for cp in sends:
            cp.wait_send()

    out = pl.pallas_call(
        body, name=name, out_shape=tuple(pltpu.HBM(b.shape, b.dtype) for b in bufs),
        in_specs=[HBM] * nb + [SEM, SEM] + [ANY] * len(after), out_specs=tuple([HBM] * nb),
        input_output_aliases={i: i for i in range(nb)},
        compiler_params=pltpu.CompilerParams(has_side_effects=DATAFLOW),
    )(*bufs, ssem, rsem, *after)
    return list(out)


def gather_plan(refs, ssem, rsem):
    land_ref, = refs
    _, _, c, k, peers = _chip_peers()
    sends = [_rcopy(land_ref.at[k, c], land_ref.at[k, c], ssem.at[j], rsem.at[j], (px, py, c))
             for j, (px, py, _) in enumerate(peers)]
    recvs = [_rcopy(land_ref.at[kp, c], land_ref.at[kp, c], ssem.at[j], rsem.at[j], (px, py, c))
             for j, (px, py, kp) in enumerate(peers)]
    return sends, recvs


def reduce_plan(refs, ssem, rsem):
    cs_ref, ss_ref, p3_ref, sp3_ref = refs
    _, _, c, k, peers = _chip_peers()
    sends, recvs = [], []
    for j, (px, py, kp) in enumerate(peers):
        sends.append(_rcopy(cs_ref.at[kp], p3_ref.at[j], ssem.at[j], rsem.at[j], (px, py, c)))
        recvs.append(_rcopy(cs_ref.at[kp], p3_ref.at[j], ssem.at[j], rsem.at[j], (px, py, c)))
        sends.append(_rcopy(ss_ref, sp3_ref.at[k], ssem.at[3 + j], rsem.at[3 + j], (px, py, c)))
        recvs.append(_rcopy(ss_ref, sp3_ref.at[kp], ssem.at[3 + j], rsem.at[3 + j], (px, py, c)))
    return sends, recvs


def gather_small(sshard):
    def body(s_ref, sf_ref, lsem, ssem, rsem):
        _, _, c, k, peers = _chip_peers()
        own = pltpu.make_async_copy(s_ref, sf_ref.at[k], lsem)
        own.start()
        sends = [_rcopy(s_ref, sf_ref.at[k], ssem.at[j], rsem.at[j], (px, py, c)) for j, (px, py, _) in enumerate(peers)]
        recvs = [_rcopy(s_ref, sf_ref.at[kp], ssem.at[j], rsem.at[j], (px, py, c)) for j, (px, py, kp) in enumerate(peers)]
        for cp in sends:
            cp.start()
        for cp in recvs:
            cp.wait_recv()
        for cp in sends:
            cp.wait_send()
        own.wait()

    return pl.pallas_call(
        body, name="gather_small", out_shape=SDS((N_CHIPS,) + sshard.shape, sshard.dtype),
        in_specs=[ANY], out_specs=ANY,
        scratch_shapes=[pltpu.SemaphoreType.DMA, pltpu.SemaphoreType.DMA((3,)), pltpu.SemaphoreType.DMA((3,))],
    )(sshard)


def gather_pair(land):
    def body(land_in, land_ref, ssem, rsem):
        x, y, c, k, peers = _chip_peers()
        sib = (x, y, 1 - c)
        sends = [_rcopy(land_ref.at[kp, c], land_ref.at[kp, c], ssem.at[j], rsem.at[j], sib) for j, (_, _, kp) in enumerate(peers)]
        recvs = [_rcopy(land_ref.at[kp, 1 - c], land_ref.at[kp, 1 - c], ssem.at[j], rsem.at[j], sib)
                 for j, (_, _, kp) in enumerate(peers)]
        for cp in sends:
            cp.start()
        for cp in recvs:
            cp.wait_recv()
        for cp in sends:
            cp.wait_send()

    return pl.pallas_call(
        body, name="gather_pair", out_shape=SDS(land.shape, land.dtype),
        in_specs=[ANY], out_specs=ANY, input_output_aliases={0: 0},
        scratch_shapes=[pltpu.SemaphoreType.DMA((3,)), pltpu.SemaphoreType.DMA((3,))],
    )(land)


def exchange_pair(gb, sb):
    n, _, rh, d = gb.shape

    def body(gb_ref, sb_ref, p_ref, sp_ref, ssem, rsem):
        x, y, c = _mesh_pos()
        sib = (x, y, 1 - c)
        sends = [_rcopy(gb_ref.at[kk, 1 - c], p_ref.at[kk], ssem.at[kk], rsem.at[kk], sib) for kk in range(n)]
        sends.append(_rcopy(sb_ref, sp_ref, ssem.at[n], rsem.at[n], sib))
        for cp in sends:
            cp.start()
        for cp in sends:
            cp.wait_recv()
        for cp in sends:
            cp.wait_send()

    return pl.pallas_call(
        body, name="exchange_pair",
        out_shape=(SDS((n, rh, d), gb.dtype), SDS(sb.shape, sb.dtype)),
        in_specs=[ANY, ANY], out_specs=(ANY, ANY),
        scratch_shapes=[pltpu.SemaphoreType.DMA((n + 1,)), pltpu.SemaphoreType.DMA((n + 1,))],
    )(gb, sb)


def exchange_final(gf):
    _, rh, d = gf.shape
    nch = 4 if rh % 32 == 0 else 1
    cr = rh // nch

    def body(gf_ref, out_ref, ssem, rsem):
        x, y, c = _mesh_pos()
        sib = (x, y, 1 - c)
        sends = [_rcopy(out_ref.at[c, pl.ds(q * cr, cr)], out_ref.at[c, pl.ds(q * cr, cr)], ssem.at[q], rsem.at[q], sib)
                 for q in range(nch)]
        recvs = [_rcopy(out_ref.at[1 - c, pl.ds(q * cr, cr)], out_ref.at[1 - c, pl.ds(q * cr, cr)], ssem.at[q], rsem.at[q], sib)
                 for q in range(nch)]
        for cp in sends:
            cp.start()
        for cp in recvs:
            cp.wait_recv()
        for cp in sends:
            cp.wait_send()

    return pl.pallas_call(
        body, name="exchange_final",
        out_shape=SDS(gf.shape, gf.dtype),
        in_specs=[ANY], out_specs=ANY, input_output_aliases={0: 0},
        scratch_shapes=[pltpu.SemaphoreType.DMA((nch,)), pltpu.SemaphoreType.DMA((nch,))],
    )(gf)


def pair_sum(pos, gb, p1):
    n, _, rh, d = gb.shape
    cr = _row_chunk(rh, 1024)

    def body(pos_ref, a_ref, b_ref, o_ref):
        o_ref[...] = (a_ref[...].astype(F32) + b_ref[...].astype(F32)).astype(o_ref.dtype)

    return pl.pallas_call(
        body, name="pair_sum", out_shape=SDS((n, rh, d), gb.dtype),
        grid_spec=pltpu.PrefetchScalarGridSpec(
            num_scalar_prefetch=1, grid=(n, rh // cr),
            in_specs=[pl.BlockSpec((None, None, cr, d), lambda kk, r, pos: (kk, pos[1], r, 0)),
                      pl.BlockSpec((None, cr, d), lambda kk, r, pos: (kk, r, 0))],
            out_specs=pl.BlockSpec((None, cr, d), lambda kk, r, pos: (kk, r, 0))),
        compiler_params=_cparams(),
    )(pos, gb, p1)


def chip_sum(pos, cs, p3):
    n, rh, d = cs.shape
    cr = _row_chunk(rh, 512)

    def body(pos_ref, a_ref, b_ref, o_ref):
        acc = a_ref[...].astype(F32)
        for j in range(3):
            acc = acc + b_ref[j].astype(F32)
        o_ref[...] = acc

    return pl.pallas_call(
        body, name="chip_sum", out_shape=SDS((2, rh, d), F32),
        grid_spec=pltpu.PrefetchScalarGridSpec(
            num_scalar_prefetch=1, grid=(rh // cr,),
            in_specs=[pl.BlockSpec((None, cr, d), lambda r, pos: (pos[0], r, 0)),
                      pl.BlockSpec((3, cr, d), lambda r, pos: (0, r, 0))],
            out_specs=pl.BlockSpec((None, cr, d), lambda r, pos: (pos[1], r, 0))),
        compiler_params=_cparams(),
    )(pos, cs, p3)


def small_pair_sum(a, b):
    def body(a_ref, b_ref, o_ref):
        o_ref[...] = a_ref[...] + b_ref[...]

    return pl.pallas_call(body, name="small_pair_sum", out_shape=SDS(a.shape, a.dtype),
                          compiler_params=_cparams())(a, b)


def small_chip_sum(pos, own, p):
    ns, w = own.shape

    def body(pos_ref, own_ref, p0, p1, p2, p3, o_ref):
        k = pos_ref[0]
        acc = None
        for chip, ref in enumerate((p0, p1, p2, p3)):
            term = jnp.where(k == chip, own_ref[...], ref[...])
            acc = term if acc is None else acc + term
        o_ref[...] = acc

    def slot(chip):
        return pl.BlockSpec((None, ns, w), lambda i, pos: (jnp.where(pos[0] == chip, (chip + 1) % N_CHIPS, chip), 0, 0))

    return pl.pallas_call(
        body, name="small_chip_sum", out_shape=SDS(own.shape, own.dtype),
        grid_spec=pltpu.PrefetchScalarGridSpec(
            num_scalar_prefetch=1, grid=(1,),
            in_specs=[pl.BlockSpec((ns, w), lambda i, pos: (0, 0))] + [slot(chip) for chip in range(N_CHIPS)],
            out_specs=pl.BlockSpec((ns, w), lambda i, pos: (0, 0))),
        compiler_params=_cparams(),
    )(pos, own, p, p, p, p)


def _rms(x, g):
    rs = lax.rsqrt(jnp.mean(x * x, axis=-1, keepdims=True) + EPS)
    xh = x * rs
    return xh, rs, xh * g


def _rms_bwd(dy, xh, rs, g):
    dxh = dy * g
    dx = rs * (dxh - xh * jnp.mean(dxh * xh, axis=-1, keepdims=True))
    return dx, dy * xh


def _gelu(x):
    t = jnp.tanh(GELU_C * (x + 0.044715 * x * x * x))
    return 0.5 * x * (1.0 + t), t


def _gelu_grad(x, t):
    return 0.5 * (1.0 + t) + 0.5 * x * (1.0 - t * t) * GELU_C * (1.0 + 3.0 * 0.044715 * x * x)


def _shift_rows(v, s, n):
    if s == 0:
        return v
    t = lax.broadcasted_iota(jnp.int32, v.shape, 0)
    rolled = pltpu.roll(v, (-s) % n, 0)
    return jnp.where((t + s >= 0) & (t + s < n), rolled, 0.0)


def _scan_rows(a_ref, u_ref, h_ref, acum_ref, reverse):
    s_len, w = a_ref.shape
    chains = max(1, min(SCAN_CHAINS, s_len // (8 * SCAN_SEGMENTS)))
    nseg = SCAN_SEGMENTS * chains
    seg = s_len // nseg

    def step(j, carry):
        jj = (seg - 1 - j) if reverse else j
        out = []
        for c, (h, acc) in enumerate(carry):
            idx = pl.ds(c * SCAN_SEGMENTS * seg + jj, SCAN_SEGMENTS, stride=seg)
            a = a_ref[idx, :]
            h = a * h + u_ref[idx, :]
            acc = a * acc
            h_ref[idx, :] = h
            acum_ref[idx, :] = acc
            out.append((h, acc))
        return tuple(out)

    init = tuple((jnp.zeros((SCAN_SEGMENTS, w), F32), jnp.ones((SCAN_SEGMENTS, w), F32)) for _ in range(chains))
    ends = lax.fori_loop(0, seg, step, init, unroll=min(8, seg))
    order = range(nseg - 2, -1, -1) if reverse else range(1, nseg)
    inflow = jnp.zeros((1, w), F32)
    for s in order:
        src = s + 1 if reverse else s - 1
        h, acc = ends[src // SCAN_SEGMENTS]
        r = src % SCAN_SEGMENTS
        inflow = h[r:r + 1, :] + acc[r:r + 1, :] * inflow
        rows = pl.ds(s * seg, seg)
        h_ref[rows, :] = h_ref[rows, :] + acum_ref[rows, :] * inflow


def _w_spec(rows_half, d, blk):
    return pl.BlockSpec((N_CHIPS, 2, rows_half, d), lambda *_: (0, 0, blk, 0), pipeline_mode=pl.Buffered(1))


def ffn_forward(x, gain, wfull, lay, which, deps=(), tm=512):
    s_len, d = x.shape
    tm = min(tm, s_len)
    f = 8 * lay.fh
    fc = f // 2

    def body(x_ref, g_ref, wg_ref, wu_ref, wd_ref, *rest):
        o_ref, gate_ref, up_ref = rest[len(deps):]
        x = x_ref[...]
        _, _, hn = _rms(x, g_ref[...])
        h = hn.astype(BF)
        y = jnp.zeros((tm, d), F32)
        for part in range(2):
            cols = slice(part * fc, (part + 1) * fc)
            gate = dot_nt(h, wg_ref[...].reshape(f, d)[cols])
            up = dot_nt(h, wu_ref[...].reshape(f, d)[cols])
            act = (gate * jax.nn.sigmoid(gate) * up).astype(BF)
            y = y + dot_nn(act, wd_ref[...].reshape(f, d)[cols])
            gate_ref[:, cols] = gate.astype(BF)
            up_ref[:, cols] = up.astype(BF)
        o_ref[...] = x + FFN_RES * y

    row = pl.BlockSpec((tm, d), lambda i: (i, 0))
    wide = pl.BlockSpec((tm, f), lambda i: (i, 0))
    return pl.pallas_call(
        body, name="ffn_forward", grid=(s_len // tm,),
        out_shape=(SDS((s_len, d), F32), SDS((s_len, f), BF), SDS((s_len, f), BF)),
        in_specs=[row, pl.BlockSpec((1, d), lambda i: (0, 0))]
        + [_w_spec(lay.fh, d, 3 * which + m) for m in range(3)] + [ANY] * len(deps),
        out_specs=(row, wide, wide), compiler_params=_cparams(),
    )(x, gain, wfull, wfull, wfull, *deps)


def ffn_backward_dx(x, gain, dout, gate_bf, up_bf, wfull, lay, which, deps=(), tm=256):
    s_len, d = x.shape
    tm = min(tm, s_len)
    f = 8 * lay.fh
    fc = f // 2
    nt = s_len // tm

    def body(x_ref, g_ref, do_ref, gate_ref, up_ref, wg_ref, wu_ref, wd_ref, *rest):
        dx_ref, dg_ref, dgate_ref, dup_ref, act_ref, h_ref, df_ref = rest[len(deps):]
        x = x_ref[...]
        g = g_ref[...]
        xh, rs, hn = _rms(x, g)
        h = hn.astype(BF)
        do = do_ref[...]
        df = (FFN_RES * do).astype(BF)
        dh = jnp.zeros((tm, d), F32)
        for part in range(2):
            cols = slice(part * fc, (part + 1) * fc)
            wg = wg_ref[...].reshape(f, d)[cols]
            wu = wu_ref[...].reshape(f, d)[cols]
            gate = gate_ref[:, cols].astype(F32)
            up = up_ref[:, cols].astype(F32)
            sg = jax.nn.sigmoid(gate)
            silu = gate * sg
            dact = dot_nt(df, wd_ref[...].reshape(f, d)[cols])
            dup = (dact * silu).astype(BF)
            dgate = (dact * up * (sg * (1.0 + gate * (1.0 - sg)))).astype(BF)
            dh = dh + dot_nn(dgate, wg) + dot_nn(dup, wu)
            dgate_ref[:, cols] = dgate
            dup_ref[:, cols] = dup
            act_ref[:, cols] = (silu * up).astype(BF)
        dxn, dgrow = _rms_bwd(dh, xh, rs, g)
        dx_ref[...] = do + dxn

        @pl.when(pl.program_id(0) == 0)
        def _():
            dg_ref[...] = jnp.zeros_like(dg_ref)

        dg_ref[...] += jnp.sum(dgrow, axis=0, keepdims=True)
        h_ref[...] = h
        df_ref[...] = df

    row = pl.BlockSpec((tm, d), lambda i: (i, 0))
    wide = pl.BlockSpec((tm, f), lambda i: (i, 0))
    vec = pl.BlockSpec((1, d), lambda i: (0, 0))
    return pl.pallas_call(
        body, name="ffn_backward_dx", grid=(nt,),
        out_shape=(SDS((s_len, d), F32), SDS((1, d), F32), SDS((s_len, f), BF), SDS((s_len, f), BF),
                   SDS((s_len, f), BF), SDS((s_len, d), BF), SDS((s_len, d), BF)),
        in_specs=[row, vec, row, wide, wide] + [_w_spec(lay.fh, d, 3 * which + m) for m in range(3)] + [ANY] * len(deps),
        out_specs=(row, vec, wide, wide, wide, row, row), compiler_params=_cparams(),
    )(x, gain, dout, gate_bf, up_bf, wfull, wfull, wfull, *deps)


def weight_grad_tn(a, b, gb, lay, blk, tk=512):
    s_len, f = a.shape
    tk = min(tk, s_len)
    d = b.shape[1]
    fc = f // 2
    nk = s_len // tk

    def body(a_ref, b_ref, gb_ref, o_ref, acc):
        kt = pl.program_id(1)

        @pl.when(kt == 0)
        def _():
            acc[...] = jnp.zeros_like(acc)

        acc[...] += dot_tn(a_ref[...], b_ref[...])

        @pl.when(kt == nk - 1)
        def _():
            for p in range(2):
                for q in range(2):
                    o_ref[p, q] = acc[pl.ds((2 * p + q) * lay.fh, lay.fh), :].astype(o_ref.dtype)

    return pl.pallas_call(
        body, name="weight_grad_tn", grid=(2, nk), out_shape=SDS(gb.shape, gb.dtype),
        in_specs=[pl.BlockSpec((tk, fc), lambda j, kt: (kt, j)), pl.BlockSpec((tk, d), lambda j, kt: (kt, 0)), ANY],
        out_specs=pl.BlockSpec((2, 2, lay.fh, d), lambda j, kt: (j, 0, blk, 0)),
        scratch_shapes=[pltpu.VMEM((fc, d), F32)],
        input_output_aliases={2: 0}, compiler_params=_cparams(),
    )(a, b, gb)


def _lane_blocks(v):
    return [v[:, j * LANE:(j + 1) * LANE] for j in range(v.shape[1] // LANE)]


def _join_lane_blocks(ref):
    return jnp.concatenate([ref[j] for j in range(ref.shape[0])], axis=1)


def _cbm_spec(nblk, rows, first=0):
    return pl.BlockSpec((nblk, rows, LANE), lambda i: (first // nblk, i, 0))


def mix_project(x, gain, wfull, lay, tm=512):
    s_len, d = x.shape
    tm = min(tm, s_len)
    d_in = 8 * lay.ih
    ncol = d_in // LANE

    def body(x_ref, g_ref, w_ref, o_ref):
        _, _, hn = _rms(x_ref[...], g_ref[...])
        res = dot_nt(hn.astype(BF), w_ref[:, :, :lay.ih, :].reshape(d_in, d))
        for j, piece in enumerate(_lane_blocks(res)):
            o_ref[j] = piece

    return pl.pallas_call(
        body, name="mix_project", grid=(s_len // tm,), out_shape=SDS((ncol, s_len, LANE), F32),
        in_specs=[pl.BlockSpec((tm, d), lambda i: (i, 0)), pl.BlockSpec((1, d), lambda i: (0, 0)),
                  _w_spec(lay.fh, d, lay.MIX_BLK)],
        out_specs=_cbm_spec(ncol, tm), compiler_params=_cparams(),
    )(x, gain, wfull)


def mix_project_backward(x, gain, dout, dxr, dgt, dq, dkv, dwout, wfull, gb, lay, tm=512):
    s_len, d = x.shape
    tm = min(tm, s_len)
    d_in = 8 * lay.ih
    nt = s_len // tm
    kvw = dkv.shape[1]

    def body(x_ref, g_ref, do_ref, dxr_ref, dgt_ref, dq_ref, dkv_ref, dwo_ref, w_ref, gb_ref, dx_ref, dg_ref, o_ref, acc):
        i = pl.program_id(0)
        g = g_ref[...]
        xh, rs, hn = _rms(x_ref[...], g)
        h = hn.astype(BF)
        dp = jnp.concatenate([_join_lane_blocks(dxr_ref), _join_lane_blocks(dgt_ref), _join_lane_blocks(dq_ref),
                              dkv_ref[...]], axis=1).astype(BF)
        dh = dot_nn(dp, w_ref[:, :, :lay.ih, :].reshape(d_in, d))
        dxn, dgrow = _rms_bwd(dh, xh, rs, g)
        dx_ref[...] = do_ref[...] + dxn

        @pl.when(i == 0)
        def _():
            dg_ref[...] = jnp.zeros_like(dg_ref)
            acc[...] = jnp.zeros_like(acc)

        dg_ref[...] += jnp.sum(dgrow, axis=0, keepdims=True)
        acc[...] += dot_tn(dp, h)

        @pl.when(i == nt - 1)
        def _():
            for p in range(N_CHIPS):
                for q in range(2):
                    o_ref[p, q, :lay.ih, :] = acc[pl.ds((2 * p + q) * lay.ih, lay.ih), :].astype(o_ref.dtype)
            o_ref[:, :, lay.ih:, :] = dwo_ref[...]

    row = pl.BlockSpec((tm, d), lambda i: (i, 0))
    vec = pl.BlockSpec((1, d), lambda i: (0, 0))
    return pl.pallas_call(
        body, name="mix_project_backward", grid=(nt,),
        out_shape=(SDS((s_len, d), F32), SDS((1, d), F32), SDS(gb.shape, gb.dtype)),
        in_specs=[row, vec, row, _cbm_spec(dxr.shape[0], tm), _cbm_spec(dgt.shape[0], tm),
                  _cbm_spec(dq.shape[0], tm), pl.BlockSpec((tm, kvw), lambda i: (i, 0)),
                  pl.BlockSpec(dwout.shape, lambda i: (0, 0, 0, 0)), _w_spec(lay.fh, d, lay.MIX_BLK), ANY],
        out_specs=(row, vec, pl.BlockSpec((N_CHIPS, 2, lay.fh, d), lambda i: (0, 0, lay.MIX_BLK, 0))),
        scratch_shapes=[pltpu.VMEM((d_in, d), F32)],
        input_output_aliases={9: 2}, compiler_params=_cparams(),
    )(x, gain, dout, dxr, dgt, dq, dkv, dwout, wfull, gb)


def _lru_gates(xc, wb_ref, pv_ref, direction):
    xcb = xc.astype(BF)
    r = jax.nn.sigmoid(dot_nn(xcb, wb_ref[2 * direction]) + pv_ref[1 + direction:2 + direction, :])
    i = jax.nn.sigmoid(dot_nn(xcb, wb_ref[2 * direction + 1]) + pv_ref[3 + direction:4 + direction, :])
    lam = pv_ref[5 + direction:6 + direction, :]
    sp = jnp.maximum(-lam, 0.0) + jnp.log(1.0 + jnp.exp(-jnp.abs(lam)))
    a = jnp.exp(-LRU_C * sp * r)
    mult = jnp.sqrt(1.0 - a * a)
    return xcb, r, i, a, mult, sp


def _conv_rows(xr, cv_ref, bias, n):
    acc = bias + cv_ref[0:1, :] * _shift_rows(xr, -2, n)
    for j in range(1, CONV_WIDTH):
        acc = acc + cv_ref[j:j + 1, :] * _shift_rows(xr, j - 2, n)
    return acc


def lru_forward(proj, cvec, pvec, wblk, lw, deps=(), ch=512):
    s_len = proj.shape[1]
    ncb = lw // LANE
    ch = min(ch, s_len)
    nchunk = s_len // ch

    def body(xr_ref, gt_ref, cv_ref, pv_ref, wb_ref, *rest):
        y_ref, hs_ref, xc_s, a_s, u_s, acum_s = rest[len(deps):]
        xc_s[...] = _conv_rows(xr_ref[...], cv_ref, pv_ref[0:1, :], s_len)
        for direction in range(2):
            def fill(ci, _):
                rows = pl.ds(pl.multiple_of(ci * ch, ch), ch)
                xc = xc_s[rows, :]
                _, _, i, a, mult, _ = _lru_gates(xc, wb_ref, pv_ref, direction)
                a_s[rows, :] = a
                u_s[rows, :] = mult * (i * xc)
                return 0

            lax.fori_loop(0, nchunk, fill, 0)
            _scan_rows(a_s, u_s, hs_ref.at[direction], acum_s, reverse=direction == 1)

        def out(ci, _):
            rows = pl.ds(pl.multiple_of(ci * ch, ch), ch)
            gl, _ = _gelu(gt_ref[rows, :])
            y_ref[rows, :] = gl * (hs_ref[0, rows, :] + hs_ref[1, rows, :])
            return 0

        lax.fori_loop(0, nchunk, out, 0)

    col = lambda off: pl.BlockSpec((None, s_len, LANE), lambda cb: (off + cb, 0, 0))
    return pl.pallas_call(
        body, name="lru_forward", grid=(ncb,),
        out_shape=(SDS((ncb, s_len, LANE), F32), SDS((2, ncb, s_len, LANE), F32)),
        in_specs=[col(0), col(ncb), pl.BlockSpec((8, LANE), lambda cb: (0, cb)), pl.BlockSpec((8, LANE), lambda cb: (0, cb)),
                  pl.BlockSpec((4, None, LANE, LANE), lambda cb: (0, cb, 0, 0))] + [ANY] * len(deps),
        out_specs=(col(0), pl.BlockSpec((2, None, s_len, LANE), lambda cb: (0, cb, 0, 0))),
        scratch_shapes=[pltpu.VMEM((s_len, LANE), F32)] * 4, compiler_params=_cparams(),
    )(proj, proj, cvec, pvec, wblk, *deps)


def lru_backward(proj, hs, dy, cvec, pvec, wblk, lw, ch=512):
    s_len = proj.shape[1]
    ncb = lw // LANE
    ch = min(ch, s_len)
    nchunk = s_len // ch

    def body(xr_ref, gt_ref, hs_ref, dy_ref, cv_ref, pv_ref, wb_ref, dxr_ref, dgt_ref, dcv_ref, dpv_ref, dwb_ref,
             xc_s, a_s, dh_s, lam_s, hp_s, dxc_s, acum_s):
        xr = xr_ref[...]
        xc_s[...] = _conv_rows(xr, cv_ref, pv_ref[0:1, :], s_len)
        dxc_s[...] = jnp.zeros_like(dxc_s)
        dpv_ref[...] = jnp.zeros_like(dpv_ref)
        dwb_ref[...] = jnp.zeros_like(dwb_ref)

        def head(ci, _):
            rows = pl.ds(pl.multiple_of(ci * ch, ch), ch)
            gt = gt_ref[rows, :]
            gl, t = _gelu(gt)
            dy = dy_ref[rows, :]
            dh_s[rows, :] = dy * gl
            dgt_ref[rows, :] = dy * (hs_ref[0, rows, :] + hs_ref[1, rows, :]) * _gelu_grad(gt, t)
            return 0

        lax.fori_loop(0, nchunk, head, 0)

        for direction in range(2):
            def fill(ci, _):
                rows = pl.ds(pl.multiple_of(ci * ch, ch), ch)
                _, _, _, a, _, _ = _lru_gates(xc_s[rows, :], wb_ref, pv_ref, direction)
                a_s[rows, :] = a
                return 0

            lax.fori_loop(0, nchunk, fill, 0)
            toward = 1 if direction == 0 else -1
            hp_s[...] = _shift_rows(a_s[...], toward, s_len)
            _scan_rows(hp_s, dh_s, lam_s, acum_s, reverse=direction == 0)
            hp_s[...] = _shift_rows(hs_ref[direction], -toward, s_len)

            def grads(ci, _):
                rows = pl.ds(pl.multiple_of(ci * ch, ch), ch)
                xc = xc_s[rows, :]
                xcb, r, i, a, mult, sp = _lru_gates(xc, wb_ref, pv_ref, direction)
                du = lam_s[rows, :]
                da = du * hp_s[rows, :]
                dmult = du * i * xc
                di = du * mult * xc
                dlog_a = (da - dmult * a / mult) * a
                dr = dlog_a * (-LRU_C * sp)
                dza = dr * r * (1.0 - r)
                dzx = di * i * (1.0 - i)
                dzab = dza.astype(BF)
                dzxb = dzx.astype(BF)
                dxc_s[rows, :] += (du * mult * i + dot_nt(dzab, wb_ref[2 * direction])
                                   + dot_nt(dzxb, wb_ref[2 * direction + 1]))
                dwb_ref[2 * direction] += dot_tn(xcb, dzab)
                dwb_ref[2 * direction + 1] += dot_tn(xcb, dzxb)
                dpv_ref[1 + direction:2 + direction, :] += jnp.sum(dza, axis=0, keepdims=True)
                dpv_ref[3 + direction:4 + direction, :] += jnp.sum(dzx, axis=0, keepdims=True)
                dpv_ref[5 + direction:6 + direction, :] += jnp.sum(dlog_a * (-LRU_C * r), axis=0, keepdims=True)
                return 0

            lax.fori_loop(0, nchunk, grads, 0)

        for direction in range(2):
            lam = pv_ref[5 + direction:6 + direction, :]
            dpv_ref[5 + direction:6 + direction, :] = dpv_ref[5 + direction:6 + direction, :] * (-jax.nn.sigmoid(-lam))
        dxc = dxc_s[...]
        dpv_ref[0:1, :] = jnp.sum(dxc, axis=0, keepdims=True)
        dxr = cv_ref[0:1, :] * _shift_rows(dxc, 2, s_len)
        for j in range(1, CONV_WIDTH):
            dxr = dxr + cv_ref[j:j + 1, :] * _shift_rows(dxc, 2 - j, s_len)
        dxr_ref[...] = dxr
        dcv_ref[...] = jnp.zeros_like(dcv_ref)
        for j in range(CONV_WIDTH):
            dcv_ref[j:j + 1, :] = jnp.sum(dxc * _shift_rows(xr, j - 2, s_len), axis=0, keepdims=True)

    col = lambda off: pl.BlockSpec((None, s_len, LANE), lambda cb: (off + cb, 0, 0))
    own = col(0)
    small = pl.BlockSpec((8, LANE), lambda cb: (0, cb))
    wspec = pl.BlockSpec((4, None, LANE, LANE), lambda cb: (0, cb, 0, 0))
    return pl.pallas_call(
        body, name="lru_backward", grid=(ncb,),
        out_shape=(SDS((ncb, s_len, LANE), F32), SDS((ncb, s_len, LANE), F32), SDS((8, lw), F32), SDS((8, lw), F32),
                   SDS(wblk.shape, F32)),
        in_specs=[col(0), col(ncb), pl.BlockSpec((2, None, s_len, LANE), lambda cb: (0, cb, 0, 0)), own, small, small, wspec],
        out_specs=(own, own, small, small, wspec),
        scratch_shapes=[pltpu.VMEM((s_len, LANE), F32)] * 7, compiler_params=_cparams(),
    )(proj, proj, hs, dy, cvec, pvec, wblk)


def _attn_specs(s_len, lw, att):
    nb = s_len // BLOCK
    kcol = (2 * lw + att) // LANE
    prev = lambda n: jnp.maximum(n - 1, 0)
    nxt = lambda n: jnp.minimum(n + 1, nb - 1)
    q = _cbm_spec(att // LANE, BLOCK, first=2 * lw // LANE)
    ks = [pl.BlockSpec((None, BLOCK, LANE), lambda n, f=f: (kcol, f(n), 0)) for f in (prev, lambda n: n, nxt)]
    vs = [pl.BlockSpec((None, BLOCK, LANE), lambda n, f=f: (kcol + 1, f(n), 0)) for f in (prev, lambda n: n, nxt)]
    return q, ks, vs


HEADS_PER_LANE_BLOCK = LANE // HEAD_DIM


def _stack_heads(v, kh):
    pieces = []
    for g in range(KV_GROUP):
        blk, sub = divmod(kh * KV_GROUP + g, HEADS_PER_LANE_BLOCK)
        pieces.append(v[blk][:, sub * HEAD_DIM:(sub + 1) * HEAD_DIM])
    return jnp.concatenate(pieces, axis=0)


def _unstack_heads(groups):
    heads = [grp[g * BLOCK:(g + 1) * BLOCK] for grp in groups for g in range(KV_GROUP)]
    return [jnp.concatenate(heads[b * HEADS_PER_LANE_BLOCK:(b + 1) * HEADS_PER_LANE_BLOCK], axis=1)
            for b in range(len(heads) // HEADS_PER_LANE_BLOCK)]


def _key_exists(n, nb):
    j = lax.broadcasted_iota(jnp.int32, (1, 3 * BLOCK), 1)
    return ((n > 0) | (j >= BLOCK)) & ((n < nb - 1) | (j < 2 * BLOCK))


def _attn_probs(qs, kcat, bias_g, sink_g, key_ok):
    logits = jnp.where(key_ok, dot_nt(qs, kcat) + bias_g, NEG_INF)
    m = jnp.maximum(jnp.max(logits, axis=-1, keepdims=True), sink_g)
    p = jnp.exp(logits - m)
    es = jnp.exp(sink_g - m)
    inv = 1.0 / (jnp.sum(p, axis=-1, keepdims=True) + es)
    return p * inv, es * inv


def attention_forward(proj, bias, sink, lw, att):
    s_len = proj.shape[1]
    nb = s_len // BLOCK
    q_spec, k_specs, v_specs = _attn_specs(s_len, lw, att)

    def body(q_ref, kp_ref, kc_ref, kn_ref, vp_ref, vc_ref, vn_ref, b_ref, s_ref, o_ref):
        n = pl.program_id(0)
        q = q_ref[...]
        key_ok = _key_exists(n, nb)
        kall = jnp.concatenate([kp_ref[...], kc_ref[...], kn_ref[...]], axis=0).astype(BF)
        vall = jnp.concatenate([vp_ref[...], vc_ref[...], vn_ref[...]], axis=0).astype(BF)
        outs = []
        for kh in range(N_KV_HEADS):
            grp = slice(kh * KV_GROUP * BLOCK, (kh + 1) * KV_GROUP * BLOCK)
            qs = (_stack_heads(q, kh) * (HEAD_DIM ** -0.5)).astype(BF)
            bias_g = b_ref[kh * KV_GROUP:(kh + 1) * KV_GROUP].reshape(KV_GROUP * BLOCK, 3 * BLOCK)
            p, _ = _attn_probs(qs, kall[:, kh * HEAD_DIM:(kh + 1) * HEAD_DIM], bias_g, s_ref[grp, 0:1], key_ok)
            outs.append(dot_nn(p.astype(BF), vall[:, kh * HEAD_DIM:(kh + 1) * HEAD_DIM]))
        for b, piece in enumerate(_unstack_heads(outs)):
            o_ref[b] = piece

    return pl.pallas_call(
        body, name="attention_forward", grid=(nb,), out_shape=SDS((att // LANE, s_len, LANE), F32),
        in_specs=[q_spec] + k_specs + v_specs
        + [pl.BlockSpec(bias.shape, lambda n: (0, 0, 0)), pl.BlockSpec(sink.shape, lambda n: (0, 0))],
        out_specs=_cbm_spec(att // LANE, BLOCK), compiler_params=_cparams(),
    )(proj, proj, proj, proj, proj, proj, proj, bias, sink)


def attention_backward(proj, y_att, dy, bias, sink, lw, att):
    s_len = proj.shape[1]
    nb = s_len // BLOCK
    kvw = N_KV_HEADS * HEAD_DIM
    q_spec, k_specs, v_specs = _attn_specs(s_len, lw, att)

    def body(q_ref, kp_ref, kc_ref, kn_ref, vp_ref, vc_ref, vn_ref, o_ref, do_ref, b_ref, s_ref,
             dq_ref, dkv_ref, db_ref, ds_ref):
        n = pl.program_id(0)

        @pl.when(n == 0)
        def _():
            dkv_ref[...] = jnp.zeros_like(dkv_ref)
            db_ref[...] = jnp.zeros_like(db_ref)
            ds_ref[...] = jnp.zeros_like(ds_ref)

        q = q_ref[...]
        o = o_ref[...]
        do = do_ref[...]
        kall = jnp.concatenate([kp_ref[...], kc_ref[...], kn_ref[...]], axis=0).astype(BF)
        vall = jnp.concatenate([vp_ref[...], vc_ref[...], vn_ref[...]], axis=0).astype(BF)
        key_ok = _key_exists(n, nb)
        dqs, dks, dvs = [], [], []
        for kh in range(N_KV_HEADS):
            heads = slice(kh * KV_GROUP, (kh + 1) * KV_GROUP)
            grp = slice(kh * KV_GROUP * BLOCK, (kh + 1) * KV_GROUP * BLOCK)
            kcat = kall[:, kh * HEAD_DIM:(kh + 1) * HEAD_DIM]
            vcat = vall[:, kh * HEAD_DIM:(kh + 1) * HEAD_DIM]
            qs = (_stack_heads(q, kh) * (HEAD_DIM ** -0.5)).astype(BF)
            bias_g = b_ref[heads].reshape(KV_GROUP * BLOCK, 3 * BLOCK)
            p, ps = _attn_probs(qs, kcat, bias_g, s_ref[grp, 0:1], key_ok)
            dos = _stack_heads(do, kh)
            dosb = dos.astype(BF)
            delta = jnp.sum(dos * _stack_heads(o, kh), axis=-1, keepdims=True)
            dlog = p * (dot_nt(dosb, vcat) - delta)
            dlogb = dlog.astype(BF)
            db_ref[heads] += dlog.reshape(KV_GROUP, BLOCK, 3 * BLOCK)
            dsink = -ps * delta
            for g in range(KV_GROUP):
                h = kh * KV_GROUP + g
                part = jnp.sum(dsink[g * BLOCK:(g + 1) * BLOCK], axis=0, keepdims=True)
                ds_ref[h:h + 1, :] += jnp.broadcast_to(part, (1, LANE))
            dqs.append(dot_nn(dlogb, kcat) * (HEAD_DIM ** -0.5))
            dks.append(dot_tn(dlogb, qs))
            dvs.append(dot_tn(p.astype(BF), dosb))
        for b, piece in enumerate(_unstack_heads(dqs)):
            dq_ref[b] = piece
        dkv = jnp.concatenate(dks + dvs, axis=1)
        starts = [jnp.maximum(n - 1, 0), n, jnp.minimum(n + 1, nb - 1)]
        for b, st in enumerate(starts):
            rows = pl.ds(pl.multiple_of(st * BLOCK, BLOCK), BLOCK)
            dkv_ref[rows, :] += dkv[b * BLOCK:(b + 1) * BLOCK, :]

    blk = _cbm_spec(att // LANE, BLOCK)
    return pl.pallas_call(
        body, name="attention_backward", grid=(nb,),
        out_shape=(SDS((att // LANE, s_len, LANE), F32), SDS((s_len, 2 * kvw), F32), SDS(bias.shape, F32),
                   SDS((N_HEADS, LANE), F32)),
        in_specs=[q_spec] + k_specs + v_specs
        + [blk, blk, pl.BlockSpec(bias.shape, lambda n: (0, 0, 0)), pl.BlockSpec(sink.shape, lambda n: (0, 0))],
        out_specs=(blk, pl.BlockSpec((s_len, 2 * kvw), lambda n: (0, 0)),
                   pl.BlockSpec(bias.shape, lambda n: (0, 0, 0)), pl.BlockSpec((N_HEADS, LANE), lambda n: (0, 0))),
        compiler_params=_cparams(),
    )(proj, proj, proj, proj, proj, proj, proj, y_att, dy, bias, sink)


def mix_output(x, y_rec, y_att, g_rec, g_att, wfull, lay, tm=512):
    s_len, d = x.shape
    tm = min(tm, s_len)
    lw = y_rec.shape[0] * LANE
    att = y_att.shape[0] * LANE

    def body(x_ref, yr_ref, ya_ref, gr_ref, ga_ref, w_ref, o_ref):
        _, _, nr = _rms(_join_lane_blocks(yr_ref), gr_ref[...])
        _, _, na = _rms(_join_lane_blocks(ya_ref), ga_ref[...])
        y = jnp.concatenate([nr, na], axis=1).astype(BF)
        o_ref[...] = x_ref[...] + dot_nn(y, w_ref[:, :, lay.ih:, :].reshape(d, d))

    row = pl.BlockSpec((tm, d), lambda i: (i, 0))
    return pl.pallas_call(
        body, name="mix_output", grid=(s_len // tm,), out_shape=SDS((s_len, d), F32),
        in_specs=[row, _cbm_spec(lw // LANE, tm), _cbm_spec(att // LANE, tm),
                  pl.BlockSpec((1, lw), lambda i: (0, 0)), pl.BlockSpec((1, att), lambda i: (0, 0)),
                  _w_spec(lay.fh, d, lay.MIX_BLK)],
        out_specs=row, compiler_params=_cparams(),
    )(x, y_rec, y_att, g_rec, g_att, wfull)


def mix_output_backward(dout, y_rec, y_att, g_rec, g_att, wfull, lay, tm=512):
    s_len, d = dout.shape
    tm = min(tm, s_len)
    lw = y_rec.shape[0] * LANE
    att = y_att.shape[0] * LANE
    nt = s_len // tm

    def body(do_ref, yr_ref, ya_ref, gr_ref, ga_ref, w_ref, dyr_ref, dya_ref, dgr_ref, dga_ref, o_ref, acc):
        i = pl.program_id(0)
        gr = gr_ref[...]
        ga = ga_ref[...]
        xhr, rsr, nr = _rms(_join_lane_blocks(yr_ref), gr)
        xha, rsa, na = _rms(_join_lane_blocks(ya_ref), ga)
        y = jnp.concatenate([nr, na], axis=1).astype(BF)
        dob = do_ref[...].astype(BF)
        dy = dot_nt(dob, w_ref[:, :, lay.ih:, :].reshape(d, d))
        dyr, dgr_row = _rms_bwd(dy[:, :lw], xhr, rsr, gr)
        dya, dga_row = _rms_bwd(dy[:, lw:], xha, rsa, ga)
        for j, piece in enumerate(_lane_blocks(dyr)):
            dyr_ref[j] = piece
        for j, piece in enumerate(_lane_blocks(dya)):
            dya_ref[j] = piece

        @pl.when(i == 0)
        def _():
            dgr_ref[...] = jnp.zeros_like(dgr_ref)
            dga_ref[...] = jnp.zeros_like(dga_ref)
            acc[...] = jnp.zeros_like(acc)

        dgr_ref[...] += jnp.sum(dgr_row, axis=0, keepdims=True)
        dga_ref[...] += jnp.sum(dga_row, axis=0, keepdims=True)
        acc[...] += dot_tn(y, dob)

        @pl.when(i == nt - 1)
        def _():
            for p in range(N_CHIPS):
                for q in range(2):
                    o_ref[p, q] = acc[pl.ds((2 * p + q) * lay.oh, lay.oh), :].astype(o_ref.dtype)

    row = pl.BlockSpec((tm, d), lambda i: (i, 0))
    return pl.pallas_call(
        body, name="mix_output_backward", grid=(nt,),
        out_shape=(SDS(y_rec.shape, F32), SDS(y_att.shape, F32), SDS((1, lw), F32), SDS((1, att), F32),
                   SDS((N_CHIPS, 2, lay.oh, d), BF)),
        in_specs=[row, _cbm_spec(lw // LANE, tm), _cbm_spec(att // LANE, tm),
                  pl.BlockSpec((1, lw), lambda i: (0, 0)), pl.BlockSpec((1, att), lambda i: (0, 0)),
                  _w_spec(lay.fh, d, lay.MIX_BLK)],
        out_specs=(_cbm_spec(lw // LANE, tm), _cbm_spec(att // LANE, tm),
                   pl.BlockSpec((1, lw), lambda i: (0, 0)), pl.BlockSpec((1, att), lambda i: (0, 0)),
                   pl.BlockSpec((N_CHIPS, 2, lay.oh, d), lambda i: (0, 0, 0, 0))),
        scratch_shapes=[pltpu.VMEM((d, d), F32)], compiler_params=_cparams(),
    )(dout, y_rec, y_att, g_rec, g_att, wfull)


def loss_head(x, gain, target, tm=512):
    s_len, d = x.shape
    tm = min(tm, s_len)

    def body(x_ref, g_ref, t_ref, dx_ref, dg_ref, loss_ref):
        g = g_ref[...]
        xh, rs, y = _rms(x_ref[...], g)
        err = y - t_ref[...]

        @pl.when(pl.program_id(0) == 0)
        def _():
            dg_ref[...] = jnp.zeros_like(dg_ref)
            loss_ref[...] = jnp.zeros_like(loss_ref)

        part = 0.5 * jnp.sum(jnp.mean(err * err, axis=-1, keepdims=True), axis=0, keepdims=True)
        loss_ref[...] += jnp.broadcast_to(part, loss_ref.shape)
        dx, dgrow = _rms_bwd(err * (1.0 / d), xh, rs, g)
        dx_ref[...] = dx
        dg_ref[...] += jnp.sum(dgrow, axis=0, keepdims=True)

    row = pl.BlockSpec((tm, d), lambda i: (i, 0))
    vec = pl.BlockSpec((1, d), lambda i: (0, 0))
    return pl.pallas_call(
        body, name="loss_head", grid=(s_len // tm,),
        out_shape=(SDS((s_len, d), F32), SDS((1, d), F32), SDS((8, LANE), F32)),
        in_specs=[row, vec, row], out_specs=(row, vec, pl.BlockSpec((8, LANE), lambda i: (0, 0))),
        compiler_params=_cparams(),
    )(x, gain, target)


def adamw(w, g, m, v, tr=512):
    rows, cols = w.shape
    tr = _row_chunk(rows, tr, 8)

    def body(w_ref, g_ref, m_ref, v_ref, d_ref, nm_ref, nv_ref):
        g = g_ref[...]
        m = ADAM_B1 * m_ref[...] + (1.0 - ADAM_B1) * g
        v = ADAM_B2 * v_ref[...] + (1.0 - ADAM_B2) * (g * g)
        m_hat = m / (1.0 - ADAM_B1 ** ADAM_STEP)
        v_hat = v / (1.0 - ADAM_B2 ** ADAM_STEP)
        d_ref[...] = -ADAM_LR * (m_hat / (jnp.sqrt(v_hat) + ADAM_EPS) + ADAM_WD * w_ref[...])
        nm_ref[...] = m
        nv_ref[...] = v

    blk = pl.BlockSpec((tr, cols), lambda i: (i, 0))
    return pl.pallas_call(
        body, name="adamw", grid=(rows // tr,), out_shape=(SDS(w.shape, F32),) * 3,
        in_specs=[blk] * 4, out_specs=(blk,) * 3, compiler_params=_cparams(),
    )(w, g, m, v)


def _pack_rows(arrays, width):
    flat = jnp.concatenate([a.reshape(-1).astype(F32) for a in arrays])
    rows = -(-flat.shape[0] // (8 * width)) * 8
    return jnp.pad(flat, (0, rows * width - flat.shape[0])).reshape(rows, width)


def _unpack_rows(buf, shapes):
    flat = buf.reshape(-1)
    out, off = [], 0
    for shp in shapes:
        n = int(np.prod(shp))
        out.append(flat[off:off + n].reshape(shp))
        off += n
    return out


def _t5_buckets(rel):
    half = N_BUCKETS // 2
    max_exact = half // 2
    ret = (rel > 0).astype(jnp.int32) * half
    n = jnp.abs(rel)
    n_f = jnp.maximum(n, 1).astype(F32)
    large = max_exact + (jnp.log(n_f / max_exact) / math.log(MAX_DISTANCE / max_exact) * (half - max_exact)).astype(jnp.int32)
    large = jnp.minimum(large, half - 1)
    return ret + jnp.where(n < max_exact, n, large)


def _band_buckets():
    t = jnp.arange(BLOCK)[:, None]
    j = jnp.arange(3 * BLOCK)[None, :]
    rel = j - BLOCK - t
    return _t5_buckets(rel), jnp.abs(rel) <= WINDOW


def _block_diag_pairs(w):
    depth, two, nblk, bw, _ = w.shape
    pairs = w.reshape(depth, two, nblk // 2, 2, bw, bw)
    z = jnp.zeros_like(pairs[:, :, :, 0])
    top = jnp.concatenate([pairs[:, :, :, 0], z], axis=-1)
    bot = jnp.concatenate([z, pairs[:, :, :, 1]], axis=-1)
    return jnp.concatenate([top, bot], axis=-2)


def _diag_blocks(dw):
    bw = dw.shape[-1] // 2
    a = dw[:, :, :bw, :bw]
    b = dw[:, :, bw:, bw:]
    return jnp.stack([a, b], axis=2).reshape(dw.shape[0], 2 * dw.shape[1], bw, bw)


def _pack_layer_shard(lay, w_in_l, w_out_l, mats_l):
    halves = []
    for hf in range(2):
        parts = []
        for m, a in enumerate(mats_l):
            a = a if m % 3 == 2 else a.T
            parts.append(a[hf * lay.fh:(hf + 1) * lay.fh])
        parts.append(w_in_l.T[hf * lay.ih:(hf + 1) * lay.ih])
        parts.append(w_out_l[hf * lay.oh:(hf + 1) * lay.oh])
        halves.append(jnp.concatenate(parts, axis=0))
    return jnp.stack(halves).astype(BF)


def kernel(x, ffn1_norm, ffn1_w_gate, ffn1_w_up, ffn1_w_down, mix_norm, w_in, conv_w, conv_b, lru_w_a, lru_b_a, lru_w_x, lru_b_x, lru_lambda, attn_sink, rel_bias, lru_out_norm, attn_out_norm, w_out, ffn2_norm, ffn2_w_gate, ffn2_w_up, ffn2_w_down, final_norm, loss_target, m_ffn1_norm, m_ffn1_w_gate, m_ffn1_w_up, m_ffn1_w_down, m_mix_norm, m_w_in, m_conv_w, m_conv_b, m_lru_w_a, m_lru_b_a, m_lru_w_x, m_lru_b_x, m_lru_lambda, m_attn_sink, m_rel_bias, m_lru_out_norm, m_attn_out_norm, m_w_out, m_ffn2_norm, m_ffn2_w_gate, m_ffn2_w_up, m_ffn2_w_down, m_final_norm, v_ffn1_norm, v_ffn1_w_gate, v_ffn1_w_up, v_ffn1_w_down, v_mix_norm, v_w_in, v_conv_w, v_conv_b, v_lru_w_a, v_lru_b_a, v_lru_w_x, v_lru_b_x, v_lru_lambda, v_attn_sink, v_rel_bias, v_lru_out_norm, v_attn_out_norm, v_w_out, v_ffn2_norm, v_ffn2_w_gate, v_ffn2_w_up, v_ffn2_w_down, v_final_norm):
    depth, d = ffn1_norm.shape
    d_ff = N_CHIPS * ffn1_w_gate.shape[2]
    d_in = N_CHIPS * w_in.shape[2]
    lw = conv_b.shape[1]
    att = N_HEADS * HEAD_DIM
    lay = Layout(d, d_ff, d_in)
    k_chip = 2 * lax.axis_index("x") + lax.axis_index("y")
    pos = jnp.stack([k_chip, lax.axis_index("c")]).astype(jnp.int32)

    mats = (ffn1_w_gate, ffn1_w_up, ffn1_w_down, ffn2_w_gate, ffn2_w_up, ffn2_w_down)
    lands = [lax.dynamic_update_slice(lax.empty((N_CHIPS, 2, lay.rows, d), BF),
                                      _pack_layer_shard(lay, w_in[l], w_out[l], [m[l] for m in mats])[None], (k_chip, 0, 0, 0))
             for l in range(depth)]

    def gather_start(l):
        return split_start(f"gather_start_{l}", [lands[l]], 3, gather_plan)

    def gather_finish(l, started, after):
        ssem, rsem, bufs, _ = started
        land, = split_wait(f"gather_wait_{l}", ssem, rsem, bufs, after, gather_plan)
        return gather_pair(land)

    sharded_small = (conv_w, lru_b_a, lru_b_x, lru_lambda)
    sshard = jnp.concatenate([a.reshape(-1, LANE) for a in sharded_small], axis=0)
    sfull = gather_small(sshard)
    small_full, off = [], 0
    for a in sharded_small:
        r = a.shape[0] * a.shape[1]
        piece = sfull[:, off:off + r].reshape((N_CHIPS,) + a.shape)
        small_full.append(jnp.moveaxis(piece, 0, 2).reshape(a.shape[0], a.shape[1], N_CHIPS * LANE))
        off += r
    conv_w_f, b_a_f, b_x_f, lam_f = small_full

    zrow = jnp.zeros((1, lw), F32)
    wblk_a = _block_diag_pairs(lru_w_a)
    wblk_x = _block_diag_pairs(lru_w_x)
    buckets, in_band = _band_buckets()
    onehot = (buckets.reshape(-1)[:, None] == jnp.arange(N_BUCKETS)[None, :]).astype(F32)
    bias = jnp.dot(rel_bias.T, onehot.T, precision=lax.Precision.HIGHEST).reshape(N_HEADS, BLOCK, 3 * BLOCK)
    bias = jnp.where(in_band[None], bias, NEG_INF)

    def layer_small(l):
        cvec = jnp.concatenate([conv_w_f[l], jnp.zeros((8 - CONV_WIDTH, lw), F32)], axis=0)
        pvec = jnp.concatenate([conv_b[l][None], b_a_f[l], b_x_f[l], lam_f[l], zrow], axis=0)
        wblk = jnp.stack([wblk_a[l, 0], wblk_x[l, 0], wblk_a[l, 1], wblk_x[l, 1]]).astype(BF)
        sink = jnp.broadcast_to(attn_sink[l][:, None, None], (N_HEADS, BLOCK, LANE)).reshape(N_HEADS * BLOCK, LANE)
        return cvec, pvec, wblk, sink

    xs = x[0]
    wfull = [None] * depth
    prep = lands[1:] + [xs, bias, wblk_a, wblk_x]
    wfull[0] = gather_finish(0, gather_start(0), prep)
    started = gather_start(1) if depth > 1 else None
    saved = []
    for l in range(depth):
        cvec, pvec, wblk, sink = layer_small(l)
        deps = (started[3],) if started is not None else ()
        x1, gate1, up1 = ffn_forward(xs, ffn1_norm[l][None], wfull[l], lay, 0, deps=deps)
        proj = mix_project(x1, mix_norm[l][None], wfull[l], lay)
        y_rec, hs = lru_forward(proj, cvec, pvec, wblk, lw)
        y_att = attention_forward(proj, bias, sink, lw, att)
        x2 = mix_output(x1, y_rec, y_att, lru_out_norm[l][None], attn_out_norm[l][None], wfull[l], lay)
        x3, gate2, up2 = ffn_forward(x2, ffn2_norm[l][None], wfull[l], lay, 1)
        saved.append((xs, x1, x2, proj, y_rec, hs, y_att, (gate1, up1), (gate2, up2)))
        xs = x3
        if l + 1 < depth:
            wfull[l + 1] = gather_finish(l + 1, started, [x3])
            started = gather_start(l + 2) if l + 2 < depth else None

    dx, d_final, loss_tile = loss_head(xs, final_norm[None], loss_target[0])
    loss = lax.psum(loss_tile[0, 0], ("x", "y", "c"))

    layer_names = ["ffn1_norm", "mix_norm", "conv_w", "conv_b", "lru_w_a", "lru_b_a", "lru_w_x", "lru_b_x", "lru_lambda",
                   "attn_sink", "lru_out_norm", "attn_out_norm", "ffn2_norm"]
    dbias_total = jnp.zeros(bias.shape, F32)

    def ffn_back(xin, gain, dout, pre, gb, l, which, deps=()):
        dxo, dg, dgate, dup, act, h, df = ffn_backward_dx(xin, gain, dout, *pre, wfull[l], lay, which, deps=deps)
        gb = weight_grad_tn(dgate, h, gb, lay, 3 * which + 0)
        gb = weight_grad_tn(dup, h, gb, lay, 3 * which + 1)
        gb = weight_grad_tn(act, df, gb, lay, 3 * which + 2)
        return dxo, dg[0], gb

    def reduce_start(l, gb, sb):
        p1, sp1 = exchange_pair(gb, sb)
        cs = pair_sum(pos, gb, p1)
        ss = small_pair_sum(sb, sp1)
        lands = [lax.empty((3,) + cs.shape[1:], cs.dtype), lax.empty((N_CHIPS,) + ss.shape, ss.dtype)]
        return split_start(f"reduce_start_{l}", [cs, ss] + lands, 6, reduce_plan)

    def reduce_finish(l, started, after):
        ssem, rsem, bufs, _ = started
        cs, ss, p3, sp3 = split_wait(f"reduce_wait_{l}", ssem, rsem, bufs, after, reduce_plan)
        return exchange_final(chip_sum(pos, cs, p3)), small_chip_sum(pos, ss, sp3)

    gf = [None] * depth
    small_sums = [None] * depth
    small_shapes = [None] * depth
    in_flight = None
    for l in reversed(range(depth)):
        x0, x1, x2, proj, y_rec, hs, y_att, pre1, pre2 = saved[l]
        cvec, pvec, wblk, sink = layer_small(l)
        gb = lax.empty((N_CHIPS, 2, lay.rows, d), BF)
        part = {}
        deps = (in_flight[1][3],) if in_flight is not None else ()
        dx, part["ffn2_norm"], gb = ffn_back(x2, ffn2_norm[l][None], dx, pre2, gb, l, 1, deps=deps)
        dyr, dya, dgr, dga, dwout = mix_output_backward(dx, y_rec, y_att, lru_out_norm[l][None], attn_out_norm[l][None],
                                                        wfull[l], lay)
        part["lru_out_norm"] = dgr[0]
        part["attn_out_norm"] = dga[0]
        dq, dkv, dbias, dsink = attention_backward(proj, y_att, dya, bias, sink, lw, att)
        dbias_total = dbias_total + dbias
        part["attn_sink"] = dsink[:, 0]
        dxr, dgt, dcv, dpv, dwb = lru_backward(proj, hs, dyr, cvec, pvec, wblk, lw)
        part["conv_w"] = dcv[:CONV_WIDTH]
        part["conv_b"] = dpv[0]
        part["lru_b_a"] = dpv[1:3]
        part["lru_b_x"] = dpv[3:5]
        part["lru_lambda"] = dpv[5:7]
        part["lru_w_a"] = _diag_blocks(jnp.stack([dwb[0], dwb[2]]))
        part["lru_w_x"] = _diag_blocks(jnp.stack([dwb[1], dwb[3]]))
        dx, dgm, gb = mix_project_backward(x1, mix_norm[l][None], dx, dxr, dgt, dq, dkv, dwout, wfull[l], gb, lay)
        part["mix_norm"] = dgm[0]
        dx, part["ffn1_norm"], gb = ffn_back(x0, ffn1_norm[l][None], dx, pre1, gb, l, 0)
        pieces = [part[n] for n in layer_names]
        if l == 0:
            d_rel_bias = jnp.dot(dbias_total.reshape(N_HEADS, -1), onehot, precision=lax.Precision.HIGHEST).T
            pieces += [d_rel_bias, d_final[0]]
        small_shapes[l] = [p.shape for p in pieces]
        if in_flight is not None:
            gf[in_flight[0]], small_sums[in_flight[0]] = reduce_finish(in_flight[0], in_flight[1], [dx])
        in_flight = (l, reduce_start(l, gb, _pack_rows(pieces, 1024)))
    grad_x = dx[None]

    ffn_names = ("ffn1_w_gate", "ffn1_w_up", "ffn1_w_down", "ffn2_w_gate", "ffn2_w_up", "ffn2_w_down")

    def natural(l):
        def from_halves(row0, n, transpose):
            a = jnp.concatenate([gf[l][0, row0:row0 + n], gf[l][1, row0:row0 + n]], axis=0)
            return a.T if transpose else a

        out = {name: from_halves(m * lay.fh, lay.fh, m % 3 != 2) for m, name in enumerate(ffn_names)}
        out["w_in"] = from_halves(6 * lay.fh, lay.ih, True)
        out["w_out"] = from_halves(6 * lay.fh + lay.ih, lay.oh, False)
        return out

    nat = {l: natural(l) for l in range(depth) if l != in_flight[0]}
    ready = [a for l in sorted(nat) for a in nat[l].values()]
    gf[in_flight[0]], small_sums[in_flight[0]] = reduce_finish(in_flight[0], in_flight[1], [dx] + ready)
    nat[in_flight[0]] = natural(in_flight[0])

    per_layer = [_unpack_rows(small_sums[l], small_shapes[l]) for l in range(depth)]
    grads = {n: jnp.stack([per_layer[l][i] for l in range(depth)]) for i, n in enumerate(layer_names)}
    grads["rel_bias"], grads["final_norm"] = per_layer[0][len(layer_names):]
    for name in ("conv_w", "lru_b_a", "lru_b_x", "lru_lambda"):
        grads[name] = lax.dynamic_slice_in_dim(grads[name], k_chip * LANE, LANE, axis=2)
    for name in ffn_names + ("w_in", "w_out"):
        grads[name] = jnp.stack([nat[l][name] for l in range(depth)])

    weights = dict(ffn1_norm=ffn1_norm, ffn1_w_gate=ffn1_w_gate, ffn1_w_up=ffn1_w_up, ffn1_w_down=ffn1_w_down, mix_norm=mix_norm, w_in=w_in, conv_w=conv_w, conv_b=conv_b, lru_w_a=lru_w_a, lru_b_a=lru_b_a, lru_w_x=lru_w_x, lru_b_x=lru_b_x, lru_lambda=lru_lambda, attn_sink=attn_sink, rel_bias=rel_bias, lru_out_norm=lru_out_norm, attn_out_norm=attn_out_norm, w_out=w_out, ffn2_norm=ffn2_norm, ffn2_w_gate=ffn2_w_gate, ffn2_w_up=ffn2_w_up, ffn2_w_down=ffn2_w_down, final_norm=final_norm)
    m_in = dict(ffn1_norm=m_ffn1_norm, ffn1_w_gate=m_ffn1_w_gate, ffn1_w_up=m_ffn1_w_up, ffn1_w_down=m_ffn1_w_down, mix_norm=m_mix_norm, w_in=m_w_in, conv_w=m_conv_w, conv_b=m_conv_b, lru_w_a=m_lru_w_a, lru_b_a=m_lru_b_a, lru_w_x=m_lru_w_x, lru_b_x=m_lru_b_x, lru_lambda=m_lru_lambda, attn_sink=m_attn_sink, rel_bias=m_rel_bias, lru_out_norm=m_lru_out_norm, attn_out_norm=m_attn_out_norm, w_out=m_w_out, ffn2_norm=m_ffn2_norm, ffn2_w_gate=m_ffn2_w_gate, ffn2_w_up=m_ffn2_w_up, ffn2_w_down=m_ffn2_w_down, final_norm=m_final_norm)
    v_in = dict(ffn1_norm=v_ffn1_norm, ffn1_w_gate=v_ffn1_w_gate, ffn1_w_up=v_ffn1_w_up, ffn1_w_down=v_ffn1_w_down, mix_norm=v_mix_norm, w_in=v_w_in, conv_w=v_conv_w, conv_b=v_conv_b, lru_w_a=v_lru_w_a, lru_b_a=v_lru_b_a, lru_w_x=v_lru_w_x, lru_b_x=v_lru_b_x, lru_lambda=v_lru_lambda, attn_sink=v_attn_sink, rel_bias=v_rel_bias, lru_out_norm=v_lru_out_norm, attn_out_norm=v_attn_out_norm, w_out=v_w_out, ffn2_norm=v_ffn2_norm, ffn2_w_gate=v_ffn2_w_gate, ffn2_w_up=v_ffn2_w_up, ffn2_w_down=v_ffn2_w_down, final_norm=v_final_norm)
    order = list(weights)
    large = ("ffn1_w_gate", "ffn1_w_up", "ffn1_w_down", "w_in", "w_out", "ffn2_w_gate", "ffn2_w_up", "ffn2_w_down")
    delta, new_m, new_v = {}, {}, {}
    for name in large:
        shp = weights[name].shape
        two_d = lambda a: a.reshape(-1, shp[-1])
        dl, nm, nv = adamw(two_d(weights[name]), two_d(grads[name]), two_d(m_in[name]), two_d(v_in[name]))
        delta[name], new_m[name], new_v[name] = dl.reshape(shp), nm.reshape(shp), nv.reshape(shp)
    small = [n for n in order if n not in large]
    packed = [_pack_rows([src[n] for n in small], 1024) for src in (weights, grads, m_in, v_in)]
    outs = adamw(*packed)
    shapes = [weights[n].shape for n in small]
    for dst, buf in zip((delta, new_m, new_v), outs):
        dst.update(zip(small, _unpack_rows(buf, shapes)))

    return (loss, grad_x, *[grads[n] for n in order], *[delta[n] for n in order],
            *[new_m[n] for n in order], *[new_v[n] for n in order])
```

```python
import functools
import math

import jax
import jax.numpy as jnp
import numpy as np
from jax import lax
from jax.experimental import pallas as pl
from jax.experimental.pallas import tpu as pltpu

BF = jnp.bfloat16
F32 = jnp.float32
SDS = jax.ShapeDtypeStruct
MESH = pl.DeviceIdType.MESH
ANY = pl.BlockSpec(memory_space=pl.ANY)

N_CHIPS = 4
N_HEADS = 8
N_KV_HEADS = 2
KV_GROUP = N_HEADS // N_KV_HEADS
HEAD_DIM = 64
BLOCK = 128
WINDOW = 128
N_BUCKETS = 32
MAX_DISTANCE = 128
LRU_C = 8.0
CONV_WIDTH = 4
LANE = 128
SCAN_SEGMENTS = 8
SCAN_CHAINS = 8
EPS = 1e-6
FFN_RES = 0.5
NEG_INF = -1e30
ADAM_LR = 0.001
ADAM_B1 = 0.9
ADAM_B2 = 0.999
ADAM_EPS = 1e-08
ADAM_WD = 0.01
ADAM_STEP = 10
VMEM_LIMIT = 60000 * 1024
GELU_C = math.sqrt(2.0 / math.pi)


def dot_nn(a, b):
    return lax.dot_general(a, b, (((1,), (0,)), ((), ())), preferred_element_type=F32)


def dot_nt(a, b):
    return lax.dot_general(a, b, (((1,), (1,)), ((), ())), preferred_element_type=F32)


def dot_tn(a, b):
    return lax.dot_general(a, b, (((0,), (0,)), ((), ())), preferred_element_type=F32)


def _cparams(**kw):
    return pltpu.CompilerParams(vmem_limit_bytes=VMEM_LIMIT, **kw)


class Layout:
    MIX_BLK = 6

    def __init__(self, d_model, d_ff, d_in):
        self.fh = d_ff // (2 * N_CHIPS)
        self.ih = d_in // (2 * N_CHIPS)
        self.oh = d_model // (2 * N_CHIPS)
        assert self.ih + self.oh == self.fh, "w_in^T and w_out rows must fill one ffn-sized block"
        self.rows = 7 * self.fh


def _row_chunk(rows, target, step=16):
    best = rows
    for c in range(step, min(rows, target) + 1, step):
        if rows % c == 0:
            best = c
    return best


def _mesh_pos():
    return lax.axis_index("x"), lax.axis_index("y"), lax.axis_index("c")


def _rcopy(src, dst, ssem, rsem, dev):
    return pltpu.make_async_remote_copy(src_ref=src, dst_ref=dst, send_sem=ssem, recv_sem=rsem,
                                        device_id=dev, device_id_type=MESH)


HBM = pl.BlockSpec(memory_space=pltpu.HBM)
SEM = pl.BlockSpec(memory_space=pltpu.SEMAPHORE)
DATAFLOW = pltpu.SideEffectType.DATAFLOW_SIDE_EFFECTING


def _chip_peers():
    x, y, c = _mesh_pos()
    peers = [(1 - x, y), (x, 1 - y), (1 - x, 1 - y)]
    return x, y, c, 2 * x + y, [(px, py, 2 * px + py) for px, py in peers]


def split_start(name, bufs, n, plan):
    nb = len(bufs)

    def body(*refs):
        sends, _ = plan(refs[:nb], refs[nb], refs[nb + 1])
        for cp in sends:
            cp.start()
        refs[-1][...] = jnp.zeros_like(refs[-1])

    out = pl.pallas_call(
        body, name=name,
        out_shape=(pltpu.SemaphoreType.DMA((n,)), pltpu.SemaphoreType.DMA((n,)),
                   *[pltpu.HBM(b.shape, b.dtype) for b in bufs], SDS((8, LANE), F32)),
        in_specs=[HBM] * nb, out_specs=(SEM, SEM, *([HBM] * nb), pl.BlockSpec(memory_space=pltpu.VMEM)),
        input_output_aliases={i: 2 + i for i in range(nb)},
        compiler_params=pltpu.CompilerParams(has_side_effects=DATAFLOW),
    )(*[pltpu.with_memory_space_constraint(b, pltpu.HBM) for b in bufs])
    return out[0], out[1], list(out[2:2 + nb]), out[-1]


def split_wait(name, ssem, rsem, bufs, after, plan):
    nb = len(bufs)

    def body(*refs):
        sends, recvs = plan(refs[:nb], refs[nb], refs[nb + 1])
        for cp in recvs:
            cp.wait_recv()
        for cp in sends:
            cp.wait_send()

    out = pl.pallas_call(
        body, name=name, out_shape=tuple(pltpu.HBM(b.shape, b.dtype) for b in bufs),
        in_specs=[HBM] * nb + [SEM, SEM] + [ANY] * len(after), out_specs=tuple([HBM] * nb),
        input_output_aliases={i: i for i in range(nb)},
        compiler_params=pltpu.CompilerParams(has_side_effects=DATAFLOW),
    )(*bufs, ssem, rsem, *after)
    return list(out)


def gather_plan(refs, ssem, rsem):
    land_ref, = refs
    _, _, c, k, peers = _chip_peers()
    sends = [_rcopy(land_ref.at[k, c], land_ref.at[k, c], ssem.at[j], rsem.at[j], (px, py, c))
             for j, (px, py, _) in enumerate(peers)]
    recvs = [_rcopy(land_ref.at[kp, c], land_ref.at[kp, c], ssem.at[j], rsem.at[j], (px, py, c))
             for j, (px, py, kp) in enumerate(peers)]
    return sends, recvs


def reduce_plan(refs, ssem, rsem):
    cs_ref, ss_ref, p3_ref, sp3_ref = refs
    _, _, c, k, peers = _chip_peers()
    sends, recvs = [], []
    for j, (px, py, kp) in enumerate(peers):
        sends.append(_rcopy(cs_ref.at[kp], p3_ref.at[j], ssem.at[j], rsem.at[j], (px, py, c)))
        recvs.append(_rcopy(cs_ref.at[kp], p3_ref.at[j], ssem.at[j], rsem.at[j], (px, py, c)))
        sends.append(_rcopy(ss_ref, sp3_ref.at[k], ssem.at[3 + j], rsem.at[3 + j], (px, py, c)))
        recvs.append(_rcopy(ss_ref, sp3_ref.at[kp], ssem.at[3 + j], rsem.at[3 + j], (px, py, c)))
    return sends, recvs


def gather_small(sshard):
    def body(s_ref, sf_ref, lsem, ssem, rsem):
        _, _, c, k, peers = _chip_peers()
        own = pltpu.make_async_copy(s_ref, sf_ref.at[k], lsem)
        own.start()
        sends = [_rcopy(s_ref, sf_ref.at[k], ssem.at[j], rsem.at[j], (px, py, c)) for j, (px, py, _) in enumerate(peers)]
        recvs = [_rcopy(s_ref, sf_ref.at[kp], ssem.at[j], rsem.at[j], (px, py, c)) for j, (px, py, kp) in enumerate(peers)]
        for cp in sends:
            cp.start()
        for cp in recvs:
            cp.wait_recv()
        for cp in sends:
            cp.wait_send()
        own.wait()

    return pl.pallas_call(
        body, name="gather_small", out_shape=SDS((N_CHIPS,) + sshard.shape, sshard.dtype),
        in_specs=[ANY], out_specs=ANY,
        scratch_shapes=[pltpu.SemaphoreType.DMA, pltpu.SemaphoreType.DMA((3,)), pltpu.SemaphoreType.DMA((3,))],
    )(sshard)


def gather_pair(land):
    def body(land_in, land_ref, ssem, rsem):
        x, y, c, k, peers = _chip_peers()
        sib = (x, y, 1 - c)
        sends = [_rcopy(land_ref.at[kp, c], land_ref.at[kp, c], ssem.at[j], rsem.at[j], sib) for j, (_, _, kp) in enumerate(peers)]
        recvs = [_rcopy(land_ref.at[kp, 1 - c], land_ref.at[kp, 1 - c], ssem.at[j], rsem.at[j], sib)
                 for j, (_, _, kp) in enumerate(peers)]
        for cp in sends:
            cp.start()
        for cp in recvs:
            cp.wait_recv()
        for cp in sends:
            cp.wait_send()

    return pl.pallas_call(
        body, name="gather_pair", out_shape=SDS(land.shape, land.dtype),
        in_specs=[ANY], out_specs=ANY, input_output_aliases={0: 0},
        scratch_shapes=[pltpu.SemaphoreType.DMA((3,)), pltpu.SemaphoreType.DMA((3,))],
    )(land)


def exchange_pair(gb, sb):
    n, _, rh, d = gb.shape

    def body(gb_ref, sb_ref, p_ref, sp_ref, ssem, rsem):
        x, y, c = _mesh_pos()
        sib = (x, y, 1 - c)
        sends = [_rcopy(gb_ref.at[kk, 1 - c], p_ref.at[kk], ssem.at[kk], rsem.at[kk], sib) for kk in range(n)]
        sends.append(_rcopy(sb_ref, sp_ref, ssem.at[n], rsem.at[n], sib))
        for cp in sends:
            cp.start()
        for cp in sends:
            cp.wait_recv()
        for cp in sends:
            cp.wait_send()

    return pl.pallas_call(
        body, name="exchange_pair",
        out_shape=(SDS((n, rh, d), gb.dtype), SDS(sb.shape, sb.dtype)),
        in_specs=[ANY, ANY], out_specs=(ANY, ANY),
        scratch_shapes=[pltpu.SemaphoreType.DMA((n + 1,)), pltpu.SemaphoreType.DMA((n + 1,))],
    )(gb, sb)


def exchange_final(gf):
    _, rh, d = gf.shape
    nch = 4 if rh % 32 == 0 else 1
    cr = rh // nch

    def body(gf_ref, out_ref, ssem, rsem):
        x, y, c = _mesh_pos()
        sib = (x, y, 1 - c)
        sends = [_rcopy(out_ref.at[c, pl.ds(q * cr, cr)], out_ref.at[c, pl.ds(q * cr, cr)], ssem.at[q], rsem.at[q], sib)
                 for q in range(nch)]
        recvs = [_rcopy(out_ref.at[1 - c, pl.ds(q * cr, cr)], out_ref.at[1 - c, pl.ds(q * cr, cr)], ssem.at[q], rsem.at[q], sib)
                 for q in range(nch)]
        for cp in sends:
            cp.start()
        for cp in recvs:
            cp.wait_recv()
        for cp in sends:
            cp.wait_send()

    return pl.pallas_call(
        body, name="exchange_final",
        out_shape=SDS(gf.shape, gf.dtype),
        in_specs=[ANY], out_specs=ANY, input_output_aliases={0: 0},
        scratch_shapes=[pltpu.SemaphoreType.DMA((nch,)), pltpu.SemaphoreType.DMA((nch,))],
    )(gf)


def pair_sum(pos, gb, p1):
    n, _, rh, d = gb.shape
    cr = _row_chunk(rh, 1024)

    def body(pos_ref, a_ref, b_ref, o_ref):
        o_ref[...] = (a_ref[...].astype(F32) + b_ref[...].astype(F32)).astype(o_ref.dtype)

    return pl.pallas_call(
        body, name="pair_sum", out_shape=SDS((n, rh, d), gb.dtype),
        grid_spec=pltpu.PrefetchScalarGridSpec(
            num_scalar_prefetch=1, grid=(n, rh // cr),
            in_specs=[pl.BlockSpec((None, None, cr, d), lambda kk, r, pos: (kk, pos[1], r, 0)),
                      pl.BlockSpec((None, cr, d), lambda kk, r, pos: (kk, r, 0))],
            out_specs=pl.BlockSpec((None, cr, d), lambda kk, r, pos: (kk, r, 0))),
        compiler_params=_cparams(),
    )(pos, gb, p1)


def chip_sum(pos, cs, p3):
    n, rh, d = cs.shape
    cr = _row_chunk(rh, 512)

    def body(pos_ref, a_ref, b_ref, o_ref):
        acc = a_ref[...].astype(F32)
        for j in range(3):
            acc = acc + b_ref[j].astype(F32)
        o_ref[...] = acc

    return pl.pallas_call(
        body, name="chip_sum", out_shape=SDS((2, rh, d), F32),
        grid_spec=pltpu.PrefetchScalarGridSpec(
            num_scalar_prefetch=1, grid=(rh // cr,),
            in_specs=[pl.BlockSpec((None, cr, d), lambda r, pos: (pos[0], r, 0)),
                      pl.BlockSpec((3, cr, d), lambda r, pos: (0, r, 0))],
            out_specs=pl.BlockSpec((None, cr, d), lambda r, pos: (pos[1], r, 0))),
        compiler_params=_cparams(),
    )(pos, cs, p3)


def small_pair_sum(a, b):
    def body(a_ref, b_ref, o_ref):
        o_ref[...] = a_ref[...] + b_ref[...]

    return pl.pallas_call(body, name="small_pair_sum", out_shape=SDS(a.shape, a.dtype),
                          compiler_params=_cparams())(a, b)


def small_chip_sum(pos, own, p):
    ns, w = own.shape

    def body(pos_ref, own_ref, p0, p1, p2, p3, o_ref):
        k = pos_ref[0]
        acc = None
        for chip, ref in enumerate((p0, p1, p2, p3)):
            term = jnp.where(k == chip, own_ref[...], ref[...])
            acc = term if acc is None else acc + term
        o_ref[...] = acc

    def slot(chip):
        return pl.BlockSpec((None, ns, w), lambda i, pos: (jnp.where(pos[0] == chip, (chip + 1) % N_CHIPS, chip), 0, 0))

    return pl.pallas_call(
        body, name="small_chip_sum", out_shape=SDS(own.shape, own.dtype),
        grid_spec=pltpu.PrefetchScalarGridSpec(
            num_scalar_prefetch=1, grid=(1,),
            in_specs=[pl.BlockSpec((ns, w), lambda i, pos: (0, 0))] + [slot(chip) for chip in range(N_CHIPS)],
            out_specs=pl.BlockSpec((ns, w), lambda i, pos: (0, 0))),
        compiler_params=_cparams(),
    )(pos, own, p, p, p, p)


def _rms(x, g):
    rs = lax.rsqrt(jnp.mean(x * x, axis=-1, keepdims=True) + EPS)
    xh = x * rs
    return xh, rs, xh * g


def _rms_bwd(dy, xh, rs, g):
    dxh = dy * g
    dx = rs * (dxh - xh * jnp.mean(dxh * xh, axis=-1, keepdims=True))
    return dx, dy * xh


def _gelu(x):
    t = jnp.tanh(GELU_C * (x + 0.044715 * x * x * x))
    return 0.5 * x * (1.0 + t), t


def _gelu_grad(x, t):
    return 0.5 * (1.0 + t) + 0.5 * x * (1.0 - t * t) * GELU_C * (1.0 + 3.0 * 0.044715 * x * x)


def _shift_rows(v, s, n):
    if s == 0:
        return v
    t = lax.broadcasted_iota(jnp.int32, v.shape, 0)
    rolled = pltpu.roll(v, (-s) % n, 0)
    return jnp.where((t + s >= 0) & (t + s < n), rolled, 0.0)


def _scan_rows(a_ref, u_ref, h_ref, acum_ref, reverse):
    s_len, w = a_ref.shape
    chains = max(1, min(SCAN_CHAINS, s_len // (8 * SCAN_SEGMENTS)))
    nseg = SCAN_SEGMENTS * chains
    seg = s_len // nseg

    def step(j, carry):
        jj = (seg - 1 - j) if reverse else j
        out = []
        for c, (h, acc) in enumerate(carry):
            idx = pl.ds(c * SCAN_SEGMENTS * seg + jj, SCAN_SEGMENTS, stride=seg)
            a = a_ref[idx, :]
            h = a * h + u_ref[idx, :]
            acc = a * acc
            h_ref[idx, :] = h
            acum_ref[idx, :] = acc
            out.append((h, acc))
        return tuple(out)

    init = tuple((jnp.zeros((SCAN_SEGMENTS, w), F32), jnp.ones((SCAN_SEGMENTS, w), F32)) for _ in range(chains))
    ends = lax.fori_loop(0, seg, step, init, unroll=min(8, seg))
    order = range(nseg - 2, -1, -1) if reverse else range(1, nseg)
    inflow = jnp.zeros((1, w), F32)
    for s in order:
        src = s + 1 if reverse else s - 1
        h, acc = ends[src // SCAN_SEGMENTS]
        r = src % SCAN_SEGMENTS
        inflow = h[r:r + 1, :] + acc[r:r + 1, :] * inflow
        rows = pl.ds(s * seg, seg)
        h_ref[rows, :] = h_ref[rows, :] + acum_ref[rows, :] * inflow


def _w_spec(rows_half, d, blk):
    return pl.BlockSpec((N_CHIPS, 2, rows_half, d), lambda *_: (0, 0, blk, 0), pipeline_mode=pl.Buffered(1))


def ffn_forward(x, gain, wfull, lay, which, deps=(), tm=512):
    s_len, d = x.shape
    tm = min(tm, s_len)
    f = 8 * lay.fh
    fc = f // 2

    def body(x_ref, g_ref, wg_ref, wu_ref, wd_ref, *rest):
        o_ref, gate_ref, up_ref = rest[len(deps):]
        x = x_ref[...]
        _, _, hn = _rms(x, g_ref[...])
        h = hn.astype(BF)
        y = jnp.zeros((tm, d), F32)
        for part in range(2):
            cols = slice(part * fc, (part + 1) * fc)
            gate = dot_nt(h, wg_ref[...].reshape(f, d)[cols])
            up = dot_nt(h, wu_ref[...].reshape(f, d)[cols])
            act = (gate * jax.nn.sigmoid(gate) * up).astype(BF)
            y = y + dot_nn(act, wd_ref[...].reshape(f, d)[cols])
            gate_ref[:, cols] = gate.astype(BF)
            up_ref[:, cols] = up.astype(BF)
        o_ref[...] = x + FFN_RES * y

    row = pl.BlockSpec((tm, d), lambda i: (i, 0))
    wide = pl.BlockSpec((tm, f), lambda i: (i, 0))
    return pl.pallas_call(
        body, name="ffn_forward", grid=(s_len // tm,),
        out_shape=(SDS((s_len, d), F32), SDS((s_len, f), BF), SDS((s_len, f), BF)),
        in_specs=[row, pl.BlockSpec((1, d), lambda i: (0, 0))]
        + [_w_spec(lay.fh, d, 3 * which + m) for m in range(3)] + [ANY] * len(deps),
        out_specs=(row, wide, wide), compiler_params=_cparams(),
    )(x, gain, wfull, wfull, wfull, *deps)


def ffn_backward_dx(x, gain, dout, gate_bf, up_bf, wfull, lay, which, deps=(), tm=256):
    s_len, d = x.shape
    tm = min(tm, s_len)
    f = 8 * lay.fh
    fc = f // 2
    nt = s_len // tm

    def body(x_ref, g_ref, do_ref, gate_ref, up_ref, wg_ref, wu_ref, wd_ref, *rest):
        dx_ref, dg_ref, dgate_ref, dup_ref, act_ref, h_ref, df_ref = rest[len(deps):]
        x = x_ref[...]
        g = g_ref[...]
        xh, rs, hn = _rms(x, g)
        h = hn.astype(BF)
        do = do_ref[...]
        df = (FFN_RES * do).astype(BF)
        dh = jnp.zeros((tm, d), F32)
        for part in range(2):
            cols = slice(part * fc, (part + 1) * fc)
            wg = wg_ref[...].reshape(f, d)[cols]
            wu = wu_ref[...].reshape(f, d)[cols]
            gate = gate_ref[:, cols].astype(F32)
            up = up_ref[:, cols].astype(F32)
            sg = jax.nn.sigmoid(gate)
            silu = gate * sg
            dact = dot_nt(df, wd_ref[...].reshape(f, d)[cols])
            dup = (dact * silu).astype(BF)
            dgate = (dact * up * (sg * (1.0 + gate * (1.0 - sg)))).astype(BF)
            dh = dh + dot_nn(dgate, wg) + dot_nn(dup, wu)
            dgate_ref[:, cols] = dgate
            dup_ref[:, cols] = dup
            act_ref[:, cols] = (silu * up).astype(BF)
        dxn, dgrow = _rms_bwd(dh, xh, rs, g)
        dx_ref[...] = do + dxn

        @pl.when(pl.program_id(0) == 0)
        def _():
            dg_ref[...] = jnp.zeros_like(dg_ref)

        dg_ref[...] += jnp.sum(dgrow, axis=0, keepdims=True)
        h_ref[...] = h
        df_ref[...] = df

    row = pl.BlockSpec((tm, d), lambda i: (i, 0))
    wide = pl.BlockSpec((tm, f), lambda i: (i, 0))
    vec = pl.BlockSpec((1, d), lambda i: (0, 0))
    return pl.pallas_call(
        body, name="ffn_backward_dx", grid=(nt,),
        out_shape=(SDS((s_len, d), F32), SDS((1, d), F32), SDS((s_len, f), BF), SDS((s_len, f), BF),
                   SDS((s_len, f), BF), SDS((s_len, d), BF), SDS((s_len, d), BF)),
        in_specs=[row, vec, row, wide, wide] + [_w_spec(lay.fh, d, 3 * which + m) for m in range(3)] + [ANY] * len(deps),
        out_specs=(row, vec, wide, wide, wide, row, row), compiler_params=_cparams(),
    )(x, gain, dout, gate_bf, up_bf, wfull, wfull, wfull, *deps)


def weight_grad_tn(a, b, gb, lay, blk, tk=512):
    s_len, f = a.shape
    tk = min(tk, s_len)
    d = b.shape[1]
    fc = f // 2
    nk = s_len // tk

    def body(a_ref, b_ref, gb_ref, o_ref, acc):
        kt = pl.program_id(1)

        @pl.when(kt == 0)
        def _():
            acc[...] = jnp.zeros_like(acc)

        acc[...] += dot_tn(a_ref[...], b_ref[...])

        @pl.when(kt == nk - 1)
        def _():
            for p in range(2):
                for q in range(2):
                    o_ref[p, q] = acc[pl.ds((2 * p + q) * lay.fh, lay.fh), :].astype(o_ref.dtype)

    return pl.pallas_call(
        body, name="weight_grad_tn", grid=(2, nk), out_shape=SDS(gb.shape, gb.dtype),
        in_specs=[pl.BlockSpec((tk, fc), lambda j, kt: (kt, j)), pl.BlockSpec((tk, d), lambda j, kt: (kt, 0)), ANY],
        out_specs=pl.BlockSpec((2, 2, lay.fh, d), lambda j, kt: (j, 0, blk, 0)),
        scratch_shapes=[pltpu.VMEM((fc, d), F32)],
        input_output_aliases={2: 0}, compiler_params=_cparams(),
    )(a, b, gb)


def _lane_blocks(v):
    return [v[:, j * LANE:(j + 1) * LANE] for j in range(v.shape[1] // LANE)]


def _join_lane_blocks(ref):
    return jnp.concatenate([ref[j] for j in range(ref.shape[0])], axis=1)


def _cbm_spec(nblk, rows, first=0):
    return pl.BlockSpec((nblk, rows, LANE), lambda i: (first // nblk, i, 0))


def mix_project(x, gain, wfull, lay, tm=512):
    s_len, d = x.shape
    tm = min(tm, s_len)
    d_in = 8 * lay.ih
    ncol = d_in // LANE

    def body(x_ref, g_ref, w_ref, o_ref):
        _, _, hn = _rms(x_ref[...], g_ref[...])
        res = dot_nt(hn.astype(BF), w_ref[:, :, :lay.ih, :].reshape(d_in, d))
        for j, piece in enumerate(_lane_blocks(res)):
            o_ref[j] = piece

    return pl.pallas_call(
        body, name="mix_project", grid=(s_len // tm,), out_shape=SDS((ncol, s_len, LANE), F32),
        in_specs=[pl.BlockSpec((tm, d), lambda i: (i, 0)), pl.BlockSpec((1, d), lambda i: (0, 0)),
                  _w_spec(lay.fh, d, lay.MIX_BLK)],
        out_specs=_cbm_spec(ncol, tm), compiler_params=_cparams(),
    )(x, gain, wfull)


def mix_project_backward(x, gain, dout, dxr, dgt, dq, dkv, dwout, wfull, gb, lay, tm=512):
    s_len, d = x.shape
    tm = min(tm, s_len)
    d_in = 8 * lay.ih
    nt = s_len // tm
    kvw = dkv.shape[1]

    def body(x_ref, g_ref, do_ref, dxr_ref, dgt_ref, dq_ref, dkv_ref, dwo_ref, w_ref, gb_ref, dx_ref, dg_ref, o_ref, acc):
        i = pl.program_id(0)
        g = g_ref[...]
        xh, rs, hn = _rms(x_ref[...], g)
        h = hn.astype(BF)
        dp = jnp.concatenate([_join_lane_blocks(dxr_ref), _join_lane_blocks(dgt_ref), _join_lane_blocks(dq_ref),
                              dkv_ref[...]], axis=1).astype(BF)
        dh = dot_nn(dp, w_ref[:, :, :lay.ih, :].reshape(d_in, d))
        dxn, dgrow = _rms_bwd(dh, xh, rs, g)
        dx_ref[...] = do_ref[...] + dxn

        @pl.when(i == 0)
        def _():
            dg_ref[...] = jnp.zeros_like(dg_ref)
            acc[...] = jnp.zeros_like(acc)

        dg_ref[...] += jnp.sum(dgrow, axis=0, keepdims=True)
        acc[...] += dot_tn(dp, h)

        @pl.when(i == nt - 1)
        def _():
            for p in range(N_CHIPS):
                for q in range(2):
                    o_ref[p, q, :lay.ih, :] = acc[pl.ds((2 * p + q) * lay.ih, lay.ih), :].astype(o_ref.dtype)
            o_ref[:, :, lay.ih:, :] = dwo_ref[...]

    row = pl.BlockSpec((tm, d), lambda i: (i, 0))
    vec = pl.BlockSpec((1, d), lambda i: (0, 0))
    return pl.pallas_call(
        body, name="mix_project_backward", grid=(nt,),
        out_shape=(SDS((s_len, d), F32), SDS((1, d), F32), SDS(gb.shape, gb.dtype)),
        in_specs=[row, vec, row, _cbm_spec(dxr.shape[0], tm), _cbm_spec(dgt.shape[0], tm),
                  _cbm_spec(dq.shape[0], tm), pl.BlockSpec((tm, kvw), lambda i: (i, 0)),
                  pl.BlockSpec(dwout.shape, lambda i: (0, 0, 0, 0)), _w_spec(lay.fh, d, lay.MIX_BLK), ANY],
        out_specs=(row, vec, pl.BlockSpec((N_CHIPS, 2, lay.fh, d), lambda i: (0, 0, lay.MIX_BLK, 0))),
        scratch_shapes=[pltpu.VMEM((d_in, d), F32)],
        input_output_aliases={9: 2}, compiler_params=_cparams(),
    )(x, gain, dout, dxr, dgt, dq, dkv, dwout, wfull, gb)


def _lru_gates(xc, wb_ref, pv_ref, direction):
    xcb = xc.astype(BF)
    r = jax.nn.sigmoid(dot_nn(xcb, wb_ref[2 * direction]) + pv_ref[1 + direction:2 + direction, :])
    i = jax.nn.sigmoid(dot_nn(xcb, wb_ref[2 * direction + 1]) + pv_ref[3 + direction:4 + direction, :])
    lam = pv_ref[5 + direction:6 + direction, :]
    sp = jnp.maximum(-lam, 0.0) + jnp.log(1.0 + jnp.exp(-jnp.abs(lam)))
    a = jnp.exp(-LRU_C * sp * r)
    mult = jnp.sqrt(1.0 - a * a)
    return xcb, r, i, a, mult, sp


def _conv_rows(xr, cv_ref, bias, n):
    acc = bias + cv_ref[0:1, :] * _shift_rows(xr, -2, n)
    for j in range(1, CONV_WIDTH):
        acc = acc + cv_ref[j:j + 1, :] * _shift_rows(xr, j - 2, n)
    return acc


def lru_forward(proj, cvec, pvec, wblk, lw, deps=(), ch=512):
    s_len = proj.shape[1]
    ncb = lw // LANE
    ch = min(ch, s_len)
    nchunk = s_len // ch

    def body(xr_ref, gt_ref, cv_ref, pv_ref, wb_ref, *rest):
        y_ref, hs_ref, xc_s, a_s, u_s, acum_s = rest[len(deps):]
        xc_s[...] = _conv_rows(xr_ref[...], cv_ref, pv_ref[0:1, :], s_len)
        for direction in range(2):
            def fill(ci, _):
                rows = pl.ds(pl.multiple_of(ci * ch, ch), ch)
                xc = xc_s[rows, :]
                _, _, i, a, mult, _ = _lru_gates(xc, wb_ref, pv_ref, direction)
                a_s[rows, :] = a
                u_s[rows, :] = mult * (i * xc)
                return 0

            lax.fori_loop(0, nchunk, fill, 0)
            _scan_rows(a_s, u_s, hs_ref.at[direction], acum_s, reverse=direction == 1)

        def out(ci, _):
            rows = pl.ds(pl.multiple_of(ci * ch, ch), ch)
            gl, _ = _gelu(gt_ref[rows, :])
            y_ref[rows, :] = gl * (hs_ref[0, rows, :] + hs_ref[1, rows, :])
            return 0

        lax.fori_loop(0, nchunk, out, 0)

    col = lambda off: pl.BlockSpec((None, s_len, LANE), lambda cb: (off + cb, 0, 0))
    return pl.pallas_call(
        body, name="lru_forward", grid=(ncb,),
        out_shape=(SDS((ncb, s_len, LANE), F32), SDS((2, ncb, s_len, LANE), F32)),
        in_specs=[col(0), col(ncb), pl.BlockSpec((8, LANE), lambda cb: (0, cb)), pl.BlockSpec((8, LANE), lambda cb: (0, cb)),
                  pl.BlockSpec((4, None, LANE, LANE), lambda cb: (0, cb, 0, 0))] + [ANY] * len(deps),
        out_specs=(col(0), pl.BlockSpec((2, None, s_len, LANE), lambda cb: (0, cb, 0, 0))),
        scratch_shapes=[pltpu.VMEM((s_len, LANE), F32)] * 4, compiler_params=_cparams(),
    )(proj, proj, cvec, pvec, wblk, *deps)


def lru_backward(proj, hs, dy, cvec, pvec, wblk, lw, ch=512):
    s_len = proj.shape[1]
    ncb = lw // LANE
    ch = min(ch, s_len)
    nchunk = s_len // ch

    def body(xr_ref, gt_ref, hs_ref, dy_ref, cv_ref, pv_ref, wb_ref, dxr_ref, dgt_ref, dcv_ref, dpv_ref, dwb_ref,
             xc_s, a_s, dh_s, lam_s, hp_s, dxc_s, acum_s):
        xr = xr_ref[...]
        xc_s[...] = _conv_rows(xr, cv_ref, pv_ref[0:1, :], s_len)
        dxc_s[...] = jnp.zeros_like(dxc_s)
        dpv_ref[...] = jnp.zeros_like(dpv_ref)
        dwb_ref[...] = jnp.zeros_like(dwb_ref)

        def head(ci, _):
            rows = pl.ds(pl.multiple_of(ci * ch, ch), ch)
            gt = gt_ref[rows, :]
            gl, t = _gelu(gt)
            dy = dy_ref[rows, :]
            dh_s[rows, :] = dy * gl
            dgt_ref[rows, :] = dy * (hs_ref[0, rows, :] + hs_ref[1, rows, :]) * _gelu_grad(gt, t)
            return 0

        lax.fori_loop(0, nchunk, head, 0)

        for direction in range(2):
            def fill(ci, _):
                rows = pl.ds(pl.multiple_of(ci * ch, ch), ch)
                _, _, _, a, _, _ = _lru_gates(xc_s[rows, :], wb_ref, pv_ref, direction)
                a_s[rows, :] = a
                return 0

            lax.fori_loop(0, nchunk, fill, 0)
            toward = 1 if direction == 0 else -1
            hp_s[...] = _shift_rows(a_s[...], toward, s_len)
            _scan_rows(hp_s, dh_s, lam_s, acum_s, reverse=direction == 0)
            hp_s[...] = _shift_rows(hs_ref[direction], -toward, s_len)

            def grads(ci, _):
                rows = pl.ds(pl.multiple_of(ci * ch, ch), ch)
                xc = xc_s[rows, :]
                xcb, r, i, a, mult, sp = _lru_gates(xc, wb_ref, pv_ref, direction)
                du = lam_s[rows, :]
                da = du * hp_s[rows, :]
                dmult = du * i * xc
                di = du * mult * xc
                dlog_a = (da - dmult * a / mult) * a
                dr = dlog_a * (-LRU_C * sp)
                dza = dr * r * (1.0 - r)
                dzx = di * i * (1.0 - i)
                dzab = dza.astype(BF)
                dzxb = dzx.astype(BF)
                dxc_s[rows, :] += (du * mult * i + dot_nt(dzab, wb_ref[2 * direction])
                                   + dot_nt(dzxb, wb_ref[2 * direction + 1]))
                dwb_ref[2 * direction] += dot_tn(xcb, dzab)
                dwb_ref[2 * direction + 1] += dot_tn(xcb, dzxb)
                dpv_ref[1 + direction:2 + direction, :] += jnp.sum(dza, axis=0, keepdims=True)
                dpv_ref[3 + direction:4 + direction, :] += jnp.sum(dzx, axis=0, keepdims=True)
                dpv_ref[5 + direction:6 + direction, :] += jnp.sum(dlog_a * (-LRU_C * r), axis=0, keepdims=True)
                return 0

            lax.fori_loop(0, nchunk, grads, 0)

        for direction in range(2):
            lam = pv_ref[5 + direction:6 + direction, :]
            dpv_ref[5 + direction:6 + direction, :] = dpv_ref[5 + direction:6 + direction, :] * (-jax.nn.sigmoid(-lam))
        dxc = dxc_s[...]
        dpv_ref[0:1, :] = jnp.sum(dxc, axis=0, keepdims=True)
        dxr = cv_ref[0:1, :] * _shift_rows(dxc, 2, s_len)
        for j in range(1, CONV_WIDTH):
            dxr = dxr + cv_ref[j:j + 1, :] * _shift_rows(dxc, 2 - j, s_len)
        dxr_ref[...] = dxr
        dcv_ref[...] = jnp.zeros_like(dcv_ref)
        for j in range(CONV_WIDTH):
            dcv_ref[j:j + 1, :] = jnp.sum(dxc * _shift_rows(xr, j - 2, s_len), axis=0, keepdims=True)

    col = lambda off: pl.BlockSpec((None, s_len, LANE), lambda cb: (off + cb, 0, 0))
    own = col(0)
    small = pl.BlockSpec((8, LANE), lambda cb: (0, cb))
    wspec = pl.BlockSpec((4, None, LANE, LANE), lambda cb: (0, cb, 0, 0))
    return pl.pallas_call(
        body, name="lru_backward", grid=(ncb,),
        out_shape=(SDS((ncb, s_len, LANE), F32), SDS((ncb, s_len, LANE), F32), SDS((8, lw), F32), SDS((8, lw), F32),
                   SDS(wblk.shape, F32)),
        in_specs=[col(0), col(ncb), pl.BlockSpec((2, None, s_len, LANE), lambda cb: (0, cb, 0, 0)), own, small, small, wspec],
        out_specs=(own, own, small, small, wspec),
        scratch_shapes=[pltpu.VMEM((s_len, LANE), F32)] * 7, compiler_params=_cparams(),
    )(proj, proj, hs, dy, cvec, pvec, wblk)


def _attn_specs(s_len, lw, att):
    nb = s_len // BLOCK
    kcol = (2 * lw + att) // LANE
    prev = lambda n: jnp.maximum(n - 1, 0)
    nxt = lambda n: jnp.minimum(n + 1, nb - 1)
    q = _cbm_spec(att // LANE, BLOCK, first=2 * lw // LANE)
    ks = [pl.BlockSpec((None, BLOCK, LANE), lambda n, f=f: (kcol, f(n), 0)) for f in (prev, lambda n: n, nxt)]
    vs = [pl.BlockSpec((None, BLOCK, LANE), lambda n, f=f: (kcol + 1, f(n), 0)) for f in (prev, lambda n: n, nxt)]
    return q, ks, vs


HEADS_PER_LANE_BLOCK = LANE // HEAD_DIM


def _stack_heads(v, kh):
    pieces = []
    for g in range(KV_GROUP):
        blk, sub = divmod(kh * KV_GROUP + g, HEADS_PER_LANE_BLOCK)
        pieces.append(v[blk][:, sub * HEAD_DIM:(sub + 1) * HEAD_DIM])
    return jnp.concatenate(pieces, axis=0)


def _unstack_heads(groups):
    heads = [grp[g * BLOCK:(g + 1) * BLOCK] for grp in groups for g in range(KV_GROUP)]
    return [jnp.concatenate(heads[b * HEADS_PER_LANE_BLOCK:(b + 1) * HEADS_PER_LANE_BLOCK], axis=1)
            for b in range(len(heads) // HEADS_PER_LANE_BLOCK)]


def _key_exists(n, nb):
    j = lax.broadcasted_iota(jnp.int32, (1, 3 * BLOCK), 1)
    return ((n > 0) | (j >= BLOCK)) & ((n < nb - 1) | (j < 2 * BLOCK))


def _attn_probs(qs, kcat, bias_g, sink_g, key_ok):
    logits = jnp.where(key_ok, dot_nt(qs, kcat) + bias_g, NEG_INF)
    m = jnp.maximum(jnp.max(logits, axis=-1, keepdims=True), sink_g)
    p = jnp.exp(logits - m)
    es = jnp.exp(sink_g - m)
    inv = 1.0 / (jnp.sum(p, axis=-1, keepdims=True) + es)
    return p * inv, es * inv


def attention_forward(proj, bias, sink, lw, att):
    s_len = proj.shape[1]
    nb = s_len // BLOCK
    q_spec, k_specs, v_specs = _attn_specs(s_len, lw, att)

    def body(q_ref, kp_ref, kc_ref, kn_ref, vp_ref, vc_ref, vn_ref, b_ref, s_ref, o_ref):
        n = pl.program_id(0)
        q = q_ref[...]
        key_ok = _key_exists(n, nb)
        kall = jnp.concatenate([kp_ref[...], kc_ref[...], kn_ref[...]], axis=0).astype(BF)
        vall = jnp.concatenate([vp_ref[...], vc_ref[...], vn_ref[...]], axis=0).astype(BF)
        outs = []
        for kh in range(N_KV_HEADS):
            grp = slice(kh * KV_GROUP * BLOCK, (kh + 1) * KV_GROUP * BLOCK)
            qs = (_stack_heads(q, kh) * (HEAD_DIM ** -0.5)).astype(BF)
            bias_g = b_ref[kh * KV_GROUP:(kh + 1) * KV_GROUP].reshape(KV_GROUP * BLOCK, 3 * BLOCK)
            p, _ = _attn_probs(qs, kall[:, kh * HEAD_DIM:(kh + 1) * HEAD_DIM], bias_g, s_ref[grp, 0:1], key_ok)
            outs.append(dot_nn(p.astype(BF), vall[:, kh * HEAD_DIM:(kh + 1) * HEAD_DIM]))
        for b, piece in enumerate(_unstack_heads(outs)):
            o_ref[b] = piece

    return pl.pallas_call(
        body, name="attention_forward", grid=(nb,), out_shape=SDS((att // LANE, s_len, LANE), F32),
        in_specs=[q_spec] + k_specs + v_specs
        + [pl.BlockSpec(bias.shape, lambda n: (0, 0, 0)), pl.BlockSpec(sink.shape, lambda n: (0, 0))],
        out_specs=_cbm_spec(att // LANE, BLOCK), compiler_params=_cparams(),
    )(proj, proj, proj, proj, proj, proj, proj, bias, sink)


def attention_backward(proj, y_att, dy, bias, sink, lw, att):
    s_len = proj.shape[1]
    nb = s_len // BLOCK
    kvw = N_KV_HEADS * HEAD_DIM
    q_spec, k_specs, v_specs = _attn_specs(s_len, lw, att)

    def body(q_ref, kp_ref, kc_ref, kn_ref, vp_ref, vc_ref, vn_ref, o_ref, do_ref, b_ref, s_ref,
             dq_ref, dkv_ref, db_ref, ds_ref):
        n = pl.program_id(0)

        @pl.when(n == 0)
        def _():
            dkv_ref[...] = jnp.zeros_like(dkv_ref)
            db_ref[...] = jnp.zeros_like(db_ref)
            ds_ref[...] = jnp.zeros_like(ds_ref)

        q = q_ref[...]
        o = o_ref[...]
        do = do_ref[...]
        kall = jnp.concatenate([kp_ref[...], kc_ref[...], kn_ref[...]], axis=0).astype(BF)
        vall = jnp.concatenate([vp_ref[...], vc_ref[...], vn_ref[...]], axis=0).astype(BF)
        key_ok = _key_exists(n, nb)
        dqs, dks, dvs = [], [], []
        for kh in range(N_KV_HEADS):
            heads = slice(kh * KV_GROUP, (kh + 1) * KV_GROUP)
            grp = slice(kh * KV_GROUP * BLOCK, (kh + 1) * KV_GROUP * BLOCK)
            kcat = kall[:, kh * HEAD_DIM:(kh + 1) * HEAD_DIM]
            vcat = vall[:, kh * HEAD_DIM:(kh + 1) * HEAD_DIM]
            qs = (_stack_heads(q, kh) * (HEAD_DIM ** -0.5)).astype(BF)
            bias_g = b_ref[heads].reshape(KV_GROUP * BLOCK, 3 * BLOCK)
            p, ps = _attn_probs(qs, kcat, bias_g, s_ref[grp, 0:1], key_ok)
            dos = _stack_heads(do, kh)
            dosb = dos.astype(BF)
            delta = jnp.sum(dos * _stack_heads(o, kh), axis=-1, keepdims=True)
            dlog = p * (dot_nt(dosb, vcat) - delta)
            dlogb = dlog.astype(BF)
            db_ref[heads] += dlog.reshape(KV_GROUP, BLOCK, 3 * BLOCK)
            dsink = -ps * delta
            for g in range(KV_GROUP):
                h = kh * KV_GROUP + g
                part = jnp.sum(dsink[g * BLOCK:(g + 1) * BLOCK], axis=0, keepdims=True)
                ds_ref[h:h + 1, :] += jnp.broadcast_to(part, (1, LANE))
            dqs.append(dot_nn(dlogb, kcat) * (HEAD_DIM ** -0.5))
            dks.append(dot_tn(dlogb, qs))
            dvs.append(dot_tn(p.astype(BF), dosb))
        for b, piece in enumerate(_unstack_heads(dqs)):
            dq_ref[b] = piece
        dkv = jnp.concatenate(dks + dvs, axis=1)
        starts = [jnp.maximum(n - 1, 0), n, jnp.minimum(n + 1, nb - 1)]
        for b, st in enumerate(starts):
            rows = pl.ds(pl.multiple_of(st * BLOCK, BLOCK), BLOCK)
            dkv_ref[rows, :] += dkv[b * BLOCK:(b + 1) * BLOCK, :]

    blk = _cbm_spec(att // LANE, BLOCK)
    return pl.pallas_call(
        body, name="attention_backward", grid=(nb,),
        out_shape=(SDS((att // LANE, s_len, LANE), F32), SDS((s_len, 2 * kvw), F32), SDS(bias.shape, F32),
                   SDS((N_HEADS, LANE), F32)),
        in_specs=[q_spec] + k_specs + v_specs
        + [blk, blk, pl.BlockSpec(bias.shape, lambda n: (0, 0, 0)), pl.BlockSpec(sink.shape, lambda n: (0, 0))],
        out_specs=(blk, pl.BlockSpec((s_len, 2 * kvw), lambda n: (0, 0)),
                   pl.BlockSpec(bias.shape, lambda n: (0, 0, 0)), pl.BlockSpec((N_HEADS, LANE), lambda n: (0, 0))),
        compiler_params=_cparams(),
    )(proj, proj, proj, proj, proj, proj, proj, y_att, dy, bias, sink)


def mix_output(x, y_rec, y_att, g_rec, g_att, wfull, lay, tm=512):
    s_len, d = x.shape
    tm = min(tm, s_len)
    lw = y_rec.shape[0] * LANE
    att = y_att.shape[0] * LANE

    def body(x_ref, yr_ref, ya_ref, gr_ref, ga_ref, w_ref, o_ref):
        _, _, nr = _rms(_join_lane_blocks(yr_ref), gr_ref[...])
        _, _, na = _rms(_join_lane_blocks(ya_ref), ga_ref[...])
        y = jnp.concatenate([nr, na], axis=1).astype(BF)
        o_ref[...] = x_ref[...] + dot_nn(y, w_ref[:, :, lay.ih:, :].reshape(d, d))

    row = pl.BlockSpec((tm, d), lambda i: (i, 0))
    return pl.pallas_call(
        body, name="mix_output", grid=(s_len // tm,), out_shape=SDS((s_len, d), F32),
        in_specs=[row, _cbm_spec(lw // LANE, tm), _cbm_spec(att // LANE, tm),
                  pl.BlockSpec((1, lw), lambda i: (0, 0)), pl.BlockSpec((1, att), lambda i: (0, 0)),
                  _w_spec(lay.fh, d, lay.MIX_BLK)],
        out_specs=row, compiler_params=_cparams(),
    )(x, y_rec, y_att, g_rec, g_att, wfull)


def mix_output_backward(dout, y_rec, y_att, g_rec, g_att, wfull, lay, tm=512):
    s_len, d = dout.shape
    tm = min(tm, s_len)
    lw = y_rec.shape[0] * LANE
    att = y_att.shape[0] * LANE
    nt = s_len // tm

    def body(do_ref, yr_ref, ya_ref, gr_ref, ga_ref, w_ref, dyr_ref, dya_ref, dgr_ref, dga_ref, o_ref, acc):
        i = pl.program_id(0)
        gr = gr_ref[...]
        ga = ga_ref[...]
        xhr, rsr, nr = _rms(_join_lane_blocks(yr_ref), gr)
        xha, rsa, na = _rms(_join_lane_blocks(ya_ref), ga)
        y = jnp.concatenate([nr, na], axis=1).astype(BF)
        dob = do_ref[...].astype(BF)
        dy = dot_nt(dob, w_ref[:, :, lay.ih:, :].reshape(d, d))
        dyr, dgr_row = _rms_bwd(dy[:, :lw], xhr, rsr, gr)
        dya, dga_row = _rms_bwd(dy[:, lw:], xha, rsa, ga)
        for j, piece in enumerate(_lane_blocks(dyr)):
            dyr_ref[j] = piece
        for j, piece in enumerate(_lane_blocks(dya)):
            dya_ref[j] = piece

        @pl.when(i == 0)
        def _():
            dgr_ref[...] = jnp.zeros_like(dgr_ref)
            dga_ref[...] = jnp.zeros_like(dga_ref)
            acc[...] = jnp.zeros_like(acc)

        dgr_ref[...] += jnp.sum(dgr_row, axis=0, keepdims=True)
        dga_ref[...] += jnp.sum(dga_row, axis=0, keepdims=True)
        acc[...] += dot_tn(y, dob)

        @pl.when(i == nt - 1)
        def _():
            for p in range(N_CHIPS):
                for q in range(2):
                    o_ref[p, q] = acc[pl.ds((2 * p + q) * lay.oh, lay.oh), :].astype(o_ref.dtype)

    row = pl.BlockSpec((tm, d), lambda i: (i, 0))
    return pl.pallas_call(
        body, name="mix_output_backward", grid=(nt,),
        out_shape=(SDS(y_rec.shape, F32), SDS(y_att.shape, F32), SDS((1, lw), F32), SDS((1, att), F32),
                   SDS((N_CHIPS, 2, lay.oh, d), BF)),
        in_specs=[row, _cbm_spec(lw // LANE, tm), _cbm_spec(att // LANE, tm),
                  pl.BlockSpec((1, lw), lambda i: (0, 0)), pl.BlockSpec((1, att), lambda i: (0, 0)),
                  _w_spec(lay.fh, d, lay.MIX_BLK)],
        out_specs=(_cbm_spec(lw // LANE, tm), _cbm_spec(att // LANE, tm),
                   pl.BlockSpec((1, lw), lambda i: (0, 0)), pl.BlockSpec((1, att), lambda i: (0, 0)),
                   pl.BlockSpec((N_CHIPS, 2, lay.oh, d), lambda i: (0, 0, 0, 0))),
        scratch_shapes=[pltpu.VMEM((d, d), F32)], compiler_params=_cparams(),
    )(dout, y_rec, y_att, g_rec, g_att, wfull)


def loss_head(x, gain, target, tm=512):
    s_len, d = x.shape
    tm = min(tm, s_len)

    def body(x_ref, g_ref, t_ref, dx_ref, dg_ref, loss_ref):
        g = g_ref[...]
        xh, rs, y = _rms(x_ref[...], g)
        err = y - t_ref[...]

        @pl.when(pl.program_id(0) == 0)
        def _():
            dg_ref[...] = jnp.zeros_like(dg_ref)
            loss_ref[...] = jnp.zeros_like(loss_ref)

        part = 0.5 * jnp.sum(jnp.mean(err * err, axis=-1, keepdims=True), axis=0, keepdims=True)
        loss_ref[...] += jnp.broadcast_to(part, loss_ref.shape)
        dx, dgrow = _rms_bwd(err * (1.0 / d), xh, rs, g)
        dx_ref[...] = dx
        dg_ref[...] += jnp.sum(dgrow, axis=0, keepdims=True)

    row = pl.BlockSpec((tm, d), lambda i: (i, 0))
    vec = pl.BlockSpec((1, d), lambda i: (0, 0))
    return pl.pallas_call(
        body, name="loss_head", grid=(s_len // tm,),
        out_shape=(SDS((s_len, d), F32), SDS((1, d), F32), SDS((8, LANE), F32)),
        in_specs=[row, vec, row], out_specs=(row, vec, pl.BlockSpec((8, LANE), lambda i: (0, 0))),
        compiler_params=_cparams(),
    )(x, gain, target)


def _adamw_update(w, g, m, v):
    m = ADAM_B1 * m + (1.0 - ADAM_B1) * g
    v = ADAM_B2 * v + (1.0 - ADAM_B2) * (g * g)
    m_hat = m / (1.0 - ADAM_B1 ** ADAM_STEP)
    v_hat = v / (1.0 - ADAM_B2 ** ADAM_STEP)
    return -ADAM_LR * (m_hat / (jnp.sqrt(v_hat) + ADAM_EPS) + ADAM_WD * w), m, v


def adamw(w, g, m, v, tr=512):
    rows, cols = w.shape
    tr = _row_chunk(rows, tr, 8)

    def body(w_ref, g_ref, m_ref, v_ref, d_ref, nm_ref, nv_ref):
        d_ref[...], nm_ref[...], nv_ref[...] = _adamw_update(w_ref[...], g_ref[...], m_ref[...], v_ref[...])

    blk = pl.BlockSpec((tr, cols), lambda i: (i, 0))
    return pl.pallas_call(
        body, name="adamw", grid=(rows // tr,), out_shape=(SDS(w.shape, F32),) * 3,
        in_specs=[blk] * 4, out_specs=(blk,) * 3, compiler_params=_cparams(),
    )(w, g, m, v)


def _eye(n):
    return (lax.broadcasted_iota(jnp.int32, (n, n), 0) == lax.broadcasted_iota(jnp.int32, (n, n), 1)).astype(BF)


def _transpose_f32(a):
    eye = _eye(a.shape[0])
    hi = a.astype(BF)
    rest = a - hi.astype(F32)
    mid = rest.astype(BF)
    lo = (rest - mid.astype(F32)).astype(BF)
    return (dot_tn(hi, eye) + dot_tn(mid, eye)) + dot_tn(lo, eye)


def adamw_layer(gf, blk, row_off, n_half, transposed, l, w, m, v, outs, deps=(), tr=256):
    depth, r, c = w.shape
    fh = gf.shape[1] // 7
    d = gf.shape[2]
    nd = len(deps)

    def body(gf_ref, w_ref, m_ref, v_ref, *rest):
        g_ref, d_ref, nm_ref, nv_ref = rest[4 + nd:]
        if transposed:
            g2 = jnp.concatenate([gf_ref[0, row_off:row_off + n_half, :], gf_ref[1, row_off:row_off + n_half, :]], axis=0)
            g = _transpose_f32(g2)
        else:
            g = gf_ref[row_off:row_off + n_half, :]
        g_ref[...] = g
        d_ref[...], nm_ref[...], nv_ref[...] = _adamw_update(w_ref[...], g, m_ref[...], v_ref[...])

    if transposed:
        tr = min(tr, r)
        grid = (r // tr,)
        gspec = pl.BlockSpec((2, fh, tr), lambda i: (0, blk, i))
        wspec = pl.BlockSpec((None, tr, c), lambda i: (l, i, 0))
    else:
        grid = (2,)
        gspec = pl.BlockSpec((None, fh, d), lambda h: (h, blk, 0))
        wspec = pl.BlockSpec((None, n_half, c), lambda h: (l, h, 0))
    return pl.pallas_call(
        body, name="adamw_layer", grid=grid, out_shape=tuple(SDS(o.shape, o.dtype) for o in outs),
        in_specs=[gspec, wspec, wspec, wspec] + [ANY] * (4 + nd), out_specs=(wspec,) * 4,
        input_output_aliases={4 + i: i for i in range(4)}, compiler_params=_cparams(),
    )(gf, w, m, v, *outs, *deps)


def pack_weight(pos, land, blk, l, w, transposed, extra=None, deps=()):
    fh, d = land.shape[2] // 7, land.shape[3]
    nd = len(deps)

    def rows_of(ref, flip):
        a = ref[...].astype(BF)
        return dot_tn(a, _eye(a.shape[0])).astype(BF) if flip else a

    def body(pos_ref, w_ref, *rest):
        o_ref = rest[-1]
        a = rows_of(w_ref, transposed)
        n = a.shape[0] // 2
        for h in range(2):
            o_ref[h, 0:n, :] = a[h * n:(h + 1) * n]
        if extra is not None:
            b = rows_of(rest[0], False)
            nb = b.shape[0] // 2
            for h in range(2):
                o_ref[h, n:n + nb, :] = b[h * nb:(h + 1) * nb]

    def whole(a):
        return pl.BlockSpec((None,) + a.shape[1:], lambda i, p: (l, 0, 0))

    ins = [w] + ([extra] if extra is not None else [])
    return pl.pallas_call(
        body, name="pack_weight", out_shape=SDS(land.shape, land.dtype),
        grid_spec=pltpu.PrefetchScalarGridSpec(
            num_scalar_prefetch=1, grid=(1,),
            in_specs=[whole(a) for a in ins] + [ANY] * (1 + nd),
            out_specs=pl.BlockSpec((None, 2, fh, d), lambda i, p: (p[0], 0, blk, 0))),
        input_output_aliases={1 + len(ins): 0}, compiler_params=_cparams(),
    )(pos, *ins, land, *deps)


def _pack_rows(arrays, width):
    flat = jnp.concatenate([a.reshape(-1).astype(F32) for a in arrays])
    rows = -(-flat.shape[0] // (8 * width)) * 8
    return jnp.pad(flat, (0, rows * width - flat.shape[0])).reshape(rows, width)


def _unpack_rows(buf, shapes):
    flat = buf.reshape(-1)
    out, off = [], 0
    for shp in shapes:
        n = int(np.prod(shp))
        out.append(flat[off:off + n].reshape(shp))
        off += n
    return out


def _t5_buckets(rel):
    half = N_BUCKETS // 2
    max_exact = half // 2
    ret = (rel > 0).astype(jnp.int32) * half
    n = jnp.abs(rel)
    n_f = jnp.maximum(n, 1).astype(F32)
    large = max_exact + (jnp.log(n_f / max_exact) / math.log(MAX_DISTANCE / max_exact) * (half - max_exact)).astype(jnp.int32)
    large = jnp.minimum(large, half - 1)
    return ret + jnp.where(n < max_exact, n, large)


def _band_buckets():
    t = jnp.arange(BLOCK)[:, None]
    j = jnp.arange(3 * BLOCK)[None, :]
    rel = j - BLOCK - t
    return _t5_buckets(rel), jnp.abs(rel) <= WINDOW


def _block_diag_pairs(w):
    depth, two, nblk, bw, _ = w.shape
    pairs = w.reshape(depth, two, nblk // 2, 2, bw, bw)
    z = jnp.zeros_like(pairs[:, :, :, 0])
    top = jnp.concatenate([pairs[:, :, :, 0], z], axis=-1)
    bot = jnp.concatenate([z, pairs[:, :, :, 1]], axis=-1)
    return jnp.concatenate([top, bot], axis=-2)


def _diag_blocks(dw):
    bw = dw.shape[-1] // 2
    a = dw[:, :, :bw, :bw]
    b = dw[:, :, bw:, bw:]
    return jnp.stack([a, b], axis=2).reshape(dw.shape[0], 2 * dw.shape[1], bw, bw)


def kernel(x, ffn1_norm, ffn1_w_gate, ffn1_w_up, ffn1_w_down, mix_norm, w_in, conv_w, conv_b, lru_w_a, lru_b_a, lru_w_x, lru_b_x, lru_lambda, attn_sink, rel_bias, lru_out_norm, attn_out_norm, w_out, ffn2_norm, ffn2_w_gate, ffn2_w_up, ffn2_w_down, final_norm, loss_target, m_ffn1_norm, m_ffn1_w_gate, m_ffn1_w_up, m_ffn1_w_down, m_mix_norm, m_w_in, m_conv_w, m_conv_b, m_lru_w_a, m_lru_b_a, m_lru_w_x, m_lru_b_x, m_lru_lambda, m_attn_sink, m_rel_bias, m_lru_out_norm, m_attn_out_norm, m_w_out, m_ffn2_norm, m_ffn2_w_gate, m_ffn2_w_up, m_ffn2_w_down, m_final_norm, v_ffn1_norm, v_ffn1_w_gate, v_ffn1_w_up, v_ffn1_w_down, v_mix_norm, v_w_in, v_conv_w, v_conv_b, v_lru_w_a, v_lru_b_a, v_lru_w_x, v_lru_b_x, v_lru_lambda, v_attn_sink, v_rel_bias, v_lru_out_norm, v_attn_out_norm, v_w_out, v_ffn2_norm, v_ffn2_w_gate, v_ffn2_w_up, v_ffn2_w_down, v_final_norm):
    depth, d = ffn1_norm.shape
    d_ff = N_CHIPS * ffn1_w_gate.shape[2]
    d_in = N_CHIPS * w_in.shape[2]
    lw = conv_b.shape[1]
    att = N_HEADS * HEAD_DIM
    lay = Layout(d, d_ff, d_in)
    k_chip = 2 * lax.axis_index("x") + lax.axis_index("y")
    pos = jnp.stack([k_chip, lax.axis_index("c")]).astype(jnp.int32)

    mats = (ffn1_w_gate, ffn1_w_up, ffn1_w_down, ffn2_w_gate, ffn2_w_up, ffn2_w_down)
    def pack_layer(l, deps=()):
        land = lax.empty((N_CHIPS, 2, lay.rows, d), BF)
        for m, a in enumerate(mats):
            land = pack_weight(pos, land, m, l, a, m % 3 != 2, deps=deps if m == 0 else ())
        return pack_weight(pos, land, lay.MIX_BLK, l, w_in, True, extra=w_out)

    def gather_start(l, land):
        return split_start(f"gather_start_{l}", [land], 3, gather_plan)

    def gather_finish(l, started, after):
        ssem, rsem, bufs, _ = started
        land, = split_wait(f"gather_wait_{l}", ssem, rsem, bufs, after, gather_plan)
        return gather_pair(land)

    sharded_small = (conv_w, lru_b_a, lru_b_x, lru_lambda)
    sshard = jnp.concatenate([a.reshape(-1, LANE) for a in sharded_small], axis=0)
    sfull = gather_small(sshard)
    small_full, off = [], 0
    for a in sharded_small:
        r = a.shape[0] * a.shape[1]
        piece = sfull[:, off:off + r].reshape((N_CHIPS,) + a.shape)
        small_full.append(jnp.moveaxis(piece, 0, 2).reshape(a.shape[0], a.shape[1], N_CHIPS * LANE))
        off += r
    conv_w_f, b_a_f, b_x_f, lam_f = small_full

    zrow = jnp.zeros((1, lw), F32)
    wblk_a = _block_diag_pairs(lru_w_a)
    wblk_x = _block_diag_pairs(lru_w_x)
    buckets, in_band = _band_buckets()
    onehot = (buckets.reshape(-1)[:, None] == jnp.arange(N_BUCKETS)[None, :]).astype(F32)
    bias = jnp.dot(rel_bias.T, onehot.T, precision=lax.Precision.HIGHEST).reshape(N_HEADS, BLOCK, 3 * BLOCK)
    bias = jnp.where(in_band[None], bias, NEG_INF)

    def layer_small(l):
        cvec = jnp.concatenate([conv_w_f[l], jnp.zeros((8 - CONV_WIDTH, lw), F32)], axis=0)
        pvec = jnp.concatenate([conv_b[l][None], b_a_f[l], b_x_f[l], lam_f[l], zrow], axis=0)
        wblk = jnp.stack([wblk_a[l, 0], wblk_x[l, 0], wblk_a[l, 1], wblk_x[l, 1]]).astype(BF)
        sink = jnp.broadcast_to(attn_sink[l][:, None, None], (N_HEADS, BLOCK, LANE)).reshape(N_HEADS * BLOCK, LANE)
        return cvec, pvec, wblk, sink

    xs = x[0]
    wfull = [None] * depth
    first = gather_start(0, pack_layer(0))
    lands = {l: pack_layer(l, deps=(first[3],)) for l in range(1, depth)}
    wfull[0] = gather_finish(0, first, [xs] + list(lands.values()))
    started = gather_start(1, lands[1]) if depth > 1 else None
    saved = []
    for l in range(depth):
        cvec, pvec, wblk, sink = layer_small(l)
        deps = (started[3],) if started is not None else ()
        x1, gate1, up1 = ffn_forward(xs, ffn1_norm[l][None], wfull[l], lay, 0, deps=deps)
        proj = mix_project(x1, mix_norm[l][None], wfull[l], lay)
        y_rec, hs = lru_forward(proj, cvec, pvec, wblk, lw)
        y_att = attention_forward(proj, bias, sink, lw, att)
        x2 = mix_output(x1, y_rec, y_att, lru_out_norm[l][None], attn_out_norm[l][None], wfull[l], lay)
        x3, gate2, up2 = ffn_forward(x2, ffn2_norm[l][None], wfull[l], lay, 1)
        saved.append((xs, x1, x2, proj, y_rec, hs, y_att, (gate1, up1), (gate2, up2)))
        xs = x3
        if l + 1 < depth:
            wfull[l + 1] = gather_finish(l + 1, started, [x3])
            started = gather_start(l + 2, lands[l + 2]) if l + 2 < depth else None

    dx, d_final, loss_tile = loss_head(xs, final_norm[None], loss_target[0])
    loss = lax.psum(loss_tile[0, 0], ("x", "y", "c"))

    layer_names = ["ffn1_norm", "mix_norm", "conv_w", "conv_b", "lru_w_a", "lru_b_a", "lru_w_x", "lru_b_x", "lru_lambda",
                   "attn_sink", "lru_out_norm", "attn_out_norm", "ffn2_norm"]
    dbias_total = jnp.zeros(bias.shape, F32)

    def ffn_back(xin, gain, dout, pre, gb, l, which, deps=()):
        dxo, dg, dgate, dup, act, h, df = ffn_backward_dx(xin, gain, dout, *pre, wfull[l], lay, which, deps=deps)
        gb = weight_grad_tn(dgate, h, gb, lay, 3 * which + 0)
        gb = weight_grad_tn(dup, h, gb, lay, 3 * which + 1)
        gb = weight_grad_tn(act, df, gb, lay, 3 * which + 2)
        return dxo, dg[0], gb

    def reduce_start(l, gb, sb):
        p1, sp1 = exchange_pair(gb, sb)
        cs = pair_sum(pos, gb, p1)
        ss = small_pair_sum(sb, sp1)
        lands = [lax.empty((3,) + cs.shape[1:], cs.dtype), lax.empty((N_CHIPS,) + ss.shape, ss.dtype)]
        return split_start(f"reduce_start_{l}", [cs, ss] + lands, 6, reduce_plan)

    def reduce_finish(l, started, after):
        ssem, rsem, bufs, _ = started
        cs, ss, p3, sp3 = split_wait(f"reduce_wait_{l}", ssem, rsem, bufs, after, reduce_plan)
        return exchange_final(chip_sum(pos, cs, p3)), small_chip_sum(pos, ss, sp3)

    gf = [None] * depth
    small_sums = [None] * depth
    small_shapes = [None] * depth
    in_flight = None
    for l in reversed(range(depth)):
        x0, x1, x2, proj, y_rec, hs, y_att, pre1, pre2 = saved[l]
        cvec, pvec, wblk, sink = layer_small(l)
        gb = lax.empty((N_CHIPS, 2, lay.rows, d), BF)
        part = {}
        deps = (in_flight[1][3],) if in_flight is not None else ()
        dx, part["ffn2_norm"], gb = ffn_back(x2, ffn2_norm[l][None], dx, pre2, gb, l, 1, deps=deps)
        dyr, dya, dgr, dga, dwout = mix_output_backward(dx, y_rec, y_att, lru_out_norm[l][None], attn_out_norm[l][None],
                                                        wfull[l], lay)
        part["lru_out_norm"] = dgr[0]
        part["attn_out_norm"] = dga[0]
        dq, dkv, dbias, dsink = attention_backward(proj, y_att, dya, bias, sink, lw, att)
        dbias_total = dbias_total + dbias
        part["attn_sink"] = dsink[:, 0]
        dxr, dgt, dcv, dpv, dwb = lru_backward(proj, hs, dyr, cvec, pvec, wblk, lw)
        part["conv_w"] = dcv[:CONV_WIDTH]
        part["conv_b"] = dpv[0]
        part["lru_b_a"] = dpv[1:3]
        part["lru_b_x"] = dpv[3:5]
        part["lru_lambda"] = dpv[5:7]
        part["lru_w_a"] = _diag_blocks(jnp.stack([dwb[0], dwb[2]]))
        part["lru_w_x"] = _diag_blocks(jnp.stack([dwb[1], dwb[3]]))
        dx, dgm, gb = mix_project_backward(x1, mix_norm[l][None], dx, dxr, dgt, dq, dkv, dwout, wfull[l], gb, lay)
        part["mix_norm"] = dgm[0]
        dx, part["ffn1_norm"], gb = ffn_back(x0, ffn1_norm[l][None], dx, pre1, gb, l, 0)
        pieces = [part[n] for n in layer_names]
        if l == 0:
            d_rel_bias = jnp.dot(dbias_total.reshape(N_HEADS, -1), onehot, precision=lax.Precision.HIGHEST).T
            pieces += [d_rel_bias, d_final[0]]
        small_shapes[l] = [p.shape for p in pieces]
        if in_flight is not None:
            gf[in_flight[0]], small_sums[in_flight[0]] = reduce_finish(in_flight[0], in_flight[1], [dx])
        in_flight = (l, reduce_start(l, gb, _pack_rows(pieces, 1024)))
    grad_x = dx[None]

    weights = dict(ffn1_norm=ffn1_norm, ffn1_w_gate=ffn1_w_gate, ffn1_w_up=ffn1_w_up, ffn1_w_down=ffn1_w_down, mix_norm=mix_norm, w_in=w_in, conv_w=conv_w, conv_b=conv_b, lru_w_a=lru_w_a, lru_b_a=lru_b_a, lru_w_x=lru_w_x, lru_b_x=lru_b_x, lru_lambda=lru_lambda, attn_sink=attn_sink, rel_bias=rel_bias, lru_out_norm=lru_out_norm, attn_out_norm=attn_out_norm, w_out=w_out, ffn2_norm=ffn2_norm, ffn2_w_gate=ffn2_w_gate, ffn2_w_up=ffn2_w_up, ffn2_w_down=ffn2_w_down, final_norm=final_norm)
    m_in = dict(ffn1_norm=m_ffn1_norm, ffn1_w_gate=m_ffn1_w_gate, ffn1_w_up=m_ffn1_w_up, ffn1_w_down=m_ffn1_w_down, mix_norm=m_mix_norm, w_in=m_w_in, conv_w=m_conv_w, conv_b=m_conv_b, lru_w_a=m_lru_w_a, lru_b_a=m_lru_b_a, lru_w_x=m_lru_w_x, lru_b_x=m_lru_b_x, lru_lambda=m_lru_lambda, attn_sink=m_attn_sink, rel_bias=m_rel_bias, lru_out_norm=m_lru_out_norm, attn_out_norm=m_attn_out_norm, w_out=m_w_out, ffn2_norm=m_ffn2_norm, ffn2_w_gate=m_ffn2_w_gate, ffn2_w_up=m_ffn2_w_up, ffn2_w_down=m_ffn2_w_down, final_norm=m_final_norm)
    v_in = dict(ffn1_norm=v_ffn1_norm, ffn1_w_gate=v_ffn1_w_gate, ffn1_w_up=v_ffn1_w_up, ffn1_w_down=v_ffn1_w_down, mix_norm=v_mix_norm, w_in=v_w_in, conv_w=v_conv_w, conv_b=v_conv_b, lru_w_a=v_lru_w_a, lru_b_a=v_lru_b_a, lru_w_x=v_lru_w_x, lru_b_x=v_lru_b_x, lru_lambda=v_lru_lambda, attn_sink=v_attn_sink, rel_bias=v_rel_bias, lru_out_norm=v_lru_out_norm, attn_out_norm=v_attn_out_norm, w_out=v_w_out, ffn2_norm=v_ffn2_norm, ffn2_w_gate=v_ffn2_w_gate, ffn2_w_up=v_ffn2_w_up, ffn2_w_down=v_ffn2_w_down, final_norm=v_final_norm)
    order = list(weights)
    large = [(name, m, 0, lay.fh, m % 3 != 2) for m, name in
             enumerate(("ffn1_w_gate", "ffn1_w_up", "ffn1_w_down", "ffn2_w_gate", "ffn2_w_up", "ffn2_w_down"))]
    large += [("w_in", lay.MIX_BLK, 0, lay.ih, True), ("w_out", lay.MIX_BLK, lay.ih, lay.oh, False)]
    stacked = {name: tuple(lax.empty(weights[name].shape, F32) for _ in range(4)) for name, *_ in large}

    def adamw_large(l, deps=()):
        for i, (name, blk, row_off, n_half, transposed) in enumerate(large):
            stacked[name] = adamw_layer(gf[l], blk, row_off, n_half, transposed, l, weights[name], m_in[name], v_in[name],
                                        stacked[name], deps=deps if i == 0 else ())

    last = in_flight[0]
    for l in range(depth):
        if l != last:
            adamw_large(l, deps=(in_flight[1][3],))
    ready = [buf for name, *_ in large for buf in stacked[name]] if depth > 1 else []
    gf[last], small_sums[last] = reduce_finish(last, in_flight[1], [dx] + ready)
    adamw_large(last)

    per_layer = [_unpack_rows(small_sums[l], small_shapes[l]) for l in range(depth)]
    grads = {n: jnp.stack([per_layer[l][i] for l in range(depth)]) for i, n in enumerate(layer_names)}
    grads["rel_bias"], grads["final_norm"] = per_layer[0][len(layer_names):]
    for name in ("conv_w", "lru_b_a", "lru_b_x", "lru_lambda"):
        grads[name] = lax.dynamic_slice_in_dim(grads[name], k_chip * LANE, LANE, axis=2)
    delta, new_m, new_v = {}, {}, {}
    for name, *_ in large:
        grads[name], delta[name], new_m[name], new_v[name] = stacked[name]
    small = [n for n in order if n not in stacked]
    packed = [_pack_rows([src[n] for n in small], 1024) for src in (weights, grads, m_in, v_in)]
    outs = adamw(*packed)
    shapes = [weights[n].shape for n in small]
    for dst, buf in zip((delta, new_m, new_v), outs):
        dst.update(zip(small, _unpack_rows(buf, shapes)))

    return (loss, grad_x, *[grads[n] for n in order], *[delta[n] for n in order],
            *[new_m[n] for n in order], *[new_v[n] for n in order])
```

```python
import functools
import math

import jax
import jax.numpy as jnp
import numpy as np
from jax import lax
from jax.experimental import pallas as pl
from jax.experimental.pallas import tpu as pltpu

BF = jnp.bfloat16
F32 = jnp.float32
SDS = jax.ShapeDtypeStruct
MESH = pl.DeviceIdType.MESH
ANY = pl.BlockSpec(memory_space=pl.ANY)

N_CHIPS = 4
N_HEADS = 8
N_KV_HEADS = 2
KV_GROUP = N_HEADS // N_KV_HEADS
HEAD_DIM = 64
BLOCK = 128
WINDOW = 128
N_BUCKETS = 32
MAX_DISTANCE = 128
LRU_C = 8.0
CONV_WIDTH = 4
LANE = 128
SCAN_SEGMENTS = 8
SCAN_CHAINS = 8
EPS = 1e-6
FFN_RES = 0.5
NEG_INF = -1e30
ADAM_LR = 0.001
ADAM_B1 = 0.9
ADAM_B2 = 0.999
ADAM_EPS = 1e-08
ADAM_WD = 0.01
ADAM_STEP = 10
VMEM_LIMIT = 60000 * 1024
GELU_C = math.sqrt(2.0 / math.pi)


def dot_nn(a, b):
    return lax.dot_general(a, b, (((1,), (0,)), ((), ())), preferred_element_type=F32)


def dot_nt(a, b):
    return lax.dot_general(a, b, (((1,), (1,)), ((), ())), preferred_element_type=F32)


def dot_tn(a, b):
    return lax.dot_general(a, b, (((0,), (0,)), ((), ())), preferred_element_type=F32)


def _cparams(**kw):
    return pltpu.CompilerParams(vmem_limit_bytes=VMEM_LIMIT, **kw)


class Layout:
    MIX_BLK = 6

    def __init__(self, d_model, d_ff, d_in):
        self.fh = d_ff // (2 * N_CHIPS)
        self.ih = d_in // (2 * N_CHIPS)
        self.oh = d_model // (2 * N_CHIPS)
        assert self.ih + self.oh == self.fh, "w_in^T and w_out rows must fill one ffn-sized block"
        self.rows = 7 * self.fh


def _row_chunk(rows, target, step=16):
    best = rows
    for c in range(step, min(rows, target) + 1, step):
        if rows % c == 0:
            best = c
    return best


def _mesh_pos():
    return lax.axis_index("x"), lax.axis_index("y"), lax.axis_index("c")


def _rcopy(src, dst, ssem, rsem, dev):
    return pltpu.make_async_remote_copy(src_ref=src, dst_ref=dst, send_sem=ssem, recv_sem=rsem,
                                        device_id=dev, device_id_type=MESH)


HBM = pl.BlockSpec(memory_space=pltpu.HBM)
SEM = pl.BlockSpec(memory_space=pltpu.SEMAPHORE)
DATAFLOW = pltpu.SideEffectType.DATAFLOW_SIDE_EFFECTING


def _chip_peers():
    x, y, c = _mesh_pos()
    peers = [(1 - x, y), (x, 1 - y), (1 - x, 1 - y)]
    return x, y, c, 2 * x + y, [(px, py, 2 * px + py) for px, py in peers]


def split_start(name, bufs, n, plan):
    nb = len(bufs)

    def body(*refs):
        sends, _ = plan(refs[:nb], refs[nb], refs[nb + 1])
        for cp in sends:
            cp.start()
        refs[-1][...] = jnp.zeros_like(refs[-1])

    out = pl.pallas_call(
        body, name=name,
        out_shape=(pltpu.SemaphoreType.DMA((n,)), pltpu.SemaphoreType.DMA((n,)),
                   *[pltpu.HBM(b.shape, b.dtype) for b in bufs], SDS((8, LANE), F32)),
        in_specs=[HBM] * nb, out_specs=(SEM, SEM, *([HBM] * nb), pl.BlockSpec(memory_space=pltpu.VMEM)),
        input_output_aliases={i: 2 + i for i in range(nb)},
        compiler_params=pltpu.CompilerParams(has_side_effects=DATAFLOW),
    )(*[pltpu.with_memory_space_constraint(b, pltpu.HBM) for b in bufs])
    return out[0], out[1], list(out[2:2 + nb]), out[-1]


def split_wait(name, ssem, rsem, bufs, after, plan):
    nb = len(bufs)

    def body(*refs):
        sends, recvs = plan(refs[:nb], refs[nb], refs[nb + 1])
        for cp in recvs:
            cp.wait_recv()
        for cp in sends:
            cp.wait_send()

    out = pl.pallas_call(
        body, name=name, out_shape=tuple(pltpu.HBM(b.shape, b.dtype) for b in bufs),
        in_specs=[HBM] * nb + [SEM, SEM] + [ANY] * len(after), out_specs=tuple([HBM] * nb),
        input_output_aliases={i: i for i in range(nb)},
        compiler_params=pltpu.CompilerParams(has_side_effects=DATAFLOW),
    )(*bufs, ssem, rsem, *after)
    return list(out)


def gather_plan(refs, ssem, rsem):
    land_ref, = refs
    _, _, c, k, peers = _chip_peers()
    sends = [_rcopy(land_ref.at[k, c], land_ref.at[k, c], ssem.at[j], rsem.at[j], (px, py, c))
             for j, (px, py, _) in enumerate(peers)]
    recvs = [_rcopy(land_ref.at[kp, c], land_ref.at[kp, c], ssem.at[j], rsem.at[j], (px, py, c))
             for j, (px, py, kp) in enumerate(peers)]
    return sends, recvs


def reduce_plan(refs, ssem, rsem):
    cs_ref, ss_ref, p3_ref, sp3_ref = refs
    _, _, c, k, peers = _chip_peers()
    sends, recvs = [], []
    for j, (px, py, kp) in enumerate(peers):
        sends.append(_rcopy(cs_ref.at[kp], p3_ref.at[j], ssem.at[j], rsem.at[j], (px, py, c)))
        recvs.append(_rcopy(cs_ref.at[kp], p3_ref.at[j], ssem.at[j], rsem.at[j], (px, py, c)))
        sends.append(_rcopy(ss_ref, sp3_ref.at[k], ssem.at[3 + j], rsem.at[3 + j], (px, py, c)))
        recvs.append(_rcopy(ss_ref, sp3_ref.at[kp], ssem.at[3 + j], rsem.at[3 + j], (px, py, c)))
    return sends, recvs


def gather_small(sshard):
    def body(s_ref, sf_ref, lsem, ssem, rsem):
        _, _, c, k, peers = _chip_peers()
        own = pltpu.make_async_copy(s_ref, sf_ref.at[k], lsem)
        own.start()
        sends = [_rcopy(s_ref, sf_ref.at[k], ssem.at[j], rsem.at[j], (px, py, c)) for j, (px, py, _) in enumerate(peers)]
        recvs = [_rcopy(s_ref, sf_ref.at[kp], ssem.at[j], rsem.at[j], (px, py, c)) for j, (px, py, kp) in enumerate(peers)]
        for cp in sends:
            cp.start()
        for cp in recvs:
            cp.wait_recv()
        for cp in sends:
            cp.wait_send()
        own.wait()

    return pl.pallas_call(
        body, name="gather_small", out_shape=SDS((N_CHIPS,) + sshard.shape, sshard.dtype),
        in_specs=[ANY], out_specs=ANY,
        scratch_shapes=[pltpu.SemaphoreType.DMA, pltpu.SemaphoreType.DMA((3,)), pltpu.SemaphoreType.DMA((3,))],
    )(sshard)


def gather_pair(land):
    def body(land_in, land_ref, ssem, rsem):
        x, y, c, k, peers = _chip_peers()
        sib = (x, y, 1 - c)
        sends = [_rcopy(land_ref.at[kp, c], land_ref.at[kp, c], ssem.at[j], rsem.at[j], sib) for j, (_, _, kp) in enumerate(peers)]
        recvs = [_rcopy(land_ref.at[kp, 1 - c], land_ref.at[kp, 1 - c], ssem.at[j], rsem.at[j], sib)
                 for j, (_, _, kp) in enumerate(peers)]
        for cp in sends:
            cp.start()
        for cp in recvs:
            cp.wait_recv()
        for cp in sends:
            cp.wait_send()

    return pl.pallas_call(
        body, name="gather_pair", out_shape=SDS(land.shape, land.dtype),
        in_specs=[ANY], out_specs=ANY, input_output_aliases={0: 0},
        scratch_shapes=[pltpu.SemaphoreType.DMA((3,)), pltpu.SemaphoreType.DMA((3,))],
    )(land)


def exchange_pair(gb, sb):
    n, _, rh, d = gb.shape

    def body(gb_ref, sb_ref, p_ref, sp_ref, ssem, rsem):
        x, y, c = _mesh_pos()
        sib = (x, y, 1 - c)
        sends = [_rcopy(gb_ref.at[kk, 1 - c], p_ref.at[kk], ssem.at[kk], rsem.at[kk], sib) for kk in range(n)]
        sends.append(_rcopy(sb_ref, sp_ref, ssem.at[n], rsem.at[n], sib))
        for cp in sends:
            cp.start()
        for cp in sends:
            cp.wait_recv()
        for cp in sends:
            cp.wait_send()

    return pl.pallas_call(
        body, name="exchange_pair",
        out_shape=(SDS((n, rh, d), gb.dtype), SDS(sb.shape, sb.dtype)),
        in_specs=[ANY, ANY], out_specs=(ANY, ANY),
        scratch_shapes=[pltpu.SemaphoreType.DMA((n + 1,)), pltpu.SemaphoreType.DMA((n + 1,))],
    )(gb, sb)


def exchange_final(gf):
    _, rh, d = gf.shape
    nch = 4 if rh % 32 == 0 else 1
    cr = rh // nch

    def body(gf_ref, out_ref, ssem, rsem):
        x, y, c = _mesh_pos()
        sib = (x, y, 1 - c)
        sends = [_rcopy(out_ref.at[c, pl.ds(q * cr, cr)], out_ref.at[c, pl.ds(q * cr, cr)], ssem.at[q], rsem.at[q], sib)
                 for q in range(nch)]
        recvs = [_rcopy(out_ref.at[1 - c, pl.ds(q * cr, cr)], out_ref.at[1 - c, pl.ds(q * cr, cr)], ssem.at[q], rsem.at[q], sib)
                 for q in range(nch)]
        for cp in sends:
            cp.start()
        for cp in recvs:
            cp.wait_recv()
        for cp in sends:
            cp.wait_send()

    return pl.pallas_call(
        body, name="exchange_final",
        out_shape=SDS(gf.shape, gf.dtype),
        in_specs=[ANY], out_specs=ANY, input_output_aliases={0: 0},
        scratch_shapes=[pltpu.SemaphoreType.DMA((nch,)), pltpu.SemaphoreType.DMA((nch,))],
    )(gf)


def pair_sum(pos, gb, p1):
    n, _, rh, d = gb.shape
    cr = _row_chunk(rh, 1024)

    def body(pos_ref, a_ref, b_ref, o_ref):
        o_ref[...] = (a_ref[...].astype(F32) + b_ref[...].astype(F32)).astype(o_ref.dtype)

    return pl.pallas_call(
        body, name="pair_sum", out_shape=SDS((n, rh, d), gb.dtype),
        grid_spec=pltpu.PrefetchScalarGridSpec(
            num_scalar_prefetch=1, grid=(n, rh // cr),
            in_specs=[pl.BlockSpec((None, None, cr, d), lambda kk, r, pos: (kk, pos[1], r, 0)),
                      pl.BlockSpec((None, cr, d), lambda kk, r, pos: (kk, r, 0))],
            out_specs=pl.BlockSpec((None, cr, d), lambda kk, r, pos: (kk, r, 0))),
        compiler_params=_cparams(),
    )(pos, gb, p1)


def chip_sum(pos, cs, p3):
    n, rh, d = cs.shape
    cr = _row_chunk(rh, 512)

    def body(pos_ref, a_ref, b_ref, o_ref):
        acc = a_ref[...].astype(F32)
        for j in range(3):
            acc = acc + b_ref[j].astype(F32)
        o_ref[...] = acc

    return pl.pallas_call(
        body, name="chip_sum", out_shape=SDS((2, rh, d), F32),
        grid_spec=pltpu.PrefetchScalarGridSpec(
            num_scalar_prefetch=1, grid=(rh // cr,),
            in_specs=[pl.BlockSpec((None, cr, d), lambda r, pos: (pos[0], r, 0)),
                      pl.BlockSpec((3, cr, d), lambda r, pos: (0, r, 0))],
            out_specs=pl.BlockSpec((None, cr, d), lambda r, pos: (pos[1], r, 0))),
        compiler_params=_cparams(),
    )(pos, cs, p3)


def small_pair_sum(a, b):
    def body(a_ref, b_ref, o_ref):
        o_ref[...] = a_ref[...] + b_ref[...]

    return pl.pallas_call(body, name="small_pair_sum", out_shape=SDS(a.shape, a.dtype),
                          compiler_params=_cparams())(a, b)


def small_chip_sum(pos, own, p):
    ns, w = own.shape

    def body(pos_ref, own_ref, p0, p1, p2, p3, o_ref):
        k = pos_ref[0]
        acc = None
        for chip, ref in enumerate((p0, p1, p2, p3)):
            term = jnp.where(k == chip, own_ref[...], ref[...])
            acc = term if acc is None else acc + term
        o_ref[...] = acc

    def slot(chip):
        return pl.BlockSpec((None, ns, w), lambda i, pos: (jnp.where(pos[0] == chip, (chip + 1) % N_CHIPS, chip), 0, 0))

    return pl.pallas_call(
        body, name="small_chip_sum", out_shape=SDS(own.shape, own.dtype),
        grid_spec=pltpu.PrefetchScalarGridSpec(
            num_scalar_prefetch=1, grid=(1,),
            in_specs=[pl.BlockSpec((ns, w), lambda i, pos: (0, 0))] + [slot(chip) for chip in range(N_CHIPS)],
            out_specs=pl.BlockSpec((ns, w), lambda i, pos: (0, 0))),
        compiler_params=_cparams(),
    )(pos, own, p, p, p, p)


def _rms(x, g):
    rs = lax.rsqrt(jnp.mean(x * x, axis=-1, keepdims=True) + EPS)
    xh = x * rs
    return xh, rs, xh * g


def _rms_bwd(dy, xh, rs, g):
    dxh = dy * g
    dx = rs * (dxh - xh * jnp.mean(dxh * xh, axis=-1, keepdims=True))
    return dx, dy * xh


def _gelu(x):
    t = jnp.tanh(GELU_C * (x + 0.044715 * x * x * x))
    return 0.5 * x * (1.0 + t), t


def _gelu_grad(x, t):
    return 0.5 * (1.0 + t) + 0.5 * x * (1.0 - t * t) * GELU_C * (1.0 + 3.0 * 0.044715 * x * x)


def _shift_rows(v, s, n):
    if s == 0:
        return v
    t = lax.broadcasted_iota(jnp.int32, v.shape, 0)
    rolled = pltpu.roll(v, (-s) % n, 0)
    return jnp.where((t + s >= 0) & (t + s < n), rolled, 0.0)


def _scan_rows(a_ref, u_ref, h_ref, acum_ref, reverse):
    s_len, w = a_ref.shape
    chains = max(1, min(SCAN_CHAINS, s_len // (8 * SCAN_SEGMENTS)))
    nseg = SCAN_SEGMENTS * chains
    seg = s_len // nseg

    def step(j, carry):
        jj = (seg - 1 - j) if reverse else j
        out = []
        for c, (h, acc) in enumerate(carry):
            idx = pl.ds(c * SCAN_SEGMENTS * seg + jj, SCAN_SEGMENTS, stride=seg)
            a = a_ref[idx, :]
            h = a * h + u_ref[idx, :]
            acc = a * acc
            h_ref[idx, :] = h
            acum_ref[idx, :] = acc
            out.append((h, acc))
        return tuple(out)

    init = tuple((jnp.zeros((SCAN_SEGMENTS, w), F32), jnp.ones((SCAN_SEGMENTS, w), F32)) for _ in range(chains))
    ends = lax.fori_loop(0, seg, step, init, unroll=min(8, seg))
    order = range(nseg - 2, -1, -1) if reverse else range(1, nseg)
    inflow = jnp.zeros((1, w), F32)
    for s in order:
        src = s + 1 if reverse else s - 1
        h, acc = ends[src // SCAN_SEGMENTS]
        r = src % SCAN_SEGMENTS
        inflow = h[r:r + 1, :] + acc[r:r + 1, :] * inflow
        rows = pl.ds(s * seg, seg)
        h_ref[rows, :] = h_ref[rows, :] + acum_ref[rows, :] * inflow


def _w_spec(rows_half, d, blk):
    return pl.BlockSpec((N_CHIPS, 2, rows_half, d), lambda *_: (0, 0, blk, 0), pipeline_mode=pl.Buffered(1))


def ffn_forward(x, gain, wfull, lay, which, deps=(), tm=512):
    s_len, d = x.shape
    tm = min(tm, s_len)
    f = 8 * lay.fh
    fc = f // 2

    def body(x_ref, g_ref, wg_ref, wu_ref, wd_ref, *rest):
        o_ref, gate_ref, up_ref = rest[len(deps):]
        x = x_ref[...]
        _, _, hn = _rms(x, g_ref[...])
        h = hn.astype(BF)
        y = jnp.zeros((tm, d), F32)
        for part in range(2):
            cols = slice(part * fc, (part + 1) * fc)
            gate = dot_nt(h, wg_ref[...].reshape(f, d)[cols])
            up = dot_nt(h, wu_ref[...].reshape(f, d)[cols])
            act = (gate * jax.nn.sigmoid(gate) * up).astype(BF)
            y = y + dot_nn(act, wd_ref[...].reshape(f, d)[cols])
            gate_ref[:, cols] = gate.astype(BF)
            up_ref[:, cols] = up.astype(BF)
        o_ref[...] = x + FFN_RES * y

    row = pl.BlockSpec((tm, d), lambda i: (i, 0))
    wide = pl.BlockSpec((tm, f), lambda i: (i, 0))
    return pl.pallas_call(
        body, name="ffn_forward", grid=(s_len // tm,),
        out_shape=(SDS((s_len, d), F32), SDS((s_len, f), BF), SDS((s_len, f), BF)),
        in_specs=[row, pl.BlockSpec((1, d), lambda i: (0, 0))]
        + [_w_spec(lay.fh, d, 3 * which + m) for m in range(3)] + [ANY] * len(deps),
        out_specs=(row, wide, wide), compiler_params=_cparams(),
    )(x, gain, wfull, wfull, wfull, *deps)


def ffn_backward_dx(x, gain, dout, gate_bf, up_bf, wfull, lay, which, deps=(), tm=256):
    s_len, d = x.shape
    tm = min(tm, s_len)
    f = 8 * lay.fh
    fc = f // 2
    nt = s_len // tm

    def body(x_ref, g_ref, do_ref, gate_ref, up_ref, wg_ref, wu_ref, wd_ref, *rest):
        dx_ref, dg_ref, dgate_ref, dup_ref, act_ref, h_ref, df_ref = rest[len(deps):]
        x = x_ref[...]
        g = g_ref[...]
        xh, rs, hn = _rms(x, g)
        h = hn.astype(BF)
        do = do_ref[...]
        df = (FFN_RES * do).astype(BF)
        dh = jnp.zeros((tm, d), F32)
        for part in range(2):
            cols = slice(part * fc, (part + 1) * fc)
            wg = wg_ref[...].reshape(f, d)[cols]
            wu = wu_ref[...].reshape(f, d)[cols]
            gate = gate_ref[:, cols].astype(F32)
            up = up_ref[:, cols].astype(F32)
            sg = jax.nn.sigmoid(gate)
            silu = gate * sg
            dact = dot_nt(df, wd_ref[...].reshape(f, d)[cols])
            dup = (dact * silu).astype(BF)
            dgate = (dact * up * (sg * (1.0 + gate * (1.0 - sg)))).astype(BF)
            dh = dh + dot_nn(dgate, wg) + dot_nn(dup, wu)
            dgate_ref[:, cols] = dgate
            dup_ref[:, cols] = dup
            act_ref[:, cols] = (silu * up).astype(BF)
        dxn, dgrow = _rms_bwd(dh, xh, rs, g)
        dx_ref[...] = do + dxn

        @pl.when(pl.program_id(0) == 0)
        def _():
            dg_ref[...] = jnp.zeros_like(dg_ref)

        dg_ref[...] += jnp.sum(dgrow, axis=0, keepdims=True)
        h_ref[...] = h
        df_ref[...] = df

    row = pl.BlockSpec((tm, d), lambda i: (i, 0))
    wide = pl.BlockSpec((tm, f), lambda i: (i, 0))
    vec = pl.BlockSpec((1, d), lambda i: (0, 0))
    return pl.pallas_call(
        body, name="ffn_backward_dx", grid=(nt,),
        out_shape=(SDS((s_len, d), F32), SDS((1, d), F32), SDS((s_len, f), BF), SDS((s_len, f), BF),
                   SDS((s_len, f), BF), SDS((s_len, d), BF), SDS((s_len, d), BF)),
        in_specs=[row, vec, row, wide, wide] + [_w_spec(lay.fh, d, 3 * which + m) for m in range(3)] + [ANY] * len(deps),
        out_specs=(row, vec, wide, wide, wide, row, row), compiler_params=_cparams(),
    )(x, gain, dout, gate_bf, up_bf, wfull, wfull, wfull, *deps)


def weight_grad_tn(a, b, gb, lay, blk, tk=512):
    s_len, f = a.shape
    tk = min(tk, s_len)
    d = b.shape[1]
    fc = f // 2
    nk = s_len // tk

    def body(a_ref, b_ref, gb_ref, o_ref, acc):
        kt = pl.program_id(1)

        @pl.when(kt == 0)
        def _():
            acc[...] = jnp.zeros_like(acc)

        acc[...] += dot_tn(a_ref[...], b_ref[...])

        @pl.when(kt == nk - 1)
        def _():
            for p in range(2):
                for q in range(2):
                    o_ref[p, q] = acc[pl.ds((2 * p + q) * lay.fh, lay.fh), :].astype(o_ref.dtype)

    return pl.pallas_call(
        body, name="weight_grad_tn", grid=(2, nk), out_shape=SDS(gb.shape, gb.dtype),
        in_specs=[pl.BlockSpec((tk, fc), lambda j, kt: (kt, j)), pl.BlockSpec((tk, d), lambda j, kt: (kt, 0)), ANY],
        out_specs=pl.BlockSpec((2, 2, lay.fh, d), lambda j, kt: (j, 0, blk, 0)),
        scratch_shapes=[pltpu.VMEM((fc, d), F32)],
        input_output_aliases={2: 0}, compiler_params=_cparams(),
    )(a, b, gb)


def _lane_blocks(v):
    return [v[:, j * LANE:(j + 1) * LANE] for j in range(v.shape[1] // LANE)]


def _join_lane_blocks(ref):
    return jnp.concatenate([ref[j] for j in range(ref.shape[0])], axis=1)


def _cbm_spec(nblk, rows, first=0):
    return pl.BlockSpec((nblk, rows, LANE), lambda i: (first // nblk, i, 0))


def mix_project(x, gain, wfull, lay, tm=512):
    s_len, d = x.shape
    tm = min(tm, s_len)
    d_in = 8 * lay.ih
    ncol = d_in // LANE

    def body(x_ref, g_ref, w_ref, o_ref):
        _, _, hn = _rms(x_ref[...], g_ref[...])
        res = dot_nt(hn.astype(BF), w_ref[:, :, :lay.ih, :].reshape(d_in, d))
        for j, piece in enumerate(_lane_blocks(res)):
            o_ref[j] = piece

    return pl.pallas_call(
        body, name="mix_project", grid=(s_len // tm,), out_shape=SDS((ncol, s_len, LANE), F32),
        in_specs=[pl.BlockSpec((tm, d), lambda i: (i, 0)), pl.BlockSpec((1, d), lambda i: (0, 0)),
                  _w_spec(lay.fh, d, lay.MIX_BLK)],
        out_specs=_cbm_spec(ncol, tm), compiler_params=_cparams(),
    )(x, gain, wfull)


def mix_project_backward(x, gain, dout, dxr, dgt, dq, dkv, dwout, wfull, gb, lay, tm=512):
    s_len, d = x.shape
    tm = min(tm, s_len)
    d_in = 8 * lay.ih
    nt = s_len // tm
    kvw = dkv.shape[1]

    def body(x_ref, g_ref, do_ref, dxr_ref, dgt_ref, dq_ref, dkv_ref, dwo_ref, w_ref, gb_ref, dx_ref, dg_ref, o_ref, acc):
        i = pl.program_id(0)
        g = g_ref[...]
        xh, rs, hn = _rms(x_ref[...], g)
        h = hn.astype(BF)
        dp = jnp.concatenate([_join_lane_blocks(dxr_ref), _join_lane_blocks(dgt_ref), _join_lane_blocks(dq_ref),
                              dkv_ref[...]], axis=1).astype(BF)
        dh = dot_nn(dp, w_ref[:, :, :lay.ih, :].reshape(d_in, d))
        dxn, dgrow = _rms_bwd(dh, xh, rs, g)
        dx_ref[...] = do_ref[...] + dxn

        @pl.when(i == 0)
        def _():
            dg_ref[...] = jnp.zeros_like(dg_ref)
            acc[...] = jnp.zeros_like(acc)

        dg_ref[...] += jnp.sum(dgrow, axis=0, keepdims=True)
        acc[...] += dot_tn(dp, h)

        @pl.when(i == nt - 1)
        def _():
            for p in range(N_CHIPS):
                for q in range(2):
                    o_ref[p, q, :lay.ih, :] = acc[pl.ds((2 * p + q) * lay.ih, lay.ih), :].astype(o_ref.dtype)
            o_ref[:, :, lay.ih:, :] = dwo_ref[...]

    row = pl.BlockSpec((tm, d), lambda i: (i, 0))
    vec = pl.BlockSpec((1, d), lambda i: (0, 0))
    return pl.pallas_call(
        body, name="mix_project_backward", grid=(nt,),
        out_shape=(SDS((s_len, d), F32), SDS((1, d), F32), SDS(gb.shape, gb.dtype)),
        in_specs=[row, vec, row, _cbm_spec(dxr.shape[0], tm), _cbm_spec(dgt.shape[0], tm),
                  _cbm_spec(dq.shape[0], tm), pl.BlockSpec((tm, kvw), lambda i: (i, 0)),
                  pl.BlockSpec(dwout.shape, lambda i: (0, 0, 0, 0)), _w_spec(lay.fh, d, lay.MIX_BLK), ANY],
        out_specs=(row, vec, pl.BlockSpec((N_CHIPS, 2, lay.fh, d), lambda i: (0, 0, lay.MIX_BLK, 0))),
        scratch_shapes=[pltpu.VMEM((d_in, d), F32)],
        input_output_aliases={9: 2}, compiler_params=_cparams(),
    )(x, gain, dout, dxr, dgt, dq, dkv, dwout, wfull, gb)


def _lru_gates(xc, wb_ref, pv_ref, direction):
    xcb = xc.astype(BF)
    r = jax.nn.sigmoid(dot_nn(xcb, wb_ref[2 * direction]) + pv_ref[1 + direction:2 + direction, :])
    i = jax.nn.sigmoid(dot_nn(xcb, wb_ref[2 * direction + 1]) + pv_ref[3 + direction:4 + direction, :])
    lam = pv_ref[5 + direction:6 + direction, :]
    sp = jnp.maximum(-lam, 0.0) + jnp.log(1.0 + jnp.exp(-jnp.abs(lam)))
    a = jnp.exp(-LRU_C * sp * r)
    mult = jnp.sqrt(1.0 - a * a)
    return xcb, r, i, a, mult, sp


def _conv_rows(xr, cv_ref, bias, n):
    acc = bias + cv_ref[0:1, :] * _shift_rows(xr, -2, n)
    for j in range(1, CONV_WIDTH):
        acc = acc + cv_ref[j:j + 1, :] * _shift_rows(xr, j - 2, n)
    return acc


def lru_forward(proj, cvec, pvec, wblk, lw, deps=(), ch=512):
    s_len = proj.shape[1]
    ncb = lw // LANE
    ch = min(ch, s_len)
    nchunk = s_len // ch

    def body(xr_ref, gt_ref, cv_ref, pv_ref, wb_ref, *rest):
        y_ref, hs_ref, xc_s, a_s, u_s, acum_s = rest[len(deps):]
        xc_s[...] = _conv_rows(xr_ref[...], cv_ref, pv_ref[0:1, :], s_len)
        for direction in range(2):
            def fill(ci, _):
                rows = pl.ds(pl.multiple_of(ci * ch, ch), ch)
                xc = xc_s[rows, :]
                _, _, i, a, mult, _ = _lru_gates(xc, wb_ref, pv_ref, direction)
                a_s[rows, :] = a
                u_s[rows, :] = mult * (i * xc)
                return 0

            lax.fori_loop(0, nchunk, fill, 0)
            _scan_rows(a_s, u_s, hs_ref.at[direction], acum_s, reverse=direction == 1)

        def out(ci, _):
            rows = pl.ds(pl.multiple_of(ci * ch, ch), ch)
            gl, _ = _gelu(gt_ref[rows, :])
            y_ref[rows, :] = gl * (hs_ref[0, rows, :] + hs_ref[1, rows, :])
            return 0

        lax.fori_loop(0, nchunk, out, 0)

    col = lambda off: pl.BlockSpec((None, s_len, LANE), lambda cb: (off + cb, 0, 0))
    return pl.pallas_call(
        body, name="lru_forward", grid=(ncb,),
        out_shape=(SDS((ncb, s_len, LANE), F32), SDS((2, ncb, s_len, LANE), F32)),
        in_specs=[col(0), col(ncb), pl.BlockSpec((8, LANE), lambda cb: (0, cb)), pl.BlockSpec((8, LANE), lambda cb: (0, cb)),
                  pl.BlockSpec((4, None, LANE, LANE), lambda cb: (0, cb, 0, 0))] + [ANY] * len(deps),
        out_specs=(col(0), pl.BlockSpec((2, None, s_len, LANE), lambda cb: (0, cb, 0, 0))),
        scratch_shapes=[pltpu.VMEM((s_len, LANE), F32)] * 4, compiler_params=_cparams(),
    )(proj, proj, cvec, pvec, wblk, *deps)


def lru_backward(proj, hs, dy, cvec, pvec, wblk, lw, ch=512):
    s_len = proj.shape[1]
    ncb = lw // LANE
    ch = min(ch, s_len)
    nchunk = s_len // ch

    def body(xr_ref, gt_ref, hs_ref, dy_ref, cv_ref, pv_ref, wb_ref, dxr_ref, dgt_ref, dcv_ref, dpv_ref, dwb_ref,
             xc_s, a_s, dh_s, lam_s, hp_s, dxc_s, acum_s):
        xr = xr_ref[...]
        xc_s[...] = _conv_rows(xr, cv_ref, pv_ref[0:1, :], s_len)
        dxc_s[...] = jnp.zeros_like(dxc_s)
        dpv_ref[...] = jnp.zeros_like(dpv_ref)
        dwb_ref[...] = jnp.zeros_like(dwb_ref)

        def head(ci, _):
            rows = pl.ds(pl.multiple_of(ci * ch, ch), ch)
            gt = gt_ref[rows, :]
            gl, t = _gelu(gt)
            dy = dy_ref[rows, :]
            dh_s[rows, :] = dy * gl
            dgt_ref[rows, :] = dy * (hs_ref[0, rows, :] + hs_ref[1, rows, :]) * _gelu_grad(gt, t)
            return 0

        lax.fori_loop(0, nchunk, head, 0)

        for direction in range(2):
            def fill(ci, _):
                rows = pl.ds(pl.multiple_of(ci * ch, ch), ch)
                _, _, _, a, _, _ = _lru_gates(xc_s[rows, :], wb_ref, pv_ref, direction)
                a_s[rows, :] = a
                return 0

            lax.fori_loop(0, nchunk, fill, 0)
            toward = 1 if direction == 0 else -1
            hp_s[...] = _shift_rows(a_s[...], toward, s_len)
            _scan_rows(hp_s, dh_s, lam_s, acum_s, reverse=direction == 0)
            hp_s[...] = _shift_rows(hs_ref[direction], -toward, s_len)

            def grads(ci, _):
                rows = pl.ds(pl.multiple_of(ci * ch, ch), ch)
                xc = xc_s[rows, :]
                xcb, r, i, a, mult, sp = _lru_gates(xc, wb_ref, pv_ref, direction)
                du = lam_s[rows, :]
                da = du * hp_s[rows, :]
                dmult = du * i * xc
                di = du * mult * xc
                dlog_a = (da - dmult * a / mult) * a
                dr = dlog_a * (-LRU_C * sp)
                dza = dr * r * (1.0 - r)
                dzx = di * i * (1.0 - i)
                dzab = dza.astype(BF)
                dzxb = dzx.astype(BF)
                dxc_s[rows, :] += (du * mult * i + dot_nt(dzab, wb_ref[2 * direction])
                                   + dot_nt(dzxb, wb_ref[2 * direction + 1]))
                dwb_ref[2 * direction] += dot_tn(xcb, dzab)
                dwb_ref[2 * direction + 1] += dot_tn(xcb, dzxb)
                dpv_ref[1 + direction:2 + direction, :] += jnp.sum(dza, axis=0, keepdims=True)
                dpv_ref[3 + direction:4 + direction, :] += jnp.sum(dzx, axis=0, keepdims=True)
                dpv_ref[5 + direction:6 + direction, :] += jnp.sum(dlog_a * (-LRU_C * r), axis=0, keepdims=True)
                return 0

            lax.fori_loop(0, nchunk, grads, 0)

        for direction in range(2):
            lam = pv_ref[5 + direction:6 + direction, :]
            dpv_ref[5 + direction:6 + direction, :] = dpv_ref[5 + direction:6 + direction, :] * (-jax.nn.sigmoid(-lam))
        dxc = dxc_s[...]
        dpv_ref[0:1, :] = jnp.sum(dxc, axis=0, keepdims=True)
        dxr = cv_ref[0:1, :] * _shift_rows(dxc, 2, s_len)
        for j in range(1, CONV_WIDTH):
            dxr = dxr + cv_ref[j:j + 1, :] * _shift_rows(dxc, 2 - j, s_len)
        dxr_ref[...] = dxr
        dcv_ref[...] = jnp.zeros_like(dcv_ref)
        for j in range(CONV_WIDTH):
            dcv_ref[j:j + 1, :] = jnp.sum(dxc * _shift_rows(xr, j - 2, s_len), axis=0, keepdims=True)

    col = lambda off: pl.BlockSpec((None, s_len, LANE), lambda cb: (off + cb, 0, 0))
    own = col(0)
    small = pl.BlockSpec((8, LANE), lambda cb: (0, cb))
    wspec = pl.BlockSpec((4, None, LANE, LANE), lambda cb: (0, cb, 0, 0))
    return pl.pallas_call(
        body, name="lru_backward", grid=(ncb,),
        out_shape=(SDS((ncb, s_len, LANE), F32), SDS((ncb, s_len, LANE), F32), SDS((8, lw), F32), SDS((8, lw), F32),
                   SDS(wblk.shape, F32)),
        in_specs=[col(0), col(ncb), pl.BlockSpec((2, None, s_len, LANE), lambda cb: (0, cb, 0, 0)), own, small, small, wspec],
        out_specs=(own, own, small, small, wspec),
        scratch_shapes=[pltpu.VMEM((s_len, LANE), F32)] * 7, compiler_params=_cparams(),
    )(proj, proj, hs, dy, cvec, pvec, wblk)


def _attn_specs(s_len, lw, att):
    nb = s_len // BLOCK
    kcol = (2 * lw + att) // LANE
    prev = lambda n: jnp.maximum(n - 1, 0)
    nxt = lambda n: jnp.minimum(n + 1, nb - 1)
    q = _cbm_spec(att // LANE, BLOCK, first=2 * lw // LANE)
    ks = [pl.BlockSpec((None, BLOCK, LANE), lambda n, f=f: (kcol, f(n), 0)) for f in (prev, lambda n: n, nxt)]
    vs = [pl.BlockSpec((None, BLOCK, LANE), lambda n, f=f: (kcol + 1, f(n), 0)) for f in (prev, lambda n: n, nxt)]
    return q, ks, vs


HEADS_PER_LANE_BLOCK = LANE // HEAD_DIM


def _stack_heads(v, kh):
    pieces = []
    for g in range(KV_GROUP):
        blk, sub = divmod(kh * KV_GROUP + g, HEADS_PER_LANE_BLOCK)
        pieces.append(v[blk][:, sub * HEAD_DIM:(sub + 1) * HEAD_DIM])
    return jnp.concatenate(pieces, axis=0)


def _unstack_heads(groups):
    heads = [grp[g * BLOCK:(g + 1) * BLOCK] for grp in groups for g in range(KV_GROUP)]
    return [jnp.concatenate(heads[b * HEADS_PER_LANE_BLOCK:(b + 1) * HEADS_PER_LANE_BLOCK], axis=1)
            for b in range(len(heads) // HEADS_PER_LANE_BLOCK)]


def _key_exists(n, nb):
    j = lax.broadcasted_iota(jnp.int32, (1, 3 * BLOCK), 1)
    return ((n > 0) | (j >= BLOCK)) & ((n < nb - 1) | (j < 2 * BLOCK))


def _attn_probs(qs, kcat, bias_g, sink_g, key_ok):
    logits = jnp.where(key_ok, dot_nt(qs, kcat) + bias_g, NEG_INF)
    m = jnp.maximum(jnp.max(logits, axis=-1, keepdims=True), sink_g)
    p = jnp.exp(logits - m)
    es = jnp.exp(sink_g - m)
    inv = 1.0 / (jnp.sum(p, axis=-1, keepdims=True) + es)
    return p * inv, es * inv


def attention_forward(proj, bias, sink, lw, att):
    s_len = proj.shape[1]
    nb = s_len // BLOCK
    q_spec, k_specs, v_specs = _attn_specs(s_len, lw, att)

    def body(q_ref, kp_ref, kc_ref, kn_ref, vp_ref, vc_ref, vn_ref, b_ref, s_ref, o_ref):
        n = pl.program_id(0)
        q = q_ref[...]
        key_ok = _key_exists(n, nb)
        kall = jnp.concatenate([kp_ref[...], kc_ref[...], kn_ref[...]], axis=0).astype(BF)
        vall = jnp.concatenate([vp_ref[...], vc_ref[...], vn_ref[...]], axis=0).astype(BF)
        outs = []
        for kh in range(N_KV_HEADS):
            grp = slice(kh * KV_GROUP * BLOCK, (kh + 1) * KV_GROUP * BLOCK)
            qs = (_stack_heads(q, kh) * (HEAD_DIM ** -0.5)).astype(BF)
            bias_g = b_ref[kh * KV_GROUP:(kh + 1) * KV_GROUP].reshape(KV_GROUP * BLOCK, 3 * BLOCK)
            p, _ = _attn_probs(qs, kall[:, kh * HEAD_DIM:(kh + 1) * HEAD_DIM], bias_g, s_ref[grp, 0:1], key_ok)
            outs.append(dot_nn(p.astype(BF), vall[:, kh * HEAD_DIM:(kh + 1) * HEAD_DIM]))
        for b, piece in enumerate(_unstack_heads(outs)):
            o_ref[b] = piece

    return pl.pallas_call(
        body, name="attention_forward", grid=(nb,), out_shape=SDS((att // LANE, s_len, LANE), F32),
        in_specs=[q_spec] + k_specs + v_specs
        + [pl.BlockSpec(bias.shape, lambda n: (0, 0, 0)), pl.BlockSpec(sink.shape, lambda n: (0, 0))],
        out_specs=_cbm_spec(att // LANE, BLOCK), compiler_params=_cparams(),
    )(proj, proj, proj, proj, proj, proj, proj, bias, sink)


def attention_backward(proj, y_att, dy, bias, sink, lw, att):
    s_len = proj.shape[1]
    nb = s_len // BLOCK
    kvw = N_KV_HEADS * HEAD_DIM
    q_spec, k_specs, v_specs = _attn_specs(s_len, lw, att)

    def body(q_ref, kp_ref, kc_ref, kn_ref, vp_ref, vc_ref, vn_ref, o_ref, do_ref, b_ref, s_ref,
             dq_ref, dkv_ref, db_ref, ds_ref):
        n = pl.program_id(0)

        @pl.when(n == 0)
        def _():
            dkv_ref[...] = jnp.zeros_like(dkv_ref)
            db_ref[...] = jnp.zeros_like(db_ref)
            ds_ref[...] = jnp.zeros_like(ds_ref)

        q = q_ref[...]
        o = o_ref[...]
        do = do_ref[...]
        kall = jnp.concatenate([kp_ref[...], kc_ref[...], kn_ref[...]], axis=0).astype(BF)
        vall = jnp.concatenate([vp_ref[...], vc_ref[...], vn_ref[...]], axis=0).astype(BF)
        key_ok = _key_exists(n, nb)
        dqs, dks, dvs = [], [], []
        for kh in range(N_KV_HEADS):
            heads = slice(kh * KV_GROUP, (kh + 1) * KV_GROUP)
            grp = slice(kh * KV_GROUP * BLOCK, (kh + 1) * KV_GROUP * BLOCK)
            kcat = kall[:, kh * HEAD_DIM:(kh + 1) * HEAD_DIM]
            vcat = vall[:, kh * HEAD_DIM:(kh + 1) * HEAD_DIM]
            qs = (_stack_heads(q, kh) * (HEAD_DIM ** -0.5)).astype(BF)
            bias_g = b_ref[heads].reshape(KV_GROUP * BLOCK, 3 * BLOCK)
            p, ps = _attn_probs(qs, kcat, bias_g, s_ref[grp, 0:1], key_ok)
            dos = _stack_heads(do, kh)
            dosb = dos.astype(BF)
            delta = jnp.sum(dos * _stack_heads(o, kh), axis=-1, keepdims=True)
            dlog = p * (dot_nt(dosb, vcat) - delta)
            dlogb = dlog.astype(BF)
            db_ref[heads] += dlog.reshape(KV_GROUP, BLOCK, 3 * BLOCK)
            dsink = -ps * delta
            for g in range(KV_GROUP):
                h = kh * KV_GROUP + g
                part = jnp.sum(dsink[g * BLOCK:(g + 1) * BLOCK], axis=0, keepdims=True)
                ds_ref[h:h + 1, :] += jnp.broadcast_to(part, (1, LANE))
            dqs.append(dot_nn(dlogb, kcat) * (HEAD_DIM ** -0.5))
            dks.append(dot_tn(dlogb, qs))
            dvs.append(dot_tn(p.astype(BF), dosb))
        for b, piece in enumerate(_unstack_heads(dqs)):
            dq_ref[b] = piece
        dkv = jnp.concatenate(dks + dvs, axis=1)
        starts = [jnp.maximum(n - 1, 0), n, jnp.minimum(n + 1, nb - 1)]
        for b, st in enumerate(starts):
            rows = pl.ds(pl.multiple_of(st * BLOCK, BLOCK), BLOCK)
            dkv_ref[rows, :] += dkv[b * BLOCK:(b + 1) * BLOCK, :]

    blk = _cbm_spec(att // LANE, BLOCK)
    return pl.pallas_call(
        body, name="attention_backward", grid=(nb,),
        out_shape=(SDS((att // LANE, s_len, LANE), F32), SDS((s_len, 2 * kvw), F32), SDS(bias.shape, F32),
                   SDS((N_HEADS, LANE), F32)),
        in_specs=[q_spec] + k_specs + v_specs
        + [blk, blk, pl.BlockSpec(bias.shape, lambda n: (0, 0, 0)), pl.BlockSpec(sink.shape, lambda n: (0, 0))],
        out_specs=(blk, pl.BlockSpec((s_len, 2 * kvw), lambda n: (0, 0)),
                   pl.BlockSpec(bias.shape, lambda n: (0, 0, 0)), pl.BlockSpec((N_HEADS, LANE), lambda n: (0, 0))),
        compiler_params=_cparams(),
    )(proj, proj, proj, proj, proj, proj, proj, y_att, dy, bias, sink)


def mix_output(x, y_rec, y_att, g_rec, g_att, wfull, lay, tm=512):
    s_len, d = x.shape
    tm = min(tm, s_len)
    lw = y_rec.shape[0] * LANE
    att = y_att.shape[0] * LANE

    def body(x_ref, yr_ref, ya_ref, gr_ref, ga_ref, w_ref, o_ref):
        _, _, nr = _rms(_join_lane_blocks(yr_ref), gr_ref[...])
        _, _, na = _rms(_join_lane_blocks(ya_ref), ga_ref[...])
        y = jnp.concatenate([nr, na], axis=1).astype(BF)
        o_ref[...] = x_ref[...] + dot_nn(y, w_ref[:, :, lay.ih:, :].reshape(d, d))

    row = pl.BlockSpec((tm, d), lambda i: (i, 0))
    return pl.pallas_call(
        body, name="mix_output", grid=(s_len // tm,), out_shape=SDS((s_len, d), F32),
        in_specs=[row, _cbm_spec(lw // LANE, tm), _cbm_spec(att // LANE, tm),
                  pl.BlockSpec((1, lw), lambda i: (0, 0)), pl.BlockSpec((1, att), lambda i: (0, 0)),
                  _w_spec(lay.fh, d, lay.MIX_BLK)],
        out_specs=row, compiler_params=_cparams(),
    )(x, y_rec, y_att, g_rec, g_att, wfull)


def mix_output_backward(dout, y_rec, y_att, g_rec, g_att, wfull, lay, tm=512):
    s_len, d = dout.shape
    tm = min(tm, s_len)
    lw = y_rec.shape[0] * LANE
    att = y_att.shape[0] * LANE
    nt = s_len // tm

    def body(do_ref, yr_ref, ya_ref, gr_ref, ga_ref, w_ref, dyr_ref, dya_ref, dgr_ref, dga_ref, o_ref, acc):
        i = pl.program_id(0)
        gr = gr_ref[...]
        ga = ga_ref[...]
        xhr, rsr, nr = _rms(_join_lane_blocks(yr_ref), gr)
        xha, rsa, na = _rms(_join_lane_blocks(ya_ref), ga)
        y = jnp.concatenate([nr, na], axis=1).astype(BF)
        dob = do_ref[...].astype(BF)
        dy = dot_nt(dob, w_ref[:, :, lay.ih:, :].reshape(d, d))
        dyr, dgr_row = _rms_bwd(dy[:, :lw], xhr, rsr, gr)
        dya, dga_row = _rms_bwd(dy[:, lw:], xha, rsa, ga)
        for j, piece in enumerate(_lane_blocks(dyr)):
            dyr_ref[j] = piece
        for j, piece in enumerate(_lane_blocks(dya)):
            dya_ref[j] = piece

        @pl.when(i == 0)
        def _():
            dgr_ref[...] = jnp.zeros_like(dgr_ref)
            dga_ref[...] = jnp.zeros_like(dga_ref)
            acc[...] = jnp.zeros_like(acc)

        dgr_ref[...] += jnp.sum(dgr_row, axis=0, keepdims=True)
        dga_ref[...] += jnp.sum(dga_row, axis=0, keepdims=True)
        acc[...] += dot_tn(y, dob)

        @pl.when(i == nt - 1)
        def _():
            for p in range(N_CHIPS):
                for q in range(2):
                    o_ref[p, q] = acc[pl.ds((2 * p + q) * lay.oh, lay.oh), :].astype(o_ref.dtype)

    row = pl.BlockSpec((tm, d), lambda i: (i, 0))
    return pl.pallas_call(
        body, name="mix_output_backward", grid=(nt,),
        out_shape=(SDS(y_rec.shape, F32), SDS(y_att.shape, F32), SDS((1, lw), F32), SDS((1, att), F32),
                   SDS((N_CHIPS, 2, lay.oh, d), BF)),
        in_specs=[row, _cbm_spec(lw // LANE, tm), _cbm_spec(att // LANE, tm),
                  pl.BlockSpec((1, lw), lambda i: (0, 0)), pl.BlockSpec((1, att), lambda i: (0, 0)),
                  _w_spec(lay.fh, d, lay.MIX_BLK)],
        out_specs=(_cbm_spec(lw // LANE, tm), _cbm_spec(att // LANE, tm),
                   pl.BlockSpec((1, lw), lambda i: (0, 0)), pl.BlockSpec((1, att), lambda i: (0, 0)),
                   pl.BlockSpec((N_CHIPS, 2, lay.oh, d), lambda i: (0, 0, 0, 0))),
        scratch_shapes=[pltpu.VMEM((d, d), F32)], compiler_params=_cparams(),
    )(dout, y_rec, y_att, g_rec, g_att, wfull)


def loss_head(x, gain, target, tm=512):
    s_len, d = x.shape
    tm = min(tm, s_len)

    def body(x_ref, g_ref, t_ref, dx_ref, dg_ref, loss_ref):
        g = g_ref[...]
        xh, rs, y = _rms(x_ref[...], g)
        err = y - t_ref[...]

        @pl.when(pl.program_id(0) == 0)
        def _():
            dg_ref[...] = jnp.zeros_like(dg_ref)
            loss_ref[...] = jnp.zeros_like(loss_ref)

        part = 0.5 * jnp.sum(jnp.mean(err * err, axis=-1, keepdims=True), axis=0, keepdims=True)
        loss_ref[...] += jnp.broadcast_to(part, loss_ref.shape)
        dx, dgrow = _rms_bwd(err * (1.0 / d), xh, rs, g)
        dx_ref[...] = dx
        dg_ref[...] += jnp.sum(dgrow, axis=0, keepdims=True)

    row = pl.BlockSpec((tm, d), lambda i: (i, 0))
    vec = pl.BlockSpec((1, d), lambda i: (0, 0))
    return pl.pallas_call(
        body, name="loss_head", grid=(s_len // tm,),
        out_shape=(SDS((s_len, d), F32), SDS((1, d), F32), SDS((8, LANE), F32)),
        in_specs=[row, vec, row], out_specs=(row, vec, pl.BlockSpec((8, LANE), lambda i: (0, 0))),
        compiler_params=_cparams(),
    )(x, gain, target)


def _adamw_update(w, g, m, v):
    m = ADAM_B1 * m + (1.0 - ADAM_B1) * g
    v = ADAM_B2 * v + (1.0 - ADAM_B2) * (g * g)
    m_hat = m / (1.0 - ADAM_B1 ** ADAM_STEP)
    v_hat = v / (1.0 - ADAM_B2 ** ADAM_STEP)
    return -ADAM_LR * (m_hat / (jnp.sqrt(v_hat) + ADAM_EPS) + ADAM_WD * w), m, v


def adamw(w, g, m, v, tr=512):
    rows, cols = w.shape
    tr = _row_chunk(rows, tr, 8)

    def body(w_ref, g_ref, m_ref, v_ref, d_ref, nm_ref, nv_ref):
        d_ref[...], nm_ref[...], nv_ref[...] = _adamw_update(w_ref[...], g_ref[...], m_ref[...], v_ref[...])

    blk = pl.BlockSpec((tr, cols), lambda i: (i, 0))
    return pl.pallas_call(
        body, name="adamw", grid=(rows // tr,), out_shape=(SDS(w.shape, F32),) * 3,
        in_specs=[blk] * 4, out_specs=(blk,) * 3, compiler_params=_cparams(),
    )(w, g, m, v)


def adamw_layer(gf, blk, row_off, n_half, l, w, m, v, outs, deps=()):
    fh = gf.shape[1] // 7
    d = gf.shape[2]
    nd = len(deps)

    def body(gf_ref, w_ref, m_ref, v_ref, *rest):
        g_ref, d_ref, nm_ref, nv_ref = rest[4 + nd:]
        g = gf_ref[row_off:row_off + n_half, :]
        g_ref[...] = g
        d_ref[...], nm_ref[...], nv_ref[...] = _adamw_update(w_ref[...], g, m_ref[...], v_ref[...])

    gspec = pl.BlockSpec((None, fh, d), lambda h: (h, blk, 0))
    wspec = pl.BlockSpec((None, n_half, d), lambda h: (l, h, 0))
    return pl.pallas_call(
        body, name="adamw_layer", grid=(2,), out_shape=tuple(SDS(o.shape, o.dtype) for o in outs),
        in_specs=[gspec, wspec, wspec, wspec] + [ANY] * (4 + nd), out_specs=(wspec,) * 4,
        input_output_aliases={4 + i: i for i in range(4)}, compiler_params=_cparams(),
    )(gf, w, m, v, *outs, *deps)


def pack_weight(pos, land, blk, l, w, extra=None, deps=()):
    fh, d = land.shape[2] // 7, land.shape[3]
    nd = len(deps)

    def body(pos_ref, w_ref, *rest):
        o_ref = rest[-1]
        a = w_ref[...].astype(BF)
        n = a.shape[0] // 2
        for h in range(2):
            o_ref[h, 0:n, :] = a[h * n:(h + 1) * n]
        if extra is not None:
            b = rest[0][...].astype(BF)
            nb = b.shape[0] // 2
            for h in range(2):
                o_ref[h, n:n + nb, :] = b[h * nb:(h + 1) * nb]

    def whole(a):
        return pl.BlockSpec((None,) + a.shape[1:], lambda i, p: (l, 0, 0))

    ins = [w] + ([extra] if extra is not None else [])
    return pl.pallas_call(
        body, name="pack_weight", out_shape=SDS(land.shape, land.dtype),
        grid_spec=pltpu.PrefetchScalarGridSpec(
            num_scalar_prefetch=1, grid=(1,),
            in_specs=[whole(a) for a in ins] + [ANY] * (1 + nd),
            out_specs=pl.BlockSpec((None, 2, fh, d), lambda i, p: (p[0], 0, blk, 0))),
        input_output_aliases={1 + len(ins): 0}, compiler_params=_cparams(),
    )(pos, *ins, land, *deps)


def _pack_rows(arrays, width):
    flat = jnp.concatenate([a.reshape(-1).astype(F32) for a in arrays])
    rows = -(-flat.shape[0] // (8 * width)) * 8
    return jnp.pad(flat, (0, rows * width - flat.shape[0])).reshape(rows, width)


def _unpack_rows(buf, shapes):
    flat = buf.reshape(-1)
    out, off = [], 0
    for shp in shapes:
        n = int(np.prod(shp))
        out.append(flat[off:off + n].reshape(shp))
        off += n
    return out


def _t5_buckets(rel):
    half = N_BUCKETS // 2
    max_exact = half // 2
    ret = (rel > 0).astype(jnp.int32) * half
    n = jnp.abs(rel)
    n_f = jnp.maximum(n, 1).astype(F32)
    large = max_exact + (jnp.log(n_f / max_exact) / math.log(MAX_DISTANCE / max_exact) * (half - max_exact)).astype(jnp.int32)
    large = jnp.minimum(large, half - 1)
    return ret + jnp.where(n < max_exact, n, large)


def _band_buckets():
    t = jnp.arange(BLOCK)[:, None]
    j = jnp.arange(3 * BLOCK)[None, :]
    rel = j - BLOCK - t
    return _t5_buckets(rel), jnp.abs(rel) <= WINDOW


def _block_diag_pairs(w):
    depth, two, nblk, bw, _ = w.shape
    pairs = w.reshape(depth, two, nblk // 2, 2, bw, bw)
    z = jnp.zeros_like(pairs[:, :, :, 0])
    top = jnp.concatenate([pairs[:, :, :, 0], z], axis=-1)
    bot = jnp.concatenate([z, pairs[:, :, :, 1]], axis=-1)
    return jnp.concatenate([top, bot], axis=-2)


def _diag_blocks(dw):
    bw = dw.shape[-1] // 2
    a = dw[:, :, :bw, :bw]
    b = dw[:, :, bw:, bw:]
    return jnp.stack([a, b], axis=2).reshape(dw.shape[0], 2 * dw.shape[1], bw, bw)


def kernel(x, ffn1_norm, ffn1_w_gate, ffn1_w_up, ffn1_w_down, mix_norm, w_in, conv_w, conv_b, lru_w_a, lru_b_a, lru_w_x, lru_b_x, lru_lambda, attn_sink, rel_bias, lru_out_norm, attn_out_norm, w_out, ffn2_norm, ffn2_w_gate, ffn2_w_up, ffn2_w_down, final_norm, loss_target, m_ffn1_norm, m_ffn1_w_gate, m_ffn1_w_up, m_ffn1_w_down, m_mix_norm, m_w_in, m_conv_w, m_conv_b, m_lru_w_a, m_lru_b_a, m_lru_w_x, m_lru_b_x, m_lru_lambda, m_attn_sink, m_rel_bias, m_lru_out_norm, m_attn_out_norm, m_w_out, m_ffn2_norm, m_ffn2_w_gate, m_ffn2_w_up, m_ffn2_w_down, m_final_norm, v_ffn1_norm, v_ffn1_w_gate, v_ffn1_w_up, v_ffn1_w_down, v_mix_norm, v_w_in, v_conv_w, v_conv_b, v_lru_w_a, v_lru_b_a, v_lru_w_x, v_lru_b_x, v_lru_lambda, v_attn_sink, v_rel_bias, v_lru_out_norm, v_attn_out_norm, v_w_out, v_ffn2_norm, v_ffn2_w_gate, v_ffn2_w_up, v_ffn2_w_down, v_final_norm):
    depth, d = ffn1_norm.shape
    d_ff = N_CHIPS * ffn1_w_gate.shape[2]
    d_in = N_CHIPS * w_in.shape[2]
    lw = conv_b.shape[1]
    att = N_HEADS * HEAD_DIM
    lay = Layout(d, d_ff, d_in)
    k_chip = 2 * lax.axis_index("x") + lax.axis_index("y")
    pos = jnp.stack([k_chip, lax.axis_index("c")]).astype(jnp.int32)

    def rows_major(a):
        return jnp.swapaxes(a, 1, 2)

    mats = (rows_major(ffn1_w_gate), rows_major(ffn1_w_up), ffn1_w_down,
            rows_major(ffn2_w_gate), rows_major(ffn2_w_up), ffn2_w_down)

    def pack_layer(l, deps=()):
        land = lax.empty((N_CHIPS, 2, lay.rows, d), BF)
        for m, a in enumerate(mats):
            land = pack_weight(pos, land, m, l, a, deps=deps if m == 0 else ())
        return pack_weight(pos, land, lay.MIX_BLK, l, rows_major(w_in), extra=w_out)

    def gather_start(l, land):
        return split_start(f"gather_start_{l}", [land], 3, gather_plan)

    def gather_finish(l, started, after):
        ssem, rsem, bufs, _ = started
        land, = split_wait(f"gather_wait_{l}", ssem, rsem, bufs, after, gather_plan)
        return gather_pair(land)

    sharded_small = (conv_w, lru_b_a, lru_b_x, lru_lambda)
    sshard = jnp.concatenate([a.reshape(-1, LANE) for a in sharded_small], axis=0)
    sfull = gather_small(sshard)
    small_full, off = [], 0
    for a in sharded_small:
        r = a.shape[0] * a.shape[1]
        piece = sfull[:, off:off + r].reshape((N_CHIPS,) + a.shape)
        small_full.append(jnp.moveaxis(piece, 0, 2).reshape(a.shape[0], a.shape[1], N_CHIPS * LANE))
        off += r
    conv_w_f, b_a_f, b_x_f, lam_f = small_full

    zrow = jnp.zeros((1, lw), F32)
    wblk_a = _block_diag_pairs(lru_w_a)
    wblk_x = _block_diag_pairs(lru_w_x)
    buckets, in_band = _band_buckets()
    onehot = (buckets.reshape(-1)[:, None] == jnp.arange(N_BUCKETS)[None, :]).astype(F32)
    bias = jnp.dot(rel_bias.T, onehot.T, precision=lax.Precision.HIGHEST).reshape(N_HEADS, BLOCK, 3 * BLOCK)
    bias = jnp.where(in_band[None], bias, NEG_INF)

    def layer_small(l):
        cvec = jnp.concatenate([conv_w_f[l], jnp.zeros((8 - CONV_WIDTH, lw), F32)], axis=0)
        pvec = jnp.concatenate([conv_b[l][None], b_a_f[l], b_x_f[l], lam_f[l], zrow], axis=0)
        wblk = jnp.stack([wblk_a[l, 0], wblk_x[l, 0], wblk_a[l, 1], wblk_x[l, 1]]).astype(BF)
        sink = jnp.broadcast_to(attn_sink[l][:, None, None], (N_HEADS, BLOCK, LANE)).reshape(N_HEADS * BLOCK, LANE)
        return cvec, pvec, wblk, sink

    xs = x[0]
    wfull = [None] * depth
    first = gather_start(0, pack_layer(0))
    lands = {l: pack_layer(l, deps=(first[3],)) for l in range(1, depth)}
    wfull[0] = gather_finish(0, first, [xs] + list(lands.values()))
    started = gather_start(1, lands[1]) if depth > 1 else None
    saved = []
    for l in range(depth):
        cvec, pvec, wblk, sink = layer_small(l)
        deps = (started[3],) if started is not None else ()
        x1, gate1, up1 = ffn_forward(xs, ffn1_norm[l][None], wfull[l], lay, 0, deps=deps)
        proj = mix_project(x1, mix_norm[l][None], wfull[l], lay)
        y_rec, hs = lru_forward(proj, cvec, pvec, wblk, lw)
        y_att = attention_forward(proj, bias, sink, lw, att)
        x2 = mix_output(x1, y_rec, y_att, lru_out_norm[l][None], attn_out_norm[l][None], wfull[l], lay)
        x3, gate2, up2 = ffn_forward(x2, ffn2_norm[l][None], wfull[l], lay, 1)
        saved.append((xs, x1, x2, proj, y_rec, hs, y_att, (gate1, up1), (gate2, up2)))
        xs = x3
        if l + 1 < depth:
            wfull[l + 1] = gather_finish(l + 1, started, [x3])
            started = gather_start(l + 2, lands[l + 2]) if l + 2 < depth else None

    dx, d_final, loss_tile = loss_head(xs, final_norm[None], loss_target[0])
    loss = lax.psum(loss_tile[0, 0], ("x", "y", "c"))

    layer_names = ["ffn1_norm", "mix_norm", "conv_w", "conv_b", "lru_w_a", "lru_b_a", "lru_w_x", "lru_b_x", "lru_lambda",
                   "attn_sink", "lru_out_norm", "attn_out_norm", "ffn2_norm"]
    dbias_total = jnp.zeros(bias.shape, F32)

    def ffn_back(xin, gain, dout, pre, gb, l, which, deps=()):
        dxo, dg, dgate, dup, act, h, df = ffn_backward_dx(xin, gain, dout, *pre, wfull[l], lay, which, deps=deps)
        gb = weight_grad_tn(dgate, h, gb, lay, 3 * which + 0)
        gb = weight_grad_tn(dup, h, gb, lay, 3 * which + 1)
        gb = weight_grad_tn(act, df, gb, lay, 3 * which + 2)
        return dxo, dg[0], gb

    def reduce_start(l, gb, sb):
        p1, sp1 = exchange_pair(gb, sb)
        cs = pair_sum(pos, gb, p1)
        ss = small_pair_sum(sb, sp1)
        lands = [lax.empty((3,) + cs.shape[1:], cs.dtype), lax.empty((N_CHIPS,) + ss.shape, ss.dtype)]
        return split_start(f"reduce_start_{l}", [cs, ss] + lands, 6, reduce_plan)

    def reduce_finish(l, started, after):
        ssem, rsem, bufs, _ = started
        cs, ss, p3, sp3 = split_wait(f"reduce_wait_{l}", ssem, rsem, bufs, after, reduce_plan)
        return exchange_final(chip_sum(pos, cs, p3)), small_chip_sum(pos, ss, sp3)

    gf = [None] * depth
    small_sums = [None] * depth
    small_shapes = [None] * depth
    in_flight = None
    for l in reversed(range(depth)):
        x0, x1, x2, proj, y_rec, hs, y_att, pre1, pre2 = saved[l]
        cvec, pvec, wblk, sink = layer_small(l)
        gb = lax.empty((N_CHIPS, 2, lay.rows, d), BF)
        part = {}
        deps = (in_flight[1][3],) if in_flight is not None else ()
        dx, part["ffn2_norm"], gb = ffn_back(x2, ffn2_norm[l][None], dx, pre2, gb, l, 1, deps=deps)
        dyr, dya, dgr, dga, dwout = mix_output_backward(dx, y_rec, y_att, lru_out_norm[l][None], attn_out_norm[l][None],
                                                        wfull[l], lay)
        part["lru_out_norm"] = dgr[0]
        part["attn_out_norm"] = dga[0]
        dq, dkv, dbias, dsink = attention_backward(proj, y_att, dya, bias, sink, lw, att)
        dbias_total = dbias_total + dbias
        part["attn_sink"] = dsink[:, 0]
        dxr, dgt, dcv, dpv, dwb = lru_backward(proj, hs, dyr, cvec, pvec, wblk, lw)
        part["conv_w"] = dcv[:CONV_WIDTH]
        part["conv_b"] = dpv[0]
        part["lru_b_a"] = dpv[1:3]
        part["lru_b_x"] = dpv[3:5]
        part["lru_lambda"] = dpv[5:7]
        part["lru_w_a"] = _diag_blocks(jnp.stack([dwb[0], dwb[2]]))
        part["lru_w_x"] = _diag_blocks(jnp.stack([dwb[1], dwb[3]]))
        dx, dgm, gb = mix_project_backward(x1, mix_norm[l][None], dx, dxr, dgt, dq, dkv, dwout, wfull[l], gb, lay)
        part["mix_norm"] = dgm[0]
        dx, part["ffn1_norm"], gb = ffn_back(x0, ffn1_norm[l][None], dx, pre1, gb, l, 0)
        pieces = [part[n] for n in layer_names]
        if l == 0:
            d_rel_bias = jnp.dot(dbias_total.reshape(N_HEADS, -1), onehot, precision=lax.Precision.HIGHEST).T
            pieces += [d_rel_bias, d_final[0]]
        small_shapes[l] = [p.shape for p in pieces]
        if in_flight is not None:
            gf[in_flight[0]], small_sums[in_flight[0]] = reduce_finish(in_flight[0], in_flight[1], [dx])
        in_flight = (l, reduce_start(l, gb, _pack_rows(pieces, 1024)))
    grad_x = dx[None]

    weights = dict(ffn1_norm=ffn1_norm, ffn1_w_gate=ffn1_w_gate, ffn1_w_up=ffn1_w_up, ffn1_w_down=ffn1_w_down, mix_norm=mix_norm, w_in=w_in, conv_w=conv_w, conv_b=conv_b, lru_w_a=lru_w_a, lru_b_a=lru_b_a, lru_w_x=lru_w_x, lru_b_x=lru_b_x, lru_lambda=lru_lambda, attn_sink=attn_sink, rel_bias=rel_bias, lru_out_norm=lru_out_norm, attn_out_norm=attn_out_norm, w_out=w_out, ffn2_norm=ffn2_norm, ffn2_w_gate=ffn2_w_gate, ffn2_w_up=ffn2_w_up, ffn2_w_down=ffn2_w_down, final_norm=final_norm)
    m_in = dict(ffn1_norm=m_ffn1_norm, ffn1_w_gate=m_ffn1_w_gate, ffn1_w_up=m_ffn1_w_up, ffn1_w_down=m_ffn1_w_down, mix_norm=m_mix_norm, w_in=m_w_in, conv_w=m_conv_w, conv_b=m_conv_b, lru_w_a=m_lru_w_a, lru_b_a=m_lru_b_a, lru_w_x=m_lru_w_x, lru_b_x=m_lru_b_x, lru_lambda=m_lru_lambda, attn_sink=m_attn_sink, rel_bias=m_rel_bias, lru_out_norm=m_lru_out_norm, attn_out_norm=m_attn_out_norm, w_out=m_w_out, ffn2_norm=m_ffn2_norm, ffn2_w_gate=m_ffn2_w_gate, ffn2_w_up=m_ffn2_w_up, ffn2_w_down=m_ffn2_w_down, final_norm=m_final_norm)
    v_in = dict(ffn1_norm=v_ffn1_norm, ffn1_w_gate=v_ffn1_w_gate, ffn1_w_up=v_ffn1_w_up, ffn1_w_down=v_ffn1_w_down, mix_norm=v_mix_norm, w_in=v_w_in, conv_w=v_conv_w, conv_b=v_conv_b, lru_w_a=v_lru_w_a, lru_b_a=v_lru_b_a, lru_w_x=v_lru_w_x, lru_b_x=v_lru_b_x, lru_lambda=v_lru_lambda, attn_sink=v_attn_sink, rel_bias=v_rel_bias, lru_out_norm=v_lru_out_norm, attn_out_norm=v_attn_out_norm, w_out=v_w_out, ffn2_norm=v_ffn2_norm, ffn2_w_gate=v_ffn2_w_gate, ffn2_w_up=v_ffn2_w_up, ffn2_w_down=v_ffn2_w_down, final_norm=v_final_norm)
    order = list(weights)
    large = [(name, m, 0, lay.fh, m % 3 != 2) for m, name in
             enumerate(("ffn1_w_gate", "ffn1_w_up", "ffn1_w_down", "ffn2_w_gate", "ffn2_w_up", "ffn2_w_down"))]
    large += [("w_in", lay.MIX_BLK, 0, lay.ih, True), ("w_out", lay.MIX_BLK, lay.ih, lay.oh, False)]
    as_rows = {name: [rows_major(src[name]) if flip else src[name] for src in (weights, m_in, v_in)]
               for name, _, _, _, flip in large}
    stacked = {name: tuple(lax.empty(as_rows[name][0].shape, F32) for _ in range(4)) for name, *_ in large}

    def adamw_large(l, deps=()):
        for i, (name, blk, row_off, n_half, _) in enumerate(large):
            stacked[name] = adamw_layer(gf[l], blk, row_off, n_half, l, *as_rows[name], stacked[name],
                                        deps=deps if i == 0 else ())

    last = in_flight[0]
    for l in range(depth):
        if l != last:
            adamw_large(l, deps=(in_flight[1][3],))
    ready = [buf for name, *_ in large for buf in stacked[name]] if depth > 1 else []
    gf[last], small_sums[last] = reduce_finish(last, in_flight[1], [dx] + ready)
    adamw_large(last)

    per_layer = [_unpack_rows(small_sums[l], small_shapes[l]) for l in range(depth)]
    grads = {n: jnp.stack([per_layer[l][i] for l in range(depth)]) for i, n in enumerate(layer_names)}
    grads["rel_bias"], grads["final_norm"] = per_layer[0][len(layer_names):]
    for name in ("conv_w", "lru_b_a", "lru_b_x", "lru_lambda"):
        grads[name] = lax.dynamic_slice_in_dim(grads[name], k_chip * LANE, LANE, axis=2)
    delta, new_m, new_v = {}, {}, {}
    for name, _, _, _, flip in large:
        grads[name], delta[name], new_m[name], new_v[name] = [rows_major(a) if flip else a for a in stacked[name]]
    small = [n for n in order if n not in stacked]
    packed = [_pack_rows([src[n] for n in small], 1024) for src in (weights, grads, m_in, v_in)]
    outs = adamw(*packed)
    shapes = [weights[n].shape for n in small]
    for dst, buf in zip((delta, new_m, new_v), outs):
        dst.update(zip(small, _unpack_rows(buf, shapes)))

    return (loss, grad_x, *[grads[n] for n in order], *[delta[n] for n in order],
            *[new_m[n] for n in order], *[new_v[n] for n in order])
```

```python
import functools
import math

import jax
import jax.numpy as jnp
import numpy as np
from jax import lax
from jax.experimental import pallas as pl
from jax.experimental.pallas import tpu as pltpu

BF = jnp.bfloat16
F32 = jnp.float32
SDS = jax.ShapeDtypeStruct
MESH = pl.DeviceIdType.MESH
ANY = pl.BlockSpec(memory_space=pl.ANY)

N_CHIPS = 4
N_HEADS = 8
N_KV_HEADS = 2
KV_GROUP = N_HEADS // N_KV_HEADS
HEAD_DIM = 64
BLOCK = 128
WINDOW = 128
N_BUCKETS = 32
MAX_DISTANCE = 128
LRU_C = 8.0
CONV_WIDTH = 4
LANE = 128
SCAN_SEGMENTS = 8
SCAN_CHAINS = 8
EPS = 1e-6
FFN_RES = 0.5
NEG_INF = -1e30
ADAM_LR = 0.001
ADAM_B1 = 0.9
ADAM_B2 = 0.999
ADAM_EPS = 1e-08
ADAM_WD = 0.01
ADAM_STEP = 10
VMEM_LIMIT = 60000 * 1024
GELU_C = math.sqrt(2.0 / math.pi)


def dot_nn(a, b):
    return lax.dot_general(a, b, (((1,), (0,)), ((), ())), preferred_element_type=F32)


def dot_nt(a, b):
    return lax.dot_general(a, b, (((1,), (1,)), ((), ())), preferred_element_type=F32)


def dot_tn(a, b):
    return lax.dot_general(a, b, (((0,), (0,)), ((), ())), preferred_element_type=F32)


def _cparams(**kw):
    return pltpu.CompilerParams(vmem_limit_bytes=VMEM_LIMIT, **kw)


class Layout:
    MIX_BLK = 6

    def __init__(self, d_model, d_ff, d_in):
        self.fh = d_ff // (2 * N_CHIPS)
        self.ih = d_in // (2 * N_CHIPS)
        self.oh = d_model // (2 * N_CHIPS)
        assert self.ih + self.oh == self.fh, "w_in^T and w_out rows must fill one ffn-sized block"
        self.rows = 7 * self.fh


def _row_chunk(rows, target, step=16):
    best = rows
    for c in range(step, min(rows, target) + 1, step):
        if rows % c == 0:
            best = c
    return best


def _mesh_pos():
    return lax.axis_index("x"), lax.axis_index("y"), lax.axis_index("c")


def _rcopy(src, dst, ssem, rsem, dev):
    return pltpu.make_async_remote_copy(src_ref=src, dst_ref=dst, send_sem=ssem, recv_sem=rsem,
                                        device_id=dev, device_id_type=MESH)


HBM = pl.BlockSpec(memory_space=pltpu.HBM)
SEM = pl.BlockSpec(memory_space=pltpu.SEMAPHORE)
DATAFLOW = pltpu.SideEffectType.DATAFLOW_SIDE_EFFECTING


def _chip_peers():
    x, y, c = _mesh_pos()
    peers = [(1 - x, y), (x, 1 - y), (1 - x, 1 - y)]
    return x, y, c, 2 * x + y, [(px, py, 2 * px + py) for px, py in peers]


def split_start(name, bufs, n, plan):
    nb = len(bufs)

    def body(*refs):
        sends, _ = plan(refs[:nb], refs[nb], refs[nb + 1])
        for cp in sends:
            cp.start()
        refs[-1][...] = jnp.zeros_like(refs[-1])

    out = pl.pallas_call(
        body, name=name,
        out_shape=(pltpu.SemaphoreType.DMA((n,)), pltpu.SemaphoreType.DMA((n,)),
                   *[pltpu.HBM(b.shape, b.dtype) for b in bufs], SDS((8, LANE), F32)),
        in_specs=[HBM] * nb, out_specs=(SEM, SEM, *([HBM] * nb), pl.BlockSpec(memory_space=pltpu.VMEM)),
        input_output_aliases={i: 2 + i for i in range(nb)},
        compiler_params=pltpu.CompilerParams(has_side_effects=DATAFLOW),
    )(*[pltpu.with_memory_space_constraint(b, pltpu.HBM) for b in bufs])
    return out[0], out[1], list(out[2:2 + nb]), out[-1]


def split_wait(name, ssem, rsem, bufs, after, plan):
    nb = len(bufs)

    def body(*refs):
        sends, recvs = plan(refs[:nb], refs[nb], refs[nb + 1])
        for cp in recvs:
            cp.wait_recv()
        for cp in sends:
            cp.wait_send()

    out = pl.pallas_call(
        body, name=name, out_shape=tuple(pltpu.HBM(b.shape, b.dtype) for b in bufs),
        in_specs=[HBM] * nb + [SEM, SEM] + [ANY] * len(after), out_specs=tuple([HBM] * nb),
        input_output_aliases={i: i for i in range(nb)},
        compiler_params=pltpu.CompilerParams(has_side_effects=DATAFLOW),
    )(*bufs, ssem, rsem, *after)
    return list(out)


def gather_plan(refs, ssem, rsem):
    land_ref, = refs
    _, _, c, k, peers = _chip_peers()
    sends = [_rcopy(land_ref.at[k, c], land_ref.at[k, c], ssem.at[j], rsem.at[j], (px, py, c))
             for j, (px, py, _) in enumerate(peers)]
    recvs = [_rcopy(land_ref.at[kp, c], land_ref.at[kp, c], ssem.at[j], rsem.at[j], (px, py, c))
             for j, (px, py, kp) in enumerate(peers)]
    return sends, recvs


def reduce_plan(refs, ssem, rsem):
    cs_ref, ss_ref, p3_ref, sp3_ref = refs
    _, _, c, k, peers = _chip_peers()
    sends, recvs = [], []
    for j, (px, py, kp) in enumerate(peers):
        sends.append(_rcopy(cs_ref.at[kp], p3_ref.at[j], ssem.at[j], rsem.at[j], (px, py, c)))
        recvs.append(_rcopy(cs_ref.at[kp], p3_ref.at[j], ssem.at[j], rsem.at[j], (px, py, c)))
        sends.append(_rcopy(ss_ref, sp3_ref.at[k], ssem.at[3 + j], rsem.at[3 + j], (px, py, c)))
        recvs.append(_rcopy(ss_ref, sp3_ref.at[kp], ssem.at[3 + j], rsem.at[3 + j], (px, py, c)))
    return sends, recvs


def gather_small(sshard):
    def body(s_ref, sf_ref, lsem, ssem, rsem):
        _, _, c, k, peers = _chip_peers()
        own = pltpu.make_async_copy(s_ref, sf_ref.at[k], lsem)
        own.start()
        sends = [_rcopy(s_ref, sf_ref.at[k], ssem.at[j], rsem.at[j], (px, py, c)) for j, (px, py, _) in enumerate(peers)]
        recvs = [_rcopy(s_ref, sf_ref.at[kp], ssem.at[j], rsem.at[j], (px, py, c)) for j, (px, py, kp) in enumerate(peers)]
        for cp in sends:
            cp.start()
        for cp in recvs:
            cp.wait_recv()
        for cp in sends:
            cp.wait_send()
        own.wait()

    return pl.pallas_call(
        body, name="gather_small", out_shape=SDS((N_CHIPS,) + sshard.shape, sshard.dtype),
        in_specs=[ANY], out_specs=ANY,
        scratch_shapes=[pltpu.SemaphoreType.DMA, pltpu.SemaphoreType.DMA((3,)), pltpu.SemaphoreType.DMA((3,))],
    )(sshard)


def gather_pair(land):
    def body(land_in, land_ref, ssem, rsem):
        x, y, c, k, peers = _chip_peers()
        sib = (x, y, 1 - c)
        sends = [_rcopy(land_ref.at[kp, c], land_ref.at[kp, c], ssem.at[j], rsem.at[j], sib) for j, (_, _, kp) in enumerate(peers)]
        recvs = [_rcopy(land_ref.at[kp, 1 - c], land_ref.at[kp, 1 - c], ssem.at[j], rsem.at[j], sib)
                 for j, (_, _, kp) in enumerate(peers)]
        for cp in sends:
            cp.start()
        for cp in recvs:
            cp.wait_recv()
        for cp in sends:
            cp.wait_send()

    return pl.pallas_call(
        body, name="gather_pair", out_shape=SDS(land.shape, land.dtype),
        in_specs=[ANY], out_specs=ANY, input_output_aliases={0: 0},
        scratch_shapes=[pltpu.SemaphoreType.DMA((3,)), pltpu.SemaphoreType.DMA((3,))],
    )(land)


def exchange_pair(gb, sb):
    n, _, rh, d = gb.shape

    def body(gb_ref, sb_ref, p_ref, sp_ref, ssem, rsem):
        x, y, c = _mesh_pos()
        sib = (x, y, 1 - c)
        sends = [_rcopy(gb_ref.at[kk, 1 - c], p_ref.at[kk], ssem.at[kk], rsem.at[kk], sib) for kk in range(n)]
        sends.append(_rcopy(sb_ref, sp_ref, ssem.at[n], rsem.at[n], sib))
        for cp in sends:
            cp.start()
        for cp in sends:
            cp.wait_recv()
        for cp in sends:
            cp.wait_send()

    return pl.pallas_call(
        body, name="exchange_pair",
        out_shape=(SDS((n, rh, d), gb.dtype), SDS(sb.shape, sb.dtype)),
        in_specs=[ANY, ANY], out_specs=(ANY, ANY),
        scratch_shapes=[pltpu.SemaphoreType.DMA((n + 1,)), pltpu.SemaphoreType.DMA((n + 1,))],
    )(gb, sb)


def exchange_final(gf):
    _, rh, d = gf.shape
    nch = 4 if rh % 32 == 0 else 1
    cr = rh // nch

    def body(gf_ref, out_ref, ssem, rsem):
        x, y, c = _mesh_pos()
        sib = (x, y, 1 - c)
        sends = [_rcopy(out_ref.at[c, pl.ds(q * cr, cr)], out_ref.at[c, pl.ds(q * cr, cr)], ssem.at[q], rsem.at[q], sib)
                 for q in range(nch)]
        recvs = [_rcopy(out_ref.at[1 - c, pl.ds(q * cr, cr)], out_ref.at[1 - c, pl.ds(q * cr, cr)], ssem.at[q], rsem.at[q], sib)
                 for q in range(nch)]
        for cp in sends:
            cp.start()
        for cp in recvs:
            cp.wait_recv()
        for cp in sends:
            cp.wait_send()

    return pl.pallas_call(
        body, name="exchange_final",
        out_shape=SDS(gf.shape, gf.dtype),
        in_specs=[ANY], out_specs=ANY, input_output_aliases={0: 0},
        scratch_shapes=[pltpu.SemaphoreType.DMA((nch,)), pltpu.SemaphoreType.DMA((nch,))],
    )(gf)


def pair_sum(pos, gb, p1):
    n, _, rh, d = gb.shape
    cr = _row_chunk(rh, 1024)

    def body(pos_ref, a_ref, b_ref, o_ref):
        o_ref[...] = (a_ref[...].astype(F32) + b_ref[...].astype(F32)).astype(o_ref.dtype)

    return pl.pallas_call(
        body, name="pair_sum", out_shape=SDS((n, rh, d), gb.dtype),
        grid_spec=pltpu.PrefetchScalarGridSpec(
            num_scalar_prefetch=1, grid=(n, rh // cr),
            in_specs=[pl.BlockSpec((None, None, cr, d), lambda kk, r, pos: (kk, pos[1], r, 0)),
                      pl.BlockSpec((None, cr, d), lambda kk, r, pos: (kk, r, 0))],
            out_specs=pl.BlockSpec((None, cr, d), lambda kk, r, pos: (kk, r, 0))),
        compiler_params=_cparams(),
    )(pos, gb, p1)


def chip_sum(pos, cs, p3):
    n, rh, d = cs.shape
    cr = _row_chunk(rh, 512)

    def body(pos_ref, a_ref, b_ref, o_ref):
        acc = a_ref[...].astype(F32)
        for j in range(3):
            acc = acc + b_ref[j].astype(F32)
        o_ref[...] = acc

    return pl.pallas_call(
        body, name="chip_sum", out_shape=SDS((2, rh, d), F32),
        grid_spec=pltpu.PrefetchScalarGridSpec(
            num_scalar_prefetch=1, grid=(rh // cr,),
            in_specs=[pl.BlockSpec((None, cr, d), lambda r, pos: (pos[0], r, 0)),
                      pl.BlockSpec((3, cr, d), lambda r, pos: (0, r, 0))],
            out_specs=pl.BlockSpec((None, cr, d), lambda r, pos: (pos[1], r, 0))),
        compiler_params=_cparams(),
    )(pos, cs, p3)


def small_pair_sum(a, b):
    def body(a_ref, b_ref, o_ref):
        o_ref[...] = a_ref[...] + b_ref[...]

    return pl.pallas_call(body, name="small_pair_sum", out_shape=SDS(a.shape, a.dtype),
                          compiler_params=_cparams())(a, b)


def small_chip_sum(pos, own, p):
    ns, w = own.shape

    def body(pos_ref, own_ref, p0, p1, p2, p3, o_ref):
        k = pos_ref[0]
        acc = None
        for chip, ref in enumerate((p0, p1, p2, p3)):
            term = jnp.where(k == chip, own_ref[...], ref[...])
            acc = term if acc is None else acc + term
        o_ref[...] = acc

    def slot(chip):
        return pl.BlockSpec((None, ns, w), lambda i, pos: (jnp.where(pos[0] == chip, (chip + 1) % N_CHIPS, chip), 0, 0))

    return pl.pallas_call(
        body, name="small_chip_sum", out_shape=SDS(own.shape, own.dtype),
        grid_spec=pltpu.PrefetchScalarGridSpec(
            num_scalar_prefetch=1, grid=(1,),
            in_specs=[pl.BlockSpec((ns, w), lambda i, pos: (0, 0))] + [slot(chip) for chip in range(N_CHIPS)],
            out_specs=pl.BlockSpec((ns, w), lambda i, pos: (0, 0))),
        compiler_params=_cparams(),
    )(pos, own, p, p, p, p)


def _rms(x, g):
    rs = lax.rsqrt(jnp.mean(x * x, axis=-1, keepdims=True) + EPS)
    xh = x * rs
    return xh, rs, xh * g


def _rms_bwd(dy, xh, rs, g):
    dxh = dy * g
    dx = rs * (dxh - xh * jnp.mean(dxh * xh, axis=-1, keepdims=True))
    return dx, dy * xh


def _gelu(x):
    t = jnp.tanh(GELU_C * (x + 0.044715 * x * x * x))
    return 0.5 * x * (1.0 + t), t


def _gelu_grad(x, t):
    return 0.5 * (1.0 + t) + 0.5 * x * (1.0 - t * t) * GELU_C * (1.0 + 3.0 * 0.044715 * x * x)


def _shift_rows(v, s, n):
    if s == 0:
        return v
    t = lax.broadcasted_iota(jnp.int32, v.shape, 0)
    rolled = pltpu.roll(v, (-s) % n, 0)
    return jnp.where((t + s >= 0) & (t + s < n), rolled, 0.0)


def _scan_rows(a_ref, u_ref, h_ref, acum_ref, reverse):
    s_len, w = a_ref.shape
    chains = max(1, min(SCAN_CHAINS, s_len // (8 * SCAN_SEGMENTS)))
    nseg = SCAN_SEGMENTS * chains
    seg = s_len // nseg

    def step(j, carry):
        jj = (seg - 1 - j) if reverse else j
        out = []
        for c, (h, acc) in enumerate(carry):
            idx = pl.ds(c * SCAN_SEGMENTS * seg + jj, SCAN_SEGMENTS, stride=seg)
            a = a_ref[idx, :]
            h = a * h + u_ref[idx, :]
            acc = a * acc
            h_ref[idx, :] = h
            acum_ref[idx, :] = acc
            out.append((h, acc))
        return tuple(out)

    init = tuple((jnp.zeros((SCAN_SEGMENTS, w), F32), jnp.ones((SCAN_SEGMENTS, w), F32)) for _ in range(chains))
    ends = lax.fori_loop(0, seg, step, init, unroll=min(8, seg))
    order = range(nseg - 2, -1, -1) if reverse else range(1, nseg)
    inflow = jnp.zeros((1, w), F32)
    for s in order:
        src = s + 1 if reverse else s - 1
        h, acc = ends[src // SCAN_SEGMENTS]
        r = src % SCAN_SEGMENTS
        inflow = h[r:r + 1, :] + acc[r:r + 1, :] * inflow
        rows = pl.ds(s * seg, seg)
        h_ref[rows, :] = h_ref[rows, :] + acum_ref[rows, :] * inflow


def _w_spec(rows_half, d, blk):
    return pl.BlockSpec((N_CHIPS, 2, rows_half, d), lambda *_: (0, 0, blk, 0), pipeline_mode=pl.Buffered(1))


def ffn_forward(x, gain, wfull, lay, which, deps=(), tm=512):
    s_len, d = x.shape
    tm = min(tm, s_len)
    f = 8 * lay.fh
    fc = f // 2

    def body(x_ref, g_ref, wg_ref, wu_ref, wd_ref, *rest):
        o_ref, gate_ref, up_ref = rest[len(deps):]
        x = x_ref[...]
        _, _, hn = _rms(x, g_ref[...])
        h = hn.astype(BF)
        y = jnp.zeros((tm, d), F32)
        for part in range(2):
            cols = slice(part * fc, (part + 1) * fc)
            gate = dot_nt(h, wg_ref[...].reshape(f, d)[cols])
            up = dot_nt(h, wu_ref[...].reshape(f, d)[cols])
            act = (gate * jax.nn.sigmoid(gate) * up).astype(BF)
            y = y + dot_nn(act, wd_ref[...].reshape(f, d)[cols])
            gate_ref[:, cols] = gate.astype(BF)
            up_ref[:, cols] = up.astype(BF)
        o_ref[...] = x + FFN_RES * y

    row = pl.BlockSpec((tm, d), lambda i: (i, 0))
    wide = pl.BlockSpec((tm, f), lambda i: (i, 0))
    return pl.pallas_call(
        body, name="ffn_forward", grid=(s_len // tm,),
        out_shape=(SDS((s_len, d), F32), SDS((s_len, f), BF), SDS((s_len, f), BF)),
        in_specs=[row, pl.BlockSpec((1, d), lambda i: (0, 0))]
        + [_w_spec(lay.fh, d, 3 * which + m) for m in range(3)] + [ANY] * len(deps),
        out_specs=(row, wide, wide), compiler_params=_cparams(),
    )(x, gain, wfull, wfull, wfull, *deps)


def ffn_backward_dx(x, gain, dout, gate_bf, up_bf, wfull, lay, which, deps=(), tm=256):
    s_len, d = x.shape
    tm = min(tm, s_len)
    f = 8 * lay.fh
    fc = f // 2
    nt = s_len // tm

    def body(x_ref, g_ref, do_ref, gate_ref, up_ref, wg_ref, wu_ref, wd_ref, *rest):
        dx_ref, dg_ref, dgate_ref, dup_ref, act_ref, h_ref, df_ref = rest[len(deps):]
        x = x_ref[...]
        g = g_ref[...]
        xh, rs, hn = _rms(x, g)
        h = hn.astype(BF)
        do = do_ref[...]
        df = (FFN_RES * do).astype(BF)
        dh = jnp.zeros((tm, d), F32)
        for part in range(2):
            cols = slice(part * fc, (part + 1) * fc)
            wg = wg_ref[...].reshape(f, d)[cols]
            wu = wu_ref[...].reshape(f, d)[cols]
            gate = gate_ref[:, cols].astype(F32)
            up = up_ref[:, cols].astype(F32)
            sg = jax.nn.sigmoid(gate)
            silu = gate * sg
            dact = dot_nt(df, wd_ref[...].reshape(f, d)[cols])
            dup = (dact * silu).astype(BF)
            dgate = (dact * up * (sg * (1.0 + gate * (1.0 - sg)))).astype(BF)
            dh = dh + dot_nn(dgate, wg) + dot_nn(dup, wu)
            dgate_ref[:, cols] = dgate
            dup_ref[:, cols] = dup
            act_ref[:, cols] = (silu * up).astype(BF)
        dxn, dgrow = _rms_bwd(dh, xh, rs, g)
        dx_ref[...] = do + dxn

        @pl.when(pl.program_id(0) == 0)
        def _():
            dg_ref[...] = jnp.zeros_like(dg_ref)

        dg_ref[...] += jnp.sum(dgrow, axis=0, keepdims=True)
        h_ref[...] = h
        df_ref[...] = df

    row = pl.BlockSpec((tm, d), lambda i: (i, 0))
    wide = pl.BlockSpec((tm, f), lambda i: (i, 0))
    vec = pl.BlockSpec((1, d), lambda i: (0, 0))
    return pl.pallas_call(
        body, name="ffn_backward_dx", grid=(nt,),
        out_shape=(SDS((s_len, d), F32), SDS((1, d), F32), SDS((s_len, f), BF), SDS((s_len, f), BF),
                   SDS((s_len, f), BF), SDS((s_len, d), BF), SDS((s_len, d), BF)),
        in_specs=[row, vec, row, wide, wide] + [_w_spec(lay.fh, d, 3 * which + m) for m in range(3)] + [ANY] * len(deps),
        out_specs=(row, vec, wide, wide, wide, row, row), compiler_params=_cparams(),
    )(x, gain, dout, gate_bf, up_bf, wfull, wfull, wfull, *deps)


def weight_grad_tn(a, b, gb, lay, blk, tk=512):
    s_len, f = a.shape
    tk = min(tk, s_len)
    d = b.shape[1]
    fc = f // 2
    nk = s_len // tk

    def body(a_ref, b_ref, gb_ref, o_ref, acc):
        kt = pl.program_id(1)

        @pl.when(kt == 0)
        def _():
            acc[...] = jnp.zeros_like(acc)

        acc[...] += dot_tn(a_ref[...], b_ref[...])

        @pl.when(kt == nk - 1)
        def _():
            for p in range(2):
                for q in range(2):
                    o_ref[p, q] = acc[pl.ds((2 * p + q) * lay.fh, lay.fh), :].astype(o_ref.dtype)

    return pl.pallas_call(
        body, name="weight_grad_tn", grid=(2, nk), out_shape=SDS(gb.shape, gb.dtype),
        in_specs=[pl.BlockSpec((tk, fc), lambda j, kt: (kt, j)), pl.BlockSpec((tk, d), lambda j, kt: (kt, 0)), ANY],
        out_specs=pl.BlockSpec((2, 2, lay.fh, d), lambda j, kt: (j, 0, blk, 0)),
        scratch_shapes=[pltpu.VMEM((fc, d), F32)],
        input_output_aliases={2: 0}, compiler_params=_cparams(),
    )(a, b, gb)


def _lane_blocks(v):
    return [v[:, j * LANE:(j + 1) * LANE] for j in range(v.shape[1] // LANE)]


def _join_lane_blocks(ref):
    return jnp.concatenate([ref[j] for j in range(ref.shape[0])], axis=1)


def _cbm_spec(nblk, rows, first=0):
    return pl.BlockSpec((nblk, rows, LANE), lambda i: (first // nblk, i, 0))


def mix_project(x, gain, wfull, lay, lw, att, tm=512):
    s_len, d = x.shape
    tm = min(tm, s_len)
    d_in = 8 * lay.ih
    kvw = (d_in - 2 * lw - att) // 2
    ncol = (2 * lw + 2 * kvw) // LANE

    def body(x_ref, g_ref, w_ref, o_ref, qt_ref, vt_ref):
        _, _, hn = _rms(x_ref[...], g_ref[...])
        h = hn.astype(BF)
        w = w_ref[:, :, :lay.ih, :].reshape(d_in, d)
        pieces = _lane_blocks(dot_nt(h, w[:2 * lw])) + _lane_blocks(dot_nt(h, w[2 * lw + att:]))
        for j, piece in enumerate(pieces):
            o_ref[j] = piece
        qt_ref[...] = dot_nt(w[2 * lw:2 * lw + att], h)
        vt_ref[...] = dot_nt(w[2 * lw + att + kvw:], h)

    return pl.pallas_call(
        body, name="mix_project", grid=(s_len // tm,),
        out_shape=(SDS((ncol, s_len, LANE), F32), SDS((att, s_len), F32), SDS((kvw, s_len), F32)),
        in_specs=[pl.BlockSpec((tm, d), lambda i: (i, 0)), pl.BlockSpec((1, d), lambda i: (0, 0)),
                  _w_spec(lay.fh, d, lay.MIX_BLK)],
        out_specs=(_cbm_spec(ncol, tm), pl.BlockSpec((att, tm), lambda i: (0, i)), pl.BlockSpec((kvw, tm), lambda i: (0, i))),
        compiler_params=_cparams(),
    )(x, gain, wfull)


def mix_project_backward(x, gain, dout, dxr, dgt, dqt, dkv, dwout, wfull, gb, lay, tm=512):
    s_len, d = x.shape
    tm = min(tm, s_len)
    d_in = 8 * lay.ih
    nt = s_len // tm
    kvw = dkv.shape[1]
    att = dqt.shape[0]
    nlru = (dxr.shape[0] + dgt.shape[0]) * LANE

    def body(x_ref, g_ref, do_ref, dxr_ref, dgt_ref, dqt_ref, dkv_ref, dwo_ref, w_ref, gb_ref, dx_ref, dg_ref, o_ref, acc):
        i = pl.program_id(0)
        g = g_ref[...]
        xh, rs, hn = _rms(x_ref[...], g)
        h = hn.astype(BF)
        w = w_ref[:, :, :lay.ih, :].reshape(d_in, d)
        dlru = jnp.concatenate([_join_lane_blocks(dxr_ref), _join_lane_blocks(dgt_ref)], axis=1).astype(BF)
        dqt = dqt_ref[...].astype(BF)
        dkv = dkv_ref[...].astype(BF)
        dh = dot_nn(dlru, w[:nlru]) + dot_tn(dqt, w[nlru:nlru + att]) + dot_nn(dkv, w[nlru + att:])
        dxn, dgrow = _rms_bwd(dh, xh, rs, g)
        dx_ref[...] = do_ref[...] + dxn

        @pl.when(i == 0)
        def _():
            dg_ref[...] = jnp.zeros_like(dg_ref)
            acc[...] = jnp.zeros_like(acc)

        dg_ref[...] += jnp.sum(dgrow, axis=0, keepdims=True)
        acc[0:nlru, :] += dot_tn(dlru, h)
        acc[nlru:nlru + att, :] += dot_nn(dqt, h)
        acc[nlru + att:, :] += dot_tn(dkv, h)

        @pl.when(i == nt - 1)
        def _():
            for p in range(N_CHIPS):
                for q in range(2):
                    o_ref[p, q, :lay.ih, :] = acc[pl.ds((2 * p + q) * lay.ih, lay.ih), :].astype(o_ref.dtype)
            o_ref[:, :, lay.ih:, :] = dwo_ref[...]

    row = pl.BlockSpec((tm, d), lambda i: (i, 0))
    vec = pl.BlockSpec((1, d), lambda i: (0, 0))
    return pl.pallas_call(
        body, name="mix_project_backward", grid=(nt,),
        out_shape=(SDS((s_len, d), F32), SDS((1, d), F32), SDS(gb.shape, gb.dtype)),
        in_specs=[row, vec, row, _cbm_spec(dxr.shape[0], tm), _cbm_spec(dgt.shape[0], tm),
                  pl.BlockSpec((att, tm), lambda i: (0, i)), pl.BlockSpec((tm, kvw), lambda i: (i, 0)),
                  pl.BlockSpec(dwout.shape, lambda i: (0, 0, 0, 0)), _w_spec(lay.fh, d, lay.MIX_BLK), ANY],
        out_specs=(row, vec, pl.BlockSpec((N_CHIPS, 2, lay.fh, d), lambda i: (0, 0, lay.MIX_BLK, 0))),
        scratch_shapes=[pltpu.VMEM((d_in, d), F32)],
        input_output_aliases={9: 2}, compiler_params=_cparams(),
    )(x, gain, dout, dxr, dgt, dqt, dkv, dwout, wfull, gb)


def _lru_gates(xc, wb_ref, pv_ref, direction):
    xcb = xc.astype(BF)
    r = jax.nn.sigmoid(dot_nn(xcb, wb_ref[2 * direction]) + pv_ref[1 + direction:2 + direction, :])
    i = jax.nn.sigmoid(dot_nn(xcb, wb_ref[2 * direction + 1]) + pv_ref[3 + direction:4 + direction, :])
    lam = pv_ref[5 + direction:6 + direction, :]
    sp = jnp.maximum(-lam, 0.0) + jnp.log(1.0 + jnp.exp(-jnp.abs(lam)))
    a = jnp.exp(-LRU_C * sp * r)
    mult = jnp.sqrt(1.0 - a * a)
    return xcb, r, i, a, mult, sp


def _conv_rows(xr, cv_ref, bias, n):
    acc = bias + cv_ref[0:1, :] * _shift_rows(xr, -2, n)
    for j in range(1, CONV_WIDTH):
        acc = acc + cv_ref[j:j + 1, :] * _shift_rows(xr, j - 2, n)
    return acc


def lru_forward(proj, cvec, pvec, wblk, lw, deps=(), ch=512):
    s_len = proj.shape[1]
    ncb = lw // LANE
    ch = min(ch, s_len)
    nchunk = s_len // ch

    def body(xr_ref, gt_ref, cv_ref, pv_ref, wb_ref, *rest):
        y_ref, hs_ref, xc_s, a_s, u_s, acum_s = rest[len(deps):]
        xc_s[...] = _conv_rows(xr_ref[...], cv_ref, pv_ref[0:1, :], s_len)
        for direction in range(2):
            def fill(ci, _):
                rows = pl.ds(pl.multiple_of(ci * ch, ch), ch)
                xc = xc_s[rows, :]
                _, _, i, a, mult, _ = _lru_gates(xc, wb_ref, pv_ref, direction)
                a_s[rows, :] = a
                u_s[rows, :] = mult * (i * xc)
                return 0

            lax.fori_loop(0, nchunk, fill, 0)
            _scan_rows(a_s, u_s, hs_ref.at[direction], acum_s, reverse=direction == 1)

        def out(ci, _):
            rows = pl.ds(pl.multiple_of(ci * ch, ch), ch)
            gl, _ = _gelu(gt_ref[rows, :])
            y_ref[rows, :] = gl * (hs_ref[0, rows, :] + hs_ref[1, rows, :])
            return 0

        lax.fori_loop(0, nchunk, out, 0)

    col = lambda off: pl.BlockSpec((None, s_len, LANE), lambda cb: (off + cb, 0, 0))
    return pl.pallas_call(
        body, name="lru_forward", grid=(ncb,),
        out_shape=(SDS((ncb, s_len, LANE), F32), SDS((2, ncb, s_len, LANE), F32)),
        in_specs=[col(0), col(ncb), pl.BlockSpec((8, LANE), lambda cb: (0, cb)), pl.BlockSpec((8, LANE), lambda cb: (0, cb)),
                  pl.BlockSpec((4, None, LANE, LANE), lambda cb: (0, cb, 0, 0))] + [ANY] * len(deps),
        out_specs=(col(0), pl.BlockSpec((2, None, s_len, LANE), lambda cb: (0, cb, 0, 0))),
        scratch_shapes=[pltpu.VMEM((s_len, LANE), F32)] * 4, compiler_params=_cparams(),
    )(proj, proj, cvec, pvec, wblk, *deps)


def lru_backward(proj, hs, dy, cvec, pvec, wblk, lw, ch=512):
    s_len = proj.shape[1]
    ncb = lw // LANE
    ch = min(ch, s_len)
    nchunk = s_len // ch

    def body(xr_ref, gt_ref, hs_ref, dy_ref, cv_ref, pv_ref, wb_ref, dxr_ref, dgt_ref, dcv_ref, dpv_ref, dwb_ref,
             xc_s, a_s, dh_s, lam_s, hp_s, dxc_s, acum_s):
        xr = xr_ref[...]
        xc_s[...] = _conv_rows(xr, cv_ref, pv_ref[0:1, :], s_len)
        dxc_s[...] = jnp.zeros_like(dxc_s)
        dpv_ref[...] = jnp.zeros_like(dpv_ref)
        dwb_ref[...] = jnp.zeros_like(dwb_ref)

        def head(ci, _):
            rows = pl.ds(pl.multiple_of(ci * ch, ch), ch)
            gt = gt_ref[rows, :]
            gl, t = _gelu(gt)
            dy = dy_ref[rows, :]
            dh_s[rows, :] = dy * gl
            dgt_ref[rows, :] = dy * (hs_ref[0, rows, :] + hs_ref[1, rows, :]) * _gelu_grad(gt, t)
            return 0

        lax.fori_loop(0, nchunk, head, 0)

        for direction in range(2):
            def fill(ci, _):
                rows = pl.ds(pl.multiple_of(ci * ch, ch), ch)
                _, _, _, a, _, _ = _lru_gates(xc_s[rows, :], wb_ref, pv_ref, direction)
                a_s[rows, :] = a
                return 0

            lax.fori_loop(0, nchunk, fill, 0)
            toward = 1 if direction == 0 else -1
            hp_s[...] = _shift_rows(a_s[...], toward, s_len)
            _scan_rows(hp_s, dh_s, lam_s, acum_s, reverse=direction == 0)
            hp_s[...] = _shift_rows(hs_ref[direction], -toward, s_len)

            def grads(ci, _):
                rows = pl.ds(pl.multiple_of(ci * ch, ch), ch)
                xc = xc_s[rows, :]
                xcb, r, i, a, mult, sp = _lru_gates(xc, wb_ref, pv_ref, direction)
                du = lam_s[rows, :]
                da = du * hp_s[rows, :]
                dmult = du * i * xc
                di = du * mult * xc
                dlog_a = (da - dmult * a / mult) * a
                dr = dlog_a * (-LRU_C * sp)
                dza = dr * r * (1.0 - r)
                dzx = di * i * (1.0 - i)
                dzab = dza.astype(BF)
                dzxb = dzx.astype(BF)
                dxc_s[rows, :] += (du * mult * i + dot_nt(dzab, wb_ref[2 * direction])
                                   + dot_nt(dzxb, wb_ref[2 * direction + 1]))
                dwb_ref[2 * direction] += dot_tn(xcb, dzab)
                dwb_ref[2 * direction + 1] += dot_tn(xcb, dzxb)
                dpv_ref[1 + direction:2 + direction, :] += jnp.sum(dza, axis=0, keepdims=True)
                dpv_ref[3 + direction:4 + direction, :] += jnp.sum(dzx, axis=0, keepdims=True)
                dpv_ref[5 + direction:6 + direction, :] += jnp.sum(dlog_a * (-LRU_C * r), axis=0, keepdims=True)
                return 0

            lax.fori_loop(0, nchunk, grads, 0)

        for direction in range(2):
            lam = pv_ref[5 + direction:6 + direction, :]
            dpv_ref[5 + direction:6 + direction, :] = dpv_ref[5 + direction:6 + direction, :] * (-jax.nn.sigmoid(-lam))
        dxc = dxc_s[...]
        dpv_ref[0:1, :] = jnp.sum(dxc, axis=0, keepdims=True)
        dxr = cv_ref[0:1, :] * _shift_rows(dxc, 2, s_len)
        for j in range(1, CONV_WIDTH):
            dxr = dxr + cv_ref[j:j + 1, :] * _shift_rows(dxc, 2 - j, s_len)
        dxr_ref[...] = dxr
        dcv_ref[...] = jnp.zeros_like(dcv_ref)
        for j in range(CONV_WIDTH):
            dcv_ref[j:j + 1, :] = jnp.sum(dxc * _shift_rows(xr, j - 2, s_len), axis=0, keepdims=True)

    col = lambda off: pl.BlockSpec((None, s_len, LANE), lambda cb: (off + cb, 0, 0))
    own = col(0)
    small = pl.BlockSpec((8, LANE), lambda cb: (0, cb))
    wspec = pl.BlockSpec((4, None, LANE, LANE), lambda cb: (0, cb, 0, 0))
    return pl.pallas_call(
        body, name="lru_backward", grid=(ncb,),
        out_shape=(SDS((ncb, s_len, LANE), F32), SDS((ncb, s_len, LANE), F32), SDS((8, lw), F32), SDS((8, lw), F32),
                   SDS(wblk.shape, F32)),
        in_specs=[col(0), col(ncb), pl.BlockSpec((2, None, s_len, LANE), lambda cb: (0, cb, 0, 0)), own, small, small, wspec],
        out_specs=(own, own, small, small, wspec),
        scratch_shapes=[pltpu.VMEM((s_len, LANE), F32)] * 7, compiler_params=_cparams(),
    )(proj, proj, hs, dy, cvec, pvec, wblk)


def _window_specs(s_len, first, width=None):
    nb = s_len // BLOCK
    where = (lambda n: jnp.maximum(n - 1, 0), lambda n: n, lambda n: jnp.minimum(n + 1, nb - 1))
    if width is None:
        return [pl.BlockSpec((None, BLOCK, LANE), lambda n, f=f: (first, f(n), 0)) for f in where]
    return [pl.BlockSpec((width, BLOCK), lambda n, f=f: (0, f(n))) for f in where]


def _stack_heads(v, kh):
    return jnp.concatenate([v[(kh * KV_GROUP + g) * HEAD_DIM:(kh * KV_GROUP + g + 1) * HEAD_DIM, :]
                            for g in range(KV_GROUP)], axis=1)


def _unstack_heads(ref, kh, v):
    for g in range(KV_GROUP):
        h = kh * KV_GROUP + g
        ref[h * HEAD_DIM:(h + 1) * HEAD_DIM, :] = v[:, g * BLOCK:(g + 1) * BLOCK]


def _key_exists(n, nb):
    j = lax.broadcasted_iota(jnp.int32, (3 * BLOCK, 1), 0)
    return ((n > 0) | (j >= BLOCK)) & ((n < nb - 1) | (j < 2 * BLOCK))


def _attn_probs(qs, kcat, bias_g, sink_g, key_ok):
    logits = jnp.where(key_ok, dot_nn(kcat, qs) + bias_g, NEG_INF)
    m = jnp.maximum(jnp.max(logits, axis=0, keepdims=True), sink_g)
    p = jnp.exp(logits - m)
    es = jnp.exp(sink_g - m)
    inv = 1.0 / (jnp.sum(p, axis=0, keepdims=True) + es)
    return p * inv, es * inv


def attention_forward(qt, proj, vt, bias, sink, kblk):
    att, s_len = qt.shape
    kvw = vt.shape[0]
    nb = s_len // BLOCK

    def body(q_ref, kp_ref, kc_ref, kn_ref, vp_ref, vc_ref, vn_ref, b_ref, s_ref, o_ref):
        n = pl.program_id(0)
        q = q_ref[...]
        key_ok = _key_exists(n, nb)
        kall = jnp.concatenate([kp_ref[...], kc_ref[...], kn_ref[...]], axis=0).astype(BF)
        vall = jnp.concatenate([vp_ref[...], vc_ref[...], vn_ref[...]], axis=1).astype(BF)
        for kh in range(N_KV_HEADS):
            qs = (_stack_heads(q, kh) * (HEAD_DIM ** -0.5)).astype(BF)
            p, _ = _attn_probs(qs, kall[:, kh * HEAD_DIM:(kh + 1) * HEAD_DIM], b_ref[kh], s_ref[kh, 0:1, :], key_ok)
            _unstack_heads(o_ref, kh, dot_nn(vall[kh * HEAD_DIM:(kh + 1) * HEAD_DIM, :], p.astype(BF)))

    blk = pl.BlockSpec((att, BLOCK), lambda n: (0, n))
    return pl.pallas_call(
        body, name="attention_forward", grid=(nb,), out_shape=SDS((att, s_len), F32),
        in_specs=[blk] + _window_specs(s_len, kblk) + _window_specs(s_len, 0, kvw)
        + [pl.BlockSpec(bias.shape, lambda n: (0, 0, 0)), pl.BlockSpec(sink.shape, lambda n: (0, 0, 0))],
        out_specs=blk, compiler_params=_cparams(),
    )(qt, proj, proj, proj, vt, vt, vt, bias, sink)


def attention_backward(qt, proj, y_att, dy, bias, sink, kblk):
    att, s_len = qt.shape
    nb = s_len // BLOCK
    kvw = N_KV_HEADS * HEAD_DIM

    def body(q_ref, kp_ref, kc_ref, kn_ref, vp_ref, vc_ref, vn_ref, o_ref, do_ref, b_ref, s_ref,
             dq_ref, dkv_ref, db_ref, ds_ref):
        n = pl.program_id(0)

        @pl.when(n == 0)
        def _():
            dkv_ref[...] = jnp.zeros_like(dkv_ref)
            db_ref[...] = jnp.zeros_like(db_ref)
            ds_ref[...] = jnp.zeros_like(ds_ref)

        q = q_ref[...]
        o = o_ref[...]
        do = do_ref[...]
        kall = jnp.concatenate([kp_ref[...], kc_ref[...], kn_ref[...]], axis=0).astype(BF)
        vall = jnp.concatenate([vp_ref[...], vc_ref[...], vn_ref[...]], axis=0).astype(BF)
        key_ok = _key_exists(n, nb)
        dks, dvs = [], []
        for kh in range(N_KV_HEADS):
            kcat = kall[:, kh * HEAD_DIM:(kh + 1) * HEAD_DIM]
            vcat = vall[:, kh * HEAD_DIM:(kh + 1) * HEAD_DIM]
            qs = (_stack_heads(q, kh) * (HEAD_DIM ** -0.5)).astype(BF)
            p, ps = _attn_probs(qs, kcat, b_ref[kh], s_ref[kh, 0:1, :], key_ok)
            dos = _stack_heads(do, kh)
            dosb = dos.astype(BF)
            delta = jnp.sum(dos * _stack_heads(o, kh), axis=0, keepdims=True)
            dlog = p * (dot_nn(vcat, dosb) - delta)
            dlogb = dlog.astype(BF)
            db_ref[kh] += dlog
            ds_ref[kh] += jnp.broadcast_to(-ps * delta, ds_ref.shape[1:])
            _unstack_heads(dq_ref, kh, dot_tn(kcat, dlogb) * (HEAD_DIM ** -0.5))
            dks.append(dot_nt(dlogb, qs))
            dvs.append(dot_nt(p.astype(BF), dosb))
        dkv = jnp.concatenate(dks + dvs, axis=1)
        starts = [jnp.maximum(n - 1, 0), n, jnp.minimum(n + 1, nb - 1)]
        for b, st in enumerate(starts):
            rows = pl.ds(pl.multiple_of(st * BLOCK, BLOCK), BLOCK)
            dkv_ref[rows, :] += dkv[b * BLOCK:(b + 1) * BLOCK, :]

    blk = pl.BlockSpec((att, BLOCK), lambda n: (0, n))
    whole = lambda a: pl.BlockSpec(a.shape, lambda n: (0, 0, 0))
    return pl.pallas_call(
        body, name="attention_backward", grid=(nb,),
        out_shape=(SDS((att, s_len), F32), SDS((s_len, 2 * kvw), F32), SDS(bias.shape, F32), SDS(sink.shape, F32)),
        in_specs=[blk] + _window_specs(s_len, kblk) + _window_specs(s_len, kblk + 1) + [blk, blk, whole(bias), whole(sink)],
        out_specs=(blk, pl.BlockSpec((s_len, 2 * kvw), lambda n: (0, 0)), whole(bias), whole(sink)),
        compiler_params=_cparams(),
    )(qt, proj, proj, proj, proj, proj, proj, y_att, dy, bias, sink)


def _rms_cols(x, g):
    rs = lax.rsqrt(jnp.mean(x * x, axis=0, keepdims=True) + EPS)
    xh = x * rs
    return xh, rs, xh * g


def _rms_cols_bwd(dy, xh, rs, g):
    dxh = dy * g
    dx = rs * (dxh - xh * jnp.mean(dxh * xh, axis=0, keepdims=True))
    return dx, dy * xh


def mix_output(x, y_rec, y_att, g_rec, g_att, wfull, lay, tm=512):
    s_len, d = x.shape
    tm = min(tm, s_len)
    lw = y_rec.shape[0] * LANE
    att = y_att.shape[0]

    def body(x_ref, yr_ref, ya_ref, gr_ref, ga_ref, w_ref, o_ref):
        _, _, nr = _rms(_join_lane_blocks(yr_ref), gr_ref[...])
        _, _, na = _rms_cols(ya_ref[...], ga_ref[...])
        w = w_ref[:, :, lay.ih:, :].reshape(d, d)
        o_ref[...] = x_ref[...] + dot_nn(nr.astype(BF), w[:lw]) + dot_tn(na.astype(BF), w[lw:])

    row = pl.BlockSpec((tm, d), lambda i: (i, 0))
    return pl.pallas_call(
        body, name="mix_output", grid=(s_len // tm,), out_shape=SDS((s_len, d), F32),
        in_specs=[row, _cbm_spec(lw // LANE, tm), pl.BlockSpec((att, tm), lambda i: (0, i)),
                  pl.BlockSpec((1, lw), lambda i: (0, 0)), pl.BlockSpec((att, 1), lambda i: (0, 0)),
                  _w_spec(lay.fh, d, lay.MIX_BLK)],
        out_specs=row, compiler_params=_cparams(),
    )(x, y_rec, y_att, g_rec, g_att, wfull)


def mix_output_backward(dout, y_rec, y_att, g_rec, g_att, wfull, lay, tm=512):
    s_len, d = dout.shape
    tm = min(tm, s_len)
    lw = y_rec.shape[0] * LANE
    att = y_att.shape[0]
    nt = s_len // tm

    def body(do_ref, yr_ref, ya_ref, gr_ref, ga_ref, w_ref, dyr_ref, dya_ref, dgr_ref, dga_ref, o_ref, acc):
        i = pl.program_id(0)
        gr = gr_ref[...]
        ga = ga_ref[...]
        xhr, rsr, nr = _rms(_join_lane_blocks(yr_ref), gr)
        xha, rsa, na = _rms_cols(ya_ref[...], ga)
        dob = do_ref[...].astype(BF)
        w = w_ref[:, :, lay.ih:, :].reshape(d, d)
        dyr, dgr_row = _rms_bwd(dot_nt(dob, w[:lw]), xhr, rsr, gr)
        dya, dga_col = _rms_cols_bwd(dot_nt(w[lw:], dob), xha, rsa, ga)
        for j, piece in enumerate(_lane_blocks(dyr)):
            dyr_ref[j] = piece
        dya_ref[...] = dya

        @pl.when(i == 0)
        def _():
            dgr_ref[...] = jnp.zeros_like(dgr_ref)
            dga_ref[...] = jnp.zeros_like(dga_ref)
            acc[...] = jnp.zeros_like(acc)

        dgr_ref[...] += jnp.sum(dgr_row, axis=0, keepdims=True)
        dga_ref[...] += jnp.sum(dga_col, axis=1, keepdims=True)
        acc[0:lw, :] += dot_tn(nr.astype(BF), dob)
        acc[lw:, :] += dot_nn(na.astype(BF), dob)

        @pl.when(i == nt - 1)
        def _():
            for p in range(N_CHIPS):
                for q in range(2):
                    o_ref[p, q] = acc[pl.ds((2 * p + q) * lay.oh, lay.oh), :].astype(o_ref.dtype)

    row = pl.BlockSpec((tm, d), lambda i: (i, 0))
    return pl.pallas_call(
        body, name="mix_output_backward", grid=(nt,),
        out_shape=(SDS(y_rec.shape, F32), SDS(y_att.shape, F32), SDS((1, lw), F32), SDS((att, 1), F32),
                   SDS((N_CHIPS, 2, lay.oh, d), BF)),
        in_specs=[row, _cbm_spec(lw // LANE, tm), pl.BlockSpec((att, tm), lambda i: (0, i)),
                  pl.BlockSpec((1, lw), lambda i: (0, 0)), pl.BlockSpec((att, 1), lambda i: (0, 0)),
                  _w_spec(lay.fh, d, lay.MIX_BLK)],
        out_specs=(_cbm_spec(lw // LANE, tm), pl.BlockSpec((att, tm), lambda i: (0, i)),
                   pl.BlockSpec((1, lw), lambda i: (0, 0)), pl.BlockSpec((att, 1), lambda i: (0, 0)),
                   pl.BlockSpec((N_CHIPS, 2, lay.oh, d), lambda i: (0, 0, 0, 0))),
        scratch_shapes=[pltpu.VMEM((d, d), F32)], compiler_params=_cparams(),
    )(dout, y_rec, y_att, g_rec, g_att, wfull)


def loss_head(x, gain, target, tm=512):
    s_len, d = x.shape
    tm = min(tm, s_len)

    def body(x_ref, g_ref, t_ref, dx_ref, dg_ref, loss_ref):
        g = g_ref[...]
        xh, rs, y = _rms(x_ref[...], g)
        err = y - t_ref[...]

        @pl.when(pl.program_id(0) == 0)
        def _():
            dg_ref[...] = jnp.zeros_like(dg_ref)
            loss_ref[...] = jnp.zeros_like(loss_ref)

        part = 0.5 * jnp.sum(jnp.mean(err * err, axis=-1, keepdims=True), axis=0, keepdims=True)
        loss_ref[...] += jnp.broadcast_to(part, loss_ref.shape)
        dx, dgrow = _rms_bwd(err * (1.0 / d), xh, rs, g)
        dx_ref[...] = dx
        dg_ref[...] += jnp.sum(dgrow, axis=0, keepdims=True)

    row = pl.BlockSpec((tm, d), lambda i: (i, 0))
    vec = pl.BlockSpec((1, d), lambda i: (0, 0))
    return pl.pallas_call(
        body, name="loss_head", grid=(s_len // tm,),
        out_shape=(SDS((s_len, d), F32), SDS((1, d), F32), SDS((8, LANE), F32)),
        in_specs=[row, vec, row], out_specs=(row, vec, pl.BlockSpec((8, LANE), lambda i: (0, 0))),
        compiler_params=_cparams(),
    )(x, gain, target)


def _adamw_update(w, g, m, v):
    m = ADAM_B1 * m + (1.0 - ADAM_B1) * g
    v = ADAM_B2 * v + (1.0 - ADAM_B2) * (g * g)
    m_hat = m / (1.0 - ADAM_B1 ** ADAM_STEP)
    v_hat = v / (1.0 - ADAM_B2 ** ADAM_STEP)
    return -ADAM_LR * (m_hat / (jnp.sqrt(v_hat) + ADAM_EPS) + ADAM_WD * w), m, v


def adamw(w, g, m, v, tr=512):
    rows, cols = w.shape
    tr = _row_chunk(rows, tr, 8)

    def body(w_ref, g_ref, m_ref, v_ref, d_ref, nm_ref, nv_ref):
        d_ref[...], nm_ref[...], nv_ref[...] = _adamw_update(w_ref[...], g_ref[...], m_ref[...], v_ref[...])

    blk = pl.BlockSpec((tr, cols), lambda i: (i, 0))
    return pl.pallas_call(
        body, name="adamw", grid=(rows // tr,), out_shape=(SDS(w.shape, F32),) * 3,
        in_specs=[blk] * 4, out_specs=(blk,) * 3, compiler_params=_cparams(),
    )(w, g, m, v)


def adamw_layer(gf, blk, row_off, n_half, l, w, m, v, outs, deps=()):
    fh = gf.shape[1] // 7
    d = gf.shape[2]
    nd = len(deps)

    def body(gf_ref, w_ref, m_ref, v_ref, *rest):
        g_ref, d_ref, nm_ref, nv_ref = rest[4 + nd:]
        g = gf_ref[row_off:row_off + n_half, :]
        g_ref[...] = g
        d_ref[...], nm_ref[...], nv_ref[...] = _adamw_update(w_ref[...], g, m_ref[...], v_ref[...])

    gspec = pl.BlockSpec((None, fh, d), lambda h: (h, blk, 0))
    wspec = pl.BlockSpec((None, n_half, d), lambda h: (l, h, 0))
    return pl.pallas_call(
        body, name="adamw_layer", grid=(2,), out_shape=tuple(SDS(o.shape, o.dtype) for o in outs),
        in_specs=[gspec, wspec, wspec, wspec] + [ANY] * (4 + nd), out_specs=(wspec,) * 4,
        input_output_aliases={4 + i: i for i in range(4)}, compiler_params=_cparams(),
    )(gf, w, m, v, *outs, *deps)


def pack_weight(pos, land, blk, l, w, extra=None, deps=()):
    fh, d = land.shape[2] // 7, land.shape[3]
    nd = len(deps)

    def body(pos_ref, w_ref, *rest):
        o_ref = rest[-1]
        a = w_ref[...].astype(BF)
        n = a.shape[0] // 2
        for h in range(2):
            o_ref[h, 0:n, :] = a[h * n:(h + 1) * n]
        if extra is not None:
            b = rest[0][...].astype(BF)
            nb = b.shape[0] // 2
            for h in range(2):
                o_ref[h, n:n + nb, :] = b[h * nb:(h + 1) * nb]

    def whole(a):
        return pl.BlockSpec((None,) + a.shape[1:], lambda i, p: (l, 0, 0))

    ins = [w] + ([extra] if extra is not None else [])
    return pl.pallas_call(
        body, name="pack_weight", out_shape=SDS(land.shape, land.dtype),
        grid_spec=pltpu.PrefetchScalarGridSpec(
            num_scalar_prefetch=1, grid=(1,),
            in_specs=[whole(a) for a in ins] + [ANY] * (1 + nd),
            out_specs=pl.BlockSpec((None, 2, fh, d), lambda i, p: (p[0], 0, blk, 0))),
        input_output_aliases={1 + len(ins): 0}, compiler_params=_cparams(),
    )(pos, *ins, land, *deps)


def _pack_rows(arrays, width):
    flat = jnp.concatenate([a.reshape(-1).astype(F32) for a in arrays])
    rows = -(-flat.shape[0] // (8 * width)) * 8
    return jnp.pad(flat, (0, rows * width - flat.shape[0])).reshape(rows, width)


def _unpack_rows(buf, shapes):
    flat = buf.reshape(-1)
    out, off = [], 0
    for shp in shapes:
        n = int(np.prod(shp))
        out.append(flat[off:off + n].reshape(shp))
        off += n
    return out


def _t5_buckets(rel):
    half = N_BUCKETS // 2
    max_exact = half // 2
    ret = (rel > 0).astype(jnp.int32) * half
    n = jnp.abs(rel)
    n_f = jnp.maximum(n, 1).astype(F32)
    large = max_exact + (jnp.log(n_f / max_exact) / math.log(MAX_DISTANCE / max_exact) * (half - max_exact)).astype(jnp.int32)
    large = jnp.minimum(large, half - 1)
    return ret + jnp.where(n < max_exact, n, large)


def _band_buckets():
    t = jnp.arange(BLOCK)[:, None]
    j = jnp.arange(3 * BLOCK)[None, :]
    rel = j - BLOCK - t
    return _t5_buckets(rel), jnp.abs(rel) <= WINDOW


def _block_diag_pairs(w):
    depth, two, nblk, bw, _ = w.shape
    pairs = w.reshape(depth, two, nblk // 2, 2, bw, bw)
    z = jnp.zeros_like(pairs[:, :, :, 0])
    top = jnp.concatenate([pairs[:, :, :, 0], z], axis=-1)
    bot = jnp.concatenate([z, pairs[:, :, :, 1]], axis=-1)
    return jnp.concatenate([top, bot], axis=-2)


def _diag_blocks(dw):
    bw = dw.shape[-1] // 2
    a = dw[:, :, :bw, :bw]
    b = dw[:, :, bw:, bw:]
    return jnp.stack([a, b], axis=2).reshape(dw.shape[0], 2 * dw.shape[1], bw, bw)


def kernel(x, ffn1_norm, ffn1_w_gate, ffn1_w_up, ffn1_w_down, mix_norm, w_in, conv_w, conv_b, lru_w_a, lru_b_a, lru_w_x, lru_b_x, lru_lambda, attn_sink, rel_bias, lru_out_norm, attn_out_norm, w_out, ffn2_norm, ffn2_w_gate, ffn2_w_up, ffn2_w_down, final_norm, loss_target, m_ffn1_norm, m_ffn1_w_gate, m_ffn1_w_up, m_ffn1_w_down, m_mix_norm, m_w_in, m_conv_w, m_conv_b, m_lru_w_a, m_lru_b_a, m_lru_w_x, m_lru_b_x, m_lru_lambda, m_attn_sink, m_rel_bias, m_lru_out_norm, m_attn_out_norm, m_w_out, m_ffn2_norm, m_ffn2_w_gate, m_ffn2_w_up, m_ffn2_w_down, m_final_norm, v_ffn1_norm, v_ffn1_w_gate, v_ffn1_w_up, v_ffn1_w_down, v_mix_norm, v_w_in, v_conv_w, v_conv_b, v_lru_w_a, v_lru_b_a, v_lru_w_x, v_lru_b_x, v_lru_lambda, v_attn_sink, v_rel_bias, v_lru_out_norm, v_attn_out_norm, v_w_out, v_ffn2_norm, v_ffn2_w_gate, v_ffn2_w_up, v_ffn2_w_down, v_final_norm):
    depth, d = ffn1_norm.shape
    d_ff = N_CHIPS * ffn1_w_gate.shape[2]
    d_in = N_CHIPS * w_in.shape[2]
    lw = conv_b.shape[1]
    att = N_HEADS * HEAD_DIM
    lay = Layout(d, d_ff, d_in)
    k_chip = 2 * lax.axis_index("x") + lax.axis_index("y")
    pos = jnp.stack([k_chip, lax.axis_index("c")]).astype(jnp.int32)

    def rows_major(a):
        return jnp.swapaxes(a, 1, 2)

    mats = (rows_major(ffn1_w_gate), rows_major(ffn1_w_up), ffn1_w_down,
            rows_major(ffn2_w_gate), rows_major(ffn2_w_up), ffn2_w_down)

    def pack_layer(l, deps=()):
        land = lax.empty((N_CHIPS, 2, lay.rows, d), BF)
        for m, a in enumerate(mats):
            land = pack_weight(pos, land, m, l, a, deps=deps if m == 0 else ())
        return pack_weight(pos, land, lay.MIX_BLK, l, rows_major(w_in), extra=w_out)

    def gather_start(l, land):
        return split_start(f"gather_start_{l}", [land], 3, gather_plan)

    def gather_finish(l, started, after):
        ssem, rsem, bufs, _ = started
        land, = split_wait(f"gather_wait_{l}", ssem, rsem, bufs, after, gather_plan)
        return gather_pair(land)

    sharded_small = (conv_w, lru_b_a, lru_b_x, lru_lambda)
    sshard = jnp.concatenate([a.reshape(-1, LANE) for a in sharded_small], axis=0)
    sfull = gather_small(sshard)
    small_full, off = [], 0
    for a in sharded_small:
        r = a.shape[0] * a.shape[1]
        piece = sfull[:, off:off + r].reshape((N_CHIPS,) + a.shape)
        small_full.append(jnp.moveaxis(piece, 0, 2).reshape(a.shape[0], a.shape[1], N_CHIPS * LANE))
        off += r
    conv_w_f, b_a_f, b_x_f, lam_f = small_full

    zrow = jnp.zeros((1, lw), F32)
    wblk_a = _block_diag_pairs(lru_w_a)
    wblk_x = _block_diag_pairs(lru_w_x)
    buckets, in_band = _band_buckets()
    onehot = (buckets.reshape(-1)[:, None] == jnp.arange(N_BUCKETS)[None, :]).astype(F32)
    bias = jnp.dot(rel_bias.T, onehot.T, precision=lax.Precision.HIGHEST).reshape(N_HEADS, BLOCK, 3 * BLOCK)
    bias = jnp.where(in_band[None], bias, NEG_INF)
    bias = bias.reshape(N_KV_HEADS, KV_GROUP, BLOCK, 3 * BLOCK).transpose(0, 3, 1, 2).reshape(N_KV_HEADS, 3 * BLOCK, KV_GROUP * BLOCK)
    kblk = 2 * lw // LANE

    def layer_small(l):
        cvec = jnp.concatenate([conv_w_f[l], jnp.zeros((8 - CONV_WIDTH, lw), F32)], axis=0)
        pvec = jnp.concatenate([conv_b[l][None], b_a_f[l], b_x_f[l], lam_f[l], zrow], axis=0)
        wblk = jnp.stack([wblk_a[l, 0], wblk_x[l, 0], wblk_a[l, 1], wblk_x[l, 1]]).astype(BF)
        sink = jnp.broadcast_to(jnp.repeat(attn_sink[l], BLOCK).reshape(N_KV_HEADS, 1, KV_GROUP * BLOCK),
                                (N_KV_HEADS, 8, KV_GROUP * BLOCK))
        return cvec, pvec, wblk, sink

    xs = x[0]
    wfull = [None] * depth
    first = gather_start(0, pack_layer(0, deps=(sfull,)))
    lands = {l: pack_layer(l, deps=(first[3],)) for l in range(1, depth)}
    wfull[0] = gather_finish(0, first, [xs] + list(lands.values()))
    started = gather_start(1, lands[1]) if depth > 1 else None
    saved = []
    for l in range(depth):
        cvec, pvec, wblk, sink = layer_small(l)
        deps = (started[3],) if started is not None else ()
        x1, gate1, up1 = ffn_forward(xs, ffn1_norm[l][None], wfull[l], lay, 0, deps=deps)
        proj, qt, vt = mix_project(x1, mix_norm[l][None], wfull[l], lay, lw, att)
        y_rec, hs = lru_forward(proj, cvec, pvec, wblk, lw)
        y_att = attention_forward(qt, proj, vt, bias, sink, kblk)
        x2 = mix_output(x1, y_rec, y_att, lru_out_norm[l][None], attn_out_norm[l][:, None], wfull[l], lay)
        x3, gate2, up2 = ffn_forward(x2, ffn2_norm[l][None], wfull[l], lay, 1)
        saved.append((xs, x1, x2, proj, qt, y_rec, hs, y_att, (gate1, up1), (gate2, up2)))
        xs = x3
        if l + 1 < depth:
            wfull[l + 1] = gather_finish(l + 1, started, [x3])
            started = gather_start(l + 2, lands[l + 2]) if l + 2 < depth else None

    dx, d_final, loss_tile = loss_head(xs, final_norm[None], loss_target[0])
    loss = lax.psum(loss_tile[0, 0], ("x", "y", "c"))

    layer_names = ["ffn1_norm", "mix_norm", "conv_w", "conv_b", "lru_w_a", "lru_b_a", "lru_w_x", "lru_b_x", "lru_lambda",
                   "attn_sink", "lru_out_norm", "attn_out_norm", "ffn2_norm"]
    dbias_total = jnp.zeros(bias.shape, F32)

    def ffn_back(xin, gain, dout, pre, gb, l, which, deps=()):
        dxo, dg, dgate, dup, act, h, df = ffn_backward_dx(xin, gain, dout, *pre, wfull[l], lay, which, deps=deps)
        gb = weight_grad_tn(dgate, h, gb, lay, 3 * which + 0)
        gb = weight_grad_tn(dup, h, gb, lay, 3 * which + 1)
        gb = weight_grad_tn(act, df, gb, lay, 3 * which + 2)
        return dxo, dg[0], gb

    def reduce_start(l, gb, sb):
        p1, sp1 = exchange_pair(gb, sb)
        cs = pair_sum(pos, gb, p1)
        ss = small_pair_sum(sb, sp1)
        lands = [lax.empty((3,) + cs.shape[1:], cs.dtype), lax.empty((N_CHIPS,) + ss.shape, ss.dtype)]
        return split_start(f"reduce_start_{l}", [cs, ss] + lands, 6, reduce_plan)

    def reduce_finish(l, started, after):
        ssem, rsem, bufs, _ = started
        cs, ss, p3, sp3 = split_wait(f"reduce_wait_{l}", ssem, rsem, bufs, after, reduce_plan)
        return exchange_final(chip_sum(pos, cs, p3)), small_chip_sum(pos, ss, sp3)

    gf = [None] * depth
    small_sums = [None] * depth
    small_shapes = [None] * depth
    in_flight = None
    for l in reversed(range(depth)):
        x0, x1, x2, proj, qt, y_rec, hs, y_att, pre1, pre2 = saved[l]
        cvec, pvec, wblk, sink = layer_small(l)
        gb = lax.empty((N_CHIPS, 2, lay.rows, d), BF)
        part = {}
        deps = (in_flight[1][3],) if in_flight is not None else ()
        dx, part["ffn2_norm"], gb = ffn_back(x2, ffn2_norm[l][None], dx, pre2, gb, l, 1, deps=deps)
        dyr, dya, dgr, dga, dwout = mix_output_backward(dx, y_rec, y_att, lru_out_norm[l][None], attn_out_norm[l][:, None],
                                                        wfull[l], lay)
        part["lru_out_norm"] = dgr[0]
        part["attn_out_norm"] = dga[:, 0]
        dq, dkv, dbias, dsink = attention_backward(qt, proj, y_att, dya, bias, sink, kblk)
        dbias_total = dbias_total + dbias
        part["attn_sink"] = jnp.sum(dsink[:, 0, :].reshape(N_HEADS, BLOCK), axis=1)
        dxr, dgt, dcv, dpv, dwb = lru_backward(proj, hs, dyr, cvec, pvec, wblk, lw)
        part["conv_w"] = dcv[:CONV_WIDTH]
        part["conv_b"] = dpv[0]
        part["lru_b_a"] = dpv[1:3]
        part["lru_b_x"] = dpv[3:5]
        part["lru_lambda"] = dpv[5:7]
        part["lru_w_a"] = _diag_blocks(jnp.stack([dwb[0], dwb[2]]))
        part["lru_w_x"] = _diag_blocks(jnp.stack([dwb[1], dwb[3]]))
        dx, dgm, gb = mix_project_backward(x1, mix_norm[l][None], dx, dxr, dgt, dq, dkv, dwout, wfull[l], gb, lay)
        part["mix_norm"] = dgm[0]
        dx, part["ffn1_norm"], gb = ffn_back(x0, ffn1_norm[l][None], dx, pre1, gb, l, 0)
        pieces = [part[n] for n in layer_names]
        if l == 0:
            dbias_heads = dbias_total.reshape(N_KV_HEADS, 3 * BLOCK, KV_GROUP, BLOCK).transpose(0, 2, 3, 1)
            d_rel_bias = jnp.dot(dbias_heads.reshape(N_HEADS, -1), onehot, precision=lax.Precision.HIGHEST).T
            pieces += [d_rel_bias, d_final[0]]
        small_shapes[l] = [p.shape for p in pieces]
        if in_flight is not None:
            gf[in_flight[0]], small_sums[in_flight[0]] = reduce_finish(in_flight[0], in_flight[1], [dx])
        in_flight = (l, reduce_start(l, gb, _pack_rows(pieces, 1024)))
    grad_x = dx[None]

    weights = dict(ffn1_norm=ffn1_norm, ffn1_w_gate=ffn1_w_gate, ffn1_w_up=ffn1_w_up, ffn1_w_down=ffn1_w_down, mix_norm=mix_norm, w_in=w_in, conv_w=conv_w, conv_b=conv_b, lru_w_a=lru_w_a, lru_b_a=lru_b_a, lru_w_x=lru_w_x, lru_b_x=lru_b_x, lru_lambda=lru_lambda, attn_sink=attn_sink, rel_bias=rel_bias, lru_out_norm=lru_out_norm, attn_out_norm=attn_out_norm, w_out=w_out, ffn2_norm=ffn2_norm, ffn2_w_gate=ffn2_w_gate, ffn2_w_up=ffn2_w_up, ffn2_w_down=ffn2_w_down, final_norm=final_norm)
    m_in = dict(ffn1_norm=m_ffn1_norm, ffn1_w_gate=m_ffn1_w_gate, ffn1_w_up=m_ffn1_w_up, ffn1_w_down=m_ffn1_w_down, mix_norm=m_mix_norm, w_in=m_w_in, conv_w=m_conv_w, conv_b=m_conv_b, lru_w_a=m_lru_w_a, lru_b_a=m_lru_b_a, lru_w_x=m_lru_w_x, lru_b_x=m_lru_b_x, lru_lambda=m_lru_lambda, attn_sink=m_attn_sink, rel_bias=m_rel_bias, lru_out_norm=m_lru_out_norm, attn_out_norm=m_attn_out_norm, w_out=m_w_out, ffn2_norm=m_ffn2_norm, ffn2_w_gate=m_ffn2_w_gate, ffn2_w_up=m_ffn2_w_up, ffn2_w_down=m_ffn2_w_down, final_norm=m_final_norm)
    v_in = dict(ffn1_norm=v_ffn1_norm, ffn1_w_gate=v_ffn1_w_gate, ffn1_w_up=v_ffn1_w_up, ffn1_w_down=v_ffn1_w_down, mix_norm=v_mix_norm, w_in=v_w_in, conv_w=v_conv_w, conv_b=v_conv_b, lru_w_a=v_lru_w_a, lru_b_a=v_lru_b_a, lru_w_x=v_lru_w_x, lru_b_x=v_lru_b_x, lru_lambda=v_lru_lambda, attn_sink=v_attn_sink, rel_bias=v_rel_bias, lru_out_norm=v_lru_out_norm, attn_out_norm=v_attn_out_norm, w_out=v_w_out, ffn2_norm=v_ffn2_norm, ffn2_w_gate=v_ffn2_w_gate, ffn2_w_up=v_ffn2_w_up, ffn2_w_down=v_ffn2_w_down, final_norm=v_final_norm)
    order = list(weights)
    large = [(name, m, 0, lay.fh, m % 3 != 2) for m, name in
             enumerate(("ffn1_w_gate", "ffn1_w_up", "ffn1_w_down", "ffn2_w_gate", "ffn2_w_up", "ffn2_w_down"))]
    large += [("w_in", lay.MIX_BLK, 0, lay.ih, True), ("w_out", lay.MIX_BLK, lay.ih, lay.oh, False)]
    as_rows = {name: [rows_major(src[name]) if flip else src[name] for src in (weights, m_in, v_in)]
               for name, _, _, _, flip in large}
    stacked = {name: tuple(lax.empty(as_rows[name][0].shape, F32) for _ in range(4)) for name, *_ in large}

    def adamw_large(l, deps=()):
        for i, (name, blk, row_off, n_half, _) in enumerate(large):
            stacked[name] = adamw_layer(gf[l], blk, row_off, n_half, l, *as_rows[name], stacked[name],
                                        deps=deps if i == 0 else ())

    last = in_flight[0]
    for l in range(depth):
        if l != last:
            adamw_large(l, deps=(in_flight[1][3],))
    ready = [buf for name, *_ in large for buf in stacked[name]] if depth > 1 else []
    gf[last], small_sums[last] = reduce_finish(last, in_flight[1], [dx] + ready)
    adamw_large(last)

    per_layer = [_unpack_rows(small_sums[l], small_shapes[l]) for l in range(depth)]
    grads = {n: jnp.stack([per_layer[l][i] for l in range(depth)]) for i, n in enumerate(layer_names)}
    grads["rel_bias"], grads["final_norm"] = per_layer[0][len(layer_names):]
    for name in ("conv_w", "lru_b_a", "lru_b_x", "lru_lambda"):
        grads[name] = lax.dynamic_slice_in_dim(grads[name], k_chip * LANE, LANE, axis=2)
    delta, new_m, new_v = {}, {}, {}
    for name, _, _, _, flip in large:
        grads[name], delta[name], new_m[name], new_v[name] = [rows_major(a) if flip else a for a in stacked[name]]
    small = [n for n in order if n not in stacked]
    packed = [_pack_rows([src[n] for n in small], 1024) for src in (weights, grads, m_in, v_in)]
    outs = adamw(*packed)
    shapes = [weights[n].shape for n in small]
    for dst, buf in zip((delta, new_m, new_v), outs):
        dst.update(zip(small, _unpack_rows(buf, shapes)))

    return (loss, grad_x, *[grads[n] for n in order], *[delta[n] for n in order],
            *[new_m[n] for n in order], *[new_v[n] for n in order])
```

```python
import functools
import math

import jax
import jax.numpy as jnp
import numpy as np
from jax import lax
from jax.experimental import pallas as pl
from jax.experimental.pallas import tpu as pltpu

BF = jnp.bfloat16
F32 = jnp.float32
SDS = jax.ShapeDtypeStruct
MESH = pl.DeviceIdType.MESH
ANY = pl.BlockSpec(memory_space=pl.ANY)

N_CHIPS = 4
N_HEADS = 8
N_KV_HEADS = 2
KV_GROUP = N_HEADS // N_KV_HEADS
HEAD_DIM = 64
BLOCK = 128
WINDOW = 128
N_BUCKETS = 32
MAX_DISTANCE = 128
LRU_C = 8.0
CONV_WIDTH = 4
LANE = 128
SUBLANES = 8
SCAN_CHAINS = 8
EPS = 1e-6
FFN_RES = 0.5
NEG_INF = -1e30
ADAM_LR = 0.001
ADAM_B1 = 0.9
ADAM_B2 = 0.999
ADAM_EPS = 1e-08
ADAM_WD = 0.01
ADAM_STEP = 10
VMEM_LIMIT = 60000 * 1024
GELU_C = math.sqrt(2.0 / math.pi)


def dot_nn(a, b):
    return lax.dot_general(a, b, (((1,), (0,)), ((), ())), preferred_element_type=F32)


def dot_nt(a, b):
    return lax.dot_general(a, b, (((1,), (1,)), ((), ())), preferred_element_type=F32)


def dot_tn(a, b):
    return lax.dot_general(a, b, (((0,), (0,)), ((), ())), preferred_element_type=F32)


def _cparams(**kw):
    return pltpu.CompilerParams(vmem_limit_bytes=VMEM_LIMIT, **kw)


class Layout:
    MIX_BLK = 6

    def __init__(self, d_model, d_ff, d_in):
        self.fh = d_ff // (2 * N_CHIPS)
        self.ih = d_in // (2 * N_CHIPS)
        self.oh = d_model // (2 * N_CHIPS)
        assert self.ih + self.oh == self.fh, "w_in^T and w_out rows must fill one ffn-sized block"
        self.rows = 7 * self.fh


def _row_chunk(rows, target, step=16):
    best = rows
    for c in range(step, min(rows, target) + 1, step):
        if rows % c == 0:
            best = c
    return best


def _mesh_pos():
    return lax.axis_index("x"), lax.axis_index("y"), lax.axis_index("c")


def _rcopy(src, dst, ssem, rsem, dev):
    return pltpu.make_async_remote_copy(src_ref=src, dst_ref=dst, send_sem=ssem, recv_sem=rsem,
                                        device_id=dev, device_id_type=MESH)


HBM = pl.BlockSpec(memory_space=pltpu.HBM)
SEM = pl.BlockSpec(memory_space=pltpu.SEMAPHORE)
DATAFLOW = pltpu.SideEffectType.DATAFLOW_SIDE_EFFECTING


def _chip_peers():
    x, y, c = _mesh_pos()
    peers = [(1 - x, y), (x, 1 - y), (1 - x, 1 - y)]
    return x, y, c, 2 * x + y, [(px, py, 2 * px + py) for px, py in peers]


def split_start(name, bufs, n, plan):
    nb = len(bufs)

    def body(*refs):
        sends, _ = plan(refs[:nb], refs[nb], refs[nb + 1])
        for cp in sends:
            cp.start()
        refs[-1][...] = jnp.zeros_like(refs[-1])

    out = pl.pallas_call(
        body, name=name,
        out_shape=(pltpu.SemaphoreType.DMA((n,)), pltpu.SemaphoreType.DMA((n,)),
                   *[pltpu.HBM(b.shape, b.dtype) for b in bufs], SDS((8, LANE), F32)),
        in_specs=[HBM] * nb, out_specs=(SEM, SEM, *([HBM] * nb), pl.BlockSpec(memory_space=pltpu.VMEM)),
        input_output_aliases={i: 2 + i for i in range(nb)},
        compiler_params=pltpu.CompilerParams(has_side_effects=DATAFLOW),
    )(*[pltpu.with_memory_space_constraint(b, pltpu.HBM) for b in bufs])
    return out[0], out[1], list(out[2:2 + nb]), out[-1]


def split_wait(name, ssem, rsem, bufs, after, plan):
    nb = len(bufs)

    def body(*refs):
        sends, recvs = plan(refs[:nb], refs[nb], refs[nb + 1])
        for cp in recvs:
            cp.wait_recv()
        for cp in sends:
            cp.wait_send()

    out = pl.pallas_call(
        body, name=name, out_shape=tuple(pltpu.HBM(b.shape, b.dtype) for b in bufs),
        in_specs=[HBM] * nb + [SEM, SEM] + [ANY] * len(after), out_specs=tuple([HBM] * nb),
        input_output_aliases={i: i for i in range(nb)},
        compiler_params=pltpu.CompilerParams(has_side_effects=DATAFLOW),
    )(*bufs, ssem, rsem, *after)
    return list(out)


def gather_plan(refs, ssem, rsem):
    land_ref, = refs
    _, _, c, k, peers = _chip_peers()
    sends = [_rcopy(land_ref.at[k, c], land_ref.at[k, c], ssem.at[j], rsem.at[j], (px, py, c))
             for j, (px, py, _) in enumerate(peers)]
    recvs = [_rcopy(land_ref.at[kp, c], land_ref.at[kp, c], ssem.at[j], rsem.at[j], (px, py, c))
             for j, (px, py, kp) in enumerate(peers)]
    return sends, recvs


def reduce_plan(refs, ssem, rsem):
    cs_ref, ss_ref, p3_ref, sp3_ref = refs
    _, _, c, k, peers = _chip_peers()
    sends, recvs = [], []
    for j, (px, py, kp) in enumerate(peers):
        sends.append(_rcopy(cs_ref.at[kp], p3_ref.at[j], ssem.at[j], rsem.at[j], (px, py, c)))
        recvs.append(_rcopy(cs_ref.at[kp], p3_ref.at[j], ssem.at[j], rsem.at[j], (px, py, c)))
        sends.append(_rcopy(ss_ref, sp3_ref.at[k], ssem.at[3 + j], rsem.at[3 + j], (px, py, c)))
        recvs.append(_rcopy(ss_ref, sp3_ref.at[kp], ssem.at[3 + j], rsem.at[3 + j], (px, py, c)))
    return sends, recvs


def gather_small(sshard):
    def body(s_ref, sf_ref, lsem, ssem, rsem):
        _, _, c, k, peers = _chip_peers()
        own = pltpu.make_async_copy(s_ref, sf_ref.at[k], lsem)
        own.start()
        sends = [_rcopy(s_ref, sf_ref.at[k], ssem.at[j], rsem.at[j], (px, py, c)) for j, (px, py, _) in enumerate(peers)]
        recvs = [_rcopy(s_ref, sf_ref.at[kp], ssem.at[j], rsem.at[j], (px, py, c)) for j, (px, py, kp) in enumerate(peers)]
        for cp in sends:
            cp.start()
        for cp in recvs:
            cp.wait_recv()
        for cp in sends:
            cp.wait_send()
        own.wait()

    return pl.pallas_call(
        body, name="gather_small", out_shape=SDS((N_CHIPS,) + sshard.shape, sshard.dtype),
        in_specs=[ANY], out_specs=ANY,
        scratch_shapes=[pltpu.SemaphoreType.DMA, pltpu.SemaphoreType.DMA((3,)), pltpu.SemaphoreType.DMA((3,))],
    )(sshard)


def gather_pair(land):
    def body(land_in, land_ref, ssem, rsem):
        x, y, c, k, peers = _chip_peers()
        sib = (x, y, 1 - c)
        sends = [_rcopy(land_ref.at[kp, c], land_ref.at[kp, c], ssem.at[j], rsem.at[j], sib) for j, (_, _, kp) in enumerate(peers)]
        recvs = [_rcopy(land_ref.at[kp, 1 - c], land_ref.at[kp, 1 - c], ssem.at[j], rsem.at[j], sib)
                 for j, (_, _, kp) in enumerate(peers)]
        for cp in sends:
            cp.start()
        for cp in recvs:
            cp.wait_recv()
        for cp in sends:
            cp.wait_send()

    return pl.pallas_call(
        body, name="gather_pair", out_shape=SDS(land.shape, land.dtype),
        in_specs=[ANY], out_specs=ANY, input_output_aliases={0: 0},
        scratch_shapes=[pltpu.SemaphoreType.DMA((3,)), pltpu.SemaphoreType.DMA((3,))],
    )(land)


def exchange_pair(gb, sb):
    n, _, rh, d = gb.shape

    def body(gb_ref, sb_ref, p_ref, sp_ref, ssem, rsem):
        x, y, c = _mesh_pos()
        sib = (x, y, 1 - c)
        sends = [_rcopy(gb_ref.at[kk, 1 - c], p_ref.at[kk], ssem.at[kk], rsem.at[kk], sib) for kk in range(n)]
        sends.append(_rcopy(sb_ref, sp_ref, ssem.at[n], rsem.at[n], sib))
        for cp in sends:
            cp.start()
        for cp in sends:
            cp.wait_recv()
        for cp in sends:
            cp.wait_send()

    return pl.pallas_call(
        body, name="exchange_pair",
        out_shape=(SDS((n, rh, d), gb.dtype), SDS(sb.shape, sb.dtype)),
        in_specs=[ANY, ANY], out_specs=(ANY, ANY),
        scratch_shapes=[pltpu.SemaphoreType.DMA((n + 1,)), pltpu.SemaphoreType.DMA((n + 1,))],
    )(gb, sb)


def exchange_final(gf):
    _, rh, d = gf.shape
    nch = 4 if rh % 32 == 0 else 1
    cr = rh // nch

    def body(gf_ref, out_ref, ssem, rsem):
        x, y, c = _mesh_pos()
        sib = (x, y, 1 - c)
        sends = [_rcopy(out_ref.at[c, pl.ds(q * cr, cr)], out_ref.at[c, pl.ds(q * cr, cr)], ssem.at[q], rsem.at[q], sib)
                 for q in range(nch)]
        recvs = [_rcopy(out_ref.at[1 - c, pl.ds(q * cr, cr)], out_ref.at[1 - c, pl.ds(q * cr, cr)], ssem.at[q], rsem.at[q], sib)
                 for q in range(nch)]
        for cp in sends:
            cp.start()
        for cp in recvs:
            cp.wait_recv()
        for cp in sends:
            cp.wait_send()

    return pl.pallas_call(
        body, name="exchange_final",
        out_shape=SDS(gf.shape, gf.dtype),
        in_specs=[ANY], out_specs=ANY, input_output_aliases={0: 0},
        scratch_shapes=[pltpu.SemaphoreType.DMA((nch,)), pltpu.SemaphoreType.DMA((nch,))],
    )(gf)


def pair_sum(pos, gb, p1):
    n, _, rh, d = gb.shape
    cr = _row_chunk(rh, 1024)

    def body(pos_ref, a_ref, b_ref, o_ref):
        o_ref[...] = (a_ref[...].astype(F32) + b_ref[...].astype(F32)).astype(o_ref.dtype)

    return pl.pallas_call(
        body, name="pair_sum", out_shape=SDS((n, rh, d), gb.dtype),
        grid_spec=pltpu.PrefetchScalarGridSpec(
            num_scalar_prefetch=1, grid=(n, rh // cr),
            in_specs=[pl.BlockSpec((None, None, cr, d), lambda kk, r, pos: (kk, pos[1], r, 0)),
                      pl.BlockSpec((None, cr, d), lambda kk, r, pos: (kk, r, 0))],
            out_specs=pl.BlockSpec((None, cr, d), lambda kk, r, pos: (kk, r, 0))),
        compiler_params=_cparams(),
    )(pos, gb, p1)


def chip_sum(pos, cs, p3):
    n, rh, d = cs.shape
    cr = _row_chunk(rh, 512)

    def body(pos_ref, a_ref, b_ref, o_ref):
        acc = a_ref[...].astype(F32)
        for j in range(3):
            acc = acc + b_ref[j].astype(F32)
        o_ref[...] = acc

    return pl.pallas_call(
        body, name="chip_sum", out_shape=SDS((2, rh, d), F32),
        grid_spec=pltpu.PrefetchScalarGridSpec(
            num_scalar_prefetch=1, grid=(rh // cr,),
            in_specs=[pl.BlockSpec((None, cr, d), lambda r, pos: (pos[0], r, 0)),
                      pl.BlockSpec((3, cr, d), lambda r, pos: (0, r, 0))],
            out_specs=pl.BlockSpec((None, cr, d), lambda r, pos: (pos[1], r, 0))),
        compiler_params=_cparams(),
    )(pos, cs, p3)


def small_pair_sum(a, b):
    def body(a_ref, b_ref, o_ref):
        o_ref[...] = a_ref[...] + b_ref[...]

    return pl.pallas_call(body, name="small_pair_sum", out_shape=SDS(a.shape, a.dtype),
                          compiler_params=_cparams())(a, b)


def small_chip_sum(pos, own, p):
    ns, w = own.shape

    def body(pos_ref, own_ref, p0, p1, p2, p3, o_ref):
        k = pos_ref[0]
        acc = None
        for chip, ref in enumerate((p0, p1, p2, p3)):
            term = jnp.where(k == chip, own_ref[...], ref[...])
            acc = term if acc is None else acc + term
        o_ref[...] = acc

    def slot(chip):
        return pl.BlockSpec((None, ns, w), lambda i, pos: (jnp.where(pos[0] == chip, (chip + 1) % N_CHIPS, chip), 0, 0))

    return pl.pallas_call(
        body, name="small_chip_sum", out_shape=SDS(own.shape, own.dtype),
        grid_spec=pltpu.PrefetchScalarGridSpec(
            num_scalar_prefetch=1, grid=(1,),
            in_specs=[pl.BlockSpec((ns, w), lambda i, pos: (0, 0))] + [slot(chip) for chip in range(N_CHIPS)],
            out_specs=pl.BlockSpec((ns, w), lambda i, pos: (0, 0))),
        compiler_params=_cparams(),
    )(pos, own, p, p, p, p)


def _rms(x, g):
    rs = lax.rsqrt(jnp.mean(x * x, axis=-1, keepdims=True) + EPS)
    xh = x * rs
    return xh, rs, xh * g


def _rms_bwd(dy, xh, rs, g):
    dxh = dy * g
    dx = rs * (dxh - xh * jnp.mean(dxh * xh, axis=-1, keepdims=True))
    return dx, dy * xh


def _gelu(x):
    t = jnp.tanh(GELU_C * (x + 0.044715 * x * x * x))
    return 0.5 * x * (1.0 + t), t


def _gelu_grad(x, t):
    return 0.5 * (1.0 + t) + 0.5 * x * (1.0 - t * t) * GELU_C * (1.0 + 3.0 * 0.044715 * x * x)


def _shift_rows(v, s, n):
    if s == 0:
        return v
    t = lax.broadcasted_iota(jnp.int32, v.shape, 0)
    rolled = pltpu.roll(v, (-s) % n, 0)
    return jnp.where((t + s >= 0) & (t + s < n), rolled, 0.0)


def _scan_rows(a_ref, u_ref, h_ref, acum_ref, reverse):
    s_len, w = a_ref.shape
    chunk = min(512, s_len)
    last = 0 if reverse else SUBLANES - 1

    def inside_vregs(ci, _):
        rows = pl.ds(pl.multiple_of(ci * chunk, chunk), chunk)
        a = a_ref[rows, :]
        u = u_ref[rows, :]
        pos = lax.broadcasted_iota(jnp.int32, (chunk, w), 0) % SUBLANES
        for dist in (1, 2, 4):
            ok = (pos < SUBLANES - dist) if reverse else (pos >= dist)
            shift = chunk - dist if reverse else dist
            u = u + a * jnp.where(ok, pltpu.roll(u, shift, 0), 0.0)
            a = a * jnp.where(ok, pltpu.roll(a, shift, 0), 1.0)
        h_ref[rows, :] = u
        acum_ref[rows, :] = a
        return 0

    lax.fori_loop(0, s_len // chunk, inside_vregs, 0)

    chains = max(1, min(SCAN_CHAINS, s_len // (8 * SUBLANES)))
    seg = s_len // chains
    nvreg = seg // SUBLANES

    def step(j, carry):
        jj = (nvreg - 1 - j) if reverse else j
        out = []
        for c, (hin, ain) in enumerate(carry):
            rows = pl.ds(pl.multiple_of(c * seg + jj * SUBLANES, SUBLANES), SUBLANES)
            acc = acum_ref[rows, :]
            h = h_ref[rows, :] + acc * hin
            acc = acc * ain
            h_ref[rows, :] = h
            acum_ref[rows, :] = acc
            out.append((jnp.broadcast_to(h[last:last + 1, :], h.shape), jnp.broadcast_to(acc[last:last + 1, :], acc.shape)))
        return tuple(out)

    init = tuple((jnp.zeros((SUBLANES, w), F32), jnp.ones((SUBLANES, w), F32)) for _ in range(chains))
    ends = lax.fori_loop(0, nvreg, step, init, unroll=min(2, nvreg))
    order = range(chains - 2, -1, -1) if reverse else range(1, chains)
    inflow = jnp.zeros((1, w), F32)
    for s in order:
        h, acc = ends[s + 1 if reverse else s - 1]
        inflow = h[0:1, :] + acc[0:1, :] * inflow
        rows = pl.ds(s * seg, seg)
        h_ref[rows, :] = h_ref[rows, :] + acum_ref[rows, :] * inflow


def _w_spec(rows_half, d, blk):
    return pl.BlockSpec((N_CHIPS, 2, rows_half, d), lambda *_: (0, 0, blk, 0), pipeline_mode=pl.Buffered(1))


def ffn_forward(x, gain, wfull, lay, which, deps=(), tm=512):
    s_len, d = x.shape
    tm = min(tm, s_len)
    f = 8 * lay.fh
    fc = f // 2

    def body(x_ref, g_ref, wg_ref, wu_ref, wd_ref, *rest):
        o_ref, gate_ref, up_ref = rest[len(deps):]
        x = x_ref[...]
        _, _, hn = _rms(x, g_ref[...])
        h = hn.astype(BF)
        y = jnp.zeros((tm, d), F32)
        for part in range(2):
            cols = slice(part * fc, (part + 1) * fc)
            gate = dot_nt(h, wg_ref[...].reshape(f, d)[cols])
            up = dot_nt(h, wu_ref[...].reshape(f, d)[cols])
            act = (gate * jax.nn.sigmoid(gate) * up).astype(BF)
            y = y + dot_nn(act, wd_ref[...].reshape(f, d)[cols])
            gate_ref[:, cols] = gate.astype(BF)
            up_ref[:, cols] = up.astype(BF)
        o_ref[...] = x + FFN_RES * y

    row = pl.BlockSpec((tm, d), lambda i: (i, 0))
    wide = pl.BlockSpec((tm, f), lambda i: (i, 0))
    return pl.pallas_call(
        body, name="ffn_forward", grid=(s_len // tm,),
        out_shape=(SDS((s_len, d), F32), SDS((s_len, f), BF), SDS((s_len, f), BF)),
        in_specs=[row, pl.BlockSpec((1, d), lambda i: (0, 0))]
        + [_w_spec(lay.fh, d, 3 * which + m) for m in range(3)] + [ANY] * len(deps),
        out_specs=(row, wide, wide), compiler_params=_cparams(),
    )(x, gain, wfull, wfull, wfull, *deps)


def ffn_backward_dx(x, gain, dout, gate_bf, up_bf, wfull, lay, which, deps=(), tm=256):
    s_len, d = x.shape
    tm = min(tm, s_len)
    f = 8 * lay.fh
    fc = f // 2
    nt = s_len // tm

    def body(x_ref, g_ref, do_ref, gate_ref, up_ref, wg_ref, wu_ref, wd_ref, *rest):
        dx_ref, dg_ref, dgate_ref, dup_ref, act_ref, h_ref, df_ref = rest[len(deps):]
        x = x_ref[...]
        g = g_ref[...]
        xh, rs, hn = _rms(x, g)
        h = hn.astype(BF)
        do = do_ref[...]
        df = (FFN_RES * do).astype(BF)
        dh = jnp.zeros((tm, d), F32)
        for part in range(2):
            cols = slice(part * fc, (part + 1) * fc)
            wg = wg_ref[...].reshape(f, d)[cols]
            wu = wu_ref[...].reshape(f, d)[cols]
            gate = gate_ref[:, cols].astype(F32)
            up = up_ref[:, cols].astype(F32)
            sg = jax.nn.sigmoid(gate)
            silu = gate * sg
            dact = dot_nt(df, wd_ref[...].reshape(f, d)[cols])
            dup = (dact * silu).astype(BF)
            dgate = (dact * up * (sg * (1.0 + gate * (1.0 - sg)))).astype(BF)
            dh = dh + dot_nn(dgate, wg) + dot_nn(dup, wu)
            dgate_ref[:, cols] = dgate
            dup_ref[:, cols] = dup
            act_ref[:, cols] = (silu * up).astype(BF)
        dxn, dgrow = _rms_bwd(dh, xh, rs, g)
        dx_ref[...] = do + dxn

        @pl.when(pl.program_id(0) == 0)
        def _():
            dg_ref[...] = jnp.zeros_like(dg_ref)

        dg_ref[...] += jnp.sum(dgrow, axis=0, keepdims=True)
        h_ref[...] = h
        df_ref[...] = df

    row = pl.BlockSpec((tm, d), lambda i: (i, 0))
    wide = pl.BlockSpec((tm, f), lambda i: (i, 0))
    vec = pl.BlockSpec((1, d), lambda i: (0, 0))
    return pl.pallas_call(
        body, name="ffn_backward_dx", grid=(nt,),
        out_shape=(SDS((s_len, d), F32), SDS((1, d), F32), SDS((s_len, f), BF), SDS((s_len, f), BF),
                   SDS((s_len, f), BF), SDS((s_len, d), BF), SDS((s_len, d), BF)),
        in_specs=[row, vec, row, wide, wide] + [_w_spec(lay.fh, d, 3 * which + m) for m in range(3)] + [ANY] * len(deps),
        out_specs=(row, vec, wide, wide, wide, row, row), compiler_params=_cparams(),
    )(x, gain, dout, gate_bf, up_bf, wfull, wfull, wfull, *deps)


def weight_grad_tn(a, b, gb, lay, blk, tk=512):
    s_len, f = a.shape
    tk = min(tk, s_len)
    d = b.shape[1]
    fc = f // 2
    nk = s_len // tk

    def body(a_ref, b_ref, gb_ref, o_ref, acc):
        kt = pl.program_id(1)

        @pl.when(kt == 0)
        def _():
            acc[...] = jnp.zeros_like(acc)

        acc[...] += dot_tn(a_ref[...], b_ref[...])

        @pl.when(kt == nk - 1)
        def _():
            for p in range(2):
                for q in range(2):
                    o_ref[p, q] = acc[pl.ds((2 * p + q) * lay.fh, lay.fh), :].astype(o_ref.dtype)

    return pl.pallas_call(
        body, name="weight_grad_tn", grid=(2, nk), out_shape=SDS(gb.shape, gb.dtype),
        in_specs=[pl.BlockSpec((tk, fc), lambda j, kt: (kt, j)), pl.BlockSpec((tk, d), lambda j, kt: (kt, 0)), ANY],
        out_specs=pl.BlockSpec((2, 2, lay.fh, d), lambda j, kt: (j, 0, blk, 0)),
        scratch_shapes=[pltpu.VMEM((fc, d), F32)],
        input_output_aliases={2: 0}, compiler_params=_cparams(),
    )(a, b, gb)


def _lane_blocks(v):
    return [v[:, j * LANE:(j + 1) * LANE] for j in range(v.shape[1] // LANE)]


def _join_lane_blocks(ref):
    return jnp.concatenate([ref[j] for j in range(ref.shape[0])], axis=1)


def _cbm_spec(nblk, rows, first=0):
    return pl.BlockSpec((nblk, rows, LANE), lambda i: (first // nblk, i, 0))


def mix_project(x, gain, wfull, lay, lw, att, tm=512):
    s_len, d = x.shape
    tm = min(tm, s_len)
    d_in = 8 * lay.ih
    kvw = (d_in - 2 * lw - att) // 2
    ncol = (2 * lw + 2 * kvw) // LANE

    def body(x_ref, g_ref, w_ref, o_ref, qt_ref, vt_ref):
        _, _, hn = _rms(x_ref[...], g_ref[...])
        h = hn.astype(BF)
        w = w_ref[:, :, :lay.ih, :].reshape(d_in, d)
        pieces = _lane_blocks(dot_nt(h, w[:2 * lw])) + _lane_blocks(dot_nt(h, w[2 * lw + att:]))
        for j, piece in enumerate(pieces):
            o_ref[j] = piece
        qt_ref[...] = dot_nt(w[2 * lw:2 * lw + att], h)
        vt_ref[...] = dot_nt(w[2 * lw + att + kvw:], h)

    return pl.pallas_call(
        body, name="mix_project", grid=(s_len // tm,),
        out_shape=(SDS((ncol, s_len, LANE), F32), SDS((att, s_len), F32), SDS((kvw, s_len), F32)),
        in_specs=[pl.BlockSpec((tm, d), lambda i: (i, 0)), pl.BlockSpec((1, d), lambda i: (0, 0)),
                  _w_spec(lay.fh, d, lay.MIX_BLK)],
        out_specs=(_cbm_spec(ncol, tm), pl.BlockSpec((att, tm), lambda i: (0, i)), pl.BlockSpec((kvw, tm), lambda i: (0, i))),
        compiler_params=_cparams(),
    )(x, gain, wfull)


def mix_project_backward(x, gain, dout, dxr, dgt, dqt, dkv, dwout, wfull, gb, lay, tm=512):
    s_len, d = x.shape
    tm = min(tm, s_len)
    d_in = 8 * lay.ih
    nt = s_len // tm
    kvw = dkv.shape[1]
    att = dqt.shape[0]
    nlru = (dxr.shape[0] + dgt.shape[0]) * LANE

    def body(x_ref, g_ref, do_ref, dxr_ref, dgt_ref, dqt_ref, dkv_ref, dwo_ref, w_ref, gb_ref, dx_ref, dg_ref, o_ref, acc):
        i = pl.program_id(0)
        g = g_ref[...]
        xh, rs, hn = _rms(x_ref[...], g)
        h = hn.astype(BF)
        w = w_ref[:, :, :lay.ih, :].reshape(d_in, d)
        dlru = jnp.concatenate([_join_lane_blocks(dxr_ref), _join_lane_blocks(dgt_ref)], axis=1).astype(BF)
        dqt = dqt_ref[...].astype(BF)
        dkv = dkv_ref[...].astype(BF)
        dh = dot_nn(dlru, w[:nlru]) + dot_tn(dqt, w[nlru:nlru + att]) + dot_nn(dkv, w[nlru + att:])
        dxn, dgrow = _rms_bwd(dh, xh, rs, g)
        dx_ref[...] = do_ref[...] + dxn

        @pl.when(i == 0)
        def _():
            dg_ref[...] = jnp.zeros_like(dg_ref)
            acc[...] = jnp.zeros_like(acc)

        dg_ref[...] += jnp.sum(dgrow, axis=0, keepdims=True)
        acc[0:nlru, :] += dot_tn(dlru, h)
        acc[nlru:nlru + att, :] += dot_nn(dqt, h)
        acc[nlru + att:, :] += dot_tn(dkv, h)

        @pl.when(i == nt - 1)
        def _():
            for p in range(N_CHIPS):
                for q in range(2):
                    o_ref[p, q, :lay.ih, :] = acc[pl.ds((2 * p + q) * lay.ih, lay.ih), :].astype(o_ref.dtype)
            o_ref[:, :, lay.ih:, :] = dwo_ref[...]

    row = pl.BlockSpec((tm, d), lambda i: (i, 0))
    vec = pl.BlockSpec((1, d), lambda i: (0, 0))
    return pl.pallas_call(
        body, name="mix_project_backward", grid=(nt,),
        out_shape=(SDS((s_len, d), F32), SDS((1, d), F32), SDS(gb.shape, gb.dtype)),
        in_specs=[row, vec, row, _cbm_spec(dxr.shape[0], tm), _cbm_spec(dgt.shape[0], tm),
                  pl.BlockSpec((att, tm), lambda i: (0, i)), pl.BlockSpec((tm, kvw), lambda i: (i, 0)),
                  pl.BlockSpec(dwout.shape, lambda i: (0, 0, 0, 0)), _w_spec(lay.fh, d, lay.MIX_BLK), ANY],
        out_specs=(row, vec, pl.BlockSpec((N_CHIPS, 2, lay.fh, d), lambda i: (0, 0, lay.MIX_BLK, 0))),
        scratch_shapes=[pltpu.VMEM((d_in, d), F32)],
        input_output_aliases={9: 2}, compiler_params=_cparams(),
    )(x, gain, dout, dxr, dgt, dqt, dkv, dwout, wfull, gb)


def _lru_gates(xc, wb_ref, pv_ref, direction):
    xcb = xc.astype(BF)
    r = jax.nn.sigmoid(dot_nn(xcb, wb_ref[2 * direction]) + pv_ref[1 + direction:2 + direction, :])
    i = jax.nn.sigmoid(dot_nn(xcb, wb_ref[2 * direction + 1]) + pv_ref[3 + direction:4 + direction, :])
    lam = pv_ref[5 + direction:6 + direction, :]
    sp = jnp.maximum(-lam, 0.0) + jnp.log(1.0 + jnp.exp(-jnp.abs(lam)))
    a = jnp.exp(-LRU_C * sp * r)
    mult = jnp.sqrt(1.0 - a * a)
    return xcb, r, i, a, mult, sp


def _conv_rows(xr, cv_ref, bias, n):
    acc = bias + cv_ref[0:1, :] * _shift_rows(xr, -2, n)
    for j in range(1, CONV_WIDTH):
        acc = acc + cv_ref[j:j + 1, :] * _shift_rows(xr, j - 2, n)
    return acc


def lru_forward(proj, cvec, pvec, wblk, lw, deps=(), ch=512):
    s_len = proj.shape[1]
    ncb = lw // LANE
    ch = min(ch, s_len)
    nchunk = s_len // ch

    def body(xr_ref, gt_ref, cv_ref, pv_ref, wb_ref, *rest):
        y_ref, hs_ref, xc_s, a_s, u_s, acum_s = rest[len(deps):]
        xc_s[...] = _conv_rows(xr_ref[...], cv_ref, pv_ref[0:1, :], s_len)
        for direction in range(2):
            def fill(ci, _):
                rows = pl.ds(pl.multiple_of(ci * ch, ch), ch)
                xc = xc_s[rows, :]
                _, _, i, a, mult, _ = _lru_gates(xc, wb_ref, pv_ref, direction)
                a_s[rows, :] = a
                u_s[rows, :] = mult * (i * xc)
                return 0

            lax.fori_loop(0, nchunk, fill, 0)
            _scan_rows(a_s, u_s, hs_ref.at[direction], acum_s, reverse=direction == 1)

        def out(ci, _):
            rows = pl.ds(pl.multiple_of(ci * ch, ch), ch)
            gl, _ = _gelu(gt_ref[rows, :])
            y_ref[rows, :] = gl * (hs_ref[0, rows, :] + hs_ref[1, rows, :])
            return 0

        lax.fori_loop(0, nchunk, out, 0)

    col = lambda off: pl.BlockSpec((None, s_len, LANE), lambda cb: (off + cb, 0, 0))
    return pl.pallas_call(
        body, name="lru_forward", grid=(ncb,),
        out_shape=(SDS((ncb, s_len, LANE), F32), SDS((2, ncb, s_len, LANE), F32)),
        in_specs=[col(0), col(ncb), pl.BlockSpec((8, LANE), lambda cb: (0, cb)), pl.BlockSpec((8, LANE), lambda cb: (0, cb)),
                  pl.BlockSpec((4, None, LANE, LANE), lambda cb: (0, cb, 0, 0))] + [ANY] * len(deps),
        out_specs=(col(0), pl.BlockSpec((2, None, s_len, LANE), lambda cb: (0, cb, 0, 0))),
        scratch_shapes=[pltpu.VMEM((s_len, LANE), F32)] * 4, compiler_params=_cparams(),
    )(proj, proj, cvec, pvec, wblk, *deps)


def lru_backward(proj, hs, dy, cvec, pvec, wblk, lw, ch=512):
    s_len = proj.shape[1]
    ncb = lw // LANE
    ch = min(ch, s_len)
    nchunk = s_len // ch

    def body(xr_ref, gt_ref, hs_ref, dy_ref, cv_ref, pv_ref, wb_ref, dxr_ref, dgt_ref, dcv_ref, dpv_ref, dwb_ref,
             xc_s, a_s, dh_s, lam_s, hp_s, dxc_s, acum_s):
        xr = xr_ref[...]
        xc_s[...] = _conv_rows(xr, cv_ref, pv_ref[0:1, :], s_len)
        dxc_s[...] = jnp.zeros_like(dxc_s)
        dpv_ref[...] = jnp.zeros_like(dpv_ref)
        dwb_ref[...] = jnp.zeros_like(dwb_ref)

        def head(ci, _):
            rows = pl.ds(pl.multiple_of(ci * ch, ch), ch)
            gt = gt_ref[rows, :]
            gl, t = _gelu(gt)
            dy = dy_ref[rows, :]
            dh_s[rows, :] = dy * gl
            dgt_ref[rows, :] = dy * (hs_ref[0, rows, :] + hs_ref[1, rows, :]) * _gelu_grad(gt, t)
            return 0

        lax.fori_loop(0, nchunk, head, 0)

        for direction in range(2):
            def fill(ci, _):
                rows = pl.ds(pl.multiple_of(ci * ch, ch), ch)
                _, _, _, a, _, _ = _lru_gates(xc_s[rows, :], wb_ref, pv_ref, direction)
                a_s[rows, :] = a
                return 0

            lax.fori_loop(0, nchunk, fill, 0)
            toward = 1 if direction == 0 else -1
            hp_s[...] = _shift_rows(a_s[...], toward, s_len)
            _scan_rows(hp_s, dh_s, lam_s, acum_s, reverse=direction == 0)
            hp_s[...] = _shift_rows(hs_ref[direction], -toward, s_len)

            def grads(ci, _):
                rows = pl.ds(pl.multiple_of(ci * ch, ch), ch)
                xc = xc_s[rows, :]
                xcb, r, i, a, mult, sp = _lru_gates(xc, wb_ref, pv_ref, direction)
                du = lam_s[rows, :]
                da = du * hp_s[rows, :]
                dmult = du * i * xc
                di = du * mult * xc
                dlog_a = (da - dmult * a / mult) * a
                dr = dlog_a * (-LRU_C * sp)
                dza = dr * r * (1.0 - r)
                dzx = di * i * (1.0 - i)
                dzab = dza.astype(BF)
                dzxb = dzx.astype(BF)
                dxc_s[rows, :] += (du * mult * i + dot_nt(dzab, wb_ref[2 * direction])
                                   + dot_nt(dzxb, wb_ref[2 * direction + 1]))
                dwb_ref[2 * direction] += dot_tn(xcb, dzab)
                dwb_ref[2 * direction + 1] += dot_tn(xcb, dzxb)
                dpv_ref[1 + direction:2 + direction, :] += jnp.sum(dza, axis=0, keepdims=True)
                dpv_ref[3 + direction:4 + direction, :] += jnp.sum(dzx, axis=0, keepdims=True)
                dpv_ref[5 + direction:6 + direction, :] += jnp.sum(dlog_a * (-LRU_C * r), axis=0, keepdims=True)
                return 0

            lax.fori_loop(0, nchunk, grads, 0)

        for direction in range(2):
            lam = pv_ref[5 + direction:6 + direction, :]
            dpv_ref[5 + direction:6 + direction, :] = dpv_ref[5 + direction:6 + direction, :] * (-jax.nn.sigmoid(-lam))
        dxc = dxc_s[...]
        dpv_ref[0:1, :] = jnp.sum(dxc, axis=0, keepdims=True)
        dxr = cv_ref[0:1, :] * _shift_rows(dxc, 2, s_len)
        for j in range(1, CONV_WIDTH):
            dxr = dxr + cv_ref[j:j + 1, :] * _shift_rows(dxc, 2 - j, s_len)
        dxr_ref[...] = dxr
        dcv_ref[...] = jnp.zeros_like(dcv_ref)
        for j in range(CONV_WIDTH):
            dcv_ref[j:j + 1, :] = jnp.sum(dxc * _shift_rows(xr, j - 2, s_len), axis=0, keepdims=True)

    col = lambda off: pl.BlockSpec((None, s_len, LANE), lambda cb: (off + cb, 0, 0))
    own = col(0)
    small = pl.BlockSpec((8, LANE), lambda cb: (0, cb))
    wspec = pl.BlockSpec((4, None, LANE, LANE), lambda cb: (0, cb, 0, 0))
    return pl.pallas_call(
        body, name="lru_backward", grid=(ncb,),
        out_shape=(SDS((ncb, s_len, LANE), F32), SDS((ncb, s_len, LANE), F32), SDS((8, lw), F32), SDS((8, lw), F32),
                   SDS(wblk.shape, F32)),
        in_specs=[col(0), col(ncb), pl.BlockSpec((2, None, s_len, LANE), lambda cb: (0, cb, 0, 0)), own, small, small, wspec],
        out_specs=(own, own, small, small, wspec),
        scratch_shapes=[pltpu.VMEM((s_len, LANE), F32)] * 7, compiler_params=_cparams(),
    )(proj, proj, hs, dy, cvec, pvec, wblk)


def _window_specs(s_len, first, width=None):
    nb = s_len // BLOCK
    where = (lambda n: jnp.maximum(n - 1, 0), lambda n: n, lambda n: jnp.minimum(n + 1, nb - 1))
    if width is None:
        return [pl.BlockSpec((None, BLOCK, LANE), lambda n, f=f: (first, f(n), 0)) for f in where]
    return [pl.BlockSpec((width, BLOCK), lambda n, f=f: (0, f(n))) for f in where]


def _stack_heads(v, kh):
    return jnp.concatenate([v[(kh * KV_GROUP + g) * HEAD_DIM:(kh * KV_GROUP + g + 1) * HEAD_DIM, :]
                            for g in range(KV_GROUP)], axis=1)


def _unstack_heads(ref, kh, v):
    for g in range(KV_GROUP):
        h = kh * KV_GROUP + g
        ref[h * HEAD_DIM:(h + 1) * HEAD_DIM, :] = v[:, g * BLOCK:(g + 1) * BLOCK]


def _key_exists(n, nb):
    j = lax.broadcasted_iota(jnp.int32, (3 * BLOCK, 1), 0)
    return ((n > 0) | (j >= BLOCK)) & ((n < nb - 1) | (j < 2 * BLOCK))


def _attn_probs(qs, kcat, bias_g, sink_g, key_ok):
    logits = jnp.where(key_ok, dot_nn(kcat, qs) + bias_g, NEG_INF)
    m = jnp.maximum(jnp.max(logits, axis=0, keepdims=True), sink_g)
    p = jnp.exp(logits - m)
    es = jnp.exp(sink_g - m)
    inv = 1.0 / (jnp.sum(p, axis=0, keepdims=True) + es)
    return p * inv, es * inv


def attention_forward(qt, proj, vt, bias, sink, kblk):
    att, s_len = qt.shape
    kvw = vt.shape[0]
    nb = s_len // BLOCK

    def body(q_ref, kp_ref, kc_ref, kn_ref, vp_ref, vc_ref, vn_ref, b_ref, s_ref, o_ref):
        n = pl.program_id(0)
        q = q_ref[...]
        key_ok = _key_exists(n, nb)
        kall = jnp.concatenate([kp_ref[...], kc_ref[...], kn_ref[...]], axis=0).astype(BF)
        vall = jnp.concatenate([vp_ref[...], vc_ref[...], vn_ref[...]], axis=1).astype(BF)
        for kh in range(N_KV_HEADS):
            qs = (_stack_heads(q, kh) * (HEAD_DIM ** -0.5)).astype(BF)
            p, _ = _attn_probs(qs, kall[:, kh * HEAD_DIM:(kh + 1) * HEAD_DIM], b_ref[kh], s_ref[kh, 0:1, :], key_ok)
            _unstack_heads(o_ref, kh, dot_nn(vall[kh * HEAD_DIM:(kh + 1) * HEAD_DIM, :], p.astype(BF)))

    blk = pl.BlockSpec((att, BLOCK), lambda n: (0, n))
    return pl.pallas_call(
        body, name="attention_forward", grid=(nb,), out_shape=SDS((att, s_len), F32),
        in_specs=[blk] + _window_specs(s_len, kblk) + _window_specs(s_len, 0, kvw)
        + [pl.BlockSpec(bias.shape, lambda n: (0, 0, 0)), pl.BlockSpec(sink.shape, lambda n: (0, 0, 0))],
        out_specs=blk, compiler_params=_cparams(),
    )(qt, proj, proj, proj, vt, vt, vt, bias, sink)


def attention_backward(qt, proj, y_att, dy, bias, sink, kblk):
    att, s_len = qt.shape
    nb = s_len // BLOCK
    kvw = N_KV_HEADS * HEAD_DIM

    def body(q_ref, kp_ref, kc_ref, kn_ref, vp_ref, vc_ref, vn_ref, o_ref, do_ref, b_ref, s_ref,
             dq_ref, dkv_ref, db_ref, ds_ref):
        n = pl.program_id(0)

        @pl.when(n == 0)
        def _():
            dkv_ref[...] = jnp.zeros_like(dkv_ref)
            db_ref[...] = jnp.zeros_like(db_ref)
            ds_ref[...] = jnp.zeros_like(ds_ref)

        q = q_ref[...]
        o = o_ref[...]
        do = do_ref[...]
        kall = jnp.concatenate([kp_ref[...], kc_ref[...], kn_ref[...]], axis=0).astype(BF)
        vall = jnp.concatenate([vp_ref[...], vc_ref[...], vn_ref[...]], axis=0).astype(BF)
        key_ok = _key_exists(n, nb)
        dks, dvs = [], []
        for kh in range(N_KV_HEADS):
            kcat = kall[:, kh * HEAD_DIM:(kh + 1) * HEAD_DIM]
            vcat = vall[:, kh * HEAD_DIM:(kh + 1) * HEAD_DIM]
            qs = (_stack_heads(q, kh) * (HEAD_DIM ** -0.5)).astype(BF)
            p, ps = _attn_probs(qs, kcat, b_ref[kh], s_ref[kh, 0:1, :], key_ok)
            dos = _stack_heads(do, kh)
            dosb = dos.astype(BF)
            delta = jnp.sum(dos * _stack_heads(o, kh), axis=0, keepdims=True)
            dlog = p * (dot_nn(vcat, dosb) - delta)
            dlogb = dlog.astype(BF)
            db_ref[kh] += dlog
            ds_ref[kh] += jnp.broadcast_to(-ps * delta, ds_ref.shape[1:])
            _unstack_heads(dq_ref, kh, dot_tn(kcat, dlogb) * (HEAD_DIM ** -0.5))
            dks.append(dot_nt(dlogb, qs))
            dvs.append(dot_nt(p.astype(BF), dosb))
        dkv = jnp.concatenate(dks + dvs, axis=1)
        starts = [jnp.maximum(n - 1, 0), n, jnp.minimum(n + 1, nb - 1)]
        for b, st in enumerate(starts):
            rows = pl.ds(pl.multiple_of(st * BLOCK, BLOCK), BLOCK)
            dkv_ref[rows, :] += dkv[b * BLOCK:(b + 1) * BLOCK, :]

    blk = pl.BlockSpec((att, BLOCK), lambda n: (0, n))
    whole = lambda a: pl.BlockSpec(a.shape, lambda n: (0, 0, 0))
    return pl.pallas_call(
        body, name="attention_backward", grid=(nb,),
        out_shape=(SDS((att, s_len), F32), SDS((s_len, 2 * kvw), F32), SDS(bias.shape, F32), SDS(sink.shape, F32)),
        in_specs=[blk] + _window_specs(s_len, kblk) + _window_specs(s_len, kblk + 1) + [blk, blk, whole(bias), whole(sink)],
        out_specs=(blk, pl.BlockSpec((s_len, 2 * kvw), lambda n: (0, 0)), whole(bias), whole(sink)),
        compiler_params=_cparams(),
    )(qt, proj, proj, proj, proj, proj, proj, y_att, dy, bias, sink)


def _rms_cols(x, g):
    rs = lax.rsqrt(jnp.mean(x * x, axis=0, keepdims=True) + EPS)
    xh = x * rs
    return xh, rs, xh * g


def _rms_cols_bwd(dy, xh, rs, g):
    dxh = dy * g
    dx = rs * (dxh - xh * jnp.mean(dxh * xh, axis=0, keepdims=True))
    return dx, dy * xh


def mix_output(x, y_rec, y_att, g_rec, g_att, wfull, lay, tm=512):
    s_len, d = x.shape
    tm = min(tm, s_len)
    lw = y_rec.shape[0] * LANE
    att = y_att.shape[0]

    def body(x_ref, yr_ref, ya_ref, gr_ref, ga_ref, w_ref, o_ref):
        _, _, nr = _rms(_join_lane_blocks(yr_ref), gr_ref[...])
        _, _, na = _rms_cols(ya_ref[...], ga_ref[...])
        w = w_ref[:, :, lay.ih:, :].reshape(d, d)
        o_ref[...] = x_ref[...] + dot_nn(nr.astype(BF), w[:lw]) + dot_tn(na.astype(BF), w[lw:])

    row = pl.BlockSpec((tm, d), lambda i: (i, 0))
    return pl.pallas_call(
        body, name="mix_output", grid=(s_len // tm,), out_shape=SDS((s_len, d), F32),
        in_specs=[row, _cbm_spec(lw // LANE, tm), pl.BlockSpec((att, tm), lambda i: (0, i)),
                  pl.BlockSpec((1, lw), lambda i: (0, 0)), pl.BlockSpec((att, 1), lambda i: (0, 0)),
                  _w_spec(lay.fh, d, lay.MIX_BLK)],
        out_specs=row, compiler_params=_cparams(),
    )(x, y_rec, y_att, g_rec, g_att, wfull)


def mix_output_backward(dout, y_rec, y_att, g_rec, g_att, wfull, lay, tm=512):
    s_len, d = dout.shape
    tm = min(tm, s_len)
    lw = y_rec.shape[0] * LANE
    att = y_att.shape[0]
    nt = s_len // tm

    def body(do_ref, yr_ref, ya_ref, gr_ref, ga_ref, w_ref, dyr_ref, dya_ref, dgr_ref, dga_ref, o_ref, acc):
        i = pl.program_id(0)
        gr = gr_ref[...]
        ga = ga_ref[...]
        xhr, rsr, nr = _rms(_join_lane_blocks(yr_ref), gr)
        xha, rsa, na = _rms_cols(ya_ref[...], ga)
        dob = do_ref[...].astype(BF)
        w = w_ref[:, :, lay.ih:, :].reshape(d, d)
        dyr, dgr_row = _rms_bwd(dot_nt(dob, w[:lw]), xhr, rsr, gr)
        dya, dga_col = _rms_cols_bwd(dot_nt(w[lw:], dob), xha, rsa, ga)
        for j, piece in enumerate(_lane_blocks(dyr)):
            dyr_ref[j] = piece
        dya_ref[...] = dya

        @pl.when(i == 0)
        def _():
            dgr_ref[...] = jnp.zeros_like(dgr_ref)
            dga_ref[...] = jnp.zeros_like(dga_ref)
            acc[...] = jnp.zeros_like(acc)

        dgr_ref[...] += jnp.sum(dgr_row, axis=0, keepdims=True)
        dga_ref[...] += jnp.sum(dga_col, axis=1, keepdims=True)
        acc[0:lw, :] += dot_tn(nr.astype(BF), dob)
        acc[lw:, :] += dot_nn(na.astype(BF), dob)

        @pl.when(i == nt - 1)
        def _():
            for p in range(N_CHIPS):
                for q in range(2):
                    o_ref[p, q] = acc[pl.ds((2 * p + q) * lay.oh, lay.oh), :].astype(o_ref.dtype)

    row = pl.BlockSpec((tm, d), lambda i: (i, 0))
    return pl.pallas_call(
        body, name="mix_output_backward", grid=(nt,),
        out_shape=(SDS(y_rec.shape, F32), SDS(y_att.shape, F32), SDS((1, lw), F32), SDS((att, 1), F32),
                   SDS((N_CHIPS, 2, lay.oh, d), BF)),
        in_specs=[row, _cbm_spec(lw // LANE, tm), pl.BlockSpec((att, tm), lambda i: (0, i)),
                  pl.BlockSpec((1, lw), lambda i: (0, 0)), pl.BlockSpec((att, 1), lambda i: (0, 0)),
                  _w_spec(lay.fh, d, lay.MIX_BLK)],
        out_specs=(_cbm_spec(lw // LANE, tm), pl.BlockSpec((att, tm), lambda i: (0, i)),
                   pl.BlockSpec((1, lw), lambda i: (0, 0)), pl.BlockSpec((att, 1), lambda i: (0, 0)),
                   pl.BlockSpec((N_CHIPS, 2, lay.oh, d), lambda i: (0, 0, 0, 0))),
        scratch_shapes=[pltpu.VMEM((d, d), F32)], compiler_params=_cparams(),
    )(dout, y_rec, y_att, g_rec, g_att, wfull)


def loss_head(x, gain, target, tm=512):
    s_len, d = x.shape
    tm = min(tm, s_len)

    def body(x_ref, g_ref, t_ref, dx_ref, dg_ref, loss_ref):
        g = g_ref[...]
        xh, rs, y = _rms(x_ref[...], g)
        err = y - t_ref[...]

        @pl.when(pl.program_id(0) == 0)
        def _():
            dg_ref[...] = jnp.zeros_like(dg_ref)
            loss_ref[...] = jnp.zeros_like(loss_ref)

        part = 0.5 * jnp.sum(jnp.mean(err * err, axis=-1, keepdims=True), axis=0, keepdims=True)
        loss_ref[...] += jnp.broadcast_to(part, loss_ref.shape)
        dx, dgrow = _rms_bwd(err * (1.0 / d), xh, rs, g)
        dx_ref[...] = dx
        dg_ref[...] += jnp.sum(dgrow, axis=0, keepdims=True)

    row = pl.BlockSpec((tm, d), lambda i: (i, 0))
    vec = pl.BlockSpec((1, d), lambda i: (0, 0))
    return pl.pallas_call(
        body, name="loss_head", grid=(s_len // tm,),
        out_shape=(SDS((s_len, d), F32), SDS((1, d), F32), SDS((8, LANE), F32)),
        in_specs=[row, vec, row], out_specs=(row, vec, pl.BlockSpec((8, LANE), lambda i: (0, 0))),
        compiler_params=_cparams(),
    )(x, gain, target)


def _adamw_update(w, g, m, v):
    m = ADAM_B1 * m + (1.0 - ADAM_B1) * g
    v = ADAM_B2 * v + (1.0 - ADAM_B2) * (g * g)
    m_hat = m / (1.0 - ADAM_B1 ** ADAM_STEP)
    v_hat = v / (1.0 - ADAM_B2 ** ADAM_STEP)
    return -ADAM_LR * (m_hat / (jnp.sqrt(v_hat) + ADAM_EPS) + ADAM_WD * w), m, v


def adamw(w, g, m, v, tr=512):
    rows, cols = w.shape
    tr = _row_chunk(rows, tr, 8)

    def body(w_ref, g_ref, m_ref, v_ref, d_ref, nm_ref, nv_ref):
        d_ref[...], nm_ref[...], nv_ref[...] = _adamw_update(w_ref[...], g_ref[...], m_ref[...], v_ref[...])

    blk = pl.BlockSpec((tr, cols), lambda i: (i, 0))
    return pl.pallas_call(
        body, name="adamw", grid=(rows // tr,), out_shape=(SDS(w.shape, F32),) * 3,
        in_specs=[blk] * 4, out_specs=(blk,) * 3, compiler_params=_cparams(),
    )(w, g, m, v)


def adamw_layer(gf, blk, row_off, n_half, l, w, m, v, outs, deps=()):
    fh = gf.shape[1] // 7
    d = gf.shape[2]
    nd = len(deps)

    def body(gf_ref, w_ref, m_ref, v_ref, *rest):
        g_ref, d_ref, nm_ref, nv_ref = rest[4 + nd:]
        g = gf_ref[row_off:row_off + n_half, :]
        g_ref[...] = g
        d_ref[...], nm_ref[...], nv_ref[...] = _adamw_update(w_ref[...], g, m_ref[...], v_ref[...])

    gspec = pl.BlockSpec((None, fh, d), lambda h: (h, blk, 0))
    wspec = pl.BlockSpec((None, n_half, d), lambda h: (l, h, 0))
    return pl.pallas_call(
        body, name="adamw_layer", grid=(2,), out_shape=tuple(SDS(o.shape, o.dtype) for o in outs),
        in_specs=[gspec, wspec, wspec, wspec] + [ANY] * (4 + nd), out_specs=(wspec,) * 4,
        input_output_aliases={4 + i: i for i in range(4)}, compiler_params=_cparams(),
    )(gf, w, m, v, *outs, *deps)


def pack_weight(pos, land, blk, l, w, extra=None, deps=()):
    fh, d = land.shape[2] // 7, land.shape[3]
    nd = len(deps)

    def body(pos_ref, w_ref, *rest):
        o_ref = rest[-1]
        a = w_ref[...].astype(BF)
        n = a.shape[0] // 2
        for h in range(2):
            o_ref[h, 0:n, :] = a[h * n:(h + 1) * n]
        if extra is not None:
            b = rest[0][...].astype(BF)
            nb = b.shape[0] // 2
            for h in range(2):
                o_ref[h, n:n + nb, :] = b[h * nb:(h + 1) * nb]

    def whole(a):
        return pl.BlockSpec((None,) + a.shape[1:], lambda i, p: (l, 0, 0))

    ins = [w] + ([extra] if extra is not None else [])
    return pl.pallas_call(
        body, name="pack_weight", out_shape=SDS(land.shape, land.dtype),
        grid_spec=pltpu.PrefetchScalarGridSpec(
            num_scalar_prefetch=1, grid=(1,),
            in_specs=[whole(a) for a in ins] + [ANY] * (1 + nd),
            out_specs=pl.BlockSpec((None, 2, fh, d), lambda i, p: (p[0], 0, blk, 0))),
        input_output_aliases={1 + len(ins): 0}, compiler_params=_cparams(),
    )(pos, *ins, land, *deps)


def _pack_rows(arrays, width):
    flat = jnp.concatenate([a.reshape(-1).astype(F32) for a in arrays])
    rows = -(-flat.shape[0] // (8 * width)) * 8
    return jnp.pad(flat, (0, rows * width - flat.shape[0])).reshape(rows, width)


def _unpack_rows(buf, shapes):
    flat = buf.reshape(-1)
    out, off = [], 0
    for shp in shapes:
        n = int(np.prod(shp))
        out.append(flat[off:off + n].reshape(shp))
        off += n
    return out


def _t5_buckets(rel):
    half = N_BUCKETS // 2
    max_exact = half // 2
    ret = (rel > 0).astype(jnp.int32) * half
    n = jnp.abs(rel)
    n_f = jnp.maximum(n, 1).astype(F32)
    large = max_exact + (jnp.log(n_f / max_exact) / math.log(MAX_DISTANCE / max_exact) * (half - max_exact)).astype(jnp.int32)
    large = jnp.minimum(large, half - 1)
    return ret + jnp.where(n < max_exact, n, large)


def _band_buckets():
    t = jnp.arange(BLOCK)[:, None]
    j = jnp.arange(3 * BLOCK)[None, :]
    rel = j - BLOCK - t
    return _t5_buckets(rel), jnp.abs(rel) <= WINDOW


def _block_diag_pairs(w):
    depth, two, nblk, bw, _ = w.shape
    pairs = w.reshape(depth, two, nblk // 2, 2, bw, bw)
    z = jnp.zeros_like(pairs[:, :, :, 0])
    top = jnp.concatenate([pairs[:, :, :, 0], z], axis=-1)
    bot = jnp.concatenate([z, pairs[:, :, :, 1]], axis=-1)
    return jnp.concatenate([top, bot], axis=-2)


def _diag_blocks(dw):
    bw = dw.shape[-1] // 2
    a = dw[:, :, :bw, :bw]
    b = dw[:, :, bw:, bw:]
    return jnp.stack([a, b], axis=2).reshape(dw.shape[0], 2 * dw.shape[1], bw, bw)


def kernel(x, ffn1_norm, ffn1_w_gate, ffn1_w_up, ffn1_w_down, mix_norm, w_in, conv_w, conv_b, lru_w_a, lru_b_a, lru_w_x, lru_b_x, lru_lambda, attn_sink, rel_bias, lru_out_norm, attn_out_norm, w_out, ffn2_norm, ffn2_w_gate, ffn2_w_up, ffn2_w_down, final_norm, loss_target, m_ffn1_norm, m_ffn1_w_gate, m_ffn1_w_up, m_ffn1_w_down, m_mix_norm, m_w_in, m_conv_w, m_conv_b, m_lru_w_a, m_lru_b_a, m_lru_w_x, m_lru_b_x, m_lru_lambda, m_attn_sink, m_rel_bias, m_lru_out_norm, m_attn_out_norm, m_w_out, m_ffn2_norm, m_ffn2_w_gate, m_ffn2_w_up, m_ffn2_w_down, m_final_norm, v_ffn1_norm, v_ffn1_w_gate, v_ffn1_w_up, v_ffn1_w_down, v_mix_norm, v_w_in, v_conv_w, v_conv_b, v_lru_w_a, v_lru_b_a, v_lru_w_x, v_lru_b_x, v_lru_lambda, v_attn_sink, v_rel_bias, v_lru_out_norm, v_attn_out_norm, v_w_out, v_ffn2_norm, v_ffn2_w_gate, v_ffn2_w_up, v_ffn2_w_down, v_final_norm):
    depth, d = ffn1_norm.shape
    d_ff = N_CHIPS * ffn1_w_gate.shape[2]
    d_in = N_CHIPS * w_in.shape[2]
    lw = conv_b.shape[1]
    att = N_HEADS * HEAD_DIM
    lay = Layout(d, d_ff, d_in)
    k_chip = 2 * lax.axis_index("x") + lax.axis_index("y")
    pos = jnp.stack([k_chip, lax.axis_index("c")]).astype(jnp.int32)

    def rows_major(a):
        return jnp.swapaxes(a, 1, 2)

    mats = (rows_major(ffn1_w_gate), rows_major(ffn1_w_up), ffn1_w_down,
            rows_major(ffn2_w_gate), rows_major(ffn2_w_up), ffn2_w_down)

    def pack_layer(l, deps=()):
        land = lax.empty((N_CHIPS, 2, lay.rows, d), BF)
        for m, a in enumerate(mats):
            land = pack_weight(pos, land, m, l, a, deps=deps if m == 0 else ())
        return pack_weight(pos, land, lay.MIX_BLK, l, rows_major(w_in), extra=w_out)

    def gather_start(l, land):
        return split_start(f"gather_start_{l}", [land], 3, gather_plan)

    def gather_finish(l, started, after):
        ssem, rsem, bufs, _ = started
        land, = split_wait(f"gather_wait_{l}", ssem, rsem, bufs, after, gather_plan)
        return gather_pair(land)

    sharded_small = (conv_w, lru_b_a, lru_b_x, lru_lambda)
    sshard = jnp.concatenate([a.reshape(-1, LANE) for a in sharded_small], axis=0)
    sfull = gather_small(sshard)
    small_full, off = [], 0
    for a in sharded_small:
        r = a.shape[0] * a.shape[1]
        piece = sfull[:, off:off + r].reshape((N_CHIPS,) + a.shape)
        small_full.append(jnp.moveaxis(piece, 0, 2).reshape(a.shape[0], a.shape[1], N_CHIPS * LANE))
        off += r
    conv_w_f, b_a_f, b_x_f, lam_f = small_full

    zrow = jnp.zeros((1, lw), F32)
    wblk_a = _block_diag_pairs(lru_w_a)
    wblk_x = _block_diag_pairs(lru_w_x)
    buckets, in_band = _band_buckets()
    onehot = (buckets.reshape(-1)[:, None] == jnp.arange(N_BUCKETS)[None, :]).astype(F32)
    bias = jnp.dot(rel_bias.T, onehot.T, precision=lax.Precision.HIGHEST).reshape(N_HEADS, BLOCK, 3 * BLOCK)
    bias = jnp.where(in_band[None], bias, NEG_INF)
    bias = bias.reshape(N_KV_HEADS, KV_GROUP, BLOCK, 3 * BLOCK).transpose(0, 3, 1, 2).reshape(N_KV_HEADS, 3 * BLOCK, KV_GROUP * BLOCK)
    kblk = 2 * lw // LANE

    def layer_small(l):
        cvec = jnp.concatenate([conv_w_f[l], jnp.zeros((8 - CONV_WIDTH, lw), F32)], axis=0)
        pvec = jnp.concatenate([conv_b[l][None], b_a_f[l], b_x_f[l], lam_f[l], zrow], axis=0)
        wblk = jnp.stack([wblk_a[l, 0], wblk_x[l, 0], wblk_a[l, 1], wblk_x[l, 1]]).astype(BF)
        sink = jnp.broadcast_to(jnp.repeat(attn_sink[l], BLOCK).reshape(N_KV_HEADS, 1, KV_GROUP * BLOCK),
                                (N_KV_HEADS, 8, KV_GROUP * BLOCK))
        return cvec, pvec, wblk, sink

    xs = x[0]
    wfull = [None] * depth
    first = gather_start(0, pack_layer(0, deps=(sfull,)))
    lands = {l: pack_layer(l, deps=(first[3],)) for l in range(1, depth)}
    wfull[0] = gather_finish(0, first, [xs] + list(lands.values()))
    started = gather_start(1, lands[1]) if depth > 1 else None
    saved = []
    for l in range(depth):
        cvec, pvec, wblk, sink = layer_small(l)
        deps = (started[3],) if started is not None else ()
        x1, gate1, up1 = ffn_forward(xs, ffn1_norm[l][None], wfull[l], lay, 0, deps=deps)
        proj, qt, vt = mix_project(x1, mix_norm[l][None], wfull[l], lay, lw, att)
        y_rec, hs = lru_forward(proj, cvec, pvec, wblk, lw)
        y_att = attention_forward(qt, proj, vt, bias, sink, kblk)
        x2 = mix_output(x1, y_rec, y_att, lru_out_norm[l][None], attn_out_norm[l][:, None], wfull[l], lay)
        x3, gate2, up2 = ffn_forward(x2, ffn2_norm[l][None], wfull[l], lay, 1)
        saved.append((xs, x1, x2, proj, qt, y_rec, hs, y_att, (gate1, up1), (gate2, up2)))
        xs = x3
        if l + 1 < depth:
            wfull[l + 1] = gather_finish(l + 1, started, [x3])
            started = gather_start(l + 2, lands[l + 2]) if l + 2 < depth else None

    dx, d_final, loss_tile = loss_head(xs, final_norm[None], loss_target[0])
    loss = lax.psum(loss_tile[0, 0], ("x", "y", "c"))

    layer_names = ["ffn1_norm", "mix_norm", "conv_w", "conv_b", "lru_w_a", "lru_b_a", "lru_w_x", "lru_b_x", "lru_lambda",
                   "attn_sink", "lru_out_norm", "attn_out_norm", "ffn2_norm"]
    dbias_total = jnp.zeros(bias.shape, F32)

    def ffn_back(xin, gain, dout, pre, gb, l, which, deps=()):
        dxo, dg, dgate, dup, act, h, df = ffn_backward_dx(xin, gain, dout, *pre, wfull[l], lay, which, deps=deps)
        gb = weight_grad_tn(dgate, h, gb, lay, 3 * which + 0)
        gb = weight_grad_tn(dup, h, gb, lay, 3 * which + 1)
        gb = weight_grad_tn(act, df, gb, lay, 3 * which + 2)
        return dxo, dg[0], gb

    def reduce_start(l, gb, sb):
        p1, sp1 = exchange_pair(gb, sb)
        cs = pair_sum(pos, gb, p1)
        ss = small_pair_sum(sb, sp1)
        lands = [lax.empty((3,) + cs.shape[1:], cs.dtype), lax.empty((N_CHIPS,) + ss.shape, ss.dtype)]
        return split_start(f"reduce_start_{l}", [cs, ss] + lands, 6, reduce_plan)

    def reduce_finish(l, started, after):
        ssem, rsem, bufs, _ = started
        cs, ss, p3, sp3 = split_wait(f"reduce_wait_{l}", ssem, rsem, bufs, after, reduce_plan)
        return exchange_final(chip_sum(pos, cs, p3)), small_chip_sum(pos, ss, sp3)

    gf = [None] * depth
    small_sums = [None] * depth
    small_shapes = [None] * depth
    in_flight = None
    for l in reversed(range(depth)):
        x0, x1, x2, proj, qt, y_rec, hs, y_att, pre1, pre2 = saved[l]
        cvec, pvec, wblk, sink = layer_small(l)
        gb = lax.empty((N_CHIPS, 2, lay.rows, d), BF)
        part = {}
        deps = (in_flight[1][3],) if in_flight is not None else ()
        dx, part["ffn2_norm"], gb = ffn_back(x2, ffn2_norm[l][None], dx, pre2, gb, l, 1, deps=deps)
        dyr, dya, dgr, dga, dwout = mix_output_backward(dx, y_rec, y_att, lru_out_norm[l][None], attn_out_norm[l][:, None],
                                                        wfull[l], lay)
        part["lru_out_norm"] = dgr[0]
        part["attn_out_norm"] = dga[:, 0]
        dq, dkv, dbias, dsink = attention_backward(qt, proj, y_att, dya, bias, sink, kblk)
        dbias_total = dbias_total + dbias
        part["attn_sink"] = jnp.sum(dsink[:, 0, :].reshape(N_HEADS, BLOCK), axis=1)
        dxr, dgt, dcv, dpv, dwb = lru_backward(proj, hs, dyr, cvec, pvec, wblk, lw)
        part["conv_w"] = dcv[:CONV_WIDTH]
        part["conv_b"] = dpv[0]
        part["lru_b_a"] = dpv[1:3]
        part["lru_b_x"] = dpv[3:5]
        part["lru_lambda"] = dpv[5:7]
        part["lru_w_a"] = _diag_blocks(jnp.stack([dwb[0], dwb[2]]))
        part["lru_w_x"] = _diag_blocks(jnp.stack([dwb[1], dwb[3]]))
        dx, dgm, gb = mix_project_backward(x1, mix_norm[l][None], dx, dxr, dgt, dq, dkv, dwout, wfull[l], gb, lay)
        part["mix_norm"] = dgm[0]
        dx, part["ffn1_norm"], gb = ffn_back(x0, ffn1_norm[l][None], dx, pre1, gb, l, 0)
        pieces = [part[n] for n in layer_names]
        if l == 0:
            dbias_heads = dbias_total.reshape(N_KV_HEADS, 3 * BLOCK, KV_GROUP, BLOCK).transpose(0, 2, 3, 1)
            d_rel_bias = jnp.dot(dbias_heads.reshape(N_HEADS, -1), onehot, precision=lax.Precision.HIGHEST).T
            pieces += [d_rel_bias, d_final[0]]
        small_shapes[l] = [p.shape for p in pieces]
        if in_flight is not None:
            gf[in_flight[0]], small_sums[in_flight[0]] = reduce_finish(in_flight[0], in_flight[1], [dx])
        in_flight = (l, reduce_start(l, gb, _pack_rows(pieces, 1024)))
    grad_x = dx[None]

    weights = dict(ffn1_norm=ffn1_norm, ffn1_w_gate=ffn1_w_gate, ffn1_w_up=ffn1_w_up, ffn1_w_down=ffn1_w_down, mix_norm=mix_norm, w_in=w_in, conv_w=conv_w, conv_b=conv_b, lru_w_a=lru_w_a, lru_b_a=lru_b_a, lru_w_x=lru_w_x, lru_b_x=lru_b_x, lru_lambda=lru_lambda, attn_sink=attn_sink, rel_bias=rel_bias, lru_out_norm=lru_out_norm, attn_out_norm=attn_out_norm, w_out=w_out, ffn2_norm=ffn2_norm, ffn2_w_gate=ffn2_w_gate, ffn2_w_up=ffn2_w_up, ffn2_w_down=ffn2_w_down, final_norm=final_norm)
    m_in = dict(ffn1_norm=m_ffn1_norm, ffn1_w_gate=m_ffn1_w_gate, ffn1_w_up=m_ffn1_w_up, ffn1_w_down=m_ffn1_w_down, mix_norm=m_mix_norm, w_in=m_w_in, conv_w=m_conv_w, conv_b=m_conv_b, lru_w_a=m_lru_w_a, lru_b_a=m_lru_b_a, lru_w_x=m_lru_w_x, lru_b_x=m_lru_b_x, lru_lambda=m_lru_lambda, attn_sink=m_attn_sink, rel_bias=m_rel_bias, lru_out_norm=m_lru_out_norm, attn_out_norm=m_attn_out_norm, w_out=m_w_out, ffn2_norm=m_ffn2_norm, ffn2_w_gate=m_ffn2_w_gate, ffn2_w_up=m_ffn2_w_up, ffn2_w_down=m_ffn2_w_down, final_norm=m_final_norm)
    v_in = dict(ffn1_norm=v_ffn1_norm, ffn1_w_gate=v_ffn1_w_gate, ffn1_w_up=v_ffn1_w_up, ffn1_w_down=v_ffn1_w_down, mix_norm=v_mix_norm, w_in=v_w_in, conv_w=v_conv_w, conv_b=v_conv_b, lru_w_a=v_lru_w_a, lru_b_a=v_lru_b_a, lru_w_x=v_lru_w_x, lru_b_x=v_lru_b_x, lru_lambda=v_lru_lambda, attn_sink=v_attn_sink, rel_bias=v_rel_bias, lru_out_norm=v_lru_out_norm, attn_out_norm=v_attn_out_norm, w_out=v_w_out, ffn2_norm=v_ffn2_norm, ffn2_w_gate=v_ffn2_w_gate, ffn2_w_up=v_ffn2_w_up, ffn2_w_down=v_ffn2_w_down, final_norm=v_final_norm)
    order = list(weights)
    large = [(name, m, 0, lay.fh, m % 3 != 2) for m, name in
             enumerate(("ffn1_w_gate", "ffn1_w_up", "ffn1_w_down", "ffn2_w_gate", "ffn2_w_up", "ffn2_w_down"))]
    large += [("w_in", lay.MIX_BLK, 0, lay.ih, True), ("w_out", lay.MIX_BLK, lay.ih, lay.oh, False)]
    as_rows = {name: [rows_major(src[name]) if flip else src[name] for src in (weights, m_in, v_in)]
               for name, _, _, _, flip in large}
    stacked = {name: tuple(lax.empty(as_rows[name][0].shape, F32) for _ in range(4)) for name, *_ in large}

    def adamw_large(l, deps=()):
        for i, (name, blk, row_off, n_half, _) in enumerate(large):
            stacked[name] = adamw_layer(gf[l], blk, row_off, n_half, l, *as_rows[name], stacked[name],
                                        deps=deps if i == 0 else ())

    last = in_flight[0]
    for l in range(depth):
        if l != last:
            adamw_large(l, deps=(in_flight[1][3],))
    ready = [buf for name, *_ in large for buf in stacked[name]] if depth > 1 else []
    gf[last], small_sums[last] = reduce_finish(last, in_flight[1], [dx] + ready)
    adamw_large(last)

    per_layer = [_unpack_rows(small_sums[l], small_shapes[l]) for l in range(depth)]
    grads = {n: jnp.stack([per_layer[l][i] for l in range(depth)]) for i, n in enumerate(layer_names)}
    grads["rel_bias"], grads["final_norm"] = per_layer[0][len(layer_names):]
    for name in ("conv_w", "lru_b_a", "lru_b_x", "lru_lambda"):
        grads[name] = lax.dynamic_slice_in_dim(grads[name], k_chip * LANE, LANE, axis=2)
    delta, new_m, new_v = {}, {}, {}
    for name, _, _, _, flip in large:
        grads[name], delta[name], new_m[name], new_v[name] = [rows_major(a) if flip else a for a in stacked[name]]
    small = [n for n in order if n not in stacked]
    packed = [_pack_rows([src[n] for n in small], 1024) for src in (weights, grads, m_in, v_in)]
    outs = adamw(*packed)
    shapes = [weights[n].shape for n in small]
    for dst, buf in zip((delta, new_m, new_v), outs):
        dst.update(zip(small, _unpack_rows(buf, shapes)))

    return (loss, grad_x, *[grads[n] for n in order], *[delta[n] for n in order],
            *[new_m[n] for n in order], *[new_v[n] for n in order])
```

```python
import functools
import math

import jax
import jax.numpy as jnp
import numpy as np
from jax import lax
from jax.experimental import pallas as pl
from jax.experimental.pallas import tpu as pltpu

BF = jnp.bfloat16
F32 = jnp.float32
SDS = jax.ShapeDtypeStruct
MESH = pl.DeviceIdType.MESH
ANY = pl.BlockSpec(memory_space=pl.ANY)

N_CHIPS = 4
N_HEADS = 8
N_KV_HEADS = 2
KV_GROUP = N_HEADS // N_KV_HEADS
HEAD_DIM = 64
BLOCK = 128
WINDOW = 128
N_BUCKETS = 32
MAX_DISTANCE = 128
LRU_C = 8.0
CONV_WIDTH = 4
LANE = 128
SUBLANES = 8
SCAN_CHAINS = 8
EPS = 1e-6
FFN_RES = 0.5
NEG_INF = -1e30
ADAM_LR = 0.001
ADAM_B1 = 0.9
ADAM_B2 = 0.999
ADAM_EPS = 1e-08
ADAM_WD = 0.01
ADAM_STEP = 10
VMEM_LIMIT = 60000 * 1024
GELU_C = math.sqrt(2.0 / math.pi)


def dot_nn(a, b):
    return lax.dot_general(a, b, (((1,), (0,)), ((), ())), preferred_element_type=F32)


def dot_nt(a, b):
    return lax.dot_general(a, b, (((1,), (1,)), ((), ())), preferred_element_type=F32)


def dot_tn(a, b):
    return lax.dot_general(a, b, (((0,), (0,)), ((), ())), preferred_element_type=F32)


def _cparams(**kw):
    return pltpu.CompilerParams(vmem_limit_bytes=VMEM_LIMIT, **kw)


class Layout:
    MIX_BLK = 6

    def __init__(self, d_model, d_ff, d_in):
        self.fh = d_ff // (2 * N_CHIPS)
        self.ih = d_in // (2 * N_CHIPS)
        self.oh = d_model // (2 * N_CHIPS)
        assert self.ih + self.oh == self.fh, "w_in^T and w_out rows must fill one ffn-sized block"
        self.rows = 7 * self.fh


def _row_chunk(rows, target, step=16):
    best = rows
    for c in range(step, min(rows, target) + 1, step):
        if rows % c == 0:
            best = c
    return best


def _mesh_pos():
    return lax.axis_index("x"), lax.axis_index("y"), lax.axis_index("c")


def _rcopy(src, dst, ssem, rsem, dev):
    return pltpu.make_async_remote_copy(src_ref=src, dst_ref=dst, send_sem=ssem, recv_sem=rsem,
                                        device_id=dev, device_id_type=MESH)


HBM = pl.BlockSpec(memory_space=pltpu.HBM)
SEM = pl.BlockSpec(memory_space=pltpu.SEMAPHORE)
DATAFLOW = pltpu.SideEffectType.DATAFLOW_SIDE_EFFECTING


def _chip_peers():
    x, y, c = _mesh_pos()
    peers = [(1 - x, y), (x, 1 - y), (1 - x, 1 - y)]
    return x, y, c, 2 * x + y, [(px, py, 2 * px + py) for px, py in peers]


def split_start(name, bufs, n, plan):
    nb = len(bufs)

    def body(*refs):
        sends, _ = plan(refs[:nb], refs[nb], refs[nb + 1])
        for cp in sends:
            cp.start()
        refs[-1][...] = jnp.zeros_like(refs[-1])

    out = pl.pallas_call(
        body, name=name,
        out_shape=(pltpu.SemaphoreType.DMA((n,)), pltpu.SemaphoreType.DMA((n,)),
                   *[pltpu.HBM(b.shape, b.dtype) for b in bufs], SDS((8, LANE), F32)),
        in_specs=[HBM] * nb, out_specs=(SEM, SEM, *([HBM] * nb), pl.BlockSpec(memory_space=pltpu.VMEM)),
        input_output_aliases={i: 2 + i for i in range(nb)},
        compiler_params=pltpu.CompilerParams(has_side_effects=DATAFLOW),
    )(*[pltpu.with_memory_space_constraint(b, pltpu.HBM) for b in bufs])
    return out[0], out[1], list(out[2:2 + nb]), out[-1]


def split_wait(name, ssem, rsem, bufs, after, plan):
    nb = len(bufs)

    def body(*refs):
        sends, recvs = plan(refs[:nb], refs[nb], refs[nb + 1])
        for cp in recvs:
            cp.wait_recv()
        for cp in sends:
            cp.wait_send()

    out = pl.pallas_call(
        body, name=name, out_shape=tuple(pltpu.HBM(b.shape, b.dtype) for b in bufs),
        in_specs=[HBM] * nb + [SEM, SEM] + [ANY] * len(after), out_specs=tuple([HBM] * nb),
        input_output_aliases={i: i for i in range(nb)},
        compiler_params=pltpu.CompilerParams(has_side_effects=DATAFLOW),
    )(*bufs, ssem, rsem, *after)
    return list(out)


def gather_plan(refs, ssem, rsem):
    land_ref, = refs
    _, _, c, k, peers = _chip_peers()
    sends = [_rcopy(land_ref.at[k, c], land_ref.at[k, c], ssem.at[j], rsem.at[j], (px, py, c))
             for j, (px, py, _) in enumerate(peers)]
    recvs = [_rcopy(land_ref.at[kp, c], land_ref.at[kp, c], ssem.at[j], rsem.at[j], (px, py, c))
             for j, (px, py, kp) in enumerate(peers)]
    return sends, recvs


def reduce_plan(refs, ssem, rsem):
    cs_ref, ss_ref, p3_ref, sp3_ref = refs
    _, _, c, k, peers = _chip_peers()
    sends, recvs = [], []
    for j, (px, py, kp) in enumerate(peers):
        sends.append(_rcopy(cs_ref.at[kp], p3_ref.at[j], ssem.at[j], rsem.at[j], (px, py, c)))
        recvs.append(_rcopy(cs_ref.at[kp], p3_ref.at[j], ssem.at[j], rsem.at[j], (px, py, c)))
        sends.append(_rcopy(ss_ref, sp3_ref.at[k], ssem.at[3 + j], rsem.at[3 + j], (px, py, c)))
        recvs.append(_rcopy(ss_ref, sp3_ref.at[kp], ssem.at[3 + j], rsem.at[3 + j], (px, py, c)))
    return sends, recvs


def gather_small(sshard):
    def body(s_ref, sf_ref, lsem, ssem, rsem):
        _, _, c, k, peers = _chip_peers()
        own = pltpu.make_async_copy(s_ref, sf_ref.at[k], lsem)
        own.start()
        sends = [_rcopy(s_ref, sf_ref.at[k], ssem.at[j], rsem.at[j], (px, py, c)) for j, (px, py, _) in enumerate(peers)]
        recvs = [_rcopy(s_ref, sf_ref.at[kp], ssem.at[j], rsem.at[j], (px, py, c)) for j, (px, py, kp) in enumerate(peers)]
        for cp in sends:
            cp.start()
        for cp in recvs:
            cp.wait_recv()
        for cp in sends:
            cp.wait_send()
        own.wait()

    return pl.pallas_call(
        body, name="gather_small", out_shape=SDS((N_CHIPS,) + sshard.shape, sshard.dtype),
        in_specs=[ANY], out_specs=ANY,
        scratch_shapes=[pltpu.SemaphoreType.DMA, pltpu.SemaphoreType.DMA((3,)), pltpu.SemaphoreType.DMA((3,))],
    )(sshard)


def exchange_now(name, bufs, n, plan):
    nb = len(bufs)

    def body(*refs):
        sends, recvs = plan(refs[nb:2 * nb], refs[2 * nb], refs[2 * nb + 1])
        for cp in sends:
            cp.start()
        for cp in recvs:
            cp.wait_recv()
        for cp in sends:
            cp.wait_send()

    return list(pl.pallas_call(
        body, name=name, out_shape=tuple(SDS(b.shape, b.dtype) for b in bufs),
        in_specs=[ANY] * nb, out_specs=tuple([ANY] * nb), input_output_aliases={i: i for i in range(nb)},
        scratch_shapes=[pltpu.SemaphoreType.DMA((n,)), pltpu.SemaphoreType.DMA((n,))],
    )(*bufs))


def handover_plan(refs, ssem, rsem):
    land_ref, = refs
    x, y, c, _, peers = _chip_peers()
    sib = (x, y, 1 - c)
    sends = [_rcopy(land_ref.at[kp, c], land_ref.at[kp, c], ssem.at[j], rsem.at[j], sib) for j, (_, _, kp) in enumerate(peers)]
    recvs = [_rcopy(land_ref.at[kp, 1 - c], land_ref.at[kp, 1 - c], ssem.at[j], rsem.at[j], sib)
             for j, (_, _, kp) in enumerate(peers)]
    return sends, recvs


def pair_plan(refs, ssem, rsem):
    gb_ref, sb_ref, p_ref, sp_ref = refs
    x, y, c = _mesh_pos()
    sib = (x, y, 1 - c)
    n = gb_ref.shape[0]
    copies = [_rcopy(gb_ref.at[kk, 1 - c], p_ref.at[kk], ssem.at[kk], rsem.at[kk], sib) for kk in range(n)]
    copies.append(_rcopy(sb_ref, sp_ref, ssem.at[n], rsem.at[n], sib))
    return copies, copies


def final_plan(refs, ssem, rsem):
    gf_ref, = refs
    x, y, c = _mesh_pos()
    sib = (x, y, 1 - c)
    return ([_rcopy(gf_ref.at[c], gf_ref.at[c], ssem.at[0], rsem.at[0], sib)],
            [_rcopy(gf_ref.at[1 - c], gf_ref.at[1 - c], ssem.at[0], rsem.at[0], sib)])


def pair_sum(pos, gb, p1):
    n, _, rh, d = gb.shape
    cr = _row_chunk(rh, 1024)

    def body(pos_ref, a_ref, b_ref, o_ref):
        o_ref[...] = (a_ref[...].astype(F32) + b_ref[...].astype(F32)).astype(o_ref.dtype)

    return pl.pallas_call(
        body, name="pair_sum", out_shape=SDS((n, rh, d), gb.dtype),
        grid_spec=pltpu.PrefetchScalarGridSpec(
            num_scalar_prefetch=1, grid=(n, rh // cr),
            in_specs=[pl.BlockSpec((None, None, cr, d), lambda kk, r, pos: (kk, pos[1], r, 0)),
                      pl.BlockSpec((None, cr, d), lambda kk, r, pos: (kk, r, 0))],
            out_specs=pl.BlockSpec((None, cr, d), lambda kk, r, pos: (kk, r, 0))),
        compiler_params=_cparams(),
    )(pos, gb, p1)


def chip_sum(pos, cs, p3):
    n, rh, d = cs.shape
    cr = _row_chunk(rh, 512)

    def body(pos_ref, a_ref, b_ref, o_ref):
        acc = a_ref[...].astype(F32)
        for j in range(3):
            acc = acc + b_ref[j].astype(F32)
        o_ref[...] = acc

    return pl.pallas_call(
        body, name="chip_sum", out_shape=SDS((2, rh, d), F32),
        grid_spec=pltpu.PrefetchScalarGridSpec(
            num_scalar_prefetch=1, grid=(rh // cr,),
            in_specs=[pl.BlockSpec((None, cr, d), lambda r, pos: (pos[0], r, 0)),
                      pl.BlockSpec((3, cr, d), lambda r, pos: (0, r, 0))],
            out_specs=pl.BlockSpec((None, cr, d), lambda r, pos: (pos[1], r, 0))),
        compiler_params=_cparams(),
    )(pos, cs, p3)


def small_pair_sum(a, b):
    def body(a_ref, b_ref, o_ref):
        o_ref[...] = a_ref[...] + b_ref[...]

    return pl.pallas_call(body, name="small_pair_sum", out_shape=SDS(a.shape, a.dtype),
                          compiler_params=_cparams())(a, b)


def small_chip_sum(pos, own, p):
    ns, w = own.shape

    def body(pos_ref, own_ref, p0, p1, p2, p3, o_ref):
        k = pos_ref[0]
        acc = None
        for chip, ref in enumerate((p0, p1, p2, p3)):
            term = jnp.where(k == chip, own_ref[...], ref[...])
            acc = term if acc is None else acc + term
        o_ref[...] = acc

    def slot(chip):
        return pl.BlockSpec((None, ns, w), lambda i, pos: (jnp.where(pos[0] == chip, (chip + 1) % N_CHIPS, chip), 0, 0))

    return pl.pallas_call(
        body, name="small_chip_sum", out_shape=SDS(own.shape, own.dtype),
        grid_spec=pltpu.PrefetchScalarGridSpec(
            num_scalar_prefetch=1, grid=(1,),
            in_specs=[pl.BlockSpec((ns, w), lambda i, pos: (0, 0))] + [slot(chip) for chip in range(N_CHIPS)],
            out_specs=pl.BlockSpec((ns, w), lambda i, pos: (0, 0))),
        compiler_params=_cparams(),
    )(pos, own, p, p, p, p)


def _rms(x, g):
    rs = lax.rsqrt(jnp.mean(x * x, axis=-1, keepdims=True) + EPS)
    xh = x * rs
    return xh, rs, xh * g


def _rms_bwd(dy, xh, rs, g):
    dxh = dy * g
    dx = rs * (dxh - xh * jnp.mean(dxh * xh, axis=-1, keepdims=True))
    return dx, dy * xh


def _gelu(x):
    t = jnp.tanh(GELU_C * (x + 0.044715 * x * x * x))
    return 0.5 * x * (1.0 + t), t


def _gelu_grad(x, t):
    return 0.5 * (1.0 + t) + 0.5 * x * (1.0 - t * t) * GELU_C * (1.0 + 3.0 * 0.044715 * x * x)


def _shift_rows(v, s, n):
    if s == 0:
        return v
    t = lax.broadcasted_iota(jnp.int32, v.shape, 0)
    rolled = pltpu.roll(v, (-s) % n, 0)
    return jnp.where((t + s >= 0) & (t + s < n), rolled, 0.0)


def _scan_rows(a_ref, u_ref, h_ref, acum_ref, reverse):
    s_len, w = a_ref.shape
    chunk = min(512, s_len)
    last = 0 if reverse else SUBLANES - 1

    def inside_vregs(ci, _):
        rows = pl.ds(pl.multiple_of(ci * chunk, chunk), chunk)
        a = a_ref[rows, :]
        u = u_ref[rows, :]
        pos = lax.broadcasted_iota(jnp.int32, (chunk, w), 0) % SUBLANES
        for dist in (1, 2, 4):
            ok = (pos < SUBLANES - dist) if reverse else (pos >= dist)
            shift = chunk - dist if reverse else dist
            u = u + a * jnp.where(ok, pltpu.roll(u, shift, 0), 0.0)
            a = a * jnp.where(ok, pltpu.roll(a, shift, 0), 1.0)
        h_ref[rows, :] = u
        acum_ref[rows, :] = a
        return 0

    lax.fori_loop(0, s_len // chunk, inside_vregs, 0)

    chains = max(1, min(SCAN_CHAINS, s_len // (8 * SUBLANES)))
    seg = s_len // chains
    nvreg = seg // SUBLANES

    def step(j, carry):
        jj = (nvreg - 1 - j) if reverse else j
        out = []
        for c, (hin, ain) in enumerate(carry):
            rows = pl.ds(pl.multiple_of(c * seg + jj * SUBLANES, SUBLANES), SUBLANES)
            acc = acum_ref[rows, :]
            h = h_ref[rows, :] + acc * hin
            acc = acc * ain
            h_ref[rows, :] = h
            acum_ref[rows, :] = acc
            out.append((jnp.broadcast_to(h[last:last + 1, :], h.shape), jnp.broadcast_to(acc[last:last + 1, :], acc.shape)))
        return tuple(out)

    init = tuple((jnp.zeros((SUBLANES, w), F32), jnp.ones((SUBLANES, w), F32)) for _ in range(chains))
    ends = lax.fori_loop(0, nvreg, step, init, unroll=min(2, nvreg))
    order = range(chains - 2, -1, -1) if reverse else range(1, chains)
    inflow = jnp.zeros((1, w), F32)
    for s in order:
        h, acc = ends[s + 1 if reverse else s - 1]
        inflow = h[0:1, :] + acc[0:1, :] * inflow
        rows = pl.ds(s * seg, seg)
        h_ref[rows, :] = h_ref[rows, :] + acum_ref[rows, :] * inflow


def _w_spec(rows_half, d, blk):
    return pl.BlockSpec((N_CHIPS, 2, rows_half, d), lambda *_: (0, 0, blk, 0), pipeline_mode=pl.Buffered(1))


def ffn_forward(x, gain, wfull, lay, which, deps=(), tm=512):
    s_len, d = x.shape
    tm = min(tm, s_len)
    f = 8 * lay.fh
    fc = f // 2

    def body(x_ref, g_ref, wg_ref, wu_ref, wd_ref, *rest):
        o_ref, gate_ref, up_ref = rest[len(deps):]
        x = x_ref[...]
        _, _, hn = _rms(x, g_ref[...])
        h = hn.astype(BF)
        y = jnp.zeros((tm, d), F32)
        for part in range(2):
            cols = slice(part * fc, (part + 1) * fc)
            gate = dot_nt(h, wg_ref[...].reshape(f, d)[cols])
            up = dot_nt(h, wu_ref[...].reshape(f, d)[cols])
            act = (gate * jax.nn.sigmoid(gate) * up).astype(BF)
            y = y + dot_nn(act, wd_ref[...].reshape(f, d)[cols])
            gate_ref[:, cols] = gate.astype(BF)
            up_ref[:, cols] = up.astype(BF)
        o_ref[...] = x + FFN_RES * y

    row = pl.BlockSpec((tm, d), lambda i: (i, 0))
    wide = pl.BlockSpec((tm, f), lambda i: (i, 0))
    return pl.pallas_call(
        body, name="ffn_forward", grid=(s_len // tm,),
        out_shape=(SDS((s_len, d), F32), SDS((s_len, f), BF), SDS((s_len, f), BF)),
        in_specs=[row, pl.BlockSpec((1, d), lambda i: (0, 0))]
        + [_w_spec(lay.fh, d, 3 * which + m) for m in range(3)] + [ANY] * len(deps),
        out_specs=(row, wide, wide), compiler_params=_cparams(),
    )(x, gain, wfull, wfull, wfull, *deps)


def ffn_backward_dx(x, gain, dout, gate_bf, up_bf, wfull, lay, which, deps=(), tm=256):
    s_len, d = x.shape
    tm = min(tm, s_len)
    f = 8 * lay.fh
    fc = f // 2
    nt = s_len // tm

    def body(x_ref, g_ref, do_ref, gate_ref, up_ref, wg_ref, wu_ref, wd_ref, *rest):
        dx_ref, dg_ref, dgate_ref, dup_ref, act_ref, h_ref, df_ref = rest[len(deps):]
        x = x_ref[...]
        g = g_ref[...]
        xh, rs, hn = _rms(x, g)
        h = hn.astype(BF)
        do = do_ref[...]
        df = (FFN_RES * do).astype(BF)
        dh = jnp.zeros((tm, d), F32)
        for part in range(2):
            cols = slice(part * fc, (part + 1) * fc)
            wg = wg_ref[...].reshape(f, d)[cols]
            wu = wu_ref[...].reshape(f, d)[cols]
            gate = gate_ref[:, cols].astype(F32)
            up = up_ref[:, cols].astype(F32)
            sg = jax.nn.sigmoid(gate)
            silu = gate * sg
            dact = dot_nt(df, wd_ref[...].reshape(f, d)[cols])
            dup = (dact * silu).astype(BF)
            dgate = (dact * up * (sg * (1.0 + gate * (1.0 - sg)))).astype(BF)
            dh = dh + dot_nn(dgate, wg) + dot_nn(dup, wu)
            dgate_ref[:, cols] = dgate
            dup_ref[:, cols] = dup
            act_ref[:, cols] = (silu * up).astype(BF)
        dxn, dgrow = _rms_bwd(dh, xh, rs, g)
        dx_ref[...] = do + dxn

        @pl.when(pl.program_id(0) == 0)
        def _():
            dg_ref[...] = jnp.zeros_like(dg_ref)

        dg_ref[...] += jnp.sum(dgrow, axis=0, keepdims=True)
        h_ref[...] = h
        df_ref[...] = df

    row = pl.BlockSpec((tm, d), lambda i: (i, 0))
    wide = pl.BlockSpec((tm, f), lambda i: (i, 0))
    vec = pl.BlockSpec((1, d), lambda i: (0, 0))
    return pl.pallas_call(
        body, name="ffn_backward_dx", grid=(nt,),
        out_shape=(SDS((s_len, d), F32), SDS((1, d), F32), SDS((s_len, f), BF), SDS((s_len, f), BF),
                   SDS((s_len, f), BF), SDS((s_len, d), BF), SDS((s_len, d), BF)),
        in_specs=[row, vec, row, wide, wide] + [_w_spec(lay.fh, d, 3 * which + m) for m in range(3)] + [ANY] * len(deps),
        out_specs=(row, vec, wide, wide, wide, row, row), compiler_params=_cparams(),
    )(x, gain, dout, gate_bf, up_bf, wfull, wfull, wfull, *deps)


def weight_grad_tn(a, b, gb, lay, blk, tk=2048):
    s_len, f = a.shape
    tk = min(tk, s_len)
    d = b.shape[1]
    fc = f // 2
    nk = s_len // tk

    def body(a_ref, b_ref, gb_ref, o_ref, acc):
        kt = pl.program_id(1)

        @pl.when(kt == 0)
        def _():
            acc[...] = jnp.zeros_like(acc)

        acc[...] += dot_tn(a_ref[...], b_ref[...])

        @pl.when(kt == nk - 1)
        def _():
            for p in range(2):
                for q in range(2):
                    o_ref[p, q] = acc[pl.ds((2 * p + q) * lay.fh, lay.fh), :].astype(o_ref.dtype)

    return pl.pallas_call(
        body, name="weight_grad_tn", grid=(2, nk), out_shape=SDS(gb.shape, gb.dtype),
        in_specs=[pl.BlockSpec((tk, fc), lambda j, kt: (kt, j)), pl.BlockSpec((tk, d), lambda j, kt: (kt, 0)), ANY],
        out_specs=pl.BlockSpec((2, 2, lay.fh, d), lambda j, kt: (j, 0, blk, 0)),
        scratch_shapes=[pltpu.VMEM((fc, d), F32)],
        input_output_aliases={2: 0}, compiler_params=_cparams(),
    )(a, b, gb)


def _lane_blocks(v):
    return [v[:, j * LANE:(j + 1) * LANE] for j in range(v.shape[1] // LANE)]


def _join_lane_blocks(ref):
    return jnp.concatenate([ref[j] for j in range(ref.shape[0])], axis=1)


def _cbm_spec(nblk, rows, first=0):
    return pl.BlockSpec((nblk, rows, LANE), lambda i: (first // nblk, i, 0))


def mix_project(x, gain, wfull, lay, lw, att, tm=512):
    s_len, d = x.shape
    tm = min(tm, s_len)
    d_in = 8 * lay.ih
    kvw = (d_in - 2 * lw - att) // 2
    ncol = (2 * lw + 2 * kvw) // LANE

    def body(x_ref, g_ref, w_ref, o_ref, qt_ref, vt_ref):
        _, _, hn = _rms(x_ref[...], g_ref[...])
        h = hn.astype(BF)
        w = w_ref[:, :, :lay.ih, :].reshape(d_in, d)
        pieces = _lane_blocks(dot_nt(h, w[:2 * lw])) + _lane_blocks(dot_nt(h, w[2 * lw + att:]))
        for j, piece in enumerate(pieces):
            o_ref[j] = piece
        qt_ref[...] = dot_nt(w[2 * lw:2 * lw + att], h)
        vt_ref[...] = dot_nt(w[2 * lw + att + kvw:], h)

    return pl.pallas_call(
        body, name="mix_project", grid=(s_len // tm,),
        out_shape=(SDS((ncol, s_len, LANE), F32), SDS((att, s_len), F32), SDS((kvw, s_len), F32)),
        in_specs=[pl.BlockSpec((tm, d), lambda i: (i, 0)), pl.BlockSpec((1, d), lambda i: (0, 0)),
                  _w_spec(lay.fh, d, lay.MIX_BLK)],
        out_specs=(_cbm_spec(ncol, tm), pl.BlockSpec((att, tm), lambda i: (0, i)), pl.BlockSpec((kvw, tm), lambda i: (0, i))),
        compiler_params=_cparams(),
    )(x, gain, wfull)


def mix_project_backward(x, gain, dout, dxr, dgt, dqt, dkv, dwout, wfull, gb, lay, tm=512):
    s_len, d = x.shape
    tm = min(tm, s_len)
    d_in = 8 * lay.ih
    nt = s_len // tm
    kvw = dkv.shape[1]
    att = dqt.shape[0]
    nlru = (dxr.shape[0] + dgt.shape[0]) * LANE

    def body(x_ref, g_ref, do_ref, dxr_ref, dgt_ref, dqt_ref, dkv_ref, dwo_ref, w_ref, gb_ref, dx_ref, dg_ref, o_ref, acc):
        i = pl.program_id(0)
        g = g_ref[...]
        xh, rs, hn = _rms(x_ref[...], g)
        h = hn.astype(BF)
        w = w_ref[:, :, :lay.ih, :].reshape(d_in, d)
        dlru = jnp.concatenate([_join_lane_blocks(dxr_ref), _join_lane_blocks(dgt_ref)], axis=1).astype(BF)
        dqt = dqt_ref[...].astype(BF)
        dkv = dkv_ref[...].astype(BF)
        dh = dot_nn(dlru, w[:nlru]) + dot_tn(dqt, w[nlru:nlru + att]) + dot_nn(dkv, w[nlru + att:])
        dxn, dgrow = _rms_bwd(dh, xh, rs, g)
        dx_ref[...] = do_ref[...] + dxn

        @pl.when(i == 0)
        def _():
            dg_ref[...] = jnp.zeros_like(dg_ref)
            acc[...] = jnp.zeros_like(acc)

        dg_ref[...] += jnp.sum(dgrow, axis=0, keepdims=True)
        acc[0:nlru, :] += dot_tn(dlru, h)
        acc[nlru:nlru + att, :] += dot_nn(dqt, h)
        acc[nlru + att:, :] += dot_tn(dkv, h)

        @pl.when(i == nt - 1)
        def _():
            for p in range(N_CHIPS):
                for q in range(2):
                    o_ref[p, q, :lay.ih, :] = acc[pl.ds((2 * p + q) * lay.ih, lay.ih), :].astype(o_ref.dtype)
            o_ref[:, :, lay.ih:, :] = dwo_ref[...]

    row = pl.BlockSpec((tm, d), lambda i: (i, 0))
    vec = pl.BlockSpec((1, d), lambda i: (0, 0))
    return pl.pallas_call(
        body, name="mix_project_backward", grid=(nt,),
        out_shape=(SDS((s_len, d), F32), SDS((1, d), F32), SDS(gb.shape, gb.dtype)),
        in_specs=[row, vec, row, _cbm_spec(dxr.shape[0], tm), _cbm_spec(dgt.shape[0], tm),
                  pl.BlockSpec((att, tm), lambda i: (0, i)), pl.BlockSpec((tm, kvw), lambda i: (i, 0)),
                  pl.BlockSpec(dwout.shape, lambda i: (0, 0, 0, 0)), _w_spec(lay.fh, d, lay.MIX_BLK), ANY],
        out_specs=(row, vec, pl.BlockSpec((N_CHIPS, 2, lay.fh, d), lambda i: (0, 0, lay.MIX_BLK, 0))),
        scratch_shapes=[pltpu.VMEM((d_in, d), F32)],
        input_output_aliases={9: 2}, compiler_params=_cparams(),
    )(x, gain, dout, dxr, dgt, dqt, dkv, dwout, wfull, gb)


def _lru_gates(xc, wb_ref, pv_ref, direction):
    xcb = xc.astype(BF)
    r = jax.nn.sigmoid(dot_nn(xcb, wb_ref[2 * direction]) + pv_ref[1 + direction:2 + direction, :])
    i = jax.nn.sigmoid(dot_nn(xcb, wb_ref[2 * direction + 1]) + pv_ref[3 + direction:4 + direction, :])
    lam = pv_ref[5 + direction:6 + direction, :]
    sp = jnp.maximum(-lam, 0.0) + jnp.log(1.0 + jnp.exp(-jnp.abs(lam)))
    a = jnp.exp(-LRU_C * sp * r)
    mult = jnp.sqrt(1.0 - a * a)
    return xcb, r, i, a, mult, sp


def _conv_rows(xr, cv_ref, bias, n):
    acc = bias + cv_ref[0:1, :] * _shift_rows(xr, -2, n)
    for j in range(1, CONV_WIDTH):
        acc = acc + cv_ref[j:j + 1, :] * _shift_rows(xr, j - 2, n)
    return acc


def lru_forward(proj, cvec, pvec, wblk, lw, deps=(), ch=512):
    s_len = proj.shape[1]
    ncb = lw // LANE
    ch = min(ch, s_len)
    nchunk = s_len // ch

    def body(xr_ref, gt_ref, cv_ref, pv_ref, wb_ref, *rest):
        y_ref, hs_ref, xc_s, a_s, u_s, acum_s = rest[len(deps):]
        xc_s[...] = _conv_rows(xr_ref[...], cv_ref, pv_ref[0:1, :], s_len)
        for direction in range(2):
            def fill(ci, _):
                rows = pl.ds(pl.multiple_of(ci * ch, ch), ch)
                xc = xc_s[rows, :]
                _, _, i, a, mult, _ = _lru_gates(xc, wb_ref, pv_ref, direction)
                a_s[rows, :] = a
                u_s[rows, :] = mult * (i * xc)
                return 0

            lax.fori_loop(0, nchunk, fill, 0)
            _scan_rows(a_s, u_s, hs_ref.at[direction], acum_s, reverse=direction == 1)

        def out(ci, _):
            rows = pl.ds(pl.multiple_of(ci * ch, ch), ch)
            gl, _ = _gelu(gt_ref[rows, :])
            y_ref[rows, :] = gl * (hs_ref[0, rows, :] + hs_ref[1, rows, :])
            return 0

        lax.fori_loop(0, nchunk, out, 0)

    col = lambda off: pl.BlockSpec((None, s_len, LANE), lambda cb: (off + cb, 0, 0))
    return pl.pallas_call(
        body, name="lru_forward", grid=(ncb,),
        out_shape=(SDS((ncb, s_len, LANE), F32), SDS((2, ncb, s_len, LANE), F32)),
        in_specs=[col(0), col(ncb), pl.BlockSpec((8, LANE), lambda cb: (0, cb)), pl.BlockSpec((8, LANE), lambda cb: (0, cb)),
                  pl.BlockSpec((4, None, LANE, LANE), lambda cb: (0, cb, 0, 0))] + [ANY] * len(deps),
        out_specs=(col(0), pl.BlockSpec((2, None, s_len, LANE), lambda cb: (0, cb, 0, 0))),
        scratch_shapes=[pltpu.VMEM((s_len, LANE), F32)] * 4, compiler_params=_cparams(),
    )(proj, proj, cvec, pvec, wblk, *deps)


def lru_backward(proj, hs, dy, cvec, pvec, wblk, lw, ch=512):
    s_len = proj.shape[1]
    ncb = lw // LANE
    ch = min(ch, s_len)
    nchunk = s_len // ch

    def body(xr_ref, gt_ref, hs_ref, dy_ref, cv_ref, pv_ref, wb_ref, dxr_ref, dgt_ref, dcv_ref, dpv_ref, dwb_ref,
             xc_s, a_s, dh_s, lam_s, hp_s, dxc_s, acum_s):
        xr = xr_ref[...]
        xc_s[...] = _conv_rows(xr, cv_ref, pv_ref[0:1, :], s_len)
        dxc_s[...] = jnp.zeros_like(dxc_s)
        dpv_ref[...] = jnp.zeros_like(dpv_ref)
        dwb_ref[...] = jnp.zeros_like(dwb_ref)

        def head(ci, _):
            rows = pl.ds(pl.multiple_of(ci * ch, ch), ch)
            gt = gt_ref[rows, :]
            gl, t = _gelu(gt)
            dy = dy_ref[rows, :]
            dh_s[rows, :] = dy * gl
            dgt_ref[rows, :] = dy * (hs_ref[0, rows, :] + hs_ref[1, rows, :]) * _gelu_grad(gt, t)
            return 0

        lax.fori_loop(0, nchunk, head, 0)

        for direction in range(2):
            def fill(ci, _):
                rows = pl.ds(pl.multiple_of(ci * ch, ch), ch)
                _, _, _, a, _, _ = _lru_gates(xc_s[rows, :], wb_ref, pv_ref, direction)
                a_s[rows, :] = a
                return 0

            lax.fori_loop(0, nchunk, fill, 0)
            toward = 1 if direction == 0 else -1
            hp_s[...] = _shift_rows(a_s[...], toward, s_len)
            _scan_rows(hp_s, dh_s, lam_s, acum_s, reverse=direction == 0)
            hp_s[...] = _shift_rows(hs_ref[direction], -toward, s_len)

            def grads(ci, _):
                rows = pl.ds(pl.multiple_of(ci * ch, ch), ch)
                xc = xc_s[rows, :]
                xcb, r, i, a, mult, sp = _lru_gates(xc, wb_ref, pv_ref, direction)
                du = lam_s[rows, :]
                da = du * hp_s[rows, :]
                dmult = du * i * xc
                di = du * mult * xc
                dlog_a = (da - dmult * a / mult) * a
                dr = dlog_a * (-LRU_C * sp)
                dza = dr * r * (1.0 - r)
                dzx = di * i * (1.0 - i)
                dzab = dza.astype(BF)
                dzxb = dzx.astype(BF)
                dxc_s[rows, :] += (du * mult * i + dot_nt(dzab, wb_ref[2 * direction])
                                   + dot_nt(dzxb, wb_ref[2 * direction + 1]))
                dwb_ref[2 * direction] += dot_tn(xcb, dzab)
                dwb_ref[2 * direction + 1] += dot_tn(xcb, dzxb)
                dpv_ref[1 + direction:2 + direction, :] += jnp.sum(dza, axis=0, keepdims=True)
                dpv_ref[3 + direction:4 + direction, :] += jnp.sum(dzx, axis=0, keepdims=True)
                dpv_ref[5 + direction:6 + direction, :] += jnp.sum(dlog_a * (-LRU_C * r), axis=0, keepdims=True)
                return 0

            lax.fori_loop(0, nchunk, grads, 0)

        for direction in range(2):
            lam = pv_ref[5 + direction:6 + direction, :]
            dpv_ref[5 + direction:6 + direction, :] = dpv_ref[5 + direction:6 + direction, :] * (-jax.nn.sigmoid(-lam))
        dxc = dxc_s[...]
        dpv_ref[0:1, :] = jnp.sum(dxc, axis=0, keepdims=True)
        dxr = cv_ref[0:1, :] * _shift_rows(dxc, 2, s_len)
        for j in range(1, CONV_WIDTH):
            dxr = dxr + cv_ref[j:j + 1, :] * _shift_rows(dxc, 2 - j, s_len)
        dxr_ref[...] = dxr
        dcv_ref[...] = jnp.zeros_like(dcv_ref)
        for j in range(CONV_WIDTH):
            dcv_ref[j:j + 1, :] = jnp.sum(dxc * _shift_rows(xr, j - 2, s_len), axis=0, keepdims=True)

    col = lambda off: pl.BlockSpec((None, s_len, LANE), lambda cb: (off + cb, 0, 0))
    own = col(0)
    small = pl.BlockSpec((8, LANE), lambda cb: (0, cb))
    wspec = pl.BlockSpec((4, None, LANE, LANE), lambda cb: (0, cb, 0, 0))
    return pl.pallas_call(
        body, name="lru_backward", grid=(ncb,),
        out_shape=(SDS((ncb, s_len, LANE), F32), SDS((ncb, s_len, LANE), F32), SDS((8, lw), F32), SDS((8, lw), F32),
                   SDS(wblk.shape, F32)),
        in_specs=[col(0), col(ncb), pl.BlockSpec((2, None, s_len, LANE), lambda cb: (0, cb, 0, 0)), own, small, small, wspec],
        out_specs=(own, own, small, small, wspec),
        scratch_shapes=[pltpu.VMEM((s_len, LANE), F32)] * 7, compiler_params=_cparams(),
    )(proj, proj, hs, dy, cvec, pvec, wblk)


def _window_specs(s_len, first, width=None):
    nb = s_len // BLOCK
    where = (lambda n: jnp.maximum(n - 1, 0), lambda n: n, lambda n: jnp.minimum(n + 1, nb - 1))
    if width is None:
        return [pl.BlockSpec((None, BLOCK, LANE), lambda n, f=f: (first, f(n), 0)) for f in where]
    return [pl.BlockSpec((width, BLOCK), lambda n, f=f: (0, f(n))) for f in where]


def _stack_heads(v, kh):
    return jnp.concatenate([v[(kh * KV_GROUP + g) * HEAD_DIM:(kh * KV_GROUP + g + 1) * HEAD_DIM, :]
                            for g in range(KV_GROUP)], axis=1)


def _unstack_heads(ref, kh, v):
    for g in range(KV_GROUP):
        h = kh * KV_GROUP + g
        ref[h * HEAD_DIM:(h + 1) * HEAD_DIM, :] = v[:, g * BLOCK:(g + 1) * BLOCK]


def _key_exists(n, nb):
    j = lax.broadcasted_iota(jnp.int32, (3 * BLOCK, 1), 0)
    return ((n > 0) | (j >= BLOCK)) & ((n < nb - 1) | (j < 2 * BLOCK))


def _attn_probs(qs, kcat, bias_g, sink_g, key_ok):
    logits = jnp.where(key_ok, dot_nn(kcat, qs) + bias_g, NEG_INF)
    m = jnp.maximum(jnp.max(logits, axis=0, keepdims=True), sink_g)
    p = jnp.exp(logits - m)
    es = jnp.exp(sink_g - m)
    inv = 1.0 / (jnp.sum(p, axis=0, keepdims=True) + es)
    return p * inv, es * inv


def attention_forward(qt, proj, vt, bias, sink, kblk):
    att, s_len = qt.shape
    kvw = vt.shape[0]
    nb = s_len // BLOCK

    def body(q_ref, kp_ref, kc_ref, kn_ref, vp_ref, vc_ref, vn_ref, b_ref, s_ref, o_ref):
        n = pl.program_id(0)
        q = q_ref[...]
        key_ok = _key_exists(n, nb)
        kall = jnp.concatenate([kp_ref[...], kc_ref[...], kn_ref[...]], axis=0).astype(BF)
        vall = jnp.concatenate([vp_ref[...], vc_ref[...], vn_ref[...]], axis=1).astype(BF)
        for kh in range(N_KV_HEADS):
            qs = (_stack_heads(q, kh) * (HEAD_DIM ** -0.5)).astype(BF)
            p, _ = _attn_probs(qs, kall[:, kh * HEAD_DIM:(kh + 1) * HEAD_DIM], b_ref[kh], s_ref[kh, 0:1, :], key_ok)
            _unstack_heads(o_ref, kh, dot_nn(vall[kh * HEAD_DIM:(kh + 1) * HEAD_DIM, :], p.astype(BF)))

    blk = pl.BlockSpec((att, BLOCK), lambda n: (0, n))
    return pl.pallas_call(
        body, name="attention_forward", grid=(nb,), out_shape=SDS((att, s_len), F32),
        in_specs=[blk] + _window_specs(s_len, kblk) + _window_specs(s_len, 0, kvw)
        + [pl.BlockSpec(bias.shape, lambda n: (0, 0, 0)), pl.BlockSpec(sink.shape, lambda n: (0, 0, 0))],
        out_specs=blk, compiler_params=_cparams(),
    )(qt, proj, proj, proj, vt, vt, vt, bias, sink)


def attention_backward(qt, proj, y_att, dy, bias, sink, kblk):
    att, s_len = qt.shape
    nb = s_len // BLOCK
    kvw = N_KV_HEADS * HEAD_DIM

    def body(q_ref, kp_ref, kc_ref, kn_ref, vp_ref, vc_ref, vn_ref, o_ref, do_ref, b_ref, s_ref,
             dq_ref, dkv_ref, db_ref, ds_ref):
        n = pl.program_id(0)

        @pl.when(n == 0)
        def _():
            dkv_ref[...] = jnp.zeros_like(dkv_ref)
            db_ref[...] = jnp.zeros_like(db_ref)
            ds_ref[...] = jnp.zeros_like(ds_ref)

        q = q_ref[...]
        o = o_ref[...]
        do = do_ref[...]
        kall = jnp.concatenate([kp_ref[...], kc_ref[...], kn_ref[...]], axis=0).astype(BF)
        vall = jnp.concatenate([vp_ref[...], vc_ref[...], vn_ref[...]], axis=0).astype(BF)
        key_ok = _key_exists(n, nb)
        dks, dvs = [], []
        for kh in range(N_KV_HEADS):
            kcat = kall[:, kh * HEAD_DIM:(kh + 1) * HEAD_DIM]
            vcat = vall[:, kh * HEAD_DIM:(kh + 1) * HEAD_DIM]
            qs = (_stack_heads(q, kh) * (HEAD_DIM ** -0.5)).astype(BF)
            p, ps = _attn_probs(qs, kcat, b_ref[kh], s_ref[kh, 0:1, :], key_ok)
            dos = _stack_heads(do, kh)
            dosb = dos.astype(BF)
            delta = jnp.sum(dos * _stack_heads(o, kh), axis=0, keepdims=True)
            dlog = p * (dot_nn(vcat, dosb) - delta)
            dlogb = dlog.astype(BF)
            db_ref[kh] += dlog
            ds_ref[kh] += jnp.broadcast_to(-ps * delta, ds_ref.shape[1:])
            _unstack_heads(dq_ref, kh, dot_tn(kcat, dlogb) * (HEAD_DIM ** -0.5))
            dks.append(dot_nt(dlogb, qs))
            dvs.append(dot_nt(p.astype(BF), dosb))
        dkv = jnp.concatenate(dks + dvs, axis=1)
        starts = [jnp.maximum(n - 1, 0), n, jnp.minimum(n + 1, nb - 1)]
        for b, st in enumerate(starts):
            rows = pl.ds(pl.multiple_of(st * BLOCK, BLOCK), BLOCK)
            dkv_ref[rows, :] += dkv[b * BLOCK:(b + 1) * BLOCK, :]

    blk = pl.BlockSpec((att, BLOCK), lambda n: (0, n))
    whole = lambda a: pl.BlockSpec(a.shape, lambda n: (0, 0, 0))
    return pl.pallas_call(
        body, name="attention_backward", grid=(nb,),
        out_shape=(SDS((att, s_len), F32), SDS((s_len, 2 * kvw), F32), SDS(bias.shape, F32), SDS(sink.shape, F32)),
        in_specs=[blk] + _window_specs(s_len, kblk) + _window_specs(s_len, kblk + 1) + [blk, blk, whole(bias), whole(sink)],
        out_specs=(blk, pl.BlockSpec((s_len, 2 * kvw), lambda n: (0, 0)), whole(bias), whole(sink)),
        compiler_params=_cparams(),
    )(qt, proj, proj, proj, proj, proj, proj, y_att, dy, bias, sink)


def _rms_cols(x, g):
    rs = lax.rsqrt(jnp.mean(x * x, axis=0, keepdims=True) + EPS)
    xh = x * rs
    return xh, rs, xh * g


def _rms_cols_bwd(dy, xh, rs, g):
    dxh = dy * g
    dx = rs * (dxh - xh * jnp.mean(dxh * xh, axis=0, keepdims=True))
    return dx, dy * xh


def mix_output(x, y_rec, y_att, g_rec, g_att, wfull, lay, tm=512):
    s_len, d = x.shape
    tm = min(tm, s_len)
    lw = y_rec.shape[0] * LANE
    att = y_att.shape[0]

    def body(x_ref, yr_ref, ya_ref, gr_ref, ga_ref, w_ref, o_ref):
        _, _, nr = _rms(_join_lane_blocks(yr_ref), gr_ref[...])
        _, _, na = _rms_cols(ya_ref[...], ga_ref[...])
        w = w_ref[:, :, lay.ih:, :].reshape(d, d)
        o_ref[...] = x_ref[...] + dot_nn(nr.astype(BF), w[:lw]) + dot_tn(na.astype(BF), w[lw:])

    row = pl.BlockSpec((tm, d), lambda i: (i, 0))
    return pl.pallas_call(
        body, name="mix_output", grid=(s_len // tm,), out_shape=SDS((s_len, d), F32),
        in_specs=[row, _cbm_spec(lw // LANE, tm), pl.BlockSpec((att, tm), lambda i: (0, i)),
                  pl.BlockSpec((1, lw), lambda i: (0, 0)), pl.BlockSpec((att, 1), lambda i: (0, 0)),
                  _w_spec(lay.fh, d, lay.MIX_BLK)],
        out_specs=row, compiler_params=_cparams(),
    )(x, y_rec, y_att, g_rec, g_att, wfull)


def mix_output_backward(dout, y_rec, y_att, g_rec, g_att, wfull, lay, deps=(), tm=512):
    s_len, d = dout.shape
    tm = min(tm, s_len)
    lw = y_rec.shape[0] * LANE
    att = y_att.shape[0]
    nt = s_len // tm

    def body(do_ref, yr_ref, ya_ref, gr_ref, ga_ref, w_ref, *rest):
        dyr_ref, dya_ref, dgr_ref, dga_ref, o_ref, acc = rest[len(deps):]
        i = pl.program_id(0)
        gr = gr_ref[...]
        ga = ga_ref[...]
        xhr, rsr, nr = _rms(_join_lane_blocks(yr_ref), gr)
        xha, rsa, na = _rms_cols(ya_ref[...], ga)
        dob = do_ref[...].astype(BF)
        w = w_ref[:, :, lay.ih:, :].reshape(d, d)
        dyr, dgr_row = _rms_bwd(dot_nt(dob, w[:lw]), xhr, rsr, gr)
        dya, dga_col = _rms_cols_bwd(dot_nt(w[lw:], dob), xha, rsa, ga)
        for j, piece in enumerate(_lane_blocks(dyr)):
            dyr_ref[j] = piece
        dya_ref[...] = dya

        @pl.when(i == 0)
        def _():
            dgr_ref[...] = jnp.zeros_like(dgr_ref)
            dga_ref[...] = jnp.zeros_like(dga_ref)
            acc[...] = jnp.zeros_like(acc)

        dgr_ref[...] += jnp.sum(dgr_row, axis=0, keepdims=True)
        dga_ref[...] += jnp.sum(dga_col, axis=1, keepdims=True)
        acc[0:lw, :] += dot_tn(nr.astype(BF), dob)
        acc[lw:, :] += dot_nn(na.astype(BF), dob)

        @pl.when(i == nt - 1)
        def _():
            for p in range(N_CHIPS):
                for q in range(2):
                    o_ref[p, q] = acc[pl.ds((2 * p + q) * lay.oh, lay.oh), :].astype(o_ref.dtype)

    row = pl.BlockSpec((tm, d), lambda i: (i, 0))
    return pl.pallas_call(
        body, name="mix_output_backward", grid=(nt,),
        out_shape=(SDS(y_rec.shape, F32), SDS(y_att.shape, F32), SDS((1, lw), F32), SDS((att, 1), F32),
                   SDS((N_CHIPS, 2, lay.oh, d), BF)),
        in_specs=[row, _cbm_spec(lw // LANE, tm), pl.BlockSpec((att, tm), lambda i: (0, i)),
                  pl.BlockSpec((1, lw), lambda i: (0, 0)), pl.BlockSpec((att, 1), lambda i: (0, 0)),
                  _w_spec(lay.fh, d, lay.MIX_BLK)] + [ANY] * len(deps),
        out_specs=(_cbm_spec(lw // LANE, tm), pl.BlockSpec((att, tm), lambda i: (0, i)),
                   pl.BlockSpec((1, lw), lambda i: (0, 0)), pl.BlockSpec((att, 1), lambda i: (0, 0)),
                   pl.BlockSpec((N_CHIPS, 2, lay.oh, d), lambda i: (0, 0, 0, 0))),
        scratch_shapes=[pltpu.VMEM((d, d), F32)], compiler_params=_cparams(),
    )(dout, y_rec, y_att, g_rec, g_att, wfull, *deps)


def loss_head(x, gain, target, tm=512):
    s_len, d = x.shape
    tm = min(tm, s_len)

    def body(x_ref, g_ref, t_ref, dx_ref, dg_ref, loss_ref):
        g = g_ref[...]
        xh, rs, y = _rms(x_ref[...], g)
        err = y - t_ref[...]

        @pl.when(pl.program_id(0) == 0)
        def _():
            dg_ref[...] = jnp.zeros_like(dg_ref)
            loss_ref[...] = jnp.zeros_like(loss_ref)

        part = 0.5 * jnp.sum(jnp.mean(err * err, axis=-1, keepdims=True), axis=0, keepdims=True)
        loss_ref[...] += jnp.broadcast_to(part, loss_ref.shape)
        dx, dgrow = _rms_bwd(err * (1.0 / d), xh, rs, g)
        dx_ref[...] = dx
        dg_ref[...] += jnp.sum(dgrow, axis=0, keepdims=True)

    row = pl.BlockSpec((tm, d), lambda i: (i, 0))
    vec = pl.BlockSpec((1, d), lambda i: (0, 0))
    return pl.pallas_call(
        body, name="loss_head", grid=(s_len // tm,),
        out_shape=(SDS((s_len, d), F32), SDS((1, d), F32), SDS((8, LANE), F32)),
        in_specs=[row, vec, row], out_specs=(row, vec, pl.BlockSpec((8, LANE), lambda i: (0, 0))),
        compiler_params=_cparams(),
    )(x, gain, target)


def _adamw_update(w, g, m, v):
    m = ADAM_B1 * m + (1.0 - ADAM_B1) * g
    v = ADAM_B2 * v + (1.0 - ADAM_B2) * (g * g)
    m_hat = m / (1.0 - ADAM_B1 ** ADAM_STEP)
    v_hat = v / (1.0 - ADAM_B2 ** ADAM_STEP)
    return -ADAM_LR * (m_hat / (jnp.sqrt(v_hat) + ADAM_EPS) + ADAM_WD * w), m, v


def adamw(w, g, m, v, tr=512):
    rows, cols = w.shape
    tr = _row_chunk(rows, tr, 8)

    def body(w_ref, g_ref, m_ref, v_ref, d_ref, nm_ref, nv_ref):
        d_ref[...], nm_ref[...], nv_ref[...] = _adamw_update(w_ref[...], g_ref[...], m_ref[...], v_ref[...])

    blk = pl.BlockSpec((tr, cols), lambda i: (i, 0))
    return pl.pallas_call(
        body, name="adamw", grid=(rows // tr,), out_shape=(SDS(w.shape, F32),) * 3,
        in_specs=[blk] * 4, out_specs=(blk,) * 3, compiler_params=_cparams(),
    )(w, g, m, v)


def adamw_layer(gf, blk, row_off, n_half, l, w, m, v, outs, deps=()):
    fh = gf.shape[1] // 7
    d = gf.shape[2]
    nd = len(deps)

    def body(gf_ref, w_ref, m_ref, v_ref, *rest):
        g_ref, d_ref, nm_ref, nv_ref = rest[4 + nd:]
        g = gf_ref[row_off:row_off + n_half, :]
        g_ref[...] = g
        d_ref[...], nm_ref[...], nv_ref[...] = _adamw_update(w_ref[...], g, m_ref[...], v_ref[...])

    gspec = pl.BlockSpec((None, fh, d), lambda h: (h, blk, 0))
    wspec = pl.BlockSpec((None, n_half, d), lambda h: (l, h, 0))
    return pl.pallas_call(
        body, name="adamw_layer", grid=(2,), out_shape=tuple(SDS(o.shape, o.dtype) for o in outs),
        in_specs=[gspec, wspec, wspec, wspec] + [ANY] * (4 + nd), out_specs=(wspec,) * 4,
        input_output_aliases={4 + i: i for i in range(4)}, compiler_params=_cparams(),
    )(gf, w, m, v, *outs, *deps)


def pack_weight(pos, land, blk, l, w, extra=None, deps=()):
    fh, d = land.shape[2] // 7, land.shape[3]
    nd = len(deps)

    def body(pos_ref, w_ref, *rest):
        o_ref = rest[-1]
        a = w_ref[...].astype(BF)
        n = a.shape[0] // 2
        for h in range(2):
            o_ref[h, 0:n, :] = a[h * n:(h + 1) * n]
        if extra is not None:
            b = rest[0][...].astype(BF)
            nb = b.shape[0] // 2
            for h in range(2):
                o_ref[h, n:n + nb, :] = b[h * nb:(h + 1) * nb]

    def whole(a):
        return pl.BlockSpec((None,) + a.shape[1:], lambda i, p: (l, 0, 0))

    ins = [w] + ([extra] if extra is not None else [])
    return pl.pallas_call(
        body, name="pack_weight", out_shape=SDS(land.shape, land.dtype),
        grid_spec=pltpu.PrefetchScalarGridSpec(
            num_scalar_prefetch=1, grid=(1,),
            in_specs=[whole(a) for a in ins] + [ANY] * (1 + nd),
            out_specs=pl.BlockSpec((None, 2, fh, d), lambda i, p: (p[0], 0, blk, 0))),
        input_output_aliases={1 + len(ins): 0}, compiler_params=_cparams(),
    )(pos, *ins, land, *deps)


def _pack_rows(arrays, width):
    flat = jnp.concatenate([a.reshape(-1).astype(F32) for a in arrays])
    rows = -(-flat.shape[0] // (8 * width)) * 8
    return jnp.pad(flat, (0, rows * width - flat.shape[0])).reshape(rows, width)


def _unpack_rows(buf, shapes):
    flat = buf.reshape(-1)
    out, off = [], 0
    for shp in shapes:
        n = int(np.prod(shp))
        out.append(flat[off:off + n].reshape(shp))
        off += n
    return out


def _t5_buckets(rel):
    half = N_BUCKETS // 2
    max_exact = half // 2
    ret = (rel > 0).astype(jnp.int32) * half
    n = jnp.abs(rel)
    n_f = jnp.maximum(n, 1).astype(F32)
    large = max_exact + (jnp.log(n_f / max_exact) / math.log(MAX_DISTANCE / max_exact) * (half - max_exact)).astype(jnp.int32)
    large = jnp.minimum(large, half - 1)
    return ret + jnp.where(n < max_exact, n, large)


def _band_buckets():
    t = jnp.arange(BLOCK)[:, None]
    j = jnp.arange(3 * BLOCK)[None, :]
    rel = j - BLOCK - t
    return _t5_buckets(rel), jnp.abs(rel) <= WINDOW


def _block_diag_pairs(w):
    depth, two, nblk, bw, _ = w.shape
    pairs = w.reshape(depth, two, nblk // 2, 2, bw, bw)
    z = jnp.zeros_like(pairs[:, :, :, 0])
    top = jnp.concatenate([pairs[:, :, :, 0], z], axis=-1)
    bot = jnp.concatenate([z, pairs[:, :, :, 1]], axis=-1)
    return jnp.concatenate([top, bot], axis=-2)


def _diag_blocks(dw):
    bw = dw.shape[-1] // 2
    a = dw[:, :, :bw, :bw]
    b = dw[:, :, bw:, bw:]
    return jnp.stack([a, b], axis=2).reshape(dw.shape[0], 2 * dw.shape[1], bw, bw)


def kernel(x, ffn1_norm, ffn1_w_gate, ffn1_w_up, ffn1_w_down, mix_norm, w_in, conv_w, conv_b, lru_w_a, lru_b_a, lru_w_x, lru_b_x, lru_lambda, attn_sink, rel_bias, lru_out_norm, attn_out_norm, w_out, ffn2_norm, ffn2_w_gate, ffn2_w_up, ffn2_w_down, final_norm, loss_target, m_ffn1_norm, m_ffn1_w_gate, m_ffn1_w_up, m_ffn1_w_down, m_mix_norm, m_w_in, m_conv_w, m_conv_b, m_lru_w_a, m_lru_b_a, m_lru_w_x, m_lru_b_x, m_lru_lambda, m_attn_sink, m_rel_bias, m_lru_out_norm, m_attn_out_norm, m_w_out, m_ffn2_norm, m_ffn2_w_gate, m_ffn2_w_up, m_ffn2_w_down, m_final_norm, v_ffn1_norm, v_ffn1_w_gate, v_ffn1_w_up, v_ffn1_w_down, v_mix_norm, v_w_in, v_conv_w, v_conv_b, v_lru_w_a, v_lru_b_a, v_lru_w_x, v_lru_b_x, v_lru_lambda, v_attn_sink, v_rel_bias, v_lru_out_norm, v_attn_out_norm, v_w_out, v_ffn2_norm, v_ffn2_w_gate, v_ffn2_w_up, v_ffn2_w_down, v_final_norm):
    depth, d = ffn1_norm.shape
    d_ff = N_CHIPS * ffn1_w_gate.shape[2]
    d_in = N_CHIPS * w_in.shape[2]
    lw = conv_b.shape[1]
    att = N_HEADS * HEAD_DIM
    lay = Layout(d, d_ff, d_in)
    k_chip = 2 * lax.axis_index("x") + lax.axis_index("y")
    pos = jnp.stack([k_chip, lax.axis_index("c")]).astype(jnp.int32)

    def rows_major(a):
        return jnp.swapaxes(a, 1, 2)

    mats = (rows_major(ffn1_w_gate), rows_major(ffn1_w_up), ffn1_w_down,
            rows_major(ffn2_w_gate), rows_major(ffn2_w_up), ffn2_w_down)

    def pack_layer(l, deps=()):
        land = lax.empty((N_CHIPS, 2, lay.rows, d), BF)
        for m, a in enumerate(mats):
            land = pack_weight(pos, land, m, l, a, deps=deps if m == 0 else ())
        return pack_weight(pos, land, lay.MIX_BLK, l, rows_major(w_in), extra=w_out)

    def gather_start(l, land):
        return split_start(f"gather_start_{l}", [land], 3, gather_plan)

    def gather_wait(l, started, after):
        ssem, rsem, bufs, _ = started
        return split_wait(f"gather_wait_{l}", ssem, rsem, bufs, after, gather_plan)

    sharded_small = (conv_w, lru_b_a, lru_b_x, lru_lambda)
    sshard = jnp.concatenate([a.reshape(-1, LANE) for a in sharded_small], axis=0)
    sfull = gather_small(sshard)
    small_full, off = [], 0
    for a in sharded_small:
        r = a.shape[0] * a.shape[1]
        piece = sfull[:, off:off + r].reshape((N_CHIPS,) + a.shape)
        small_full.append(jnp.moveaxis(piece, 0, 2).reshape(a.shape[0], a.shape[1], N_CHIPS * LANE))
        off += r
    conv_w_f, b_a_f, b_x_f, lam_f = small_full

    zrow = jnp.zeros((1, lw), F32)
    wblk_a = _block_diag_pairs(lru_w_a)
    wblk_x = _block_diag_pairs(lru_w_x)
    buckets, in_band = _band_buckets()
    onehot = (buckets.reshape(-1)[:, None] == jnp.arange(N_BUCKETS)[None, :]).astype(F32)
    bias = jnp.dot(rel_bias.T, onehot.T, precision=lax.Precision.HIGHEST).reshape(N_HEADS, BLOCK, 3 * BLOCK)
    bias = jnp.where(in_band[None], bias, NEG_INF)
    bias = bias.reshape(N_KV_HEADS, KV_GROUP, BLOCK, 3 * BLOCK).transpose(0, 3, 1, 2).reshape(N_KV_HEADS, 3 * BLOCK, KV_GROUP * BLOCK)
    kblk = 2 * lw // LANE

    def layer_small(l):
        cvec = jnp.concatenate([conv_w_f[l], jnp.zeros((8 - CONV_WIDTH, lw), F32)], axis=0)
        pvec = jnp.concatenate([conv_b[l][None], b_a_f[l], b_x_f[l], lam_f[l], zrow], axis=0)
        wblk = jnp.stack([wblk_a[l, 0], wblk_x[l, 0], wblk_a[l, 1], wblk_x[l, 1]]).astype(BF)
        sink = jnp.broadcast_to(jnp.repeat(attn_sink[l], BLOCK).reshape(N_KV_HEADS, 1, KV_GROUP * BLOCK),
                                (N_KV_HEADS, 8, KV_GROUP * BLOCK))
        return cvec, pvec, wblk, sink

    xs = x[0]
    wfull = [None] * depth
    first = gather_start(0, pack_layer(0, deps=(sfull,)))
    lands = {l: pack_layer(l, deps=(first[3],)) for l in range(1, depth)}
    wfull[0], = exchange_now("gather_handover_0", gather_wait(0, first, [xs] + list(lands.values())), 3, handover_plan)
    started = gather_start(1, lands[1]) if depth > 1 else None
    saved = []
    for l in range(depth):
        cvec, pvec, wblk, sink = layer_small(l)
        deps = (started[3],) if started is not None else ()
        x1, gate1, up1 = ffn_forward(xs, ffn1_norm[l][None], wfull[l], lay, 0, deps=deps)
        proj, qt, vt = mix_project(x1, mix_norm[l][None], wfull[l], lay, lw, att)
        y_rec, hs = lru_forward(proj, cvec, pvec, wblk, lw)
        y_att = attention_forward(qt, proj, vt, bias, sink, kblk)
        x2 = mix_output(x1, y_rec, y_att, lru_out_norm[l][None], attn_out_norm[l][:, None], wfull[l], lay)
        deps, handover = (), None
        if l + 1 < depth:
            land, = gather_wait(l + 1, started, [x2])
            started = gather_start(l + 2, lands[l + 2]) if l + 2 < depth else None
            handover = split_start(f"gather_handover_start_{l + 1}", [land], 3, handover_plan)
            deps = (handover[3],) + ((started[3],) if started is not None else ())
        x3, gate2, up2 = ffn_forward(x2, ffn2_norm[l][None], wfull[l], lay, 1, deps=deps)
        saved.append((xs, x1, x2, proj, qt, y_rec, hs, y_att, (gate1, up1), (gate2, up2)))
        xs = x3
        if handover is not None:
            wfull[l + 1], = split_wait(f"gather_handover_wait_{l + 1}", handover[0], handover[1], handover[2], [x3],
                                       handover_plan)

    dx, d_final, loss_tile = loss_head(xs, final_norm[None], loss_target[0])
    loss = lax.psum(loss_tile[0, 0], ("x", "y", "c"))

    layer_names = ["ffn1_norm", "mix_norm", "conv_w", "conv_b", "lru_w_a", "lru_b_a", "lru_w_x", "lru_b_x", "lru_lambda",
                   "attn_sink", "lru_out_norm", "attn_out_norm", "ffn2_norm"]
    dbias_total = jnp.zeros(bias.shape, F32)

    def ffn_back(xin, gain, dout, pre, gb, l, which, deps=()):
        dxo, dg, dgate, dup, act, h, df = ffn_backward_dx(xin, gain, dout, *pre, wfull[l], lay, which, deps=deps)
        gb = weight_grad_tn(dgate, h, gb, lay, 3 * which + 0)
        gb = weight_grad_tn(dup, h, gb, lay, 3 * which + 1)
        gb = weight_grad_tn(act, df, gb, lay, 3 * which + 2)
        return dxo, dg[0], gb

    def pair_start(l, gb, sb):
        lands = [lax.empty((N_CHIPS,) + gb.shape[2:], gb.dtype), lax.empty(sb.shape, sb.dtype)]
        return split_start(f"pair_start_{l}", [gb, sb] + lands, N_CHIPS + 1, pair_plan)

    def reduce_start(l, paired, after):
        gb, sb, p1, sp1 = split_wait(f"pair_wait_{l}", paired[0], paired[1], paired[2], after, pair_plan)
        cs = pair_sum(pos, gb, p1)
        ss = small_pair_sum(sb, sp1)
        lands = [lax.empty((3,) + cs.shape[1:], cs.dtype), lax.empty((N_CHIPS,) + ss.shape, ss.dtype)]
        return split_start(f"reduce_start_{l}", [cs, ss] + lands, 6, reduce_plan)

    def reduce_finish(l, started, after):
        ssem, rsem, bufs, _ = started
        cs, ss, p3, sp3 = split_wait(f"reduce_wait_{l}", ssem, rsem, bufs, after, reduce_plan)
        return chip_sum(pos, cs, p3), small_chip_sum(pos, ss, sp3)

    gf = [None] * depth
    small_sums = [None] * depth
    small_shapes = [None] * depth
    paired = None
    in_flight = None
    finals = {}
    tokens = []
    for l in reversed(range(depth)):
        x0, x1, x2, proj, qt, y_rec, hs, y_att, pre1, pre2 = saved[l]
        cvec, pvec, wblk, sink = layer_small(l)
        gb = lax.empty((N_CHIPS, 2, lay.rows, d), BF)
        part = {}
        dx, part["ffn2_norm"], gb = ffn_back(x2, ffn2_norm[l][None], dx, pre2, gb, l, 1, deps=tuple(tokens))
        deps = ()
        if paired is not None:
            in_flight = (paired[0], reduce_start(paired[0], paired[1], [dx, gb]))
            deps = (in_flight[1][3],)
        dyr, dya, dgr, dga, dwout = mix_output_backward(dx, y_rec, y_att, lru_out_norm[l][None], attn_out_norm[l][:, None],
                                                        wfull[l], lay, deps=deps)
        part["lru_out_norm"] = dgr[0]
        part["attn_out_norm"] = dga[:, 0]
        dq, dkv, dbias, dsink = attention_backward(qt, proj, y_att, dya, bias, sink, kblk)
        dbias_total = dbias_total + dbias
        part["attn_sink"] = jnp.sum(dsink[:, 0, :].reshape(N_HEADS, BLOCK), axis=1)
        dxr, dgt, dcv, dpv, dwb = lru_backward(proj, hs, dyr, cvec, pvec, wblk, lw)
        part["conv_w"] = dcv[:CONV_WIDTH]
        part["conv_b"] = dpv[0]
        part["lru_b_a"] = dpv[1:3]
        part["lru_b_x"] = dpv[3:5]
        part["lru_lambda"] = dpv[5:7]
        part["lru_w_a"] = _diag_blocks(jnp.stack([dwb[0], dwb[2]]))
        part["lru_w_x"] = _diag_blocks(jnp.stack([dwb[1], dwb[3]]))
        dx, dgm, gb = mix_project_backward(x1, mix_norm[l][None], dx, dxr, dgt, dq, dkv, dwout, wfull[l], gb, lay)
        part["mix_norm"] = dgm[0]
        dx, part["ffn1_norm"], gb = ffn_back(x0, ffn1_norm[l][None], dx, pre1, gb, l, 0)
        pieces = [part[n] for n in layer_names]
        if l == 0:
            dbias_heads = dbias_total.reshape(N_KV_HEADS, 3 * BLOCK, KV_GROUP, BLOCK).transpose(0, 2, 3, 1)
            d_rel_bias = jnp.dot(dbias_heads.reshape(N_HEADS, -1), onehot, precision=lax.Precision.HIGHEST).T
            pieces += [d_rel_bias, d_final[0]]
        small_shapes[l] = [p.shape for p in pieces]
        paired = (l, pair_start(l, gb, _pack_rows(pieces, 1024)))
        tokens = [paired[1][3]]
        if in_flight is not None:
            above = in_flight[0]
            half, small_sums[above] = reduce_finish(above, in_flight[1], [dx])
            finals[above] = split_start(f"final_start_{above}", [half], 1, final_plan)
            tokens.append(finals[above][3])
            in_flight = None
    grad_x = dx[None]

    weights = dict(ffn1_norm=ffn1_norm, ffn1_w_gate=ffn1_w_gate, ffn1_w_up=ffn1_w_up, ffn1_w_down=ffn1_w_down, mix_norm=mix_norm, w_in=w_in, conv_w=conv_w, conv_b=conv_b, lru_w_a=lru_w_a, lru_b_a=lru_b_a, lru_w_x=lru_w_x, lru_b_x=lru_b_x, lru_lambda=lru_lambda, attn_sink=attn_sink, rel_bias=rel_bias, lru_out_norm=lru_out_norm, attn_out_norm=attn_out_norm, w_out=w_out, ffn2_norm=ffn2_norm, ffn2_w_gate=ffn2_w_gate, ffn2_w_up=ffn2_w_up, ffn2_w_down=ffn2_w_down, final_norm=final_norm)
    m_in = dict(ffn1_norm=m_ffn1_norm, ffn1_w_gate=m_ffn1_w_gate, ffn1_w_up=m_ffn1_w_up, ffn1_w_down=m_ffn1_w_down, mix_norm=m_mix_norm, w_in=m_w_in, conv_w=m_conv_w, conv_b=m_conv_b, lru_w_a=m_lru_w_a, lru_b_a=m_lru_b_a, lru_w_x=m_lru_w_x, lru_b_x=m_lru_b_x, lru_lambda=m_lru_lambda, attn_sink=m_attn_sink, rel_bias=m_rel_bias, lru_out_norm=m_lru_out_norm, attn_out_norm=m_attn_out_norm, w_out=m_w_out, ffn2_norm=m_ffn2_norm, ffn2_w_gate=m_ffn2_w_gate, ffn2_w_up=m_ffn2_w_up, ffn2_w_down=m_ffn2_w_down, final_norm=m_final_norm)
    v_in = dict(ffn1_norm=v_ffn1_norm, ffn1_w_gate=v_ffn1_w_gate, ffn1_w_up=v_ffn1_w_up, ffn1_w_down=v_ffn1_w_down, mix_norm=v_mix_norm, w_in=v_w_in, conv_w=v_conv_w, conv_b=v_conv_b, lru_w_a=v_lru_w_a, lru_b_a=v_lru_b_a, lru_w_x=v_lru_w_x, lru_b_x=v_lru_b_x, lru_lambda=v_lru_lambda, attn_sink=v_attn_sink, rel_bias=v_rel_bias, lru_out_norm=v_lru_out_norm, attn_out_norm=v_attn_out_norm, w_out=v_w_out, ffn2_norm=v_ffn2_norm, ffn2_w_gate=v_ffn2_w_gate, ffn2_w_up=v_ffn2_w_up, ffn2_w_down=v_ffn2_w_down, final_norm=v_final_norm)
    order = list(weights)
    large = [(name, m, 0, lay.fh, m % 3 != 2) for m, name in
             enumerate(("ffn1_w_gate", "ffn1_w_up", "ffn1_w_down", "ffn2_w_gate", "ffn2_w_up", "ffn2_w_down"))]
    large += [("w_in", lay.MIX_BLK, 0, lay.ih, True), ("w_out", lay.MIX_BLK, lay.ih, lay.oh, False)]
    as_rows = {name: [rows_major(src[name]) if flip else src[name] for src in (weights, m_in, v_in)]
               for name, _, _, _, flip in large}
    stacked = {name: tuple(lax.empty(as_rows[name][0].shape, F32) for _ in range(4)) for name, *_ in large}

    def adamw_large(l, deps=()):
        for i, (name, blk, row_off, n_half, _) in enumerate(large):
            stacked[name] = adamw_layer(gf[l], blk, row_off, n_half, l, *as_rows[name], stacked[name],
                                        deps=deps if i == 0 else ())

    last = paired[0]
    crossing = reduce_start(last, paired[1], [dx])
    for l in sorted(finals):
        gf[l], = split_wait(f"final_wait_{l}", finals[l][0], finals[l][1], finals[l][2], [crossing[3]], final_plan)
        adamw_large(l, deps=(crossing[3],))
    ready = [buf for name, *_ in large for buf in stacked[name]] if depth > 1 else []
    half, small_sums[last] = reduce_finish(last, crossing, [dx] + ready)
    gf[last], = exchange_now(f"final_now_{last}", [half], 1, final_plan)
    adamw_large(last)

    per_layer = [_unpack_rows(small_sums[l], small_shapes[l]) for l in range(depth)]
    grads = {n: jnp.stack([per_layer[l][i] for l in range(depth)]) for i, n in enumerate(layer_names)}
    grads["rel_bias"], grads["final_norm"] = per_layer[0][len(layer_names):]
    for name in ("conv_w", "lru_b_a", "lru_b_x", "lru_lambda"):
        grads[name] = lax.dynamic_slice_in_dim(grads[name], k_chip * LANE, LANE, axis=2)
    delta, new_m, new_v = {}, {}, {}
    for name, _, _, _, flip in large:
        grads[name], delta[name], new_m[name], new_v[name] = [rows_major(a) if flip else a for a in stacked[name]]
    small = [n for n in order if n not in stacked]
    packed = [_pack_rows([src[n] for n in small], 1024) for src in (weights, grads, m_in, v_in)]
    outs = adamw(*packed)
    shapes = [weights[n].shape for n in small]
    for dst, buf in zip((delta, new_m, new_v), outs):
        dst.update(zip(small, _unpack_rows(buf, shapes)))

    return (loss, grad_x, *[grads[n] for n in order], *[delta[n] for n in order],
            *[new_m[n] for n in order], *[new_v[n] for n in order])
```

```python
import functools
import math

import jax
import jax.numpy as jnp
import numpy as np
from jax import lax
from jax.experimental import pallas as pl
from jax.experimental.pallas import tpu as pltpu

BF = jnp.bfloat16
F32 = jnp.float32
SDS = jax.ShapeDtypeStruct
MESH = pl.DeviceIdType.MESH
ANY = pl.BlockSpec(memory_space=pl.ANY)

N_CHIPS = 4
N_HEADS = 8
N_KV_HEADS = 2
KV_GROUP = N_HEADS // N_KV_HEADS
HEAD_DIM = 64
BLOCK = 128
WINDOW = 128
N_BUCKETS = 32
MAX_DISTANCE = 128
LRU_C = 8.0
CONV_WIDTH = 4
LANE = 128
SUBLANES = 8
SCAN_CHAINS = 8
EPS = 1e-6
FFN_RES = 0.5
NEG_INF = -1e30
ADAM_LR = 0.001
ADAM_B1 = 0.9
ADAM_B2 = 0.999
ADAM_EPS = 1e-08
ADAM_WD = 0.01
ADAM_STEP = 10
VMEM_LIMIT = 60000 * 1024
GELU_C = math.sqrt(2.0 / math.pi)


def dot_nn(a, b):
    return lax.dot_general(a, b, (((1,), (0,)), ((), ())), preferred_element_type=F32)


def dot_nt(a, b):
    return lax.dot_general(a, b, (((1,), (1,)), ((), ())), preferred_element_type=F32)


def dot_tn(a, b):
    return lax.dot_general(a, b, (((0,), (0,)), ((), ())), preferred_element_type=F32)


def _cparams(**kw):
    return pltpu.CompilerParams(vmem_limit_bytes=VMEM_LIMIT, **kw)


class Layout:
    MIX_BLK = 6

    def __init__(self, d_model, d_ff, d_in):
        self.fh = d_ff // (2 * N_CHIPS)
        self.ih = d_in // (2 * N_CHIPS)
        self.oh = d_model // (2 * N_CHIPS)
        assert self.ih + self.oh == self.fh, "w_in^T and w_out rows must fill one ffn-sized block"
        self.rows = 7 * self.fh


def _row_chunk(rows, target, step=16):
    best = rows
    for c in range(step, min(rows, target) + 1, step):
        if rows % c == 0:
            best = c
    return best


def _mesh_pos():
    return lax.axis_index("x"), lax.axis_index("y"), lax.axis_index("c")


def _rcopy(src, dst, ssem, rsem, dev):
    return pltpu.make_async_remote_copy(src_ref=src, dst_ref=dst, send_sem=ssem, recv_sem=rsem,
                                        device_id=dev, device_id_type=MESH)


HBM = pl.BlockSpec(memory_space=pltpu.HBM)
SEM = pl.BlockSpec(memory_space=pltpu.SEMAPHORE)
DATAFLOW = pltpu.SideEffectType.DATAFLOW_SIDE_EFFECTING


def _chip_peers():
    x, y, c = _mesh_pos()
    peers = [(1 - x, y), (x, 1 - y), (1 - x, 1 - y)]
    return x, y, c, 2 * x + y, [(px, py, 2 * px + py) for px, py in peers]


def split_start(name, bufs, n, plan):
    nb = len(bufs)

    def body(*refs):
        sends, _ = plan(refs[:nb], refs[nb], refs[nb + 1])
        for cp in sends:
            cp.start()
        refs[-1][...] = jnp.zeros_like(refs[-1])

    out = pl.pallas_call(
        body, name=name,
        out_shape=(pltpu.SemaphoreType.DMA((n,)), pltpu.SemaphoreType.DMA((n,)),
                   *[pltpu.HBM(b.shape, b.dtype) for b in bufs], SDS((8, LANE), F32)),
        in_specs=[HBM] * nb, out_specs=(SEM, SEM, *([HBM] * nb), pl.BlockSpec(memory_space=pltpu.VMEM)),
        input_output_aliases={i: 2 + i for i in range(nb)},
        compiler_params=pltpu.CompilerParams(has_side_effects=DATAFLOW),
    )(*[pltpu.with_memory_space_constraint(b, pltpu.HBM) for b in bufs])
    return out[0], out[1], list(out[2:2 + nb]), out[-1]


def split_wait(name, ssem, rsem, bufs, after, plan):
    nb = len(bufs)

    def body(*refs):
        sends, recvs = plan(refs[:nb], refs[nb], refs[nb + 1])
        for cp in recvs:
            cp.wait_recv()
        for cp in sends:
            cp.wait_send()

    out = pl.pallas_call(
        body, name=name, out_shape=tuple(pltpu.HBM(b.shape, b.dtype) for b in bufs),
        in_specs=[HBM] * nb + [SEM, SEM] + [ANY] * len(after), out_specs=tuple([HBM] * nb),
        input_output_aliases={i: i for i in range(nb)},
        compiler_params=pltpu.CompilerParams(has_side_effects=DATAFLOW),
    )(*bufs, ssem, rsem, *after)
    return list(out)


def gather_plan(refs, ssem, rsem, rows=None):
    land_ref, = refs
    _, _, c, k, peers = _chip_peers()
    part = (lambda a: a) if rows is None else (lambda a: a.at[pl.ds(rows[0], rows[1])])
    sends = [_rcopy(part(land_ref.at[k, c]), part(land_ref.at[k, c]), ssem.at[j], rsem.at[j], (px, py, c))
             for j, (px, py, _) in enumerate(peers)]
    recvs = [_rcopy(part(land_ref.at[kp, c]), part(land_ref.at[kp, c]), ssem.at[j], rsem.at[j], (px, py, c))
             for j, (px, py, kp) in enumerate(peers)]
    return sends, recvs


def reduce_plan(refs, ssem, rsem):
    cs_ref, ss_ref, p3_ref, sp3_ref = refs
    _, _, c, k, peers = _chip_peers()
    sends, recvs = [], []
    for j, (px, py, kp) in enumerate(peers):
        sends.append(_rcopy(cs_ref.at[kp], p3_ref.at[j], ssem.at[j], rsem.at[j], (px, py, c)))
        recvs.append(_rcopy(cs_ref.at[kp], p3_ref.at[j], ssem.at[j], rsem.at[j], (px, py, c)))
        sends.append(_rcopy(ss_ref, sp3_ref.at[k], ssem.at[3 + j], rsem.at[3 + j], (px, py, c)))
        recvs.append(_rcopy(ss_ref, sp3_ref.at[kp], ssem.at[3 + j], rsem.at[3 + j], (px, py, c)))
    return sends, recvs


def gather_small(sshard):
    def body(s_ref, sf_ref, lsem, ssem, rsem):
        _, _, c, k, peers = _chip_peers()
        own = pltpu.make_async_copy(s_ref, sf_ref.at[k], lsem)
        own.start()
        sends = [_rcopy(s_ref, sf_ref.at[k], ssem.at[j], rsem.at[j], (px, py, c)) for j, (px, py, _) in enumerate(peers)]
        recvs = [_rcopy(s_ref, sf_ref.at[kp], ssem.at[j], rsem.at[j], (px, py, c)) for j, (px, py, kp) in enumerate(peers)]
        for cp in sends:
            cp.start()
        for cp in recvs:
            cp.wait_recv()
        for cp in sends:
            cp.wait_send()
        own.wait()

    return pl.pallas_call(
        body, name="gather_small", out_shape=SDS((N_CHIPS,) + sshard.shape, sshard.dtype),
        in_specs=[ANY], out_specs=ANY,
        scratch_shapes=[pltpu.SemaphoreType.DMA, pltpu.SemaphoreType.DMA((3,)), pltpu.SemaphoreType.DMA((3,))],
    )(sshard)


def exchange_now(name, bufs, n, plan):
    nb = len(bufs)

    def body(*refs):
        sends, recvs = plan(refs[nb:2 * nb], refs[2 * nb], refs[2 * nb + 1])
        for cp in sends:
            cp.start()
        for cp in recvs:
            cp.wait_recv()
        for cp in sends:
            cp.wait_send()

    return list(pl.pallas_call(
        body, name=name, out_shape=tuple(SDS(b.shape, b.dtype) for b in bufs),
        in_specs=[ANY] * nb, out_specs=tuple([ANY] * nb), input_output_aliases={i: i for i in range(nb)},
        scratch_shapes=[pltpu.SemaphoreType.DMA((n,)), pltpu.SemaphoreType.DMA((n,))],
    )(*bufs))


def handover_plan(refs, ssem, rsem, rows=None):
    land_ref, = refs
    x, y, c, _, peers = _chip_peers()
    sib = (x, y, 1 - c)
    part = (lambda a: a) if rows is None else (lambda a: a.at[pl.ds(rows[0], rows[1])])
    sends = [_rcopy(part(land_ref.at[kp, c]), part(land_ref.at[kp, c]), ssem.at[j], rsem.at[j], sib)
             for j, (_, _, kp) in enumerate(peers)]
    recvs = [_rcopy(part(land_ref.at[kp, 1 - c]), part(land_ref.at[kp, 1 - c]), ssem.at[j], rsem.at[j], sib)
             for j, (_, _, kp) in enumerate(peers)]
    return sends, recvs


def pair_plan(refs, ssem, rsem):
    gb_ref, sb_ref, p_ref, sp_ref = refs
    x, y, c = _mesh_pos()
    sib = (x, y, 1 - c)
    n = gb_ref.shape[0]
    copies = [_rcopy(gb_ref.at[kk, 1 - c], p_ref.at[kk], ssem.at[kk], rsem.at[kk], sib) for kk in range(n)]
    copies.append(_rcopy(sb_ref, sp_ref, ssem.at[n], rsem.at[n], sib))
    return copies, copies


def final_plan(refs, ssem, rsem):
    gf_ref, = refs
    x, y, c = _mesh_pos()
    sib = (x, y, 1 - c)
    return ([_rcopy(gf_ref.at[c], gf_ref.at[c], ssem.at[0], rsem.at[0], sib)],
            [_rcopy(gf_ref.at[1 - c], gf_ref.at[1 - c], ssem.at[0], rsem.at[0], sib)])


def pair_sum(pos, gb, p1):
    n, _, rh, d = gb.shape
    cr = _row_chunk(rh, 1280)

    def body(pos_ref, a_ref, b_ref, o_ref):
        o_ref[...] = (a_ref[...].astype(F32) + b_ref[...].astype(F32)).astype(o_ref.dtype)

    return pl.pallas_call(
        body, name="pair_sum", out_shape=SDS((n, rh, d), gb.dtype),
        grid_spec=pltpu.PrefetchScalarGridSpec(
            num_scalar_prefetch=1, grid=(n, rh // cr),
            in_specs=[pl.BlockSpec((None, None, cr, d), lambda kk, r, pos: (kk, pos[1], r, 0)),
                      pl.BlockSpec((None, cr, d), lambda kk, r, pos: (kk, r, 0))],
            out_specs=pl.BlockSpec((None, cr, d), lambda kk, r, pos: (kk, r, 0))),
        compiler_params=_cparams(),
    )(pos, gb, p1)


def chip_sum(pos, cs, p3):
    n, rh, d = cs.shape
    cr = _row_chunk(rh, 640)

    def body(pos_ref, a_ref, b_ref, o_ref):
        acc = a_ref[...].astype(F32)
        for j in range(3):
            acc = acc + b_ref[j].astype(F32)
        o_ref[...] = acc

    return pl.pallas_call(
        body, name="chip_sum", out_shape=SDS((2, rh, d), F32),
        grid_spec=pltpu.PrefetchScalarGridSpec(
            num_scalar_prefetch=1, grid=(rh // cr,),
            in_specs=[pl.BlockSpec((None, cr, d), lambda r, pos: (pos[0], r, 0)),
                      pl.BlockSpec((3, cr, d), lambda r, pos: (0, r, 0))],
            out_specs=pl.BlockSpec((None, cr, d), lambda r, pos: (pos[1], r, 0))),
        compiler_params=_cparams(),
    )(pos, cs, p3)


def small_pair_sum(a, b):
    def body(a_ref, b_ref, o_ref):
        o_ref[...] = a_ref[...] + b_ref[...]

    return pl.pallas_call(body, name="small_pair_sum", out_shape=SDS(a.shape, a.dtype),
                          compiler_params=_cparams())(a, b)


def small_chip_sum(pos, own, p):
    ns, w = own.shape

    def body(pos_ref, own_ref, p0, p1, p2, p3, o_ref):
        k = pos_ref[0]
        acc = None
        for chip, ref in enumerate((p0, p1, p2, p3)):
            term = jnp.where(k == chip, own_ref[...], ref[...])
            acc = term if acc is None else acc + term
        o_ref[...] = acc

    def slot(chip):
        return pl.BlockSpec((None, ns, w), lambda i, pos: (jnp.where(pos[0] == chip, (chip + 1) % N_CHIPS, chip), 0, 0))

    return pl.pallas_call(
        body, name="small_chip_sum", out_shape=SDS(own.shape, own.dtype),
        grid_spec=pltpu.PrefetchScalarGridSpec(
            num_scalar_prefetch=1, grid=(1,),
            in_specs=[pl.BlockSpec((ns, w), lambda i, pos: (0, 0))] + [slot(chip) for chip in range(N_CHIPS)],
            out_specs=pl.BlockSpec((ns, w), lambda i, pos: (0, 0))),
        compiler_params=_cparams(),
    )(pos, own, p, p, p, p)


def _rms(x, g):
    rs = lax.rsqrt(jnp.mean(x * x, axis=-1, keepdims=True) + EPS)
    xh = x * rs
    return xh, rs, xh * g


def _rms_bwd(dy, xh, rs, g):
    dxh = dy * g
    dx = rs * (dxh - xh * jnp.mean(dxh * xh, axis=-1, keepdims=True))
    return dx, dy * xh


def _gelu(x):
    t = jnp.tanh(GELU_C * (x + 0.044715 * x * x * x))
    return 0.5 * x * (1.0 + t), t


def _gelu_grad(x, t):
    return 0.5 * (1.0 + t) + 0.5 * x * (1.0 - t * t) * GELU_C * (1.0 + 3.0 * 0.044715 * x * x)


def _shift_rows(v, s, n):
    if s == 0:
        return v
    t = lax.broadcasted_iota(jnp.int32, v.shape, 0)
    rolled = pltpu.roll(v, (-s) % n, 0)
    return jnp.where((t + s >= 0) & (t + s < n), rolled, 0.0)


def _scan_rows(a_ref, u_ref, h_ref, acum_ref, reverse):
    s_len, w = a_ref.shape
    chunk = min(512, s_len)
    last = 0 if reverse else SUBLANES - 1

    def inside_vregs(ci, _):
        rows = pl.ds(pl.multiple_of(ci * chunk, chunk), chunk)
        a = a_ref[rows, :]
        u = u_ref[rows, :]
        pos = lax.broadcasted_iota(jnp.int32, (chunk, w), 0) % SUBLANES
        for dist in (1, 2, 4):
            ok = (pos < SUBLANES - dist) if reverse else (pos >= dist)
            shift = chunk - dist if reverse else dist
            u = u + a * jnp.where(ok, pltpu.roll(u, shift, 0), 0.0)
            a = a * jnp.where(ok, pltpu.roll(a, shift, 0), 1.0)
        h_ref[rows, :] = u
        acum_ref[rows, :] = a
        return 0

    lax.fori_loop(0, s_len // chunk, inside_vregs, 0)

    chains = max(1, min(SCAN_CHAINS, s_len // (8 * SUBLANES)))
    seg = s_len // chains
    nvreg = seg // SUBLANES

    def step(j, carry):
        jj = (nvreg - 1 - j) if reverse else j
        out = []
        for c, (hin, ain) in enumerate(carry):
            rows = pl.ds(pl.multiple_of(c * seg + jj * SUBLANES, SUBLANES), SUBLANES)
            acc = acum_ref[rows, :]
            h = h_ref[rows, :] + acc * hin
            acc = acc * ain
            h_ref[rows, :] = h
            acum_ref[rows, :] = acc
            out.append((jnp.broadcast_to(h[last:last + 1, :], h.shape), jnp.broadcast_to(acc[last:last + 1, :], acc.shape)))
        return tuple(out)

    init = tuple((jnp.zeros((SUBLANES, w), F32), jnp.ones((SUBLANES, w), F32)) for _ in range(chains))
    ends = lax.fori_loop(0, nvreg, step, init, unroll=min(2, nvreg))
    order = range(chains - 2, -1, -1) if reverse else range(1, chains)
    inflow = jnp.zeros((1, w), F32)
    for s in order:
        h, acc = ends[s + 1 if reverse else s - 1]
        inflow = h[0:1, :] + acc[0:1, :] * inflow
        rows = pl.ds(s * seg, seg)
        h_ref[rows, :] = h_ref[rows, :] + acum_ref[rows, :] * inflow


def _w_spec(rows_half, d, blk):
    return pl.BlockSpec((N_CHIPS, 2, rows_half, d), lambda *_: (0, 0, blk, 0), pipeline_mode=pl.Buffered(1))


def ffn_forward(x, gain, wfull, lay, which, deps=(), tm=512):
    s_len, d = x.shape
    tm = min(tm, s_len)
    f = 8 * lay.fh
    fc = f // 2

    def body(x_ref, g_ref, wg_ref, wu_ref, wd_ref, *rest):
        o_ref, gate_ref, up_ref = rest[len(deps):]
        x = x_ref[...]
        _, _, hn = _rms(x, g_ref[...])
        h = hn.astype(BF)
        y = jnp.zeros((tm, d), F32)
        for part in range(2):
            cols = slice(part * fc, (part + 1) * fc)
            gate = dot_nt(h, wg_ref[...].reshape(f, d)[cols])
            up = dot_nt(h, wu_ref[...].reshape(f, d)[cols])
            act = (gate * jax.nn.sigmoid(gate) * up).astype(BF)
            y = y + dot_nn(act, wd_ref[...].reshape(f, d)[cols])
            gate_ref[:, cols] = gate.astype(BF)
            up_ref[:, cols] = up.astype(BF)
        o_ref[...] = x + FFN_RES * y

    row = pl.BlockSpec((tm, d), lambda i: (i, 0))
    wide = pl.BlockSpec((tm, f), lambda i: (i, 0))
    return pl.pallas_call(
        body, name="ffn_forward", grid=(s_len // tm,),
        out_shape=(SDS((s_len, d), F32), SDS((s_len, f), BF), SDS((s_len, f), BF)),
        in_specs=[row, pl.BlockSpec((1, d), lambda i: (0, 0))]
        + [_w_spec(lay.fh, d, 3 * which + m) for m in range(3)] + [ANY] * len(deps),
        out_specs=(row, wide, wide), compiler_params=_cparams(),
    )(x, gain, wfull, wfull, wfull, *deps)


def ffn_backward_dx(x, gain, dout, gate_bf, up_bf, wfull, lay, which, deps=(), tm=256):
    s_len, d = x.shape
    tm = min(tm, s_len)
    f = 8 * lay.fh
    fc = f // 2
    nt = s_len // tm

    def body(x_ref, g_ref, do_ref, gate_ref, up_ref, wg_ref, wu_ref, wd_ref, *rest):
        dx_ref, dg_ref, lhs_ref, rhs_ref = rest[len(deps):]
        dgate_ref, dup_ref, act_ref = lhs_ref.at[0], lhs_ref.at[1], lhs_ref.at[2]
        h_ref, df_ref = rhs_ref.at[0], rhs_ref.at[1]
        x = x_ref[...]
        g = g_ref[...]
        xh, rs, hn = _rms(x, g)
        h = hn.astype(BF)
        do = do_ref[...]
        df = (FFN_RES * do).astype(BF)
        dh = jnp.zeros((tm, d), F32)
        for part in range(2):
            cols = slice(part * fc, (part + 1) * fc)
            wg = wg_ref[...].reshape(f, d)[cols]
            wu = wu_ref[...].reshape(f, d)[cols]
            gate = gate_ref[:, cols].astype(F32)
            up = up_ref[:, cols].astype(F32)
            sg = jax.nn.sigmoid(gate)
            silu = gate * sg
            dact = dot_nt(df, wd_ref[...].reshape(f, d)[cols])
            dup = (dact * silu).astype(BF)
            dgate = (dact * up * (sg * (1.0 + gate * (1.0 - sg)))).astype(BF)
            dh = dh + dot_nn(dgate, wg) + dot_nn(dup, wu)
            dgate_ref[:, cols] = dgate
            dup_ref[:, cols] = dup
            act_ref[:, cols] = (silu * up).astype(BF)
        dxn, dgrow = _rms_bwd(dh, xh, rs, g)
        dx_ref[...] = do + dxn

        @pl.when(pl.program_id(0) == 0)
        def _():
            dg_ref[...] = jnp.zeros_like(dg_ref)

        dg_ref[...] += jnp.sum(dgrow, axis=0, keepdims=True)
        h_ref[...] = h
        df_ref[...] = df

    row = pl.BlockSpec((tm, d), lambda i: (i, 0))
    wide = pl.BlockSpec((tm, f), lambda i: (i, 0))
    vec = pl.BlockSpec((1, d), lambda i: (0, 0))
    return pl.pallas_call(
        body, name="ffn_backward_dx", grid=(nt,),
        out_shape=(SDS((s_len, d), F32), SDS((1, d), F32), SDS((3, s_len, f), BF), SDS((2, s_len, d), BF)),
        in_specs=[row, vec, row, wide, wide] + [_w_spec(lay.fh, d, 3 * which + m) for m in range(3)] + [ANY] * len(deps),
        out_specs=(row, vec, pl.BlockSpec((3, tm, f), lambda i: (0, i, 0)), pl.BlockSpec((2, tm, d), lambda i: (0, i, 0))),
        compiler_params=_cparams(),
    )(x, gain, dout, gate_bf, up_bf, wfull, wfull, wfull, *deps)


def weight_grad_tn(lhs, rhs, gb, lay, blk0, tk=2048):
    nmat, s_len, f = lhs.shape
    tk = min(tk, s_len)
    d = rhs.shape[2]
    fc = f // 2
    nk = s_len // tk

    def body(a_ref, b_ref, gb_ref, o_ref, acc):
        kt = pl.program_id(2)

        @pl.when(kt == 0)
        def _():
            acc[...] = jnp.zeros_like(acc)

        acc[...] += dot_tn(a_ref[...], b_ref[...])

        @pl.when(kt == nk - 1)
        def _():
            for p in range(2):
                for q in range(2):
                    o_ref[p, q] = acc[pl.ds((2 * p + q) * lay.fh, lay.fh), :].astype(o_ref.dtype)

    return pl.pallas_call(
        body, name="weight_grad_tn", grid=(nmat, 2, nk), out_shape=SDS(gb.shape, gb.dtype),
        in_specs=[pl.BlockSpec((None, tk, fc), lambda m, j, kt: (m, kt, j)),
                  pl.BlockSpec((None, tk, d), lambda m, j, kt: (jnp.where(m == nmat - 1, 1, 0), kt, 0)), ANY],
        out_specs=pl.BlockSpec((2, 2, lay.fh, d), lambda m, j, kt: (j, 0, blk0 + m, 0)),
        scratch_shapes=[pltpu.VMEM((fc, d), F32)],
        input_output_aliases={2: 0}, compiler_params=_cparams(),
    )(lhs, rhs, gb)


def _lane_blocks(v):
    return [v[:, j * LANE:(j + 1) * LANE] for j in range(v.shape[1] // LANE)]


def _join_lane_blocks(ref):
    return jnp.concatenate([ref[j] for j in range(ref.shape[0])], axis=1)


def _cbm_spec(nblk, rows, first=0):
    return pl.BlockSpec((nblk, rows, LANE), lambda i: (first // nblk, i, 0))


def mix_project(x, gain, wfull, lay, lw, att, tm=512):
    s_len, d = x.shape
    tm = min(tm, s_len)
    d_in = 8 * lay.ih
    kvw = (d_in - 2 * lw - att) // 2
    ncol = (2 * lw + 2 * kvw) // LANE

    def body(x_ref, g_ref, w_ref, o_ref, qt_ref, vt_ref):
        _, _, hn = _rms(x_ref[...], g_ref[...])
        h = hn.astype(BF)
        w = w_ref[:, :, :lay.ih, :].reshape(d_in, d)
        pieces = _lane_blocks(dot_nt(h, w[:2 * lw])) + _lane_blocks(dot_nt(h, w[2 * lw + att:]))
        for j, piece in enumerate(pieces):
            o_ref[j] = piece
        qt_ref[...] = dot_nt(w[2 * lw:2 * lw + att], h)
        vt_ref[...] = dot_nt(w[2 * lw + att + kvw:], h)

    return pl.pallas_call(
        body, name="mix_project", grid=(s_len // tm,),
        out_shape=(SDS((ncol, s_len, LANE), F32), SDS((att, s_len), F32), SDS((kvw, s_len), F32)),
        in_specs=[pl.BlockSpec((tm, d), lambda i: (i, 0)), pl.BlockSpec((1, d), lambda i: (0, 0)),
                  _w_spec(lay.fh, d, lay.MIX_BLK)],
        out_specs=(_cbm_spec(ncol, tm), pl.BlockSpec((att, tm), lambda i: (0, i)), pl.BlockSpec((kvw, tm), lambda i: (0, i))),
        compiler_params=_cparams(),
    )(x, gain, wfull)


def mix_project_backward(x, gain, dout, dxr, dgt, dqt, dkv, dwout, wfull, gb, lay, tm=512):
    s_len, d = x.shape
    tm = min(tm, s_len)
    d_in = 8 * lay.ih
    nt = s_len // tm
    kvw = dkv.shape[1]
    att = dqt.shape[0]
    nlru = (dxr.shape[0] + dgt.shape[0]) * LANE

    def body(x_ref, g_ref, do_ref, dxr_ref, dgt_ref, dqt_ref, dkv_ref, dwo_ref, w_ref, gb_ref, dx_ref, dg_ref, o_ref, acc):
        i = pl.program_id(0)
        g = g_ref[...]
        xh, rs, hn = _rms(x_ref[...], g)
        h = hn.astype(BF)
        w = w_ref[:, :, :lay.ih, :].reshape(d_in, d)
        dlru = jnp.concatenate([_join_lane_blocks(dxr_ref), _join_lane_blocks(dgt_ref)], axis=1).astype(BF)
        dqt = dqt_ref[...].astype(BF)
        dkv = dkv_ref[...].astype(BF)
        dh = dot_nn(dlru, w[:nlru]) + dot_tn(dqt, w[nlru:nlru + att]) + dot_nn(dkv, w[nlru + att:])
        dxn, dgrow = _rms_bwd(dh, xh, rs, g)
        dx_ref[...] = do_ref[...] + dxn

        @pl.when(i == 0)
        def _():
            dg_ref[...] = jnp.zeros_like(dg_ref)
            acc[...] = jnp.zeros_like(acc)

        dg_ref[...] += jnp.sum(dgrow, axis=0, keepdims=True)
        acc[0:nlru, :] += dot_tn(dlru, h)
        acc[nlru:nlru + att, :] += dot_nn(dqt, h)
        acc[nlru + att:, :] += dot_tn(dkv, h)

        @pl.when(i == nt - 1)
        def _():
            for p in range(N_CHIPS):
                for q in range(2):
                    o_ref[p, q, :lay.ih, :] = acc[pl.ds((2 * p + q) * lay.ih, lay.ih), :].astype(o_ref.dtype)
            o_ref[:, :, lay.ih:, :] = dwo_ref[...]

    row = pl.BlockSpec((tm, d), lambda i: (i, 0))
    vec = pl.BlockSpec((1, d), lambda i: (0, 0))
    return pl.pallas_call(
        body, name="mix_project_backward", grid=(nt,),
        out_shape=(SDS((s_len, d), F32), SDS((1, d), F32), SDS(gb.shape, gb.dtype)),
        in_specs=[row, vec, row, _cbm_spec(dxr.shape[0], tm), _cbm_spec(dgt.shape[0], tm),
                  pl.BlockSpec((att, tm), lambda i: (0, i)), pl.BlockSpec((tm, kvw), lambda i: (i, 0)),
                  pl.BlockSpec(dwout.shape, lambda i: (0, 0, 0, 0)), _w_spec(lay.fh, d, lay.MIX_BLK), ANY],
        out_specs=(row, vec, pl.BlockSpec((N_CHIPS, 2, lay.fh, d), lambda i: (0, 0, lay.MIX_BLK, 0))),
        scratch_shapes=[pltpu.VMEM((d_in, d), F32)],
        input_output_aliases={9: 2}, compiler_params=_cparams(),
    )(x, gain, dout, dxr, dgt, dqt, dkv, dwout, wfull, gb)


def _lru_gates(xc, wb_ref, pv_ref, direction):
    xcb = xc.astype(BF)
    r = jax.nn.sigmoid(dot_nn(xcb, wb_ref[2 * direction]) + pv_ref[1 + direction:2 + direction, :])
    i = jax.nn.sigmoid(dot_nn(xcb, wb_ref[2 * direction + 1]) + pv_ref[3 + direction:4 + direction, :])
    lam = pv_ref[5 + direction:6 + direction, :]
    sp = jnp.maximum(-lam, 0.0) + jnp.log(1.0 + jnp.exp(-jnp.abs(lam)))
    a = jnp.exp(-LRU_C * sp * r)
    mult = jnp.sqrt(1.0 - a * a)
    return xcb, r, i, a, mult, sp


def _conv_rows(xr, cv_ref, bias, n):
    acc = bias + cv_ref[0:1, :] * _shift_rows(xr, -2, n)
    for j in range(1, CONV_WIDTH):
        acc = acc + cv_ref[j:j + 1, :] * _shift_rows(xr, j - 2, n)
    return acc


def lru_forward(proj, cvec, pvec, wblk, lw, deps=(), ch=512):
    s_len = proj.shape[1]
    ncb = lw // LANE
    ch = min(ch, s_len)
    nchunk = s_len // ch

    def body(xr_ref, gt_ref, cv_ref, pv_ref, wb_ref, *rest):
        y_ref, hs_ref, xc_s, a_s, u_s, acum_s = rest[len(deps):]
        xc_s[...] = _conv_rows(xr_ref[...], cv_ref, pv_ref[0:1, :], s_len)
        for direction in range(2):
            def fill(ci, _):
                rows = pl.ds(pl.multiple_of(ci * ch, ch), ch)
                xc = xc_s[rows, :]
                _, _, i, a, mult, _ = _lru_gates(xc, wb_ref, pv_ref, direction)
                a_s[rows, :] = a
                u_s[rows, :] = mult * (i * xc)
                return 0

            lax.fori_loop(0, nchunk, fill, 0)
            _scan_rows(a_s, u_s, hs_ref.at[direction], acum_s, reverse=direction == 1)

        def out(ci, _):
            rows = pl.ds(pl.multiple_of(ci * ch, ch), ch)
            gl, _ = _gelu(gt_ref[rows, :])
            y_ref[rows, :] = gl * (hs_ref[0, rows, :] + hs_ref[1, rows, :])
            return 0

        lax.fori_loop(0, nchunk, out, 0)

    col = lambda off: pl.BlockSpec((None, s_len, LANE), lambda cb: (off + cb, 0, 0))
    return pl.pallas_call(
        body, name="lru_forward", grid=(ncb,),
        out_shape=(SDS((ncb, s_len, LANE), F32), SDS((2, ncb, s_len, LANE), F32)),
        in_specs=[col(0), col(ncb), pl.BlockSpec((8, LANE), lambda cb: (0, cb)), pl.BlockSpec((8, LANE), lambda cb: (0, cb)),
                  pl.BlockSpec((4, None, LANE, LANE), lambda cb: (0, cb, 0, 0))] + [ANY] * len(deps),
        out_specs=(col(0), pl.BlockSpec((2, None, s_len, LANE), lambda cb: (0, cb, 0, 0))),
        scratch_shapes=[pltpu.VMEM((s_len, LANE), F32)] * 4, compiler_params=_cparams(),
    )(proj, proj, cvec, pvec, wblk, *deps)


def lru_backward(proj, hs, dy, cvec, pvec, wblk, lw, ch=512):
    s_len = proj.shape[1]
    ncb = lw // LANE
    ch = min(ch, s_len)
    nchunk = s_len // ch

    def body(xr_ref, gt_ref, hs_ref, dy_ref, cv_ref, pv_ref, wb_ref, dxr_ref, dgt_ref, dcv_ref, dpv_ref, dwb_ref,
             xc_s, a_s, dh_s, lam_s, hp_s, dxc_s, acum_s):
        xr = xr_ref[...]
        xc_s[...] = _conv_rows(xr, cv_ref, pv_ref[0:1, :], s_len)
        dxc_s[...] = jnp.zeros_like(dxc_s)
        dpv_ref[...] = jnp.zeros_like(dpv_ref)
        dwb_ref[...] = jnp.zeros_like(dwb_ref)

        def head(ci, _):
            rows = pl.ds(pl.multiple_of(ci * ch, ch), ch)
            gt = gt_ref[rows, :]
            gl, t = _gelu(gt)
            dy = dy_ref[rows, :]
            dh_s[rows, :] = dy * gl
            dgt_ref[rows, :] = dy * (hs_ref[0, rows, :] + hs_ref[1, rows, :]) * _gelu_grad(gt, t)
            return 0

        lax.fori_loop(0, nchunk, head, 0)

        for direction in range(2):
            def fill(ci, _):
                rows = pl.ds(pl.multiple_of(ci * ch, ch), ch)
                _, _, _, a, _, _ = _lru_gates(xc_s[rows, :], wb_ref, pv_ref, direction)
                a_s[rows, :] = a
                return 0

            lax.fori_loop(0, nchunk, fill, 0)
            toward = 1 if direction == 0 else -1
            hp_s[...] = _shift_rows(a_s[...], toward, s_len)
            _scan_rows(hp_s, dh_s, lam_s, acum_s, reverse=direction == 0)
            hp_s[...] = _shift_rows(hs_ref[direction], -toward, s_len)

            def grads(ci, _):
                rows = pl.ds(pl.multiple_of(ci * ch, ch), ch)
                xc = xc_s[rows, :]
                xcb, r, i, a, mult, sp = _lru_gates(xc, wb_ref, pv_ref, direction)
                du = lam_s[rows, :]
                da = du * hp_s[rows, :]
                dmult = du * i * xc
                di = du * mult * xc
                dlog_a = (da - dmult * a / mult) * a
                dr = dlog_a * (-LRU_C * sp)
                dza = dr * r * (1.0 - r)
                dzx = di * i * (1.0 - i)
                dzab = dza.astype(BF)
                dzxb = dzx.astype(BF)
                dxc_s[rows, :] += (du * mult * i + dot_nt(dzab, wb_ref[2 * direction])
                                   + dot_nt(dzxb, wb_ref[2 * direction + 1]))
                dwb_ref[2 * direction] += dot_tn(xcb, dzab)
                dwb_ref[2 * direction + 1] += dot_tn(xcb, dzxb)
                dpv_ref[1 + direction:2 + direction, :] += jnp.sum(dza, axis=0, keepdims=True)
                dpv_ref[3 + direction:4 + direction, :] += jnp.sum(dzx, axis=0, keepdims=True)
                dpv_ref[5 + direction:6 + direction, :] += jnp.sum(dlog_a * (-LRU_C * r), axis=0, keepdims=True)
                return 0

            lax.fori_loop(0, nchunk, grads, 0)

        for direction in range(2):
            lam = pv_ref[5 + direction:6 + direction, :]
            dpv_ref[5 + direction:6 + direction, :] = dpv_ref[5 + direction:6 + direction, :] * (-jax.nn.sigmoid(-lam))
        dxc = dxc_s[...]
        dpv_ref[0:1, :] = jnp.sum(dxc, axis=0, keepdims=True)
        dxr = cv_ref[0:1, :] * _shift_rows(dxc, 2, s_len)
        for j in range(1, CONV_WIDTH):
            dxr = dxr + cv_ref[j:j + 1, :] * _shift_rows(dxc, 2 - j, s_len)
        dxr_ref[...] = dxr
        dcv_ref[...] = jnp.zeros_like(dcv_ref)
        for j in range(CONV_WIDTH):
            dcv_ref[j:j + 1, :] = jnp.sum(dxc * _shift_rows(xr, j - 2, s_len), axis=0, keepdims=True)

    col = lambda off: pl.BlockSpec((None, s_len, LANE), lambda cb: (off + cb, 0, 0))
    own = col(0)
    small = pl.BlockSpec((8, LANE), lambda cb: (0, cb))
    wspec = pl.BlockSpec((4, None, LANE, LANE), lambda cb: (0, cb, 0, 0))
    return pl.pallas_call(
        body, name="lru_backward", grid=(ncb,),
        out_shape=(SDS((ncb, s_len, LANE), F32), SDS((ncb, s_len, LANE), F32), SDS((8, lw), F32), SDS((8, lw), F32),
                   SDS(wblk.shape, F32)),
        in_specs=[col(0), col(ncb), pl.BlockSpec((2, None, s_len, LANE), lambda cb: (0, cb, 0, 0)), own, small, small, wspec],
        out_specs=(own, own, small, small, wspec),
        scratch_shapes=[pltpu.VMEM((s_len, LANE), F32)] * 7, compiler_params=_cparams(),
    )(proj, proj, hs, dy, cvec, pvec, wblk)


def _window_specs(s_len, first, width=None):
    nb = s_len // BLOCK
    where = (lambda n: jnp.maximum(n - 1, 0), lambda n: n, lambda n: jnp.minimum(n + 1, nb - 1))
    if width is None:
        return [pl.BlockSpec((None, BLOCK, LANE), lambda n, f=f: (first, f(n), 0)) for f in where]
    return [pl.BlockSpec((width, BLOCK), lambda n, f=f: (0, f(n))) for f in where]


def _stack_heads(v, kh):
    return jnp.concatenate([v[(kh * KV_GROUP + g) * HEAD_DIM:(kh * KV_GROUP + g + 1) * HEAD_DIM, :]
                            for g in range(KV_GROUP)], axis=1)


def _unstack_heads(ref, kh, v):
    for g in range(KV_GROUP):
        h = kh * KV_GROUP + g
        ref[h * HEAD_DIM:(h + 1) * HEAD_DIM, :] = v[:, g * BLOCK:(g + 1) * BLOCK]


def _key_exists(n, nb):
    j = lax.broadcasted_iota(jnp.int32, (3 * BLOCK, 1), 0)
    return ((n > 0) | (j >= BLOCK)) & ((n < nb - 1) | (j < 2 * BLOCK))


def _attn_probs(qs, kcat, bias_g, sink_g, key_ok):
    logits = jnp.where(key_ok, dot_nn(kcat, qs) + bias_g, NEG_INF)
    m = jnp.maximum(jnp.max(logits, axis=0, keepdims=True), sink_g)
    p = jnp.exp(logits - m)
    es = jnp.exp(sink_g - m)
    inv = 1.0 / (jnp.sum(p, axis=0, keepdims=True) + es)
    return p * inv, es * inv


def attention_forward(qt, proj, vt, bias, sink, kblk):
    att, s_len = qt.shape
    kvw = vt.shape[0]
    nb = s_len // BLOCK

    def body(q_ref, kp_ref, kc_ref, kn_ref, vp_ref, vc_ref, vn_ref, b_ref, s_ref, o_ref):
        n = pl.program_id(0)
        q = q_ref[...]
        key_ok = _key_exists(n, nb)
        kall = jnp.concatenate([kp_ref[...], kc_ref[...], kn_ref[...]], axis=0).astype(BF)
        vall = jnp.concatenate([vp_ref[...], vc_ref[...], vn_ref[...]], axis=1).astype(BF)
        for kh in range(N_KV_HEADS):
            qs = (_stack_heads(q, kh) * (HEAD_DIM ** -0.5)).astype(BF)
            p, _ = _attn_probs(qs, kall[:, kh * HEAD_DIM:(kh + 1) * HEAD_DIM], b_ref[kh], s_ref[kh, 0:1, :], key_ok)
            _unstack_heads(o_ref, kh, dot_nn(vall[kh * HEAD_DIM:(kh + 1) * HEAD_DIM, :], p.astype(BF)))

    blk = pl.BlockSpec((att, BLOCK), lambda n: (0, n))
    return pl.pallas_call(
        body, name="attention_forward", grid=(nb,), out_shape=SDS((att, s_len), F32),
        in_specs=[blk] + _window_specs(s_len, kblk) + _window_specs(s_len, 0, kvw)
        + [pl.BlockSpec(bias.shape, lambda n: (0, 0, 0)), pl.BlockSpec(sink.shape, lambda n: (0, 0, 0))],
        out_specs=blk, compiler_params=_cparams(),
    )(qt, proj, proj, proj, vt, vt, vt, bias, sink)


def attention_backward(qt, proj, y_att, dy, bias, sink, kblk):
    att, s_len = qt.shape
    nb = s_len // BLOCK
    kvw = N_KV_HEADS * HEAD_DIM

    def body(q_ref, kp_ref, kc_ref, kn_ref, vp_ref, vc_ref, vn_ref, o_ref, do_ref, b_ref, s_ref,
             dq_ref, dkv_ref, db_ref, ds_ref):
        n = pl.program_id(0)

        @pl.when(n == 0)
        def _():
            dkv_ref[...] = jnp.zeros_like(dkv_ref)
            db_ref[...] = jnp.zeros_like(db_ref)
            ds_ref[...] = jnp.zeros_like(ds_ref)

        q = q_ref[...]
        o = o_ref[...]
        do = do_ref[...]
        kall = jnp.concatenate([kp_ref[...], kc_ref[...], kn_ref[...]], axis=0).astype(BF)
        vall = jnp.concatenate([vp_ref[...], vc_ref[...], vn_ref[...]], axis=0).astype(BF)
        key_ok = _key_exists(n, nb)
        dks, dvs = [], []
        for kh in range(N_KV_HEADS):
            kcat = kall[:, kh * HEAD_DIM:(kh + 1) * HEAD_DIM]
            vcat = vall[:, kh * HEAD_DIM:(kh + 1) * HEAD_DIM]
            qs = (_stack_heads(q, kh) * (HEAD_DIM ** -0.5)).astype(BF)
            p, ps = _attn_probs(qs, kcat, b_ref[kh], s_ref[kh, 0:1, :], key_ok)
            dos = _stack_heads(do, kh)
            dosb = dos.astype(BF)
            delta = jnp.sum(dos * _stack_heads(o, kh), axis=0, keepdims=True)
            dlog = p * (dot_nn(vcat, dosb) - delta)
            dlogb = dlog.astype(BF)
            db_ref[kh] += dlog
            ds_ref[kh] += jnp.broadcast_to(-ps * delta, ds_ref.shape[1:])
            _unstack_heads(dq_ref, kh, dot_tn(kcat, dlogb) * (HEAD_DIM ** -0.5))
            dks.append(dot_nt(dlogb, qs))
            dvs.append(dot_nt(p.astype(BF), dosb))
        dkv = jnp.concatenate(dks + dvs, axis=1)
        starts = [jnp.maximum(n - 1, 0), n, jnp.minimum(n + 1, nb - 1)]
        for b, st in enumerate(starts):
            rows = pl.ds(pl.multiple_of(st * BLOCK, BLOCK), BLOCK)
            dkv_ref[rows, :] += dkv[b * BLOCK:(b + 1) * BLOCK, :]

    blk = pl.BlockSpec((att, BLOCK), lambda n: (0, n))
    whole = lambda a: pl.BlockSpec(a.shape, lambda n: (0, 0, 0))
    return pl.pallas_call(
        body, name="attention_backward", grid=(nb,),
        out_shape=(SDS((att, s_len), F32), SDS((s_len, 2 * kvw), F32), SDS(bias.shape, F32), SDS(sink.shape, F32)),
        in_specs=[blk] + _window_specs(s_len, kblk) + _window_specs(s_len, kblk + 1) + [blk, blk, whole(bias), whole(sink)],
        out_specs=(blk, pl.BlockSpec((s_len, 2 * kvw), lambda n: (0, 0)), whole(bias), whole(sink)),
        compiler_params=_cparams(),
    )(qt, proj, proj, proj, proj, proj, proj, y_att, dy, bias, sink)


def _rms_cols(x, g):
    rs = lax.rsqrt(jnp.mean(x * x, axis=0, keepdims=True) + EPS)
    xh = x * rs
    return xh, rs, xh * g


def _rms_cols_bwd(dy, xh, rs, g):
    dxh = dy * g
    dx = rs * (dxh - xh * jnp.mean(dxh * xh, axis=0, keepdims=True))
    return dx, dy * xh


def mix_output(x, y_rec, y_att, g_rec, g_att, wfull, lay, tm=512):
    s_len, d = x.shape
    tm = min(tm, s_len)
    lw = y_rec.shape[0] * LANE
    att = y_att.shape[0]

    def body(x_ref, yr_ref, ya_ref, gr_ref, ga_ref, w_ref, o_ref):
        _, _, nr = _rms(_join_lane_blocks(yr_ref), gr_ref[...])
        _, _, na = _rms_cols(ya_ref[...], ga_ref[...])
        w = w_ref[:, :, lay.ih:, :].reshape(d, d)
        o_ref[...] = x_ref[...] + dot_nn(nr.astype(BF), w[:lw]) + dot_tn(na.astype(BF), w[lw:])

    row = pl.BlockSpec((tm, d), lambda i: (i, 0))
    return pl.pallas_call(
        body, name="mix_output", grid=(s_len // tm,), out_shape=SDS((s_len, d), F32),
        in_specs=[row, _cbm_spec(lw // LANE, tm), pl.BlockSpec((att, tm), lambda i: (0, i)),
                  pl.BlockSpec((1, lw), lambda i: (0, 0)), pl.BlockSpec((att, 1), lambda i: (0, 0)),
                  _w_spec(lay.fh, d, lay.MIX_BLK)],
        out_specs=row, compiler_params=_cparams(),
    )(x, y_rec, y_att, g_rec, g_att, wfull)


def mix_output_backward(dout, y_rec, y_att, g_rec, g_att, wfull, lay, deps=(), tm=512):
    s_len, d = dout.shape
    tm = min(tm, s_len)
    lw = y_rec.shape[0] * LANE
    att = y_att.shape[0]
    nt = s_len // tm

    def body(do_ref, yr_ref, ya_ref, gr_ref, ga_ref, w_ref, *rest):
        dyr_ref, dya_ref, dgr_ref, dga_ref, o_ref, acc = rest[len(deps):]
        i = pl.program_id(0)
        gr = gr_ref[...]
        ga = ga_ref[...]
        xhr, rsr, nr = _rms(_join_lane_blocks(yr_ref), gr)
        xha, rsa, na = _rms_cols(ya_ref[...], ga)
        dob = do_ref[...].astype(BF)
        w = w_ref[:, :, lay.ih:, :].reshape(d, d)
        dyr, dgr_row = _rms_bwd(dot_nt(dob, w[:lw]), xhr, rsr, gr)
        dya, dga_col = _rms_cols_bwd(dot_nt(w[lw:], dob), xha, rsa, ga)
        for j, piece in enumerate(_lane_blocks(dyr)):
            dyr_ref[j] = piece
        dya_ref[...] = dya

        @pl.when(i == 0)
        def _():
            dgr_ref[...] = jnp.zeros_like(dgr_ref)
            dga_ref[...] = jnp.zeros_like(dga_ref)
            acc[...] = jnp.zeros_like(acc)

        dgr_ref[...] += jnp.sum(dgr_row, axis=0, keepdims=True)
        dga_ref[...] += jnp.sum(dga_col, axis=1, keepdims=True)
        acc[0:lw, :] += dot_tn(nr.astype(BF), dob)
        acc[lw:, :] += dot_nn(na.astype(BF), dob)

        @pl.when(i == nt - 1)
        def _():
            for p in range(N_CHIPS):
                for q in range(2):
                    o_ref[p, q] = acc[pl.ds((2 * p + q) * lay.oh, lay.oh), :].astype(o_ref.dtype)

    row = pl.BlockSpec((tm, d), lambda i: (i, 0))
    return pl.pallas_call(
        body, name="mix_output_backward", grid=(nt,),
        out_shape=(SDS(y_rec.shape, F32), SDS(y_att.shape, F32), SDS((1, lw), F32), SDS((att, 1), F32),
                   SDS((N_CHIPS, 2, lay.oh, d), BF)),
        in_specs=[row, _cbm_spec(lw // LANE, tm), pl.BlockSpec((att, tm), lambda i: (0, i)),
                  pl.BlockSpec((1, lw), lambda i: (0, 0)), pl.BlockSpec((att, 1), lambda i: (0, 0)),
                  _w_spec(lay.fh, d, lay.MIX_BLK)] + [ANY] * len(deps),
        out_specs=(_cbm_spec(lw // LANE, tm), pl.BlockSpec((att, tm), lambda i: (0, i)),
                   pl.BlockSpec((1, lw), lambda i: (0, 0)), pl.BlockSpec((att, 1), lambda i: (0, 0)),
                   pl.BlockSpec((N_CHIPS, 2, lay.oh, d), lambda i: (0, 0, 0, 0))),
        scratch_shapes=[pltpu.VMEM((d, d), F32)], compiler_params=_cparams(),
    )(dout, y_rec, y_att, g_rec, g_att, wfull, *deps)


def loss_head(x, gain, target, tm=512):
    s_len, d = x.shape
    tm = min(tm, s_len)

    def body(x_ref, g_ref, t_ref, dx_ref, dg_ref, loss_ref):
        g = g_ref[...]
        xh, rs, y = _rms(x_ref[...], g)
        err = y - t_ref[...]

        @pl.when(pl.program_id(0) == 0)
        def _():
            dg_ref[...] = jnp.zeros_like(dg_ref)
            loss_ref[...] = jnp.zeros_like(loss_ref)

        part = 0.5 * jnp.sum(jnp.mean(err * err, axis=-1, keepdims=True), axis=0, keepdims=True)
        loss_ref[...] += jnp.broadcast_to(part, loss_ref.shape)
        dx, dgrow = _rms_bwd(err * (1.0 / d), xh, rs, g)
        dx_ref[...] = dx
        dg_ref[...] += jnp.sum(dgrow, axis=0, keepdims=True)

    row = pl.BlockSpec((tm, d), lambda i: (i, 0))
    vec = pl.BlockSpec((1, d), lambda i: (0, 0))
    return pl.pallas_call(
        body, name="loss_head", grid=(s_len // tm,),
        out_shape=(SDS((s_len, d), F32), SDS((1, d), F32), SDS((8, LANE), F32)),
        in_specs=[row, vec, row], out_specs=(row, vec, pl.BlockSpec((8, LANE), lambda i: (0, 0))),
        compiler_params=_cparams(),
    )(x, gain, target)


def _adamw_update(w, g, m, v):
    m = ADAM_B1 * m + (1.0 - ADAM_B1) * g
    v = ADAM_B2 * v + (1.0 - ADAM_B2) * (g * g)
    m_hat = m / (1.0 - ADAM_B1 ** ADAM_STEP)
    v_hat = v / (1.0 - ADAM_B2 ** ADAM_STEP)
    return -ADAM_LR * (m_hat / (jnp.sqrt(v_hat) + ADAM_EPS) + ADAM_WD * w), m, v


def adamw(w, g, m, v, tr=512):
    rows, cols = w.shape
    tr = _row_chunk(rows, tr, 8)

    def body(w_ref, g_ref, m_ref, v_ref, d_ref, nm_ref, nv_ref):
        d_ref[...], nm_ref[...], nv_ref[...] = _adamw_update(w_ref[...], g_ref[...], m_ref[...], v_ref[...])

    blk = pl.BlockSpec((tr, cols), lambda i: (i, 0))
    return pl.pallas_call(
        body, name="adamw", grid=(rows // tr,), out_shape=(SDS(w.shape, F32),) * 3,
        in_specs=[blk] * 4, out_specs=(blk,) * 3, compiler_params=_cparams(),
    )(w, g, m, v)


def adamw_layer(gf, blk, row_off, n_half, l, w, m, v, outs, deps=()):
    fh = gf.shape[1] // 7
    d = gf.shape[2]
    nd = len(deps)

    def body(gf_ref, w_ref, m_ref, v_ref, *rest):
        g_ref, d_ref, nm_ref, nv_ref = rest[4 + nd:]
        g = gf_ref[row_off:row_off + n_half, :]
        g_ref[...] = g
        d_ref[...], nm_ref[...], nv_ref[...] = _adamw_update(w_ref[...], g, m_ref[...], v_ref[...])

    gspec = pl.BlockSpec((None, fh, d), lambda h: (h, blk, 0))
    wspec = pl.BlockSpec((None, n_half, d), lambda h: (l, h, 0))
    return pl.pallas_call(
        body, name="adamw_layer", grid=(2,), out_shape=tuple(SDS(o.shape, o.dtype) for o in outs),
        in_specs=[gspec, wspec, wspec, wspec] + [ANY] * (4 + nd), out_specs=(wspec,) * 4,
        input_output_aliases={4 + i: i for i in range(4)}, compiler_params=_cparams(),
    )(gf, w, m, v, *outs, *deps)


def pack_weight(pos, land, blk, l, w, extra=None, deps=()):
    fh, d = land.shape[2] // 7, land.shape[3]
    nd = len(deps)

    def body(pos_ref, w_ref, *rest):
        o_ref = rest[-1]
        a = w_ref[...].astype(BF)
        n = a.shape[0] // 2
        for h in range(2):
            o_ref[h, 0:n, :] = a[h * n:(h + 1) * n]
        if extra is not None:
            b = rest[0][...].astype(BF)
            nb = b.shape[0] // 2
            for h in range(2):
                o_ref[h, n:n + nb, :] = b[h * nb:(h + 1) * nb]

    def whole(a):
        return pl.BlockSpec((None,) + a.shape[1:], lambda i, p: (l, 0, 0))

    ins = [w] + ([extra] if extra is not None else [])
    return pl.pallas_call(
        body, name="pack_weight", out_shape=SDS(land.shape, land.dtype),
        grid_spec=pltpu.PrefetchScalarGridSpec(
            num_scalar_prefetch=1, grid=(1,),
            in_specs=[whole(a) for a in ins] + [ANY] * (1 + nd),
            out_specs=pl.BlockSpec((None, 2, fh, d), lambda i, p: (p[0], 0, blk, 0))),
        input_output_aliases={1 + len(ins): 0}, compiler_params=_cparams(),
    )(pos, *ins, land, *deps)


def _rows_of(shape, width):
    return -(-int(np.prod(shape)) // (SUBLANES * width)) * SUBLANES


def _pack_rows(arrays, width):
    parts = []
    for a in arrays:
        flat = a.reshape(-1).astype(F32)
        r = _rows_of(a.shape, width)
        parts.append(jnp.pad(flat, (0, r * width - flat.shape[0])).reshape(r, width))
    return jnp.concatenate(parts, axis=0)


def _unpack_rows(buf, shapes):
    out, row = [], 0
    for shp in shapes:
        r = _rows_of(shp, buf.shape[1])
        out.append(buf[row:row + r].reshape(-1)[:int(np.prod(shp))].reshape(shp))
        row += r
    return out


def _t5_buckets(rel):
    half = N_BUCKETS // 2
    max_exact = half // 2
    ret = (rel > 0).astype(jnp.int32) * half
    n = jnp.abs(rel)
    n_f = jnp.maximum(n, 1).astype(F32)
    large = max_exact + (jnp.log(n_f / max_exact) / math.log(MAX_DISTANCE / max_exact) * (half - max_exact)).astype(jnp.int32)
    large = jnp.minimum(large, half - 1)
    return ret + jnp.where(n < max_exact, n, large)


def _band_buckets():
    t = jnp.arange(BLOCK)[:, None]
    j = jnp.arange(3 * BLOCK)[None, :]
    rel = j - BLOCK - t
    return _t5_buckets(rel), jnp.abs(rel) <= WINDOW


def _block_diag_pairs(w):
    depth, two, nblk, bw, _ = w.shape
    pairs = w.reshape(depth, two, nblk // 2, 2, bw, bw)
    z = jnp.zeros_like(pairs[:, :, :, 0])
    top = jnp.concatenate([pairs[:, :, :, 0], z], axis=-1)
    bot = jnp.concatenate([z, pairs[:, :, :, 1]], axis=-1)
    return jnp.concatenate([top, bot], axis=-2)


def _diag_blocks(dw):
    bw = dw.shape[-1] // 2
    a = dw[:, :, :bw, :bw]
    b = dw[:, :, bw:, bw:]
    return jnp.stack([a, b], axis=2).reshape(dw.shape[0], 2 * dw.shape[1], bw, bw)


def kernel(x, ffn1_norm, ffn1_w_gate, ffn1_w_up, ffn1_w_down, mix_norm, w_in, conv_w, conv_b, lru_w_a, lru_b_a, lru_w_x, lru_b_x, lru_lambda, attn_sink, rel_bias, lru_out_norm, attn_out_norm, w_out, ffn2_norm, ffn2_w_gate, ffn2_w_up, ffn2_w_down, final_norm, loss_target, m_ffn1_norm, m_ffn1_w_gate, m_ffn1_w_up, m_ffn1_w_down, m_mix_norm, m_w_in, m_conv_w, m_conv_b, m_lru_w_a, m_lru_b_a, m_lru_w_x, m_lru_b_x, m_lru_lambda, m_attn_sink, m_rel_bias, m_lru_out_norm, m_attn_out_norm, m_w_out, m_ffn2_norm, m_ffn2_w_gate, m_ffn2_w_up, m_ffn2_w_down, m_final_norm, v_ffn1_norm, v_ffn1_w_gate, v_ffn1_w_up, v_ffn1_w_down, v_mix_norm, v_w_in, v_conv_w, v_conv_b, v_lru_w_a, v_lru_b_a, v_lru_w_x, v_lru_b_x, v_lru_lambda, v_attn_sink, v_rel_bias, v_lru_out_norm, v_attn_out_norm, v_w_out, v_ffn2_norm, v_ffn2_w_gate, v_ffn2_w_up, v_ffn2_w_down, v_final_norm):
    depth, d = ffn1_norm.shape
    d_ff = N_CHIPS * ffn1_w_gate.shape[2]
    d_in = N_CHIPS * w_in.shape[2]
    lw = conv_b.shape[1]
    att = N_HEADS * HEAD_DIM
    lay = Layout(d, d_ff, d_in)
    k_chip = 2 * lax.axis_index("x") + lax.axis_index("y")
    pos = jnp.stack([k_chip, lax.axis_index("c")]).astype(jnp.int32)

    def rows_major(a):
        return jnp.swapaxes(a, 1, 2)

    mats = (rows_major(ffn1_w_gate), rows_major(ffn1_w_up), ffn1_w_down,
            rows_major(ffn2_w_gate), rows_major(ffn2_w_up), ffn2_w_down)

    def pack_layer(l, deps=()):
        land = lax.empty((N_CHIPS, 2, lay.rows, d), BF)
        for m, a in enumerate(mats):
            land = pack_weight(pos, land, m, l, a, deps=deps if m == 0 else ())
        return pack_weight(pos, land, lay.MIX_BLK, l, rows_major(w_in), extra=w_out)

    def gather_start(l, land):
        return split_start(f"gather_start_{l}", [land], 3, gather_plan)

    def gather_wait(l, started, after):
        ssem, rsem, bufs, _ = started
        return split_wait(f"gather_wait_{l}", ssem, rsem, bufs, after, gather_plan)

    sharded_small = (conv_w, lru_b_a, lru_b_x, lru_lambda)
    sshard = jnp.concatenate([a.reshape(-1, LANE) for a in sharded_small], axis=0)
    sfull = gather_small(sshard)
    small_full, off = [], 0
    for a in sharded_small:
        r = a.shape[0] * a.shape[1]
        piece = sfull[:, off:off + r].reshape((N_CHIPS,) + a.shape)
        small_full.append(jnp.moveaxis(piece, 0, 2).reshape(a.shape[0], a.shape[1], N_CHIPS * LANE))
        off += r
    conv_w_f, b_a_f, b_x_f, lam_f = small_full

    zrow = jnp.zeros((1, lw), F32)
    wblk_a = _block_diag_pairs(lru_w_a)
    wblk_x = _block_diag_pairs(lru_w_x)
    buckets, in_band = _band_buckets()
    onehot = (buckets.reshape(-1)[:, None] == jnp.arange(N_BUCKETS)[None, :]).astype(F32)
    bias = jnp.dot(rel_bias.T, onehot.T, precision=lax.Precision.HIGHEST).reshape(N_HEADS, BLOCK, 3 * BLOCK)
    bias = jnp.where(in_band[None], bias, NEG_INF)
    bias = bias.reshape(N_KV_HEADS, KV_GROUP, BLOCK, 3 * BLOCK).transpose(0, 3, 1, 2).reshape(N_KV_HEADS, 3 * BLOCK, KV_GROUP * BLOCK)
    kblk = 2 * lw // LANE

    def layer_small(l):
        cvec = jnp.concatenate([conv_w_f[l], jnp.zeros((8 - CONV_WIDTH, lw), F32)], axis=0)
        pvec = jnp.concatenate([conv_b[l][None], b_a_f[l], b_x_f[l], lam_f[l], zrow], axis=0)
        wblk = jnp.stack([wblk_a[l, 0], wblk_x[l, 0], wblk_a[l, 1], wblk_x[l, 1]]).astype(BF)
        sink = jnp.broadcast_to(jnp.repeat(attn_sink[l], BLOCK).reshape(N_KV_HEADS, 1, KV_GROUP * BLOCK),
                                (N_KV_HEADS, 8, KV_GROUP * BLOCK))
        return cvec, pvec, wblk, sink

    xs = x[0]
    wfull = [None] * depth
    first = gather_start(0, pack_layer(0, deps=(sfull,)))
    lands = {l: pack_layer(l, deps=(first[3],)) for l in range(1, depth)}
    wfull[0], = exchange_now("gather_handover_0", gather_wait(0, first, [xs] + list(lands.values())), 3, handover_plan)
    started = gather_start(1, lands[1]) if depth > 1 else None
    saved = []
    for l in range(depth):
        cvec, pvec, wblk, sink = layer_small(l)
        deps = (started[3],) if started is not None else ()
        x1, gate1, up1 = ffn_forward(xs, ffn1_norm[l][None], wfull[l], lay, 0, deps=deps)
        proj, qt, vt = mix_project(x1, mix_norm[l][None], wfull[l], lay, lw, att)
        y_rec, hs = lru_forward(proj, cvec, pvec, wblk, lw)
        y_att = attention_forward(qt, proj, vt, bias, sink, kblk)
        x2 = mix_output(x1, y_rec, y_att, lru_out_norm[l][None], attn_out_norm[l][:, None], wfull[l], lay)
        deps, handover = (), None
        if l + 1 < depth:
            land, = gather_wait(l + 1, started, [x2])
            started = gather_start(l + 2, lands[l + 2]) if l + 2 < depth else None
            handover = split_start(f"gather_handover_start_{l + 1}", [land], 3, handover_plan)
            deps = (handover[3],) + ((started[3],) if started is not None else ())
        x3, gate2, up2 = ffn_forward(x2, ffn2_norm[l][None], wfull[l], lay, 1, deps=deps)
        saved.append((xs, x1, x2, proj, qt, y_rec, hs, y_att, (gate1, up1), (gate2, up2)))
        xs = x3
        if handover is not None:
            wfull[l + 1], = split_wait(f"gather_handover_wait_{l + 1}", handover[0], handover[1], handover[2], [x3],
                                       handover_plan)

    dx, d_final, loss_tile = loss_head(xs, final_norm[None], loss_target[0])
    loss = lax.psum(loss_tile[0, 0], ("x", "y", "c"))

    layer_names = ["ffn1_norm", "mix_norm", "conv_w", "conv_b", "lru_w_a", "lru_b_a", "lru_w_x", "lru_b_x", "lru_lambda",
                   "attn_sink", "lru_out_norm", "attn_out_norm", "ffn2_norm"]
    dbias_total = jnp.zeros(bias.shape, F32)

    def ffn_back(xin, gain, dout, pre, gb, l, which, deps=()):
        dxo, dg, lhs, rhs = ffn_backward_dx(xin, gain, dout, *pre, wfull[l], lay, which, deps=deps)
        return dxo, dg[0], weight_grad_tn(lhs, rhs, gb, lay, 3 * which)

    def pair_start(l, gb, sb):
        lands = [lax.empty((N_CHIPS,) + gb.shape[2:], gb.dtype), lax.empty(sb.shape, sb.dtype)]
        return split_start(f"pair_start_{l}", [gb, sb] + lands, N_CHIPS + 1, pair_plan)

    def reduce_start(l, paired, after):
        gb, sb, p1, sp1 = split_wait(f"pair_wait_{l}", paired[0], paired[1], paired[2], after, pair_plan)
        cs = pair_sum(pos, gb, p1)
        ss = small_pair_sum(sb, sp1)
        lands = [lax.empty((3,) + cs.shape[1:], cs.dtype), lax.empty((N_CHIPS,) + ss.shape, ss.dtype)]
        return split_start(f"reduce_start_{l}", [cs, ss] + lands, 6, reduce_plan)

    def reduce_finish(l, started, after):
        ssem, rsem, bufs, _ = started
        cs, ss, p3, sp3 = split_wait(f"reduce_wait_{l}", ssem, rsem, bufs, after, reduce_plan)
        return chip_sum(pos, cs, p3), small_chip_sum(pos, ss, sp3)

    gf = [None] * depth
    small_sums = [None] * depth
    small_shapes = [None] * depth
    paired = None
    in_flight = None
    finals = {}
    tokens = []
    for l in reversed(range(depth)):
        x0, x1, x2, proj, qt, y_rec, hs, y_att, pre1, pre2 = saved[l]
        cvec, pvec, wblk, sink = layer_small(l)
        gb = lax.empty((N_CHIPS, 2, lay.rows, d), BF)
        part = {}
        dx, part["ffn2_norm"], gb = ffn_back(x2, ffn2_norm[l][None], dx, pre2, gb, l, 1, deps=tuple(tokens))
        deps = ()
        if paired is not None:
            in_flight = (paired[0], reduce_start(paired[0], paired[1], [dx, gb]))
            deps = (in_flight[1][3],)
        dyr, dya, dgr, dga, dwout = mix_output_backward(dx, y_rec, y_att, lru_out_norm[l][None], attn_out_norm[l][:, None],
                                                        wfull[l], lay, deps=deps)
        part["lru_out_norm"] = dgr[0]
        part["attn_out_norm"] = dga[:, 0]
        dq, dkv, dbias, dsink = attention_backward(qt, proj, y_att, dya, bias, sink, kblk)
        dbias_total = dbias_total + dbias
        part["attn_sink"] = jnp.sum(dsink[:, 0, :].reshape(N_HEADS, BLOCK), axis=1)
        dxr, dgt, dcv, dpv, dwb = lru_backward(proj, hs, dyr, cvec, pvec, wblk, lw)
        part["conv_w"] = dcv[:CONV_WIDTH]
        part["conv_b"] = dpv[0]
        part["lru_b_a"] = dpv[1:3]
        part["lru_b_x"] = dpv[3:5]
        part["lru_lambda"] = dpv[5:7]
        part["lru_w_a"] = _diag_blocks(jnp.stack([dwb[0], dwb[2]]))
        part["lru_w_x"] = _diag_blocks(jnp.stack([dwb[1], dwb[3]]))
        dx, dgm, gb = mix_project_backward(x1, mix_norm[l][None], dx, dxr, dgt, dq, dkv, dwout, wfull[l], gb, lay)
        part["mix_norm"] = dgm[0]
        dx, part["ffn1_norm"], gb = ffn_back(x0, ffn1_norm[l][None], dx, pre1, gb, l, 0)
        pieces = [part[n] for n in layer_names]
        if l == 0:
            dbias_heads = dbias_total.reshape(N_KV_HEADS, 3 * BLOCK, KV_GROUP, BLOCK).transpose(0, 2, 3, 1)
            d_rel_bias = jnp.dot(dbias_heads.reshape(N_HEADS, -1), onehot, precision=lax.Precision.HIGHEST).T
            pieces += [d_rel_bias, d_final[0]]
        small_shapes[l] = [p.shape for p in pieces]
        paired = (l, pair_start(l, gb, _pack_rows(pieces, 1024)))
        tokens = [paired[1][3]]
        if in_flight is not None:
            above = in_flight[0]
            half, small_sums[above] = reduce_finish(above, in_flight[1], [dx])
            finals[above] = split_start(f"final_start_{above}", [half], 1, final_plan)
            tokens.append(finals[above][3])
            in_flight = None
    grad_x = dx[None]

    weights = dict(ffn1_norm=ffn1_norm, ffn1_w_gate=ffn1_w_gate, ffn1_w_up=ffn1_w_up, ffn1_w_down=ffn1_w_down, mix_norm=mix_norm, w_in=w_in, conv_w=conv_w, conv_b=conv_b, lru_w_a=lru_w_a, lru_b_a=lru_b_a, lru_w_x=lru_w_x, lru_b_x=lru_b_x, lru_lambda=lru_lambda, attn_sink=attn_sink, rel_bias=rel_bias, lru_out_norm=lru_out_norm, attn_out_norm=attn_out_norm, w_out=w_out, ffn2_norm=ffn2_norm, ffn2_w_gate=ffn2_w_gate, ffn2_w_up=ffn2_w_up, ffn2_w_down=ffn2_w_down, final_norm=final_norm)
    m_in = dict(ffn1_norm=m_ffn1_norm, ffn1_w_gate=m_ffn1_w_gate, ffn1_w_up=m_ffn1_w_up, ffn1_w_down=m_ffn1_w_down, mix_norm=m_mix_norm, w_in=m_w_in, conv_w=m_conv_w, conv_b=m_conv_b, lru_w_a=m_lru_w_a, lru_b_a=m_lru_b_a, lru_w_x=m_lru_w_x, lru_b_x=m_lru_b_x, lru_lambda=m_lru_lambda, attn_sink=m_attn_sink, rel_bias=m_rel_bias, lru_out_norm=m_lru_out_norm, attn_out_norm=m_attn_out_norm, w_out=m_w_out, ffn2_norm=m_ffn2_norm, ffn2_w_gate=m_ffn2_w_gate, ffn2_w_up=m_ffn2_w_up, ffn2_w_down=m_ffn2_w_down, final_norm=m_final_norm)
    v_in = dict(ffn1_norm=v_ffn1_norm, ffn1_w_gate=v_ffn1_w_gate, ffn1_w_up=v_ffn1_w_up, ffn1_w_down=v_ffn1_w_down, mix_norm=v_mix_norm, w_in=v_w_in, conv_w=v_conv_w, conv_b=v_conv_b, lru_w_a=v_lru_w_a, lru_b_a=v_lru_b_a, lru_w_x=v_lru_w_x, lru_b_x=v_lru_b_x, lru_lambda=v_lru_lambda, attn_sink=v_attn_sink, rel_bias=v_rel_bias, lru_out_norm=v_lru_out_norm, attn_out_norm=v_attn_out_norm, w_out=v_w_out, ffn2_norm=v_ffn2_norm, ffn2_w_gate=v_ffn2_w_gate, ffn2_w_up=v_ffn2_w_up, ffn2_w_down=v_ffn2_w_down, final_norm=v_final_norm)
    order = list(weights)
    large = [(name, m, 0, lay.fh, m % 3 != 2) for m, name in
             enumerate(("ffn1_w_gate", "ffn1_w_up", "ffn1_w_down", "ffn2_w_gate", "ffn2_w_up", "ffn2_w_down"))]
    large += [("w_in", lay.MIX_BLK, 0, lay.ih, True), ("w_out", lay.MIX_BLK, lay.ih, lay.oh, False)]
    as_rows = {name: [rows_major(src[name]) if flip else src[name] for src in (weights, m_in, v_in)]
               for name, _, _, _, flip in large}
    stacked = {name: tuple(lax.empty(as_rows[name][0].shape, F32) for _ in range(4)) for name, *_ in large}

    def adamw_large(l, deps=()):
        for i, (name, blk, row_off, n_half, _) in enumerate(large):
            stacked[name] = adamw_layer(gf[l], blk, row_off, n_half, l, *as_rows[name], stacked[name],
                                        deps=deps if i == 0 else ())

    last = paired[0]
    crossing = reduce_start(last, paired[1], [dx])
    for l in sorted(finals):
        gf[l], = split_wait(f"final_wait_{l}", finals[l][0], finals[l][1], finals[l][2], [crossing[3]], final_plan)
        adamw_large(l, deps=(crossing[3],))
    ready = [buf for name, *_ in large for buf in stacked[name]] if depth > 1 else []
    half, small_sums[last] = reduce_finish(last, crossing, [dx] + ready)
    gf[last], = exchange_now(f"final_now_{last}", [half], 1, final_plan)
    adamw_large(last)

    per_layer = [_unpack_rows(small_sums[l], small_shapes[l]) for l in range(depth)]
    grads = {n: jnp.stack([per_layer[l][i] for l in range(depth)]) for i, n in enumerate(layer_names)}
    grads["rel_bias"], grads["final_norm"] = per_layer[0][len(layer_names):]
    for name in ("conv_w", "lru_b_a", "lru_b_x", "lru_lambda"):
        grads[name] = lax.dynamic_slice_in_dim(grads[name], k_chip * LANE, LANE, axis=2)
    delta, new_m, new_v = {}, {}, {}
    for name, _, _, _, flip in large:
        grads[name], delta[name], new_m[name], new_v[name] = [rows_major(a) if flip else a for a in stacked[name]]
    small = [n for n in order if n not in stacked]
    packed = [_pack_rows([src[n] for n in small], 1024) for src in (weights, grads, m_in, v_in)]
    outs = adamw(*packed)
    shapes = [weights[n].shape for n in small]
    for dst, buf in zip((delta, new_m, new_v), outs):
        dst.update(zip(small, _unpack_rows(buf, shapes)))

    return (loss, grad_x, *[grads[n] for n in order], *[delta[n] for n in order],
            *[new_m[n] for n in order], *[new_v[n] for n in order])
```

```python
import functools
import math

import jax
import jax.numpy as jnp
import numpy as np
from jax import lax
from jax.experimental import pallas as pl
from jax.experimental.pallas import tpu as pltpu

BF = jnp.bfloat16
F32 = jnp.float32
SDS = jax.ShapeDtypeStruct
MESH = pl.DeviceIdType.MESH
ANY = pl.BlockSpec(memory_space=pl.ANY)

N_CHIPS = 4
N_HEADS = 8
N_KV_HEADS = 2
KV_GROUP = N_HEADS // N_KV_HEADS
HEAD_DIM = 64
BLOCK = 128
WINDOW = 128
N_BUCKETS = 32
MAX_DISTANCE = 128
LRU_C = 8.0
CONV_WIDTH = 4
LANE = 128
SUBLANES = 8
SCAN_CHAINS = 8
EPS = 1e-6
FFN_RES = 0.5
NEG_INF = -1e30
ADAM_LR = 0.001
ADAM_B1 = 0.9
ADAM_B2 = 0.999
ADAM_EPS = 1e-08
ADAM_WD = 0.01
ADAM_STEP = 10
VMEM_LIMIT = 60000 * 1024
GELU_C = math.sqrt(2.0 / math.pi)


def dot_nn(a, b):
    return lax.dot_general(a, b, (((1,), (0,)), ((), ())), preferred_element_type=F32)


def dot_nt(a, b):
    return lax.dot_general(a, b, (((1,), (1,)), ((), ())), preferred_element_type=F32)


def dot_tn(a, b):
    return lax.dot_general(a, b, (((0,), (0,)), ((), ())), preferred_element_type=F32)


def _cparams(**kw):
    return pltpu.CompilerParams(vmem_limit_bytes=VMEM_LIMIT, **kw)


class Layout:
    MIX_BLK = 6

    def __init__(self, d_model, d_ff, d_in):
        self.fh = d_ff // (2 * N_CHIPS)
        self.ih = d_in // (2 * N_CHIPS)
        self.oh = d_model // (2 * N_CHIPS)
        assert self.ih + self.oh == self.fh, "w_in^T and w_out rows must fill one ffn-sized block"
        self.rows = 7 * self.fh


def _row_chunk(rows, target, step=16):
    best = rows
    for c in range(step, min(rows, target) + 1, step):
        if rows % c == 0:
            best = c
    return best


def _mesh_pos():
    return lax.axis_index("x"), lax.axis_index("y"), lax.axis_index("c")


def _rcopy(src, dst, ssem, rsem, dev):
    return pltpu.make_async_remote_copy(src_ref=src, dst_ref=dst, send_sem=ssem, recv_sem=rsem,
                                        device_id=dev, device_id_type=MESH)


HBM = pl.BlockSpec(memory_space=pltpu.HBM)
SEM = pl.BlockSpec(memory_space=pltpu.SEMAPHORE)
DATAFLOW = pltpu.SideEffectType.DATAFLOW_SIDE_EFFECTING


def _chip_peers():
    x, y, c = _mesh_pos()
    peers = [(1 - x, y), (x, 1 - y), (1 - x, 1 - y)]
    return x, y, c, 2 * x + y, [(px, py, 2 * px + py) for px, py in peers]


def split_start(name, bufs, n, plan):
    nb = len(bufs)

    def body(*refs):
        sends, _ = plan(refs[:nb], refs[nb], refs[nb + 1])
        for cp in sends:
            cp.start()
        refs[-1][...] = jnp.zeros_like(refs[-1])

    out = pl.pallas_call(
        body, name=name,
        out_shape=(pltpu.SemaphoreType.DMA((n,)), pltpu.SemaphoreType.DMA((n,)),
                   *[pltpu.HBM(b.shape, b.dtype) for b in bufs], SDS((8, LANE), F32)),
        in_specs=[HBM] * nb, out_specs=(SEM, SEM, *([HBM] * nb), pl.BlockSpec(memory_space=pltpu.VMEM)),
        input_output_aliases={i: 2 + i for i in range(nb)},
        compiler_params=pltpu.CompilerParams(has_side_effects=DATAFLOW),
    )(*[pltpu.with_memory_space_constraint(b, pltpu.HBM) for b in bufs])
    return out[0], out[1], list(out[2:2 + nb]), out[-1]


def split_wait(name, ssem, rsem, bufs, after, plan):
    nb = len(bufs)

    def body(*refs):
        sends, recvs = plan(refs[:nb], refs[nb], refs[nb + 1])
        for cp in recvs:
            cp.wait_recv()
        for cp in sends:
            cp.wait_send()

    out = pl.pallas_call(
        body, name=name, out_shape=tuple(pltpu.HBM(b.shape, b.dtype) for b in bufs),
        in_specs=[HBM] * nb + [SEM, SEM] + [ANY] * len(after), out_specs=tuple([HBM] * nb),
        input_output_aliases={i: i for i in range(nb)},
        compiler_params=pltpu.CompilerParams(has_side_effects=DATAFLOW),
    )(*bufs, ssem, rsem, *after)
    return list(out)


def gather_plan(refs, ssem, rsem, rows=None):
    land_ref, = refs
    _, _, c, k, peers = _chip_peers()
    part = (lambda a: a) if rows is None else (lambda a: a.at[pl.ds(rows[0], rows[1])])
    sends = [_rcopy(part(land_ref.at[k, c]), part(land_ref.at[k, c]), ssem.at[j], rsem.at[j], (px, py, c))
             for j, (px, py, _) in enumerate(peers)]
    recvs = [_rcopy(part(land_ref.at[kp, c]), part(land_ref.at[kp, c]), ssem.at[j], rsem.at[j], (px, py, c))
             for j, (px, py, kp) in enumerate(peers)]
    return sends, recvs


def reduce_plan(refs, ssem, rsem):
    cs_ref, ss_ref, p3_ref, sp3_ref = refs
    _, _, c, k, peers = _chip_peers()
    sends, recvs = [], []
    for j, (px, py, kp) in enumerate(peers):
        sends.append(_rcopy(cs_ref.at[kp], p3_ref.at[j], ssem.at[j], rsem.at[j], (px, py, c)))
        recvs.append(_rcopy(cs_ref.at[kp], p3_ref.at[j], ssem.at[j], rsem.at[j], (px, py, c)))
        sends.append(_rcopy(ss_ref, sp3_ref.at[k], ssem.at[3 + j], rsem.at[3 + j], (px, py, c)))
        recvs.append(_rcopy(ss_ref, sp3_ref.at[kp], ssem.at[3 + j], rsem.at[3 + j], (px, py, c)))
    return sends, recvs


def gather_small(sshard):
    def body(s_ref, sf_ref, lsem, ssem, rsem):
        _, _, c, k, peers = _chip_peers()
        own = pltpu.make_async_copy(s_ref, sf_ref.at[k], lsem)
        own.start()
        sends = [_rcopy(s_ref, sf_ref.at[k], ssem.at[j], rsem.at[j], (px, py, c)) for j, (px, py, _) in enumerate(peers)]
        recvs = [_rcopy(s_ref, sf_ref.at[kp], ssem.at[j], rsem.at[j], (px, py, c)) for j, (px, py, kp) in enumerate(peers)]
        for cp in sends:
            cp.start()
        for cp in recvs:
            cp.wait_recv()
        for cp in sends:
            cp.wait_send()
        own.wait()

    return pl.pallas_call(
        body, name="gather_small", out_shape=SDS((N_CHIPS,) + sshard.shape, sshard.dtype),
        in_specs=[ANY], out_specs=ANY,
        scratch_shapes=[pltpu.SemaphoreType.DMA, pltpu.SemaphoreType.DMA((3,)), pltpu.SemaphoreType.DMA((3,))],
    )(sshard)


def exchange_now(name, bufs, n, plan):
    nb = len(bufs)

    def body(*refs):
        sends, recvs = plan(refs[nb:2 * nb], refs[2 * nb], refs[2 * nb + 1])
        for cp in sends:
            cp.start()
        for cp in recvs:
            cp.wait_recv()
        for cp in sends:
            cp.wait_send()

    return list(pl.pallas_call(
        body, name=name, out_shape=tuple(SDS(b.shape, b.dtype) for b in bufs),
        in_specs=[ANY] * nb, out_specs=tuple([ANY] * nb), input_output_aliases={i: i for i in range(nb)},
        scratch_shapes=[pltpu.SemaphoreType.DMA((n,)), pltpu.SemaphoreType.DMA((n,))],
    )(*bufs))


def handover_plan(refs, ssem, rsem, rows=None):
    land_ref, = refs
    x, y, c, _, peers = _chip_peers()
    sib = (x, y, 1 - c)
    part = (lambda a: a) if rows is None else (lambda a: a.at[pl.ds(rows[0], rows[1])])
    sends = [_rcopy(part(land_ref.at[kp, c]), part(land_ref.at[kp, c]), ssem.at[j], rsem.at[j], sib)
             for j, (_, _, kp) in enumerate(peers)]
    recvs = [_rcopy(part(land_ref.at[kp, 1 - c]), part(land_ref.at[kp, 1 - c]), ssem.at[j], rsem.at[j], sib)
             for j, (_, _, kp) in enumerate(peers)]
    return sends, recvs


def pair_plan(refs, ssem, rsem):
    gb_ref, sb_ref, p_ref, sp_ref = refs
    x, y, c = _mesh_pos()
    sib = (x, y, 1 - c)
    n = gb_ref.shape[0]
    copies = [_rcopy(gb_ref.at[kk, 1 - c], p_ref.at[kk], ssem.at[kk], rsem.at[kk], sib) for kk in range(n)]
    copies.append(_rcopy(sb_ref, sp_ref, ssem.at[n], rsem.at[n], sib))
    return copies, copies


def final_plan(refs, ssem, rsem):
    gf_ref, = refs
    x, y, c = _mesh_pos()
    sib = (x, y, 1 - c)
    return ([_rcopy(gf_ref.at[c], gf_ref.at[c], ssem.at[0], rsem.at[0], sib)],
            [_rcopy(gf_ref.at[1 - c], gf_ref.at[1 - c], ssem.at[0], rsem.at[0], sib)])


def pair_sum(pos, gb, p1):
    n, _, rh, d = gb.shape
    cr = _row_chunk(rh, 1280)

    def body(pos_ref, a_ref, b_ref, o_ref):
        o_ref[...] = (a_ref[...].astype(F32) + b_ref[...].astype(F32)).astype(o_ref.dtype)

    return pl.pallas_call(
        body, name="pair_sum", out_shape=SDS((n, rh, d), gb.dtype),
        grid_spec=pltpu.PrefetchScalarGridSpec(
            num_scalar_prefetch=1, grid=(n, rh // cr),
            in_specs=[pl.BlockSpec((None, None, cr, d), lambda kk, r, pos: (kk, pos[1], r, 0)),
                      pl.BlockSpec((None, cr, d), lambda kk, r, pos: (kk, r, 0))],
            out_specs=pl.BlockSpec((None, cr, d), lambda kk, r, pos: (kk, r, 0))),
        compiler_params=_cparams(),
    )(pos, gb, p1)


def chip_sum(pos, cs, p3):
    n, rh, d = cs.shape
    cr = _row_chunk(rh, 640)

    def body(pos_ref, a_ref, b_ref, o_ref):
        acc = a_ref[...].astype(F32)
        for j in range(3):
            acc = acc + b_ref[j].astype(F32)
        o_ref[...] = acc

    return pl.pallas_call(
        body, name="chip_sum", out_shape=SDS((2, rh, d), F32),
        grid_spec=pltpu.PrefetchScalarGridSpec(
            num_scalar_prefetch=1, grid=(rh // cr,),
            in_specs=[pl.BlockSpec((None, cr, d), lambda r, pos: (pos[0], r, 0)),
                      pl.BlockSpec((3, cr, d), lambda r, pos: (0, r, 0))],
            out_specs=pl.BlockSpec((None, cr, d), lambda r, pos: (pos[1], r, 0))),
        compiler_params=_cparams(),
    )(pos, cs, p3)


def small_pair_sum(a, b):
    def body(a_ref, b_ref, o_ref):
        o_ref[...] = a_ref[...] + b_ref[...]

    return pl.pallas_call(body, name="small_pair_sum", out_shape=SDS(a.shape, a.dtype),
                          compiler_params=_cparams())(a, b)


def small_chip_sum(pos, own, p):
    ns, w = own.shape

    def body(pos_ref, own_ref, p0, p1, p2, p3, o_ref):
        k = pos_ref[0]
        acc = None
        for chip, ref in enumerate((p0, p1, p2, p3)):
            term = jnp.where(k == chip, own_ref[...], ref[...])
            acc = term if acc is None else acc + term
        o_ref[...] = acc

    def slot(chip):
        return pl.BlockSpec((None, ns, w), lambda i, pos: (jnp.where(pos[0] == chip, (chip + 1) % N_CHIPS, chip), 0, 0))

    return pl.pallas_call(
        body, name="small_chip_sum", out_shape=SDS(own.shape, own.dtype),
        grid_spec=pltpu.PrefetchScalarGridSpec(
            num_scalar_prefetch=1, grid=(1,),
            in_specs=[pl.BlockSpec((ns, w), lambda i, pos: (0, 0))] + [slot(chip) for chip in range(N_CHIPS)],
            out_specs=pl.BlockSpec((ns, w), lambda i, pos: (0, 0))),
        compiler_params=_cparams(),
    )(pos, own, p, p, p, p)


def _rms(x, g):
    rs = lax.rsqrt(jnp.mean(x * x, axis=-1, keepdims=True) + EPS)
    xh = x * rs
    return xh, rs, xh * g


def _rms_bwd(dy, xh, rs, g):
    dxh = dy * g
    dx = rs * (dxh - xh * jnp.mean(dxh * xh, axis=-1, keepdims=True))
    return dx, dy * xh


def _gelu(x):
    t = jnp.tanh(GELU_C * (x + 0.044715 * x * x * x))
    return 0.5 * x * (1.0 + t), t


def _gelu_grad(x, t):
    return 0.5 * (1.0 + t) + 0.5 * x * (1.0 - t * t) * GELU_C * (1.0 + 3.0 * 0.044715 * x * x)


def _shift_rows(v, s, n):
    if s == 0:
        return v
    t = lax.broadcasted_iota(jnp.int32, v.shape, 0)
    rolled = pltpu.roll(v, (-s) % n, 0)
    inside = (t < n - s) if s > 0 else (t >= -s)
    return jnp.where(inside, rolled, 0.0)


def _scan_rows(a_ref, u_ref, h_ref, acum_ref, reverse):
    s_len, w = a_ref.shape
    chunk = min(512, s_len)
    last = 0 if reverse else SUBLANES - 1

    def inside_vregs(ci, _):
        rows = pl.ds(pl.multiple_of(ci * chunk, chunk), chunk)
        a = a_ref[rows, :]
        u = u_ref[rows, :]
        pos = lax.broadcasted_iota(jnp.int32, (chunk, w), 0) % SUBLANES
        for dist in (1, 2, 4):
            ok = (pos < SUBLANES - dist) if reverse else (pos >= dist)
            shift = chunk - dist if reverse else dist
            u = u + a * jnp.where(ok, pltpu.roll(u, shift, 0), 0.0)
            a = a * jnp.where(ok, pltpu.roll(a, shift, 0), 1.0)
        h_ref[rows, :] = u
        acum_ref[rows, :] = a
        return 0

    lax.fori_loop(0, s_len // chunk, inside_vregs, 0)

    chains = max(1, min(SCAN_CHAINS, s_len // (8 * SUBLANES)))
    seg = s_len // chains
    nvreg = seg // SUBLANES

    def step(j, carry):
        jj = (nvreg - 1 - j) if reverse else j
        out = []
        for c, (hin, ain) in enumerate(carry):
            rows = pl.ds(pl.multiple_of(c * seg + jj * SUBLANES, SUBLANES), SUBLANES)
            acc = acum_ref[rows, :]
            h = h_ref[rows, :] + acc * hin
            acc = acc * ain
            h_ref[rows, :] = h
            acum_ref[rows, :] = acc
            out.append((jnp.broadcast_to(h[last:last + 1, :], h.shape), jnp.broadcast_to(acc[last:last + 1, :], acc.shape)))
        return tuple(out)

    init = tuple((jnp.zeros((SUBLANES, w), F32), jnp.ones((SUBLANES, w), F32)) for _ in range(chains))
    ends = lax.fori_loop(0, nvreg, step, init, unroll=min(2, nvreg))
    order = range(chains - 2, -1, -1) if reverse else range(1, chains)
    inflow = jnp.zeros((1, w), F32)
    for s in order:
        h, acc = ends[s + 1 if reverse else s - 1]
        inflow = h[0:1, :] + acc[0:1, :] * inflow
        rows = pl.ds(s * seg, seg)
        h_ref[rows, :] = h_ref[rows, :] + acum_ref[rows, :] * inflow


def _w_spec(rows_half, d, blk):
    return pl.BlockSpec((N_CHIPS, 2, rows_half, d), lambda *_: (0, 0, blk, 0), pipeline_mode=pl.Buffered(1))


def ffn_forward(x, gain, wfull, lay, which, deps=(), tm=512):
    s_len, d = x.shape
    tm = min(tm, s_len)
    f = 8 * lay.fh
    fc = f // 2

    def body(x_ref, g_ref, wg_ref, wu_ref, wd_ref, *rest):
        o_ref, gate_ref, up_ref = rest[len(deps):]
        x = x_ref[...]
        _, _, hn = _rms(x, g_ref[...])
        h = hn.astype(BF)
        y = jnp.zeros((tm, d), F32)
        for part in range(2):
            cols = slice(part * fc, (part + 1) * fc)
            gate = dot_nt(h, wg_ref[...].reshape(f, d)[cols])
            up = dot_nt(h, wu_ref[...].reshape(f, d)[cols])
            act = (gate * jax.nn.sigmoid(gate) * up).astype(BF)
            y = y + dot_nn(act, wd_ref[...].reshape(f, d)[cols])
            gate_ref[:, cols] = gate.astype(BF)
            up_ref[:, cols] = up.astype(BF)
        o_ref[...] = x + FFN_RES * y

    row = pl.BlockSpec((tm, d), lambda i: (i, 0))
    wide = pl.BlockSpec((tm, f), lambda i: (i, 0))
    return pl.pallas_call(
        body, name="ffn_forward", grid=(s_len // tm,),
        out_shape=(SDS((s_len, d), F32), SDS((s_len, f), BF), SDS((s_len, f), BF)),
        in_specs=[row, pl.BlockSpec((1, d), lambda i: (0, 0))]
        + [_w_spec(lay.fh, d, 3 * which + m) for m in range(3)] + [ANY] * len(deps),
        out_specs=(row, wide, wide), compiler_params=_cparams(),
    )(x, gain, wfull, wfull, wfull, *deps)


def ffn_backward_dx(x, gain, dout, gate_bf, up_bf, wfull, lay, which, deps=(), tm=256):
    s_len, d = x.shape
    tm = min(tm, s_len)
    f = 8 * lay.fh
    fc = f // 2
    nt = s_len // tm

    def body(x_ref, g_ref, do_ref, gate_ref, up_ref, wg_ref, wu_ref, wd_ref, *rest):
        dx_ref, dg_ref, lhs_ref, rhs_ref = rest[len(deps):]
        dgate_ref, dup_ref, act_ref = lhs_ref.at[0], lhs_ref.at[1], lhs_ref.at[2]
        h_ref, df_ref = rhs_ref.at[0], rhs_ref.at[1]
        x = x_ref[...]
        g = g_ref[...]
        xh, rs, hn = _rms(x, g)
        h = hn.astype(BF)
        do = do_ref[...]
        df = (FFN_RES * do).astype(BF)
        dh = jnp.zeros((tm, d), F32)
        for part in range(2):
            cols = slice(part * fc, (part + 1) * fc)
            wg = wg_ref[...].reshape(f, d)[cols]
            wu = wu_ref[...].reshape(f, d)[cols]
            gate = gate_ref[:, cols].astype(F32)
            up = up_ref[:, cols].astype(F32)
            sg = jax.nn.sigmoid(gate)
            silu = gate * sg
            dact = dot_nt(df, wd_ref[...].reshape(f, d)[cols])
            dup = (dact * silu).astype(BF)
            dgate = (dact * up * (sg * (1.0 + gate * (1.0 - sg)))).astype(BF)
            dh = dh + dot_nn(dgate, wg) + dot_nn(dup, wu)
            dgate_ref[:, cols] = dgate
            dup_ref[:, cols] = dup
            act_ref[:, cols] = (silu * up).astype(BF)
        dxn, dgrow = _rms_bwd(dh, xh, rs, g)
        dx_ref[...] = do + dxn

        @pl.when(pl.program_id(0) == 0)
        def _():
            dg_ref[...] = jnp.zeros_like(dg_ref)

        dg_ref[...] += jnp.sum(dgrow, axis=0, keepdims=True)
        h_ref[...] = h
        df_ref[...] = df

    row = pl.BlockSpec((tm, d), lambda i: (i, 0))
    wide = pl.BlockSpec((tm, f), lambda i: (i, 0))
    vec = pl.BlockSpec((1, d), lambda i: (0, 0))
    return pl.pallas_call(
        body, name="ffn_backward_dx", grid=(nt,),
        out_shape=(SDS((s_len, d), F32), SDS((1, d), F32), SDS((3, s_len, f), BF), SDS((2, s_len, d), BF)),
        in_specs=[row, vec, row, wide, wide] + [_w_spec(lay.fh, d, 3 * which + m) for m in range(3)] + [ANY] * len(deps),
        out_specs=(row, vec, pl.BlockSpec((3, tm, f), lambda i: (0, i, 0)), pl.BlockSpec((2, tm, d), lambda i: (0, i, 0))),
        compiler_params=_cparams(),
    )(x, gain, dout, gate_bf, up_bf, wfull, wfull, wfull, *deps)


def weight_grad_tn(lhs, rhs, gb, lay, blk0, tk=2048):
    nmat, s_len, f = lhs.shape
    tk = min(tk, s_len)
    d = rhs.shape[2]
    fc = f // 2
    nk = s_len // tk

    def body(a_ref, b_ref, gb_ref, o_ref, acc):
        kt = pl.program_id(2)

        @pl.when(kt == 0)
        def _():
            acc[...] = jnp.zeros_like(acc)

        acc[...] += dot_tn(a_ref[...], b_ref[...])

        @pl.when(kt == nk - 1)
        def _():
            for p in range(2):
                for q in range(2):
                    o_ref[p, q] = acc[pl.ds((2 * p + q) * lay.fh, lay.fh), :].astype(o_ref.dtype)

    return pl.pallas_call(
        body, name="weight_grad_tn", grid=(nmat, 2, nk), out_shape=SDS(gb.shape, gb.dtype),
        in_specs=[pl.BlockSpec((None, tk, fc), lambda m, j, kt: (m, kt, j)),
                  pl.BlockSpec((None, tk, d), lambda m, j, kt: (jnp.where(m == nmat - 1, 1, 0), kt, 0)), ANY],
        out_specs=pl.BlockSpec((2, 2, lay.fh, d), lambda m, j, kt: (j, 0, blk0 + m, 0)),
        scratch_shapes=[pltpu.VMEM((fc, d), F32)],
        input_output_aliases={2: 0}, compiler_params=_cparams(),
    )(lhs, rhs, gb)


def _lane_blocks(v):
    return [v[:, j * LANE:(j + 1) * LANE] for j in range(v.shape[1] // LANE)]


def _join_lane_blocks(ref):
    return jnp.concatenate([ref[j] for j in range(ref.shape[0])], axis=1)


def _cbm_spec(nblk, rows, first=0):
    return pl.BlockSpec((nblk, rows, LANE), lambda i: (first // nblk, i, 0))


def mix_project(x, gain, wfull, lay, lw, att, tm=512):
    s_len, d = x.shape
    tm = min(tm, s_len)
    d_in = 8 * lay.ih
    kvw = (d_in - 2 * lw - att) // 2
    ncol = (2 * lw + 2 * kvw) // LANE

    def body(x_ref, g_ref, w_ref, o_ref, qt_ref, vt_ref):
        _, _, hn = _rms(x_ref[...], g_ref[...])
        h = hn.astype(BF)
        w = w_ref[:, :, :lay.ih, :].reshape(d_in, d)
        pieces = _lane_blocks(dot_nt(h, w[:2 * lw])) + _lane_blocks(dot_nt(h, w[2 * lw + att:]))
        for j, piece in enumerate(pieces):
            o_ref[j] = piece
        qt_ref[...] = dot_nt(w[2 * lw:2 * lw + att], h)
        vt_ref[...] = dot_nt(w[2 * lw + att + kvw:], h)

    return pl.pallas_call(
        body, name="mix_project", grid=(s_len // tm,),
        out_shape=(SDS((ncol, s_len, LANE), F32), SDS((att, s_len), F32), SDS((kvw, s_len), F32)),
        in_specs=[pl.BlockSpec((tm, d), lambda i: (i, 0)), pl.BlockSpec((1, d), lambda i: (0, 0)),
                  _w_spec(lay.fh, d, lay.MIX_BLK)],
        out_specs=(_cbm_spec(ncol, tm), pl.BlockSpec((att, tm), lambda i: (0, i)), pl.BlockSpec((kvw, tm), lambda i: (0, i))),
        compiler_params=_cparams(),
    )(x, gain, wfull)


def mix_project_backward(x, gain, dout, dxr, dgt, dqt, dkv, dwout, wfull, gb, lay, tm=512):
    s_len, d = x.shape
    tm = min(tm, s_len)
    d_in = 8 * lay.ih
    nt = s_len // tm
    kvw = dkv.shape[1]
    att = dqt.shape[0]
    nlru = (dxr.shape[0] + dgt.shape[0]) * LANE

    def body(x_ref, g_ref, do_ref, dxr_ref, dgt_ref, dqt_ref, dkv_ref, dwo_ref, w_ref, gb_ref, dx_ref, dg_ref, o_ref, acc):
        i = pl.program_id(0)
        g = g_ref[...]
        xh, rs, hn = _rms(x_ref[...], g)
        h = hn.astype(BF)
        w = w_ref[:, :, :lay.ih, :].reshape(d_in, d)
        dlru = jnp.concatenate([_join_lane_blocks(dxr_ref), _join_lane_blocks(dgt_ref)], axis=1).astype(BF)
        dqt = dqt_ref[...].astype(BF)
        dkv = dkv_ref[...].astype(BF)
        dh = dot_nn(dlru, w[:nlru]) + dot_tn(dqt, w[nlru:nlru + att]) + dot_nn(dkv, w[nlru + att:])
        dxn, dgrow = _rms_bwd(dh, xh, rs, g)
        dx_ref[...] = do_ref[...] + dxn

        @pl.when(i == 0)
        def _():
            dg_ref[...] = jnp.zeros_like(dg_ref)
            acc[...] = jnp.zeros_like(acc)

        dg_ref[...] += jnp.sum(dgrow, axis=0, keepdims=True)
        acc[0:nlru, :] += dot_tn(dlru, h)
        acc[nlru:nlru + att, :] += dot_nn(dqt, h)
        acc[nlru + att:, :] += dot_tn(dkv, h)

        @pl.when(i == nt - 1)
        def _():
            for p in range(N_CHIPS):
                for q in range(2):
                    o_ref[p, q, :lay.ih, :] = acc[pl.ds((2 * p + q) * lay.ih, lay.ih), :].astype(o_ref.dtype)
            o_ref[:, :, lay.ih:, :] = dwo_ref[...]

    row = pl.BlockSpec((tm, d), lambda i: (i, 0))
    vec = pl.BlockSpec((1, d), lambda i: (0, 0))
    return pl.pallas_call(
        body, name="mix_project_backward", grid=(nt,),
        out_shape=(SDS((s_len, d), F32), SDS((1, d), F32), SDS(gb.shape, gb.dtype)),
        in_specs=[row, vec, row, _cbm_spec(dxr.shape[0], tm), _cbm_spec(dgt.shape[0], tm),
                  pl.BlockSpec((att, tm), lambda i: (0, i)), pl.BlockSpec((tm, kvw), lambda i: (i, 0)),
                  pl.BlockSpec(dwout.shape, lambda i: (0, 0, 0, 0)), _w_spec(lay.fh, d, lay.MIX_BLK), ANY],
        out_specs=(row, vec, pl.BlockSpec((N_CHIPS, 2, lay.fh, d), lambda i: (0, 0, lay.MIX_BLK, 0))),
        scratch_shapes=[pltpu.VMEM((d_in, d), F32)],
        input_output_aliases={9: 2}, compiler_params=_cparams(),
    )(x, gain, dout, dxr, dgt, dqt, dkv, dwout, wfull, gb)


def _lru_gates(xc, wb_ref, pv_ref, direction):
    xcb = xc.astype(BF)
    r = jax.nn.sigmoid(dot_nn(xcb, wb_ref[2 * direction]) + pv_ref[1 + direction:2 + direction, :])
    i = jax.nn.sigmoid(dot_nn(xcb, wb_ref[2 * direction + 1]) + pv_ref[3 + direction:4 + direction, :])
    lam = pv_ref[5 + direction:6 + direction, :]
    sp = jnp.maximum(-lam, 0.0) + jnp.log(1.0 + jnp.exp(-jnp.abs(lam)))
    a = jnp.exp(-LRU_C * sp * r)
    mult = jnp.sqrt(1.0 - a * a)
    return xcb, r, i, a, mult, sp


def _conv_rows(xr, cv_ref, bias, n):
    acc = bias + cv_ref[0:1, :] * _shift_rows(xr, -2, n)
    for j in range(1, CONV_WIDTH):
        acc = acc + cv_ref[j:j + 1, :] * _shift_rows(xr, j - 2, n)
    return acc


def lru_forward(proj, cvec, pvec, wblk, lw, deps=(), ch=512):
    s_len = proj.shape[1]
    ncb = lw // LANE
    ch = min(ch, s_len)
    nchunk = s_len // ch

    def body(xr_ref, gt_ref, cv_ref, pv_ref, wb_ref, *rest):
        y_ref, hs_ref, xc_s, a_s, u_s, acum_s = rest[len(deps):]
        xc_s[...] = _conv_rows(xr_ref[...], cv_ref, pv_ref[0:1, :], s_len)
        for direction in range(2):
            def fill(ci, _):
                rows = pl.ds(pl.multiple_of(ci * ch, ch), ch)
                xc = xc_s[rows, :]
                _, _, i, a, mult, _ = _lru_gates(xc, wb_ref, pv_ref, direction)
                a_s[rows, :] = a
                u_s[rows, :] = mult * (i * xc)
                return 0

            lax.fori_loop(0, nchunk, fill, 0)
            _scan_rows(a_s, u_s, hs_ref.at[direction], acum_s, reverse=direction == 1)

        def out(ci, _):
            rows = pl.ds(pl.multiple_of(ci * ch, ch), ch)
            gl, _ = _gelu(gt_ref[rows, :])
            y_ref[rows, :] = gl * (hs_ref[0, rows, :] + hs_ref[1, rows, :])
            return 0

        lax.fori_loop(0, nchunk, out, 0)

    col = lambda off: pl.BlockSpec((None, s_len, LANE), lambda cb: (off + cb, 0, 0))
    return pl.pallas_call(
        body, name="lru_forward", grid=(ncb,),
        out_shape=(SDS((ncb, s_len, LANE), F32), SDS((2, ncb, s_len, LANE), F32)),
        in_specs=[col(0), col(ncb), pl.BlockSpec((8, LANE), lambda cb: (0, cb)), pl.BlockSpec((8, LANE), lambda cb: (0, cb)),
                  pl.BlockSpec((4, None, LANE, LANE), lambda cb: (0, cb, 0, 0))] + [ANY] * len(deps),
        out_specs=(col(0), pl.BlockSpec((2, None, s_len, LANE), lambda cb: (0, cb, 0, 0))),
        scratch_shapes=[pltpu.VMEM((s_len, LANE), F32)] * 4, compiler_params=_cparams(),
    )(proj, proj, cvec, pvec, wblk, *deps)


def lru_backward(proj, hs, dy, cvec, pvec, wblk, lw, ch=512):
    s_len = proj.shape[1]
    ncb = lw // LANE
    ch = min(ch, s_len)
    nchunk = s_len // ch

    def body(xr_ref, gt_ref, hs_ref, dy_ref, cv_ref, pv_ref, wb_ref, dxr_ref, dgt_ref, dcv_ref, dpv_ref, dwb_ref,
             xc_s, a_s, dh_s, lam_s, hp_s, dxc_s, acum_s):
        xr = xr_ref[...]
        xc_s[...] = _conv_rows(xr, cv_ref, pv_ref[0:1, :], s_len)
        dxc_s[...] = jnp.zeros_like(dxc_s)
        dpv_ref[...] = jnp.zeros_like(dpv_ref)
        dwb_ref[...] = jnp.zeros_like(dwb_ref)

        def head(ci, _):
            rows = pl.ds(pl.multiple_of(ci * ch, ch), ch)
            gt = gt_ref[rows, :]
            gl, t = _gelu(gt)
            dy = dy_ref[rows, :]
            dh_s[rows, :] = dy * gl
            dgt_ref[rows, :] = dy * (hs_ref[0, rows, :] + hs_ref[1, rows, :]) * _gelu_grad(gt, t)
            return 0

        lax.fori_loop(0, nchunk, head, 0)

        for direction in range(2):
            def fill(ci, _):
                rows = pl.ds(pl.multiple_of(ci * ch, ch), ch)
                _, _, _, a, _, _ = _lru_gates(xc_s[rows, :], wb_ref, pv_ref, direction)
                a_s[rows, :] = a
                return 0

            lax.fori_loop(0, nchunk, fill, 0)
            toward = 1 if direction == 0 else -1
            hp_s[...] = _shift_rows(a_s[...], toward, s_len)
            _scan_rows(hp_s, dh_s, lam_s, acum_s, reverse=direction == 0)
            hp_s[...] = _shift_rows(hs_ref[direction], -toward, s_len)

            def grads(ci, _):
                rows = pl.ds(pl.multiple_of(ci * ch, ch), ch)
                xc = xc_s[rows, :]
                xcb, r, i, a, mult, sp = _lru_gates(xc, wb_ref, pv_ref, direction)
                du = lam_s[rows, :]
                da = du * hp_s[rows, :]
                dmult = du * i * xc
                di = du * mult * xc
                dlog_a = (da - dmult * a / mult) * a
                dr = dlog_a * (-LRU_C * sp)
                dza = dr * r * (1.0 - r)
                dzx = di * i * (1.0 - i)
                dzab = dza.astype(BF)
                dzxb = dzx.astype(BF)
                dxc_s[rows, :] += (du * mult * i + dot_nt(dzab, wb_ref[2 * direction])
                                   + dot_nt(dzxb, wb_ref[2 * direction + 1]))
                dwb_ref[2 * direction] += dot_tn(xcb, dzab)
                dwb_ref[2 * direction + 1] += dot_tn(xcb, dzxb)
                dpv_ref[1 + direction:2 + direction, :] += jnp.sum(dza, axis=0, keepdims=True)
                dpv_ref[3 + direction:4 + direction, :] += jnp.sum(dzx, axis=0, keepdims=True)
                dpv_ref[5 + direction:6 + direction, :] += jnp.sum(dlog_a * (-LRU_C * r), axis=0, keepdims=True)
                return 0

            lax.fori_loop(0, nchunk, grads, 0)

        for direction in range(2):
            lam = pv_ref[5 + direction:6 + direction, :]
            dpv_ref[5 + direction:6 + direction, :] = dpv_ref[5 + direction:6 + direction, :] * (-jax.nn.sigmoid(-lam))
        dxc = dxc_s[...]
        dpv_ref[0:1, :] = jnp.sum(dxc, axis=0, keepdims=True)
        dxr = cv_ref[0:1, :] * _shift_rows(dxc, 2, s_len)
        for j in range(1, CONV_WIDTH):
            dxr = dxr + cv_ref[j:j + 1, :] * _shift_rows(dxc, 2 - j, s_len)
        dxr_ref[...] = dxr
        dcv_ref[...] = jnp.zeros_like(dcv_ref)
        for j in range(CONV_WIDTH):
            dcv_ref[j:j + 1, :] = jnp.sum(dxc * _shift_rows(xr, j - 2, s_len), axis=0, keepdims=True)

    col = lambda off: pl.BlockSpec((None, s_len, LANE), lambda cb: (off + cb, 0, 0))
    own = col(0)
    small = pl.BlockSpec((8, LANE), lambda cb: (0, cb))
    wspec = pl.BlockSpec((4, None, LANE, LANE), lambda cb: (0, cb, 0, 0))
    return pl.pallas_call(
        body, name="lru_backward", grid=(ncb,),
        out_shape=(SDS((ncb, s_len, LANE), F32), SDS((ncb, s_len, LANE), F32), SDS((8, lw), F32), SDS((8, lw), F32),
                   SDS(wblk.shape, F32)),
        in_specs=[col(0), col(ncb), pl.BlockSpec((2, None, s_len, LANE), lambda cb: (0, cb, 0, 0)), own, small, small, wspec],
        out_specs=(own, own, small, small, wspec),
        scratch_shapes=[pltpu.VMEM((s_len, LANE), F32)] * 7, compiler_params=_cparams(),
    )(proj, proj, hs, dy, cvec, pvec, wblk)


def _window_specs(s_len, first, width=None):
    nb = s_len // BLOCK
    where = (lambda n: jnp.maximum(n - 1, 0), lambda n: n, lambda n: jnp.minimum(n + 1, nb - 1))
    if width is None:
        return [pl.BlockSpec((None, BLOCK, LANE), lambda n, f=f: (first, f(n), 0)) for f in where]
    return [pl.BlockSpec((width, BLOCK), lambda n, f=f: (0, f(n))) for f in where]


def _stack_heads(v, kh):
    return jnp.concatenate([v[(kh * KV_GROUP + g) * HEAD_DIM:(kh * KV_GROUP + g + 1) * HEAD_DIM, :]
                            for g in range(KV_GROUP)], axis=1)


def _unstack_heads(ref, kh, v):
    for g in range(KV_GROUP):
        h = kh * KV_GROUP + g
        ref[h * HEAD_DIM:(h + 1) * HEAD_DIM, :] = v[:, g * BLOCK:(g + 1) * BLOCK]


def _key_exists(n, nb):
    j = lax.broadcasted_iota(jnp.int32, (3 * BLOCK, 1), 0)
    return ((n > 0) | (j >= BLOCK)) & ((n < nb - 1) | (j < 2 * BLOCK))


def _attn_probs(qs, kcat, bias_g, sink_g, key_ok):
    logits = jnp.where(key_ok, dot_nn(kcat, qs) + bias_g, NEG_INF)
    m = jnp.maximum(jnp.max(logits, axis=0, keepdims=True), sink_g)
    p = jnp.exp(logits - m)
    es = jnp.exp(sink_g - m)
    inv = 1.0 / (jnp.sum(p, axis=0, keepdims=True) + es)
    return p * inv, es * inv


def attention_forward(qt, proj, vt, bias, sink, kblk):
    att, s_len = qt.shape
    kvw = vt.shape[0]
    nb = s_len // BLOCK

    def body(q_ref, kp_ref, kc_ref, kn_ref, vp_ref, vc_ref, vn_ref, b_ref, s_ref, o_ref):
        n = pl.program_id(0)
        q = q_ref[...]
        key_ok = _key_exists(n, nb)
        kall = jnp.concatenate([kp_ref[...], kc_ref[...], kn_ref[...]], axis=0).astype(BF)
        vall = jnp.concatenate([vp_ref[...], vc_ref[...], vn_ref[...]], axis=1).astype(BF)
        for kh in range(N_KV_HEADS):
            qs = (_stack_heads(q, kh) * (HEAD_DIM ** -0.5)).astype(BF)
            p, _ = _attn_probs(qs, kall[:, kh * HEAD_DIM:(kh + 1) * HEAD_DIM], b_ref[kh], s_ref[kh, 0:1, :], key_ok)
            _unstack_heads(o_ref, kh, dot_nn(vall[kh * HEAD_DIM:(kh + 1) * HEAD_DIM, :], p.astype(BF)))

    blk = pl.BlockSpec((att, BLOCK), lambda n: (0, n))
    return pl.pallas_call(
        body, name="attention_forward", grid=(nb,), out_shape=SDS((att, s_len), F32),
        in_specs=[blk] + _window_specs(s_len, kblk) + _window_specs(s_len, 0, kvw)
        + [pl.BlockSpec(bias.shape, lambda n: (0, 0, 0)), pl.BlockSpec(sink.shape, lambda n: (0, 0, 0))],
        out_specs=blk, compiler_params=_cparams(),
    )(qt, proj, proj, proj, vt, vt, vt, bias, sink)


def attention_backward(qt, proj, y_att, dy, bias, sink, kblk):
    att, s_len = qt.shape
    nb = s_len // BLOCK
    kvw = N_KV_HEADS * HEAD_DIM

    def body(q_ref, kp_ref, kc_ref, kn_ref, vp_ref, vc_ref, vn_ref, o_ref, do_ref, b_ref, s_ref,
             dq_ref, dkv_ref, db_ref, ds_ref):
        n = pl.program_id(0)

        @pl.when(n == 0)
        def _():
            dkv_ref[...] = jnp.zeros_like(dkv_ref)
            db_ref[...] = jnp.zeros_like(db_ref)
            ds_ref[...] = jnp.zeros_like(ds_ref)

        q = q_ref[...]
        o = o_ref[...]
        do = do_ref[...]
        kall = jnp.concatenate([kp_ref[...], kc_ref[...], kn_ref[...]], axis=0).astype(BF)
        vall = jnp.concatenate([vp_ref[...], vc_ref[...], vn_ref[...]], axis=0).astype(BF)
        key_ok = _key_exists(n, nb)
        dks, dvs = [], []
        for kh in range(N_KV_HEADS):
            kcat = kall[:, kh * HEAD_DIM:(kh + 1) * HEAD_DIM]
            vcat = vall[:, kh * HEAD_DIM:(kh + 1) * HEAD_DIM]
            qs = (_stack_heads(q, kh) * (HEAD_DIM ** -0.5)).astype(BF)
            p, ps = _attn_probs(qs, kcat, b_ref[kh], s_ref[kh, 0:1, :], key_ok)
            dos = _stack_heads(do, kh)
            dosb = dos.astype(BF)
            delta = jnp.sum(dos * _stack_heads(o, kh), axis=0, keepdims=True)
            dlog = p * (dot_nn(vcat, dosb) - delta)
            dlogb = dlog.astype(BF)
            db_ref[kh] += dlog
            ds_ref[kh] += jnp.broadcast_to(-ps * delta, ds_ref.shape[1:])
            _unstack_heads(dq_ref, kh, dot_tn(kcat, dlogb) * (HEAD_DIM ** -0.5))
            dks.append(dot_nt(dlogb, qs))
            dvs.append(dot_nt(p.astype(BF), dosb))
        dkv = jnp.concatenate(dks + dvs, axis=1)
        starts = [jnp.maximum(n - 1, 0), n, jnp.minimum(n + 1, nb - 1)]
        for b, st in enumerate(starts):
            rows = pl.ds(pl.multiple_of(st * BLOCK, BLOCK), BLOCK)
            dkv_ref[rows, :] += dkv[b * BLOCK:(b + 1) * BLOCK, :]

    blk = pl.BlockSpec((att, BLOCK), lambda n: (0, n))
    whole = lambda a: pl.BlockSpec(a.shape, lambda n: (0, 0, 0))
    return pl.pallas_call(
        body, name="attention_backward", grid=(nb,),
        out_shape=(SDS((att, s_len), F32), SDS((s_len, 2 * kvw), F32), SDS(bias.shape, F32), SDS(sink.shape, F32)),
        in_specs=[blk] + _window_specs(s_len, kblk) + _window_specs(s_len, kblk + 1) + [blk, blk, whole(bias), whole(sink)],
        out_specs=(blk, pl.BlockSpec((s_len, 2 * kvw), lambda n: (0, 0)), whole(bias), whole(sink)),
        compiler_params=_cparams(),
    )(qt, proj, proj, proj, proj, proj, proj, y_att, dy, bias, sink)


def _rms_cols(x, g):
    rs = lax.rsqrt(jnp.mean(x * x, axis=0, keepdims=True) + EPS)
    xh = x * rs
    return xh, rs, xh * g


def _rms_cols_bwd(dy, xh, rs, g):
    dxh = dy * g
    dx = rs * (dxh - xh * jnp.mean(dxh * xh, axis=0, keepdims=True))
    return dx, dy * xh


def mix_output(x, y_rec, y_att, g_rec, g_att, wfull, lay, tm=512):
    s_len, d = x.shape
    tm = min(tm, s_len)
    lw = y_rec.shape[0] * LANE
    att = y_att.shape[0]

    def body(x_ref, yr_ref, ya_ref, gr_ref, ga_ref, w_ref, o_ref):
        _, _, nr = _rms(_join_lane_blocks(yr_ref), gr_ref[...])
        _, _, na = _rms_cols(ya_ref[...], ga_ref[...])
        w = w_ref[:, :, lay.ih:, :].reshape(d, d)
        o_ref[...] = x_ref[...] + dot_nn(nr.astype(BF), w[:lw]) + dot_tn(na.astype(BF), w[lw:])

    row = pl.BlockSpec((tm, d), lambda i: (i, 0))
    return pl.pallas_call(
        body, name="mix_output", grid=(s_len // tm,), out_shape=SDS((s_len, d), F32),
        in_specs=[row, _cbm_spec(lw // LANE, tm), pl.BlockSpec((att, tm), lambda i: (0, i)),
                  pl.BlockSpec((1, lw), lambda i: (0, 0)), pl.BlockSpec((att, 1), lambda i: (0, 0)),
                  _w_spec(lay.fh, d, lay.MIX_BLK)],
        out_specs=row, compiler_params=_cparams(),
    )(x, y_rec, y_att, g_rec, g_att, wfull)


def mix_output_backward(dout, y_rec, y_att, g_rec, g_att, wfull, lay, deps=(), tm=512):
    s_len, d = dout.shape
    tm = min(tm, s_len)
    lw = y_rec.shape[0] * LANE
    att = y_att.shape[0]
    nt = s_len // tm

    def body(do_ref, yr_ref, ya_ref, gr_ref, ga_ref, w_ref, *rest):
        dyr_ref, dya_ref, dgr_ref, dga_ref, o_ref, acc = rest[len(deps):]
        i = pl.program_id(0)
        gr = gr_ref[...]
        ga = ga_ref[...]
        xhr, rsr, nr = _rms(_join_lane_blocks(yr_ref), gr)
        xha, rsa, na = _rms_cols(ya_ref[...], ga)
        dob = do_ref[...].astype(BF)
        w = w_ref[:, :, lay.ih:, :].reshape(d, d)
        dyr, dgr_row = _rms_bwd(dot_nt(dob, w[:lw]), xhr, rsr, gr)
        dya, dga_col = _rms_cols_bwd(dot_nt(w[lw:], dob), xha, rsa, ga)
        for j, piece in enumerate(_lane_blocks(dyr)):
            dyr_ref[j] = piece
        dya_ref[...] = dya

        @pl.when(i == 0)
        def _():
            dgr_ref[...] = jnp.zeros_like(dgr_ref)
            dga_ref[...] = jnp.zeros_like(dga_ref)
            acc[...] = jnp.zeros_like(acc)

        dgr_ref[...] += jnp.sum(dgr_row, axis=0, keepdims=True)
        dga_ref[...] += jnp.sum(dga_col, axis=1, keepdims=True)
        acc[0:lw, :] += dot_tn(nr.astype(BF), dob)
        acc[lw:, :] += dot_nn(na.astype(BF), dob)

        @pl.when(i == nt - 1)
        def _():
            for p in range(N_CHIPS):
                for q in range(2):
                    o_ref[p, q] = acc[pl.ds((2 * p + q) * lay.oh, lay.oh), :].astype(o_ref.dtype)

    row = pl.BlockSpec((tm, d), lambda i: (i, 0))
    return pl.pallas_call(
        body, name="mix_output_backward", grid=(nt,),
        out_shape=(SDS(y_rec.shape, F32), SDS(y_att.shape, F32), SDS((1, lw), F32), SDS((att, 1), F32),
                   SDS((N_CHIPS, 2, lay.oh, d), BF)),
        in_specs=[row, _cbm_spec(lw // LANE, tm), pl.BlockSpec((att, tm), lambda i: (0, i)),
                  pl.BlockSpec((1, lw), lambda i: (0, 0)), pl.BlockSpec((att, 1), lambda i: (0, 0)),
                  _w_spec(lay.fh, d, lay.MIX_BLK)] + [ANY] * len(deps),
        out_specs=(_cbm_spec(lw // LANE, tm), pl.BlockSpec((att, tm), lambda i: (0, i)),
                   pl.BlockSpec((1, lw), lambda i: (0, 0)), pl.BlockSpec((att, 1), lambda i: (0, 0)),
                   pl.BlockSpec((N_CHIPS, 2, lay.oh, d), lambda i: (0, 0, 0, 0))),
        scratch_shapes=[pltpu.VMEM((d, d), F32)], compiler_params=_cparams(),
    )(dout, y_rec, y_att, g_rec, g_att, wfull, *deps)


def loss_head(x, gain, target, tm=512):
    s_len, d = x.shape
    tm = min(tm, s_len)

    def body(x_ref, g_ref, t_ref, dx_ref, dg_ref, loss_ref):
        g = g_ref[...]
        xh, rs, y = _rms(x_ref[...], g)
        err = y - t_ref[...]

        @pl.when(pl.program_id(0) == 0)
        def _():
            dg_ref[...] = jnp.zeros_like(dg_ref)
            loss_ref[...] = jnp.zeros_like(loss_ref)

        part = 0.5 * jnp.sum(jnp.mean(err * err, axis=-1, keepdims=True), axis=0, keepdims=True)
        loss_ref[...] += jnp.broadcast_to(part, loss_ref.shape)
        dx, dgrow = _rms_bwd(err * (1.0 / d), xh, rs, g)
        dx_ref[...] = dx
        dg_ref[...] += jnp.sum(dgrow, axis=0, keepdims=True)

    row = pl.BlockSpec((tm, d), lambda i: (i, 0))
    vec = pl.BlockSpec((1, d), lambda i: (0, 0))
    return pl.pallas_call(
        body, name="loss_head", grid=(s_len // tm,),
        out_shape=(SDS((s_len, d), F32), SDS((1, d), F32), SDS((8, LANE), F32)),
        in_specs=[row, vec, row], out_specs=(row, vec, pl.BlockSpec((8, LANE), lambda i: (0, 0))),
        compiler_params=_cparams(),
    )(x, gain, target)


def _adamw_update(w, g, m, v):
    m = ADAM_B1 * m + (1.0 - ADAM_B1) * g
    v = ADAM_B2 * v + (1.0 - ADAM_B2) * (g * g)
    m_hat = m / (1.0 - ADAM_B1 ** ADAM_STEP)
    v_hat = v / (1.0 - ADAM_B2 ** ADAM_STEP)
    return -ADAM_LR * (m_hat / (jnp.sqrt(v_hat) + ADAM_EPS) + ADAM_WD * w), m, v


def adamw(w, g, m, v, tr=512):
    rows, cols = w.shape
    tr = _row_chunk(rows, tr, 8)

    def body(w_ref, g_ref, m_ref, v_ref, d_ref, nm_ref, nv_ref):
        d_ref[...], nm_ref[...], nv_ref[...] = _adamw_update(w_ref[...], g_ref[...], m_ref[...], v_ref[...])

    blk = pl.BlockSpec((tr, cols), lambda i: (i, 0))
    return pl.pallas_call(
        body, name="adamw", grid=(rows // tr,), out_shape=(SDS(w.shape, F32),) * 3,
        in_specs=[blk] * 4, out_specs=(blk,) * 3, compiler_params=_cparams(),
    )(w, g, m, v)


def adamw_layer(gf, blk, row_off, n_half, l, w, m, v, outs, deps=()):
    fh = gf.shape[1] // 7
    d = gf.shape[2]
    nd = len(deps)

    def body(gf_ref, w_ref, m_ref, v_ref, *rest):
        g_ref, d_ref, nm_ref, nv_ref = rest[4 + nd:]
        g = gf_ref[row_off:row_off + n_half, :]
        g_ref[...] = g
        d_ref[...], nm_ref[...], nv_ref[...] = _adamw_update(w_ref[...], g, m_ref[...], v_ref[...])

    gspec = pl.BlockSpec((None, fh, d), lambda h: (h, blk, 0))
    wspec = pl.BlockSpec((None, n_half, d), lambda h: (l, h, 0))
    return pl.pallas_call(
        body, name="adamw_layer", grid=(2,), out_shape=tuple(SDS(o.shape, o.dtype) for o in outs),
        in_specs=[gspec, wspec, wspec, wspec] + [ANY] * (4 + nd), out_specs=(wspec,) * 4,
        input_output_aliases={4 + i: i for i in range(4)}, compiler_params=_cparams(),
    )(gf, w, m, v, *outs, *deps)


def pack_weight(pos, land, blk, l, w, extra=None, deps=()):
    fh, d = land.shape[2] // 7, land.shape[3]
    nd = len(deps)

    def body(pos_ref, w_ref, *rest):
        o_ref = rest[-1]
        a = w_ref[...].astype(BF)
        n = a.shape[0] // 2
        for h in range(2):
            o_ref[h, 0:n, :] = a[h * n:(h + 1) * n]
        if extra is not None:
            b = rest[0][...].astype(BF)
            nb = b.shape[0] // 2
            for h in range(2):
                o_ref[h, n:n + nb, :] = b[h * nb:(h + 1) * nb]

    def whole(a):
        return pl.BlockSpec((None,) + a.shape[1:], lambda i, p: (l, 0, 0))

    ins = [w] + ([extra] if extra is not None else [])
    return pl.pallas_call(
        body, name="pack_weight", out_shape=SDS(land.shape, land.dtype),
        grid_spec=pltpu.PrefetchScalarGridSpec(
            num_scalar_prefetch=1, grid=(1,),
            in_specs=[whole(a) for a in ins] + [ANY] * (1 + nd),
            out_specs=pl.BlockSpec((None, 2, fh, d), lambda i, p: (p[0], 0, blk, 0))),
        input_output_aliases={1 + len(ins): 0}, compiler_params=_cparams(),
    )(pos, *ins, land, *deps)


def _rows_of(shape, width):
    return -(-int(np.prod(shape)) // (SUBLANES * width)) * SUBLANES


def _pack_rows(arrays, width):
    parts = []
    for a in arrays:
        flat = a.reshape(-1).astype(F32)
        r = _rows_of(a.shape, width)
        parts.append(jnp.pad(flat, (0, r * width - flat.shape[0])).reshape(r, width))
    return jnp.concatenate(parts, axis=0)


def _unpack_rows(buf, shapes):
    out, row = [], 0
    for shp in shapes:
        r = _rows_of(shp, buf.shape[1])
        out.append(buf[row:row + r].reshape(-1)[:int(np.prod(shp))].reshape(shp))
        row += r
    return out


def _t5_buckets(rel):
    half = N_BUCKETS // 2
    max_exact = half // 2
    ret = (rel > 0).astype(jnp.int32) * half
    n = jnp.abs(rel)
    n_f = jnp.maximum(n, 1).astype(F32)
    large = max_exact + (jnp.log(n_f / max_exact) / math.log(MAX_DISTANCE / max_exact) * (half - max_exact)).astype(jnp.int32)
    large = jnp.minimum(large, half - 1)
    return ret + jnp.where(n < max_exact, n, large)


def _band_buckets():
    t = jnp.arange(BLOCK)[:, None]
    j = jnp.arange(3 * BLOCK)[None, :]
    rel = j - BLOCK - t
    return _t5_buckets(rel), jnp.abs(rel) <= WINDOW


def _block_diag_pairs(w):
    depth, two, nblk, bw, _ = w.shape
    pairs = w.reshape(depth, two, nblk // 2, 2, bw, bw)
    z = jnp.zeros_like(pairs[:, :, :, 0])
    top = jnp.concatenate([pairs[:, :, :, 0], z], axis=-1)
    bot = jnp.concatenate([z, pairs[:, :, :, 1]], axis=-1)
    return jnp.concatenate([top, bot], axis=-2)


def _diag_blocks(dw):
    bw = dw.shape[-1] // 2
    a = dw[:, :, :bw, :bw]
    b = dw[:, :, bw:, bw:]
    return jnp.stack([a, b], axis=2).reshape(dw.shape[0], 2 * dw.shape[1], bw, bw)


def kernel(x, ffn1_norm, ffn1_w_gate, ffn1_w_up, ffn1_w_down, mix_norm, w_in, conv_w, conv_b, lru_w_a, lru_b_a, lru_w_x, lru_b_x, lru_lambda, attn_sink, rel_bias, lru_out_norm, attn_out_norm, w_out, ffn2_norm, ffn2_w_gate, ffn2_w_up, ffn2_w_down, final_norm, loss_target, m_ffn1_norm, m_ffn1_w_gate, m_ffn1_w_up, m_ffn1_w_down, m_mix_norm, m_w_in, m_conv_w, m_conv_b, m_lru_w_a, m_lru_b_a, m_lru_w_x, m_lru_b_x, m_lru_lambda, m_attn_sink, m_rel_bias, m_lru_out_norm, m_attn_out_norm, m_w_out, m_ffn2_norm, m_ffn2_w_gate, m_ffn2_w_up, m_ffn2_w_down, m_final_norm, v_ffn1_norm, v_ffn1_w_gate, v_ffn1_w_up, v_ffn1_w_down, v_mix_norm, v_w_in, v_conv_w, v_conv_b, v_lru_w_a, v_lru_b_a, v_lru_w_x, v_lru_b_x, v_lru_lambda, v_attn_sink, v_rel_bias, v_lru_out_norm, v_attn_out_norm, v_w_out, v_ffn2_norm, v_ffn2_w_gate, v_ffn2_w_up, v_ffn2_w_down, v_final_norm):
    depth, d = ffn1_norm.shape
    d_ff = N_CHIPS * ffn1_w_gate.shape[2]
    d_in = N_CHIPS * w_in.shape[2]
    lw = conv_b.shape[1]
    att = N_HEADS * HEAD_DIM
    lay = Layout(d, d_ff, d_in)
    k_chip = 2 * lax.axis_index("x") + lax.axis_index("y")
    pos = jnp.stack([k_chip, lax.axis_index("c")]).astype(jnp.int32)

    def rows_major(a):
        return jnp.swapaxes(a, 1, 2)

    mats = (rows_major(ffn1_w_gate), rows_major(ffn1_w_up), ffn1_w_down,
            rows_major(ffn2_w_gate), rows_major(ffn2_w_up), ffn2_w_down)

    def pack_layer(l, deps=()):
        land = lax.empty((N_CHIPS, 2, lay.rows, d), BF)
        for m, a in enumerate(mats):
            land = pack_weight(pos, land, m, l, a, deps=deps if m == 0 else ())
        return pack_weight(pos, land, lay.MIX_BLK, l, rows_major(w_in), extra=w_out)

    def gather_start(l, land):
        return split_start(f"gather_start_{l}", [land], 3, gather_plan)

    def gather_wait(l, started, after):
        ssem, rsem, bufs, _ = started
        return split_wait(f"gather_wait_{l}", ssem, rsem, bufs, after, gather_plan)

    sharded_small = (conv_w, lru_b_a, lru_b_x, lru_lambda)
    sshard = jnp.concatenate([a.reshape(-1, LANE) for a in sharded_small], axis=0)
    sfull = gather_small(sshard)
    small_full, off = [], 0
    for a in sharded_small:
        r = a.shape[0] * a.shape[1]
        piece = sfull[:, off:off + r].reshape((N_CHIPS,) + a.shape)
        small_full.append(jnp.moveaxis(piece, 0, 2).reshape(a.shape[0], a.shape[1], N_CHIPS * LANE))
        off += r
    conv_w_f, b_a_f, b_x_f, lam_f = small_full

    zrow = jnp.zeros((1, lw), F32)
    wblk_a = _block_diag_pairs(lru_w_a)
    wblk_x = _block_diag_pairs(lru_w_x)
    buckets, in_band = _band_buckets()
    onehot = (buckets.reshape(-1)[:, None] == jnp.arange(N_BUCKETS)[None, :]).astype(F32)
    bias = jnp.dot(rel_bias.T, onehot.T, precision=lax.Precision.HIGHEST).reshape(N_HEADS, BLOCK, 3 * BLOCK)
    bias = jnp.where(in_band[None], bias, NEG_INF)
    bias = bias.reshape(N_KV_HEADS, KV_GROUP, BLOCK, 3 * BLOCK).transpose(0, 3, 1, 2).reshape(N_KV_HEADS, 3 * BLOCK, KV_GROUP * BLOCK)
    kblk = 2 * lw // LANE

    def layer_small(l):
        cvec = jnp.concatenate([conv_w_f[l], jnp.zeros((8 - CONV_WIDTH, lw), F32)], axis=0)
        pvec = jnp.concatenate([conv_b[l][None], b_a_f[l], b_x_f[l], lam_f[l], zrow], axis=0)
        wblk = jnp.stack([wblk_a[l, 0], wblk_x[l, 0], wblk_a[l, 1], wblk_x[l, 1]]).astype(BF)
        sink = jnp.broadcast_to(jnp.repeat(attn_sink[l], BLOCK).reshape(N_KV_HEADS, 1, KV_GROUP * BLOCK),
                                (N_KV_HEADS, 8, KV_GROUP * BLOCK))
        return cvec, pvec, wblk, sink

    xs = x[0]
    wfull = [None] * depth
    parts = [(0, 3 * lay.fh), (3 * lay.fh, lay.rows - 3 * lay.fh)]
    plans = [(functools.partial(gather_plan, rows=p), functools.partial(handover_plan, rows=p)) for p in parts]
    land = pack_layer(0, deps=(sfull,))
    first = split_start("gather_start_0a", [land], 3, plans[0][0])
    second = split_start("gather_start_0b", first[2], 3, plans[1][0])
    lands = {l: pack_layer(l, deps=(second[3],)) for l in range(1, depth)}
    land = split_wait("gather_wait_0a", first[0], first[1], second[2], [xs] + list(lands.values()), plans[0][0])
    wfull[0], = exchange_now("gather_handover_0a", land, 3, plans[0][1])
    started = None
    saved = []
    for l in range(depth):
        cvec, pvec, wblk, sink = layer_small(l)
        deps = (started[3],) if started is not None else ()
        x1, gate1, up1 = ffn_forward(xs, ffn1_norm[l][None], wfull[l], lay, 0, deps=deps)
        deps = ()
        if l == 0:
            land = split_wait("gather_wait_0b", second[0], second[1], [wfull[0]], [x1], plans[1][0])
            wfull[0], = exchange_now("gather_handover_0b", land, 3, plans[1][1])
            if depth > 1:
                started = gather_start(1, lands[1])
                deps = (started[3],)
        proj, qt, vt = mix_project(x1, mix_norm[l][None], wfull[l], lay, lw, att)
        y_rec, hs = lru_forward(proj, cvec, pvec, wblk, lw, deps=deps)
        y_att = attention_forward(qt, proj, vt, bias, sink, kblk)
        x2 = mix_output(x1, y_rec, y_att, lru_out_norm[l][None], attn_out_norm[l][:, None], wfull[l], lay)
        deps, handover = (), None
        if 0 < l < depth - 1:
            land, = gather_wait(l + 1, started, [x2])
            started = gather_start(l + 2, lands[l + 2]) if l + 2 < depth else None
            handover = split_start(f"gather_handover_start_{l + 1}", [land], 3, handover_plan)
            deps = (handover[3],) + ((started[3],) if started is not None else ())
        x3, gate2, up2 = ffn_forward(x2, ffn2_norm[l][None], wfull[l], lay, 1, deps=deps)
        saved.append((xs, x1, x2, proj, qt, y_rec, hs, y_att, (gate1, up1), (gate2, up2)))
        xs = x3
        if handover is not None:
            wfull[l + 1], = split_wait(f"gather_handover_wait_{l + 1}", handover[0], handover[1], handover[2], [x3],
                                       handover_plan)
        elif l == 0 and depth > 1:
            wfull[1], = exchange_now("gather_handover_1", gather_wait(1, started, [x3]), 3, handover_plan)
            started = gather_start(2, lands[2]) if depth > 2 else None

    dx, d_final, loss_tile = loss_head(xs, final_norm[None], loss_target[0])
    loss = lax.psum(loss_tile[0, 0], ("x", "y", "c"))

    layer_names = ["ffn1_norm", "mix_norm", "conv_w", "conv_b", "lru_w_a", "lru_b_a", "lru_w_x", "lru_b_x", "lru_lambda",
                   "attn_sink", "lru_out_norm", "attn_out_norm", "ffn2_norm"]
    dbias_total = jnp.zeros(bias.shape, F32)

    def ffn_back(xin, gain, dout, pre, gb, l, which, deps=()):
        dxo, dg, lhs, rhs = ffn_backward_dx(xin, gain, dout, *pre, wfull[l], lay, which, deps=deps)
        return dxo, dg[0], weight_grad_tn(lhs, rhs, gb, lay, 3 * which)

    def pair_start(l, gb, sb):
        lands = [lax.empty((N_CHIPS,) + gb.shape[2:], gb.dtype), lax.empty(sb.shape, sb.dtype)]
        return split_start(f"pair_start_{l}", [gb, sb] + lands, N_CHIPS + 1, pair_plan)

    def reduce_start(l, paired, after):
        gb, sb, p1, sp1 = split_wait(f"pair_wait_{l}", paired[0], paired[1], paired[2], after, pair_plan)
        cs = pair_sum(pos, gb, p1)
        ss = small_pair_sum(sb, sp1)
        lands = [lax.empty((3,) + cs.shape[1:], cs.dtype), lax.empty((N_CHIPS,) + ss.shape, ss.dtype)]
        return split_start(f"reduce_start_{l}", [cs, ss] + lands, 6, reduce_plan)

    def reduce_finish(l, started, after):
        ssem, rsem, bufs, _ = started
        cs, ss, p3, sp3 = split_wait(f"reduce_wait_{l}", ssem, rsem, bufs, after, reduce_plan)
        return chip_sum(pos, cs, p3), small_chip_sum(pos, ss, sp3)

    gf = [None] * depth
    small_sums = [None] * depth
    small_shapes = [None] * depth
    paired = None
    in_flight = None
    finals = {}
    tokens = []
    for l in reversed(range(depth)):
        x0, x1, x2, proj, qt, y_rec, hs, y_att, pre1, pre2 = saved[l]
        cvec, pvec, wblk, sink = layer_small(l)
        gb = lax.empty((N_CHIPS, 2, lay.rows, d), BF)
        part = {}
        dx, part["ffn2_norm"], gb = ffn_back(x2, ffn2_norm[l][None], dx, pre2, gb, l, 1, deps=tuple(tokens))
        deps = ()
        if paired is not None:
            in_flight = (paired[0], reduce_start(paired[0], paired[1], [dx, gb]))
            deps = (in_flight[1][3],)
        dyr, dya, dgr, dga, dwout = mix_output_backward(dx, y_rec, y_att, lru_out_norm[l][None], attn_out_norm[l][:, None],
                                                        wfull[l], lay, deps=deps)
        part["lru_out_norm"] = dgr[0]
        part["attn_out_norm"] = dga[:, 0]
        dq, dkv, dbias, dsink = attention_backward(qt, proj, y_att, dya, bias, sink, kblk)
        dbias_total = dbias_total + dbias
        part["attn_sink"] = jnp.sum(dsink[:, 0, :].reshape(N_HEADS, BLOCK), axis=1)
        dxr, dgt, dcv, dpv, dwb = lru_backward(proj, hs, dyr, cvec, pvec, wblk, lw)
        part["conv_w"] = dcv[:CONV_WIDTH]
        part["conv_b"] = dpv[0]
        part["lru_b_a"] = dpv[1:3]
        part["lru_b_x"] = dpv[3:5]
        part["lru_lambda"] = dpv[5:7]
        part["lru_w_a"] = _diag_blocks(jnp.stack([dwb[0], dwb[2]]))
        part["lru_w_x"] = _diag_blocks(jnp.stack([dwb[1], dwb[3]]))
        dx, dgm, gb = mix_project_backward(x1, mix_norm[l][None], dx, dxr, dgt, dq, dkv, dwout, wfull[l], gb, lay)
        part["mix_norm"] = dgm[0]
        dx, part["ffn1_norm"], gb = ffn_back(x0, ffn1_norm[l][None], dx, pre1, gb, l, 0)
        pieces = [part[n] for n in layer_names]
        if l == 0:
            dbias_heads = dbias_total.reshape(N_KV_HEADS, 3 * BLOCK, KV_GROUP, BLOCK).transpose(0, 2, 3, 1)
            d_rel_bias = jnp.dot(dbias_heads.reshape(N_HEADS, -1), onehot, precision=lax.Precision.HIGHEST).T
            pieces += [d_rel_bias, d_final[0]]
        small_shapes[l] = [p.shape for p in pieces]
        paired = (l, pair_start(l, gb, _pack_rows(pieces, 1024)))
        tokens = [paired[1][3]]
        if in_flight is not None:
            above = in_flight[0]
            half, small_sums[above] = reduce_finish(above, in_flight[1], [dx])
            finals[above] = split_start(f"final_start_{above}", [half], 1, final_plan)
            tokens.append(finals[above][3])
            in_flight = None
    grad_x = dx[None]

    weights = dict(ffn1_norm=ffn1_norm, ffn1_w_gate=ffn1_w_gate, ffn1_w_up=ffn1_w_up, ffn1_w_down=ffn1_w_down, mix_norm=mix_norm, w_in=w_in, conv_w=conv_w, conv_b=conv_b, lru_w_a=lru_w_a, lru_b_a=lru_b_a, lru_w_x=lru_w_x, lru_b_x=lru_b_x, lru_lambda=lru_lambda, attn_sink=attn_sink, rel_bias=rel_bias, lru_out_norm=lru_out_norm, attn_out_norm=attn_out_norm, w_out=w_out, ffn2_norm=ffn2_norm, ffn2_w_gate=ffn2_w_gate, ffn2_w_up=ffn2_w_up, ffn2_w_down=ffn2_w_down, final_norm=final_norm)
    m_in = dict(ffn1_norm=m_ffn1_norm, ffn1_w_gate=m_ffn1_w_gate, ffn1_w_up=m_ffn1_w_up, ffn1_w_down=m_ffn1_w_down, mix_norm=m_mix_norm, w_in=m_w_in, conv_w=m_conv_w, conv_b=m_conv_b, lru_w_a=m_lru_w_a, lru_b_a=m_lru_b_a, lru_w_x=m_lru_w_x, lru_b_x=m_lru_b_x, lru_lambda=m_lru_lambda, attn_sink=m_attn_sink, rel_bias=m_rel_bias, lru_out_norm=m_lru_out_norm, attn_out_norm=m_attn_out_norm, w_out=m_w_out, ffn2_norm=m_ffn2_norm, ffn2_w_gate=m_ffn2_w_gate, ffn2_w_up=m_ffn2_w_up, ffn2_w_down=m_ffn2_w_down, final_norm=m_final_norm)
    v_in = dict(ffn1_norm=v_ffn1_norm, ffn1_w_gate=v_ffn1_w_gate, ffn1_w_up=v_ffn1_w_up, ffn1_w_down=v_ffn1_w_down, mix_norm=v_mix_norm, w_in=v_w_in, conv_w=v_conv_w, conv_b=v_conv_b, lru_w_a=v_lru_w_a, lru_b_a=v_lru_b_a, lru_w_x=v_lru_w_x, lru_b_x=v_lru_b_x, lru_lambda=v_lru_lambda, attn_sink=v_attn_sink, rel_bias=v_rel_bias, lru_out_norm=v_lru_out_norm, attn_out_norm=v_attn_out_norm, w_out=v_w_out, ffn2_norm=v_ffn2_norm, ffn2_w_gate=v_ffn2_w_gate, ffn2_w_up=v_ffn2_w_up, ffn2_w_down=v_ffn2_w_down, final_norm=v_final_norm)
    order = list(weights)
    large = [(name, m, 0, lay.fh, m % 3 != 2) for m, name in
             enumerate(("ffn1_w_gate", "ffn1_w_up", "ffn1_w_down", "ffn2_w_gate", "ffn2_w_up", "ffn2_w_down"))]
    large += [("w_in", lay.MIX_BLK, 0, lay.ih, True), ("w_out", lay.MIX_BLK, lay.ih, lay.oh, False)]
    as_rows = {name: [rows_major(src[name]) if flip else src[name] for src in (weights, m_in, v_in)]
               for name, _, _, _, flip in large}
    stacked = {name: tuple(lax.empty(as_rows[name][0].shape, F32) for _ in range(4)) for name, *_ in large}

    def adamw_large(l, deps=()):
        for i, (name, blk, row_off, n_half, _) in enumerate(large):
            stacked[name] = adamw_layer(gf[l], blk, row_off, n_half, l, *as_rows[name], stacked[name],
                                        deps=deps if i == 0 else ())

    last = paired[0]
    crossing = reduce_start(last, paired[1], [dx])
    for l in sorted(finals):
        gf[l], = split_wait(f"final_wait_{l}", finals[l][0], finals[l][1], finals[l][2], [crossing[3]], final_plan)
        adamw_large(l, deps=(crossing[3],))
    ready = [buf for name, *_ in large for buf in stacked[name]] if depth > 1 else []
    half, small_sums[last] = reduce_finish(last, crossing, [dx] + ready)
    gf[last], = exchange_now(f"final_now_{last}", [half], 1, final_plan)
    adamw_large(last)

    per_layer = [_unpack_rows(small_sums[l], small_shapes[l]) for l in range(depth)]
    grads = {n: jnp.stack([per_layer[l][i] for l in range(depth)]) for i, n in enumerate(layer_names)}
    grads["rel_bias"], grads["final_norm"] = per_layer[0][len(layer_names):]
    for name in ("conv_w", "lru_b_a", "lru_b_x", "lru_lambda"):
        grads[name] = lax.dynamic_slice_in_dim(grads[name], k_chip * LANE, LANE, axis=2)
    delta, new_m, new_v = {}, {}, {}
    for name, _, _, _, flip in large:
        grads[name], delta[name], new_m[name], new_v[name] = [rows_major(a) if flip else a for a in stacked[name]]
    small = [n for n in order if n not in stacked]
    packed = [_pack_rows([src[n] for n in small], 1024) for src in (weights, grads, m_in, v_in)]
    outs = adamw(*packed)
    shapes = [weights[n].shape for n in small]
    for dst, buf in zip((delta, new_m, new_v), outs):
        dst.update(zip(small, _unpack_rows(buf, shapes)))

    return (loss, grad_x, *[grads[n] for n in order], *[delta[n] for n in order],
            *[new_m[n] for n in order], *[new_v[n] for n in order])
```

```python
import functools
import math

import jax
import jax.numpy as jnp
import numpy as np
from jax import lax
from jax.experimental import pallas as pl
from jax.experimental.pallas import tpu as pltpu

BF = jnp.bfloat16
F32 = jnp.float32
SDS = jax.ShapeDtypeStruct
MESH = pl.DeviceIdType.MESH
ANY = pl.BlockSpec(memory_space=pl.ANY)

N_CHIPS = 4
N_HEADS = 8
N_KV_HEADS = 2
KV_GROUP = N_HEADS // N_KV_HEADS
HEAD_DIM = 64
BLOCK = 128
WINDOW = 128
N_BUCKETS = 32
MAX_DISTANCE = 128
LRU_C = 8.0
CONV_WIDTH = 4
LANE = 128
SUBLANES = 8
SCAN_CHAINS = 8
EPS = 1e-6
FFN_RES = 0.5
NEG_INF = -1e30
ADAM_LR = 0.001
ADAM_B1 = 0.9
ADAM_B2 = 0.999
ADAM_EPS = 1e-08
ADAM_WD = 0.01
ADAM_STEP = 10
VMEM_LIMIT = 60000 * 1024
GELU_C = math.sqrt(2.0 / math.pi)


def dot_nn(a, b):
    return lax.dot_general(a, b, (((1,), (0,)), ((), ())), preferred_element_type=F32)


def dot_nt(a, b):
    return lax.dot_general(a, b, (((1,), (1,)), ((), ())), preferred_element_type=F32)


def dot_tn(a, b):
    return lax.dot_general(a, b, (((0,), (0,)), ((), ())), preferred_element_type=F32)


def _cparams(**kw):
    return pltpu.CompilerParams(vmem_limit_bytes=VMEM_LIMIT, **kw)


class Layout:
    MIX_BLK = 6
    BLOCKS = 7

    def __init__(self, d_model, d_ff, d_in):
        self.fh = d_ff // (2 * N_CHIPS)
        self.ih = d_in // (2 * N_CHIPS)
        self.oh = d_model // (2 * N_CHIPS)
        assert self.ih + self.oh == self.fh, "w_in^T and w_out rows must fill one ffn-sized block"
        self.rows = self.BLOCKS * self.fh


def _row_chunk(rows, target, step=16):
    best = rows
    for c in range(step, min(rows, target) + 1, step):
        if rows % c == 0:
            best = c
    return best


def _mesh_pos():
    return lax.axis_index("x"), lax.axis_index("y"), lax.axis_index("c")


def _rcopy(src, dst, ssem, rsem, dev):
    return pltpu.make_async_remote_copy(src_ref=src, dst_ref=dst, send_sem=ssem, recv_sem=rsem,
                                        device_id=dev, device_id_type=MESH)


HBM = pl.BlockSpec(memory_space=pltpu.HBM)
SEM = pl.BlockSpec(memory_space=pltpu.SEMAPHORE)
DATAFLOW = pltpu.SideEffectType.DATAFLOW_SIDE_EFFECTING


def _chip_peers():
    x, y, c = _mesh_pos()
    peers = [(1 - x, y), (x, 1 - y), (1 - x, 1 - y)]
    return x, y, c, 2 * x + y, [(px, py, 2 * px + py) for px, py in peers]


def split_start(name, bufs, n, plan):
    nb = len(bufs)

    def body(*refs):
        sends, _ = plan(refs[:nb], refs[nb], refs[nb + 1])
        for cp in sends:
            cp.start()
        refs[-1][...] = jnp.zeros_like(refs[-1])

    out = pl.pallas_call(
        body, name=name,
        out_shape=(pltpu.SemaphoreType.DMA((n,)), pltpu.SemaphoreType.DMA((n,)),
                   *[pltpu.HBM(b.shape, b.dtype) for b in bufs], SDS((8, LANE), F32)),
        in_specs=[HBM] * nb, out_specs=(SEM, SEM, *([HBM] * nb), pl.BlockSpec(memory_space=pltpu.VMEM)),
        input_output_aliases={i: 2 + i for i in range(nb)},
        compiler_params=pltpu.CompilerParams(has_side_effects=DATAFLOW),
    )(*[pltpu.with_memory_space_constraint(b, pltpu.HBM) for b in bufs])
    return out[0], out[1], list(out[2:2 + nb]), out[-1]


def split_wait(name, ssem, rsem, bufs, after, plan):
    nb = len(bufs)

    def body(*refs):
        sends, recvs = plan(refs[:nb], refs[nb], refs[nb + 1])
        for cp in recvs:
            cp.wait_recv()
        for cp in sends:
            cp.wait_send()

    out = pl.pallas_call(
        body, name=name, out_shape=tuple(pltpu.HBM(b.shape, b.dtype) for b in bufs),
        in_specs=[HBM] * nb + [SEM, SEM] + [ANY] * len(after), out_specs=tuple([HBM] * nb),
        input_output_aliases={i: i for i in range(nb)},
        compiler_params=pltpu.CompilerParams(has_side_effects=DATAFLOW),
    )(*bufs, ssem, rsem, *after)
    return list(out)


def gather_plan(refs, ssem, rsem, rows=None):
    land_ref, = refs
    _, _, c, k, peers = _chip_peers()
    part = (lambda a: a) if rows is None else (lambda a: a.at[pl.ds(rows[0], rows[1])])
    sends = [_rcopy(part(land_ref.at[k, c]), part(land_ref.at[k, c]), ssem.at[j], rsem.at[j], (px, py, c))
             for j, (px, py, _) in enumerate(peers)]
    recvs = [_rcopy(part(land_ref.at[kp, c]), part(land_ref.at[kp, c]), ssem.at[j], rsem.at[j], (px, py, c))
             for j, (px, py, kp) in enumerate(peers)]
    return sends, recvs


def reduce_plan(refs, ssem, rsem):
    cs_ref, ss_ref, p3_ref, sp3_ref = refs
    _, _, c, k, peers = _chip_peers()
    sends, recvs = [], []
    for j, (px, py, kp) in enumerate(peers):
        sends.append(_rcopy(cs_ref.at[kp], p3_ref.at[j], ssem.at[j], rsem.at[j], (px, py, c)))
        recvs.append(_rcopy(cs_ref.at[kp], p3_ref.at[j], ssem.at[j], rsem.at[j], (px, py, c)))
        sends.append(_rcopy(ss_ref, sp3_ref.at[k], ssem.at[3 + j], rsem.at[3 + j], (px, py, c)))
        recvs.append(_rcopy(ss_ref, sp3_ref.at[kp], ssem.at[3 + j], rsem.at[3 + j], (px, py, c)))
    return sends, recvs


def gather_small(sshard):
    def body(s_ref, sf_ref, lsem, ssem, rsem):
        _, _, c, k, peers = _chip_peers()
        own = pltpu.make_async_copy(s_ref, sf_ref.at[k], lsem)
        own.start()
        sends = [_rcopy(s_ref, sf_ref.at[k], ssem.at[j], rsem.at[j], (px, py, c)) for j, (px, py, _) in enumerate(peers)]
        recvs = [_rcopy(s_ref, sf_ref.at[kp], ssem.at[j], rsem.at[j], (px, py, c)) for j, (px, py, kp) in enumerate(peers)]
        for cp in sends:
            cp.start()
        for cp in recvs:
            cp.wait_recv()
        for cp in sends:
            cp.wait_send()
        own.wait()

    return pl.pallas_call(
        body, name="gather_small", out_shape=SDS((N_CHIPS,) + sshard.shape, sshard.dtype),
        in_specs=[ANY], out_specs=ANY,
        scratch_shapes=[pltpu.SemaphoreType.DMA, pltpu.SemaphoreType.DMA((3,)), pltpu.SemaphoreType.DMA((3,))],
    )(sshard)


def exchange_now(name, bufs, n, plan):
    nb = len(bufs)

    def body(*refs):
        sends, recvs = plan(refs[nb:2 * nb], refs[2 * nb], refs[2 * nb + 1])
        for cp in sends:
            cp.start()
        for cp in recvs:
            cp.wait_recv()
        for cp in sends:
            cp.wait_send()

    return list(pl.pallas_call(
        body, name=name, out_shape=tuple(SDS(b.shape, b.dtype) for b in bufs),
        in_specs=[ANY] * nb, out_specs=tuple([ANY] * nb), input_output_aliases={i: i for i in range(nb)},
        scratch_shapes=[pltpu.SemaphoreType.DMA((n,)), pltpu.SemaphoreType.DMA((n,))],
    )(*bufs))


def handover_plan(refs, ssem, rsem, rows=None):
    land_ref, = refs
    x, y, c, _, peers = _chip_peers()
    sib = (x, y, 1 - c)
    part = (lambda a: a) if rows is None else (lambda a: a.at[pl.ds(rows[0], rows[1])])
    sends = [_rcopy(part(land_ref.at[kp, c]), part(land_ref.at[kp, c]), ssem.at[j], rsem.at[j], sib)
             for j, (_, _, kp) in enumerate(peers)]
    recvs = [_rcopy(part(land_ref.at[kp, 1 - c]), part(land_ref.at[kp, 1 - c]), ssem.at[j], rsem.at[j], sib)
             for j, (_, _, kp) in enumerate(peers)]
    return sends, recvs


def pair_plan(refs, ssem, rsem):
    gb_ref, sb_ref, p_ref, sp_ref = refs
    x, y, c = _mesh_pos()
    sib = (x, y, 1 - c)
    n = gb_ref.shape[0]
    copies = [_rcopy(gb_ref.at[kk, 1 - c], p_ref.at[kk], ssem.at[kk], rsem.at[kk], sib) for kk in range(n)]
    copies.append(_rcopy(sb_ref, sp_ref, ssem.at[n], rsem.at[n], sib))
    return copies, copies


def final_plan(refs, ssem, rsem):
    gf_ref, = refs
    x, y, c = _mesh_pos()
    sib = (x, y, 1 - c)
    return ([_rcopy(gf_ref.at[c], gf_ref.at[c], ssem.at[0], rsem.at[0], sib)],
            [_rcopy(gf_ref.at[1 - c], gf_ref.at[1 - c], ssem.at[0], rsem.at[0], sib)])


def pair_sum(pos, gb, p1):
    n, _, rh, d = gb.shape
    cr = _row_chunk(rh, 1280)

    def body(pos_ref, a_ref, b_ref, o_ref):
        o_ref[...] = (a_ref[...].astype(F32) + b_ref[...].astype(F32)).astype(o_ref.dtype)

    return pl.pallas_call(
        body, name="pair_sum", out_shape=SDS((n, rh, d), gb.dtype),
        grid_spec=pltpu.PrefetchScalarGridSpec(
            num_scalar_prefetch=1, grid=(n, rh // cr),
            in_specs=[pl.BlockSpec((None, None, cr, d), lambda kk, r, pos: (kk, pos[1], r, 0)),
                      pl.BlockSpec((None, cr, d), lambda kk, r, pos: (kk, r, 0))],
            out_specs=pl.BlockSpec((None, cr, d), lambda kk, r, pos: (kk, r, 0))),
        compiler_params=_cparams(),
    )(pos, gb, p1)


def chip_sum(pos, cs, p3):
    n, rh, d = cs.shape
    cr = _row_chunk(rh, 640)

    def body(pos_ref, a_ref, b_ref, o_ref):
        acc = a_ref[...].astype(F32)
        for j in range(3):
            acc = acc + b_ref[j].astype(F32)
        o_ref[...] = acc

    return pl.pallas_call(
        body, name="chip_sum", out_shape=SDS((2, rh, d), F32),
        grid_spec=pltpu.PrefetchScalarGridSpec(
            num_scalar_prefetch=1, grid=(rh // cr,),
            in_specs=[pl.BlockSpec((None, cr, d), lambda r, pos: (pos[0], r, 0)),
                      pl.BlockSpec((3, cr, d), lambda r, pos: (0, r, 0))],
            out_specs=pl.BlockSpec((None, cr, d), lambda r, pos: (pos[1], r, 0))),
        compiler_params=_cparams(),
    )(pos, cs, p3)


def small_pair_sum(a, b):
    def body(a_ref, b_ref, o_ref):
        o_ref[...] = a_ref[...] + b_ref[...]

    return pl.pallas_call(body, name="small_pair_sum", out_shape=SDS(a.shape, a.dtype),
                          compiler_params=_cparams())(a, b)


def small_chip_sum(pos, own, p):
    ns, w = own.shape

    def body(pos_ref, own_ref, p0, p1, p2, p3, o_ref):
        k = pos_ref[0]
        acc = None
        for chip, ref in enumerate((p0, p1, p2, p3)):
            term = jnp.where(k == chip, own_ref[...], ref[...])
            acc = term if acc is None else acc + term
        o_ref[...] = acc

    def slot(chip):
        return pl.BlockSpec((None, ns, w), lambda i, pos: (jnp.where(pos[0] == chip, (chip + 1) % N_CHIPS, chip), 0, 0))

    return pl.pallas_call(
        body, name="small_chip_sum", out_shape=SDS(own.shape, own.dtype),
        grid_spec=pltpu.PrefetchScalarGridSpec(
            num_scalar_prefetch=1, grid=(1,),
            in_specs=[pl.BlockSpec((ns, w), lambda i, pos: (0, 0))] + [slot(chip) for chip in range(N_CHIPS)],
            out_specs=pl.BlockSpec((ns, w), lambda i, pos: (0, 0))),
        compiler_params=_cparams(),
    )(pos, own, p, p, p, p)


def _rms(x, g):
    rs = lax.rsqrt(jnp.mean(x * x, axis=-1, keepdims=True) + EPS)
    xh = x * rs
    return xh, rs, xh * g


def _rms_bwd(dy, xh, rs, g):
    dxh = dy * g
    dx = rs * (dxh - xh * jnp.mean(dxh * xh, axis=-1, keepdims=True))
    return dx, dy * xh


def _gelu(x):
    t = jnp.tanh(GELU_C * (x + 0.044715 * x * x * x))
    return 0.5 * x * (1.0 + t), t


def _gelu_grad(x, t):
    return 0.5 * (1.0 + t) + 0.5 * x * (1.0 - t * t) * GELU_C * (1.0 + 3.0 * 0.044715 * x * x)


def _shift_rows(v, s, n):
    if s == 0:
        return v
    t = lax.broadcasted_iota(jnp.int32, v.shape, 0)
    rolled = pltpu.roll(v, (-s) % n, 0)
    inside = (t < n - s) if s > 0 else (t >= -s)
    return jnp.where(inside, rolled, 0.0)


def _scan_rows(a_ref, u_ref, h_ref, acum_ref, reverse):
    s_len, w = a_ref.shape
    chunk = min(512, s_len)
    last = 0 if reverse else SUBLANES - 1

    def inside_vregs(ci, _):
        rows = pl.ds(pl.multiple_of(ci * chunk, chunk), chunk)
        a = a_ref[rows, :].reshape(chunk // SUBLANES, SUBLANES, w)
        u = u_ref[rows, :].reshape(chunk // SUBLANES, SUBLANES, w)
        pos = lax.broadcasted_iota(jnp.int32, (1, SUBLANES, w), 1)
        for dist in (1, 2, 4):
            ok = (pos < SUBLANES - dist) if reverse else (pos >= dist)
            shift = SUBLANES - dist if reverse else dist
            u = u + a * jnp.where(ok, pltpu.roll(u, shift, 1), 0.0)
            a = a * jnp.where(ok, pltpu.roll(a, shift, 1), 1.0)
        h_ref[rows, :] = u.reshape(chunk, w)
        acum_ref[rows, :] = a.reshape(chunk, w)
        return 0

    lax.fori_loop(0, s_len // chunk, inside_vregs, 0)

    chains = max(1, min(SCAN_CHAINS, s_len // (8 * SUBLANES)))
    seg = s_len // chains
    nvreg = seg // SUBLANES

    def step(j, carry):
        jj = (nvreg - 1 - j) if reverse else j
        out = []
        for c, (hin, ain) in enumerate(carry):
            rows = pl.ds(pl.multiple_of(c * seg + jj * SUBLANES, SUBLANES), SUBLANES)
            acc = acum_ref[rows, :]
            h = h_ref[rows, :] + acc * hin
            acc = acc * ain
            h_ref[rows, :] = h
            acum_ref[rows, :] = acc
            out.append((jnp.broadcast_to(h[last:last + 1, :], h.shape), jnp.broadcast_to(acc[last:last + 1, :], acc.shape)))
        return tuple(out)

    init = tuple((jnp.zeros((SUBLANES, w), F32), jnp.ones((SUBLANES, w), F32)) for _ in range(chains))
    ends = lax.fori_loop(0, nvreg, step, init, unroll=min(2, nvreg))
    order = range(chains - 2, -1, -1) if reverse else range(1, chains)
    inflow = jnp.zeros((1, w), F32)
    for s in order:
        h, acc = ends[s + 1 if reverse else s - 1]
        inflow = h[0:1, :] + acc[0:1, :] * inflow
        rows = pl.ds(s * seg, seg)
        h_ref[rows, :] = h_ref[rows, :] + acum_ref[rows, :] * inflow


def _w_spec(rows_half, d, blk):
    return pl.BlockSpec((N_CHIPS, 2, rows_half, d), lambda *_: (0, 0, blk, 0), pipeline_mode=pl.Buffered(1))


def ffn_forward(x, gain, wfull, lay, which, deps=(), tm=512):
    s_len, d = x.shape
    tm = min(tm, s_len)
    f = 8 * lay.fh
    fc = f // 2

    def body(x_ref, g_ref, wg_ref, wu_ref, wd_ref, *rest):
        o_ref, gate_ref, up_ref = rest[len(deps):]
        x = x_ref[...]
        _, _, hn = _rms(x, g_ref[...])
        h = hn.astype(BF)
        y = jnp.zeros((tm, d), F32)
        for part in range(2):
            cols = slice(part * fc, (part + 1) * fc)
            gate = dot_nt(h, wg_ref[...].reshape(f, d)[cols])
            up = dot_nt(h, wu_ref[...].reshape(f, d)[cols])
            act = (gate * jax.nn.sigmoid(gate) * up).astype(BF)
            y = y + dot_nn(act, wd_ref[...].reshape(f, d)[cols])
            gate_ref[:, cols] = gate.astype(BF)
            up_ref[:, cols] = up.astype(BF)
        o_ref[...] = x + FFN_RES * y

    row = pl.BlockSpec((tm, d), lambda i: (i, 0))
    wide = pl.BlockSpec((tm, f), lambda i: (i, 0))
    return pl.pallas_call(
        body, name="ffn_forward", grid=(s_len // tm,),
        out_shape=(SDS((s_len, d), F32), SDS((s_len, f), BF), SDS((s_len, f), BF)),
        in_specs=[row, pl.BlockSpec((1, d), lambda i: (0, 0))]
        + [_w_spec(lay.fh, d, 3 * which + m) for m in range(3)] + [ANY] * len(deps),
        out_specs=(row, wide, wide), compiler_params=_cparams(),
    )(x, gain, wfull, wfull, wfull, *deps)


def ffn_backward_dx(x, gain, dout, gate_bf, up_bf, wfull, lay, which, deps=(), tm=256):
    s_len, d = x.shape
    tm = min(tm, s_len)
    f = 8 * lay.fh
    fc = f // 2
    nt = s_len // tm

    def body(x_ref, g_ref, do_ref, gate_ref, up_ref, wg_ref, wu_ref, wd_ref, *rest):
        dx_ref, dg_ref, lhs_ref, rhs_ref = rest[len(deps):]
        dgate_ref, dup_ref, act_ref = lhs_ref.at[0], lhs_ref.at[1], lhs_ref.at[2]
        h_ref, df_ref = rhs_ref.at[0], rhs_ref.at[1]
        x = x_ref[...]
        g = g_ref[...]
        xh, rs, hn = _rms(x, g)
        h = hn.astype(BF)
        do = do_ref[...]
        df = (FFN_RES * do).astype(BF)
        dh = jnp.zeros((tm, d), F32)
        for part in range(2):
            cols = slice(part * fc, (part + 1) * fc)
            wg = wg_ref[...].reshape(f, d)[cols]
            wu = wu_ref[...].reshape(f, d)[cols]
            gate = gate_ref[:, cols].astype(F32)
            up = up_ref[:, cols].astype(F32)
            sg = jax.nn.sigmoid(gate)
            silu = gate * sg
            dact = dot_nt(df, wd_ref[...].reshape(f, d)[cols])
            dup = (dact * silu).astype(BF)
            dgate = (dact * up * (sg * (1.0 + gate * (1.0 - sg)))).astype(BF)
            dh = dh + dot_nn(dgate, wg) + dot_nn(dup, wu)
            dgate_ref[:, cols] = dgate
            dup_ref[:, cols] = dup
            act_ref[:, cols] = (silu * up).astype(BF)
        dxn, dgrow = _rms_bwd(dh, xh, rs, g)
        dx_ref[...] = do + dxn

        @pl.when(pl.program_id(0) == 0)
        def _():
            dg_ref[...] = jnp.zeros_like(dg_ref)

        dg_ref[...] += jnp.sum(dgrow, axis=0, keepdims=True)
        h_ref[...] = h
        df_ref[...] = df

    row = pl.BlockSpec((tm, d), lambda i: (i, 0))
    wide = pl.BlockSpec((tm, f), lambda i: (i, 0))
    vec = pl.BlockSpec((1, d), lambda i: (0, 0))
    return pl.pallas_call(
        body, name="ffn_backward_dx", grid=(nt,),
        out_shape=(SDS((s_len, d), F32), SDS((1, d), F32), SDS((3, s_len, f), BF), SDS((2, s_len, d), BF)),
        in_specs=[row, vec, row, wide, wide] + [_w_spec(lay.fh, d, 3 * which + m) for m in range(3)] + [ANY] * len(deps),
        out_specs=(row, vec, pl.BlockSpec((3, tm, f), lambda i: (0, i, 0)), pl.BlockSpec((2, tm, d), lambda i: (0, i, 0))),
        compiler_params=_cparams(),
    )(x, gain, dout, gate_bf, up_bf, wfull, wfull, wfull, *deps)


def weight_grad_tn(lhs, rhs, gb, lay, blk0, tk=2048):
    nmat, s_len, f = lhs.shape
    tk = min(tk, s_len)
    d = rhs.shape[2]
    fc = f // 2
    nk = s_len // tk

    def body(a_ref, b_ref, gb_ref, o_ref, acc):
        kt = pl.program_id(2)

        @pl.when(kt == 0)
        def _():
            acc[...] = jnp.zeros_like(acc)

        acc[...] += dot_tn(a_ref[...], b_ref[...])

        @pl.when(kt == nk - 1)
        def _():
            for p in range(2):
                for q in range(2):
                    o_ref[p, q] = acc[pl.ds((2 * p + q) * lay.fh, lay.fh), :].astype(o_ref.dtype)

    return pl.pallas_call(
        body, name="weight_grad_tn", grid=(nmat, 2, nk), out_shape=SDS(gb.shape, gb.dtype),
        in_specs=[pl.BlockSpec((None, tk, fc), lambda m, j, kt: (m, kt, j)),
                  pl.BlockSpec((None, tk, d), lambda m, j, kt: (jnp.where(m == nmat - 1, 1, 0), kt, 0)), ANY],
        out_specs=pl.BlockSpec((2, 2, lay.fh, d), lambda m, j, kt: (j, 0, blk0 + m, 0)),
        scratch_shapes=[pltpu.VMEM((fc, d), F32)],
        input_output_aliases={2: 0}, compiler_params=_cparams(),
    )(lhs, rhs, gb)


def _lane_blocks(v):
    return [v[:, j * LANE:(j + 1) * LANE] for j in range(v.shape[1] // LANE)]


def _join_lane_blocks(ref):
    return jnp.concatenate([ref[j] for j in range(ref.shape[0])], axis=1)


def _cbm_spec(nblk, rows, first=0):
    return pl.BlockSpec((nblk, rows, LANE), lambda i: (first // nblk, i, 0))


def mix_project(x, gain, wfull, lay, lw, att, tm=512):
    s_len, d = x.shape
    tm = min(tm, s_len)
    d_in = 8 * lay.ih
    kvw = (d_in - 2 * lw - att) // 2
    ncol = (2 * lw + 2 * kvw) // LANE

    def body(x_ref, g_ref, w_ref, o_ref, qt_ref, vt_ref):
        _, _, hn = _rms(x_ref[...], g_ref[...])
        h = hn.astype(BF)
        w = w_ref[:, :, :lay.ih, :].reshape(d_in, d)
        pieces = _lane_blocks(dot_nt(h, w[:2 * lw])) + _lane_blocks(dot_nt(h, w[2 * lw + att:]))
        for j, piece in enumerate(pieces):
            o_ref[j] = piece
        qt_ref[...] = dot_nt(w[2 * lw:2 * lw + att], h)
        vt_ref[...] = dot_nt(w[2 * lw + att + kvw:], h)

    return pl.pallas_call(
        body, name="mix_project", grid=(s_len // tm,),
        out_shape=(SDS((ncol, s_len, LANE), F32), SDS((att, s_len), F32), SDS((kvw, s_len), F32)),
        in_specs=[pl.BlockSpec((tm, d), lambda i: (i, 0)), pl.BlockSpec((1, d), lambda i: (0, 0)),
                  _w_spec(lay.fh, d, lay.MIX_BLK)],
        out_specs=(_cbm_spec(ncol, tm), pl.BlockSpec((att, tm), lambda i: (0, i)), pl.BlockSpec((kvw, tm), lambda i: (0, i))),
        compiler_params=_cparams(),
    )(x, gain, wfull)


def mix_project_backward(x, gain, dout, dxr, dgt, dqt, dkv, dwout, wfull, gb, lay, tm=512):
    s_len, d = x.shape
    tm = min(tm, s_len)
    d_in = 8 * lay.ih
    nt = s_len // tm
    kvw = dkv.shape[1]
    att = dqt.shape[0]
    nlru = (dxr.shape[0] + dgt.shape[0]) * LANE

    def body(x_ref, g_ref, do_ref, dxr_ref, dgt_ref, dqt_ref, dkv_ref, dwo_ref, w_ref, gb_ref, dx_ref, dg_ref, o_ref, acc):
        i = pl.program_id(0)
        g = g_ref[...]
        xh, rs, hn = _rms(x_ref[...], g)
        h = hn.astype(BF)
        w = w_ref[:, :, :lay.ih, :].reshape(d_in, d)
        dlru = jnp.concatenate([_join_lane_blocks(dxr_ref), _join_lane_blocks(dgt_ref)], axis=1).astype(BF)
        dqt = dqt_ref[...].astype(BF)
        dkv = dkv_ref[...].astype(BF)
        dh = dot_nn(dlru, w[:nlru]) + dot_tn(dqt, w[nlru:nlru + att]) + dot_nn(dkv, w[nlru + att:])
        dxn, dgrow = _rms_bwd(dh, xh, rs, g)
        dx_ref[...] = do_ref[...] + dxn

        @pl.when(i == 0)
        def _():
            dg_ref[...] = jnp.zeros_like(dg_ref)
            acc[...] = jnp.zeros_like(acc)

        dg_ref[...] += jnp.sum(dgrow, axis=0, keepdims=True)
        acc[0:nlru, :] += dot_tn(dlru, h)
        acc[nlru:nlru + att, :] += dot_nn(dqt, h)
        acc[nlru + att:, :] += dot_tn(dkv, h)

        @pl.when(i == nt - 1)
        def _():
            for p in range(N_CHIPS):
                for q in range(2):
                    o_ref[p, q, :lay.ih, :] = acc[pl.ds((2 * p + q) * lay.ih, lay.ih), :].astype(o_ref.dtype)
            o_ref[:, :, lay.ih:, :] = dwo_ref[...]

    row = pl.BlockSpec((tm, d), lambda i: (i, 0))
    vec = pl.BlockSpec((1, d), lambda i: (0, 0))
    return pl.pallas_call(
        body, name="mix_project_backward", grid=(nt,),
        out_shape=(SDS((s_len, d), F32), SDS((1, d), F32), SDS(gb.shape, gb.dtype)),
        in_specs=[row, vec, row, _cbm_spec(dxr.shape[0], tm), _cbm_spec(dgt.shape[0], tm),
                  pl.BlockSpec((att, tm), lambda i: (0, i)), pl.BlockSpec((tm, kvw), lambda i: (i, 0)),
                  pl.BlockSpec(dwout.shape, lambda i: (0, 0, 0, 0)), _w_spec(lay.fh, d, lay.MIX_BLK), ANY],
        out_specs=(row, vec, pl.BlockSpec((N_CHIPS, 2, lay.fh, d), lambda i: (0, 0, lay.MIX_BLK, 0))),
        scratch_shapes=[pltpu.VMEM((d_in, d), F32)],
        input_output_aliases={9: 2}, compiler_params=_cparams(),
    )(x, gain, dout, dxr, dgt, dqt, dkv, dwout, wfull, gb)


def _lru_gates(xc, wb_ref, pv_ref, direction):
    xcb = xc.astype(BF)
    r = jax.nn.sigmoid(dot_nn(xcb, wb_ref[2 * direction]) + pv_ref[1 + direction:2 + direction, :])
    i = jax.nn.sigmoid(dot_nn(xcb, wb_ref[2 * direction + 1]) + pv_ref[3 + direction:4 + direction, :])
    lam = pv_ref[5 + direction:6 + direction, :]
    sp = jnp.maximum(-lam, 0.0) + jnp.log(1.0 + jnp.exp(-jnp.abs(lam)))
    a = jnp.exp(-LRU_C * sp * r)
    mult = jnp.sqrt(1.0 - a * a)
    return xcb, r, i, a, mult, sp


def _conv_rows(xr, cv_ref, bias, n):
    acc = bias + cv_ref[0:1, :] * _shift_rows(xr, -2, n)
    for j in range(1, CONV_WIDTH):
        acc = acc + cv_ref[j:j + 1, :] * _shift_rows(xr, j - 2, n)
    return acc


def lru_forward(proj, cvec, pvec, wblk, lw, deps=(), ch=512):
    s_len = proj.shape[1]
    ncb = lw // LANE
    ch = min(ch, s_len)
    nchunk = s_len // ch

    def body(xr_ref, gt_ref, cv_ref, pv_ref, wb_ref, *rest):
        y_ref, hs_ref, xc_s, a_s, u_s, acum_s = rest[len(deps):]
        xc_s[...] = _conv_rows(xr_ref[...], cv_ref, pv_ref[0:1, :], s_len)
        for direction in range(2):
            def fill(ci, _):
                rows = pl.ds(pl.multiple_of(ci * ch, ch), ch)
                xc = xc_s[rows, :]
                _, _, i, a, mult, _ = _lru_gates(xc, wb_ref, pv_ref, direction)
                a_s[rows, :] = a
                u_s[rows, :] = mult * (i * xc)
                return 0

            lax.fori_loop(0, nchunk, fill, 0)
            _scan_rows(a_s, u_s, hs_ref.at[direction], acum_s, reverse=direction == 1)

        def out(ci, _):
            rows = pl.ds(pl.multiple_of(ci * ch, ch), ch)
            gl, _ = _gelu(gt_ref[rows, :])
            y_ref[rows, :] = gl * (hs_ref[0, rows, :] + hs_ref[1, rows, :])
            return 0

        lax.fori_loop(0, nchunk, out, 0)

    col = lambda off: pl.BlockSpec((None, s_len, LANE), lambda cb: (off + cb, 0, 0))
    return pl.pallas_call(
        body, name="lru_forward", grid=(ncb,),
        out_shape=(SDS((ncb, s_len, LANE), F32), SDS((2, ncb, s_len, LANE), F32)),
        in_specs=[col(0), col(ncb), pl.BlockSpec((8, LANE), lambda cb: (0, cb)), pl.BlockSpec((8, LANE), lambda cb: (0, cb)),
                  pl.BlockSpec((4, None, LANE, LANE), lambda cb: (0, cb, 0, 0))] + [ANY] * len(deps),
        out_specs=(col(0), pl.BlockSpec((2, None, s_len, LANE), lambda cb: (0, cb, 0, 0))),
        scratch_shapes=[pltpu.VMEM((s_len, LANE), F32)] * 4, compiler_params=_cparams(),
    )(proj, proj, cvec, pvec, wblk, *deps)


def lru_backward(proj, hs, dy, cvec, pvec, wblk, lw, ch=512):
    s_len = proj.shape[1]
    ncb = lw // LANE
    ch = min(ch, s_len)
    nchunk = s_len // ch

    def body(xr_ref, gt_ref, hs_ref, dy_ref, cv_ref, pv_ref, wb_ref, dxr_ref, dgt_ref, dcv_ref, dpv_ref, dwb_ref,
             xc_s, a_s, dh_s, lam_s, hp_s, dxc_s, acum_s):
        xr = xr_ref[...]
        xc_s[...] = _conv_rows(xr, cv_ref, pv_ref[0:1, :], s_len)
        dxc_s[...] = jnp.zeros_like(dxc_s)
        dpv_ref[...] = jnp.zeros_like(dpv_ref)
        dwb_ref[...] = jnp.zeros_like(dwb_ref)

        def head(ci, _):
            rows = pl.ds(pl.multiple_of(ci * ch, ch), ch)
            gt = gt_ref[rows, :]
            gl, t = _gelu(gt)
            dy = dy_ref[rows, :]
            dh_s[rows, :] = dy * gl
            dgt_ref[rows, :] = dy * (hs_ref[0, rows, :] + hs_ref[1, rows, :]) * _gelu_grad(gt, t)
            return 0

        lax.fori_loop(0, nchunk, head, 0)

        for direction in range(2):
            def fill(ci, _):
                rows = pl.ds(pl.multiple_of(ci * ch, ch), ch)
                _, _, _, a, _, _ = _lru_gates(xc_s[rows, :], wb_ref, pv_ref, direction)
                a_s[rows, :] = a
                return 0

            lax.fori_loop(0, nchunk, fill, 0)
            toward = 1 if direction == 0 else -1
            hp_s[...] = _shift_rows(a_s[...], toward, s_len)
            _scan_rows(hp_s, dh_s, lam_s, acum_s, reverse=direction == 0)
            hp_s[...] = _shift_rows(hs_ref[direction], -toward, s_len)

            def grads(ci, _):
                rows = pl.ds(pl.multiple_of(ci * ch, ch), ch)
                xc = xc_s[rows, :]
                xcb, r, i, a, mult, sp = _lru_gates(xc, wb_ref, pv_ref, direction)
                du = lam_s[rows, :]
                da = du * hp_s[rows, :]
                dmult = du * i * xc
                di = du * mult * xc
                dlog_a = (da - dmult * a / mult) * a
                dr = dlog_a * (-LRU_C * sp)
                dza = dr * r * (1.0 - r)
                dzx = di * i * (1.0 - i)
                dzab = dza.astype(BF)
                dzxb = dzx.astype(BF)
                dxc_s[rows, :] += (du * mult * i + dot_nt(dzab, wb_ref[2 * direction])
                                   + dot_nt(dzxb, wb_ref[2 * direction + 1]))
                dwb_ref[2 * direction] += dot_tn(xcb, dzab)
                dwb_ref[2 * direction + 1] += dot_tn(xcb, dzxb)
                dpv_ref[1 + direction:2 + direction, :] += jnp.sum(dza, axis=0, keepdims=True)
                dpv_ref[3 + direction:4 + direction, :] += jnp.sum(dzx, axis=0, keepdims=True)
                dpv_ref[5 + direction:6 + direction, :] += jnp.sum(dlog_a * (-LRU_C * r), axis=0, keepdims=True)
                return 0

            lax.fori_loop(0, nchunk, grads, 0)

        for direction in range(2):
            lam = pv_ref[5 + direction:6 + direction, :]
            dpv_ref[5 + direction:6 + direction, :] = dpv_ref[5 + direction:6 + direction, :] * (-jax.nn.sigmoid(-lam))
        dxc = dxc_s[...]
        dpv_ref[0:1, :] = jnp.sum(dxc, axis=0, keepdims=True)
        dxr = cv_ref[0:1, :] * _shift_rows(dxc, 2, s_len)
        for j in range(1, CONV_WIDTH):
            dxr = dxr + cv_ref[j:j + 1, :] * _shift_rows(dxc, 2 - j, s_len)
        dxr_ref[...] = dxr
        dcv_ref[...] = jnp.zeros_like(dcv_ref)
        for j in range(CONV_WIDTH):
            dcv_ref[j:j + 1, :] = jnp.sum(dxc * _shift_rows(xr, j - 2, s_len), axis=0, keepdims=True)

    col = lambda off: pl.BlockSpec((None, s_len, LANE), lambda cb: (off + cb, 0, 0))
    own = col(0)
    small = pl.BlockSpec((8, LANE), lambda cb: (0, cb))
    wspec = pl.BlockSpec((4, None, LANE, LANE), lambda cb: (0, cb, 0, 0))
    return pl.pallas_call(
        body, name="lru_backward", grid=(ncb,),
        out_shape=(SDS((ncb, s_len, LANE), F32), SDS((ncb, s_len, LANE), F32), SDS((8, lw), F32), SDS((8, lw), F32),
                   SDS(wblk.shape, F32)),
        in_specs=[col(0), col(ncb), pl.BlockSpec((2, None, s_len, LANE), lambda cb: (0, cb, 0, 0)), own, small, small, wspec],
        out_specs=(own, own, small, small, wspec),
        scratch_shapes=[pltpu.VMEM((s_len, LANE), F32)] * 7, compiler_params=_cparams(),
    )(proj, proj, hs, dy, cvec, pvec, wblk)


def _window_specs(s_len, first, width=None):
    nb = s_len // BLOCK
    where = (lambda n: jnp.maximum(n - 1, 0), lambda n: n, lambda n: jnp.minimum(n + 1, nb - 1))
    if width is None:
        return [pl.BlockSpec((None, BLOCK, LANE), lambda n, f=f: (first, f(n), 0)) for f in where]
    return [pl.BlockSpec((width, BLOCK), lambda n, f=f: (0, f(n))) for f in where]


def _stack_heads(v, kh):
    return jnp.concatenate([v[(kh * KV_GROUP + g) * HEAD_DIM:(kh * KV_GROUP + g + 1) * HEAD_DIM, :]
                            for g in range(KV_GROUP)], axis=1)


def _unstack_heads(ref, kh, v):
    for g in range(KV_GROUP):
        h = kh * KV_GROUP + g
        ref[h * HEAD_DIM:(h + 1) * HEAD_DIM, :] = v[:, g * BLOCK:(g + 1) * BLOCK]


def _key_exists(n, nb):
    j = lax.broadcasted_iota(jnp.int32, (3 * BLOCK, 1), 0)
    return ((n > 0) | (j >= BLOCK)) & ((n < nb - 1) | (j < 2 * BLOCK))


def _attn_probs(qs, kcat, bias_g, sink_g, key_ok):
    logits = jnp.where(key_ok, dot_nn(kcat, qs) + bias_g, NEG_INF)
    m = jnp.maximum(jnp.max(logits, axis=0, keepdims=True), sink_g)
    p = jnp.exp(logits - m)
    es = jnp.exp(sink_g - m)
    inv = 1.0 / (jnp.sum(p, axis=0, keepdims=True) + es)
    return p * inv, es * inv


def attention_forward(qt, proj, vt, bias, sink, kblk):
    att, s_len = qt.shape
    kvw = vt.shape[0]
    nb = s_len // BLOCK

    def body(q_ref, kp_ref, kc_ref, kn_ref, vp_ref, vc_ref, vn_ref, b_ref, s_ref, o_ref):
        n = pl.program_id(0)
        q = q_ref[...]
        key_ok = _key_exists(n, nb)
        kall = jnp.concatenate([kp_ref[...], kc_ref[...], kn_ref[...]], axis=0).astype(BF)
        vall = jnp.concatenate([vp_ref[...], vc_ref[...], vn_ref[...]], axis=1).astype(BF)
        for kh in range(N_KV_HEADS):
            qs = (_stack_heads(q, kh) * (HEAD_DIM ** -0.5)).astype(BF)
            p, _ = _attn_probs(qs, kall[:, kh * HEAD_DIM:(kh + 1) * HEAD_DIM], b_ref[kh], s_ref[kh, 0:1, :], key_ok)
            _unstack_heads(o_ref, kh, dot_nn(vall[kh * HEAD_DIM:(kh + 1) * HEAD_DIM, :], p.astype(BF)))

    blk = pl.BlockSpec((att, BLOCK), lambda n: (0, n))
    return pl.pallas_call(
        body, name="attention_forward", grid=(nb,), out_shape=SDS((att, s_len), F32),
        in_specs=[blk] + _window_specs(s_len, kblk) + _window_specs(s_len, 0, kvw)
        + [pl.BlockSpec(bias.shape, lambda n: (0, 0, 0)), pl.BlockSpec(sink.shape, lambda n: (0, 0, 0))],
        out_specs=blk, compiler_params=_cparams(),
    )(qt, proj, proj, proj, vt, vt, vt, bias, sink)


def attention_backward(qt, proj, y_att, dy, bias, sink, kblk):
    att, s_len = qt.shape
    nb = s_len // BLOCK
    kvw = N_KV_HEADS * HEAD_DIM

    def body(q_ref, kp_ref, kc_ref, kn_ref, vp_ref, vc_ref, vn_ref, o_ref, do_ref, b_ref, s_ref,
             dq_ref, dkv_ref, db_ref, ds_ref):
        n = pl.program_id(0)

        @pl.when(n == 0)
        def _():
            dkv_ref[...] = jnp.zeros_like(dkv_ref)
            db_ref[...] = jnp.zeros_like(db_ref)
            ds_ref[...] = jnp.zeros_like(ds_ref)

        q = q_ref[...]
        o = o_ref[...]
        do = do_ref[...]
        kall = jnp.concatenate([kp_ref[...], kc_ref[...], kn_ref[...]], axis=0).astype(BF)
        vall = jnp.concatenate([vp_ref[...], vc_ref[...], vn_ref[...]], axis=0).astype(BF)
        key_ok = _key_exists(n, nb)
        dks, dvs = [], []
        for kh in range(N_KV_HEADS):
            kcat = kall[:, kh * HEAD_DIM:(kh + 1) * HEAD_DIM]
            vcat = vall[:, kh * HEAD_DIM:(kh + 1) * HEAD_DIM]
            qs = (_stack_heads(q, kh) * (HEAD_DIM ** -0.5)).astype(BF)
            p, ps = _attn_probs(qs, kcat, b_ref[kh], s_ref[kh, 0:1, :], key_ok)
            dos = _stack_heads(do, kh)
            dosb = dos.astype(BF)
            delta = jnp.sum(dos * _stack_heads(o, kh), axis=0, keepdims=True)
            dlog = p * (dot_nn(vcat, dosb) - delta)
            dlogb = dlog.astype(BF)
            db_ref[kh] += dlog
            ds_ref[kh] += jnp.broadcast_to(-ps * delta, ds_ref.shape[1:])
            _unstack_heads(dq_ref, kh, dot_tn(kcat, dlogb) * (HEAD_DIM ** -0.5))
            dks.append(dot_nt(dlogb, qs))
            dvs.append(dot_nt(p.astype(BF), dosb))
        dkv = jnp.concatenate(dks + dvs, axis=1)
        starts = [jnp.maximum(n - 1, 0), n, jnp.minimum(n + 1, nb - 1)]
        for b, st in enumerate(starts):
            rows = pl.ds(pl.multiple_of(st * BLOCK, BLOCK), BLOCK)
            dkv_ref[rows, :] += dkv[b * BLOCK:(b + 1) * BLOCK, :]

    blk = pl.BlockSpec((att, BLOCK), lambda n: (0, n))
    whole = lambda a: pl.BlockSpec(a.shape, lambda n: (0, 0, 0))
    return pl.pallas_call(
        body, name="attention_backward", grid=(nb,),
        out_shape=(SDS((att, s_len), F32), SDS((s_len, 2 * kvw), F32), SDS(bias.shape, F32), SDS(sink.shape, F32)),
        in_specs=[blk] + _window_specs(s_len, kblk) + _window_specs(s_len, kblk + 1) + [blk, blk, whole(bias), whole(sink)],
        out_specs=(blk, pl.BlockSpec((s_len, 2 * kvw), lambda n: (0, 0)), whole(bias), whole(sink)),
        compiler_params=_cparams(),
    )(qt, proj, proj, proj, proj, proj, proj, y_att, dy, bias, sink)


def _rms_cols(x, g):
    rs = lax.rsqrt(jnp.mean(x * x, axis=0, keepdims=True) + EPS)
    xh = x * rs
    return xh, rs, xh * g


def _rms_cols_bwd(dy, xh, rs, g):
    dxh = dy * g
    dx = rs * (dxh - xh * jnp.mean(dxh * xh, axis=0, keepdims=True))
    return dx, dy * xh


def mix_output(x, y_rec, y_att, g_rec, g_att, wfull, lay, tm=512):
    s_len, d = x.shape
    tm = min(tm, s_len)
    lw = y_rec.shape[0] * LANE
    att = y_att.shape[0]

    def body(x_ref, yr_ref, ya_ref, gr_ref, ga_ref, w_ref, o_ref):
        _, _, nr = _rms(_join_lane_blocks(yr_ref), gr_ref[...])
        _, _, na = _rms_cols(ya_ref[...], ga_ref[...])
        w = w_ref[:, :, lay.ih:, :].reshape(d, d)
        o_ref[...] = x_ref[...] + dot_nn(nr.astype(BF), w[:lw]) + dot_tn(na.astype(BF), w[lw:])

    row = pl.BlockSpec((tm, d), lambda i: (i, 0))
    return pl.pallas_call(
        body, name="mix_output", grid=(s_len // tm,), out_shape=SDS((s_len, d), F32),
        in_specs=[row, _cbm_spec(lw // LANE, tm), pl.BlockSpec((att, tm), lambda i: (0, i)),
                  pl.BlockSpec((1, lw), lambda i: (0, 0)), pl.BlockSpec((att, 1), lambda i: (0, 0)),
                  _w_spec(lay.fh, d, lay.MIX_BLK)],
        out_specs=row, compiler_params=_cparams(),
    )(x, y_rec, y_att, g_rec, g_att, wfull)


def mix_output_backward(dout, y_rec, y_att, g_rec, g_att, wfull, lay, deps=(), tm=1024):
    s_len, d = dout.shape
    tm = min(tm, s_len)
    lw = y_rec.shape[0] * LANE
    att = y_att.shape[0]
    nt = s_len // tm

    def body(do_ref, yr_ref, ya_ref, gr_ref, ga_ref, w_ref, *rest):
        dyr_ref, dya_ref, dgr_ref, dga_ref, o_ref, acc = rest[len(deps):]
        i = pl.program_id(0)
        gr = gr_ref[...]
        ga = ga_ref[...]
        xhr, rsr, nr = _rms(_join_lane_blocks(yr_ref), gr)
        xha, rsa, na = _rms_cols(ya_ref[...], ga)
        dob = do_ref[...].astype(BF)
        w = w_ref[:, :, lay.ih:, :].reshape(d, d)
        dyr, dgr_row = _rms_bwd(dot_nt(dob, w[:lw]), xhr, rsr, gr)
        dya, dga_col = _rms_cols_bwd(dot_nt(w[lw:], dob), xha, rsa, ga)
        for j, piece in enumerate(_lane_blocks(dyr)):
            dyr_ref[j] = piece
        dya_ref[...] = dya

        @pl.when(i == 0)
        def _():
            dgr_ref[...] = jnp.zeros_like(dgr_ref)
            dga_ref[...] = jnp.zeros_like(dga_ref)
            acc[...] = jnp.zeros_like(acc)

        dgr_ref[...] += jnp.sum(dgr_row, axis=0, keepdims=True)
        dga_ref[...] += jnp.sum(dga_col, axis=1, keepdims=True)
        acc[0:lw, :] += dot_tn(nr.astype(BF), dob)
        acc[lw:, :] += dot_nn(na.astype(BF), dob)

        @pl.when(i == nt - 1)
        def _():
            for p in range(N_CHIPS):
                for q in range(2):
                    o_ref[p, q] = acc[pl.ds((2 * p + q) * lay.oh, lay.oh), :].astype(o_ref.dtype)

    row = pl.BlockSpec((tm, d), lambda i: (i, 0))
    return pl.pallas_call(
        body, name="mix_output_backward", grid=(nt,),
        out_shape=(SDS(y_rec.shape, F32), SDS(y_att.shape, F32), SDS((1, lw), F32), SDS((att, 1), F32),
                   SDS((N_CHIPS, 2, lay.oh, d), BF)),
        in_specs=[row, _cbm_spec(lw // LANE, tm), pl.BlockSpec((att, tm), lambda i: (0, i)),
                  pl.BlockSpec((1, lw), lambda i: (0, 0)), pl.BlockSpec((att, 1), lambda i: (0, 0)),
                  _w_spec(lay.fh, d, lay.MIX_BLK)] + [ANY] * len(deps),
        out_specs=(_cbm_spec(lw // LANE, tm), pl.BlockSpec((att, tm), lambda i: (0, i)),
                   pl.BlockSpec((1, lw), lambda i: (0, 0)), pl.BlockSpec((att, 1), lambda i: (0, 0)),
                   pl.BlockSpec((N_CHIPS, 2, lay.oh, d), lambda i: (0, 0, 0, 0))),
        scratch_shapes=[pltpu.VMEM((d, d), F32)], compiler_params=_cparams(),
    )(dout, y_rec, y_att, g_rec, g_att, wfull, *deps)


def loss_head(x, gain, target, tm=512):
    s_len, d = x.shape
    tm = min(tm, s_len)

    def body(x_ref, g_ref, t_ref, dx_ref, dg_ref, loss_ref):
        g = g_ref[...]
        xh, rs, y = _rms(x_ref[...], g)
        err = y - t_ref[...]

        @pl.when(pl.program_id(0) == 0)
        def _():
            dg_ref[...] = jnp.zeros_like(dg_ref)
            loss_ref[...] = jnp.zeros_like(loss_ref)

        part = 0.5 * jnp.sum(jnp.mean(err * err, axis=-1, keepdims=True), axis=0, keepdims=True)
        loss_ref[...] += jnp.broadcast_to(part, loss_ref.shape)
        dx, dgrow = _rms_bwd(err * (1.0 / d), xh, rs, g)
        dx_ref[...] = dx
        dg_ref[...] += jnp.sum(dgrow, axis=0, keepdims=True)

    row = pl.BlockSpec((tm, d), lambda i: (i, 0))
    vec = pl.BlockSpec((1, d), lambda i: (0, 0))
    return pl.pallas_call(
        body, name="loss_head", grid=(s_len // tm,),
        out_shape=(SDS((s_len, d), F32), SDS((1, d), F32), SDS((8, LANE), F32)),
        in_specs=[row, vec, row], out_specs=(row, vec, pl.BlockSpec((8, LANE), lambda i: (0, 0))),
        compiler_params=_cparams(),
    )(x, gain, target)


def _adamw_update(w, g, m, v):
    m = ADAM_B1 * m + (1.0 - ADAM_B1) * g
    v = ADAM_B2 * v + (1.0 - ADAM_B2) * (g * g)
    m_hat = m / (1.0 - ADAM_B1 ** ADAM_STEP)
    v_hat = v / (1.0 - ADAM_B2 ** ADAM_STEP)
    return -ADAM_LR * (m_hat / (jnp.sqrt(v_hat) + ADAM_EPS) + ADAM_WD * w), m, v


def adamw(w, g, m, v, tr=512):
    rows, cols = w.shape
    tr = _row_chunk(rows, tr, 8)

    def body(w_ref, g_ref, m_ref, v_ref, d_ref, nm_ref, nv_ref):
        d_ref[...], nm_ref[...], nv_ref[...] = _adamw_update(w_ref[...], g_ref[...], m_ref[...], v_ref[...])

    blk = pl.BlockSpec((tr, cols), lambda i: (i, 0))
    return pl.pallas_call(
        body, name="adamw", grid=(rows // tr,), out_shape=(SDS(w.shape, F32),) * 3,
        in_specs=[blk] * 4, out_specs=(blk,) * 3, compiler_params=_cparams(),
    )(w, g, m, v)


def adamw_layer(gf, blk, row_off, n_half, l, w, m, v, outs, deps=()):
    fh = gf.shape[1] // Layout.BLOCKS
    d = gf.shape[2]
    nd = len(deps)

    def body(gf_ref, w_ref, m_ref, v_ref, *rest):
        g_ref, d_ref, nm_ref, nv_ref = rest[4 + nd:]
        g = gf_ref[row_off:row_off + n_half, :]
        g_ref[...] = g
        d_ref[...], nm_ref[...], nv_ref[...] = _adamw_update(w_ref[...], g, m_ref[...], v_ref[...])

    gspec = pl.BlockSpec((None, fh, d), lambda h: (h, blk, 0))
    wspec = pl.BlockSpec((None, n_half, d), lambda h: (l, h, 0))
    return pl.pallas_call(
        body, name="adamw_layer", grid=(2,), out_shape=tuple(SDS(o.shape, o.dtype) for o in outs),
        in_specs=[gspec, wspec, wspec, wspec] + [ANY] * (4 + nd), out_specs=(wspec,) * 4,
        input_output_aliases={4 + i: i for i in range(4)}, compiler_params=_cparams(),
    )(gf, w, m, v, *outs, *deps)


def pack_weight(pos, land, blk, l, w, extra=None, deps=()):
    fh, d = land.shape[2] // Layout.BLOCKS, land.shape[3]
    nd = len(deps)

    def body(pos_ref, w_ref, *rest):
        o_ref = rest[-1]
        a = w_ref[...].astype(BF)
        n = a.shape[0] // 2
        for h in range(2):
            o_ref[h, 0:n, :] = a[h * n:(h + 1) * n]
        if extra is not None:
            b = rest[0][...].astype(BF)
            nb = b.shape[0] // 2
            for h in range(2):
                o_ref[h, n:n + nb, :] = b[h * nb:(h + 1) * nb]

    def whole(a):
        return pl.BlockSpec((None,) + a.shape[1:], lambda i, p: (l, 0, 0))

    ins = [w] + ([extra] if extra is not None else [])
    return pl.pallas_call(
        body, name="pack_weight", out_shape=SDS(land.shape, land.dtype),
        grid_spec=pltpu.PrefetchScalarGridSpec(
            num_scalar_prefetch=1, grid=(1,),
            in_specs=[whole(a) for a in ins] + [ANY] * (1 + nd),
            out_specs=pl.BlockSpec((None, 2, fh, d), lambda i, p: (p[0], 0, blk, 0))),
        input_output_aliases={1 + len(ins): 0}, compiler_params=_cparams(),
    )(pos, *ins, land, *deps)


def _rows_of(shape, width):
    return -(-int(np.prod(shape)) // (SUBLANES * width)) * SUBLANES


def _pack_rows(arrays, width):
    parts = []
    for a in arrays:
        flat = a.reshape(-1).astype(F32)
        r = _rows_of(a.shape, width)
        parts.append(jnp.pad(flat, (0, r * width - flat.shape[0])).reshape(r, width))
    return jnp.concatenate(parts, axis=0)


def _unpack_rows(buf, shapes):
    out, row = [], 0
    for shp in shapes:
        r = _rows_of(shp, buf.shape[1])
        out.append(buf[row:row + r].reshape(-1)[:int(np.prod(shp))].reshape(shp))
        row += r
    return out


def _t5_buckets(rel):
    half = N_BUCKETS // 2
    max_exact = half // 2
    ret = (rel > 0).astype(jnp.int32) * half
    n = jnp.abs(rel)
    n_f = jnp.maximum(n, 1).astype(F32)
    large = max_exact + (jnp.log(n_f / max_exact) / math.log(MAX_DISTANCE / max_exact) * (half - max_exact)).astype(jnp.int32)
    large = jnp.minimum(large, half - 1)
    return ret + jnp.where(n < max_exact, n, large)


def _band_buckets():
    t = jnp.arange(BLOCK)[:, None]
    j = jnp.arange(3 * BLOCK)[None, :]
    rel = j - BLOCK - t
    return _t5_buckets(rel), jnp.abs(rel) <= WINDOW


def _block_diag_pairs(w):
    depth, two, nblk, bw, _ = w.shape
    pairs = w.reshape(depth, two, nblk // 2, 2, bw, bw)
    z = jnp.zeros_like(pairs[:, :, :, 0])
    top = jnp.concatenate([pairs[:, :, :, 0], z], axis=-1)
    bot = jnp.concatenate([z, pairs[:, :, :, 1]], axis=-1)
    return jnp.concatenate([top, bot], axis=-2)


def _diag_blocks(dw):
    bw = dw.shape[-1] // 2
    a = dw[:, :, :bw, :bw]
    b = dw[:, :, bw:, bw:]
    return jnp.stack([a, b], axis=2).reshape(dw.shape[0], 2 * dw.shape[1], bw, bw)


def kernel(x, ffn1_norm, ffn1_w_gate, ffn1_w_up, ffn1_w_down, mix_norm, w_in, conv_w, conv_b, lru_w_a, lru_b_a, lru_w_x, lru_b_x, lru_lambda, attn_sink, rel_bias, lru_out_norm, attn_out_norm, w_out, ffn2_norm, ffn2_w_gate, ffn2_w_up, ffn2_w_down, final_norm, loss_target, m_ffn1_norm, m_ffn1_w_gate, m_ffn1_w_up, m_ffn1_w_down, m_mix_norm, m_w_in, m_conv_w, m_conv_b, m_lru_w_a, m_lru_b_a, m_lru_w_x, m_lru_b_x, m_lru_lambda, m_attn_sink, m_rel_bias, m_lru_out_norm, m_attn_out_norm, m_w_out, m_ffn2_norm, m_ffn2_w_gate, m_ffn2_w_up, m_ffn2_w_down, m_final_norm, v_ffn1_norm, v_ffn1_w_gate, v_ffn1_w_up, v_ffn1_w_down, v_mix_norm, v_w_in, v_conv_w, v_conv_b, v_lru_w_a, v_lru_b_a, v_lru_w_x, v_lru_b_x, v_lru_lambda, v_attn_sink, v_rel_bias, v_lru_out_norm, v_attn_out_norm, v_w_out, v_ffn2_norm, v_ffn2_w_gate, v_ffn2_w_up, v_ffn2_w_down, v_final_norm):
    depth, d = ffn1_norm.shape
    d_ff = N_CHIPS * ffn1_w_gate.shape[2]
    d_in = N_CHIPS * w_in.shape[2]
    lw = conv_b.shape[1]
    att = N_HEADS * HEAD_DIM
    lay = Layout(d, d_ff, d_in)
    k_chip = 2 * lax.axis_index("x") + lax.axis_index("y")
    pos = jnp.stack([k_chip, lax.axis_index("c")]).astype(jnp.int32)

    def rows_major(a):
        return jnp.swapaxes(a, 1, 2)

    mats = (rows_major(ffn1_w_gate), rows_major(ffn1_w_up), ffn1_w_down,
            rows_major(ffn2_w_gate), rows_major(ffn2_w_up), ffn2_w_down)

    def pack_layer(l, deps=()):
        land = lax.empty((N_CHIPS, 2, lay.rows, d), BF)
        for m, a in enumerate(mats):
            land = pack_weight(pos, land, m, l, a, deps=deps if m == 0 else ())
        return pack_weight(pos, land, lay.MIX_BLK, l, rows_major(w_in), extra=w_out)

    def gather_start(l, land):
        return split_start(f"gather_start_{l}", [land], 3, gather_plan)

    def gather_wait(l, started, after):
        ssem, rsem, bufs, _ = started
        return split_wait(f"gather_wait_{l}", ssem, rsem, bufs, after, gather_plan)

    sharded_small = (conv_w, lru_b_a, lru_b_x, lru_lambda)
    sshard = jnp.concatenate([a.reshape(-1, LANE) for a in sharded_small], axis=0)
    sfull = gather_small(sshard)
    small_full, off = [], 0
    for a in sharded_small:
        r = a.shape[0] * a.shape[1]
        piece = sfull[:, off:off + r].reshape((N_CHIPS,) + a.shape)
        small_full.append(jnp.moveaxis(piece, 0, 2).reshape(a.shape[0], a.shape[1], N_CHIPS * LANE))
        off += r
    conv_w_f, b_a_f, b_x_f, lam_f = small_full

    zrow = jnp.zeros((1, lw), F32)
    wblk_a = _block_diag_pairs(lru_w_a)
    wblk_x = _block_diag_pairs(lru_w_x)
    buckets, in_band = _band_buckets()
    onehot = (buckets.reshape(-1)[:, None] == jnp.arange(N_BUCKETS)[None, :]).astype(F32)
    bias = jnp.dot(rel_bias.T, onehot.T, precision=lax.Precision.HIGHEST).reshape(N_HEADS, BLOCK, 3 * BLOCK)
    bias = jnp.where(in_band[None], bias, NEG_INF)
    bias = bias.reshape(N_KV_HEADS, KV_GROUP, BLOCK, 3 * BLOCK).transpose(0, 3, 1, 2).reshape(N_KV_HEADS, 3 * BLOCK, KV_GROUP * BLOCK)
    kblk = 2 * lw // LANE

    def layer_small(l):
        cvec = jnp.concatenate([conv_w_f[l], jnp.zeros((8 - CONV_WIDTH, lw), F32)], axis=0)
        pvec = jnp.concatenate([conv_b[l][None], b_a_f[l], b_x_f[l], lam_f[l], zrow], axis=0)
        wblk = jnp.stack([wblk_a[l, 0], wblk_x[l, 0], wblk_a[l, 1], wblk_x[l, 1]]).astype(BF)
        sink = jnp.broadcast_to(jnp.repeat(attn_sink[l], BLOCK).reshape(N_KV_HEADS, 1, KV_GROUP * BLOCK),
                                (N_KV_HEADS, 8, KV_GROUP * BLOCK))
        return cvec, pvec, wblk, sink

    xs = x[0]
    wfull = [None] * depth
    parts = [(0, 3 * lay.fh), (3 * lay.fh, lay.rows - 3 * lay.fh)]
    plans = [(functools.partial(gather_plan, rows=p), functools.partial(handover_plan, rows=p)) for p in parts]
    land = pack_layer(0, deps=(sfull,))
    first = split_start("gather_start_0a", [land], 3, plans[0][0])
    second = split_start("gather_start_0b", first[2], 3, plans[1][0])
    lands = {l: pack_layer(l, deps=(second[3],)) for l in range(1, depth)}
    land = split_wait("gather_wait_0a", first[0], first[1], second[2], [xs] + list(lands.values()), plans[0][0])
    wfull[0], = exchange_now("gather_handover_0a", land, 3, plans[0][1])
    started = None
    saved = []
    for l in range(depth):
        cvec, pvec, wblk, sink = layer_small(l)
        deps = (started[3],) if started is not None else ()
        x1, gate1, up1 = ffn_forward(xs, ffn1_norm[l][None], wfull[l], lay, 0, deps=deps)
        deps = ()
        if l == 0:
            land = split_wait("gather_wait_0b", second[0], second[1], [wfull[0]], [x1], plans[1][0])
            wfull[0], = exchange_now("gather_handover_0b", land, 3, plans[1][1])
            if depth > 1:
                started = gather_start(1, lands[1])
                deps = (started[3],)
        proj, qt, vt = mix_project(x1, mix_norm[l][None], wfull[l], lay, lw, att)
        y_rec, hs = lru_forward(proj, cvec, pvec, wblk, lw, deps=deps)
        y_att = attention_forward(qt, proj, vt, bias, sink, kblk)
        x2 = mix_output(x1, y_rec, y_att, lru_out_norm[l][None], attn_out_norm[l][:, None], wfull[l], lay)
        deps, handover = (), None
        if 0 < l < depth - 1:
            land, = gather_wait(l + 1, started, [x2])
            started = gather_start(l + 2, lands[l + 2]) if l + 2 < depth else None
            handover = split_start(f"gather_handover_start_{l + 1}", [land], 3, handover_plan)
            deps = (handover[3],) + ((started[3],) if started is not None else ())
        x3, gate2, up2 = ffn_forward(x2, ffn2_norm[l][None], wfull[l], lay, 1, deps=deps)
        saved.append((xs, x1, x2, proj, qt, y_rec, hs, y_att, (gate1, up1), (gate2, up2)))
        xs = x3
        if handover is not None:
            wfull[l + 1], = split_wait(f"gather_handover_wait_{l + 1}", handover[0], handover[1], handover[2], [x3],
                                       handover_plan)
        elif l == 0 and depth > 1:
            wfull[1], = exchange_now("gather_handover_1", gather_wait(1, started, [x3]), 3, handover_plan)
            started = gather_start(2, lands[2]) if depth > 2 else None

    dx, d_final, loss_tile = loss_head(xs, final_norm[None], loss_target[0])
    loss = lax.psum(loss_tile[0, 0], ("x", "y", "c"))

    layer_names = ["ffn1_norm", "mix_norm", "conv_w", "conv_b", "lru_w_a", "lru_b_a", "lru_w_x", "lru_b_x", "lru_lambda",
                   "attn_sink", "lru_out_norm", "attn_out_norm", "ffn2_norm"]
    dbias_total = jnp.zeros(bias.shape, F32)

    def ffn_back(xin, gain, dout, pre, gb, l, which, deps=()):
        dxo, dg, lhs, rhs = ffn_backward_dx(xin, gain, dout, *pre, wfull[l], lay, which, deps=deps)
        return dxo, dg[0], weight_grad_tn(lhs, rhs, gb, lay, 3 * which)

    def pair_start(l, gb, sb):
        lands = [lax.empty((N_CHIPS,) + gb.shape[2:], gb.dtype), lax.empty(sb.shape, sb.dtype)]
        return split_start(f"pair_start_{l}", [gb, sb] + lands, N_CHIPS + 1, pair_plan)

    def reduce_start(l, paired, after):
        gb, sb, p1, sp1 = split_wait(f"pair_wait_{l}", paired[0], paired[1], paired[2], after, pair_plan)
        cs = pair_sum(pos, gb, p1)
        ss = small_pair_sum(sb, sp1)
        lands = [lax.empty((3,) + cs.shape[1:], cs.dtype), lax.empty((N_CHIPS,) + ss.shape, ss.dtype)]
        return split_start(f"reduce_start_{l}", [cs, ss] + lands, 6, reduce_plan)

    def reduce_finish(l, started, after):
        ssem, rsem, bufs, _ = started
        cs, ss, p3, sp3 = split_wait(f"reduce_wait_{l}", ssem, rsem, bufs, after, reduce_plan)
        return chip_sum(pos, cs, p3), small_chip_sum(pos, ss, sp3)

    gf = [None] * depth
    small_sums = [None] * depth
    small_shapes = [None] * depth
    paired = None
    in_flight = None
    finals = {}
    tokens = []
    for l in reversed(range(depth)):
        x0, x1, x2, proj, qt, y_rec, hs, y_att, pre1, pre2 = saved[l]
        cvec, pvec, wblk, sink = layer_small(l)
        gb = lax.empty((N_CHIPS, 2, lay.rows, d), BF)
        part = {}
        dx, part["ffn2_norm"], gb = ffn_back(x2, ffn2_norm[l][None], dx, pre2, gb, l, 1, deps=tuple(tokens))
        deps = ()
        if paired is not None:
            in_flight = (paired[0], reduce_start(paired[0], paired[1], [dx, gb]))
            deps = (in_flight[1][3],)
        dyr, dya, dgr, dga, dwout = mix_output_backward(dx, y_rec, y_att, lru_out_norm[l][None], attn_out_norm[l][:, None],
                                                        wfull[l], lay, deps=deps)
        part["lru_out_norm"] = dgr[0]
        part["attn_out_norm"] = dga[:, 0]
        dq, dkv, dbias, dsink = attention_backward(qt, proj, y_att, dya, bias, sink, kblk)
        dbias_total = dbias_total + dbias
        part["attn_sink"] = jnp.sum(dsink[:, 0, :].reshape(N_HEADS, BLOCK), axis=1)
        dxr, dgt, dcv, dpv, dwb = lru_backward(proj, hs, dyr, cvec, pvec, wblk, lw)
        part["conv_w"] = dcv[:CONV_WIDTH]
        part["conv_b"] = dpv[0]
        part["lru_b_a"] = dpv[1:3]
        part["lru_b_x"] = dpv[3:5]
        part["lru_lambda"] = dpv[5:7]
        part["lru_w_a"] = _diag_blocks(jnp.stack([dwb[0], dwb[2]]))
        part["lru_w_x"] = _diag_blocks(jnp.stack([dwb[1], dwb[3]]))
        dx, dgm, gb = mix_project_backward(x1, mix_norm[l][None], dx, dxr, dgt, dq, dkv, dwout, wfull[l], gb, lay)
        part["mix_norm"] = dgm[0]
        dx, part["ffn1_norm"], gb = ffn_back(x0, ffn1_norm[l][None], dx, pre1, gb, l, 0)
        pieces = [part[n] for n in layer_names]
        if l == 0:
            dbias_heads = dbias_total.reshape(N_KV_HEADS, 3 * BLOCK, KV_GROUP, BLOCK).transpose(0, 2, 3, 1)
            d_rel_bias = jnp.dot(dbias_heads.reshape(N_HEADS, -1), onehot, precision=lax.Precision.HIGHEST).T
            pieces += [d_rel_bias, d_final[0]]
        small_shapes[l] = [p.shape for p in pieces]
        paired = (l, pair_start(l, gb, _pack_rows(pieces, 1024)))
        tokens = [paired[1][3]]
        if in_flight is not None:
            above = in_flight[0]
            half, small_sums[above] = reduce_finish(above, in_flight[1], [dx])
            finals[above] = split_start(f"final_start_{above}", [half], 1, final_plan)
            tokens.append(finals[above][3])
            in_flight = None
    grad_x = dx[None]

    weights = dict(ffn1_norm=ffn1_norm, ffn1_w_gate=ffn1_w_gate, ffn1_w_up=ffn1_w_up, ffn1_w_down=ffn1_w_down, mix_norm=mix_norm, w_in=w_in, conv_w=conv_w, conv_b=conv_b, lru_w_a=lru_w_a, lru_b_a=lru_b_a, lru_w_x=lru_w_x, lru_b_x=lru_b_x, lru_lambda=lru_lambda, attn_sink=attn_sink, rel_bias=rel_bias, lru_out_norm=lru_out_norm, attn_out_norm=attn_out_norm, w_out=w_out, ffn2_norm=ffn2_norm, ffn2_w_gate=ffn2_w_gate, ffn2_w_up=ffn2_w_up, ffn2_w_down=ffn2_w_down, final_norm=final_norm)
    m_in = dict(ffn1_norm=m_ffn1_norm, ffn1_w_gate=m_ffn1_w_gate, ffn1_w_up=m_ffn1_w_up, ffn1_w_down=m_ffn1_w_down, mix_norm=m_mix_norm, w_in=m_w_in, conv_w=m_conv_w, conv_b=m_conv_b, lru_w_a=m_lru_w_a, lru_b_a=m_lru_b_a, lru_w_x=m_lru_w_x, lru_b_x=m_lru_b_x, lru_lambda=m_lru_lambda, attn_sink=m_attn_sink, rel_bias=m_rel_bias, lru_out_norm=m_lru_out_norm, attn_out_norm=m_attn_out_norm, w_out=m_w_out, ffn2_norm=m_ffn2_norm, ffn2_w_gate=m_ffn2_w_gate, ffn2_w_up=m_ffn2_w_up, ffn2_w_down=m_ffn2_w_down, final_norm=m_final_norm)
    v_in = dict(ffn1_norm=v_ffn1_norm, ffn1_w_gate=v_ffn1_w_gate, ffn1_w_up=v_ffn1_w_up, ffn1_w_down=v_ffn1_w_down, mix_norm=v_mix_norm, w_in=v_w_in, conv_w=v_conv_w, conv_b=v_conv_b, lru_w_a=v_lru_w_a, lru_b_a=v_lru_b_a, lru_w_x=v_lru_w_x, lru_b_x=v_lru_b_x, lru_lambda=v_lru_lambda, attn_sink=v_attn_sink, rel_bias=v_rel_bias, lru_out_norm=v_lru_out_norm, attn_out_norm=v_attn_out_norm, w_out=v_w_out, ffn2_norm=v_ffn2_norm, ffn2_w_gate=v_ffn2_w_gate, ffn2_w_up=v_ffn2_w_up, ffn2_w_down=v_ffn2_w_down, final_norm=v_final_norm)
    order = list(weights)
    large = [(name, m, 0, lay.fh, m % 3 != 2) for m, name in
             enumerate(("ffn1_w_gate", "ffn1_w_up", "ffn1_w_down", "ffn2_w_gate", "ffn2_w_up", "ffn2_w_down"))]
    large += [("w_in", lay.MIX_BLK, 0, lay.ih, True), ("w_out", lay.MIX_BLK, lay.ih, lay.oh, False)]
    as_rows = {name: [rows_major(src[name]) if flip else src[name] for src in (weights, m_in, v_in)]
               for name, _, _, _, flip in large}
    stacked = {name: tuple(lax.empty(as_rows[name][0].shape, F32) for _ in range(4)) for name, *_ in large}

    def adamw_large(l, deps=()):
        for i, (name, blk, row_off, n_half, _) in enumerate(large):
            stacked[name] = adamw_layer(gf[l], blk, row_off, n_half, l, *as_rows[name], stacked[name],
                                        deps=deps if i == 0 else ())

    last = paired[0]
    crossing = reduce_start(last, paired[1], [dx])
    for l in sorted(finals):
        gf[l], = split_wait(f"final_wait_{l}", finals[l][0], finals[l][1], finals[l][2], [crossing[3]], final_plan)
        adamw_large(l, deps=(crossing[3],))
    ready = [buf for name, *_ in large for buf in stacked[name]] if depth > 1 else []
    half, small_sums[last] = reduce_finish(last, crossing, [dx] + ready)
    gf[last], = exchange_now(f"final_now_{last}", [half], 1, final_plan)
    adamw_large(last)

    per_layer = [_unpack_rows(small_sums[l], small_shapes[l]) for l in range(depth)]
    grads = {n: jnp.stack([per_layer[l][i] for l in range(depth)]) for i, n in enumerate(layer_names)}
    grads["rel_bias"], grads["final_norm"] = per_layer[0][len(layer_names):]
    for name in ("conv_w", "lru_b_a", "lru_b_x", "lru_lambda"):
        grads[name] = lax.dynamic_slice_in_dim(grads[name], k_chip * LANE, LANE, axis=2)
    delta, new_m, new_v = {}, {}, {}
    for name, _, _, _, flip in large:
        grads[name], delta[name], new_m[name], new_v[name] = [rows_major(a) if flip else a for a in stacked[name]]
    small = [n for n in order if n not in stacked]
    packed = [_pack_rows([src[n] for n in small], 1024) for src in (weights, grads, m_in, v_in)]
    outs = adamw(*packed)
    shapes = [weights[n].shape for n in small]
    for dst, buf in zip((delta, new_m, new_v), outs):
        dst.update(zip(small, _unpack_rows(buf, shapes)))

    return (loss, grad_x, *[grads[n] for n in order], *[delta[n] for n in order],
            *[new_m[n] for n in order], *[new_v[n] for n in order])
```

```python
import functools
import math

import jax
import jax.numpy as jnp
import numpy as np
from jax import lax
from jax.experimental import pallas as pl
from jax.experimental.pallas import tpu as pltpu

BF = jnp.bfloat16
F32 = jnp.float32
SDS = jax.ShapeDtypeStruct
MESH = pl.DeviceIdType.MESH
ANY = pl.BlockSpec(memory_space=pl.ANY)

N_CHIPS = 4
N_HEADS = 8
N_KV_HEADS = 2
KV_GROUP = N_HEADS // N_KV_HEADS
HEAD_DIM = 64
BLOCK = 128
WINDOW = 128
N_BUCKETS = 32
MAX_DISTANCE = 128
LRU_C = 8.0
CONV_WIDTH = 4
LANE = 128
SUBLANES = 8
SCAN_CHAINS = 8
EPS = 1e-6
FFN_RES = 0.5
NEG_INF = -1e30
ADAM_LR = 0.001
ADAM_B1 = 0.9
ADAM_B2 = 0.999
ADAM_EPS = 1e-08
ADAM_WD = 0.01
ADAM_STEP = 10
VMEM_LIMIT = 60000 * 1024
GELU_C = math.sqrt(2.0 / math.pi)


def dot_nn(a, b):
    return lax.dot_general(a, b, (((1,), (0,)), ((), ())), preferred_element_type=F32)


def dot_nt(a, b):
    return lax.dot_general(a, b, (((1,), (1,)), ((), ())), preferred_element_type=F32)


def dot_tn(a, b):
    return lax.dot_general(a, b, (((0,), (0,)), ((), ())), preferred_element_type=F32)


def _cparams(**kw):
    return pltpu.CompilerParams(vmem_limit_bytes=VMEM_LIMIT, **kw)


class Layout:
    MIX_BLK = 6
    BLOCKS = 7

    def __init__(self, d_model, d_ff, d_in):
        self.fh = d_ff // (2 * N_CHIPS)
        self.ih = d_in // (2 * N_CHIPS)
        self.oh = d_model // (2 * N_CHIPS)
        assert self.ih + self.oh == self.fh, "w_in^T and w_out rows must fill one ffn-sized block"
        self.rows = self.BLOCKS * self.fh


def _row_chunk(rows, target, step=16):
    best = rows
    for c in range(step, min(rows, target) + 1, step):
        if rows % c == 0:
            best = c
    return best


def _mesh_pos():
    return lax.axis_index("x"), lax.axis_index("y"), lax.axis_index("c")


def _rcopy(src, dst, ssem, rsem, dev):
    return pltpu.make_async_remote_copy(src_ref=src, dst_ref=dst, send_sem=ssem, recv_sem=rsem,
                                        device_id=dev, device_id_type=MESH)


HBM = pl.BlockSpec(memory_space=pltpu.HBM)
SEM = pl.BlockSpec(memory_space=pltpu.SEMAPHORE)
DATAFLOW = pltpu.SideEffectType.DATAFLOW_SIDE_EFFECTING


def _chip_peers():
    x, y, c = _mesh_pos()
    peers = [(1 - x, y), (x, 1 - y), (1 - x, 1 - y)]
    return x, y, c, 2 * x + y, [(px, py, 2 * px + py) for px, py in peers]


def split_start(name, bufs, n, plan):
    nb = len(bufs)

    def body(*refs):
        sends, _ = plan(refs[:nb], refs[nb], refs[nb + 1])
        for cp in sends:
            cp.start()
        refs[-1][...] = jnp.zeros_like(refs[-1])

    out = pl.pallas_call(
        body, name=name,
        out_shape=(pltpu.SemaphoreType.DMA((n,)), pltpu.SemaphoreType.DMA((n,)),
                   *[pltpu.HBM(b.shape, b.dtype) for b in bufs], SDS((8, LANE), F32)),
        in_specs=[HBM] * nb, out_specs=(SEM, SEM, *([HBM] * nb), pl.BlockSpec(memory_space=pltpu.VMEM)),
        input_output_aliases={i: 2 + i for i in range(nb)},
        compiler_params=pltpu.CompilerParams(has_side_effects=DATAFLOW),
    )(*[pltpu.with_memory_space_constraint(b, pltpu.HBM) for b in bufs])
    return out[0], out[1], list(out[2:2 + nb]), out[-1]


def split_wait(name, ssem, rsem, bufs, after, plan):
    nb = len(bufs)

    def body(*refs):
        sends, recvs = plan(refs[:nb], refs[nb], refs[nb + 1])
        for cp in recvs:
            cp.wait_recv()
        for cp in sends:
            cp.wait_send()

    out = pl.pallas_call(
        body, name=name, out_shape=tuple(pltpu.HBM(b.shape, b.dtype) for b in bufs),
        in_specs=[HBM] * nb + [SEM, SEM] + [ANY] * len(after), out_specs=tuple([HBM] * nb),
        input_output_aliases={i: i for i in range(nb)},
        compiler_params=pltpu.CompilerParams(has_side_effects=DATAFLOW),
    )(*bufs, ssem, rsem, *after)
    return list(out)


def gather_plan(refs, ssem, rsem, rows=None):
    land_ref, = refs
    _, _, c, k, peers = _chip_peers()
    part = (lambda a: a) if rows is None else (lambda a: a.at[pl.ds(rows[0], rows[1])])
    sends = [_rcopy(part(land_ref.at[k, c]), part(land_ref.at[k, c]), ssem.at[j], rsem.at[j], (px, py, c))
             for j, (px, py, _) in enumerate(peers)]
    recvs = [_rcopy(part(land_ref.at[kp, c]), part(land_ref.at[kp, c]), ssem.at[j], rsem.at[j], (px, py, c))
             for j, (px, py, kp) in enumerate(peers)]
    return sends, recvs


def reduce_plan(refs, ssem, rsem):
    cs_ref, ss_ref, p3_ref, sp3_ref = refs
    _, _, c, k, peers = _chip_peers()
    sends, recvs = [], []
    for j, (px, py, kp) in enumerate(peers):
        sends.append(_rcopy(cs_ref.at[kp], p3_ref.at[j], ssem.at[j], rsem.at[j], (px, py, c)))
        recvs.append(_rcopy(cs_ref.at[kp], p3_ref.at[j], ssem.at[j], rsem.at[j], (px, py, c)))
        sends.append(_rcopy(ss_ref, sp3_ref.at[k], ssem.at[3 + j], rsem.at[3 + j], (px, py, c)))
        recvs.append(_rcopy(ss_ref, sp3_ref.at[kp], ssem.at[3 + j], rsem.at[3 + j], (px, py, c)))
    return sends, recvs


def gather_small(sshard):
    def body(s_ref, sf_ref, lsem, ssem, rsem):
        _, _, c, k, peers = _chip_peers()
        own = pltpu.make_async_copy(s_ref, sf_ref.at[k], lsem)
        own.start()
        sends = [_rcopy(s_ref, sf_ref.at[k], ssem.at[j], rsem.at[j], (px, py, c)) for j, (px, py, _) in enumerate(peers)]
        recvs = [_rcopy(s_ref, sf_ref.at[kp], ssem.at[j], rsem.at[j], (px, py, c)) for j, (px, py, kp) in enumerate(peers)]
        for cp in sends:
            cp.start()
        for cp in recvs:
            cp.wait_recv()
        for cp in sends:
            cp.wait_send()
        own.wait()

    return pl.pallas_call(
        body, name="gather_small", out_shape=SDS((N_CHIPS,) + sshard.shape, sshard.dtype),
        in_specs=[ANY], out_specs=ANY,
        scratch_shapes=[pltpu.SemaphoreType.DMA, pltpu.SemaphoreType.DMA((3,)), pltpu.SemaphoreType.DMA((3,))],
    )(sshard)


def exchange_now(name, bufs, n, plan):
    nb = len(bufs)

    def body(*refs):
        sends, recvs = plan(refs[nb:2 * nb], refs[2 * nb], refs[2 * nb + 1])
        for cp in sends:
            cp.start()
        for cp in recvs:
            cp.wait_recv()
        for cp in sends:
            cp.wait_send()

    return list(pl.pallas_call(
        body, name=name, out_shape=tuple(SDS(b.shape, b.dtype) for b in bufs),
        in_specs=[ANY] * nb, out_specs=tuple([ANY] * nb), input_output_aliases={i: i for i in range(nb)},
        scratch_shapes=[pltpu.SemaphoreType.DMA((n,)), pltpu.SemaphoreType.DMA((n,))],
    )(*bufs))


def handover_plan(refs, ssem, rsem, rows=None):
    land_ref, = refs
    x, y, c, _, peers = _chip_peers()
    sib = (x, y, 1 - c)
    part = (lambda a: a) if rows is None else (lambda a: a.at[pl.ds(rows[0], rows[1])])
    sends = [_rcopy(part(land_ref.at[kp, c]), part(land_ref.at[kp, c]), ssem.at[j], rsem.at[j], sib)
             for j, (_, _, kp) in enumerate(peers)]
    recvs = [_rcopy(part(land_ref.at[kp, 1 - c]), part(land_ref.at[kp, 1 - c]), ssem.at[j], rsem.at[j], sib)
             for j, (_, _, kp) in enumerate(peers)]
    return sends, recvs


def pair_plan(refs, ssem, rsem):
    gb_ref, sb_ref, p_ref, sp_ref = refs
    x, y, c = _mesh_pos()
    sib = (x, y, 1 - c)
    n = gb_ref.shape[0]
    copies = [_rcopy(gb_ref.at[kk, 1 - c], p_ref.at[kk], ssem.at[kk], rsem.at[kk], sib) for kk in range(n)]
    copies.append(_rcopy(sb_ref, sp_ref, ssem.at[n], rsem.at[n], sib))
    return copies, copies


def final_plan(refs, ssem, rsem):
    gf_ref, = refs
    x, y, c = _mesh_pos()
    sib = (x, y, 1 - c)
    return ([_rcopy(gf_ref.at[c], gf_ref.at[c], ssem.at[0], rsem.at[0], sib)],
            [_rcopy(gf_ref.at[1 - c], gf_ref.at[1 - c], ssem.at[0], rsem.at[0], sib)])


def pair_sum(pos, gb, p1):
    n, _, rh, d = gb.shape
    cr = _row_chunk(rh, 1280)

    def body(pos_ref, a_ref, b_ref, o_ref):
        o_ref[...] = (a_ref[...].astype(F32) + b_ref[...].astype(F32)).astype(o_ref.dtype)

    return pl.pallas_call(
        body, name="pair_sum", out_shape=SDS((n, rh, d), gb.dtype),
        grid_spec=pltpu.PrefetchScalarGridSpec(
            num_scalar_prefetch=1, grid=(n, rh // cr),
            in_specs=[pl.BlockSpec((None, None, cr, d), lambda kk, r, pos: (kk, pos[1], r, 0)),
                      pl.BlockSpec((None, cr, d), lambda kk, r, pos: (kk, r, 0))],
            out_specs=pl.BlockSpec((None, cr, d), lambda kk, r, pos: (kk, r, 0))),
        compiler_params=_cparams(),
    )(pos, gb, p1)


def chip_sum(pos, cs, p3):
    n, rh, d = cs.shape
    cr = _row_chunk(rh, 640)

    def body(pos_ref, a_ref, b_ref, o_ref):
        acc = a_ref[...].astype(F32)
        for j in range(3):
            acc = acc + b_ref[j].astype(F32)
        o_ref[...] = acc

    return pl.pallas_call(
        body, name="chip_sum", out_shape=SDS((2, rh, d), F32),
        grid_spec=pltpu.PrefetchScalarGridSpec(
            num_scalar_prefetch=1, grid=(rh // cr,),
            in_specs=[pl.BlockSpec((None, cr, d), lambda r, pos: (pos[0], r, 0)),
                      pl.BlockSpec((3, cr, d), lambda r, pos: (0, r, 0))],
            out_specs=pl.BlockSpec((None, cr, d), lambda r, pos: (pos[1], r, 0))),
        compiler_params=_cparams(),
    )(pos, cs, p3)


def small_pair_sum(a, b):
    def body(a_ref, b_ref, o_ref):
        o_ref[...] = a_ref[...] + b_ref[...]

    return pl.pallas_call(body, name="small_pair_sum", out_shape=SDS(a.shape, a.dtype),
                          compiler_params=_cparams())(a, b)


def small_chip_sum(pos, own, p):
    ns, w = own.shape

    def body(pos_ref, own_ref, p0, p1, p2, p3, o_ref):
        k = pos_ref[0]
        acc = None
        for chip, ref in enumerate((p0, p1, p2, p3)):
            term = jnp.where(k == chip, own_ref[...], ref[...])
            acc = term if acc is None else acc + term
        o_ref[...] = acc

    def slot(chip):
        return pl.BlockSpec((None, ns, w), lambda i, pos: (jnp.where(pos[0] == chip, (chip + 1) % N_CHIPS, chip), 0, 0))

    return pl.pallas_call(
        body, name="small_chip_sum", out_shape=SDS(own.shape, own.dtype),
        grid_spec=pltpu.PrefetchScalarGridSpec(
            num_scalar_prefetch=1, grid=(1,),
            in_specs=[pl.BlockSpec((ns, w), lambda i, pos: (0, 0))] + [slot(chip) for chip in range(N_CHIPS)],
            out_specs=pl.BlockSpec((ns, w), lambda i, pos: (0, 0))),
        compiler_params=_cparams(),
    )(pos, own, p, p, p, p)


def _rms(x, g):
    rs = lax.rsqrt(jnp.mean(x * x, axis=-1, keepdims=True) + EPS)
    xh = x * rs
    return xh, rs, xh * g


def _rms_bwd(dy, xh, rs, g):
    dxh = dy * g
    dx = rs * (dxh - xh * jnp.mean(dxh * xh, axis=-1, keepdims=True))
    return dx, dy * xh


def _gelu(x):
    t = jnp.tanh(GELU_C * (x + 0.044715 * x * x * x))
    return 0.5 * x * (1.0 + t), t


def _gelu_grad(x, t):
    return 0.5 * (1.0 + t) + 0.5 * x * (1.0 - t * t) * GELU_C * (1.0 + 3.0 * 0.044715 * x * x)


def _shift_rows(v, s, n):
    if s == 0:
        return v
    t = lax.broadcasted_iota(jnp.int32, v.shape, 0)
    rolled = pltpu.roll(v, (-s) % n, 0)
    inside = (t < n - s) if s > 0 else (t >= -s)
    return jnp.where(inside, rolled, 0.0)


def _scan_rows(a_ref, u_ref, h_ref, acum_ref, reverse):
    s_len, w = a_ref.shape
    chunk = min(512, s_len)
    last = 0 if reverse else SUBLANES - 1

    def inside_vregs(ci, _):
        rows = pl.ds(pl.multiple_of(ci * chunk, chunk), chunk)
        a = a_ref[rows, :].reshape(chunk // SUBLANES, SUBLANES, w)
        u = u_ref[rows, :].reshape(chunk // SUBLANES, SUBLANES, w)
        pos = lax.broadcasted_iota(jnp.int32, (1, SUBLANES, w), 1)
        for dist in (1, 2, 4):
            ok = (pos < SUBLANES - dist) if reverse else (pos >= dist)
            shift = SUBLANES - dist if reverse else dist
            u = u + a * jnp.where(ok, pltpu.roll(u, shift, 1), 0.0)
            a = a * jnp.where(ok, pltpu.roll(a, shift, 1), 1.0)
        h_ref[rows, :] = u.reshape(chunk, w)
        acum_ref[rows, :] = a.reshape(chunk, w)
        return 0

    lax.fori_loop(0, s_len // chunk, inside_vregs, 0)

    chains = max(1, min(SCAN_CHAINS, s_len // (8 * SUBLANES)))
    seg = s_len // chains
    nvreg = seg // SUBLANES

    def step(j, carry):
        jj = (nvreg - 1 - j) if reverse else j
        out = []
        for c, (hin, ain) in enumerate(carry):
            rows = pl.ds(pl.multiple_of(c * seg + jj * SUBLANES, SUBLANES), SUBLANES)
            acc = acum_ref[rows, :]
            h = h_ref[rows, :] + acc * hin
            acc = acc * ain
            h_ref[rows, :] = h
            acum_ref[rows, :] = acc
            out.append((jnp.broadcast_to(h[last:last + 1, :], h.shape), jnp.broadcast_to(acc[last:last + 1, :], acc.shape)))
        return tuple(out)

    init = tuple((jnp.zeros((SUBLANES, w), F32), jnp.ones((SUBLANES, w), F32)) for _ in range(chains))
    ends = lax.fori_loop(0, nvreg, step, init, unroll=min(2, nvreg))
    order = range(chains - 2, -1, -1) if reverse else range(1, chains)
    inflow = jnp.zeros((1, w), F32)
    for s in order:
        h, acc = ends[s + 1 if reverse else s - 1]
        inflow = h[0:1, :] + acc[0:1, :] * inflow
        rows = pl.ds(s * seg, seg)
        h_ref[rows, :] = h_ref[rows, :] + acum_ref[rows, :] * inflow


def _w_spec(rows_half, d, blk):
    return pl.BlockSpec((N_CHIPS, 2, rows_half, d), lambda *_: (0, 0, blk, 0), pipeline_mode=pl.Buffered(1))


def ffn_forward(x, gain, wfull, lay, which, deps=(), tm=512):
    s_len, d = x.shape
    tm = min(tm, s_len)
    f = 8 * lay.fh
    fc = f // 2

    def body(x_ref, g_ref, wg_ref, wu_ref, wd_ref, *rest):
        o_ref, gate_ref, up_ref = rest[len(deps):]
        x = x_ref[...]
        _, _, hn = _rms(x, g_ref[...])
        h = hn.astype(BF)
        y = jnp.zeros((tm, d), F32)
        for part in range(2):
            cols = slice(part * fc, (part + 1) * fc)
            gate = dot_nt(h, wg_ref[...].reshape(f, d)[cols])
            up = dot_nt(h, wu_ref[...].reshape(f, d)[cols])
            act = (gate * jax.nn.sigmoid(gate) * up).astype(BF)
            y = y + dot_nn(act, wd_ref[...].reshape(f, d)[cols])
            gate_ref[:, cols] = gate.astype(BF)
            up_ref[:, cols] = up.astype(BF)
        o_ref[...] = x + FFN_RES * y

    row = pl.BlockSpec((tm, d), lambda i: (i, 0))
    wide = pl.BlockSpec((tm, f), lambda i: (i, 0))
    return pl.pallas_call(
        body, name="ffn_forward", grid=(s_len // tm,),
        out_shape=(SDS((s_len, d), F32), SDS((s_len, f), BF), SDS((s_len, f), BF)),
        in_specs=[row, pl.BlockSpec((1, d), lambda i: (0, 0))]
        + [_w_spec(lay.fh, d, 3 * which + m) for m in range(3)] + [ANY] * len(deps),
        out_specs=(row, wide, wide), compiler_params=_cparams(),
    )(x, gain, wfull, wfull, wfull, *deps)


def ffn_backward_dx(x, gain, dout, gate_bf, up_bf, wfull, lay, which, deps=(), tm=256):
    s_len, d = x.shape
    tm = min(tm, s_len)
    f = 8 * lay.fh
    fc = f // 2
    nt = s_len // tm

    def body(x_ref, g_ref, do_ref, gate_ref, up_ref, wg_ref, wu_ref, wd_ref, *rest):
        dx_ref, dg_ref, lhs_ref, rhs_ref = rest[len(deps):]
        dgate_ref, dup_ref, act_ref = lhs_ref.at[0], lhs_ref.at[1], lhs_ref.at[2]
        h_ref, df_ref = rhs_ref.at[0], rhs_ref.at[1]
        x = x_ref[...]
        g = g_ref[...]
        xh, rs, hn = _rms(x, g)
        h = hn.astype(BF)
        do = do_ref[...]
        df = (FFN_RES * do).astype(BF)
        dh = jnp.zeros((tm, d), F32)
        for part in range(2):
            cols = slice(part * fc, (part + 1) * fc)
            wg = wg_ref[...].reshape(f, d)[cols]
            wu = wu_ref[...].reshape(f, d)[cols]
            gate = gate_ref[:, cols].astype(F32)
            up = up_ref[:, cols].astype(F32)
            sg = jax.nn.sigmoid(gate)
            silu = gate * sg
            dact = dot_nt(df, wd_ref[...].reshape(f, d)[cols])
            dup = (dact * silu).astype(BF)
            dgate = (dact * up * (sg * (1.0 + gate * (1.0 - sg)))).astype(BF)
            dh = dh + dot_nn(dgate, wg) + dot_nn(dup, wu)
            dgate_ref[:, cols] = dgate
            dup_ref[:, cols] = dup
            act_ref[:, cols] = (silu * up).astype(BF)
        dxn, dgrow = _rms_bwd(dh, xh, rs, g)
        dx_ref[...] = do + dxn

        @pl.when(pl.program_id(0) == 0)
        def _():
            dg_ref[...] = jnp.zeros_like(dg_ref)

        dg_ref[...] += jnp.sum(dgrow, axis=0, keepdims=True)
        h_ref[...] = h
        df_ref[...] = df

    row = pl.BlockSpec((tm, d), lambda i: (i, 0))
    wide = pl.BlockSpec((tm, f), lambda i: (i, 0))
    vec = pl.BlockSpec((1, d), lambda i: (0, 0))
    return pl.pallas_call(
        body, name="ffn_backward_dx", grid=(nt,),
        out_shape=(SDS((s_len, d), F32), SDS((1, d), F32), SDS((3, s_len, f), BF), SDS((2, s_len, d), BF)),
        in_specs=[row, vec, row, wide, wide] + [_w_spec(lay.fh, d, 3 * which + m) for m in range(3)] + [ANY] * len(deps),
        out_specs=(row, vec, pl.BlockSpec((3, tm, f), lambda i: (0, i, 0)), pl.BlockSpec((2, tm, d), lambda i: (0, i, 0))),
        compiler_params=_cparams(),
    )(x, gain, dout, gate_bf, up_bf, wfull, wfull, wfull, *deps)


def weight_grad_tn(lhs, rhs, gb, lay, blk0, tk=2048):
    nmat, s_len, f = lhs.shape
    tk = min(tk, s_len)
    d = rhs.shape[2]
    fc = f // 2
    nk = s_len // tk

    def body(a_ref, b_ref, gb_ref, o_ref, acc):
        kt = pl.program_id(2)

        @pl.when(kt == 0)
        def _():
            acc[...] = jnp.zeros_like(acc)

        acc[...] += dot_tn(a_ref[...], b_ref[...])

        @pl.when(kt == nk - 1)
        def _():
            for p in range(2):
                for q in range(2):
                    o_ref[p, q] = acc[pl.ds((2 * p + q) * lay.fh, lay.fh), :].astype(o_ref.dtype)

    return pl.pallas_call(
        body, name="weight_grad_tn", grid=(nmat, 2, nk), out_shape=SDS(gb.shape, gb.dtype),
        in_specs=[pl.BlockSpec((None, tk, fc), lambda m, j, kt: (m, kt, j)),
                  pl.BlockSpec((None, tk, d), lambda m, j, kt: (jnp.where(m == nmat - 1, 1, 0), kt, 0)), ANY],
        out_specs=pl.BlockSpec((2, 2, lay.fh, d), lambda m, j, kt: (j, 0, blk0 + m, 0)),
        scratch_shapes=[pltpu.VMEM((fc, d), F32)],
        input_output_aliases={2: 0}, compiler_params=_cparams(),
    )(lhs, rhs, gb)


def _lane_blocks(v):
    return [v[:, j * LANE:(j + 1) * LANE] for j in range(v.shape[1] // LANE)]


def _join_lane_blocks(ref):
    return jnp.concatenate([ref[j] for j in range(ref.shape[0])], axis=1)


def _cbm_spec(nblk, rows, first=0):
    return pl.BlockSpec((nblk, rows, LANE), lambda i: (first // nblk, i, 0))


def mix_project(x, gain, wfull, lay, lw, att, tm=512):
    s_len, d = x.shape
    tm = min(tm, s_len)
    d_in = 8 * lay.ih
    kvw = (d_in - 2 * lw - att) // 2
    ncol = (2 * lw + 2 * kvw) // LANE

    def body(x_ref, g_ref, w_ref, o_ref, qt_ref, vt_ref):
        _, _, hn = _rms(x_ref[...], g_ref[...])
        h = hn.astype(BF)
        w = w_ref[:, :, :lay.ih, :].reshape(d_in, d)
        pieces = _lane_blocks(dot_nt(h, w[:2 * lw])) + _lane_blocks(dot_nt(h, w[2 * lw + att:]))
        for j, piece in enumerate(pieces):
            o_ref[j] = piece
        qt_ref[...] = dot_nt(w[2 * lw:2 * lw + att], h)
        vt_ref[...] = dot_nt(w[2 * lw + att + kvw:], h)

    return pl.pallas_call(
        body, name="mix_project", grid=(s_len // tm,),
        out_shape=(SDS((ncol, s_len, LANE), F32), SDS((att, s_len), F32), SDS((kvw, s_len), F32)),
        in_specs=[pl.BlockSpec((tm, d), lambda i: (i, 0)), pl.BlockSpec((1, d), lambda i: (0, 0)),
                  _w_spec(lay.fh, d, lay.MIX_BLK)],
        out_specs=(_cbm_spec(ncol, tm), pl.BlockSpec((att, tm), lambda i: (0, i)), pl.BlockSpec((kvw, tm), lambda i: (0, i))),
        compiler_params=_cparams(),
    )(x, gain, wfull)


def mix_project_backward(x, gain, dout, dxr, dgt, dqt, dkv, dwout, wfull, gb, lay, tm=512):
    s_len, d = x.shape
    tm = min(tm, s_len)
    d_in = 8 * lay.ih
    nt = s_len // tm
    kvw = dkv.shape[1]
    att = dqt.shape[0]
    nlru = (dxr.shape[0] + dgt.shape[0]) * LANE

    def body(x_ref, g_ref, do_ref, dxr_ref, dgt_ref, dqt_ref, dkv_ref, dwo_ref, w_ref, gb_ref, dx_ref, dg_ref, o_ref, acc):
        i = pl.program_id(0)
        g = g_ref[...]
        xh, rs, hn = _rms(x_ref[...], g)
        h = hn.astype(BF)
        w = w_ref[:, :, :lay.ih, :].reshape(d_in, d)
        dlru = jnp.concatenate([_join_lane_blocks(dxr_ref), _join_lane_blocks(dgt_ref)], axis=1).astype(BF)
        dqt = dqt_ref[...].astype(BF)
        dkv = dkv_ref[...].astype(BF)
        dh = dot_nn(dlru, w[:nlru]) + dot_tn(dqt, w[nlru:nlru + att]) + dot_nn(dkv, w[nlru + att:])
        dxn, dgrow = _rms_bwd(dh, xh, rs, g)
        dx_ref[...] = do_ref[...] + dxn

        @pl.when(i == 0)
        def _():
            dg_ref[...] = jnp.zeros_like(dg_ref)
            acc[...] = jnp.zeros_like(acc)

        dg_ref[...] += jnp.sum(dgrow, axis=0, keepdims=True)
        acc[0:nlru, :] += dot_tn(dlru, h)
        acc[nlru:nlru + att, :] += dot_nn(dqt, h)
        acc[nlru + att:, :] += dot_tn(dkv, h)

        @pl.when(i == nt - 1)
        def _():
            for p in range(N_CHIPS):
                for q in range(2):
                    o_ref[p, q, :lay.ih, :] = acc[pl.ds((2 * p + q) * lay.ih, lay.ih), :].astype(o_ref.dtype)
            o_ref[:, :, lay.ih:, :] = dwo_ref[...]

    row = pl.BlockSpec((tm, d), lambda i: (i, 0))
    vec = pl.BlockSpec((1, d), lambda i: (0, 0))
    return pl.pallas_call(
        body, name="mix_project_backward", grid=(nt,),
        out_shape=(SDS((s_len, d), F32), SDS((1, d), F32), SDS(gb.shape, gb.dtype)),
        in_specs=[row, vec, row, _cbm_spec(dxr.shape[0], tm), _cbm_spec(dgt.shape[0], tm),
                  pl.BlockSpec((att, tm), lambda i: (0, i)), pl.BlockSpec((tm, kvw), lambda i: (i, 0)),
                  pl.BlockSpec(dwout.shape, lambda i: (0, 0, 0, 0)), _w_spec(lay.fh, d, lay.MIX_BLK), ANY],
        out_specs=(row, vec, pl.BlockSpec((N_CHIPS, 2, lay.fh, d), lambda i: (0, 0, lay.MIX_BLK, 0))),
        scratch_shapes=[pltpu.VMEM((d_in, d), F32)],
        input_output_aliases={9: 2}, compiler_params=_cparams(),
    )(x, gain, dout, dxr, dgt, dqt, dkv, dwout, wfull, gb)


def _lru_gates(xc, wb_ref, pv_ref, direction):
    xcb = xc.astype(BF)
    r = jax.nn.sigmoid(dot_nn(xcb, wb_ref[2 * direction]) + pv_ref[1 + direction:2 + direction, :])
    i = jax.nn.sigmoid(dot_nn(xcb, wb_ref[2 * direction + 1]) + pv_ref[3 + direction:4 + direction, :])
    lam = pv_ref[5 + direction:6 + direction, :]
    sp = jnp.maximum(-lam, 0.0) + jnp.log(1.0 + jnp.exp(-jnp.abs(lam)))
    a = jnp.exp(-LRU_C * sp * r)
    mult = jnp.sqrt(1.0 - a * a)
    return xcb, r, i, a, mult, sp


def _conv_rows(xr, cv_ref, bias, n):
    acc = bias + cv_ref[0:1, :] * _shift_rows(xr, -2, n)
    for j in range(1, CONV_WIDTH):
        acc = acc + cv_ref[j:j + 1, :] * _shift_rows(xr, j - 2, n)
    return acc


def lru_forward(proj, cvec, pvec, wblk, lw, deps=(), ch=512):
    s_len = proj.shape[1]
    ncb = lw // LANE
    ch = min(ch, s_len)
    nchunk = s_len // ch

    def body(xr_ref, gt_ref, cv_ref, pv_ref, wb_ref, *rest):
        y_ref, hs_ref, xc_s, a_s, u_s, acum_s = rest[len(deps):]
        xc_s[...] = _conv_rows(xr_ref[...], cv_ref, pv_ref[0:1, :], s_len)
        for direction in range(2):
            def fill(ci, _):
                rows = pl.ds(pl.multiple_of(ci * ch, ch), ch)
                xc = xc_s[rows, :]
                _, _, i, a, mult, _ = _lru_gates(xc, wb_ref, pv_ref, direction)
                a_s[rows, :] = a
                u_s[rows, :] = mult * (i * xc)
                return 0

            lax.fori_loop(0, nchunk, fill, 0)
            _scan_rows(a_s, u_s, hs_ref.at[direction], acum_s, reverse=direction == 1)

        def out(ci, _):
            rows = pl.ds(pl.multiple_of(ci * ch, ch), ch)
            gl, _ = _gelu(gt_ref[rows, :])
            y_ref[rows, :] = gl * (hs_ref[0, rows, :] + hs_ref[1, rows, :])
            return 0

        lax.fori_loop(0, nchunk, out, 0)

    col = lambda off: pl.BlockSpec((None, s_len, LANE), lambda cb: (off + cb, 0, 0))
    return pl.pallas_call(
        body, name="lru_forward", grid=(ncb,),
        out_shape=(SDS((ncb, s_len, LANE), F32), SDS((2, ncb, s_len, LANE), F32)),
        in_specs=[col(0), col(ncb), pl.BlockSpec((8, LANE), lambda cb: (0, cb)), pl.BlockSpec((8, LANE), lambda cb: (0, cb)),
                  pl.BlockSpec((4, None, LANE, LANE), lambda cb: (0, cb, 0, 0))] + [ANY] * len(deps),
        out_specs=(col(0), pl.BlockSpec((2, None, s_len, LANE), lambda cb: (0, cb, 0, 0))),
        scratch_shapes=[pltpu.VMEM((s_len, LANE), F32)] * 4, compiler_params=_cparams(),
    )(proj, proj, cvec, pvec, wblk, *deps)


def lru_backward(proj, hs, dy, cvec, pvec, wblk, lw, ch=512):
    s_len = proj.shape[1]
    ncb = lw // LANE
    ch = min(ch, s_len)
    nchunk = s_len // ch

    def body(xr_ref, gt_ref, hs_ref, dy_ref, cv_ref, pv_ref, wb_ref, dxr_ref, dgt_ref, dcv_ref, dpv_ref, dwb_ref,
             xc_s, a_s, r_s, i_s, dh_s, lam_s, hp_s, dxc_s, acum_s):
        xr = xr_ref[...]
        xc_s[...] = _conv_rows(xr, cv_ref, pv_ref[0:1, :], s_len)
        dxc_s[...] = jnp.zeros_like(dxc_s)
        dpv_ref[...] = jnp.zeros_like(dpv_ref)
        dwb_ref[...] = jnp.zeros_like(dwb_ref)

        def head(ci, _):
            rows = pl.ds(pl.multiple_of(ci * ch, ch), ch)
            gt = gt_ref[rows, :]
            gl, t = _gelu(gt)
            dy = dy_ref[rows, :]
            dh_s[rows, :] = dy * gl
            dgt_ref[rows, :] = dy * (hs_ref[0, rows, :] + hs_ref[1, rows, :]) * _gelu_grad(gt, t)
            return 0

        lax.fori_loop(0, nchunk, head, 0)

        for direction in range(2):
            def fill(ci, _):
                rows = pl.ds(pl.multiple_of(ci * ch, ch), ch)
                _, r, i, a, _, _ = _lru_gates(xc_s[rows, :], wb_ref, pv_ref, direction)
                a_s[rows, :] = a
                r_s[rows, :] = r
                i_s[rows, :] = i
                return 0

            lax.fori_loop(0, nchunk, fill, 0)
            toward = 1 if direction == 0 else -1
            hp_s[...] = _shift_rows(a_s[...], toward, s_len)
            _scan_rows(hp_s, dh_s, lam_s, acum_s, reverse=direction == 0)
            hp_s[...] = _shift_rows(hs_ref[direction], -toward, s_len)

            def grads(ci, _):
                rows = pl.ds(pl.multiple_of(ci * ch, ch), ch)
                xc = xc_s[rows, :]
                xcb = xc.astype(BF)
                r, i, a = r_s[rows, :], i_s[rows, :], a_s[rows, :]
                mult = jnp.sqrt(1.0 - a * a)
                lam = pv_ref[5 + direction:6 + direction, :]
                sp = jnp.maximum(-lam, 0.0) + jnp.log(1.0 + jnp.exp(-jnp.abs(lam)))
                du = lam_s[rows, :]
                da = du * hp_s[rows, :]
                dmult = du * i * xc
                di = du * mult * xc
                dlog_a = (da - dmult * a / mult) * a
                dr = dlog_a * (-LRU_C * sp)
                dza = dr * r * (1.0 - r)
                dzx = di * i * (1.0 - i)
                dzab = dza.astype(BF)
                dzxb = dzx.astype(BF)
                dxc_s[rows, :] += (du * mult * i + dot_nt(dzab, wb_ref[2 * direction])
                                   + dot_nt(dzxb, wb_ref[2 * direction + 1]))
                dwb_ref[2 * direction] += dot_tn(xcb, dzab)
                dwb_ref[2 * direction + 1] += dot_tn(xcb, dzxb)
                dpv_ref[1 + direction:2 + direction, :] += jnp.sum(dza, axis=0, keepdims=True)
                dpv_ref[3 + direction:4 + direction, :] += jnp.sum(dzx, axis=0, keepdims=True)
                dpv_ref[5 + direction:6 + direction, :] += jnp.sum(dlog_a * (-LRU_C * r), axis=0, keepdims=True)
                return 0

            lax.fori_loop(0, nchunk, grads, 0)

        for direction in range(2):
            lam = pv_ref[5 + direction:6 + direction, :]
            dpv_ref[5 + direction:6 + direction, :] = dpv_ref[5 + direction:6 + direction, :] * (-jax.nn.sigmoid(-lam))
        dxc = dxc_s[...]
        dpv_ref[0:1, :] = jnp.sum(dxc, axis=0, keepdims=True)
        dxr = cv_ref[0:1, :] * _shift_rows(dxc, 2, s_len)
        for j in range(1, CONV_WIDTH):
            dxr = dxr + cv_ref[j:j + 1, :] * _shift_rows(dxc, 2 - j, s_len)
        dxr_ref[...] = dxr
        dcv_ref[...] = jnp.zeros_like(dcv_ref)
        for j in range(CONV_WIDTH):
            dcv_ref[j:j + 1, :] = jnp.sum(dxc * _shift_rows(xr, j - 2, s_len), axis=0, keepdims=True)

    col = lambda off: pl.BlockSpec((None, s_len, LANE), lambda cb: (off + cb, 0, 0))
    own = col(0)
    small = pl.BlockSpec((8, LANE), lambda cb: (0, cb))
    wspec = pl.BlockSpec((4, None, LANE, LANE), lambda cb: (0, cb, 0, 0))
    return pl.pallas_call(
        body, name="lru_backward", grid=(ncb,),
        out_shape=(SDS((ncb, s_len, LANE), F32), SDS((ncb, s_len, LANE), F32), SDS((8, lw), F32), SDS((8, lw), F32),
                   SDS(wblk.shape, F32)),
        in_specs=[col(0), col(ncb), pl.BlockSpec((2, None, s_len, LANE), lambda cb: (0, cb, 0, 0)), own, small, small, wspec],
        out_specs=(own, own, small, small, wspec),
        scratch_shapes=[pltpu.VMEM((s_len, LANE), F32)] * 9, compiler_params=_cparams(),
    )(proj, proj, hs, dy, cvec, pvec, wblk)


def _window_specs(s_len, first, width=None):
    nb = s_len // BLOCK
    where = (lambda n: jnp.maximum(n - 1, 0), lambda n: n, lambda n: jnp.minimum(n + 1, nb - 1))
    if width is None:
        return [pl.BlockSpec((None, BLOCK, LANE), lambda n, f=f: (first, f(n), 0)) for f in where]
    return [pl.BlockSpec((width, BLOCK), lambda n, f=f: (0, f(n))) for f in where]


def _stack_heads(v, kh):
    return jnp.concatenate([v[(kh * KV_GROUP + g) * HEAD_DIM:(kh * KV_GROUP + g + 1) * HEAD_DIM, :]
                            for g in range(KV_GROUP)], axis=1)


def _unstack_heads(ref, kh, v):
    for g in range(KV_GROUP):
        h = kh * KV_GROUP + g
        ref[h * HEAD_DIM:(h + 1) * HEAD_DIM, :] = v[:, g * BLOCK:(g + 1) * BLOCK]


def _key_exists(n, nb):
    j = lax.broadcasted_iota(jnp.int32, (3 * BLOCK, 1), 0)
    return ((n > 0) | (j >= BLOCK)) & ((n < nb - 1) | (j < 2 * BLOCK))


def _attn_probs(qs, kcat, bias_g, sink_g, key_ok):
    logits = jnp.where(key_ok, dot_nn(kcat, qs) + bias_g, NEG_INF)
    m = jnp.maximum(jnp.max(logits, axis=0, keepdims=True), sink_g)
    p = jnp.exp(logits - m)
    es = jnp.exp(sink_g - m)
    inv = 1.0 / (jnp.sum(p, axis=0, keepdims=True) + es)
    return p * inv, es * inv


def attention_forward(qt, proj, vt, bias, sink, kblk):
    att, s_len = qt.shape
    kvw = vt.shape[0]
    nb = s_len // BLOCK

    def body(q_ref, kp_ref, kc_ref, kn_ref, vp_ref, vc_ref, vn_ref, b_ref, s_ref, o_ref):
        n = pl.program_id(0)
        q = q_ref[...]
        key_ok = _key_exists(n, nb)
        kall = jnp.concatenate([kp_ref[...], kc_ref[...], kn_ref[...]], axis=0).astype(BF)
        vall = jnp.concatenate([vp_ref[...], vc_ref[...], vn_ref[...]], axis=1).astype(BF)
        for kh in range(N_KV_HEADS):
            qs = (_stack_heads(q, kh) * (HEAD_DIM ** -0.5)).astype(BF)
            p, _ = _attn_probs(qs, kall[:, kh * HEAD_DIM:(kh + 1) * HEAD_DIM], b_ref[kh], s_ref[kh, 0:1, :], key_ok)
            _unstack_heads(o_ref, kh, dot_nn(vall[kh * HEAD_DIM:(kh + 1) * HEAD_DIM, :], p.astype(BF)))

    blk = pl.BlockSpec((att, BLOCK), lambda n: (0, n))
    return pl.pallas_call(
        body, name="attention_forward", grid=(nb,), out_shape=SDS((att, s_len), F32),
        in_specs=[blk] + _window_specs(s_len, kblk) + _window_specs(s_len, 0, kvw)
        + [pl.BlockSpec(bias.shape, lambda n: (0, 0, 0)), pl.BlockSpec(sink.shape, lambda n: (0, 0, 0))],
        out_specs=blk, compiler_params=_cparams(),
    )(qt, proj, proj, proj, vt, vt, vt, bias, sink)


def attention_backward(qt, proj, y_att, dy, bias, sink, kblk):
    att, s_len = qt.shape
    nb = s_len // BLOCK
    kvw = N_KV_HEADS * HEAD_DIM

    def body(q_ref, kp_ref, kc_ref, kn_ref, vp_ref, vc_ref, vn_ref, o_ref, do_ref, b_ref, s_ref,
             dq_ref, dkv_ref, db_ref, ds_ref):
        n = pl.program_id(0)

        @pl.when(n == 0)
        def _():
            dkv_ref[...] = jnp.zeros_like(dkv_ref)
            db_ref[...] = jnp.zeros_like(db_ref)
            ds_ref[...] = jnp.zeros_like(ds_ref)

        q = q_ref[...]
        o = o_ref[...]
        do = do_ref[...]
        kall = jnp.concatenate([kp_ref[...], kc_ref[...], kn_ref[...]], axis=0).astype(BF)
        vall = jnp.concatenate([vp_ref[...], vc_ref[...], vn_ref[...]], axis=0).astype(BF)
        key_ok = _key_exists(n, nb)
        dks, dvs = [], []
        for kh in range(N_KV_HEADS):
            kcat = kall[:, kh * HEAD_DIM:(kh + 1) * HEAD_DIM]
            vcat = vall[:, kh * HEAD_DIM:(kh + 1) * HEAD_DIM]
            qs = (_stack_heads(q, kh) * (HEAD_DIM ** -0.5)).astype(BF)
            p, ps = _attn_probs(qs, kcat, b_ref[kh], s_ref[kh, 0:1, :], key_ok)
            dos = _stack_heads(do, kh)
            dosb = dos.astype(BF)
            delta = jnp.sum(dos * _stack_heads(o, kh), axis=0, keepdims=True)
            dlog = p * (dot_nn(vcat, dosb) - delta)
            dlogb = dlog.astype(BF)
            db_ref[kh] += dlog
            ds_ref[kh] += jnp.broadcast_to(-ps * delta, ds_ref.shape[1:])
            _unstack_heads(dq_ref, kh, dot_tn(kcat, dlogb) * (HEAD_DIM ** -0.5))
            dks.append(dot_nt(dlogb, qs))
            dvs.append(dot_nt(p.astype(BF), dosb))
        dkv = jnp.concatenate(dks + dvs, axis=1)
        starts = [jnp.maximum(n - 1, 0), n, jnp.minimum(n + 1, nb - 1)]
        for b, st in enumerate(starts):
            rows = pl.ds(pl.multiple_of(st * BLOCK, BLOCK), BLOCK)
            dkv_ref[rows, :] += dkv[b * BLOCK:(b + 1) * BLOCK, :]

    blk = pl.BlockSpec((att, BLOCK), lambda n: (0, n))
    whole = lambda a: pl.BlockSpec(a.shape, lambda n: (0, 0, 0))
    return pl.pallas_call(
        body, name="attention_backward", grid=(nb,),
        out_shape=(SDS((att, s_len), F32), SDS((s_len, 2 * kvw), F32), SDS(bias.shape, F32), SDS(sink.shape, F32)),
        in_specs=[blk] + _window_specs(s_len, kblk) + _window_specs(s_len, kblk + 1) + [blk, blk, whole(bias), whole(sink)],
        out_specs=(blk, pl.BlockSpec((s_len, 2 * kvw), lambda n: (0, 0)), whole(bias), whole(sink)),
        compiler_params=_cparams(),
    )(qt, proj, proj, proj, proj, proj, proj, y_att, dy, bias, sink)


def _rms_cols(x, g):
    rs = lax.rsqrt(jnp.mean(x * x, axis=0, keepdims=True) + EPS)
    xh = x * rs
    return xh, rs, xh * g


def _rms_cols_bwd(dy, xh, rs, g):
    dxh = dy * g
    dx = rs * (dxh - xh * jnp.mean(dxh * xh, axis=0, keepdims=True))
    return dx, dy * xh


def mix_output(x, y_rec, y_att, g_rec, g_att, wfull, lay, tm=512):
    s_len, d = x.shape
    tm = min(tm, s_len)
    lw = y_rec.shape[0] * LANE
    att = y_att.shape[0]

    def body(x_ref, yr_ref, ya_ref, gr_ref, ga_ref, w_ref, o_ref):
        _, _, nr = _rms(_join_lane_blocks(yr_ref), gr_ref[...])
        _, _, na = _rms_cols(ya_ref[...], ga_ref[...])
        w = w_ref[:, :, lay.ih:, :].reshape(d, d)
        o_ref[...] = x_ref[...] + dot_nn(nr.astype(BF), w[:lw]) + dot_tn(na.astype(BF), w[lw:])

    row = pl.BlockSpec((tm, d), lambda i: (i, 0))
    return pl.pallas_call(
        body, name="mix_output", grid=(s_len // tm,), out_shape=SDS((s_len, d), F32),
        in_specs=[row, _cbm_spec(lw // LANE, tm), pl.BlockSpec((att, tm), lambda i: (0, i)),
                  pl.BlockSpec((1, lw), lambda i: (0, 0)), pl.BlockSpec((att, 1), lambda i: (0, 0)),
                  _w_spec(lay.fh, d, lay.MIX_BLK)],
        out_specs=row, compiler_params=_cparams(),
    )(x, y_rec, y_att, g_rec, g_att, wfull)


def mix_output_backward(dout, y_rec, y_att, g_rec, g_att, wfull, lay, deps=(), tm=1024):
    s_len, d = dout.shape
    tm = min(tm, s_len)
    lw = y_rec.shape[0] * LANE
    att = y_att.shape[0]
    nt = s_len // tm

    def body(do_ref, yr_ref, ya_ref, gr_ref, ga_ref, w_ref, *rest):
        dyr_ref, dya_ref, dgr_ref, dga_ref, o_ref, acc = rest[len(deps):]
        i = pl.program_id(0)
        gr = gr_ref[...]
        ga = ga_ref[...]
        xhr, rsr, nr = _rms(_join_lane_blocks(yr_ref), gr)
        xha, rsa, na = _rms_cols(ya_ref[...], ga)
        dob = do_ref[...].astype(BF)
        w = w_ref[:, :, lay.ih:, :].reshape(d, d)
        dyr, dgr_row = _rms_bwd(dot_nt(dob, w[:lw]), xhr, rsr, gr)
        dya, dga_col = _rms_cols_bwd(dot_nt(w[lw:], dob), xha, rsa, ga)
        for j, piece in enumerate(_lane_blocks(dyr)):
            dyr_ref[j] = piece
        dya_ref[...] = dya

        @pl.when(i == 0)
        def _():
            dgr_ref[...] = jnp.zeros_like(dgr_ref)
            dga_ref[...] = jnp.zeros_like(dga_ref)
            acc[...] = jnp.zeros_like(acc)

        dgr_ref[...] += jnp.sum(dgr_row, axis=0, keepdims=True)
        dga_ref[...] += jnp.sum(dga_col, axis=1, keepdims=True)
        acc[0:lw, :] += dot_tn(nr.astype(BF), dob)
        acc[lw:, :] += dot_nn(na.astype(BF), dob)

        @pl.when(i == nt - 1)
        def _():
            for p in range(N_CHIPS):
                for q in range(2):
                    o_ref[p, q] = acc[pl.ds((2 * p + q) * lay.oh, lay.oh), :].astype(o_ref.dtype)

    row = pl.BlockSpec((tm, d), lambda i: (i, 0))
    return pl.pallas_call(
        body, name="mix_output_backward", grid=(nt,),
        out_shape=(SDS(y_rec.shape, F32), SDS(y_att.shape, F32), SDS((1, lw), F32), SDS((att, 1), F32),
                   SDS((N_CHIPS, 2, lay.oh, d), BF)),
        in_specs=[row, _cbm_spec(lw // LANE, tm), pl.BlockSpec((att, tm), lambda i: (0, i)),
                  pl.BlockSpec((1, lw), lambda i: (0, 0)), pl.BlockSpec((att, 1), lambda i: (0, 0)),
                  _w_spec(lay.fh, d, lay.MIX_BLK)] + [ANY] * len(deps),
        out_specs=(_cbm_spec(lw // LANE, tm), pl.BlockSpec((att, tm), lambda i: (0, i)),
                   pl.BlockSpec((1, lw), lambda i: (0, 0)), pl.BlockSpec((att, 1), lambda i: (0, 0)),
                   pl.BlockSpec((N_CHIPS, 2, lay.oh, d), lambda i: (0, 0, 0, 0))),
        scratch_shapes=[pltpu.VMEM((d, d), F32)], compiler_params=_cparams(),
    )(dout, y_rec, y_att, g_rec, g_att, wfull, *deps)


def loss_head(x, gain, target, tm=512):
    s_len, d = x.shape
    tm = min(tm, s_len)

    def body(x_ref, g_ref, t_ref, dx_ref, dg_ref, loss_ref):
        g = g_ref[...]
        xh, rs, y = _rms(x_ref[...], g)
        err = y - t_ref[...]

        @pl.when(pl.program_id(0) == 0)
        def _():
            dg_ref[...] = jnp.zeros_like(dg_ref)
            loss_ref[...] = jnp.zeros_like(loss_ref)

        part = 0.5 * jnp.sum(jnp.mean(err * err, axis=-1, keepdims=True), axis=0, keepdims=True)
        loss_ref[...] += jnp.broadcast_to(part, loss_ref.shape)
        dx, dgrow = _rms_bwd(err * (1.0 / d), xh, rs, g)
        dx_ref[...] = dx
        dg_ref[...] += jnp.sum(dgrow, axis=0, keepdims=True)

    row = pl.BlockSpec((tm, d), lambda i: (i, 0))
    vec = pl.BlockSpec((1, d), lambda i: (0, 0))
    return pl.pallas_call(
        body, name="loss_head", grid=(s_len // tm,),
        out_shape=(SDS((s_len, d), F32), SDS((1, d), F32), SDS((8, LANE), F32)),
        in_specs=[row, vec, row], out_specs=(row, vec, pl.BlockSpec((8, LANE), lambda i: (0, 0))),
        compiler_params=_cparams(),
    )(x, gain, target)


def _adamw_update(w, g, m, v):
    m = ADAM_B1 * m + (1.0 - ADAM_B1) * g
    v = ADAM_B2 * v + (1.0 - ADAM_B2) * (g * g)
    m_hat = m / (1.0 - ADAM_B1 ** ADAM_STEP)
    v_hat = v / (1.0 - ADAM_B2 ** ADAM_STEP)
    return -ADAM_LR * (m_hat / (jnp.sqrt(v_hat) + ADAM_EPS) + ADAM_WD * w), m, v


def adamw(w, g, m, v, tr=512):
    rows, cols = w.shape
    tr = _row_chunk(rows, tr, 8)

    def body(w_ref, g_ref, m_ref, v_ref, d_ref, nm_ref, nv_ref):
        d_ref[...], nm_ref[...], nv_ref[...] = _adamw_update(w_ref[...], g_ref[...], m_ref[...], v_ref[...])

    blk = pl.BlockSpec((tr, cols), lambda i: (i, 0))
    return pl.pallas_call(
        body, name="adamw", grid=(rows // tr,), out_shape=(SDS(w.shape, F32),) * 3,
        in_specs=[blk] * 4, out_specs=(blk,) * 3, compiler_params=_cparams(),
    )(w, g, m, v)


def adamw_layer(gf, blk, row_off, n_half, l, w, m, v, outs, deps=()):
    fh = gf.shape[1] // Layout.BLOCKS
    d = gf.shape[2]
    nd = len(deps)

    def body(gf_ref, w_ref, m_ref, v_ref, *rest):
        g_ref, d_ref, nm_ref, nv_ref = rest[4 + nd:]
        g = gf_ref[row_off:row_off + n_half, :]
        g_ref[...] = g
        d_ref[...], nm_ref[...], nv_ref[...] = _adamw_update(w_ref[...], g, m_ref[...], v_ref[...])

    gspec = pl.BlockSpec((None, fh, d), lambda h: (h, blk, 0))
    wspec = pl.BlockSpec((None, n_half, d), lambda h: (l, h, 0))
    return pl.pallas_call(
        body, name="adamw_layer", grid=(2,), out_shape=tuple(SDS(o.shape, o.dtype) for o in outs),
        in_specs=[gspec, wspec, wspec, wspec] + [ANY] * (4 + nd), out_specs=(wspec,) * 4,
        input_output_aliases={4 + i: i for i in range(4)}, compiler_params=_cparams(),
    )(gf, w, m, v, *outs, *deps)


def pack_weight(pos, land, blk, l, w, extra=None, deps=()):
    fh, d = land.shape[2] // Layout.BLOCKS, land.shape[3]
    nd = len(deps)

    def body(pos_ref, w_ref, *rest):
        o_ref = rest[-1]
        a = w_ref[...].astype(BF)
        n = a.shape[0] // 2
        for h in range(2):
            o_ref[h, 0:n, :] = a[h * n:(h + 1) * n]
        if extra is not None:
            b = rest[0][...].astype(BF)
            nb = b.shape[0] // 2
            for h in range(2):
                o_ref[h, n:n + nb, :] = b[h * nb:(h + 1) * nb]

    def whole(a):
        return pl.BlockSpec((None,) + a.shape[1:], lambda i, p: (l, 0, 0))

    ins = [w] + ([extra] if extra is not None else [])
    return pl.pallas_call(
        body, name="pack_weight", out_shape=SDS(land.shape, land.dtype),
        grid_spec=pltpu.PrefetchScalarGridSpec(
            num_scalar_prefetch=1, grid=(1,),
            in_specs=[whole(a) for a in ins] + [ANY] * (1 + nd),
            out_specs=pl.BlockSpec((None, 2, fh, d), lambda i, p: (p[0], 0, blk, 0))),
        input_output_aliases={1 + len(ins): 0}, compiler_params=_cparams(),
    )(pos, *ins, land, *deps)


def _rows_of(shape, width):
    return -(-int(np.prod(shape)) // (SUBLANES * width)) * SUBLANES


def _pack_rows(arrays, width):
    parts = []
    for a in arrays:
        flat = a.reshape(-1).astype(F32)
        r = _rows_of(a.shape, width)
        parts.append(jnp.pad(flat, (0, r * width - flat.shape[0])).reshape(r, width))
    return jnp.concatenate(parts, axis=0)


def _unpack_rows(buf, shapes):
    out, row = [], 0
    for shp in shapes:
        r = _rows_of(shp, buf.shape[1])
        out.append(buf[row:row + r].reshape(-1)[:int(np.prod(shp))].reshape(shp))
        row += r
    return out


def _t5_buckets(rel):
    half = N_BUCKETS // 2
    max_exact = half // 2
    ret = (rel > 0).astype(jnp.int32) * half
    n = jnp.abs(rel)
    n_f = jnp.maximum(n, 1).astype(F32)
    large = max_exact + (jnp.log(n_f / max_exact) / math.log(MAX_DISTANCE / max_exact) * (half - max_exact)).astype(jnp.int32)
    large = jnp.minimum(large, half - 1)
    return ret + jnp.where(n < max_exact, n, large)


def _band_buckets():
    t = jnp.arange(BLOCK)[:, None]
    j = jnp.arange(3 * BLOCK)[None, :]
    rel = j - BLOCK - t
    return _t5_buckets(rel), jnp.abs(rel) <= WINDOW


def _block_diag_pairs(w):
    depth, two, nblk, bw, _ = w.shape
    pairs = w.reshape(depth, two, nblk // 2, 2, bw, bw)
    z = jnp.zeros_like(pairs[:, :, :, 0])
    top = jnp.concatenate([pairs[:, :, :, 0], z], axis=-1)
    bot = jnp.concatenate([z, pairs[:, :, :, 1]], axis=-1)
    return jnp.concatenate([top, bot], axis=-2)


def _diag_blocks(dw):
    bw = dw.shape[-1] // 2
    a = dw[:, :, :bw, :bw]
    b = dw[:, :, bw:, bw:]
    return jnp.stack([a, b], axis=2).reshape(dw.shape[0], 2 * dw.shape[1], bw, bw)


def kernel(x, ffn1_norm, ffn1_w_gate, ffn1_w_up, ffn1_w_down, mix_norm, w_in, conv_w, conv_b, lru_w_a, lru_b_a, lru_w_x, lru_b_x, lru_lambda, attn_sink, rel_bias, lru_out_norm, attn_out_norm, w_out, ffn2_norm, ffn2_w_gate, ffn2_w_up, ffn2_w_down, final_norm, loss_target, m_ffn1_norm, m_ffn1_w_gate, m_ffn1_w_up, m_ffn1_w_down, m_mix_norm, m_w_in, m_conv_w, m_conv_b, m_lru_w_a, m_lru_b_a, m_lru_w_x, m_lru_b_x, m_lru_lambda, m_attn_sink, m_rel_bias, m_lru_out_norm, m_attn_out_norm, m_w_out, m_ffn2_norm, m_ffn2_w_gate, m_ffn2_w_up, m_ffn2_w_down, m_final_norm, v_ffn1_norm, v_ffn1_w_gate, v_ffn1_w_up, v_ffn1_w_down, v_mix_norm, v_w_in, v_conv_w, v_conv_b, v_lru_w_a, v_lru_b_a, v_lru_w_x, v_lru_b_x, v_lru_lambda, v_attn_sink, v_rel_bias, v_lru_out_norm, v_attn_out_norm, v_w_out, v_ffn2_norm, v_ffn2_w_gate, v_ffn2_w_up, v_ffn2_w_down, v_final_norm):
    depth, d = ffn1_norm.shape
    d_ff = N_CHIPS * ffn1_w_gate.shape[2]
    d_in = N_CHIPS * w_in.shape[2]
    lw = conv_b.shape[1]
    att = N_HEADS * HEAD_DIM
    lay = Layout(d, d_ff, d_in)
    k_chip = 2 * lax.axis_index("x") + lax.axis_index("y")
    pos = jnp.stack([k_chip, lax.axis_index("c")]).astype(jnp.int32)

    def rows_major(a):
        return jnp.swapaxes(a, 1, 2)

    mats = (rows_major(ffn1_w_gate), rows_major(ffn1_w_up), ffn1_w_down,
            rows_major(ffn2_w_gate), rows_major(ffn2_w_up), ffn2_w_down)

    def pack_layer(l, deps=()):
        land = lax.empty((N_CHIPS, 2, lay.rows, d), BF)
        for m, a in enumerate(mats):
            land = pack_weight(pos, land, m, l, a, deps=deps if m == 0 else ())
        return pack_weight(pos, land, lay.MIX_BLK, l, rows_major(w_in), extra=w_out)

    def gather_start(l, land):
        return split_start(f"gather_start_{l}", [land], 3, gather_plan)

    def gather_wait(l, started, after):
        ssem, rsem, bufs, _ = started
        return split_wait(f"gather_wait_{l}", ssem, rsem, bufs, after, gather_plan)

    sharded_small = (conv_w, lru_b_a, lru_b_x, lru_lambda)
    sshard = jnp.concatenate([a.reshape(-1, LANE) for a in sharded_small], axis=0)
    sfull = gather_small(sshard)
    small_full, off = [], 0
    for a in sharded_small:
        r = a.shape[0] * a.shape[1]
        piece = sfull[:, off:off + r].reshape((N_CHIPS,) + a.shape)
        small_full.append(jnp.moveaxis(piece, 0, 2).reshape(a.shape[0], a.shape[1], N_CHIPS * LANE))
        off += r
    conv_w_f, b_a_f, b_x_f, lam_f = small_full

    zrow = jnp.zeros((1, lw), F32)
    wblk_a = _block_diag_pairs(lru_w_a)
    wblk_x = _block_diag_pairs(lru_w_x)
    buckets, in_band = _band_buckets()
    onehot = (buckets.reshape(-1)[:, None] == jnp.arange(N_BUCKETS)[None, :]).astype(F32)
    bias = jnp.dot(rel_bias.T, onehot.T, precision=lax.Precision.HIGHEST).reshape(N_HEADS, BLOCK, 3 * BLOCK)
    bias = jnp.where(in_band[None], bias, NEG_INF)
    bias = bias.reshape(N_KV_HEADS, KV_GROUP, BLOCK, 3 * BLOCK).transpose(0, 3, 1, 2).reshape(N_KV_HEADS, 3 * BLOCK, KV_GROUP * BLOCK)
    kblk = 2 * lw // LANE

    def layer_small(l):
        cvec = jnp.concatenate([conv_w_f[l], jnp.zeros((8 - CONV_WIDTH, lw), F32)], axis=0)
        pvec = jnp.concatenate([conv_b[l][None], b_a_f[l], b_x_f[l], lam_f[l], zrow], axis=0)
        wblk = jnp.stack([wblk_a[l, 0], wblk_x[l, 0], wblk_a[l, 1], wblk_x[l, 1]]).astype(BF)
        sink = jnp.broadcast_to(jnp.repeat(attn_sink[l], BLOCK).reshape(N_KV_HEADS, 1, KV_GROUP * BLOCK),
                                (N_KV_HEADS, 8, KV_GROUP * BLOCK))
        return cvec, pvec, wblk, sink

    xs = x[0]
    wfull = [None] * depth
    parts = [(0, 3 * lay.fh), (3 * lay.fh, lay.rows - 3 * lay.fh)]
    plans = [(functools.partial(gather_plan, rows=p), functools.partial(handover_plan, rows=p)) for p in parts]
    land = pack_layer(0, deps=(sfull,))
    first = split_start("gather_start_0a", [land], 3, plans[0][0])
    second = split_start("gather_start_0b", first[2], 3, plans[1][0])
    lands = {l: pack_layer(l, deps=(second[3],)) for l in range(1, depth)}
    land = split_wait("gather_wait_0a", first[0], first[1], second[2], [xs] + list(lands.values()), plans[0][0])
    wfull[0], = exchange_now("gather_handover_0a", land, 3, plans[0][1])
    started = None
    saved = []
    for l in range(depth):
        cvec, pvec, wblk, sink = layer_small(l)
        deps = (started[3],) if started is not None else ()
        x1, gate1, up1 = ffn_forward(xs, ffn1_norm[l][None], wfull[l], lay, 0, deps=deps)
        deps = ()
        if l == 0:
            land = split_wait("gather_wait_0b", second[0], second[1], [wfull[0]], [x1], plans[1][0])
            wfull[0], = exchange_now("gather_handover_0b", land, 3, plans[1][1])
            if depth > 1:
                started = gather_start(1, lands[1])
                deps = (started[3],)
        proj, qt, vt = mix_project(x1, mix_norm[l][None], wfull[l], lay, lw, att)
        y_rec, hs = lru_forward(proj, cvec, pvec, wblk, lw, deps=deps)
        y_att = attention_forward(qt, proj, vt, bias, sink, kblk)
        x2 = mix_output(x1, y_rec, y_att, lru_out_norm[l][None], attn_out_norm[l][:, None], wfull[l], lay)
        deps, handover = (), None
        if 0 < l < depth - 1:
            land, = gather_wait(l + 1, started, [x2])
            started = gather_start(l + 2, lands[l + 2]) if l + 2 < depth else None
            handover = split_start(f"gather_handover_start_{l + 1}", [land], 3, handover_plan)
            deps = (handover[3],) + ((started[3],) if started is not None else ())
        x3, gate2, up2 = ffn_forward(x2, ffn2_norm[l][None], wfull[l], lay, 1, deps=deps)
        saved.append((xs, x1, x2, proj, qt, y_rec, hs, y_att, (gate1, up1), (gate2, up2)))
        xs = x3
        if handover is not None:
            wfull[l + 1], = split_wait(f"gather_handover_wait_{l + 1}", handover[0], handover[1], handover[2], [x3],
                                       handover_plan)
        elif l == 0 and depth > 1:
            wfull[1], = exchange_now("gather_handover_1", gather_wait(1, started, [x3]), 3, handover_plan)
            started = gather_start(2, lands[2]) if depth > 2 else None

    dx, d_final, loss_tile = loss_head(xs, final_norm[None], loss_target[0])
    loss = lax.psum(loss_tile[0, 0], ("x", "y", "c"))

    layer_names = ["ffn1_norm", "mix_norm", "conv_w", "conv_b", "lru_w_a", "lru_b_a", "lru_w_x", "lru_b_x", "lru_lambda",
                   "attn_sink", "lru_out_norm", "attn_out_norm", "ffn2_norm"]
    dbias_total = jnp.zeros(bias.shape, F32)

    def ffn_back(xin, gain, dout, pre, gb, l, which, deps=()):
        dxo, dg, lhs, rhs = ffn_backward_dx(xin, gain, dout, *pre, wfull[l], lay, which, deps=deps)
        return dxo, dg[0], weight_grad_tn(lhs, rhs, gb, lay, 3 * which)

    def pair_start(l, gb, sb):
        lands = [lax.empty((N_CHIPS,) + gb.shape[2:], gb.dtype), lax.empty(sb.shape, sb.dtype)]
        return split_start(f"pair_start_{l}", [gb, sb] + lands, N_CHIPS + 1, pair_plan)

    def reduce_start(l, paired, after):
        gb, sb, p1, sp1 = split_wait(f"pair_wait_{l}", paired[0], paired[1], paired[2], after, pair_plan)
        cs = pair_sum(pos, gb, p1)
        ss = small_pair_sum(sb, sp1)
        lands = [lax.empty((3,) + cs.shape[1:], cs.dtype), lax.empty((N_CHIPS,) + ss.shape, ss.dtype)]
        return split_start(f"reduce_start_{l}", [cs, ss] + lands, 6, reduce_plan)

    def reduce_finish(l, started, after):
        ssem, rsem, bufs, _ = started
        cs, ss, p3, sp3 = split_wait(f"reduce_wait_{l}", ssem, rsem, bufs, after, reduce_plan)
        return chip_sum(pos, cs, p3), small_chip_sum(pos, ss, sp3)

    gf = [None] * depth
    small_sums = [None] * depth
    small_shapes = [None] * depth
    paired = None
    in_flight = None
    finals = {}
    tokens = []
    for l in reversed(range(depth)):
        x0, x1, x2, proj, qt, y_rec, hs, y_att, pre1, pre2 = saved[l]
        cvec, pvec, wblk, sink = layer_small(l)
        gb = lax.empty((N_CHIPS, 2, lay.rows, d), BF)
        part = {}
        dx, part["ffn2_norm"], gb = ffn_back(x2, ffn2_norm[l][None], dx, pre2, gb, l, 1, deps=tuple(tokens))
        deps = ()
        if paired is not None:
            in_flight = (paired[0], reduce_start(paired[0], paired[1], [dx, gb]))
            deps = (in_flight[1][3],)
        dyr, dya, dgr, dga, dwout = mix_output_backward(dx, y_rec, y_att, lru_out_norm[l][None], attn_out_norm[l][:, None],
                                                        wfull[l], lay, deps=deps)
        part["lru_out_norm"] = dgr[0]
        part["attn_out_norm"] = dga[:, 0]
        dq, dkv, dbias, dsink = attention_backward(qt, proj, y_att, dya, bias, sink, kblk)
        dbias_total = dbias_total + dbias
        part["attn_sink"] = jnp.sum(dsink[:, 0, :].reshape(N_HEADS, BLOCK), axis=1)
        dxr, dgt, dcv, dpv, dwb = lru_backward(proj, hs, dyr, cvec, pvec, wblk, lw)
        part["conv_w"] = dcv[:CONV_WIDTH]
        part["conv_b"] = dpv[0]
        part["lru_b_a"] = dpv[1:3]
        part["lru_b_x"] = dpv[3:5]
        part["lru_lambda"] = dpv[5:7]
        part["lru_w_a"] = _diag_blocks(jnp.stack([dwb[0], dwb[2]]))
        part["lru_w_x"] = _diag_blocks(jnp.stack([dwb[1], dwb[3]]))
        dx, dgm, gb = mix_project_backward(x1, mix_norm[l][None], dx, dxr, dgt, dq, dkv, dwout, wfull[l], gb, lay)
        part["mix_norm"] = dgm[0]
        dx, part["ffn1_norm"], gb = ffn_back(x0, ffn1_norm[l][None], dx, pre1, gb, l, 0)
        pieces = [part[n] for n in layer_names]
        if l == 0:
            dbias_heads = dbias_total.reshape(N_KV_HEADS, 3 * BLOCK, KV_GROUP, BLOCK).transpose(0, 2, 3, 1)
            d_rel_bias = jnp.dot(dbias_heads.reshape(N_HEADS, -1), onehot, precision=lax.Precision.HIGHEST).T
            pieces += [d_rel_bias, d_final[0]]
        small_shapes[l] = [p.shape for p in pieces]
        paired = (l, pair_start(l, gb, _pack_rows(pieces, 1024)))
        tokens = [paired[1][3]]
        if in_flight is not None:
            above = in_flight[0]
            half, small_sums[above] = reduce_finish(above, in_flight[1], [dx])
            finals[above] = split_start(f"final_start_{above}", [half], 1, final_plan)
            tokens.append(finals[above][3])
            in_flight = None
    grad_x = dx[None]

    weights = dict(ffn1_norm=ffn1_norm, ffn1_w_gate=ffn1_w_gate, ffn1_w_up=ffn1_w_up, ffn1_w_down=ffn1_w_down, mix_norm=mix_norm, w_in=w_in, conv_w=conv_w, conv_b=conv_b, lru_w_a=lru_w_a, lru_b_a=lru_b_a, lru_w_x=lru_w_x, lru_b_x=lru_b_x, lru_lambda=lru_lambda, attn_sink=attn_sink, rel_bias=rel_bias, lru_out_norm=lru_out_norm, attn_out_norm=attn_out_norm, w_out=w_out, ffn2_norm=ffn2_norm, ffn2_w_gate=ffn2_w_gate, ffn2_w_up=ffn2_w_up, ffn2_w_down=ffn2_w_down, final_norm=final_norm)
    m_in = dict(ffn1_norm=m_ffn1_norm, ffn1_w_gate=m_ffn1_w_gate, ffn1_w_up=m_ffn1_w_up, ffn1_w_down=m_ffn1_w_down, mix_norm=m_mix_norm, w_in=m_w_in, conv_w=m_conv_w, conv_b=m_conv_b, lru_w_a=m_lru_w_a, lru_b_a=m_lru_b_a, lru_w_x=m_lru_w_x, lru_b_x=m_lru_b_x, lru_lambda=m_lru_lambda, attn_sink=m_attn_sink, rel_bias=m_rel_bias, lru_out_norm=m_lru_out_norm, attn_out_norm=m_attn_out_norm, w_out=m_w_out, ffn2_norm=m_ffn2_norm, ffn2_w_gate=m_ffn2_w_gate, ffn2_w_up=m_ffn2_w_up, ffn2_w_down=m_ffn2_w_down, final_norm=m_final_norm)
    v_in = dict(ffn1_norm=v_ffn1_norm, ffn1_w_gate=v_ffn1_w_gate, ffn1_w_up=v_ffn1_w_up, ffn1_w_down=v_ffn1_w_down, mix_norm=v_mix_norm, w_in=v_w_in, conv_w=v_conv_w, conv_b=v_conv_b, lru_w_a=v_lru_w_a, lru_b_a=v_lru_b_a, lru_w_x=v_lru_w_x, lru_b_x=v_lru_b_x, lru_lambda=v_lru_lambda, attn_sink=v_attn_sink, rel_bias=v_rel_bias, lru_out_norm=v_lru_out_norm, attn_out_norm=v_attn_out_norm, w_out=v_w_out, ffn2_norm=v_ffn2_norm, ffn2_w_gate=v_ffn2_w_gate, ffn2_w_up=v_ffn2_w_up, ffn2_w_down=v_ffn2_w_down, final_norm=v_final_norm)
    order = list(weights)
    large = [(name, m, 0, lay.fh, m % 3 != 2) for m, name in
             enumerate(("ffn1_w_gate", "ffn1_w_up", "ffn1_w_down", "ffn2_w_gate", "ffn2_w_up", "ffn2_w_down"))]
    large += [("w_in", lay.MIX_BLK, 0, lay.ih, True), ("w_out", lay.MIX_BLK, lay.ih, lay.oh, False)]
    as_rows = {name: [rows_major(src[name]) if flip else src[name] for src in (weights, m_in, v_in)]
               for name, _, _, _, flip in large}
    stacked = {name: tuple(lax.empty(as_rows[name][0].shape, F32) for _ in range(4)) for name, *_ in large}

    def adamw_large(l, deps=()):
        for i, (name, blk, row_off, n_half, _) in enumerate(large):
            stacked[name] = adamw_layer(gf[l], blk, row_off, n_half, l, *as_rows[name], stacked[name],
                                        deps=deps if i == 0 else ())

    last = paired[0]
    crossing = reduce_start(last, paired[1], [dx])
    for l in sorted(finals):
        gf[l], = split_wait(f"final_wait_{l}", finals[l][0], finals[l][1], finals[l][2], [crossing[3]], final_plan)
        adamw_large(l, deps=(crossing[3],))
    ready = [buf for name, *_ in large for buf in stacked[name]] if depth > 1 else []
    half, small_sums[last] = reduce_finish(last, crossing, [dx] + ready)
    gf[last], = exchange_now(f"final_now_{last}", [half], 1, final_plan)
    adamw_large(last)

    per_layer = [_unpack_rows(small_sums[l], small_shapes[l]) for l in range(depth)]
    grads = {n: jnp.stack([per_layer[l][i] for l in range(depth)]) for i, n in enumerate(layer_names)}
    grads["rel_bias"], grads["final_norm"] = per_layer[0][len(layer_names):]
    for name in ("conv_w", "lru_b_a", "lru_b_x", "lru_lambda"):
        grads[name] = lax.dynamic_slice_in_dim(grads[name], k_chip * LANE, LANE, axis=2)
    delta, new_m, new_v = {}, {}, {}
    for name, _, _, _, flip in large:
        grads[name], delta[name], new_m[name], new_v[name] = [rows_major(a) if flip else a for a in stacked[name]]
    small = [n for n in order if n not in stacked]
    packed = [_pack_rows([src[n] for n in small], 1024) for src in (weights, grads, m_in, v_in)]
    outs = adamw(*packed)
    shapes = [weights[n].shape for n in small]
    for dst, buf in zip((delta, new_m, new_v), outs):
        dst.update(zip(small, _unpack_rows(buf, shapes)))

    return (loss, grad_x, *[grads[n] for n in order], *[delta[n] for n in order],
            *[new_m[n] for n in order], *[new_v[n] for n in order])
```

```python
import functools
import math

import jax
import jax.numpy as jnp
import numpy as np
from jax import lax
from jax.experimental import pallas as pl
from jax.experimental.pallas import tpu as pltpu

BF = jnp.bfloat16
F32 = jnp.float32
SDS = jax.ShapeDtypeStruct
MESH = pl.DeviceIdType.MESH
ANY = pl.BlockSpec(memory_space=pl.ANY)

N_CHIPS = 4
N_HEADS = 8
N_KV_HEADS = 2
KV_GROUP = N_HEADS // N_KV_HEADS
HEAD_DIM = 64
BLOCK = 128
WINDOW = 128
N_BUCKETS = 32
MAX_DISTANCE = 128
LRU_C = 8.0
CONV_WIDTH = 4
LANE = 128
SUBLANES = 8
SCAN_CHAINS = 8
EPS = 1e-6
FFN_RES = 0.5
NEG_INF = -1e30
ADAM_LR = 0.001
ADAM_B1 = 0.9
ADAM_B2 = 0.999
ADAM_EPS = 1e-08
ADAM_WD = 0.01
ADAM_STEP = 10
VMEM_LIMIT = 60000 * 1024
GELU_C = math.sqrt(2.0 / math.pi)


def dot_nn(a, b):
    return lax.dot_general(a, b, (((1,), (0,)), ((), ())), preferred_element_type=F32)


def dot_nt(a, b):
    return lax.dot_general(a, b, (((1,), (1,)), ((), ())), preferred_element_type=F32)


def dot_tn(a, b):
    return lax.dot_general(a, b, (((0,), (0,)), ((), ())), preferred_element_type=F32)


def _cparams(**kw):
    return pltpu.CompilerParams(vmem_limit_bytes=VMEM_LIMIT, **kw)


class Layout:
    MIX_BLK = 6
    BLOCKS = 7

    def __init__(self, d_model, d_ff, d_in):
        self.fh = d_ff // (2 * N_CHIPS)
        self.ih = d_in // (2 * N_CHIPS)
        self.oh = d_model // (2 * N_CHIPS)
        assert self.ih + self.oh == self.fh, "w_in^T and w_out rows must fill one ffn-sized block"
        self.rows = self.BLOCKS * self.fh


def _row_chunk(rows, target, step=16):
    best = rows
    for c in range(step, min(rows, target) + 1, step):
        if rows % c == 0:
            best = c
    return best


def _mesh_pos():
    return lax.axis_index("x"), lax.axis_index("y"), lax.axis_index("c")


def _rcopy(src, dst, ssem, rsem, dev):
    return pltpu.make_async_remote_copy(src_ref=src, dst_ref=dst, send_sem=ssem, recv_sem=rsem,
                                        device_id=dev, device_id_type=MESH)


HBM = pl.BlockSpec(memory_space=pltpu.HBM)
SEM = pl.BlockSpec(memory_space=pltpu.SEMAPHORE)
DATAFLOW = pltpu.SideEffectType.DATAFLOW_SIDE_EFFECTING


def _chip_peers():
    x, y, c = _mesh_pos()
    peers = [(1 - x, y), (x, 1 - y), (1 - x, 1 - y)]
    return x, y, c, 2 * x + y, [(px, py, 2 * px + py) for px, py in peers]


def split_start(name, bufs, n, plan):
    nb = len(bufs)

    def body(*refs):
        sends, _ = plan(refs[:nb], refs[nb], refs[nb + 1])
        for cp in sends:
            cp.start()
        refs[-1][...] = jnp.zeros_like(refs[-1])

    out = pl.pallas_call(
        body, name=name,
        out_shape=(pltpu.SemaphoreType.DMA((n,)), pltpu.SemaphoreType.DMA((n,)),
                   *[pltpu.HBM(b.shape, b.dtype) for b in bufs], SDS((8, LANE), F32)),
        in_specs=[HBM] * nb, out_specs=(SEM, SEM, *([HBM] * nb), pl.BlockSpec(memory_space=pltpu.VMEM)),
        input_output_aliases={i: 2 + i for i in range(nb)},
        compiler_params=pltpu.CompilerParams(has_side_effects=DATAFLOW),
    )(*[pltpu.with_memory_space_constraint(b, pltpu.HBM) for b in bufs])
    return out[0], out[1], list(out[2:2 + nb]), out[-1]


def split_wait(name, ssem, rsem, bufs, after, plan):
    nb = len(bufs)

    def body(*refs):
        sends, recvs = plan(refs[:nb], refs[nb], refs[nb + 1])
        for cp in recvs:
            cp.wait_recv()
        for cp in sends:
            cp.wait_send()

    out = pl.pallas_call(
        body, name=name, out_shape=tuple(pltpu.HBM(b.shape, b.dtype) for b in bufs),
        in_specs=[HBM] * nb + [SEM, SEM] + [ANY] * len(after), out_specs=tuple([HBM] * nb),
        input_output_aliases={i: i for i in range(nb)},
        compiler_params=pltpu.CompilerParams(has_side_effects=DATAFLOW),
    )(*bufs, ssem, rsem, *after)
    return list(out)


def gather_plan(refs, ssem, rsem, rows=None):
    land_ref, = refs
    _, _, c, k, peers = _chip_peers()
    part = (lambda a: a) if rows is None else (lambda a: a.at[pl.ds(rows[0], rows[1])])
    sends = [_rcopy(part(land_ref.at[k, c]), part(land_ref.at[k, c]), ssem.at[j], rsem.at[j], (px, py, c))
             for j, (px, py, _) in enumerate(peers)]
    recvs = [_rcopy(part(land_ref.at[kp, c]), part(land_ref.at[kp, c]), ssem.at[j], rsem.at[j], (px, py, c))
             for j, (px, py, kp) in enumerate(peers)]
    return sends, recvs


def reduce_plan(refs, ssem, rsem):
    cs_ref, ss_ref, p3_ref, sp3_ref = refs
    _, _, c, k, peers = _chip_peers()
    sends, recvs = [], []
    for j, (px, py, kp) in enumerate(peers):
        sends.append(_rcopy(cs_ref.at[kp], p3_ref.at[j], ssem.at[j], rsem.at[j], (px, py, c)))
        recvs.append(_rcopy(cs_ref.at[kp], p3_ref.at[j], ssem.at[j], rsem.at[j], (px, py, c)))
        sends.append(_rcopy(ss_ref, sp3_ref.at[k], ssem.at[3 + j], rsem.at[3 + j], (px, py, c)))
        recvs.append(_rcopy(ss_ref, sp3_ref.at[kp], ssem.at[3 + j], rsem.at[3 + j], (px, py, c)))
    return sends, recvs


def gather_small(sshard):
    def body(s_ref, sf_ref, lsem, ssem, rsem):
        _, _, c, k, peers = _chip_peers()
        own = pltpu.make_async_copy(s_ref, sf_ref.at[k], lsem)
        own.start()
        sends = [_rcopy(s_ref, sf_ref.at[k], ssem.at[j], rsem.at[j], (px, py, c)) for j, (px, py, _) in enumerate(peers)]
        recvs = [_rcopy(s_ref, sf_ref.at[kp], ssem.at[j], rsem.at[j], (px, py, c)) for j, (px, py, kp) in enumerate(peers)]
        for cp in sends:
            cp.start()
        for cp in recvs:
            cp.wait_recv()
        for cp in sends:
            cp.wait_send()
        own.wait()

    return pl.pallas_call(
        body, name="gather_small", out_shape=SDS((N_CHIPS,) + sshard.shape, sshard.dtype),
        in_specs=[ANY], out_specs=ANY,
        scratch_shapes=[pltpu.SemaphoreType.DMA, pltpu.SemaphoreType.DMA((3,)), pltpu.SemaphoreType.DMA((3,))],
    )(sshard)


def exchange_now(name, bufs, n, plan):
    nb = len(bufs)

    def body(*refs):
        sends, recvs = plan(refs[nb:2 * nb], refs[2 * nb], refs[2 * nb + 1])
        for cp in sends:
            cp.start()
        for cp in recvs:
            cp.wait_recv()
        for cp in sends:
            cp.wait_send()

    return list(pl.pallas_call(
        body, name=name, out_shape=tuple(SDS(b.shape, b.dtype) for b in bufs),
        in_specs=[ANY] * nb, out_specs=tuple([ANY] * nb), input_output_aliases={i: i for i in range(nb)},
        scratch_shapes=[pltpu.SemaphoreType.DMA((n,)), pltpu.SemaphoreType.DMA((n,))],
    )(*bufs))


def handover_plan(refs, ssem, rsem, rows=None):
    land_ref, = refs
    x, y, c, _, peers = _chip_peers()
    sib = (x, y, 1 - c)
    part = (lambda a: a) if rows is None else (lambda a: a.at[pl.ds(rows[0], rows[1])])
    sends = [_rcopy(part(land_ref.at[kp, c]), part(land_ref.at[kp, c]), ssem.at[j], rsem.at[j], sib)
             for j, (_, _, kp) in enumerate(peers)]
    recvs = [_rcopy(part(land_ref.at[kp, 1 - c]), part(land_ref.at[kp, 1 - c]), ssem.at[j], rsem.at[j], sib)
             for j, (_, _, kp) in enumerate(peers)]
    return sends, recvs


def pair_plan(refs, ssem, rsem):
    gb_ref, sb_ref, p_ref, sp_ref = refs
    x, y, c = _mesh_pos()
    sib = (x, y, 1 - c)
    n = gb_ref.shape[0]
    copies = [_rcopy(gb_ref.at[kk, 1 - c], p_ref.at[kk], ssem.at[kk], rsem.at[kk], sib) for kk in range(n)]
    copies.append(_rcopy(sb_ref, sp_ref, ssem.at[n], rsem.at[n], sib))
    return copies, copies


def final_plan(refs, ssem, rsem):
    gf_ref, = refs
    x, y, c = _mesh_pos()
    sib = (x, y, 1 - c)
    return ([_rcopy(gf_ref.at[c], gf_ref.at[c], ssem.at[0], rsem.at[0], sib)],
            [_rcopy(gf_ref.at[1 - c], gf_ref.at[1 - c], ssem.at[0], rsem.at[0], sib)])


def pair_sum(pos, gb, p1):
    n, _, rh, d = gb.shape
    cr = _row_chunk(rh, 1280)

    def body(pos_ref, a_ref, b_ref, o_ref):
        o_ref[...] = (a_ref[...].astype(F32) + b_ref[...].astype(F32)).astype(o_ref.dtype)

    return pl.pallas_call(
        body, name="pair_sum", out_shape=SDS((n, rh, d), gb.dtype),
        grid_spec=pltpu.PrefetchScalarGridSpec(
            num_scalar_prefetch=1, grid=(n, rh // cr),
            in_specs=[pl.BlockSpec((None, None, cr, d), lambda kk, r, pos: (kk, pos[1], r, 0)),
                      pl.BlockSpec((None, cr, d), lambda kk, r, pos: (kk, r, 0))],
            out_specs=pl.BlockSpec((None, cr, d), lambda kk, r, pos: (kk, r, 0))),
        compiler_params=_cparams(),
    )(pos, gb, p1)


def chip_sum(pos, cs, p3):
    n, rh, d = cs.shape
    cr = _row_chunk(rh, 640)

    def body(pos_ref, a_ref, b_ref, o_ref):
        acc = a_ref[...].astype(F32)
        for j in range(3):
            acc = acc + b_ref[j].astype(F32)
        o_ref[...] = acc

    return pl.pallas_call(
        body, name="chip_sum", out_shape=SDS((2, rh, d), F32),
        grid_spec=pltpu.PrefetchScalarGridSpec(
            num_scalar_prefetch=1, grid=(rh // cr,),
            in_specs=[pl.BlockSpec((None, cr, d), lambda r, pos: (pos[0], r, 0)),
                      pl.BlockSpec((3, cr, d), lambda r, pos: (0, r, 0))],
            out_specs=pl.BlockSpec((None, cr, d), lambda r, pos: (pos[1], r, 0))),
        compiler_params=_cparams(),
    )(pos, cs, p3)


def small_pair_sum(a, b):
    def body(a_ref, b_ref, o_ref):
        o_ref[...] = a_ref[...] + b_ref[...]

    return pl.pallas_call(body, name="small_pair_sum", out_shape=SDS(a.shape, a.dtype),
                          compiler_params=_cparams())(a, b)


def small_chip_sum(pos, own, p):
    ns, w = own.shape

    def body(pos_ref, own_ref, p0, p1, p2, p3, o_ref):
        k = pos_ref[0]
        acc = None
        for chip, ref in enumerate((p0, p1, p2, p3)):
            term = jnp.where(k == chip, own_ref[...], ref[...])
            acc = term if acc is None else acc + term
        o_ref[...] = acc

    def slot(chip):
        return pl.BlockSpec((None, ns, w), lambda i, pos: (jnp.where(pos[0] == chip, (chip + 1) % N_CHIPS, chip), 0, 0))

    return pl.pallas_call(
        body, name="small_chip_sum", out_shape=SDS(own.shape, own.dtype),
        grid_spec=pltpu.PrefetchScalarGridSpec(
            num_scalar_prefetch=1, grid=(1,),
            in_specs=[pl.BlockSpec((ns, w), lambda i, pos: (0, 0))] + [slot(chip) for chip in range(N_CHIPS)],
            out_specs=pl.BlockSpec((ns, w), lambda i, pos: (0, 0))),
        compiler_params=_cparams(),
    )(pos, own, p, p, p, p)


def _rms(x, g):
    rs = lax.rsqrt(jnp.mean(x * x, axis=-1, keepdims=True) + EPS)
    xh = x * rs
    return xh, rs, xh * g


def _rms_bwd(dy, xh, rs, g):
    dxh = dy * g
    dx = rs * (dxh - xh * jnp.mean(dxh * xh, axis=-1, keepdims=True))
    return dx, dy * xh


def _gelu(x):
    t = jnp.tanh(GELU_C * (x + 0.044715 * x * x * x))
    return 0.5 * x * (1.0 + t), t


def _gelu_grad(x, t):
    return 0.5 * (1.0 + t) + 0.5 * x * (1.0 - t * t) * GELU_C * (1.0 + 3.0 * 0.044715 * x * x)


def _shift_rows(v, s, n):
    if s == 0:
        return v
    t = lax.broadcasted_iota(jnp.int32, v.shape, 0)
    rolled = pltpu.roll(v, (-s) % n, 0)
    inside = (t < n - s) if s > 0 else (t >= -s)
    return jnp.where(inside, rolled, 0.0)


def _scan_rows(a_ref, u_ref, h_ref, acum_ref, reverse):
    s_len, w = a_ref.shape
    chunk = min(512, s_len)
    last = 0 if reverse else SUBLANES - 1

    def inside_vregs(ci, _):
        rows = pl.ds(pl.multiple_of(ci * chunk, chunk), chunk)
        a = a_ref[rows, :].reshape(chunk // SUBLANES, SUBLANES, w)
        u = u_ref[rows, :].reshape(chunk // SUBLANES, SUBLANES, w)
        pos = lax.broadcasted_iota(jnp.int32, (1, SUBLANES, w), 1)
        for dist in (1, 2, 4):
            ok = (pos < SUBLANES - dist) if reverse else (pos >= dist)
            shift = SUBLANES - dist if reverse else dist
            u = u + a * jnp.where(ok, pltpu.roll(u, shift, 1), 0.0)
            a = a * jnp.where(ok, pltpu.roll(a, shift, 1), 1.0)
        h_ref[rows, :] = u.reshape(chunk, w)
        acum_ref[rows, :] = a.reshape(chunk, w)
        return 0

    lax.fori_loop(0, s_len // chunk, inside_vregs, 0)

    chains = max(1, min(SCAN_CHAINS, s_len // (8 * SUBLANES)))
    seg = s_len // chains
    nvreg = seg // SUBLANES

    def step(j, carry):
        jj = (nvreg - 1 - j) if reverse else j
        out = []
        for c, (hin, ain) in enumerate(carry):
            rows = pl.ds(pl.multiple_of(c * seg + jj * SUBLANES, SUBLANES), SUBLANES)
            acc = acum_ref[rows, :]
            h = h_ref[rows, :] + acc * hin
            acc = acc * ain
            h_ref[rows, :] = h
            acum_ref[rows, :] = acc
            out.append((jnp.broadcast_to(h[last:last + 1, :], h.shape), jnp.broadcast_to(acc[last:last + 1, :], acc.shape)))
        return tuple(out)

    init = tuple((jnp.zeros((SUBLANES, w), F32), jnp.ones((SUBLANES, w), F32)) for _ in range(chains))
    ends = lax.fori_loop(0, nvreg, step, init, unroll=min(2, nvreg))
    order = range(chains - 2, -1, -1) if reverse else range(1, chains)
    inflow = jnp.zeros((1, w), F32)
    for s in order:
        h, acc = ends[s + 1 if reverse else s - 1]
        inflow = h[0:1, :] + acc[0:1, :] * inflow
        rows = pl.ds(s * seg, seg)
        h_ref[rows, :] = h_ref[rows, :] + acum_ref[rows, :] * inflow


def _w_spec(rows_half, d, blk):
    return pl.BlockSpec((N_CHIPS, 2, rows_half, d), lambda *_: (0, 0, blk, 0), pipeline_mode=pl.Buffered(1))


def ffn_forward(x, gain, wfull, lay, which, deps=(), tm=512):
    s_len, d = x.shape
    tm = min(tm, s_len)
    f = 8 * lay.fh
    fc = f // 2

    def body(x_ref, g_ref, wg_ref, wu_ref, wd_ref, *rest):
        o_ref, gate_ref, up_ref = rest[len(deps):]
        x = x_ref[...]
        _, _, hn = _rms(x, g_ref[...])
        h = hn.astype(BF)
        y = jnp.zeros((tm, d), F32)
        for part in range(2):
            cols = slice(part * fc, (part + 1) * fc)
            gate = dot_nt(h, wg_ref[...].reshape(f, d)[cols])
            up = dot_nt(h, wu_ref[...].reshape(f, d)[cols])
            act = (gate * jax.nn.sigmoid(gate) * up).astype(BF)
            y = y + dot_nn(act, wd_ref[...].reshape(f, d)[cols])
            gate_ref[:, cols] = gate.astype(BF)
            up_ref[:, cols] = up.astype(BF)
        o_ref[...] = x + FFN_RES * y

    row = pl.BlockSpec((tm, d), lambda i: (i, 0))
    wide = pl.BlockSpec((tm, f), lambda i: (i, 0))
    return pl.pallas_call(
        body, name="ffn_forward", grid=(s_len // tm,),
        out_shape=(SDS((s_len, d), F32), SDS((s_len, f), BF), SDS((s_len, f), BF)),
        in_specs=[row, pl.BlockSpec((1, d), lambda i: (0, 0))]
        + [_w_spec(lay.fh, d, 3 * which + m) for m in range(3)] + [ANY] * len(deps),
        out_specs=(row, wide, wide), compiler_params=_cparams(),
    )(x, gain, wfull, wfull, wfull, *deps)


def ffn_backward_dx(x, gain, dout, gate_bf, up_bf, wfull, lay, which, deps=(), tm=256):
    s_len, d = x.shape
    tm = min(tm, s_len)
    f = 8 * lay.fh
    fc = f // 2
    nt = s_len // tm

    def body(x_ref, g_ref, do_ref, gate_ref, up_ref, wg_ref, wu_ref, wd_ref, *rest):
        dx_ref, dg_ref, lhs_ref, rhs_ref = rest[len(deps):]
        dgate_ref, dup_ref, act_ref = lhs_ref.at[0], lhs_ref.at[1], lhs_ref.at[2]
        h_ref, df_ref = rhs_ref.at[0], rhs_ref.at[1]
        x = x_ref[...]
        g = g_ref[...]
        xh, rs, hn = _rms(x, g)
        h = hn.astype(BF)
        do = do_ref[...]
        df = (FFN_RES * do).astype(BF)
        dh = jnp.zeros((tm, d), F32)
        for part in range(2):
            cols = slice(part * fc, (part + 1) * fc)
            wg = wg_ref[...].reshape(f, d)[cols]
            wu = wu_ref[...].reshape(f, d)[cols]
            gate = gate_ref[:, cols].astype(F32)
            up = up_ref[:, cols].astype(F32)
            sg = jax.nn.sigmoid(gate)
            silu = gate * sg
            dact = dot_nt(df, wd_ref[...].reshape(f, d)[cols])
            dup = (dact * silu).astype(BF)
            dgate = (dact * up * (sg * (1.0 + gate * (1.0 - sg)))).astype(BF)
            dh = dh + dot_nn(dgate, wg) + dot_nn(dup, wu)
            dgate_ref[:, cols] = dgate
            dup_ref[:, cols] = dup
            act_ref[:, cols] = (silu * up).astype(BF)
        dxn, dgrow = _rms_bwd(dh, xh, rs, g)
        dx_ref[...] = do + dxn

        @pl.when(pl.program_id(0) == 0)
        def _():
            dg_ref[...] = jnp.zeros_like(dg_ref)

        dg_ref[...] += jnp.sum(dgrow, axis=0, keepdims=True)
        h_ref[...] = h
        df_ref[...] = df

    row = pl.BlockSpec((tm, d), lambda i: (i, 0))
    wide = pl.BlockSpec((tm, f), lambda i: (i, 0))
    vec = pl.BlockSpec((1, d), lambda i: (0, 0))
    return pl.pallas_call(
        body, name="ffn_backward_dx", grid=(nt,),
        out_shape=(SDS((s_len, d), F32), SDS((1, d), F32), SDS((3, s_len, f), BF), SDS((2, s_len, d), BF)),
        in_specs=[row, vec, row, wide, wide] + [_w_spec(lay.fh, d, 3 * which + m) for m in range(3)] + [ANY] * len(deps),
        out_specs=(row, vec, pl.BlockSpec((3, tm, f), lambda i: (0, i, 0)), pl.BlockSpec((2, tm, d), lambda i: (0, i, 0))),
        compiler_params=_cparams(),
    )(x, gain, dout, gate_bf, up_bf, wfull, wfull, wfull, *deps)


def weight_grad_tn(lhs, rhs, gb, lay, blk0, tk=4096):
    nmat, s_len, f = lhs.shape
    tk = min(tk, s_len)
    d = rhs.shape[2]
    fc = f // 2
    nk = s_len // tk

    def body(a_ref, b_ref, gb_ref, o_ref, acc):
        kt = pl.program_id(2)

        @pl.when(kt == 0)
        def _():
            acc[...] = jnp.zeros_like(acc)

        acc[...] += dot_tn(a_ref[...], b_ref[...])

        @pl.when(kt == nk - 1)
        def _():
            for p in range(2):
                for q in range(2):
                    o_ref[p, q] = acc[pl.ds((2 * p + q) * lay.fh, lay.fh), :].astype(o_ref.dtype)

    return pl.pallas_call(
        body, name="weight_grad_tn", grid=(nmat, 2, nk), out_shape=SDS(gb.shape, gb.dtype),
        in_specs=[pl.BlockSpec((None, tk, fc), lambda m, j, kt: (m, kt, j)),
                  pl.BlockSpec((None, tk, d), lambda m, j, kt: (jnp.where(m == nmat - 1, 1, 0), kt, 0)), ANY],
        out_specs=pl.BlockSpec((2, 2, lay.fh, d), lambda m, j, kt: (j, 0, blk0 + m, 0)),
        scratch_shapes=[pltpu.VMEM((fc, d), F32)],
        input_output_aliases={2: 0}, compiler_params=_cparams(),
    )(lhs, rhs, gb)


def _lane_blocks(v):
    return [v[:, j * LANE:(j + 1) * LANE] for j in range(v.shape[1] // LANE)]


def _join_lane_blocks(ref):
    return jnp.concatenate([ref[j] for j in range(ref.shape[0])], axis=1)


def _cbm_spec(nblk, rows, first=0):
    return pl.BlockSpec((nblk, rows, LANE), lambda i: (first // nblk, i, 0))


def mix_project(x, gain, wfull, lay, lw, att, tm=512):
    s_len, d = x.shape
    tm = min(tm, s_len)
    d_in = 8 * lay.ih
    kvw = (d_in - 2 * lw - att) // 2
    ncol = (2 * lw + 2 * kvw) // LANE

    def body(x_ref, g_ref, w_ref, o_ref, qt_ref, vt_ref):
        _, _, hn = _rms(x_ref[...], g_ref[...])
        h = hn.astype(BF)
        w = w_ref[:, :, :lay.ih, :].reshape(d_in, d)
        pieces = _lane_blocks(dot_nt(h, w[:2 * lw])) + _lane_blocks(dot_nt(h, w[2 * lw + att:]))
        for j, piece in enumerate(pieces):
            o_ref[j] = piece
        qt_ref[...] = dot_nt(w[2 * lw:2 * lw + att], h)
        vt_ref[...] = dot_nt(w[2 * lw + att + kvw:], h)

    return pl.pallas_call(
        body, name="mix_project", grid=(s_len // tm,),
        out_shape=(SDS((ncol, s_len, LANE), F32), SDS((att, s_len), F32), SDS((kvw, s_len), F32)),
        in_specs=[pl.BlockSpec((tm, d), lambda i: (i, 0)), pl.BlockSpec((1, d), lambda i: (0, 0)),
                  _w_spec(lay.fh, d, lay.MIX_BLK)],
        out_specs=(_cbm_spec(ncol, tm), pl.BlockSpec((att, tm), lambda i: (0, i)), pl.BlockSpec((kvw, tm), lambda i: (0, i))),
        compiler_params=_cparams(),
    )(x, gain, wfull)


def mix_project_backward(x, gain, dout, dxr, dgt, dqt, dkv, dwout, wfull, gb, lay, tm=512):
    s_len, d = x.shape
    tm = min(tm, s_len)
    d_in = 8 * lay.ih
    nt = s_len // tm
    kvw = dkv.shape[1]
    att = dqt.shape[0]
    nlru = (dxr.shape[0] + dgt.shape[0]) * LANE

    def body(x_ref, g_ref, do_ref, dxr_ref, dgt_ref, dqt_ref, dkv_ref, dwo_ref, w_ref, gb_ref, dx_ref, dg_ref, o_ref, acc):
        i = pl.program_id(0)
        g = g_ref[...]
        xh, rs, hn = _rms(x_ref[...], g)
        h = hn.astype(BF)
        w = w_ref[:, :, :lay.ih, :].reshape(d_in, d)
        dlru = jnp.concatenate([_join_lane_blocks(dxr_ref), _join_lane_blocks(dgt_ref)], axis=1).astype(BF)
        dqt = dqt_ref[...].astype(BF)
        dkv = dkv_ref[...].astype(BF)
        dh = dot_nn(dlru, w[:nlru]) + dot_tn(dqt, w[nlru:nlru + att]) + dot_nn(dkv, w[nlru + att:])
        dxn, dgrow = _rms_bwd(dh, xh, rs, g)
        dx_ref[...] = do_ref[...] + dxn

        @pl.when(i == 0)
        def _():
            dg_ref[...] = jnp.zeros_like(dg_ref)
            acc[...] = jnp.zeros_like(acc)

        dg_ref[...] += jnp.sum(dgrow, axis=0, keepdims=True)
        acc[0:nlru, :] += dot_tn(dlru, h)
        acc[nlru:nlru + att, :] += dot_nn(dqt, h)
        acc[nlru + att:, :] += dot_tn(dkv, h)

        @pl.when(i == nt - 1)
        def _():
            for p in range(N_CHIPS):
                for q in range(2):
                    o_ref[p, q, :lay.ih, :] = acc[pl.ds((2 * p + q) * lay.ih, lay.ih), :].astype(o_ref.dtype)
            o_ref[:, :, lay.ih:, :] = dwo_ref[...]

    row = pl.BlockSpec((tm, d), lambda i: (i, 0))
    vec = pl.BlockSpec((1, d), lambda i: (0, 0))
    return pl.pallas_call(
        body, name="mix_project_backward", grid=(nt,),
        out_shape=(SDS((s_len, d), F32), SDS((1, d), F32), SDS(gb.shape, gb.dtype)),
        in_specs=[row, vec, row, _cbm_spec(dxr.shape[0], tm), _cbm_spec(dgt.shape[0], tm),
                  pl.BlockSpec((att, tm), lambda i: (0, i)), pl.BlockSpec((tm, kvw), lambda i: (i, 0)),
                  pl.BlockSpec(dwout.shape, lambda i: (0, 0, 0, 0)), _w_spec(lay.fh, d, lay.MIX_BLK), ANY],
        out_specs=(row, vec, pl.BlockSpec((N_CHIPS, 2, lay.fh, d), lambda i: (0, 0, lay.MIX_BLK, 0))),
        scratch_shapes=[pltpu.VMEM((d_in, d), F32)],
        input_output_aliases={9: 2}, compiler_params=_cparams(),
    )(x, gain, dout, dxr, dgt, dqt, dkv, dwout, wfull, gb)


def _lru_gates(xc, wb_ref, pv_ref, direction):
    xcb = xc.astype(BF)
    r = jax.nn.sigmoid(dot_nn(xcb, wb_ref[2 * direction]) + pv_ref[1 + direction:2 + direction, :])
    i = jax.nn.sigmoid(dot_nn(xcb, wb_ref[2 * direction + 1]) + pv_ref[3 + direction:4 + direction, :])
    lam = pv_ref[5 + direction:6 + direction, :]
    sp = jnp.maximum(-lam, 0.0) + jnp.log(1.0 + jnp.exp(-jnp.abs(lam)))
    a = jnp.exp(-LRU_C * sp * r)
    mult = jnp.sqrt(1.0 - a * a)
    return xcb, r, i, a, mult, sp


def _conv_rows(xr, cv_ref, bias, n):
    acc = bias + cv_ref[0:1, :] * _shift_rows(xr, -2, n)
    for j in range(1, CONV_WIDTH):
        acc = acc + cv_ref[j:j + 1, :] * _shift_rows(xr, j - 2, n)
    return acc


def lru_forward(proj, cvec, pvec, wblk, lw, deps=(), ch=512):
    s_len = proj.shape[1]
    ncb = lw // LANE
    ch = min(ch, s_len)
    nchunk = s_len // ch

    def body(xr_ref, gt_ref, cv_ref, pv_ref, wb_ref, *rest):
        y_ref, hs_ref, xc_s, a_s, u_s, acum_s = rest[len(deps):]
        xc_s[...] = _conv_rows(xr_ref[...], cv_ref, pv_ref[0:1, :], s_len)
        for direction in range(2):
            def fill(ci, _):
                rows = pl.ds(pl.multiple_of(ci * ch, ch), ch)
                xc = xc_s[rows, :]
                _, _, i, a, mult, _ = _lru_gates(xc, wb_ref, pv_ref, direction)
                a_s[rows, :] = a
                u_s[rows, :] = mult * (i * xc)
                return 0

            lax.fori_loop(0, nchunk, fill, 0)
            _scan_rows(a_s, u_s, hs_ref.at[direction], acum_s, reverse=direction == 1)

        def out(ci, _):
            rows = pl.ds(pl.multiple_of(ci * ch, ch), ch)
            gl, _ = _gelu(gt_ref[rows, :])
            y_ref[rows, :] = gl * (hs_ref[0, rows, :] + hs_ref[1, rows, :])
            return 0

        lax.fori_loop(0, nchunk, out, 0)

    col = lambda off: pl.BlockSpec((None, s_len, LANE), lambda cb: (off + cb, 0, 0))
    return pl.pallas_call(
        body, name="lru_forward", grid=(ncb,),
        out_shape=(SDS((ncb, s_len, LANE), F32), SDS((2, ncb, s_len, LANE), F32)),
        in_specs=[col(0), col(ncb), pl.BlockSpec((8, LANE), lambda cb: (0, cb)), pl.BlockSpec((8, LANE), lambda cb: (0, cb)),
                  pl.BlockSpec((4, None, LANE, LANE), lambda cb: (0, cb, 0, 0))] + [ANY] * len(deps),
        out_specs=(col(0), pl.BlockSpec((2, None, s_len, LANE), lambda cb: (0, cb, 0, 0))),
        scratch_shapes=[pltpu.VMEM((s_len, LANE), F32)] * 4, compiler_params=_cparams(),
    )(proj, proj, cvec, pvec, wblk, *deps)


def lru_backward(proj, hs, dy, cvec, pvec, wblk, lw, ch=512):
    s_len = proj.shape[1]
    ncb = lw // LANE
    ch = min(ch, s_len)
    nchunk = s_len // ch

    def body(xr_ref, gt_ref, hs_ref, dy_ref, cv_ref, pv_ref, wb_ref, dxr_ref, dgt_ref, dcv_ref, dpv_ref, dwb_ref,
             xc_s, a_s, r_s, i_s, dh_s, lam_s, hp_s, dxc_s, acum_s):
        xr = xr_ref[...]
        xc_s[...] = _conv_rows(xr, cv_ref, pv_ref[0:1, :], s_len)
        dxc_s[...] = jnp.zeros_like(dxc_s)
        dpv_ref[...] = jnp.zeros_like(dpv_ref)
        dwb_ref[...] = jnp.zeros_like(dwb_ref)

        def head(ci, _):
            rows = pl.ds(pl.multiple_of(ci * ch, ch), ch)
            gt = gt_ref[rows, :]
            gl, t = _gelu(gt)
            dy = dy_ref[rows, :]
            dh_s[rows, :] = dy * gl
            dgt_ref[rows, :] = dy * (hs_ref[0, rows, :] + hs_ref[1, rows, :]) * _gelu_grad(gt, t)
            return 0

        lax.fori_loop(0, nchunk, head, 0)

        for direction in range(2):
            def fill(ci, _):
                rows = pl.ds(pl.multiple_of(ci * ch, ch), ch)
                _, r, i, a, _, _ = _lru_gates(xc_s[rows, :], wb_ref, pv_ref, direction)
                a_s[rows, :] = a
                r_s[rows, :] = r
                i_s[rows, :] = i
                return 0

            lax.fori_loop(0, nchunk, fill, 0)
            toward = 1 if direction == 0 else -1
            hp_s[...] = _shift_rows(a_s[...], toward, s_len)
            _scan_rows(hp_s, dh_s, lam_s, acum_s, reverse=direction == 0)
            hp_s[...] = _shift_rows(hs_ref[direction], -toward, s_len)

            def grads(ci, _):
                rows = pl.ds(pl.multiple_of(ci * ch, ch), ch)
                xc = xc_s[rows, :]
                xcb = xc.astype(BF)
                r, i, a = r_s[rows, :], i_s[rows, :], a_s[rows, :]
                mult = jnp.sqrt(1.0 - a * a)
                lam = pv_ref[5 + direction:6 + direction, :]
                sp = jnp.maximum(-lam, 0.0) + jnp.log(1.0 + jnp.exp(-jnp.abs(lam)))
                du = lam_s[rows, :]
                da = du * hp_s[rows, :]
                dmult = du * i * xc
                di = du * mult * xc
                dlog_a = (da - dmult * a / mult) * a
                dr = dlog_a * (-LRU_C * sp)
                dza = dr * r * (1.0 - r)
                dzx = di * i * (1.0 - i)
                dzab = dza.astype(BF)
                dzxb = dzx.astype(BF)
                dxc_s[rows, :] += (du * mult * i + dot_nt(dzab, wb_ref[2 * direction])
                                   + dot_nt(dzxb, wb_ref[2 * direction + 1]))
                dwb_ref[2 * direction] += dot_tn(xcb, dzab)
                dwb_ref[2 * direction + 1] += dot_tn(xcb, dzxb)
                dpv_ref[1 + direction:2 + direction, :] += jnp.sum(dza, axis=0, keepdims=True)
                dpv_ref[3 + direction:4 + direction, :] += jnp.sum(dzx, axis=0, keepdims=True)
                dpv_ref[5 + direction:6 + direction, :] += jnp.sum(dlog_a * (-LRU_C * r), axis=0, keepdims=True)
                return 0

            lax.fori_loop(0, nchunk, grads, 0)

        for direction in range(2):
            lam = pv_ref[5 + direction:6 + direction, :]
            dpv_ref[5 + direction:6 + direction, :] = dpv_ref[5 + direction:6 + direction, :] * (-jax.nn.sigmoid(-lam))
        dxc = dxc_s[...]
        dpv_ref[0:1, :] = jnp.sum(dxc, axis=0, keepdims=True)
        dxr = cv_ref[0:1, :] * _shift_rows(dxc, 2, s_len)
        for j in range(1, CONV_WIDTH):
            dxr = dxr + cv_ref[j:j + 1, :] * _shift_rows(dxc, 2 - j, s_len)
        dxr_ref[...] = dxr
        dcv_ref[...] = jnp.zeros_like(dcv_ref)
        for j in range(CONV_WIDTH):
            dcv_ref[j:j + 1, :] = jnp.sum(dxc * _shift_rows(xr, j - 2, s_len), axis=0, keepdims=True)

    col = lambda off: pl.BlockSpec((None, s_len, LANE), lambda cb: (off + cb, 0, 0))
    own = col(0)
    small = pl.BlockSpec((8, LANE), lambda cb: (0, cb))
    wspec = pl.BlockSpec((4, None, LANE, LANE), lambda cb: (0, cb, 0, 0))
    return pl.pallas_call(
        body, name="lru_backward", grid=(ncb,),
        out_shape=(SDS((ncb, s_len, LANE), F32), SDS((ncb, s_len, LANE), F32), SDS((8, lw), F32), SDS((8, lw), F32),
                   SDS(wblk.shape, F32)),
        in_specs=[col(0), col(ncb), pl.BlockSpec((2, None, s_len, LANE), lambda cb: (0, cb, 0, 0)), own, small, small, wspec],
        out_specs=(own, own, small, small, wspec),
        scratch_shapes=[pltpu.VMEM((s_len, LANE), F32)] * 9, compiler_params=_cparams(),
    )(proj, proj, hs, dy, cvec, pvec, wblk)


def _window_specs(s_len, first, width=None):
    nb = s_len // BLOCK
    where = (lambda n: jnp.maximum(n - 1, 0), lambda n: n, lambda n: jnp.minimum(n + 1, nb - 1))
    if width is None:
        return [pl.BlockSpec((None, BLOCK, LANE), lambda n, f=f: (first, f(n), 0)) for f in where]
    return [pl.BlockSpec((width, BLOCK), lambda n, f=f: (0, f(n))) for f in where]


def _stack_heads(v, kh):
    return jnp.concatenate([v[(kh * KV_GROUP + g) * HEAD_DIM:(kh * KV_GROUP + g + 1) * HEAD_DIM, :]
                            for g in range(KV_GROUP)], axis=1)


def _unstack_heads(ref, kh, v):
    for g in range(KV_GROUP):
        h = kh * KV_GROUP + g
        ref[h * HEAD_DIM:(h + 1) * HEAD_DIM, :] = v[:, g * BLOCK:(g + 1) * BLOCK]


def _key_exists(n, nb):
    j = lax.broadcasted_iota(jnp.int32, (3 * BLOCK, 1), 0)
    return ((n > 0) | (j >= BLOCK)) & ((n < nb - 1) | (j < 2 * BLOCK))


def _attn_probs(qs, kcat, bias_g, sink_g, key_ok):
    logits = jnp.where(key_ok, dot_nn(kcat, qs) + bias_g, NEG_INF)
    m = jnp.maximum(jnp.max(logits, axis=0, keepdims=True), sink_g)
    p = jnp.exp(logits - m)
    es = jnp.exp(sink_g - m)
    inv = 1.0 / (jnp.sum(p, axis=0, keepdims=True) + es)
    return p * inv, es * inv


def attention_forward(qt, proj, vt, bias, sink, kblk):
    att, s_len = qt.shape
    kvw = vt.shape[0]
    nb = s_len // BLOCK

    def body(q_ref, kp_ref, kc_ref, kn_ref, vp_ref, vc_ref, vn_ref, b_ref, s_ref, o_ref):
        n = pl.program_id(0)
        q = q_ref[...]
        key_ok = _key_exists(n, nb)
        kall = jnp.concatenate([kp_ref[...], kc_ref[...], kn_ref[...]], axis=0).astype(BF)
        vall = jnp.concatenate([vp_ref[...], vc_ref[...], vn_ref[...]], axis=1).astype(BF)
        for kh in range(N_KV_HEADS):
            qs = (_stack_heads(q, kh) * (HEAD_DIM ** -0.5)).astype(BF)
            p, _ = _attn_probs(qs, kall[:, kh * HEAD_DIM:(kh + 1) * HEAD_DIM], b_ref[kh], s_ref[kh, 0:1, :], key_ok)
            _unstack_heads(o_ref, kh, dot_nn(vall[kh * HEAD_DIM:(kh + 1) * HEAD_DIM, :], p.astype(BF)))

    blk = pl.BlockSpec((att, BLOCK), lambda n: (0, n))
    return pl.pallas_call(
        body, name="attention_forward", grid=(nb,), out_shape=SDS((att, s_len), F32),
        in_specs=[blk] + _window_specs(s_len, kblk) + _window_specs(s_len, 0, kvw)
        + [pl.BlockSpec(bias.shape, lambda n: (0, 0, 0)), pl.BlockSpec(sink.shape, lambda n: (0, 0, 0))],
        out_specs=blk, compiler_params=_cparams(),
    )(qt, proj, proj, proj, vt, vt, vt, bias, sink)


def attention_backward(qt, proj, y_att, dy, bias, sink, kblk):
    att, s_len = qt.shape
    nb = s_len // BLOCK
    kvw = N_KV_HEADS * HEAD_DIM

    def body(q_ref, kp_ref, kc_ref, kn_ref, vp_ref, vc_ref, vn_ref, o_ref, do_ref, b_ref, s_ref,
             dq_ref, dkv_ref, db_ref, ds_ref):
        n = pl.program_id(0)

        @pl.when(n == 0)
        def _():
            dkv_ref[...] = jnp.zeros_like(dkv_ref)
            db_ref[...] = jnp.zeros_like(db_ref)
            ds_ref[...] = jnp.zeros_like(ds_ref)

        q = q_ref[...]
        o = o_ref[...]
        do = do_ref[...]
        kall = jnp.concatenate([kp_ref[...], kc_ref[...], kn_ref[...]], axis=0).astype(BF)
        vall = jnp.concatenate([vp_ref[...], vc_ref[...], vn_ref[...]], axis=0).astype(BF)
        key_ok = _key_exists(n, nb)
        dks, dvs = [], []
        for kh in range(N_KV_HEADS):
            kcat = kall[:, kh * HEAD_DIM:(kh + 1) * HEAD_DIM]
            vcat = vall[:, kh * HEAD_DIM:(kh + 1) * HEAD_DIM]
            qs = (_stack_heads(q, kh) * (HEAD_DIM ** -0.5)).astype(BF)
            p, ps = _attn_probs(qs, kcat, b_ref[kh], s_ref[kh, 0:1, :], key_ok)
            dos = _stack_heads(do, kh)
            dosb = dos.astype(BF)
            delta = jnp.sum(dos * _stack_heads(o, kh), axis=0, keepdims=True)
            dlog = p * (dot_nn(vcat, dosb) - delta)
            dlogb = dlog.astype(BF)
            db_ref[kh] += dlog
            ds_ref[kh] += jnp.broadcast_to(-ps * delta, ds_ref.shape[1:])
            _unstack_heads(dq_ref, kh, dot_tn(kcat, dlogb) * (HEAD_DIM ** -0.5))
            dks.append(dot_nt(dlogb, qs))
            dvs.append(dot_nt(p.astype(BF), dosb))
        dkv = jnp.concatenate(dks + dvs, axis=1)
        starts = [jnp.maximum(n - 1, 0), n, jnp.minimum(n + 1, nb - 1)]
        for b, st in enumerate(starts):
            rows = pl.ds(pl.multiple_of(st * BLOCK, BLOCK), BLOCK)
            dkv_ref[rows, :] += dkv[b * BLOCK:(b + 1) * BLOCK, :]

    blk = pl.BlockSpec((att, BLOCK), lambda n: (0, n))
    whole = lambda a: pl.BlockSpec(a.shape, lambda n: (0, 0, 0))
    return pl.pallas_call(
        body, name="attention_backward", grid=(nb,),
        out_shape=(SDS((att, s_len), F32), SDS((s_len, 2 * kvw), F32), SDS(bias.shape, F32), SDS(sink.shape, F32)),
        in_specs=[blk] + _window_specs(s_len, kblk) + _window_specs(s_len, kblk + 1) + [blk, blk, whole(bias), whole(sink)],
        out_specs=(blk, pl.BlockSpec((s_len, 2 * kvw), lambda n: (0, 0)), whole(bias), whole(sink)),
        compiler_params=_cparams(),
    )(qt, proj, proj, proj, proj, proj, proj, y_att, dy, bias, sink)


def _rms_cols(x, g):
    rs = lax.rsqrt(jnp.mean(x * x, axis=0, keepdims=True) + EPS)
    xh = x * rs
    return xh, rs, xh * g


def _rms_cols_bwd(dy, xh, rs, g):
    dxh = dy * g
    dx = rs * (dxh - xh * jnp.mean(dxh * xh, axis=0, keepdims=True))
    return dx, dy * xh


def mix_output(x, y_rec, y_att, g_rec, g_att, wfull, lay, tm=512):
    s_len, d = x.shape
    tm = min(tm, s_len)
    lw = y_rec.shape[0] * LANE
    att = y_att.shape[0]

    def body(x_ref, yr_ref, ya_ref, gr_ref, ga_ref, w_ref, o_ref):
        _, _, nr = _rms(_join_lane_blocks(yr_ref), gr_ref[...])
        _, _, na = _rms_cols(ya_ref[...], ga_ref[...])
        w = w_ref[:, :, lay.ih:, :].reshape(d, d)
        o_ref[...] = x_ref[...] + dot_nn(nr.astype(BF), w[:lw]) + dot_tn(na.astype(BF), w[lw:])

    row = pl.BlockSpec((tm, d), lambda i: (i, 0))
    return pl.pallas_call(
        body, name="mix_output", grid=(s_len // tm,), out_shape=SDS((s_len, d), F32),
        in_specs=[row, _cbm_spec(lw // LANE, tm), pl.BlockSpec((att, tm), lambda i: (0, i)),
                  pl.BlockSpec((1, lw), lambda i: (0, 0)), pl.BlockSpec((att, 1), lambda i: (0, 0)),
                  _w_spec(lay.fh, d, lay.MIX_BLK)],
        out_specs=row, compiler_params=_cparams(),
    )(x, y_rec, y_att, g_rec, g_att, wfull)


def mix_output_backward(dout, y_rec, y_att, g_rec, g_att, wfull, lay, deps=(), tm=1024):
    s_len, d = dout.shape
    tm = min(tm, s_len)
    lw = y_rec.shape[0] * LANE
    att = y_att.shape[0]
    nt = s_len // tm

    def body(do_ref, yr_ref, ya_ref, gr_ref, ga_ref, w_ref, *rest):
        dyr_ref, dya_ref, dgr_ref, dga_ref, o_ref, acc = rest[len(deps):]
        i = pl.program_id(0)
        gr = gr_ref[...]
        ga = ga_ref[...]
        xhr, rsr, nr = _rms(_join_lane_blocks(yr_ref), gr)
        xha, rsa, na = _rms_cols(ya_ref[...], ga)
        dob = do_ref[...].astype(BF)
        w = w_ref[:, :, lay.ih:, :].reshape(d, d)
        dyr, dgr_row = _rms_bwd(dot_nt(dob, w[:lw]), xhr, rsr, gr)
        dya, dga_col = _rms_cols_bwd(dot_nt(w[lw:], dob), xha, rsa, ga)
        for j, piece in enumerate(_lane_blocks(dyr)):
            dyr_ref[j] = piece
        dya_ref[...] = dya

        @pl.when(i == 0)
        def _():
            dgr_ref[...] = jnp.zeros_like(dgr_ref)
            dga_ref[...] = jnp.zeros_like(dga_ref)
            acc[...] = jnp.zeros_like(acc)

        dgr_ref[...] += jnp.sum(dgr_row, axis=0, keepdims=True)
        dga_ref[...] += jnp.sum(dga_col, axis=1, keepdims=True)
        acc[0:lw, :] += dot_tn(nr.astype(BF), dob)
        acc[lw:, :] += dot_nn(na.astype(BF), dob)

        @pl.when(i == nt - 1)
        def _():
            for p in range(N_CHIPS):
                for q in range(2):
                    o_ref[p, q] = acc[pl.ds((2 * p + q) * lay.oh, lay.oh), :].astype(o_ref.dtype)

    row = pl.BlockSpec((tm, d), lambda i: (i, 0))
    return pl.pallas_call(
        body, name="mix_output_backward", grid=(nt,),
        out_shape=(SDS(y_rec.shape, F32), SDS(y_att.shape, F32), SDS((1, lw), F32), SDS((att, 1), F32),
                   SDS((N_CHIPS, 2, lay.oh, d), BF)),
        in_specs=[row, _cbm_spec(lw // LANE, tm), pl.BlockSpec((att, tm), lambda i: (0, i)),
                  pl.BlockSpec((1, lw), lambda i: (0, 0)), pl.BlockSpec((att, 1), lambda i: (0, 0)),
                  _w_spec(lay.fh, d, lay.MIX_BLK)] + [ANY] * len(deps),
        out_specs=(_cbm_spec(lw // LANE, tm), pl.BlockSpec((att, tm), lambda i: (0, i)),
                   pl.BlockSpec((1, lw), lambda i: (0, 0)), pl.BlockSpec((att, 1), lambda i: (0, 0)),
                   pl.BlockSpec((N_CHIPS, 2, lay.oh, d), lambda i: (0, 0, 0, 0))),
        scratch_shapes=[pltpu.VMEM((d, d), F32)], compiler_params=_cparams(),
    )(dout, y_rec, y_att, g_rec, g_att, wfull, *deps)


def loss_head(x, gain, target, tm=512):
    s_len, d = x.shape
    tm = min(tm, s_len)

    def body(x_ref, g_ref, t_ref, dx_ref, dg_ref, loss_ref):
        g = g_ref[...]
        xh, rs, y = _rms(x_ref[...], g)
        err = y - t_ref[...]

        @pl.when(pl.program_id(0) == 0)
        def _():
            dg_ref[...] = jnp.zeros_like(dg_ref)
            loss_ref[...] = jnp.zeros_like(loss_ref)

        part = 0.5 * jnp.sum(jnp.mean(err * err, axis=-1, keepdims=True), axis=0, keepdims=True)
        loss_ref[...] += jnp.broadcast_to(part, loss_ref.shape)
        dx, dgrow = _rms_bwd(err * (1.0 / d), xh, rs, g)
        dx_ref[...] = dx
        dg_ref[...] += jnp.sum(dgrow, axis=0, keepdims=True)

    row = pl.BlockSpec((tm, d), lambda i: (i, 0))
    vec = pl.BlockSpec((1, d), lambda i: (0, 0))
    return pl.pallas_call(
        body, name="loss_head", grid=(s_len // tm,),
        out_shape=(SDS((s_len, d), F32), SDS((1, d), F32), SDS((8, LANE), F32)),
        in_specs=[row, vec, row], out_specs=(row, vec, pl.BlockSpec((8, LANE), lambda i: (0, 0))),
        compiler_params=_cparams(),
    )(x, gain, target)


def _adamw_update(w, g, m, v):
    m = ADAM_B1 * m + (1.0 - ADAM_B1) * g
    v = ADAM_B2 * v + (1.0 - ADAM_B2) * (g * g)
    m_hat = m / (1.0 - ADAM_B1 ** ADAM_STEP)
    v_hat = v / (1.0 - ADAM_B2 ** ADAM_STEP)
    return -ADAM_LR * (m_hat / (jnp.sqrt(v_hat) + ADAM_EPS) + ADAM_WD * w), m, v


def adamw(w, g, m, v, tr=512):
    rows, cols = w.shape
    tr = _row_chunk(rows, tr, 8)

    def body(w_ref, g_ref, m_ref, v_ref, d_ref, nm_ref, nv_ref):
        d_ref[...], nm_ref[...], nv_ref[...] = _adamw_update(w_ref[...], g_ref[...], m_ref[...], v_ref[...])

    blk = pl.BlockSpec((tr, cols), lambda i: (i, 0))
    return pl.pallas_call(
        body, name="adamw", grid=(rows // tr,), out_shape=(SDS(w.shape, F32),) * 3,
        in_specs=[blk] * 4, out_specs=(blk,) * 3, compiler_params=_cparams(),
    )(w, g, m, v)


def adamw_layer(gf, blk, row_off, n_half, l, w, m, v, outs, deps=()):
    fh = gf.shape[1] // Layout.BLOCKS
    d = gf.shape[2]
    nd = len(deps)

    def body(gf_ref, w_ref, m_ref, v_ref, *rest):
        g_ref, d_ref, nm_ref, nv_ref = rest[4 + nd:]
        g = gf_ref[row_off:row_off + n_half, :]
        g_ref[...] = g
        d_ref[...], nm_ref[...], nv_ref[...] = _adamw_update(w_ref[...], g, m_ref[...], v_ref[...])

    gspec = pl.BlockSpec((None, fh, d), lambda h: (h, blk, 0))
    wspec = pl.BlockSpec((None, n_half, d), lambda h: (l, h, 0))
    return pl.pallas_call(
        body, name="adamw_layer", grid=(2,), out_shape=tuple(SDS(o.shape, o.dtype) for o in outs),
        in_specs=[gspec, wspec, wspec, wspec] + [ANY] * (4 + nd), out_specs=(wspec,) * 4,
        input_output_aliases={4 + i: i for i in range(4)}, compiler_params=_cparams(),
    )(gf, w, m, v, *outs, *deps)


def pack_weight(pos, land, blk, l, w, extra=None, deps=()):
    fh, d = land.shape[2] // Layout.BLOCKS, land.shape[3]
    nd = len(deps)

    def body(pos_ref, w_ref, *rest):
        o_ref = rest[-1]
        a = w_ref[...].astype(BF)
        n = a.shape[0] // 2
        for h in range(2):
            o_ref[h, 0:n, :] = a[h * n:(h + 1) * n]
        if extra is not None:
            b = rest[0][...].astype(BF)
            nb = b.shape[0] // 2
            for h in range(2):
                o_ref[h, n:n + nb, :] = b[h * nb:(h + 1) * nb]

    def whole(a):
        return pl.BlockSpec((None,) + a.shape[1:], lambda i, p: (l, 0, 0))

    ins = [w] + ([extra] if extra is not None else [])
    return pl.pallas_call(
        body, name="pack_weight", out_shape=SDS(land.shape, land.dtype),
        grid_spec=pltpu.PrefetchScalarGridSpec(
            num_scalar_prefetch=1, grid=(1,),
            in_specs=[whole(a) for a in ins] + [ANY] * (1 + nd),
            out_specs=pl.BlockSpec((None, 2, fh, d), lambda i, p: (p[0], 0, blk, 0))),
        input_output_aliases={1 + len(ins): 0}, compiler_params=_cparams(),
    )(pos, *ins, land, *deps)


def _rows_of(shape, width):
    return -(-int(np.prod(shape)) // (SUBLANES * width)) * SUBLANES


def _pack_rows(arrays, width):
    parts = []
    for a in arrays:
        flat = a.reshape(-1).astype(F32)
        r = _rows_of(a.shape, width)
        parts.append(jnp.pad(flat, (0, r * width - flat.shape[0])).reshape(r, width))
    return jnp.concatenate(parts, axis=0)


def _unpack_rows(buf, shapes):
    out, row = [], 0
    for shp in shapes:
        r = _rows_of(shp, buf.shape[1])
        out.append(buf[row:row + r].reshape(-1)[:int(np.prod(shp))].reshape(shp))
        row += r
    return out


def _t5_buckets(rel):
    half = N_BUCKETS // 2
    max_exact = half // 2
    ret = (rel > 0).astype(jnp.int32) * half
    n = jnp.abs(rel)
    n_f = jnp.maximum(n, 1).astype(F32)
    large = max_exact + (jnp.log(n_f / max_exact) / math.log(MAX_DISTANCE / max_exact) * (half - max_exact)).astype(jnp.int32)
    large = jnp.minimum(large, half - 1)
    return ret + jnp.where(n < max_exact, n, large)


def _band_buckets():
    t = jnp.arange(BLOCK)[:, None]
    j = jnp.arange(3 * BLOCK)[None, :]
    rel = j - BLOCK - t
    return _t5_buckets(rel), jnp.abs(rel) <= WINDOW


def _block_diag_pairs(w):
    depth, two, nblk, bw, _ = w.shape
    pairs = w.reshape(depth, two, nblk // 2, 2, bw, bw)
    z = jnp.zeros_like(pairs[:, :, :, 0])
    top = jnp.concatenate([pairs[:, :, :, 0], z], axis=-1)
    bot = jnp.concatenate([z, pairs[:, :, :, 1]], axis=-1)
    return jnp.concatenate([top, bot], axis=-2)


def _diag_blocks(dw):
    bw = dw.shape[-1] // 2
    a = dw[:, :, :bw, :bw]
    b = dw[:, :, bw:, bw:]
    return jnp.stack([a, b], axis=2).reshape(dw.shape[0], 2 * dw.shape[1], bw, bw)


def kernel(x, ffn1_norm, ffn1_w_gate, ffn1_w_up, ffn1_w_down, mix_norm, w_in, conv_w, conv_b, lru_w_a, lru_b_a, lru_w_x, lru_b_x, lru_lambda, attn_sink, rel_bias, lru_out_norm, attn_out_norm, w_out, ffn2_norm, ffn2_w_gate, ffn2_w_up, ffn2_w_down, final_norm, loss_target, m_ffn1_norm, m_ffn1_w_gate, m_ffn1_w_up, m_ffn1_w_down, m_mix_norm, m_w_in, m_conv_w, m_conv_b, m_lru_w_a, m_lru_b_a, m_lru_w_x, m_lru_b_x, m_lru_lambda, m_attn_sink, m_rel_bias, m_lru_out_norm, m_attn_out_norm, m_w_out, m_ffn2_norm, m_ffn2_w_gate, m_ffn2_w_up, m_ffn2_w_down, m_final_norm, v_ffn1_norm, v_ffn1_w_gate, v_ffn1_w_up, v_ffn1_w_down, v_mix_norm, v_w_in, v_conv_w, v_conv_b, v_lru_w_a, v_lru_b_a, v_lru_w_x, v_lru_b_x, v_lru_lambda, v_attn_sink, v_rel_bias, v_lru_out_norm, v_attn_out_norm, v_w_out, v_ffn2_norm, v_ffn2_w_gate, v_ffn2_w_up, v_ffn2_w_down, v_final_norm):
    depth, d = ffn1_norm.shape
    d_ff = N_CHIPS * ffn1_w_gate.shape[2]
    d_in = N_CHIPS * w_in.shape[2]
    lw = conv_b.shape[1]
    att = N_HEADS * HEAD_DIM
    lay = Layout(d, d_ff, d_in)
    k_chip = 2 * lax.axis_index("x") + lax.axis_index("y")
    pos = jnp.stack([k_chip, lax.axis_index("c")]).astype(jnp.int32)

    def rows_major(a):
        return jnp.swapaxes(a, 1, 2)

    mats = (rows_major(ffn1_w_gate), rows_major(ffn1_w_up), ffn1_w_down,
            rows_major(ffn2_w_gate), rows_major(ffn2_w_up), ffn2_w_down)

    def pack_layer(l, deps=()):
        land = lax.empty((N_CHIPS, 2, lay.rows, d), BF)
        for m, a in enumerate(mats):
            land = pack_weight(pos, land, m, l, a, deps=deps if m == 0 else ())
        return pack_weight(pos, land, lay.MIX_BLK, l, rows_major(w_in), extra=w_out)

    def gather_start(l, land):
        return split_start(f"gather_start_{l}", [land], 3, gather_plan)

    def gather_wait(l, started, after):
        ssem, rsem, bufs, _ = started
        return split_wait(f"gather_wait_{l}", ssem, rsem, bufs, after, gather_plan)

    sharded_small = (conv_w, lru_b_a, lru_b_x, lru_lambda)
    sshard = jnp.concatenate([a.reshape(-1, LANE) for a in sharded_small], axis=0)
    sfull = gather_small(sshard)
    small_full, off = [], 0
    for a in sharded_small:
        r = a.shape[0] * a.shape[1]
        piece = sfull[:, off:off + r].reshape((N_CHIPS,) + a.shape)
        small_full.append(jnp.moveaxis(piece, 0, 2).reshape(a.shape[0], a.shape[1], N_CHIPS * LANE))
        off += r
    conv_w_f, b_a_f, b_x_f, lam_f = small_full

    zrow = jnp.zeros((1, lw), F32)
    wblk_a = _block_diag_pairs(lru_w_a)
    wblk_x = _block_diag_pairs(lru_w_x)
    buckets, in_band = _band_buckets()
    onehot = (buckets.reshape(-1)[:, None] == jnp.arange(N_BUCKETS)[None, :]).astype(F32)
    bias = jnp.dot(rel_bias.T, onehot.T, precision=lax.Precision.HIGHEST).reshape(N_HEADS, BLOCK, 3 * BLOCK)
    bias = jnp.where(in_band[None], bias, NEG_INF)
    bias = bias.reshape(N_KV_HEADS, KV_GROUP, BLOCK, 3 * BLOCK).transpose(0, 3, 1, 2).reshape(N_KV_HEADS, 3 * BLOCK, KV_GROUP * BLOCK)
    kblk = 2 * lw // LANE

    def layer_small(l):
        cvec = jnp.concatenate([conv_w_f[l], jnp.zeros((8 - CONV_WIDTH, lw), F32)], axis=0)
        pvec = jnp.concatenate([conv_b[l][None], b_a_f[l], b_x_f[l], lam_f[l], zrow], axis=0)
        wblk = jnp.stack([wblk_a[l, 0], wblk_x[l, 0], wblk_a[l, 1], wblk_x[l, 1]]).astype(BF)
        sink = jnp.broadcast_to(jnp.repeat(attn_sink[l], BLOCK).reshape(N_KV_HEADS, 1, KV_GROUP * BLOCK),
                                (N_KV_HEADS, 8, KV_GROUP * BLOCK))
        return cvec, pvec, wblk, sink

    xs = x[0]
    wfull = [None] * depth
    parts = [(0, 3 * lay.fh), (3 * lay.fh, lay.rows - 3 * lay.fh)]
    plans = [(functools.partial(gather_plan, rows=p), functools.partial(handover_plan, rows=p)) for p in parts]
    land = pack_layer(0, deps=(sfull,))
    first = split_start("gather_start_0a", [land], 3, plans[0][0])
    second = split_start("gather_start_0b", first[2], 3, plans[1][0])
    lands = {l: pack_layer(l, deps=(second[3],)) for l in range(1, depth)}
    land = split_wait("gather_wait_0a", first[0], first[1], second[2], [xs] + list(lands.values()), plans[0][0])
    wfull[0], = exchange_now("gather_handover_0a", land, 3, plans[0][1])
    started = None
    saved = []
    for l in range(depth):
        cvec, pvec, wblk, sink = layer_small(l)
        deps = (started[3],) if started is not None else ()
        x1, gate1, up1 = ffn_forward(xs, ffn1_norm[l][None], wfull[l], lay, 0, deps=deps)
        deps = ()
        if l == 0:
            land = split_wait("gather_wait_0b", second[0], second[1], [wfull[0]], [x1], plans[1][0])
            wfull[0], = exchange_now("gather_handover_0b", land, 3, plans[1][1])
            if depth > 1:
                started = gather_start(1, lands[1])
                deps = (started[3],)
        proj, qt, vt = mix_project(x1, mix_norm[l][None], wfull[l], lay, lw, att)
        y_rec, hs = lru_forward(proj, cvec, pvec, wblk, lw, deps=deps)
        y_att = attention_forward(qt, proj, vt, bias, sink, kblk)
        x2 = mix_output(x1, y_rec, y_att, lru_out_norm[l][None], attn_out_norm[l][:, None], wfull[l], lay)
        deps, handover = (), None
        if 0 < l < depth - 1:
            land, = gather_wait(l + 1, started, [x2])
            started = gather_start(l + 2, lands[l + 2]) if l + 2 < depth else None
            handover = split_start(f"gather_handover_start_{l + 1}", [land], 3, handover_plan)
            deps = (handover[3],) + ((started[3],) if started is not None else ())
        x3, gate2, up2 = ffn_forward(x2, ffn2_norm[l][None], wfull[l], lay, 1, deps=deps)
        saved.append((xs, x1, x2, proj, qt, y_rec, hs, y_att, (gate1, up1), (gate2, up2)))
        xs = x3
        if handover is not None:
            wfull[l + 1], = split_wait(f"gather_handover_wait_{l + 1}", handover[0], handover[1], handover[2], [x3],
                                       handover_plan)
        elif l == 0 and depth > 1:
            wfull[1], = exchange_now("gather_handover_1", gather_wait(1, started, [x3]), 3, handover_plan)
            started = gather_start(2, lands[2]) if depth > 2 else None

    dx, d_final, loss_tile = loss_head(xs, final_norm[None], loss_target[0])
    loss = lax.psum(loss_tile[0, 0], ("x", "y", "c"))

    layer_names = ["ffn1_norm", "mix_norm", "conv_w", "conv_b", "lru_w_a", "lru_b_a", "lru_w_x", "lru_b_x", "lru_lambda",
                   "attn_sink", "lru_out_norm", "attn_out_norm", "ffn2_norm"]
    dbias_total = jnp.zeros(bias.shape, F32)

    def ffn_back(xin, gain, dout, pre, gb, l, which, deps=()):
        dxo, dg, lhs, rhs = ffn_backward_dx(xin, gain, dout, *pre, wfull[l], lay, which, deps=deps)
        return dxo, dg[0], weight_grad_tn(lhs, rhs, gb, lay, 3 * which)

    def pair_start(l, gb, sb):
        lands = [lax.empty((N_CHIPS,) + gb.shape[2:], gb.dtype), lax.empty(sb.shape, sb.dtype)]
        return split_start(f"pair_start_{l}", [gb, sb] + lands, N_CHIPS + 1, pair_plan)

    def reduce_start(l, paired, after):
        gb, sb, p1, sp1 = split_wait(f"pair_wait_{l}", paired[0], paired[1], paired[2], after, pair_plan)
        cs = pair_sum(pos, gb, p1)
        ss = small_pair_sum(sb, sp1)
        lands = [lax.empty((3,) + cs.shape[1:], cs.dtype), lax.empty((N_CHIPS,) + ss.shape, ss.dtype)]
        return split_start(f"reduce_start_{l}", [cs, ss] + lands, 6, reduce_plan)

    def reduce_finish(l, started, after):
        ssem, rsem, bufs, _ = started
        cs, ss, p3, sp3 = split_wait(f"reduce_wait_{l}", ssem, rsem, bufs, after, reduce_plan)
        return chip_sum(pos, cs, p3), small_chip_sum(pos, ss, sp3)

    gf = [None] * depth
    small_sums = [None] * depth
    small_shapes = [None] * depth
    paired = None
    in_flight = None
    finals = {}
    tokens = []
    for l in reversed(range(depth)):
        x0, x1, x2, proj, qt, y_rec, hs, y_att, pre1, pre2 = saved[l]
        cvec, pvec, wblk, sink = layer_small(l)
        gb = lax.empty((N_CHIPS, 2, lay.rows, d), BF)
        part = {}
        dx, part["ffn2_norm"], gb = ffn_back(x2, ffn2_norm[l][None], dx, pre2, gb, l, 1, deps=tuple(tokens))
        deps = ()
        if paired is not None:
            in_flight = (paired[0], reduce_start(paired[0], paired[1], [dx, gb]))
            deps = (in_flight[1][3],)
        dyr, dya, dgr, dga, dwout = mix_output_backward(dx, y_rec, y_att, lru_out_norm[l][None], attn_out_norm[l][:, None],
                                                        wfull[l], lay, deps=deps)
        part["lru_out_norm"] = dgr[0]
        part["attn_out_norm"] = dga[:, 0]
        dq, dkv, dbias, dsink = attention_backward(qt, proj, y_att, dya, bias, sink, kblk)
        dbias_total = dbias_total + dbias
        part["attn_sink"] = jnp.sum(dsink[:, 0, :].reshape(N_HEADS, BLOCK), axis=1)
        dxr, dgt, dcv, dpv, dwb = lru_backward(proj, hs, dyr, cvec, pvec, wblk, lw)
        part["conv_w"] = dcv[:CONV_WIDTH]
        part["conv_b"] = dpv[0]
        part["lru_b_a"] = dpv[1:3]
        part["lru_b_x"] = dpv[3:5]
        part["lru_lambda"] = dpv[5:7]
        part["lru_w_a"] = _diag_blocks(jnp.stack([dwb[0], dwb[2]]))
        part["lru_w_x"] = _diag_blocks(jnp.stack([dwb[1], dwb[3]]))
        dx, dgm, gb = mix_project_backward(x1, mix_norm[l][None], dx, dxr, dgt, dq, dkv, dwout, wfull[l], gb, lay)
        part["mix_norm"] = dgm[0]
        dx, part["ffn1_norm"], gb = ffn_back(x0, ffn1_norm[l][None], dx, pre1, gb, l, 0)
        pieces = [part[n] for n in layer_names]
        if l == 0:
            dbias_heads = dbias_total.reshape(N_KV_HEADS, 3 * BLOCK, KV_GROUP, BLOCK).transpose(0, 2, 3, 1)
            d_rel_bias = jnp.dot(dbias_heads.reshape(N_HEADS, -1), onehot, precision=lax.Precision.HIGHEST).T
            pieces += [d_rel_bias, d_final[0]]
        small_shapes[l] = [p.shape for p in pieces]
        paired = (l, pair_start(l, gb, _pack_rows(pieces, 1024)))
        tokens = [paired[1][3]]
        if in_flight is not None:
            above = in_flight[0]
            half, small_sums[above] = reduce_finish(above, in_flight[1], [dx])
            finals[above] = split_start(f"final_start_{above}", [half], 1, final_plan)
            tokens.append(finals[above][3])
            in_flight = None
    grad_x = dx[None]

    weights = dict(ffn1_norm=ffn1_norm, ffn1_w_gate=ffn1_w_gate, ffn1_w_up=ffn1_w_up, ffn1_w_down=ffn1_w_down, mix_norm=mix_norm, w_in=w_in, conv_w=conv_w, conv_b=conv_b, lru_w_a=lru_w_a, lru_b_a=lru_b_a, lru_w_x=lru_w_x, lru_b_x=lru_b_x, lru_lambda=lru_lambda, attn_sink=attn_sink, rel_bias=rel_bias, lru_out_norm=lru_out_norm, attn_out_norm=attn_out_norm, w_out=w_out, ffn2_norm=ffn2_norm, ffn2_w_gate=ffn2_w_gate, ffn2_w_up=ffn2_w_up, ffn2_w_down=ffn2_w_down, final_norm=final_norm)
    m_in = dict(ffn1_norm=m_ffn1_norm, ffn1_w_gate=m_ffn1_w_gate, ffn1_w_up=m_ffn1_w_up, ffn1_w_down=m_ffn1_w_down, mix_norm=m_mix_norm, w_in=m_w_in, conv_w=m_conv_w, conv_b=m_conv_b, lru_w_a=m_lru_w_a, lru_b_a=m_lru_b_a, lru_w_x=m_lru_w_x, lru_b_x=m_lru_b_x, lru_lambda=m_lru_lambda, attn_sink=m_attn_sink, rel_bias=m_rel_bias, lru_out_norm=m_lru_out_norm, attn_out_norm=m_attn_out_norm, w_out=m_w_out, ffn2_norm=m_ffn2_norm, ffn2_w_gate=m_ffn2_w_gate, ffn2_w_up=m_ffn2_w_up, ffn2_w_down=m_ffn2_w_down, final_norm=m_final_norm)
    v_in = dict(ffn1_norm=v_ffn1_norm, ffn1_w_gate=v_ffn1_w_gate, ffn1_w_up=v_ffn1_w_up, ffn1_w_down=v_ffn1_w_down, mix_norm=v_mix_norm, w_in=v_w_in, conv_w=v_conv_w, conv_b=v_conv_b, lru_w_a=v_lru_w_a, lru_b_a=v_lru_b_a, lru_w_x=v_lru_w_x, lru_b_x=v_lru_b_x, lru_lambda=v_lru_lambda, attn_sink=v_attn_sink, rel_bias=v_rel_bias, lru_out_norm=v_lru_out_norm, attn_out_norm=v_attn_out_norm, w_out=v_w_out, ffn2_norm=v_ffn2_norm, ffn2_w_gate=v_ffn2_w_gate, ffn2_w_up=v_ffn2_w_up, ffn2_w_down=v_ffn2_w_down, final_norm=v_final_norm)
    order = list(weights)
    large = [(name, m, 0, lay.fh, m % 3 != 2) for m, name in
             enumerate(("ffn1_w_gate", "ffn1_w_up", "ffn1_w_down", "ffn2_w_gate", "ffn2_w_up", "ffn2_w_down"))]
    large += [("w_in", lay.MIX_BLK, 0, lay.ih, True), ("w_out", lay.MIX_BLK, lay.ih, lay.oh, False)]
    as_rows = {name: [rows_major(src[name]) if flip else src[name] for src in (weights, m_in, v_in)]
               for name, _, _, _, flip in large}
    stacked = {name: tuple(lax.empty(as_rows[name][0].shape, F32) for _ in range(4)) for name, *_ in large}

    def adamw_large(l, deps=()):
        for i, (name, blk, row_off, n_half, _) in enumerate(large):
            stacked[name] = adamw_layer(gf[l], blk, row_off, n_half, l, *as_rows[name], stacked[name],
                                        deps=deps if i == 0 else ())

    last = paired[0]
    crossing = reduce_start(last, paired[1], [dx])
    for l in sorted(finals):
        gf[l], = split_wait(f"final_wait_{l}", finals[l][0], finals[l][1], finals[l][2], [crossing[3]], final_plan)
        adamw_large(l, deps=(crossing[3],))
    ready = [buf for name, *_ in large for buf in stacked[name]] if depth > 1 else []
    half, small_sums[last] = reduce_finish(last, crossing, [dx] + ready)
    gf[last], = exchange_now(f"final_now_{last}", [half], 1, final_plan)
    adamw_large(last)

    per_layer = [_unpack_rows(small_sums[l], small_shapes[l]) for l in range(depth)]
    grads = {n: jnp.stack([per_layer[l][i] for l in range(depth)]) for i, n in enumerate(layer_names)}
    grads["rel_bias"], grads["final_norm"] = per_layer[0][len(layer_names):]
    for name in ("conv_w", "lru_b_a", "lru_b_x", "lru_lambda"):
        grads[name] = lax.dynamic_slice_in_dim(grads[name], k_chip * LANE, LANE, axis=2)
    delta, new_m, new_v = {}, {}, {}
    for name, _, _, _, flip in large:
        grads[name], delta[name], new_m[name], new_v[name] = [rows_major(a) if flip else a for a in stacked[name]]
    small = [n for n in order if n not in stacked]
    packed = [_pack_rows([src[n] for n in small], 1024) for src in (weights, grads, m_in, v_in)]
    outs = adamw(*packed)
    shapes = [weights[n].shape for n in small]
    for dst, buf in zip((delta, new_m, new_v), outs):
        dst.update(zip(small, _unpack_rows(buf, shapes)))

    return (loss, grad_x, *[grads[n] for n in order], *[delta[n] for n in order],
            *[new_m[n] for n in order], *[new_v[n] for n in order])
```

```python
import functools
import math

import jax
import jax.numpy as jnp
import numpy as np
from jax import lax
from jax.experimental import pallas as pl
from jax.experimental.pallas import tpu as pltpu

BF = jnp.bfloat16
F32 = jnp.float32
SDS = jax.ShapeDtypeStruct
MESH = pl.DeviceIdType.MESH
ANY = pl.BlockSpec(memory_space=pl.ANY)

N_CHIPS = 4
N_HEADS = 8
N_KV_HEADS = 2
KV_GROUP = N_HEADS // N_KV_HEADS
HEAD_DIM = 64
BLOCK = 128
WINDOW = 128
N_BUCKETS = 32
MAX_DISTANCE = 128
LRU_C = 8.0
CONV_WIDTH = 4
LANE = 128
SUBLANES = 8
MXU_WIDTH = 256
SCAN_CHAINS = 8
EPS = 1e-6
FFN_RES = 0.5
NEG_INF = -1e30
ADAM_LR = 0.001
ADAM_B1 = 0.9
ADAM_B2 = 0.999
ADAM_EPS = 1e-08
ADAM_WD = 0.01
ADAM_STEP = 10
VMEM_LIMIT = 60000 * 1024
GELU_C = math.sqrt(2.0 / math.pi)


def dot_nn(a, b):
    return lax.dot_general(a, b, (((1,), (0,)), ((), ())), preferred_element_type=F32)


def dot_nt(a, b):
    return lax.dot_general(a, b, (((1,), (1,)), ((), ())), preferred_element_type=F32)


def dot_tn(a, b):
    return lax.dot_general(a, b, (((0,), (0,)), ((), ())), preferred_element_type=F32)


def _cparams(**kw):
    return pltpu.CompilerParams(vmem_limit_bytes=VMEM_LIMIT, **kw)


class Layout:
    MIX_BLK = 6
    BLOCKS = 7

    def __init__(self, d_model, d_ff, d_in):
        self.fh = d_ff // (2 * N_CHIPS)
        self.ih = d_in // (2 * N_CHIPS)
        self.oh = d_model // (2 * N_CHIPS)
        assert self.ih + self.oh == self.fh, "w_in^T and w_out rows must fill one ffn-sized block"
        self.rows = self.BLOCKS * self.fh


def _row_chunk(rows, target, step=16):
    best = rows
    for c in range(step, min(rows, target) + 1, step):
        if rows % c == 0:
            best = c
    return best


def _mesh_pos():
    return lax.axis_index("x"), lax.axis_index("y"), lax.axis_index("c")


def _rcopy(src, dst, ssem, rsem, dev):
    return pltpu.make_async_remote_copy(src_ref=src, dst_ref=dst, send_sem=ssem, recv_sem=rsem,
                                        device_id=dev, device_id_type=MESH)


HBM = pl.BlockSpec(memory_space=pltpu.HBM)
SEM = pl.BlockSpec(memory_space=pltpu.SEMAPHORE)
DATAFLOW = pltpu.SideEffectType.DATAFLOW_SIDE_EFFECTING


def _chip_peers():
    x, y, c = _mesh_pos()
    peers = [(1 - x, y), (x, 1 - y), (1 - x, 1 - y)]
    return x, y, c, 2 * x + y, [(px, py, 2 * px + py) for px, py in peers]


def split_start(name, bufs, n, plan):
    nb = len(bufs)

    def body(*refs):
        sends, _ = plan(refs[:nb], refs[nb], refs[nb + 1])
        for cp in sends:
            cp.start()
        refs[-1][...] = jnp.zeros_like(refs[-1])

    out = pl.pallas_call(
        body, name=name,
        out_shape=(pltpu.SemaphoreType.DMA((n,)), pltpu.SemaphoreType.DMA((n,)),
                   *[pltpu.HBM(b.shape, b.dtype) for b in bufs], SDS((8, LANE), F32)),
        in_specs=[HBM] * nb, out_specs=(SEM, SEM, *([HBM] * nb), pl.BlockSpec(memory_space=pltpu.VMEM)),
        input_output_aliases={i: 2 + i for i in range(nb)},
        compiler_params=pltpu.CompilerParams(has_side_effects=DATAFLOW),
    )(*[pltpu.with_memory_space_constraint(b, pltpu.HBM) for b in bufs])
    return out[0], out[1], list(out[2:2 + nb]), out[-1]


def split_wait(name, ssem, rsem, bufs, after, plan):
    nb = len(bufs)

    def body(*refs):
        sends, recvs = plan(refs[:nb], refs[nb], refs[nb + 1])
        for cp in recvs:
            cp.wait_recv()
        for cp in sends:
            cp.wait_send()

    out = pl.pallas_call(
        body, name=name, out_shape=tuple(pltpu.HBM(b.shape, b.dtype) for b in bufs),
        in_specs=[HBM] * nb + [SEM, SEM] + [ANY] * len(after), out_specs=tuple([HBM] * nb),
        input_output_aliases={i: i for i in range(nb)},
        compiler_params=pltpu.CompilerParams(has_side_effects=DATAFLOW),
    )(*bufs, ssem, rsem, *after)
    return list(out)


def gather_plan(refs, ssem, rsem, rows=None):
    land_ref, = refs
    _, _, c, k, peers = _chip_peers()
    part = (lambda a: a) if rows is None else (lambda a: a.at[pl.ds(rows[0], rows[1])])
    sends = [_rcopy(part(land_ref.at[k, c]), part(land_ref.at[k, c]), ssem.at[j], rsem.at[j], (px, py, c))
             for j, (px, py, _) in enumerate(peers)]
    recvs = [_rcopy(part(land_ref.at[kp, c]), part(land_ref.at[kp, c]), ssem.at[j], rsem.at[j], (px, py, c))
             for j, (px, py, kp) in enumerate(peers)]
    return sends, recvs


def reduce_plan(refs, ssem, rsem):
    cs_ref, ss_ref, p3_ref, sp3_ref = refs
    _, _, c, k, peers = _chip_peers()
    sends, recvs = [], []
    for j, (px, py, kp) in enumerate(peers):
        sends.append(_rcopy(cs_ref.at[kp], p3_ref.at[j], ssem.at[j], rsem.at[j], (px, py, c)))
        recvs.append(_rcopy(cs_ref.at[kp], p3_ref.at[j], ssem.at[j], rsem.at[j], (px, py, c)))
        sends.append(_rcopy(ss_ref, sp3_ref.at[k], ssem.at[3 + j], rsem.at[3 + j], (px, py, c)))
        recvs.append(_rcopy(ss_ref, sp3_ref.at[kp], ssem.at[3 + j], rsem.at[3 + j], (px, py, c)))
    return sends, recvs


def gather_small(sshard):
    def body(s_ref, sf_ref, lsem, ssem, rsem):
        _, _, c, k, peers = _chip_peers()
        own = pltpu.make_async_copy(s_ref, sf_ref.at[k], lsem)
        own.start()
        sends = [_rcopy(s_ref, sf_ref.at[k], ssem.at[j], rsem.at[j], (px, py, c)) for j, (px, py, _) in enumerate(peers)]
        recvs = [_rcopy(s_ref, sf_ref.at[kp], ssem.at[j], rsem.at[j], (px, py, c)) for j, (px, py, kp) in enumerate(peers)]
        for cp in sends:
            cp.start()
        for cp in recvs:
            cp.wait_recv()
        for cp in sends:
            cp.wait_send()
        own.wait()

    return pl.pallas_call(
        body, name="gather_small", out_shape=SDS((N_CHIPS,) + sshard.shape, sshard.dtype),
        in_specs=[ANY], out_specs=ANY,
        scratch_shapes=[pltpu.SemaphoreType.DMA, pltpu.SemaphoreType.DMA((3,)), pltpu.SemaphoreType.DMA((3,))],
    )(sshard)


def exchange_now(name, bufs, n, plan):
    nb = len(bufs)

    def body(*refs):
        sends, recvs = plan(refs[nb:2 * nb], refs[2 * nb], refs[2 * nb + 1])
        for cp in sends:
            cp.start()
        for cp in recvs:
            cp.wait_recv()
        for cp in sends:
            cp.wait_send()

    return list(pl.pallas_call(
        body, name=name, out_shape=tuple(SDS(b.shape, b.dtype) for b in bufs),
        in_specs=[ANY] * nb, out_specs=tuple([ANY] * nb), input_output_aliases={i: i for i in range(nb)},
        scratch_shapes=[pltpu.SemaphoreType.DMA((n,)), pltpu.SemaphoreType.DMA((n,))],
    )(*bufs))


def handover_plan(refs, ssem, rsem, rows=None):
    land_ref, = refs
    x, y, c, _, peers = _chip_peers()
    sib = (x, y, 1 - c)
    part = (lambda a: a) if rows is None else (lambda a: a.at[pl.ds(rows[0], rows[1])])
    sends = [_rcopy(part(land_ref.at[kp, c]), part(land_ref.at[kp, c]), ssem.at[j], rsem.at[j], sib)
             for j, (_, _, kp) in enumerate(peers)]
    recvs = [_rcopy(part(land_ref.at[kp, 1 - c]), part(land_ref.at[kp, 1 - c]), ssem.at[j], rsem.at[j], sib)
             for j, (_, _, kp) in enumerate(peers)]
    return sends, recvs


def pair_plan(refs, ssem, rsem):
    gb_ref, sb_ref, p_ref, sp_ref = refs
    x, y, c = _mesh_pos()
    sib = (x, y, 1 - c)
    n = gb_ref.shape[0]
    copies = [_rcopy(gb_ref.at[kk, 1 - c], p_ref.at[kk], ssem.at[kk], rsem.at[kk], sib) for kk in range(n)]
    copies.append(_rcopy(sb_ref, sp_ref, ssem.at[n], rsem.at[n], sib))
    return copies, copies


def final_plan(refs, ssem, rsem):
    gf_ref, = refs
    x, y, c = _mesh_pos()
    sib = (x, y, 1 - c)
    return ([_rcopy(gf_ref.at[c], gf_ref.at[c], ssem.at[0], rsem.at[0], sib)],
            [_rcopy(gf_ref.at[1 - c], gf_ref.at[1 - c], ssem.at[0], rsem.at[0], sib)])


def pair_sum(pos, gb, p1):
    n, _, rh, d = gb.shape
    cr = _row_chunk(rh, 1280)

    def body(pos_ref, a_ref, b_ref, o_ref):
        o_ref[...] = (a_ref[...].astype(F32) + b_ref[...].astype(F32)).astype(o_ref.dtype)

    return pl.pallas_call(
        body, name="pair_sum", out_shape=SDS((n, rh, d), gb.dtype),
        grid_spec=pltpu.PrefetchScalarGridSpec(
            num_scalar_prefetch=1, grid=(n, rh // cr),
            in_specs=[pl.BlockSpec((None, None, cr, d), lambda kk, r, pos: (kk, pos[1], r, 0)),
                      pl.BlockSpec((None, cr, d), lambda kk, r, pos: (kk, r, 0))],
            out_specs=pl.BlockSpec((None, cr, d), lambda kk, r, pos: (kk, r, 0))),
        compiler_params=_cparams(),
    )(pos, gb, p1)


def chip_sum(pos, cs, p3):
    n, rh, d = cs.shape
    cr = _row_chunk(rh, 640)

    def body(pos_ref, a_ref, b_ref, o_ref):
        acc = a_ref[...].astype(F32)
        for j in range(3):
            acc = acc + b_ref[j].astype(F32)
        o_ref[...] = acc

    return pl.pallas_call(
        body, name="chip_sum", out_shape=SDS((2, rh, d), F32),
        grid_spec=pltpu.PrefetchScalarGridSpec(
            num_scalar_prefetch=1, grid=(rh // cr,),
            in_specs=[pl.BlockSpec((None, cr, d), lambda r, pos: (pos[0], r, 0)),
                      pl.BlockSpec((3, cr, d), lambda r, pos: (0, r, 0))],
            out_specs=pl.BlockSpec((None, cr, d), lambda r, pos: (pos[1], r, 0))),
        compiler_params=_cparams(),
    )(pos, cs, p3)


def small_pair_sum(a, b):
    def body(a_ref, b_ref, o_ref):
        o_ref[...] = a_ref[...] + b_ref[...]

    return pl.pallas_call(body, name="small_pair_sum", out_shape=SDS(a.shape, a.dtype),
                          compiler_params=_cparams())(a, b)


def small_chip_sum(pos, own, p):
    ns, w = own.shape

    def body(pos_ref, own_ref, p0, p1, p2, p3, o_ref):
        k = pos_ref[0]
        acc = None
        for chip, ref in enumerate((p0, p1, p2, p3)):
            term = jnp.where(k == chip, own_ref[...], ref[...])
            acc = term if acc is None else acc + term
        o_ref[...] = acc

    def slot(chip):
        return pl.BlockSpec((None, ns, w), lambda i, pos: (jnp.where(pos[0] == chip, (chip + 1) % N_CHIPS, chip), 0, 0))

    return pl.pallas_call(
        body, name="small_chip_sum", out_shape=SDS(own.shape, own.dtype),
        grid_spec=pltpu.PrefetchScalarGridSpec(
            num_scalar_prefetch=1, grid=(1,),
            in_specs=[pl.BlockSpec((ns, w), lambda i, pos: (0, 0))] + [slot(chip) for chip in range(N_CHIPS)],
            out_specs=pl.BlockSpec((ns, w), lambda i, pos: (0, 0))),
        compiler_params=_cparams(),
    )(pos, own, p, p, p, p)


def _rms(x, g):
    rs = lax.rsqrt(jnp.mean(x * x, axis=-1, keepdims=True) + EPS)
    xh = x * rs
    return xh, rs, xh * g


def _rms_bwd(dy, xh, rs, g):
    dxh = dy * g
    dx = rs * (dxh - xh * jnp.mean(dxh * xh, axis=-1, keepdims=True))
    return dx, dy * xh


def _gelu(x):
    t = jnp.tanh(GELU_C * (x + 0.044715 * x * x * x))
    return 0.5 * x * (1.0 + t), t


def _gelu_grad(x, t):
    return 0.5 * (1.0 + t) + 0.5 * x * (1.0 - t * t) * GELU_C * (1.0 + 3.0 * 0.044715 * x * x)


def _shift_rows(v, s, n):
    if s == 0:
        return v
    t = lax.broadcasted_iota(jnp.int32, v.shape, 0)
    rolled = pltpu.roll(v, (-s) % n, 0)
    inside = (t < n - s) if s > 0 else (t >= -s)
    return jnp.where(inside, rolled, 0.0)


def _scan_rows(a_ref, u_ref, h_ref, acum_ref, reverse):
    s_len, w = a_ref.shape
    chunk = min(512, s_len)
    last = 0 if reverse else SUBLANES - 1

    def inside_vregs(ci, _):
        rows = pl.ds(pl.multiple_of(ci * chunk, chunk), chunk)
        a = a_ref[rows, :].reshape(chunk // SUBLANES, SUBLANES, w)
        u = u_ref[rows, :].reshape(chunk // SUBLANES, SUBLANES, w)
        pos = lax.broadcasted_iota(jnp.int32, (1, SUBLANES, w), 1)
        for dist in (1, 2, 4):
            ok = (pos < SUBLANES - dist) if reverse else (pos >= dist)
            shift = SUBLANES - dist if reverse else dist
            u = u + a * jnp.where(ok, pltpu.roll(u, shift, 1), 0.0)
            a = a * jnp.where(ok, pltpu.roll(a, shift, 1), 1.0)
        h_ref[rows, :] = u.reshape(chunk, w)
        acum_ref[rows, :] = a.reshape(chunk, w)
        return 0

    lax.fori_loop(0, s_len // chunk, inside_vregs, 0)

    chains = max(1, min(SCAN_CHAINS, s_len // (8 * SUBLANES)))
    seg = s_len // chains
    nvreg = seg // SUBLANES

    def step(j, carry):
        jj = (nvreg - 1 - j) if reverse else j
        out = []
        for c, (hin, ain) in enumerate(carry):
            rows = pl.ds(pl.multiple_of(c * seg + jj * SUBLANES, SUBLANES), SUBLANES)
            acc = acum_ref[rows, :]
            h = h_ref[rows, :] + acc * hin
            acc = acc * ain
            h_ref[rows, :] = h
            acum_ref[rows, :] = acc
            out.append((jnp.broadcast_to(h[last:last + 1, :], h.shape), jnp.broadcast_to(acc[last:last + 1, :], acc.shape)))
        return tuple(out)

    init = tuple((jnp.zeros((SUBLANES, w), F32), jnp.ones((SUBLANES, w), F32)) for _ in range(chains))
    ends = lax.fori_loop(0, nvreg, step, init, unroll=min(2, nvreg))
    order = range(chains - 2, -1, -1) if reverse else range(1, chains)
    inflow = jnp.zeros((1, w), F32)
    for s in order:
        h, acc = ends[s + 1 if reverse else s - 1]
        inflow = h[0:1, :] + acc[0:1, :] * inflow
        rows = pl.ds(s * seg, seg)
        h_ref[rows, :] = h_ref[rows, :] + acum_ref[rows, :] * inflow


def _width_parts(f):
    cut = (f // 2) // MXU_WIDTH * MXU_WIDTH
    return [slice(0, cut), slice(cut, f)] if cut and f % MXU_WIDTH == 0 else [slice(0, f // 2), slice(f // 2, f)]


def _w_spec(rows_half, d, blk):
    return pl.BlockSpec((N_CHIPS, 2, rows_half, d), lambda *_: (0, 0, blk, 0), pipeline_mode=pl.Buffered(1))


def ffn_forward(x, gain, wfull, lay, which, deps=(), tm=512):
    s_len, d = x.shape
    tm = min(tm, s_len)
    f = 8 * lay.fh

    def body(x_ref, g_ref, wg_ref, wu_ref, wd_ref, *rest):
        o_ref, gate_ref, up_ref = rest[len(deps):]
        x = x_ref[...]
        _, _, hn = _rms(x, g_ref[...])
        h = hn.astype(BF)
        y = jnp.zeros((tm, d), F32)
        for cols in _width_parts(f):
            gate = dot_nt(h, wg_ref[...].reshape(f, d)[cols])
            up = dot_nt(h, wu_ref[...].reshape(f, d)[cols])
            act = (gate * jax.nn.sigmoid(gate) * up).astype(BF)
            y = y + dot_nn(act, wd_ref[...].reshape(f, d)[cols])
            gate_ref[:, cols] = gate.astype(BF)
            up_ref[:, cols] = up.astype(BF)
        o_ref[...] = x + FFN_RES * y

    row = pl.BlockSpec((tm, d), lambda i: (i, 0))
    wide = pl.BlockSpec((tm, f), lambda i: (i, 0))
    return pl.pallas_call(
        body, name="ffn_forward", grid=(s_len // tm,),
        out_shape=(SDS((s_len, d), F32), SDS((s_len, f), BF), SDS((s_len, f), BF)),
        in_specs=[row, pl.BlockSpec((1, d), lambda i: (0, 0))]
        + [_w_spec(lay.fh, d, 3 * which + m) for m in range(3)] + [ANY] * len(deps),
        out_specs=(row, wide, wide), compiler_params=_cparams(),
    )(x, gain, wfull, wfull, wfull, *deps)


def ffn_backward_dx(x, gain, dout, gate_bf, up_bf, wfull, lay, which, deps=(), tm=256):
    s_len, d = x.shape
    tm = min(tm, s_len)
    f = 8 * lay.fh
    nt = s_len // tm

    def body(x_ref, g_ref, do_ref, gate_ref, up_ref, wg_ref, wu_ref, wd_ref, *rest):
        dx_ref, dg_ref, lhs_ref, rhs_ref = rest[len(deps):]
        dgate_ref, dup_ref, act_ref = lhs_ref.at[0], lhs_ref.at[1], lhs_ref.at[2]
        h_ref, df_ref = rhs_ref.at[0], rhs_ref.at[1]
        x = x_ref[...]
        g = g_ref[...]
        xh, rs, hn = _rms(x, g)
        h = hn.astype(BF)
        do = do_ref[...]
        df = (FFN_RES * do).astype(BF)
        dh = jnp.zeros((tm, d), F32)
        wd = wd_ref[...].reshape(f, d)
        parts = _width_parts(f)
        dacts = [dot_nt(df, wd[cols]) for cols in parts]
        for part, cols in enumerate(parts):
            wg = wg_ref[...].reshape(f, d)[cols]
            wu = wu_ref[...].reshape(f, d)[cols]
            gate = gate_ref[:, cols].astype(F32)
            up = up_ref[:, cols].astype(F32)
            sg = jax.nn.sigmoid(gate)
            silu = gate * sg
            dact = dacts[part]
            dup = (dact * silu).astype(BF)
            dgate = (dact * up * (sg * (1.0 + gate * (1.0 - sg)))).astype(BF)
            dh = dh + dot_nn(dgate, wg) + dot_nn(dup, wu)
            dgate_ref[:, cols] = dgate
            dup_ref[:, cols] = dup
            act_ref[:, cols] = (silu * up).astype(BF)
        dxn, dgrow = _rms_bwd(dh, xh, rs, g)
        dx_ref[...] = do + dxn

        @pl.when(pl.program_id(0) == 0)
        def _():
            dg_ref[...] = jnp.zeros_like(dg_ref)

        dg_ref[...] += jnp.sum(dgrow, axis=0, keepdims=True)
        h_ref[...] = h
        df_ref[...] = df

    row = pl.BlockSpec((tm, d), lambda i: (i, 0))
    wide = pl.BlockSpec((tm, f), lambda i: (i, 0))
    vec = pl.BlockSpec((1, d), lambda i: (0, 0))
    return pl.pallas_call(
        body, name="ffn_backward_dx", grid=(nt,),
        out_shape=(SDS((s_len, d), F32), SDS((1, d), F32), SDS((3, s_len, f), BF), SDS((2, s_len, d), BF)),
        in_specs=[row, vec, row, wide, wide] + [_w_spec(lay.fh, d, 3 * which + m) for m in range(3)] + [ANY] * len(deps),
        out_specs=(row, vec, pl.BlockSpec((3, tm, f), lambda i: (0, i, 0)), pl.BlockSpec((2, tm, d), lambda i: (0, i, 0))),
        compiler_params=_cparams(),
    )(x, gain, dout, gate_bf, up_bf, wfull, wfull, wfull, *deps)


def weight_grad_tn(lhs, rhs, gb, lay, blk0, tk=4096):
    nmat, s_len, f = lhs.shape
    tk = min(tk, s_len)
    d = rhs.shape[2]
    fc = f // 2
    nk = s_len // tk

    def body(a_ref, b_ref, gb_ref, o_ref, acc):
        kt = pl.program_id(2)

        @pl.when(kt == 0)
        def _():
            acc[...] = jnp.zeros_like(acc)

        acc[...] += dot_tn(a_ref[...], b_ref[...])

        @pl.when(kt == nk - 1)
        def _():
            for p in range(2):
                for q in range(2):
                    o_ref[p, q] = acc[pl.ds((2 * p + q) * lay.fh, lay.fh), :].astype(o_ref.dtype)

    return pl.pallas_call(
        body, name="weight_grad_tn", grid=(nmat, 2, nk), out_shape=SDS(gb.shape, gb.dtype),
        in_specs=[pl.BlockSpec((None, tk, fc), lambda m, j, kt: (m, kt, j)),
                  pl.BlockSpec((None, tk, d), lambda m, j, kt: (jnp.where(m == nmat - 1, 1, 0), kt, 0)), ANY],
        out_specs=pl.BlockSpec((2, 2, lay.fh, d), lambda m, j, kt: (j, 0, blk0 + m, 0)),
        scratch_shapes=[pltpu.VMEM((fc, d), F32)],
        input_output_aliases={2: 0}, compiler_params=_cparams(),
    )(lhs, rhs, gb)


def _lane_blocks(v):
    return [v[:, j * LANE:(j + 1) * LANE] for j in range(v.shape[1] // LANE)]


def _join_lane_blocks(ref):
    return jnp.concatenate([ref[j] for j in range(ref.shape[0])], axis=1)


def _cbm_spec(nblk, rows, first=0):
    return pl.BlockSpec((nblk, rows, LANE), lambda i: (first // nblk, i, 0))


def mix_project(x, gain, wfull, lay, lw, att, tm=512):
    s_len, d = x.shape
    tm = min(tm, s_len)
    d_in = 8 * lay.ih
    kvw = (d_in - 2 * lw - att) // 2
    ncol = (2 * lw + 2 * kvw) // LANE

    def body(x_ref, g_ref, w_ref, o_ref, qt_ref, vt_ref):
        _, _, hn = _rms(x_ref[...], g_ref[...])
        h = hn.astype(BF)
        w = w_ref[:, :, :lay.ih, :].reshape(d_in, d)
        pieces = _lane_blocks(dot_nt(h, w[:2 * lw])) + _lane_blocks(dot_nt(h, w[2 * lw + att:]))
        for j, piece in enumerate(pieces):
            o_ref[j] = piece
        qt_ref[...] = dot_nt(w[2 * lw:2 * lw + att], h)
        vt_ref[...] = dot_nt(w[2 * lw + att + kvw:], h)

    return pl.pallas_call(
        body, name="mix_project", grid=(s_len // tm,),
        out_shape=(SDS((ncol, s_len, LANE), F32), SDS((att, s_len), F32), SDS((kvw, s_len), F32)),
        in_specs=[pl.BlockSpec((tm, d), lambda i: (i, 0)), pl.BlockSpec((1, d), lambda i: (0, 0)),
                  _w_spec(lay.fh, d, lay.MIX_BLK)],
        out_specs=(_cbm_spec(ncol, tm), pl.BlockSpec((att, tm), lambda i: (0, i)), pl.BlockSpec((kvw, tm), lambda i: (0, i))),
        compiler_params=_cparams(),
    )(x, gain, wfull)


def mix_project_backward(x, gain, dout, dxr, dgt, dqt, dkv, dwout, wfull, gb, lay, tm=512):
    s_len, d = x.shape
    tm = min(tm, s_len)
    d_in = 8 * lay.ih
    nt = s_len // tm
    kvw = dkv.shape[1]
    att = dqt.shape[0]
    nlru = (dxr.shape[0] + dgt.shape[0]) * LANE

    def body(x_ref, g_ref, do_ref, dxr_ref, dgt_ref, dqt_ref, dkv_ref, dwo_ref, w_ref, gb_ref, dx_ref, dg_ref, o_ref, acc):
        i = pl.program_id(0)
        g = g_ref[...]
        xh, rs, hn = _rms(x_ref[...], g)
        h = hn.astype(BF)
        w = w_ref[:, :, :lay.ih, :].reshape(d_in, d)
        dlru = jnp.concatenate([_join_lane_blocks(dxr_ref), _join_lane_blocks(dgt_ref)], axis=1).astype(BF)
        dqt = dqt_ref[...].astype(BF)
        dkv = dkv_ref[...].astype(BF)
        dh = dot_nn(dlru, w[:nlru]) + dot_tn(dqt, w[nlru:nlru + att]) + dot_nn(dkv, w[nlru + att:])
        dxn, dgrow = _rms_bwd(dh, xh, rs, g)
        dx_ref[...] = do_ref[...] + dxn

        @pl.when(i == 0)
        def _():
            dg_ref[...] = jnp.zeros_like(dg_ref)
            acc[...] = jnp.zeros_like(acc)

        dg_ref[...] += jnp.sum(dgrow, axis=0, keepdims=True)
        acc[0:nlru, :] += dot_tn(dlru, h)
        acc[nlru:nlru + att, :] += dot_nn(dqt, h)
        acc[nlru + att:, :] += dot_tn(dkv, h)

        @pl.when(i == nt - 1)
        def _():
            for p in range(N_CHIPS):
                for q in range(2):
                    o_ref[p, q, :lay.ih, :] = acc[pl.ds((2 * p + q) * lay.ih, lay.ih), :].astype(o_ref.dtype)
            o_ref[:, :, lay.ih:, :] = dwo_ref[...]

    row = pl.BlockSpec((tm, d), lambda i: (i, 0))
    vec = pl.BlockSpec((1, d), lambda i: (0, 0))
    return pl.pallas_call(
        body, name="mix_project_backward", grid=(nt,),
        out_shape=(SDS((s_len, d), F32), SDS((1, d), F32), SDS(gb.shape, gb.dtype)),
        in_specs=[row, vec, row, _cbm_spec(dxr.shape[0], tm), _cbm_spec(dgt.shape[0], tm),
                  pl.BlockSpec((att, tm), lambda i: (0, i)), pl.BlockSpec((tm, kvw), lambda i: (i, 0)),
                  pl.BlockSpec(dwout.shape, lambda i: (0, 0, 0, 0)), _w_spec(lay.fh, d, lay.MIX_BLK), ANY],
        out_specs=(row, vec, pl.BlockSpec((N_CHIPS, 2, lay.fh, d), lambda i: (0, 0, lay.MIX_BLK, 0))),
        scratch_shapes=[pltpu.VMEM((d_in, d), F32)],
        input_output_aliases={9: 2}, compiler_params=_cparams(),
    )(x, gain, dout, dxr, dgt, dqt, dkv, dwout, wfull, gb)


def _lru_gates(xc, wb_ref, pv_ref, direction):
    xcb = xc.astype(BF)
    r = jax.nn.sigmoid(dot_nn(xcb, wb_ref[2 * direction]) + pv_ref[1 + direction:2 + direction, :])
    i = jax.nn.sigmoid(dot_nn(xcb, wb_ref[2 * direction + 1]) + pv_ref[3 + direction:4 + direction, :])
    lam = pv_ref[5 + direction:6 + direction, :]
    sp = jnp.maximum(-lam, 0.0) + jnp.log(1.0 + jnp.exp(-jnp.abs(lam)))
    a = jnp.exp(-LRU_C * sp * r)
    mult = jnp.sqrt(1.0 - a * a)
    return xcb, r, i, a, mult, sp


def _conv_rows(xr, cv_ref, bias, n):
    acc = bias + cv_ref[0:1, :] * _shift_rows(xr, -2, n)
    for j in range(1, CONV_WIDTH):
        acc = acc + cv_ref[j:j + 1, :] * _shift_rows(xr, j - 2, n)
    return acc


def lru_forward(proj, cvec, pvec, wblk, lw, deps=(), ch=512):
    s_len = proj.shape[1]
    ncb = lw // LANE
    ch = min(ch, s_len)
    nchunk = s_len // ch

    def body(xr_ref, gt_ref, cv_ref, pv_ref, wb_ref, *rest):
        y_ref, hs_ref, xc_s, a_s, u_s, acum_s = rest[len(deps):]
        xc_s[...] = _conv_rows(xr_ref[...], cv_ref, pv_ref[0:1, :], s_len)
        for direction in range(2):
            def fill(ci, _):
                rows = pl.ds(pl.multiple_of(ci * ch, ch), ch)
                xc = xc_s[rows, :]
                _, _, i, a, mult, _ = _lru_gates(xc, wb_ref, pv_ref, direction)
                a_s[rows, :] = a
                u_s[rows, :] = mult * (i * xc)
                return 0

            lax.fori_loop(0, nchunk, fill, 0)
            _scan_rows(a_s, u_s, hs_ref.at[direction], acum_s, reverse=direction == 1)

        def out(ci, _):
            rows = pl.ds(pl.multiple_of(ci * ch, ch), ch)
            gl, _ = _gelu(gt_ref[rows, :])
            y_ref[rows, :] = gl * (hs_ref[0, rows, :] + hs_ref[1, rows, :])
            return 0

        lax.fori_loop(0, nchunk, out, 0)

    col = lambda off: pl.BlockSpec((None, s_len, LANE), lambda cb: (off + cb, 0, 0))
    return pl.pallas_call(
        body, name="lru_forward", grid=(ncb,),
        out_shape=(SDS((ncb, s_len, LANE), F32), SDS((2, ncb, s_len, LANE), F32)),
        in_specs=[col(0), col(ncb), pl.BlockSpec((8, LANE), lambda cb: (0, cb)), pl.BlockSpec((8, LANE), lambda cb: (0, cb)),
                  pl.BlockSpec((4, None, LANE, LANE), lambda cb: (0, cb, 0, 0))] + [ANY] * len(deps),
        out_specs=(col(0), pl.BlockSpec((2, None, s_len, LANE), lambda cb: (0, cb, 0, 0))),
        scratch_shapes=[pltpu.VMEM((s_len, LANE), F32)] * 4, compiler_params=_cparams(),
    )(proj, proj, cvec, pvec, wblk, *deps)


def lru_backward(proj, hs, dy, cvec, pvec, wblk, lw, ch=512):
    s_len = proj.shape[1]
    ncb = lw // LANE
    ch = min(ch, s_len)
    nchunk = s_len // ch

    def body(xr_ref, gt_ref, hs_ref, dy_ref, cv_ref, pv_ref, wb_ref, dxr_ref, dgt_ref, dcv_ref, dpv_ref, dwb_ref,
             xc_s, a_s, r_s, i_s, dh_s, lam_s, hp_s, dxc_s, acum_s):
        xr = xr_ref[...]
        xc_s[...] = _conv_rows(xr, cv_ref, pv_ref[0:1, :], s_len)
        dxc_s[...] = jnp.zeros_like(dxc_s)
        dpv_ref[...] = jnp.zeros_like(dpv_ref)
        dwb_ref[...] = jnp.zeros_like(dwb_ref)

        def head(ci, _):
            rows = pl.ds(pl.multiple_of(ci * ch, ch), ch)
            gt = gt_ref[rows, :]
            gl, t = _gelu(gt)
            dy = dy_ref[rows, :]
            dh_s[rows, :] = dy * gl
            dgt_ref[rows, :] = dy * (hs_ref[0, rows, :] + hs_ref[1, rows, :]) * _gelu_grad(gt, t)
            return 0

        lax.fori_loop(0, nchunk, head, 0)

        for direction in range(2):
            def fill(ci, _):
                rows = pl.ds(pl.multiple_of(ci * ch, ch), ch)
                _, r, i, a, _, _ = _lru_gates(xc_s[rows, :], wb_ref, pv_ref, direction)
                a_s[rows, :] = a
                r_s[rows, :] = r
                i_s[rows, :] = i
                return 0

            lax.fori_loop(0, nchunk, fill, 0)
            toward = 1 if direction == 0 else -1
            hp_s[...] = _shift_rows(a_s[...], toward, s_len)
            _scan_rows(hp_s, dh_s, lam_s, acum_s, reverse=direction == 0)
            hp_s[...] = _shift_rows(hs_ref[direction], -toward, s_len)

            def grads(ci, _):
                rows = pl.ds(pl.multiple_of(ci * ch, ch), ch)
                xc = xc_s[rows, :]
                xcb = xc.astype(BF)
                r, i, a = r_s[rows, :], i_s[rows, :], a_s[rows, :]
                mult = jnp.sqrt(1.0 - a * a)
                lam = pv_ref[5 + direction:6 + direction, :]
                sp = jnp.maximum(-lam, 0.0) + jnp.log(1.0 + jnp.exp(-jnp.abs(lam)))
                du = lam_s[rows, :]
                da = du * hp_s[rows, :]
                dmult = du * i * xc
                di = du * mult * xc
                dlog_a = (da - dmult * a / mult) * a
                dr = dlog_a * (-LRU_C * sp)
                dza = dr * r * (1.0 - r)
                dzx = di * i * (1.0 - i)
                dzab = dza.astype(BF)
                dzxb = dzx.astype(BF)
                dxc_s[rows, :] += (du * mult * i + dot_nt(dzab, wb_ref[2 * direction])
                                   + dot_nt(dzxb, wb_ref[2 * direction + 1]))
                dwb_ref[2 * direction] += dot_tn(xcb, dzab)
                dwb_ref[2 * direction + 1] += dot_tn(xcb, dzxb)
                dpv_ref[1 + direction:2 + direction, :] += jnp.sum(dza, axis=0, keepdims=True)
                dpv_ref[3 + direction:4 + direction, :] += jnp.sum(dzx, axis=0, keepdims=True)
                dpv_ref[5 + direction:6 + direction, :] += jnp.sum(dlog_a * (-LRU_C * r), axis=0, keepdims=True)
                return 0

            lax.fori_loop(0, nchunk, grads, 0)

        for direction in range(2):
            lam = pv_ref[5 + direction:6 + direction, :]
            dpv_ref[5 + direction:6 + direction, :] = dpv_ref[5 + direction:6 + direction, :] * (-jax.nn.sigmoid(-lam))
        dxc = dxc_s[...]
        dpv_ref[0:1, :] = jnp.sum(dxc, axis=0, keepdims=True)
        dxr = cv_ref[0:1, :] * _shift_rows(dxc, 2, s_len)
        for j in range(1, CONV_WIDTH):
            dxr = dxr + cv_ref[j:j + 1, :] * _shift_rows(dxc, 2 - j, s_len)
        dxr_ref[...] = dxr
        dcv_ref[...] = jnp.zeros_like(dcv_ref)
        for j in range(CONV_WIDTH):
            dcv_ref[j:j + 1, :] = jnp.sum(dxc * _shift_rows(xr, j - 2, s_len), axis=0, keepdims=True)

    col = lambda off: pl.BlockSpec((None, s_len, LANE), lambda cb: (off + cb, 0, 0))
    own = col(0)
    small = pl.BlockSpec((8, LANE), lambda cb: (0, cb))
    wspec = pl.BlockSpec((4, None, LANE, LANE), lambda cb: (0, cb, 0, 0))
    return pl.pallas_call(
        body, name="lru_backward", grid=(ncb,),
        out_shape=(SDS((ncb, s_len, LANE), F32), SDS((ncb, s_len, LANE), F32), SDS((8, lw), F32), SDS((8, lw), F32),
                   SDS(wblk.shape, F32)),
        in_specs=[col(0), col(ncb), pl.BlockSpec((2, None, s_len, LANE), lambda cb: (0, cb, 0, 0)), own, small, small, wspec],
        out_specs=(own, own, small, small, wspec),
        scratch_shapes=[pltpu.VMEM((s_len, LANE), F32)] * 9, compiler_params=_cparams(),
    )(proj, proj, hs, dy, cvec, pvec, wblk)


def _window_specs(s_len, first, width=None):
    nb = s_len // BLOCK
    where = (lambda n: jnp.maximum(n - 1, 0), lambda n: n, lambda n: jnp.minimum(n + 1, nb - 1))
    if width is None:
        return [pl.BlockSpec((None, BLOCK, LANE), lambda n, f=f: (first, f(n), 0)) for f in where]
    return [pl.BlockSpec((width, BLOCK), lambda n, f=f: (0, f(n))) for f in where]


def _stack_heads(v, kh):
    return jnp.concatenate([v[(kh * KV_GROUP + g) * HEAD_DIM:(kh * KV_GROUP + g + 1) * HEAD_DIM, :]
                            for g in range(KV_GROUP)], axis=1)


def _unstack_heads(ref, kh, v):
    for g in range(KV_GROUP):
        h = kh * KV_GROUP + g
        ref[h * HEAD_DIM:(h + 1) * HEAD_DIM, :] = v[:, g * BLOCK:(g + 1) * BLOCK]


def _key_exists(n, nb):
    j = lax.broadcasted_iota(jnp.int32, (3 * BLOCK, 1), 0)
    return ((n > 0) | (j >= BLOCK)) & ((n < nb - 1) | (j < 2 * BLOCK))


def _attn_probs(qs, kcat, bias_g, sink_g, key_ok):
    logits = jnp.where(key_ok, dot_nn(kcat, qs) + bias_g, NEG_INF)
    m = jnp.maximum(jnp.max(logits, axis=0, keepdims=True), sink_g)
    p = jnp.exp(logits - m)
    es = jnp.exp(sink_g - m)
    inv = 1.0 / (jnp.sum(p, axis=0, keepdims=True) + es)
    return p * inv, es * inv


def attention_forward(qt, proj, vt, bias, sink, kblk):
    att, s_len = qt.shape
    kvw = vt.shape[0]
    nb = s_len // BLOCK

    def body(q_ref, kp_ref, kc_ref, kn_ref, vp_ref, vc_ref, vn_ref, b_ref, s_ref, o_ref):
        n = pl.program_id(0)
        q = q_ref[...]
        key_ok = _key_exists(n, nb)
        kall = jnp.concatenate([kp_ref[...], kc_ref[...], kn_ref[...]], axis=0).astype(BF)
        vall = jnp.concatenate([vp_ref[...], vc_ref[...], vn_ref[...]], axis=1).astype(BF)
        for kh in range(N_KV_HEADS):
            qs = (_stack_heads(q, kh) * (HEAD_DIM ** -0.5)).astype(BF)
            p, _ = _attn_probs(qs, kall[:, kh * HEAD_DIM:(kh + 1) * HEAD_DIM], b_ref[kh], s_ref[kh, 0:1, :], key_ok)
            _unstack_heads(o_ref, kh, dot_nn(vall[kh * HEAD_DIM:(kh + 1) * HEAD_DIM, :], p.astype(BF)))

    blk = pl.BlockSpec((att, BLOCK), lambda n: (0, n))
    return pl.pallas_call(
        body, name="attention_forward", grid=(nb,), out_shape=SDS((att, s_len), F32),
        in_specs=[blk] + _window_specs(s_len, kblk) + _window_specs(s_len, 0, kvw)
        + [pl.BlockSpec(bias.shape, lambda n: (0, 0, 0)), pl.BlockSpec(sink.shape, lambda n: (0, 0, 0))],
        out_specs=blk, compiler_params=_cparams(),
    )(qt, proj, proj, proj, vt, vt, vt, bias, sink)


def attention_backward(qt, proj, y_att, dy, bias, sink, kblk):
    att, s_len = qt.shape
    nb = s_len // BLOCK
    kvw = N_KV_HEADS * HEAD_DIM

    def body(q_ref, kp_ref, kc_ref, kn_ref, vp_ref, vc_ref, vn_ref, o_ref, do_ref, b_ref, s_ref,
             dq_ref, dkv_ref, db_ref, ds_ref):
        n = pl.program_id(0)

        @pl.when(n == 0)
        def _():
            dkv_ref[...] = jnp.zeros_like(dkv_ref)
            db_ref[...] = jnp.zeros_like(db_ref)
            ds_ref[...] = jnp.zeros_like(ds_ref)

        q = q_ref[...]
        o = o_ref[...]
        do = do_ref[...]
        kall = jnp.concatenate([kp_ref[...], kc_ref[...], kn_ref[...]], axis=0).astype(BF)
        vall = jnp.concatenate([vp_ref[...], vc_ref[...], vn_ref[...]], axis=0).astype(BF)
        key_ok = _key_exists(n, nb)
        dks, dvs = [], []
        for kh in range(N_KV_HEADS):
            kcat = kall[:, kh * HEAD_DIM:(kh + 1) * HEAD_DIM]
            vcat = vall[:, kh * HEAD_DIM:(kh + 1) * HEAD_DIM]
            qs = (_stack_heads(q, kh) * (HEAD_DIM ** -0.5)).astype(BF)
            p, ps = _attn_probs(qs, kcat, b_ref[kh], s_ref[kh, 0:1, :], key_ok)
            dos = _stack_heads(do, kh)
            dosb = dos.astype(BF)
            delta = jnp.sum(dos * _stack_heads(o, kh), axis=0, keepdims=True)
            dlog = p * (dot_nn(vcat, dosb) - delta)
            dlogb = dlog.astype(BF)
            db_ref[kh] += dlog
            ds_ref[kh] += jnp.broadcast_to(-ps * delta, ds_ref.shape[1:])
            _unstack_heads(dq_ref, kh, dot_tn(kcat, dlogb) * (HEAD_DIM ** -0.5))
            dks.append(dot_nt(dlogb, qs))
            dvs.append(dot_nt(p.astype(BF), dosb))
        dkv = jnp.concatenate(dks + dvs, axis=1)
        starts = [jnp.maximum(n - 1, 0), n, jnp.minimum(n + 1, nb - 1)]
        for b, st in enumerate(starts):
            rows = pl.ds(pl.multiple_of(st * BLOCK, BLOCK), BLOCK)
            dkv_ref[rows, :] += dkv[b * BLOCK:(b + 1) * BLOCK, :]

    blk = pl.BlockSpec((att, BLOCK), lambda n: (0, n))
    whole = lambda a: pl.BlockSpec(a.shape, lambda n: (0, 0, 0))
    return pl.pallas_call(
        body, name="attention_backward", grid=(nb,),
        out_shape=(SDS((att, s_len), F32), SDS((s_len, 2 * kvw), F32), SDS(bias.shape, F32), SDS(sink.shape, F32)),
        in_specs=[blk] + _window_specs(s_len, kblk) + _window_specs(s_len, kblk + 1) + [blk, blk, whole(bias), whole(sink)],
        out_specs=(blk, pl.BlockSpec((s_len, 2 * kvw), lambda n: (0, 0)), whole(bias), whole(sink)),
        compiler_params=_cparams(),
    )(qt, proj, proj, proj, proj, proj, proj, y_att, dy, bias, sink)


def _rms_cols(x, g):
    rs = lax.rsqrt(jnp.mean(x * x, axis=0, keepdims=True) + EPS)
    xh = x * rs
    return xh, rs, xh * g


def _rms_cols_bwd(dy, xh, rs, g):
    dxh = dy * g
    dx = rs * (dxh - xh * jnp.mean(dxh * xh, axis=0, keepdims=True))
    return dx, dy * xh


def mix_output(x, y_rec, y_att, g_rec, g_att, wfull, lay, tm=512):
    s_len, d = x.shape
    tm = min(tm, s_len)
    lw = y_rec.shape[0] * LANE
    att = y_att.shape[0]

    def body(x_ref, yr_ref, ya_ref, gr_ref, ga_ref, w_ref, o_ref):
        _, _, nr = _rms(_join_lane_blocks(yr_ref), gr_ref[...])
        _, _, na = _rms_cols(ya_ref[...], ga_ref[...])
        w = w_ref[:, :, lay.ih:, :].reshape(d, d)
        o_ref[...] = x_ref[...] + dot_nn(nr.astype(BF), w[:lw]) + dot_tn(na.astype(BF), w[lw:])

    row = pl.BlockSpec((tm, d), lambda i: (i, 0))
    return pl.pallas_call(
        body, name="mix_output", grid=(s_len // tm,), out_shape=SDS((s_len, d), F32),
        in_specs=[row, _cbm_spec(lw // LANE, tm), pl.BlockSpec((att, tm), lambda i: (0, i)),
                  pl.BlockSpec((1, lw), lambda i: (0, 0)), pl.BlockSpec((att, 1), lambda i: (0, 0)),
                  _w_spec(lay.fh, d, lay.MIX_BLK)],
        out_specs=row, compiler_params=_cparams(),
    )(x, y_rec, y_att, g_rec, g_att, wfull)


def mix_output_backward(dout, y_rec, y_att, g_rec, g_att, wfull, lay, deps=(), tm=1024):
    s_len, d = dout.shape
    tm = min(tm, s_len)
    lw = y_rec.shape[0] * LANE
    att = y_att.shape[0]
    nt = s_len // tm

    def body(do_ref, yr_ref, ya_ref, gr_ref, ga_ref, w_ref, *rest):
        dyr_ref, dya_ref, dgr_ref, dga_ref, o_ref, acc = rest[len(deps):]
        i = pl.program_id(0)
        gr = gr_ref[...]
        ga = ga_ref[...]
        xhr, rsr, nr = _rms(_join_lane_blocks(yr_ref), gr)
        xha, rsa, na = _rms_cols(ya_ref[...], ga)
        dob = do_ref[...].astype(BF)
        w = w_ref[:, :, lay.ih:, :].reshape(d, d)
        dyr, dgr_row = _rms_bwd(dot_nt(dob, w[:lw]), xhr, rsr, gr)
        dya, dga_col = _rms_cols_bwd(dot_nt(w[lw:], dob), xha, rsa, ga)
        for j, piece in enumerate(_lane_blocks(dyr)):
            dyr_ref[j] = piece
        dya_ref[...] = dya

        @pl.when(i == 0)
        def _():
            dgr_ref[...] = jnp.zeros_like(dgr_ref)
            dga_ref[...] = jnp.zeros_like(dga_ref)
            acc[...] = jnp.zeros_like(acc)

        dgr_ref[...] += jnp.sum(dgr_row, axis=0, keepdims=True)
        dga_ref[...] += jnp.sum(dga_col, axis=1, keepdims=True)
        acc[0:lw, :] += dot_tn(nr.astype(BF), dob)
        acc[lw:, :] += dot_nn(na.astype(BF), dob)

        @pl.when(i == nt - 1)
        def _():
            for p in range(N_CHIPS):
                for q in range(2):
                    o_ref[p, q] = acc[pl.ds((2 * p + q) * lay.oh, lay.oh), :].astype(o_ref.dtype)

    row = pl.BlockSpec((tm, d), lambda i: (i, 0))
    return pl.pallas_call(
        body, name="mix_output_backward", grid=(nt,),
        out_shape=(SDS(y_rec.shape, F32), SDS(y_att.shape, F32), SDS((1, lw), F32), SDS((att, 1), F32),
                   SDS((N_CHIPS, 2, lay.oh, d), BF)),
        in_specs=[row, _cbm_spec(lw // LANE, tm), pl.BlockSpec((att, tm), lambda i: (0, i)),
                  pl.BlockSpec((1, lw), lambda i: (0, 0)), pl.BlockSpec((att, 1), lambda i: (0, 0)),
                  _w_spec(lay.fh, d, lay.MIX_BLK)] + [ANY] * len(deps),
        out_specs=(_cbm_spec(lw // LANE, tm), pl.BlockSpec((att, tm), lambda i: (0, i)),
                   pl.BlockSpec((1, lw), lambda i: (0, 0)), pl.BlockSpec((att, 1), lambda i: (0, 0)),
                   pl.BlockSpec((N_CHIPS, 2, lay.oh, d), lambda i: (0, 0, 0, 0))),
        scratch_shapes=[pltpu.VMEM((d, d), F32)], compiler_params=_cparams(),
    )(dout, y_rec, y_att, g_rec, g_att, wfull, *deps)


def loss_head(x, gain, target, tm=512):
    s_len, d = x.shape
    tm = min(tm, s_len)

    def body(x_ref, g_ref, t_ref, dx_ref, dg_ref, loss_ref):
        g = g_ref[...]
        xh, rs, y = _rms(x_ref[...], g)
        err = y - t_ref[...]

        @pl.when(pl.program_id(0) == 0)
        def _():
            dg_ref[...] = jnp.zeros_like(dg_ref)
            loss_ref[...] = jnp.zeros_like(loss_ref)

        part = 0.5 * jnp.sum(jnp.mean(err * err, axis=-1, keepdims=True), axis=0, keepdims=True)
        loss_ref[...] += jnp.broadcast_to(part, loss_ref.shape)
        dx, dgrow = _rms_bwd(err * (1.0 / d), xh, rs, g)
        dx_ref[...] = dx
        dg_ref[...] += jnp.sum(dgrow, axis=0, keepdims=True)

    row = pl.BlockSpec((tm, d), lambda i: (i, 0))
    vec = pl.BlockSpec((1, d), lambda i: (0, 0))
    return pl.pallas_call(
        body, name="loss_head", grid=(s_len // tm,),
        out_shape=(SDS((s_len, d), F32), SDS((1, d), F32), SDS((8, LANE), F32)),
        in_specs=[row, vec, row], out_specs=(row, vec, pl.BlockSpec((8, LANE), lambda i: (0, 0))),
        compiler_params=_cparams(),
    )(x, gain, target)


def _adamw_update(w, g, m, v):
    m = ADAM_B1 * m + (1.0 - ADAM_B1) * g
    v = ADAM_B2 * v + (1.0 - ADAM_B2) * (g * g)
    m_hat = m / (1.0 - ADAM_B1 ** ADAM_STEP)
    v_hat = v / (1.0 - ADAM_B2 ** ADAM_STEP)
    return -ADAM_LR * (m_hat / (jnp.sqrt(v_hat) + ADAM_EPS) + ADAM_WD * w), m, v


def adamw(w, g, m, v, tr=512):
    rows, cols = w.shape
    tr = _row_chunk(rows, tr, 8)

    def body(w_ref, g_ref, m_ref, v_ref, d_ref, nm_ref, nv_ref):
        d_ref[...], nm_ref[...], nv_ref[...] = _adamw_update(w_ref[...], g_ref[...], m_ref[...], v_ref[...])

    blk = pl.BlockSpec((tr, cols), lambda i: (i, 0))
    return pl.pallas_call(
        body, name="adamw", grid=(rows // tr,), out_shape=(SDS(w.shape, F32),) * 3,
        in_specs=[blk] * 4, out_specs=(blk,) * 3, compiler_params=_cparams(),
    )(w, g, m, v)


def adamw_layer(gf, blk, row_off, n_half, l, w, m, v, outs, deps=()):
    fh = gf.shape[1] // Layout.BLOCKS
    d = gf.shape[2]
    nd = len(deps)

    def body(gf_ref, w_ref, m_ref, v_ref, *rest):
        g_ref, d_ref, nm_ref, nv_ref = rest[4 + nd:]
        g = gf_ref[row_off:row_off + n_half, :]
        g_ref[...] = g
        d_ref[...], nm_ref[...], nv_ref[...] = _adamw_update(w_ref[...], g, m_ref[...], v_ref[...])

    gspec = pl.BlockSpec((None, fh, d), lambda h: (h, blk, 0))
    wspec = pl.BlockSpec((None, n_half, d), lambda h: (l, h, 0))
    return pl.pallas_call(
        body, name="adamw_layer", grid=(2,), out_shape=tuple(SDS(o.shape, o.dtype) for o in outs),
        in_specs=[gspec, wspec, wspec, wspec] + [ANY] * (4 + nd), out_specs=(wspec,) * 4,
        input_output_aliases={4 + i: i for i in range(4)}, compiler_params=_cparams(),
    )(gf, w, m, v, *outs, *deps)


def pack_weight(pos, land, blk, l, w, extra=None, deps=()):
    fh, d = land.shape[2] // Layout.BLOCKS, land.shape[3]
    nd = len(deps)

    def body(pos_ref, w_ref, *rest):
        o_ref = rest[-1]
        a = w_ref[...].astype(BF)
        n = a.shape[0] // 2
        for h in range(2):
            o_ref[h, 0:n, :] = a[h * n:(h + 1) * n]
        if extra is not None:
            b = rest[0][...].astype(BF)
            nb = b.shape[0] // 2
            for h in range(2):
                o_ref[h, n:n + nb, :] = b[h * nb:(h + 1) * nb]

    def whole(a):
        return pl.BlockSpec((None,) + a.shape[1:], lambda i, p: (l, 0, 0))

    ins = [w] + ([extra] if extra is not None else [])
    return pl.pallas_call(
        body, name="pack_weight", out_shape=SDS(land.shape, land.dtype),
        grid_spec=pltpu.PrefetchScalarGridSpec(
            num_scalar_prefetch=1, grid=(1,),
            in_specs=[whole(a) for a in ins] + [ANY] * (1 + nd),
            out_specs=pl.BlockSpec((None, 2, fh, d), lambda i, p: (p[0], 0, blk, 0))),
        input_output_aliases={1 + len(ins): 0}, compiler_params=_cparams(),
    )(pos, *ins, land, *deps)


def _rows_of(shape, width):
    return -(-int(np.prod(shape)) // (SUBLANES * width)) * SUBLANES


def _pack_rows(arrays, width):
    parts = []
    for a in arrays:
        flat = a.reshape(-1).astype(F32)
        r = _rows_of(a.shape, width)
        parts.append(jnp.pad(flat, (0, r * width - flat.shape[0])).reshape(r, width))
    return jnp.concatenate(parts, axis=0)


def _unpack_rows(buf, shapes):
    out, row = [], 0
    for shp in shapes:
        r = _rows_of(shp, buf.shape[1])
        out.append(buf[row:row + r].reshape(-1)[:int(np.prod(shp))].reshape(shp))
        row += r
    return out


def _t5_buckets(rel):
    half = N_BUCKETS // 2
    max_exact = half // 2
    ret = (rel > 0).astype(jnp.int32) * half
    n = jnp.abs(rel)
    n_f = jnp.maximum(n, 1).astype(F32)
    large = max_exact + (jnp.log(n_f / max_exact) / math.log(MAX_DISTANCE / max_exact) * (half - max_exact)).astype(jnp.int32)
    large = jnp.minimum(large, half - 1)
    return ret + jnp.where(n < max_exact, n, large)


def _band_buckets():
    t = jnp.arange(BLOCK)[:, None]
    j = jnp.arange(3 * BLOCK)[None, :]
    rel = j - BLOCK - t
    return _t5_buckets(rel), jnp.abs(rel) <= WINDOW


def _block_diag_pairs(w):
    depth, two, nblk, bw, _ = w.shape
    pairs = w.reshape(depth, two, nblk // 2, 2, bw, bw)
    z = jnp.zeros_like(pairs[:, :, :, 0])
    top = jnp.concatenate([pairs[:, :, :, 0], z], axis=-1)
    bot = jnp.concatenate([z, pairs[:, :, :, 1]], axis=-1)
    return jnp.concatenate([top, bot], axis=-2)


def _diag_blocks(dw):
    bw = dw.shape[-1] // 2
    a = dw[:, :, :bw, :bw]
    b = dw[:, :, bw:, bw:]
    return jnp.stack([a, b], axis=2).reshape(dw.shape[0], 2 * dw.shape[1], bw, bw)


def kernel(x, ffn1_norm, ffn1_w_gate, ffn1_w_up, ffn1_w_down, mix_norm, w_in, conv_w, conv_b, lru_w_a, lru_b_a, lru_w_x, lru_b_x, lru_lambda, attn_sink, rel_bias, lru_out_norm, attn_out_norm, w_out, ffn2_norm, ffn2_w_gate, ffn2_w_up, ffn2_w_down, final_norm, loss_target, m_ffn1_norm, m_ffn1_w_gate, m_ffn1_w_up, m_ffn1_w_down, m_mix_norm, m_w_in, m_conv_w, m_conv_b, m_lru_w_a, m_lru_b_a, m_lru_w_x, m_lru_b_x, m_lru_lambda, m_attn_sink, m_rel_bias, m_lru_out_norm, m_attn_out_norm, m_w_out, m_ffn2_norm, m_ffn2_w_gate, m_ffn2_w_up, m_ffn2_w_down, m_final_norm, v_ffn1_norm, v_ffn1_w_gate, v_ffn1_w_up, v_ffn1_w_down, v_mix_norm, v_w_in, v_conv_w, v_conv_b, v_lru_w_a, v_lru_b_a, v_lru_w_x, v_lru_b_x, v_lru_lambda, v_attn_sink, v_rel_bias, v_lru_out_norm, v_attn_out_norm, v_w_out, v_ffn2_norm, v_ffn2_w_gate, v_ffn2_w_up, v_ffn2_w_down, v_final_norm):
    depth, d = ffn1_norm.shape
    d_ff = N_CHIPS * ffn1_w_gate.shape[2]
    d_in = N_CHIPS * w_in.shape[2]
    lw = conv_b.shape[1]
    att = N_HEADS * HEAD_DIM
    lay = Layout(d, d_ff, d_in)
    k_chip = 2 * lax.axis_index("x") + lax.axis_index("y")
    pos = jnp.stack([k_chip, lax.axis_index("c")]).astype(jnp.int32)

    def rows_major(a):
        return jnp.swapaxes(a, 1, 2)

    mats = (rows_major(ffn1_w_gate), rows_major(ffn1_w_up), ffn1_w_down,
            rows_major(ffn2_w_gate), rows_major(ffn2_w_up), ffn2_w_down)

    def pack_layer(l, deps=()):
        land = lax.empty((N_CHIPS, 2, lay.rows, d), BF)
        for m, a in enumerate(mats):
            land = pack_weight(pos, land, m, l, a, deps=deps if m == 0 else ())
        return pack_weight(pos, land, lay.MIX_BLK, l, rows_major(w_in), extra=w_out)

    def gather_start(l, land):
        return split_start(f"gather_start_{l}", [land], 3, gather_plan)

    def gather_wait(l, started, after):
        ssem, rsem, bufs, _ = started
        return split_wait(f"gather_wait_{l}", ssem, rsem, bufs, after, gather_plan)

    sharded_small = (conv_w, lru_b_a, lru_b_x, lru_lambda)
    sshard = jnp.concatenate([a.reshape(-1, LANE) for a in sharded_small], axis=0)
    sfull = gather_small(sshard)
    small_full, off = [], 0
    for a in sharded_small:
        r = a.shape[0] * a.shape[1]
        piece = sfull[:, off:off + r].reshape((N_CHIPS,) + a.shape)
        small_full.append(jnp.moveaxis(piece, 0, 2).reshape(a.shape[0], a.shape[1], N_CHIPS * LANE))
        off += r
    conv_w_f, b_a_f, b_x_f, lam_f = small_full

    zrow = jnp.zeros((1, lw), F32)
    wblk_a = _block_diag_pairs(lru_w_a)
    wblk_x = _block_diag_pairs(lru_w_x)
    buckets, in_band = _band_buckets()
    onehot = (buckets.reshape(-1)[:, None] == jnp.arange(N_BUCKETS)[None, :]).astype(F32)
    bias = jnp.dot(rel_bias.T, onehot.T, precision=lax.Precision.HIGHEST).reshape(N_HEADS, BLOCK, 3 * BLOCK)
    bias = jnp.where(in_band[None], bias, NEG_INF)
    bias = bias.reshape(N_KV_HEADS, KV_GROUP, BLOCK, 3 * BLOCK).transpose(0, 3, 1, 2).reshape(N_KV_HEADS, 3 * BLOCK, KV_GROUP * BLOCK)
    kblk = 2 * lw // LANE

    def layer_small(l):
        cvec = jnp.concatenate([conv_w_f[l], jnp.zeros((8 - CONV_WIDTH, lw), F32)], axis=0)
        pvec = jnp.concatenate([conv_b[l][None], b_a_f[l], b_x_f[l], lam_f[l], zrow], axis=0)
        wblk = jnp.stack([wblk_a[l, 0], wblk_x[l, 0], wblk_a[l, 1], wblk_x[l, 1]]).astype(BF)
        sink = jnp.broadcast_to(jnp.repeat(attn_sink[l], BLOCK).reshape(N_KV_HEADS, 1, KV_GROUP * BLOCK),
                                (N_KV_HEADS, 8, KV_GROUP * BLOCK))
        return cvec, pvec, wblk, sink

    xs = x[0]
    wfull = [None] * depth
    parts = [(0, 3 * lay.fh), (3 * lay.fh, lay.rows - 3 * lay.fh)]
    plans = [(functools.partial(gather_plan, rows=p), functools.partial(handover_plan, rows=p)) for p in parts]
    land = pack_layer(0, deps=(sfull,))
    first = split_start("gather_start_0a", [land], 3, plans[0][0])
    second = split_start("gather_start_0b", first[2], 3, plans[1][0])
    lands = {l: pack_layer(l, deps=(second[3],)) for l in range(1, depth)}
    land = split_wait("gather_wait_0a", first[0], first[1], second[2], [xs] + list(lands.values()), plans[0][0])
    wfull[0], = exchange_now("gather_handover_0a", land, 3, plans[0][1])
    started = None
    saved = []
    for l in range(depth):
        cvec, pvec, wblk, sink = layer_small(l)
        deps = (started[3],) if started is not None else ()
        x1, gate1, up1 = ffn_forward(xs, ffn1_norm[l][None], wfull[l], lay, 0, deps=deps)
        deps = ()
        if l == 0:
            land = split_wait("gather_wait_0b", second[0], second[1], [wfull[0]], [x1], plans[1][0])
            wfull[0], = exchange_now("gather_handover_0b", land, 3, plans[1][1])
            if depth > 1:
                started = gather_start(1, lands[1])
                deps = (started[3],)
        proj, qt, vt = mix_project(x1, mix_norm[l][None], wfull[l], lay, lw, att)
        y_rec, hs = lru_forward(proj, cvec, pvec, wblk, lw, deps=deps)
        y_att = attention_forward(qt, proj, vt, bias, sink, kblk)
        x2 = mix_output(x1, y_rec, y_att, lru_out_norm[l][None], attn_out_norm[l][:, None], wfull[l], lay)
        deps, handover = (), None
        if 0 < l < depth - 1:
            land, = gather_wait(l + 1, started, [x2])
            started = gather_start(l + 2, lands[l + 2]) if l + 2 < depth else None
            handover = split_start(f"gather_handover_start_{l + 1}", [land], 3, handover_plan)
            deps = (handover[3],) + ((started[3],) if started is not None else ())
        x3, gate2, up2 = ffn_forward(x2, ffn2_norm[l][None], wfull[l], lay, 1, deps=deps)
        saved.append((xs, x1, x2, proj, qt, y_rec, hs, y_att, (gate1, up1), (gate2, up2)))
        xs = x3
        if handover is not None:
            wfull[l + 1], = split_wait(f"gather_handover_wait_{l + 1}", handover[0], handover[1], handover[2], [x3],
                                       handover_plan)
        elif l == 0 and depth > 1:
            wfull[1], = exchange_now("gather_handover_1", gather_wait(1, started, [x3]), 3, handover_plan)
            started = gather_start(2, lands[2]) if depth > 2 else None

    dx, d_final, loss_tile = loss_head(xs, final_norm[None], loss_target[0])
    loss = lax.psum(loss_tile[0, 0], ("x", "y", "c"))

    layer_names = ["ffn1_norm", "mix_norm", "conv_w", "conv_b", "lru_w_a", "lru_b_a", "lru_w_x", "lru_b_x", "lru_lambda",
                   "attn_sink", "lru_out_norm", "attn_out_norm", "ffn2_norm"]
    dbias_total = jnp.zeros(bias.shape, F32)

    def ffn_back(xin, gain, dout, pre, gb, l, which, deps=()):
        dxo, dg, lhs, rhs = ffn_backward_dx(xin, gain, dout, *pre, wfull[l], lay, which, deps=deps)
        return dxo, dg[0], weight_grad_tn(lhs, rhs, gb, lay, 3 * which)

    def pair_start(l, gb, sb):
        lands = [lax.empty((N_CHIPS,) + gb.shape[2:], gb.dtype), lax.empty(sb.shape, sb.dtype)]
        return split_start(f"pair_start_{l}", [gb, sb] + lands, N_CHIPS + 1, pair_plan)

    def reduce_start(l, paired, after):
        gb, sb, p1, sp1 = split_wait(f"pair_wait_{l}", paired[0], paired[1], paired[2], after, pair_plan)
        cs = pair_sum(pos, gb, p1)
        ss = small_pair_sum(sb, sp1)
        lands = [lax.empty((3,) + cs.shape[1:], cs.dtype), lax.empty((N_CHIPS,) + ss.shape, ss.dtype)]
        return split_start(f"reduce_start_{l}", [cs, ss] + lands, 6, reduce_plan)

    def reduce_finish(l, started, after):
        ssem, rsem, bufs, _ = started
        cs, ss, p3, sp3 = split_wait(f"reduce_wait_{l}", ssem, rsem, bufs, after, reduce_plan)
        return chip_sum(pos, cs, p3), small_chip_sum(pos, ss, sp3)

    gf = [None] * depth
    small_sums = [None] * depth
    small_shapes = [None] * depth
    paired = None
    in_flight = None
    finals = {}
    tokens = []
    for l in reversed(range(depth)):
        x0, x1, x2, proj, qt, y_rec, hs, y_att, pre1, pre2 = saved[l]
        cvec, pvec, wblk, sink = layer_small(l)
        gb = lax.empty((N_CHIPS, 2, lay.rows, d), BF)
        part = {}
        dx, part["ffn2_norm"], gb = ffn_back(x2, ffn2_norm[l][None], dx, pre2, gb, l, 1, deps=tuple(tokens))
        deps = ()
        if paired is not None:
            in_flight = (paired[0], reduce_start(paired[0], paired[1], [dx, gb]))
            deps = (in_flight[1][3],)
        dyr, dya, dgr, dga, dwout = mix_output_backward(dx, y_rec, y_att, lru_out_norm[l][None], attn_out_norm[l][:, None],
                                                        wfull[l], lay, deps=deps)
        part["lru_out_norm"] = dgr[0]
        part["attn_out_norm"] = dga[:, 0]
        dq, dkv, dbias, dsink = attention_backward(qt, proj, y_att, dya, bias, sink, kblk)
        dbias_total = dbias_total + dbias
        part["attn_sink"] = jnp.sum(dsink[:, 0, :].reshape(N_HEADS, BLOCK), axis=1)
        dxr, dgt, dcv, dpv, dwb = lru_backward(proj, hs, dyr, cvec, pvec, wblk, lw)
        part["conv_w"] = dcv[:CONV_WIDTH]
        part["conv_b"] = dpv[0]
        part["lru_b_a"] = dpv[1:3]
        part["lru_b_x"] = dpv[3:5]
        part["lru_lambda"] = dpv[5:7]
        part["lru_w_a"] = _diag_blocks(jnp.stack([dwb[0], dwb[2]]))
        part["lru_w_x"] = _diag_blocks(jnp.stack([dwb[1], dwb[3]]))
        dx, dgm, gb = mix_project_backward(x1, mix_norm[l][None], dx, dxr, dgt, dq, dkv, dwout, wfull[l], gb, lay)
        part["mix_norm"] = dgm[0]
        dx, part["ffn1_norm"], gb = ffn_back(x0, ffn1_norm[l][None], dx, pre1, gb, l, 0)
        pieces = [part[n] for n in layer_names]
        if l == 0:
            dbias_heads = dbias_total.reshape(N_KV_HEADS, 3 * BLOCK, KV_GROUP, BLOCK).transpose(0, 2, 3, 1)
            d_rel_bias = jnp.dot(dbias_heads.reshape(N_HEADS, -1), onehot, precision=lax.Precision.HIGHEST).T
            pieces += [d_rel_bias, d_final[0]]
        small_shapes[l] = [p.shape for p in pieces]
        paired = (l, pair_start(l, gb, _pack_rows(pieces, 1024)))
        tokens = [paired[1][3]]
        if in_flight is not None:
            above = in_flight[0]
            half, small_sums[above] = reduce_finish(above, in_flight[1], [dx])
            finals[above] = split_start(f"final_start_{above}", [half], 1, final_plan)
            tokens.append(finals[above][3])
            in_flight = None
    grad_x = dx[None]

    weights = dict(ffn1_norm=ffn1_norm, ffn1_w_gate=ffn1_w_gate, ffn1_w_up=ffn1_w_up, ffn1_w_down=ffn1_w_down, mix_norm=mix_norm, w_in=w_in, conv_w=conv_w, conv_b=conv_b, lru_w_a=lru_w_a, lru_b_a=lru_b_a, lru_w_x=lru_w_x, lru_b_x=lru_b_x, lru_lambda=lru_lambda, attn_sink=attn_sink, rel_bias=rel_bias, lru_out_norm=lru_out_norm, attn_out_norm=attn_out_norm, w_out=w_out, ffn2_norm=ffn2_norm, ffn2_w_gate=ffn2_w_gate, ffn2_w_up=ffn2_w_up, ffn2_w_down=ffn2_w_down, final_norm=final_norm)
    m_in = dict(ffn1_norm=m_ffn1_norm, ffn1_w_gate=m_ffn1_w_gate, ffn1_w_up=m_ffn1_w_up, ffn1_w_down=m_ffn1_w_down, mix_norm=m_mix_norm, w_in=m_w_in, conv_w=m_conv_w, conv_b=m_conv_b, lru_w_a=m_lru_w_a, lru_b_a=m_lru_b_a, lru_w_x=m_lru_w_x, lru_b_x=m_lru_b_x, lru_lambda=m_lru_lambda, attn_sink=m_attn_sink, rel_bias=m_rel_bias, lru_out_norm=m_lru_out_norm, attn_out_norm=m_attn_out_norm, w_out=m_w_out, ffn2_norm=m_ffn2_norm, ffn2_w_gate=m_ffn2_w_gate, ffn2_w_up=m_ffn2_w_up, ffn2_w_down=m_ffn2_w_down, final_norm=m_final_norm)
    v_in = dict(ffn1_norm=v_ffn1_norm, ffn1_w_gate=v_ffn1_w_gate, ffn1_w_up=v_ffn1_w_up, ffn1_w_down=v_ffn1_w_down, mix_norm=v_mix_norm, w_in=v_w_in, conv_w=v_conv_w, conv_b=v_conv_b, lru_w_a=v_lru_w_a, lru_b_a=v_lru_b_a, lru_w_x=v_lru_w_x, lru_b_x=v_lru_b_x, lru_lambda=v_lru_lambda, attn_sink=v_attn_sink, rel_bias=v_rel_bias, lru_out_norm=v_lru_out_norm, attn_out_norm=v_attn_out_norm, w_out=v_w_out, ffn2_norm=v_ffn2_norm, ffn2_w_gate=v_ffn2_w_gate, ffn2_w_up=v_ffn2_w_up, ffn2_w_down=v_ffn2_w_down, final_norm=v_final_norm)
    order = list(weights)
    large = [(name, m, 0, lay.fh, m % 3 != 2) for m, name in
             enumerate(("ffn1_w_gate", "ffn1_w_up", "ffn1_w_down", "ffn2_w_gate", "ffn2_w_up", "ffn2_w_down"))]
    large += [("w_in", lay.MIX_BLK, 0, lay.ih, True), ("w_out", lay.MIX_BLK, lay.ih, lay.oh, False)]
    as_rows = {name: [rows_major(src[name]) if flip else src[name] for src in (weights, m_in, v_in)]
               for name, _, _, _, flip in large}
    stacked = {name: tuple(lax.empty(as_rows[name][0].shape, F32) for _ in range(4)) for name, *_ in large}

    def adamw_large(l, deps=()):
        for i, (name, blk, row_off, n_half, _) in enumerate(large):
            stacked[name] = adamw_layer(gf[l], blk, row_off, n_half, l, *as_rows[name], stacked[name],
                                        deps=deps if i == 0 else ())

    last = paired[0]
    crossing = reduce_start(last, paired[1], [dx])
    for l in sorted(finals):
        gf[l], = split_wait(f"final_wait_{l}", finals[l][0], finals[l][1], finals[l][2], [crossing[3]], final_plan)
        adamw_large(l, deps=(crossing[3],))
    ready = [buf for name, *_ in large for buf in stacked[name]] if depth > 1 else []
    half, small_sums[last] = reduce_finish(last, crossing, [dx] + ready)
    gf[last], = exchange_now(f"final_now_{last}", [half], 1, final_plan)
    adamw_large(last)

    per_layer = [_unpack_rows(small_sums[l], small_shapes[l]) for l in range(depth)]
    grads = {n: jnp.stack([per_layer[l][i] for l in range(depth)]) for i, n in enumerate(layer_names)}
    grads["rel_bias"], grads["final_norm"] = per_layer[0][len(layer_names):]
    for name in ("conv_w", "lru_b_a", "lru_b_x", "lru_lambda"):
        grads[name] = lax.dynamic_slice_in_dim(grads[name], k_chip * LANE, LANE, axis=2)
    delta, new_m, new_v = {}, {}, {}
    for name, _, _, _, flip in large:
        grads[name], delta[name], new_m[name], new_v[name] = [rows_major(a) if flip else a for a in stacked[name]]
    small = [n for n in order if n not in stacked]
    packed = [_pack_rows([src[n] for n in small], 1024) for src in (weights, grads, m_in, v_in)]
    outs = adamw(*packed)
    shapes = [weights[n].shape for n in small]
    for dst, buf in zip((delta, new_m, new_v), outs):
        dst.update(zip(small, _unpack_rows(buf, shapes)))

    return (loss, grad_x, *[grads[n] for n in order], *[delta[n] for n in order],
            *[new_m[n] for n in order], *[new_v[n] for n in order])
```

```python
import functools
import math

import jax
import jax.numpy as jnp
import numpy as np
from jax import lax
from jax.experimental import pallas as pl
from jax.experimental.pallas import tpu as pltpu

BF = jnp.bfloat16
F32 = jnp.float32
SDS = jax.ShapeDtypeStruct
MESH = pl.DeviceIdType.MESH
ANY = pl.BlockSpec(memory_space=pl.ANY)

N_CHIPS = 4
N_HEADS = 8
N_KV_HEADS = 2
KV_GROUP = N_HEADS // N_KV_HEADS
HEAD_DIM = 64
BLOCK = 128
WINDOW = 128
N_BUCKETS = 32
MAX_DISTANCE = 128
LRU_C = 8.0
CONV_WIDTH = 4
LANE = 128
SUBLANES = 8
MXU_WIDTH = 256
SCAN_CHAINS = 8
EPS = 1e-6
FFN_RES = 0.5
NEG_INF = -1e30
ADAM_LR = 0.001
ADAM_B1 = 0.9
ADAM_B2 = 0.999
ADAM_EPS = 1e-08
ADAM_WD = 0.01
ADAM_STEP = 10
VMEM_LIMIT = 60000 * 1024
GELU_C = math.sqrt(2.0 / math.pi)


def dot_nn(a, b):
    return lax.dot_general(a, b, (((1,), (0,)), ((), ())), preferred_element_type=F32)


def dot_nt(a, b):
    return lax.dot_general(a, b, (((1,), (1,)), ((), ())), preferred_element_type=F32)


def dot_tn(a, b):
    return lax.dot_general(a, b, (((0,), (0,)), ((), ())), preferred_element_type=F32)


def _cparams(**kw):
    return pltpu.CompilerParams(vmem_limit_bytes=VMEM_LIMIT, **kw)


class Layout:
    MIX_BLK = 6
    BLOCKS = 7

    def __init__(self, d_model, d_ff, d_in):
        self.fh = d_ff // (2 * N_CHIPS)
        self.ih = d_in // (2 * N_CHIPS)
        self.oh = d_model // (2 * N_CHIPS)
        assert self.ih + self.oh == self.fh, "w_in^T and w_out rows must fill one ffn-sized block"
        self.rows = self.BLOCKS * self.fh


def _row_chunk(rows, target, step=16):
    best = rows
    for c in range(step, min(rows, target) + 1, step):
        if rows % c == 0:
            best = c
    return best


def _mesh_pos():
    return lax.axis_index("x"), lax.axis_index("y"), lax.axis_index("c")


def _rcopy(src, dst, ssem, rsem, dev):
    return pltpu.make_async_remote_copy(src_ref=src, dst_ref=dst, send_sem=ssem, recv_sem=rsem,
                                        device_id=dev, device_id_type=MESH)


HBM = pl.BlockSpec(memory_space=pltpu.HBM)
SEM = pl.BlockSpec(memory_space=pltpu.SEMAPHORE)
DATAFLOW = pltpu.SideEffectType.DATAFLOW_SIDE_EFFECTING


def _chip_peers():
    x, y, c = _mesh_pos()
    peers = [(1 - x, y), (x, 1 - y), (1 - x, 1 - y)]
    return x, y, c, 2 * x + y, [(px, py, 2 * px + py) for px, py in peers]


def split_start(name, bufs, n, plan):
    nb = len(bufs)

    def body(*refs):
        sends, _ = plan(refs[:nb], refs[nb], refs[nb + 1])
        for cp in sends:
            cp.start()
        refs[-1][...] = jnp.zeros_like(refs[-1])

    out = pl.pallas_call(
        body, name=name,
        out_shape=(pltpu.SemaphoreType.DMA((n,)), pltpu.SemaphoreType.DMA((n,)),
                   *[pltpu.HBM(b.shape, b.dtype) for b in bufs], SDS((8, LANE), F32)),
        in_specs=[HBM] * nb, out_specs=(SEM, SEM, *([HBM] * nb), pl.BlockSpec(memory_space=pltpu.VMEM)),
        input_output_aliases={i: 2 + i for i in range(nb)},
        compiler_params=pltpu.CompilerParams(has_side_effects=DATAFLOW),
    )(*[pltpu.with_memory_space_constraint(b, pltpu.HBM) for b in bufs])
    return out[0], out[1], list(out[2:2 + nb]), out[-1]


def split_wait(name, ssem, rsem, bufs, after, plan):
    nb = len(bufs)

    def body(*refs):
        sends, recvs = plan(refs[:nb], refs[nb], refs[nb + 1])
        for cp in recvs:
            cp.wait_recv()
        for cp in sends:
            cp.wait_send()

    out = pl.pallas_call(
        body, name=name, out_shape=tuple(pltpu.HBM(b.shape, b.dtype) for b in bufs),
        in_specs=[HBM] * nb + [SEM, SEM] + [ANY] * len(after), out_specs=tuple([HBM] * nb),
        input_output_aliases={i: i for i in range(nb)},
        compiler_params=pltpu.CompilerParams(has_side_effects=DATAFLOW),
    )(*bufs, ssem, rsem, *after)
    return list(out)


def gather_plan(refs, ssem, rsem, rows=None):
    land_ref, = refs
    _, _, c, k, peers = _chip_peers()
    part = (lambda a: a) if rows is None else (lambda a: a.at[pl.ds(rows[0], rows[1])])
    sends = [_rcopy(part(land_ref.at[k, c]), part(land_ref.at[k, c]), ssem.at[j], rsem.at[j], (px, py, c))
             for j, (px, py, _) in enumerate(peers)]
    recvs = [_rcopy(part(land_ref.at[kp, c]), part(land_ref.at[kp, c]), ssem.at[j], rsem.at[j], (px, py, c))
             for j, (px, py, kp) in enumerate(peers)]
    return sends, recvs


def reduce_plan(refs, ssem, rsem):
    cs_ref, ss_ref, p3_ref, sp3_ref = refs
    _, _, c, k, peers = _chip_peers()
    sends, recvs = [], []
    for j, (px, py, kp) in enumerate(peers):
        sends.append(_rcopy(cs_ref.at[kp], p3_ref.at[j], ssem.at[j], rsem.at[j], (px, py, c)))
        recvs.append(_rcopy(cs_ref.at[kp], p3_ref.at[j], ssem.at[j], rsem.at[j], (px, py, c)))
        sends.append(_rcopy(ss_ref, sp3_ref.at[k], ssem.at[3 + j], rsem.at[3 + j], (px, py, c)))
        recvs.append(_rcopy(ss_ref, sp3_ref.at[kp], ssem.at[3 + j], rsem.at[3 + j], (px, py, c)))
    return sends, recvs


def gather_small(sshard):
    def body(s_ref, sf_ref, lsem, ssem, rsem):
        _, _, c, k, peers = _chip_peers()
        own = pltpu.make_async_copy(s_ref, sf_ref.at[k], lsem)
        own.start()
        sends = [_rcopy(s_ref, sf_ref.at[k], ssem.at[j], rsem.at[j], (px, py, c)) for j, (px, py, _) in enumerate(peers)]
        recvs = [_rcopy(s_ref, sf_ref.at[kp], ssem.at[j], rsem.at[j], (px, py, c)) for j, (px, py, kp) in enumerate(peers)]
        for cp in sends:
            cp.start()
        for cp in recvs:
            cp.wait_recv()
        for cp in sends:
            cp.wait_send()
        own.wait()

    return pl.pallas_call(
        body, name="gather_small", out_shape=SDS((N_CHIPS,) + sshard.shape, sshard.dtype),
        in_specs=[ANY], out_specs=ANY,
        scratch_shapes=[pltpu.SemaphoreType.DMA, pltpu.SemaphoreType.DMA((3,)), pltpu.SemaphoreType.DMA((3,))],
    )(sshard)


def exchange_now(name, bufs, n, plan):
    nb = len(bufs)

    def body(*refs):
        sends, recvs = plan(refs[nb:2 * nb], refs[2 * nb], refs[2 * nb + 1])
        for cp in sends:
            cp.start()
        for cp in recvs:
            cp.wait_recv()
        for cp in sends:
            cp.wait_send()

    return list(pl.pallas_call(
        body, name=name, out_shape=tuple(SDS(b.shape, b.dtype) for b in bufs),
        in_specs=[ANY] * nb, out_specs=tuple([ANY] * nb), input_output_aliases={i: i for i in range(nb)},
        scratch_shapes=[pltpu.SemaphoreType.DMA((n,)), pltpu.SemaphoreType.DMA((n,))],
    )(*bufs))


def handover_plan(refs, ssem, rsem, rows=None):
    land_ref, = refs
    x, y, c, _, peers = _chip_peers()
    sib = (x, y, 1 - c)
    part = (lambda a: a) if rows is None else (lambda a: a.at[pl.ds(rows[0], rows[1])])
    sends = [_rcopy(part(land_ref.at[kp, c]), part(land_ref.at[kp, c]), ssem.at[j], rsem.at[j], sib)
             for j, (_, _, kp) in enumerate(peers)]
    recvs = [_rcopy(part(land_ref.at[kp, 1 - c]), part(land_ref.at[kp, 1 - c]), ssem.at[j], rsem.at[j], sib)
             for j, (_, _, kp) in enumerate(peers)]
    return sends, recvs


def pair_plan(refs, ssem, rsem):
    gb_ref, sb_ref, p_ref, sp_ref = refs
    x, y, c = _mesh_pos()
    sib = (x, y, 1 - c)
    n = gb_ref.shape[0]
    copies = [_rcopy(gb_ref.at[kk, 1 - c], p_ref.at[kk], ssem.at[kk], rsem.at[kk], sib) for kk in range(n)]
    copies.append(_rcopy(sb_ref, sp_ref, ssem.at[n], rsem.at[n], sib))
    return copies, copies


def final_plan(refs, ssem, rsem):
    gf_ref, = refs
    x, y, c = _mesh_pos()
    sib = (x, y, 1 - c)
    return ([_rcopy(gf_ref.at[c], gf_ref.at[c], ssem.at[0], rsem.at[0], sib)],
            [_rcopy(gf_ref.at[1 - c], gf_ref.at[1 - c], ssem.at[0], rsem.at[0], sib)])


def pair_sum(pos, gb, p1):
    n, _, rh, d = gb.shape
    cr = _row_chunk(rh, 1280)

    def body(pos_ref, a_ref, b_ref, o_ref):
        o_ref[...] = (a_ref[...].astype(F32) + b_ref[...].astype(F32)).astype(o_ref.dtype)

    return pl.pallas_call(
        body, name="pair_sum", out_shape=SDS((n, rh, d), gb.dtype),
        grid_spec=pltpu.PrefetchScalarGridSpec(
            num_scalar_prefetch=1, grid=(n, rh // cr),
            in_specs=[pl.BlockSpec((None, None, cr, d), lambda kk, r, pos: (kk, pos[1], r, 0)),
                      pl.BlockSpec((None, cr, d), lambda kk, r, pos: (kk, r, 0))],
            out_specs=pl.BlockSpec((None, cr, d), lambda kk, r, pos: (kk, r, 0))),
        compiler_params=_cparams(),
    )(pos, gb, p1)


def chip_sum(pos, cs, p3):
    n, rh, d = cs.shape
    cr = _row_chunk(rh, 640)

    def body(pos_ref, a_ref, b_ref, o_ref):
        acc = a_ref[...].astype(F32)
        for j in range(3):
            acc = acc + b_ref[j].astype(F32)
        o_ref[...] = acc

    return pl.pallas_call(
        body, name="chip_sum", out_shape=SDS((2, rh, d), F32),
        grid_spec=pltpu.PrefetchScalarGridSpec(
            num_scalar_prefetch=1, grid=(rh // cr,),
            in_specs=[pl.BlockSpec((None, cr, d), lambda r, pos: (pos[0], r, 0)),
                      pl.BlockSpec((3, cr, d), lambda r, pos: (0, r, 0))],
            out_specs=pl.BlockSpec((None, cr, d), lambda r, pos: (pos[1], r, 0))),
        compiler_params=_cparams(),
    )(pos, cs, p3)


def small_pair_sum(a, b):
    def body(a_ref, b_ref, o_ref):
        o_ref[...] = a_ref[...] + b_ref[...]

    return pl.pallas_call(body, name="small_pair_sum", out_shape=SDS(a.shape, a.dtype),
                          compiler_params=_cparams())(a, b)


def small_chip_sum(pos, own, p):
    ns, w = own.shape

    def body(pos_ref, own_ref, p0, p1, p2, p3, o_ref):
        k = pos_ref[0]
        acc = None
        for chip, ref in enumerate((p0, p1, p2, p3)):
            term = jnp.where(k == chip, own_ref[...], ref[...])
            acc = term if acc is None else acc + term
        o_ref[...] = acc

    def slot(chip):
        return pl.BlockSpec((None, ns, w), lambda i, pos: (jnp.where(pos[0] == chip, (chip + 1) % N_CHIPS, chip), 0, 0))

    return pl.pallas_call(
        body, name="small_chip_sum", out_shape=SDS(own.shape, own.dtype),
        grid_spec=pltpu.PrefetchScalarGridSpec(
            num_scalar_prefetch=1, grid=(1,),
            in_specs=[pl.BlockSpec((ns, w), lambda i, pos: (0, 0))] + [slot(chip) for chip in range(N_CHIPS)],
            out_specs=pl.BlockSpec((ns, w), lambda i, pos: (0, 0))),
        compiler_params=_cparams(),
    )(pos, own, p, p, p, p)


def _rms(x, g):
    rs = lax.rsqrt(jnp.mean(x * x, axis=-1, keepdims=True) + EPS)
    xh = x * rs
    return xh, rs, xh * g


def _rms_bwd(dy, xh, rs, g):
    dxh = dy * g
    dx = rs * (dxh - xh * jnp.mean(dxh * xh, axis=-1, keepdims=True))
    return dx, dy * xh


def _gelu(x):
    t = jnp.tanh(GELU_C * (x + 0.044715 * x * x * x))
    return 0.5 * x * (1.0 + t), t


def _gelu_grad(x, t):
    return 0.5 * (1.0 + t) + 0.5 * x * (1.0 - t * t) * GELU_C * (1.0 + 3.0 * 0.044715 * x * x)


def _shift_rows(v, s, n):
    if s == 0:
        return v
    t = lax.broadcasted_iota(jnp.int32, v.shape, 0)
    rolled = pltpu.roll(v, (-s) % n, 0)
    inside = (t < n - s) if s > 0 else (t >= -s)
    return jnp.where(inside, rolled, 0.0)


def _scan_rows(a_ref, u_ref, h_ref, acum_ref, reverse):
    s_len, w = a_ref.shape
    chunk = min(512, s_len)
    last = 0 if reverse else SUBLANES - 1

    def inside_vregs(ci, _):
        rows = pl.ds(pl.multiple_of(ci * chunk, chunk), chunk)
        a = a_ref[rows, :].reshape(chunk // SUBLANES, SUBLANES, w)
        u = u_ref[rows, :].reshape(chunk // SUBLANES, SUBLANES, w)
        pos = lax.broadcasted_iota(jnp.int32, (1, SUBLANES, w), 1)
        for dist in (1, 2, 4):
            ok = (pos < SUBLANES - dist) if reverse else (pos >= dist)
            shift = SUBLANES - dist if reverse else dist
            u = u + a * jnp.where(ok, pltpu.roll(u, shift, 1), 0.0)
            a = a * jnp.where(ok, pltpu.roll(a, shift, 1), 1.0)
        h_ref[rows, :] = u.reshape(chunk, w)
        acum_ref[rows, :] = a.reshape(chunk, w)
        return 0

    lax.fori_loop(0, s_len // chunk, inside_vregs, 0)

    chains = max(1, min(SCAN_CHAINS, s_len // (8 * SUBLANES)))
    seg = s_len // chains
    nvreg = seg // SUBLANES

    def step(j, carry):
        jj = (nvreg - 1 - j) if reverse else j
        out = []
        for c, (hin, ain) in enumerate(carry):
            rows = pl.ds(pl.multiple_of(c * seg + jj * SUBLANES, SUBLANES), SUBLANES)
            acc = acum_ref[rows, :]
            h = h_ref[rows, :] + acc * hin
            acc = acc * ain
            h_ref[rows, :] = h
            acum_ref[rows, :] = acc
            out.append((jnp.broadcast_to(h[last:last + 1, :], h.shape), jnp.broadcast_to(acc[last:last + 1, :], acc.shape)))
        return tuple(out)

    init = tuple((jnp.zeros((SUBLANES, w), F32), jnp.ones((SUBLANES, w), F32)) for _ in range(chains))
    ends = lax.fori_loop(0, nvreg, step, init, unroll=min(2, nvreg))
    order = range(chains - 2, -1, -1) if reverse else range(1, chains)
    inflow = jnp.zeros((1, w), F32)
    for s in order:
        h, acc = ends[s + 1 if reverse else s - 1]
        inflow = h[0:1, :] + acc[0:1, :] * inflow
        rows = pl.ds(s * seg, seg)
        h_ref[rows, :] = h_ref[rows, :] + acum_ref[rows, :] * inflow


def _width_parts(f):
    cut = (f // 2) // MXU_WIDTH * MXU_WIDTH
    return [slice(0, cut), slice(cut, f)] if cut and f % MXU_WIDTH == 0 else [slice(0, f // 2), slice(f // 2, f)]


def _w_spec(rows_half, d, blk):
    return pl.BlockSpec((N_CHIPS, 2, rows_half, d), lambda *_: (0, 0, blk, 0), pipeline_mode=pl.Buffered(1))


def ffn_forward(x, gain, wfull, lay, which, deps=(), tm=512):
    s_len, d = x.shape
    tm = min(tm, s_len)
    f = 8 * lay.fh

    def body(x_ref, g_ref, wg_ref, wu_ref, wd_ref, *rest):
        o_ref, gate_ref, up_ref = rest[len(deps):]
        x = x_ref[...]
        _, _, hn = _rms(x, g_ref[...])
        h = hn.astype(BF)
        y = jnp.zeros((tm, d), F32)
        for cols in _width_parts(f):
            gate = dot_nt(h, wg_ref[...].reshape(f, d)[cols])
            up = dot_nt(h, wu_ref[...].reshape(f, d)[cols])
            act = (gate * jax.nn.sigmoid(gate) * up).astype(BF)
            y = y + dot_nn(act, wd_ref[...].reshape(f, d)[cols])
            gate_ref[:, cols] = gate.astype(BF)
            up_ref[:, cols] = up.astype(BF)
        o_ref[...] = x + FFN_RES * y

    row = pl.BlockSpec((tm, d), lambda i: (i, 0))
    wide = pl.BlockSpec((tm, f), lambda i: (i, 0))
    return pl.pallas_call(
        body, name="ffn_forward", grid=(s_len // tm,),
        out_shape=(SDS((s_len, d), F32), SDS((s_len, f), BF), SDS((s_len, f), BF)),
        in_specs=[row, pl.BlockSpec((1, d), lambda i: (0, 0))]
        + [_w_spec(lay.fh, d, 3 * which + m) for m in range(3)] + [ANY] * len(deps),
        out_specs=(row, wide, wide), compiler_params=_cparams(),
    )(x, gain, wfull, wfull, wfull, *deps)


def ffn_backward_dx(x, gain, dout, gate_bf, up_bf, wfull, lay, which, deps=(), tm=256):
    s_len, d = x.shape
    tm = min(tm, s_len)
    f = 8 * lay.fh
    nt = s_len // tm

    def body(x_ref, g_ref, do_ref, gate_ref, up_ref, wg_ref, wu_ref, wd_ref, *rest):
        dx_ref, dg_ref, lhs_ref, rhs_ref = rest[len(deps):]
        dgate_ref, dup_ref, act_ref = lhs_ref.at[0], lhs_ref.at[1], lhs_ref.at[2]
        h_ref, df_ref = rhs_ref.at[0], rhs_ref.at[1]
        x = x_ref[...]
        g = g_ref[...]
        xh, rs, hn = _rms(x, g)
        h = hn.astype(BF)
        do = do_ref[...]
        df = (FFN_RES * do).astype(BF)
        dh = jnp.zeros((tm, d), F32)
        wd = wd_ref[...].reshape(f, d)
        parts = _width_parts(f)
        dacts = [dot_nt(df, wd[cols]) for cols in parts]
        for part, cols in enumerate(parts):
            wg = wg_ref[...].reshape(f, d)[cols]
            wu = wu_ref[...].reshape(f, d)[cols]
            gate = gate_ref[:, cols].astype(F32)
            up = up_ref[:, cols].astype(F32)
            sg = jax.nn.sigmoid(gate)
            silu = gate * sg
            dact = dacts[part]
            dup = (dact * silu).astype(BF)
            dgate = (dact * up * (sg * (1.0 + gate * (1.0 - sg)))).astype(BF)
            dh = dh + dot_nn(dgate, wg) + dot_nn(dup, wu)
            dgate_ref[:, cols] = dgate
            dup_ref[:, cols] = dup
            act_ref[:, cols] = (silu * up).astype(BF)
        dxn, dgrow = _rms_bwd(dh, xh, rs, g)
        dx_ref[...] = do + dxn

        @pl.when(pl.program_id(0) == 0)
        def _():
            dg_ref[...] = jnp.zeros_like(dg_ref)

        dg_ref[...] += jnp.sum(dgrow, axis=0, keepdims=True)
        h_ref[...] = h
        df_ref[...] = df

    row = pl.BlockSpec((tm, d), lambda i: (i, 0))
    wide = pl.BlockSpec((tm, f), lambda i: (i, 0))
    vec = pl.BlockSpec((1, d), lambda i: (0, 0))
    return pl.pallas_call(
        body, name="ffn_backward_dx", grid=(nt,),
        out_shape=(SDS((s_len, d), F32), SDS((1, d), F32), SDS((3, s_len, f), BF), SDS((2, s_len, d), BF)),
        in_specs=[row, vec, row, wide, wide] + [_w_spec(lay.fh, d, 3 * which + m) for m in range(3)] + [ANY] * len(deps),
        out_specs=(row, vec, pl.BlockSpec((3, tm, f), lambda i: (0, i, 0)), pl.BlockSpec((2, tm, d), lambda i: (0, i, 0))),
        compiler_params=_cparams(),
    )(x, gain, dout, gate_bf, up_bf, wfull, wfull, wfull, *deps)


def weight_grad_tn(lhs, rhs, gb, lay, blk0, tk=4096):
    nmat, s_len, f = lhs.shape
    tk = min(tk, s_len)
    d = rhs.shape[2]
    fc = f // 2
    nk = s_len // tk

    def body(a_ref, b_ref, gb_ref, o_ref, acc):
        kt = pl.program_id(2)

        @pl.when(kt == 0)
        def _():
            acc[...] = jnp.zeros_like(acc)

        acc[...] += dot_tn(a_ref[...], b_ref[...])

        @pl.when(kt == nk - 1)
        def _():
            for p in range(2):
                for q in range(2):
                    o_ref[p, q] = acc[pl.ds((2 * p + q) * lay.fh, lay.fh), :].astype(o_ref.dtype)

    return pl.pallas_call(
        body, name="weight_grad_tn", grid=(nmat, 2, nk), out_shape=SDS(gb.shape, gb.dtype),
        in_specs=[pl.BlockSpec((None, tk, fc), lambda m, j, kt: (m, kt, j)),
                  pl.BlockSpec((None, tk, d), lambda m, j, kt: (jnp.where(m == nmat - 1, 1, 0), kt, 0)), ANY],
        out_specs=pl.BlockSpec((2, 2, lay.fh, d), lambda m, j, kt: (j, 0, blk0 + m, 0)),
        scratch_shapes=[pltpu.VMEM((fc, d), F32)],
        input_output_aliases={2: 0}, compiler_params=_cparams(),
    )(lhs, rhs, gb)


def _lane_blocks(v):
    return [v[:, j * LANE:(j + 1) * LANE] for j in range(v.shape[1] // LANE)]


def _join_lane_blocks(ref):
    return jnp.concatenate([ref[j] for j in range(ref.shape[0])], axis=1)


def _cbm_spec(nblk, rows, first=0):
    return pl.BlockSpec((nblk, rows, LANE), lambda i: (first // nblk, i, 0))


def mix_project(x, gain, wfull, lay, lw, att, tm=512):
    s_len, d = x.shape
    tm = min(tm, s_len)
    d_in = 8 * lay.ih
    kvw = (d_in - 2 * lw - att) // 2
    ncol = (2 * lw + 2 * kvw) // LANE

    def body(x_ref, g_ref, w_ref, o_ref, qt_ref, vt_ref):
        _, _, hn = _rms(x_ref[...], g_ref[...])
        h = hn.astype(BF)
        w = w_ref[:, :, :lay.ih, :].reshape(d_in, d)
        pieces = _lane_blocks(dot_nt(h, w[:2 * lw])) + _lane_blocks(dot_nt(h, w[2 * lw + att:]))
        for j, piece in enumerate(pieces):
            o_ref[j] = piece
        qt_ref[...] = dot_nt(w[2 * lw:2 * lw + att], h)
        vt_ref[...] = dot_nt(w[2 * lw + att + kvw:], h)

    return pl.pallas_call(
        body, name="mix_project", grid=(s_len // tm,),
        out_shape=(SDS((ncol, s_len, LANE), F32), SDS((att, s_len), F32), SDS((kvw, s_len), F32)),
        in_specs=[pl.BlockSpec((tm, d), lambda i: (i, 0)), pl.BlockSpec((1, d), lambda i: (0, 0)),
                  _w_spec(lay.fh, d, lay.MIX_BLK)],
        out_specs=(_cbm_spec(ncol, tm), pl.BlockSpec((att, tm), lambda i: (0, i)), pl.BlockSpec((kvw, tm), lambda i: (0, i))),
        compiler_params=_cparams(),
    )(x, gain, wfull)


def mix_project_backward(x, gain, dout, dxr, dgt, dqt, dkv, dwout, wfull, gb, lay, tm=512):
    s_len, d = x.shape
    tm = min(tm, s_len)
    d_in = 8 * lay.ih
    nt = s_len // tm
    kvw = dkv.shape[1]
    att = dqt.shape[0]
    nlru = (dxr.shape[0] + dgt.shape[0]) * LANE

    def body(x_ref, g_ref, do_ref, dxr_ref, dgt_ref, dqt_ref, dkv_ref, dwo_ref, w_ref, gb_ref, dx_ref, dg_ref, o_ref, acc):
        i = pl.program_id(0)
        g = g_ref[...]
        xh, rs, hn = _rms(x_ref[...], g)
        h = hn.astype(BF)
        w = w_ref[:, :, :lay.ih, :].reshape(d_in, d)
        dlru = jnp.concatenate([_join_lane_blocks(dxr_ref), _join_lane_blocks(dgt_ref)], axis=1).astype(BF)
        dqt = dqt_ref[...].astype(BF)
        dkv = dkv_ref[...].astype(BF)
        dh = dot_nn(dlru, w[:nlru]) + dot_tn(dqt, w[nlru:nlru + att]) + dot_nn(dkv, w[nlru + att:])
        dxn, dgrow = _rms_bwd(dh, xh, rs, g)
        dx_ref[...] = do_ref[...] + dxn

        @pl.when(i == 0)
        def _():
            dg_ref[...] = jnp.zeros_like(dg_ref)
            acc[...] = jnp.zeros_like(acc)

        dg_ref[...] += jnp.sum(dgrow, axis=0, keepdims=True)
        acc[0:nlru, :] += dot_tn(dlru, h)
        acc[nlru:nlru + att, :] += dot_nn(dqt, h)
        acc[nlru + att:, :] += dot_tn(dkv, h)

        @pl.when(i == nt - 1)
        def _():
            for p in range(N_CHIPS):
                for q in range(2):
                    o_ref[p, q, :lay.ih, :] = acc[pl.ds((2 * p + q) * lay.ih, lay.ih), :].astype(o_ref.dtype)
            o_ref[:, :, lay.ih:, :] = dwo_ref[...]

    row = pl.BlockSpec((tm, d), lambda i: (i, 0))
    vec = pl.BlockSpec((1, d), lambda i: (0, 0))
    return pl.pallas_call(
        body, name="mix_project_backward", grid=(nt,),
        out_shape=(SDS((s_len, d), F32), SDS((1, d), F32), SDS(gb.shape, gb.dtype)),
        in_specs=[row, vec, row, _cbm_spec(dxr.shape[0], tm), _cbm_spec(dgt.shape[0], tm),
                  pl.BlockSpec((att, tm), lambda i: (0, i)), pl.BlockSpec((tm, kvw), lambda i: (i, 0)),
                  pl.BlockSpec(dwout.shape, lambda i: (0, 0, 0, 0)), _w_spec(lay.fh, d, lay.MIX_BLK), ANY],
        out_specs=(row, vec, pl.BlockSpec((N_CHIPS, 2, lay.fh, d), lambda i: (0, 0, lay.MIX_BLK, 0))),
        scratch_shapes=[pltpu.VMEM((d_in, d), F32)],
        input_output_aliases={9: 2}, compiler_params=_cparams(),
    )(x, gain, dout, dxr, dgt, dqt, dkv, dwout, wfull, gb)


def _neg_expm1(x):
    series = -x * (1.0 + x * (0.5 + x * (1.0 / 6.0 + x * (1.0 / 24.0))))
    return jnp.where(x > -0.03, series, 1.0 - jnp.exp(x))


def _decay(pv_ref, direction, r):
    lam = pv_ref[5 + direction:6 + direction, :]
    sp = jnp.maximum(-lam, 0.0) + jnp.log1p(jnp.exp(-jnp.abs(lam)))
    log_a = -LRU_C * sp * r
    return jnp.exp(log_a), jnp.sqrt(_neg_expm1(2.0 * log_a)), sp


def _lru_gates(xc, wb_ref, pv_ref, direction):
    xcb = xc.astype(BF)
    sigmoid = lambda z: 0.5 * jnp.tanh(0.5 * z) + 0.5
    r = sigmoid(dot_nn(xcb, wb_ref[2 * direction]) + pv_ref[1 + direction:2 + direction, :])
    i = sigmoid(dot_nn(xcb, wb_ref[2 * direction + 1]) + pv_ref[3 + direction:4 + direction, :])
    a, mult, sp = _decay(pv_ref, direction, r)
    return xcb, r, i, a, mult, sp


def _conv_rows(xr, cv_ref, bias, n):
    acc = bias + cv_ref[0:1, :] * _shift_rows(xr, -2, n)
    for j in range(1, CONV_WIDTH):
        acc = acc + cv_ref[j:j + 1, :] * _shift_rows(xr, j - 2, n)
    return acc


def lru_forward(proj, cvec, pvec, wblk, lw, deps=(), ch=512):
    s_len = proj.shape[1]
    ncb = lw // LANE
    ch = min(ch, s_len)
    nchunk = s_len // ch

    def body(xr_ref, gt_ref, cv_ref, pv_ref, wb_ref, *rest):
        y_ref, hs_ref, xc_s, a_s, u_s, acum_s = rest[len(deps):]
        xc_s[...] = _conv_rows(xr_ref[...], cv_ref, pv_ref[0:1, :], s_len)
        for direction in range(2):
            def fill(ci, _):
                rows = pl.ds(pl.multiple_of(ci * ch, ch), ch)
                xc = xc_s[rows, :]
                _, _, i, a, mult, _ = _lru_gates(xc, wb_ref, pv_ref, direction)
                a_s[rows, :] = a
                u_s[rows, :] = mult * (i * xc)
                return 0

            lax.fori_loop(0, nchunk, fill, 0)
            _scan_rows(a_s, u_s, hs_ref.at[direction], acum_s, reverse=direction == 1)

        def out(ci, _):
            rows = pl.ds(pl.multiple_of(ci * ch, ch), ch)
            gl, _ = _gelu(gt_ref[rows, :])
            y_ref[rows, :] = gl * (hs_ref[0, rows, :] + hs_ref[1, rows, :])
            return 0

        lax.fori_loop(0, nchunk, out, 0)

    col = lambda off: pl.BlockSpec((None, s_len, LANE), lambda cb: (off + cb, 0, 0))
    return pl.pallas_call(
        body, name="lru_forward", grid=(ncb,),
        out_shape=(SDS((ncb, s_len, LANE), F32), SDS((2, ncb, s_len, LANE), F32)),
        in_specs=[col(0), col(ncb), pl.BlockSpec((8, LANE), lambda cb: (0, cb)), pl.BlockSpec((8, LANE), lambda cb: (0, cb)),
                  pl.BlockSpec((4, None, LANE, LANE), lambda cb: (0, cb, 0, 0))] + [ANY] * len(deps),
        out_specs=(col(0), pl.BlockSpec((2, None, s_len, LANE), lambda cb: (0, cb, 0, 0))),
        scratch_shapes=[pltpu.VMEM((s_len, LANE), F32)] * 4, compiler_params=_cparams(),
    )(proj, proj, cvec, pvec, wblk, *deps)


def lru_backward(proj, hs, dy, cvec, pvec, wblk, lw, ch=512):
    s_len = proj.shape[1]
    ncb = lw // LANE
    ch = min(ch, s_len)
    nchunk = s_len // ch

    def body(xr_ref, gt_ref, hs_ref, dy_ref, cv_ref, pv_ref, wb_ref, dxr_ref, dgt_ref, dcv_ref, dpv_ref, dwb_ref,
             xc_s, a_s, r_s, i_s, dh_s, lam_s, hp_s, dxc_s, acum_s):
        xr = xr_ref[...]
        xc_s[...] = _conv_rows(xr, cv_ref, pv_ref[0:1, :], s_len)
        dxc_s[...] = jnp.zeros_like(dxc_s)
        dpv_ref[...] = jnp.zeros_like(dpv_ref)
        dwb_ref[...] = jnp.zeros_like(dwb_ref)

        def head(ci, _):
            rows = pl.ds(pl.multiple_of(ci * ch, ch), ch)
            gt = gt_ref[rows, :]
            gl, t = _gelu(gt)
            dy = dy_ref[rows, :]
            dh_s[rows, :] = dy * gl
            dgt_ref[rows, :] = dy * (hs_ref[0, rows, :] + hs_ref[1, rows, :]) * _gelu_grad(gt, t)
            return 0

        lax.fori_loop(0, nchunk, head, 0)

        for direction in range(2):
            def fill(ci, _):
                rows = pl.ds(pl.multiple_of(ci * ch, ch), ch)
                _, r, i, a, _, _ = _lru_gates(xc_s[rows, :], wb_ref, pv_ref, direction)
                a_s[rows, :] = a
                r_s[rows, :] = r
                i_s[rows, :] = i
                return 0

            lax.fori_loop(0, nchunk, fill, 0)
            toward = 1 if direction == 0 else -1
            hp_s[...] = _shift_rows(a_s[...], toward, s_len)
            _scan_rows(hp_s, dh_s, lam_s, acum_s, reverse=direction == 0)
            hp_s[...] = _shift_rows(hs_ref[direction], -toward, s_len)

            def grads(ci, _):
                rows = pl.ds(pl.multiple_of(ci * ch, ch), ch)
                xc = xc_s[rows, :]
                xcb = xc.astype(BF)
                r, i = r_s[rows, :], i_s[rows, :]
                a, mult, sp = _decay(pv_ref, direction, r)
                du = lam_s[rows, :]
                da = du * hp_s[rows, :]
                dmult = du * i * xc
                di = du * mult * xc
                dlog_a = (da - dmult * a / mult) * a
                dr = dlog_a * (-LRU_C * sp)
                dza = dr * r * (1.0 - r)
                dzx = di * i * (1.0 - i)
                dzab = dza.astype(BF)
                dzxb = dzx.astype(BF)
                dxc_s[rows, :] += (du * mult * i + dot_nt(dzab, wb_ref[2 * direction])
                                   + dot_nt(dzxb, wb_ref[2 * direction + 1]))
                dwb_ref[2 * direction] += dot_tn(xcb, dzab)
                dwb_ref[2 * direction + 1] += dot_tn(xcb, dzxb)
                dpv_ref[1 + direction:2 + direction, :] += jnp.sum(dza, axis=0, keepdims=True)
                dpv_ref[3 + direction:4 + direction, :] += jnp.sum(dzx, axis=0, keepdims=True)
                dpv_ref[5 + direction:6 + direction, :] += jnp.sum(dlog_a * (-LRU_C * r), axis=0, keepdims=True)
                return 0

            lax.fori_loop(0, nchunk, grads, 0)

        for direction in range(2):
            lam = pv_ref[5 + direction:6 + direction, :]
            dpv_ref[5 + direction:6 + direction, :] = dpv_ref[5 + direction:6 + direction, :] * (-jax.nn.sigmoid(-lam))
        dxc = dxc_s[...]
        dpv_ref[0:1, :] = jnp.sum(dxc, axis=0, keepdims=True)
        dxr = cv_ref[0:1, :] * _shift_rows(dxc, 2, s_len)
        for j in range(1, CONV_WIDTH):
            dxr = dxr + cv_ref[j:j + 1, :] * _shift_rows(dxc, 2 - j, s_len)
        dxr_ref[...] = dxr
        dcv_ref[...] = jnp.zeros_like(dcv_ref)
        for j in range(CONV_WIDTH):
            dcv_ref[j:j + 1, :] = jnp.sum(dxc * _shift_rows(xr, j - 2, s_len), axis=0, keepdims=True)

    col = lambda off: pl.BlockSpec((None, s_len, LANE), lambda cb: (off + cb, 0, 0))
    own = col(0)
    small = pl.BlockSpec((8, LANE), lambda cb: (0, cb))
    wspec = pl.BlockSpec((4, None, LANE, LANE), lambda cb: (0, cb, 0, 0))
    return pl.pallas_call(
        body, name="lru_backward", grid=(ncb,),
        out_shape=(SDS((ncb, s_len, LANE), F32), SDS((ncb, s_len, LANE), F32), SDS((8, lw), F32), SDS((8, lw), F32),
                   SDS(wblk.shape, F32)),
        in_specs=[col(0), col(ncb), pl.BlockSpec((2, None, s_len, LANE), lambda cb: (0, cb, 0, 0)), own, small, small, wspec],
        out_specs=(own, own, small, small, wspec),
        scratch_shapes=[pltpu.VMEM((s_len, LANE), F32)] * 9, compiler_params=_cparams(),
    )(proj, proj, hs, dy, cvec, pvec, wblk)


def _window_specs(s_len, first, width=None):
    nb = s_len // BLOCK
    where = (lambda n: jnp.maximum(n - 1, 0), lambda n: n, lambda n: jnp.minimum(n + 1, nb - 1))
    if width is None:
        return [pl.BlockSpec((None, BLOCK, LANE), lambda n, f=f: (first, f(n), 0)) for f in where]
    return [pl.BlockSpec((width, BLOCK), lambda n, f=f: (0, f(n))) for f in where]


def _stack_heads(v, kh):
    return jnp.concatenate([v[(kh * KV_GROUP + g) * HEAD_DIM:(kh * KV_GROUP + g + 1) * HEAD_DIM, :]
                            for g in range(KV_GROUP)], axis=1)


def _unstack_heads(ref, kh, v):
    for g in range(KV_GROUP):
        h = kh * KV_GROUP + g
        ref[h * HEAD_DIM:(h + 1) * HEAD_DIM, :] = v[:, g * BLOCK:(g + 1) * BLOCK]


def _key_exists(n, nb):
    j = lax.broadcasted_iota(jnp.int32, (3 * BLOCK, 1), 0)
    return ((n > 0) | (j >= BLOCK)) & ((n < nb - 1) | (j < 2 * BLOCK))


def _attn_probs(qs, kcat, bias_g, sink_g, key_ok):
    logits = jnp.where(key_ok, dot_nn(kcat, qs) + bias_g, NEG_INF)
    m = jnp.maximum(jnp.max(logits, axis=0, keepdims=True), sink_g)
    p = jnp.exp(logits - m)
    es = jnp.exp(sink_g - m)
    inv = 1.0 / (jnp.sum(p, axis=0, keepdims=True) + es)
    return p * inv, es * inv


def attention_forward(qt, proj, vt, bias, sink, kblk):
    att, s_len = qt.shape
    kvw = vt.shape[0]
    nb = s_len // BLOCK

    def body(q_ref, kp_ref, kc_ref, kn_ref, vp_ref, vc_ref, vn_ref, b_ref, s_ref, o_ref):
        n = pl.program_id(0)
        q = q_ref[...]
        key_ok = _key_exists(n, nb)
        kall = jnp.concatenate([kp_ref[...], kc_ref[...], kn_ref[...]], axis=0).astype(BF)
        vall = jnp.concatenate([vp_ref[...], vc_ref[...], vn_ref[...]], axis=1).astype(BF)
        for kh in range(N_KV_HEADS):
            qs = (_stack_heads(q, kh) * (HEAD_DIM ** -0.5)).astype(BF)
            p, _ = _attn_probs(qs, kall[:, kh * HEAD_DIM:(kh + 1) * HEAD_DIM], b_ref[kh], s_ref[kh, 0:1, :], key_ok)
            _unstack_heads(o_ref, kh, dot_nn(vall[kh * HEAD_DIM:(kh + 1) * HEAD_DIM, :], p.astype(BF)))

    blk = pl.BlockSpec((att, BLOCK), lambda n: (0, n))
    return pl.pallas_call(
        body, name="attention_forward", grid=(nb,), out_shape=SDS((att, s_len), F32),
        in_specs=[blk] + _window_specs(s_len, kblk) + _window_specs(s_len, 0, kvw)
        + [pl.BlockSpec(bias.shape, lambda n: (0, 0, 0)), pl.BlockSpec(sink.shape, lambda n: (0, 0, 0))],
        out_specs=blk, compiler_params=_cparams(),
    )(qt, proj, proj, proj, vt, vt, vt, bias, sink)


def attention_backward(qt, proj, y_att, dy, bias, sink, kblk):
    att, s_len = qt.shape
    nb = s_len // BLOCK
    kvw = N_KV_HEADS * HEAD_DIM

    def body(q_ref, kp_ref, kc_ref, kn_ref, vp_ref, vc_ref, vn_ref, o_ref, do_ref, b_ref, s_ref,
             dq_ref, dkv_ref, db_ref, ds_ref):
        n = pl.program_id(0)

        @pl.when(n == 0)
        def _():
            dkv_ref[...] = jnp.zeros_like(dkv_ref)
            db_ref[...] = jnp.zeros_like(db_ref)
            ds_ref[...] = jnp.zeros_like(ds_ref)

        q = q_ref[...]
        o = o_ref[...]
        do = do_ref[...]
        kall = jnp.concatenate([kp_ref[...], kc_ref[...], kn_ref[...]], axis=0).astype(BF)
        vall = jnp.concatenate([vp_ref[...], vc_ref[...], vn_ref[...]], axis=0).astype(BF)
        key_ok = _key_exists(n, nb)
        dks, dvs = [], []
        for kh in range(N_KV_HEADS):
            kcat = kall[:, kh * HEAD_DIM:(kh + 1) * HEAD_DIM]
            vcat = vall[:, kh * HEAD_DIM:(kh + 1) * HEAD_DIM]
            qs = (_stack_heads(q, kh) * (HEAD_DIM ** -0.5)).astype(BF)
            p, ps = _attn_probs(qs, kcat, b_ref[kh], s_ref[kh, 0:1, :], key_ok)
            dos = _stack_heads(do, kh)
            dosb = dos.astype(BF)
            delta = jnp.sum(dos * _stack_heads(o, kh), axis=0, keepdims=True)
            dlog = p * (dot_nn(vcat, dosb) - delta)
            dlogb = dlog.astype(BF)
            db_ref[kh] += dlog
            ds_ref[kh] += jnp.broadcast_to(-ps * delta, ds_ref.shape[1:])
            _unstack_heads(dq_ref, kh, dot_tn(kcat, dlogb) * (HEAD_DIM ** -0.5))
            dks.append(dot_nt(dlogb, qs))
            dvs.append(dot_nt(p.astype(BF), dosb))
        dkv = jnp.concatenate(dks + dvs, axis=1)
        starts = [jnp.maximum(n - 1, 0), n, jnp.minimum(n + 1, nb - 1)]
        for b, st in enumerate(starts):
            rows = pl.ds(pl.multiple_of(st * BLOCK, BLOCK), BLOCK)
            dkv_ref[rows, :] += dkv[b * BLOCK:(b + 1) * BLOCK, :]

    blk = pl.BlockSpec((att, BLOCK), lambda n: (0, n))
    whole = lambda a: pl.BlockSpec(a.shape, lambda n: (0, 0, 0))
    return pl.pallas_call(
        body, name="attention_backward", grid=(nb,),
        out_shape=(SDS((att, s_len), F32), SDS((s_len, 2 * kvw), F32), SDS(bias.shape, F32), SDS(sink.shape, F32)),
        in_specs=[blk] + _window_specs(s_len, kblk) + _window_specs(s_len, kblk + 1) + [blk, blk, whole(bias), whole(sink)],
        out_specs=(blk, pl.BlockSpec((s_len, 2 * kvw), lambda n: (0, 0)), whole(bias), whole(sink)),
        compiler_params=_cparams(),
    )(qt, proj, proj, proj, proj, proj, proj, y_att, dy, bias, sink)


def _rms_cols(x, g):
    rs = lax.rsqrt(jnp.mean(x * x, axis=0, keepdims=True) + EPS)
    xh = x * rs
    return xh, rs, xh * g


def _rms_cols_bwd(dy, xh, rs, g):
    dxh = dy * g
    dx = rs * (dxh - xh * jnp.mean(dxh * xh, axis=0, keepdims=True))
    return dx, dy * xh


def mix_output(x, y_rec, y_att, g_rec, g_att, wfull, lay, tm=512):
    s_len, d = x.shape
    tm = min(tm, s_len)
    lw = y_rec.shape[0] * LANE
    att = y_att.shape[0]

    def body(x_ref, yr_ref, ya_ref, gr_ref, ga_ref, w_ref, o_ref):
        _, _, nr = _rms(_join_lane_blocks(yr_ref), gr_ref[...])
        _, _, na = _rms_cols(ya_ref[...], ga_ref[...])
        w = w_ref[:, :, lay.ih:, :].reshape(d, d)
        o_ref[...] = x_ref[...] + dot_nn(nr.astype(BF), w[:lw]) + dot_tn(na.astype(BF), w[lw:])

    row = pl.BlockSpec((tm, d), lambda i: (i, 0))
    return pl.pallas_call(
        body, name="mix_output", grid=(s_len // tm,), out_shape=SDS((s_len, d), F32),
        in_specs=[row, _cbm_spec(lw // LANE, tm), pl.BlockSpec((att, tm), lambda i: (0, i)),
                  pl.BlockSpec((1, lw), lambda i: (0, 0)), pl.BlockSpec((att, 1), lambda i: (0, 0)),
                  _w_spec(lay.fh, d, lay.MIX_BLK)],
        out_specs=row, compiler_params=_cparams(),
    )(x, y_rec, y_att, g_rec, g_att, wfull)


def mix_output_backward(dout, y_rec, y_att, g_rec, g_att, wfull, lay, deps=(), tm=1024):
    s_len, d = dout.shape
    tm = min(tm, s_len)
    lw = y_rec.shape[0] * LANE
    att = y_att.shape[0]
    nt = s_len // tm

    def body(do_ref, yr_ref, ya_ref, gr_ref, ga_ref, w_ref, *rest):
        dyr_ref, dya_ref, dgr_ref, dga_ref, o_ref, acc = rest[len(deps):]
        i = pl.program_id(0)
        gr = gr_ref[...]
        ga = ga_ref[...]
        xhr, rsr, nr = _rms(_join_lane_blocks(yr_ref), gr)
        xha, rsa, na = _rms_cols(ya_ref[...], ga)
        dob = do_ref[...].astype(BF)
        w = w_ref[:, :, lay.ih:, :].reshape(d, d)
        dyr, dgr_row = _rms_bwd(dot_nt(dob, w[:lw]), xhr, rsr, gr)
        dya, dga_col = _rms_cols_bwd(dot_nt(w[lw:], dob), xha, rsa, ga)
        for j, piece in enumerate(_lane_blocks(dyr)):
            dyr_ref[j] = piece
        dya_ref[...] = dya

        @pl.when(i == 0)
        def _():
            dgr_ref[...] = jnp.zeros_like(dgr_ref)
            dga_ref[...] = jnp.zeros_like(dga_ref)
            acc[...] = jnp.zeros_like(acc)

        dgr_ref[...] += jnp.sum(dgr_row, axis=0, keepdims=True)
        dga_ref[...] += jnp.sum(dga_col, axis=1, keepdims=True)
        acc[0:lw, :] += dot_tn(nr.astype(BF), dob)
        acc[lw:, :] += dot_nn(na.astype(BF), dob)

        @pl.when(i == nt - 1)
        def _():
            for p in range(N_CHIPS):
                for q in range(2):
                    o_ref[p, q] = acc[pl.ds((2 * p + q) * lay.oh, lay.oh), :].astype(o_ref.dtype)

    row = pl.BlockSpec((tm, d), lambda i: (i, 0))
    return pl.pallas_call(
        body, name="mix_output_backward", grid=(nt,),
        out_shape=(SDS(y_rec.shape, F32), SDS(y_att.shape, F32), SDS((1, lw), F32), SDS((att, 1), F32),
                   SDS((N_CHIPS, 2, lay.oh, d), BF)),
        in_specs=[row, _cbm_spec(lw // LANE, tm), pl.BlockSpec((att, tm), lambda i: (0, i)),
                  pl.BlockSpec((1, lw), lambda i: (0, 0)), pl.BlockSpec((att, 1), lambda i: (0, 0)),
                  _w_spec(lay.fh, d, lay.MIX_BLK)] + [ANY] * len(deps),
        out_specs=(_cbm_spec(lw // LANE, tm), pl.BlockSpec((att, tm), lambda i: (0, i)),
                   pl.BlockSpec((1, lw), lambda i: (0, 0)), pl.BlockSpec((att, 1), lambda i: (0, 0)),
                   pl.BlockSpec((N_CHIPS, 2, lay.oh, d), lambda i: (0, 0, 0, 0))),
        scratch_shapes=[pltpu.VMEM((d, d), F32)], compiler_params=_cparams(),
    )(dout, y_rec, y_att, g_rec, g_att, wfull, *deps)


def loss_head(x, gain, target, tm=512):
    s_len, d = x.shape
    tm = min(tm, s_len)

    def body(x_ref, g_ref, t_ref, dx_ref, dg_ref, loss_ref):
        g = g_ref[...]
        xh, rs, y = _rms(x_ref[...], g)
        err = y - t_ref[...]

        @pl.when(pl.program_id(0) == 0)
        def _():
            dg_ref[...] = jnp.zeros_like(dg_ref)
            loss_ref[...] = jnp.zeros_like(loss_ref)

        part = 0.5 * jnp.sum(jnp.mean(err * err, axis=-1, keepdims=True), axis=0, keepdims=True)
        loss_ref[...] += jnp.broadcast_to(part, loss_ref.shape)
        dx, dgrow = _rms_bwd(err * (1.0 / d), xh, rs, g)
        dx_ref[...] = dx
        dg_ref[...] += jnp.sum(dgrow, axis=0, keepdims=True)

    row = pl.BlockSpec((tm, d), lambda i: (i, 0))
    vec = pl.BlockSpec((1, d), lambda i: (0, 0))
    return pl.pallas_call(
        body, name="loss_head", grid=(s_len // tm,),
        out_shape=(SDS((s_len, d), F32), SDS((1, d), F32), SDS((8, LANE), F32)),
        in_specs=[row, vec, row], out_specs=(row, vec, pl.BlockSpec((8, LANE), lambda i: (0, 0))),
        compiler_params=_cparams(),
    )(x, gain, target)


def _adamw_update(w, g, m, v):
    m = ADAM_B1 * m + (1.0 - ADAM_B1) * g
    v = ADAM_B2 * v + (1.0 - ADAM_B2) * (g * g)
    m_hat = m / (1.0 - ADAM_B1 ** ADAM_STEP)
    v_hat = v / (1.0 - ADAM_B2 ** ADAM_STEP)
    return -ADAM_LR * (m_hat / (jnp.sqrt(v_hat) + ADAM_EPS) + ADAM_WD * w), m, v


def adamw(w, g, m, v, tr=512):
    rows, cols = w.shape
    tr = _row_chunk(rows, tr, 8)

    def body(w_ref, g_ref, m_ref, v_ref, d_ref, nm_ref, nv_ref):
        d_ref[...], nm_ref[...], nv_ref[...] = _adamw_update(w_ref[...], g_ref[...], m_ref[...], v_ref[...])

    blk = pl.BlockSpec((tr, cols), lambda i: (i, 0))
    return pl.pallas_call(
        body, name="adamw", grid=(rows // tr,), out_shape=(SDS(w.shape, F32),) * 3,
        in_specs=[blk] * 4, out_specs=(blk,) * 3, compiler_params=_cparams(),
    )(w, g, m, v)


def adamw_layer(gf, blk, row_off, n_half, l, w, m, v, outs, deps=()):
    fh = gf.shape[1] // Layout.BLOCKS
    d = gf.shape[2]
    nd = len(deps)

    def body(gf_ref, w_ref, m_ref, v_ref, *rest):
        g_ref, d_ref, nm_ref, nv_ref = rest[4 + nd:]
        g = gf_ref[row_off:row_off + n_half, :]
        g_ref[...] = g
        d_ref[...], nm_ref[...], nv_ref[...] = _adamw_update(w_ref[...], g, m_ref[...], v_ref[...])

    gspec = pl.BlockSpec((None, fh, d), lambda h: (h, blk, 0))
    wspec = pl.BlockSpec((None, n_half, d), lambda h: (l, h, 0))
    return pl.pallas_call(
        body, name="adamw_layer", grid=(2,), out_shape=tuple(SDS(o.shape, o.dtype) for o in outs),
        in_specs=[gspec, wspec, wspec, wspec] + [ANY] * (4 + nd), out_specs=(wspec,) * 4,
        input_output_aliases={4 + i: i for i in range(4)}, compiler_params=_cparams(),
    )(gf, w, m, v, *outs, *deps)


def pack_weight(pos, land, blk, l, w, extra=None, deps=()):
    fh, d = land.shape[2] // Layout.BLOCKS, land.shape[3]
    nd = len(deps)

    def body(pos_ref, w_ref, *rest):
        o_ref = rest[-1]
        a = w_ref[...].astype(BF)
        n = a.shape[0] // 2
        for h in range(2):
            o_ref[h, 0:n, :] = a[h * n:(h + 1) * n]
        if extra is not None:
            b = rest[0][...].astype(BF)
            nb = b.shape[0] // 2
            for h in range(2):
                o_ref[h, n:n + nb, :] = b[h * nb:(h + 1) * nb]

    def whole(a):
        return pl.BlockSpec((None,) + a.shape[1:], lambda i, p: (l, 0, 0))

    ins = [w] + ([extra] if extra is not None else [])
    return pl.pallas_call(
        body, name="pack_weight", out_shape=SDS(land.shape, land.dtype),
        grid_spec=pltpu.PrefetchScalarGridSpec(
            num_scalar_prefetch=1, grid=(1,),
            in_specs=[whole(a) for a in ins] + [ANY] * (1 + nd),
            out_specs=pl.BlockSpec((None, 2, fh, d), lambda i, p: (p[0], 0, blk, 0))),
        input_output_aliases={1 + len(ins): 0}, compiler_params=_cparams(),
    )(pos, *ins, land, *deps)


def _rows_of(shape, width):
    return -(-int(np.prod(shape)) // (SUBLANES * width)) * SUBLANES


def _pack_rows(arrays, width):
    parts = []
    for a in arrays:
        flat = a.reshape(-1).astype(F32)
        r = _rows_of(a.shape, width)
        parts.append(jnp.pad(flat, (0, r * width - flat.shape[0])).reshape(r, width))
    return jnp.concatenate(parts, axis=0)


def _unpack_rows(buf, shapes):
    out, row = [], 0
    for shp in shapes:
        r = _rows_of(shp, buf.shape[1])
        out.append(buf[row:row + r].reshape(-1)[:int(np.prod(shp))].reshape(shp))
        row += r
    return out


def _t5_buckets(rel):
    half = N_BUCKETS // 2
    max_exact = half // 2
    ret = (rel > 0).astype(jnp.int32) * half
    n = jnp.abs(rel)
    n_f = jnp.maximum(n, 1).astype(F32)
    large = max_exact + (jnp.log(n_f / max_exact) / math.log(MAX_DISTANCE / max_exact) * (half - max_exact)).astype(jnp.int32)
    large = jnp.minimum(large, half - 1)
    return ret + jnp.where(n < max_exact, n, large)


def _band_buckets():
    t = jnp.arange(BLOCK)[:, None]
    j = jnp.arange(3 * BLOCK)[None, :]
    rel = j - BLOCK - t
    return _t5_buckets(rel), jnp.abs(rel) <= WINDOW


def _block_diag_pairs(w):
    depth, two, nblk, bw, _ = w.shape
    pairs = w.reshape(depth, two, nblk // 2, 2, bw, bw)
    z = jnp.zeros_like(pairs[:, :, :, 0])
    top = jnp.concatenate([pairs[:, :, :, 0], z], axis=-1)
    bot = jnp.concatenate([z, pairs[:, :, :, 1]], axis=-1)
    return jnp.concatenate([top, bot], axis=-2)


def _diag_blocks(dw):
    bw = dw.shape[-1] // 2
    a = dw[:, :, :bw, :bw]
    b = dw[:, :, bw:, bw:]
    return jnp.stack([a, b], axis=2).reshape(dw.shape[0], 2 * dw.shape[1], bw, bw)


def kernel(x, ffn1_norm, ffn1_w_gate, ffn1_w_up, ffn1_w_down, mix_norm, w_in, conv_w, conv_b, lru_w_a, lru_b_a, lru_w_x, lru_b_x, lru_lambda, attn_sink, rel_bias, lru_out_norm, attn_out_norm, w_out, ffn2_norm, ffn2_w_gate, ffn2_w_up, ffn2_w_down, final_norm, loss_target, m_ffn1_norm, m_ffn1_w_gate, m_ffn1_w_up, m_ffn1_w_down, m_mix_norm, m_w_in, m_conv_w, m_conv_b, m_lru_w_a, m_lru_b_a, m_lru_w_x, m_lru_b_x, m_lru_lambda, m_attn_sink, m_rel_bias, m_lru_out_norm, m_attn_out_norm, m_w_out, m_ffn2_norm, m_ffn2_w_gate, m_ffn2_w_up, m_ffn2_w_down, m_final_norm, v_ffn1_norm, v_ffn1_w_gate, v_ffn1_w_up, v_ffn1_w_down, v_mix_norm, v_w_in, v_conv_w, v_conv_b, v_lru_w_a, v_lru_b_a, v_lru_w_x, v_lru_b_x, v_lru_lambda, v_attn_sink, v_rel_bias, v_lru_out_norm, v_attn_out_norm, v_w_out, v_ffn2_norm, v_ffn2_w_gate, v_ffn2_w_up, v_ffn2_w_down, v_final_norm):
    depth, d = ffn1_norm.shape
    d_ff = N_CHIPS * ffn1_w_gate.shape[2]
    d_in = N_CHIPS * w_in.shape[2]
    lw = conv_b.shape[1]
    att = N_HEADS * HEAD_DIM
    lay = Layout(d, d_ff, d_in)
    k_chip = 2 * lax.axis_index("x") + lax.axis_index("y")
    pos = jnp.stack([k_chip, lax.axis_index("c")]).astype(jnp.int32)

    def rows_major(a):
        return jnp.swapaxes(a, 1, 2)

    mats = (rows_major(ffn1_w_gate), rows_major(ffn1_w_up), ffn1_w_down,
            rows_major(ffn2_w_gate), rows_major(ffn2_w_up), ffn2_w_down)

    def pack_layer(l, deps=()):
        land = lax.empty((N_CHIPS, 2, lay.rows, d), BF)
        for m, a in enumerate(mats):
            land = pack_weight(pos, land, m, l, a, deps=deps if m == 0 else ())
        return pack_weight(pos, land, lay.MIX_BLK, l, rows_major(w_in), extra=w_out)

    def gather_start(l, land):
        return split_start(f"gather_start_{l}", [land], 3, gather_plan)

    def gather_wait(l, started, after):
        ssem, rsem, bufs, _ = started
        return split_wait(f"gather_wait_{l}", ssem, rsem, bufs, after, gather_plan)

    sharded_small = (conv_w, lru_b_a, lru_b_x, lru_lambda)
    sshard = jnp.concatenate([a.reshape(-1, LANE) for a in sharded_small], axis=0)
    sfull = gather_small(sshard)
    small_full, off = [], 0
    for a in sharded_small:
        r = a.shape[0] * a.shape[1]
        piece = sfull[:, off:off + r].reshape((N_CHIPS,) + a.shape)
        small_full.append(jnp.moveaxis(piece, 0, 2).reshape(a.shape[0], a.shape[1], N_CHIPS * LANE))
        off += r
    conv_w_f, b_a_f, b_x_f, lam_f = small_full

    zrow = jnp.zeros((1, lw), F32)
    wblk_a = _block_diag_pairs(lru_w_a)
    wblk_x = _block_diag_pairs(lru_w_x)
    buckets, in_band = _band_buckets()
    onehot = (buckets.reshape(-1)[:, None] == jnp.arange(N_BUCKETS)[None, :]).astype(F32)
    bias = jnp.dot(rel_bias.T, onehot.T, precision=lax.Precision.HIGHEST).reshape(N_HEADS, BLOCK, 3 * BLOCK)
    bias = jnp.where(in_band[None], bias, NEG_INF)
    bias = bias.reshape(N_KV_HEADS, KV_GROUP, BLOCK, 3 * BLOCK).transpose(0, 3, 1, 2).reshape(N_KV_HEADS, 3 * BLOCK, KV_GROUP * BLOCK)
    kblk = 2 * lw // LANE

    def layer_small(l):
        cvec = jnp.concatenate([conv_w_f[l], jnp.zeros((8 - CONV_WIDTH, lw), F32)], axis=0)
        pvec = jnp.concatenate([conv_b[l][None], b_a_f[l], b_x_f[l], lam_f[l], zrow], axis=0)
        wblk = jnp.stack([wblk_a[l, 0], wblk_x[l, 0], wblk_a[l, 1], wblk_x[l, 1]]).astype(BF)
        sink = jnp.broadcast_to(jnp.repeat(attn_sink[l], BLOCK).reshape(N_KV_HEADS, 1, KV_GROUP * BLOCK),
                                (N_KV_HEADS, 8, KV_GROUP * BLOCK))
        return cvec, pvec, wblk, sink

    xs = x[0]
    wfull = [None] * depth
    parts = [(0, 3 * lay.fh), (3 * lay.fh, lay.rows - 3 * lay.fh)]
    plans = [(functools.partial(gather_plan, rows=p), functools.partial(handover_plan, rows=p)) for p in parts]
    land = pack_layer(0, deps=(sfull,))
    first = split_start("gather_start_0a", [land], 3, plans[0][0])
    second = split_start("gather_start_0b", first[2], 3, plans[1][0])
    lands = {l: pack_layer(l, deps=(second[3],)) for l in range(1, depth)}
    land = split_wait("gather_wait_0a", first[0], first[1], second[2], [xs] + list(lands.values()), plans[0][0])
    wfull[0], = exchange_now("gather_handover_0a", land, 3, plans[0][1])
    started = None
    saved = []
    for l in range(depth):
        cvec, pvec, wblk, sink = layer_small(l)
        deps = (started[3],) if started is not None else ()
        x1, gate1, up1 = ffn_forward(xs, ffn1_norm[l][None], wfull[l], lay, 0, deps=deps)
        deps = ()
        if l == 0:
            land = split_wait("gather_wait_0b", second[0], second[1], [wfull[0]], [x1], plans[1][0])
            wfull[0], = exchange_now("gather_handover_0b", land, 3, plans[1][1])
            if depth > 1:
                started = gather_start(1, lands[1])
                deps = (started[3],)
        proj, qt, vt = mix_project(x1, mix_norm[l][None], wfull[l], lay, lw, att)
        y_rec, hs = lru_forward(proj, cvec, pvec, wblk, lw, deps=deps)
        y_att = attention_forward(qt, proj, vt, bias, sink, kblk)
        x2 = mix_output(x1, y_rec, y_att, lru_out_norm[l][None], attn_out_norm[l][:, None], wfull[l], lay)
        deps, handover = (), None
        if 0 < l < depth - 1:
            land, = gather_wait(l + 1, started, [x2])
            started = gather_start(l + 2, lands[l + 2]) if l + 2 < depth else None
            handover = split_start(f"gather_handover_start_{l + 1}", [land], 3, handover_plan)
            deps = (handover[3],) + ((started[3],) if started is not None else ())
        x3, gate2, up2 = ffn_forward(x2, ffn2_norm[l][None], wfull[l], lay, 1, deps=deps)
        saved.append((xs, x1, x2, proj, qt, y_rec, hs, y_att, (gate1, up1), (gate2, up2)))
        xs = x3
        if handover is not None:
            wfull[l + 1], = split_wait(f"gather_handover_wait_{l + 1}", handover[0], handover[1], handover[2], [x3],
                                       handover_plan)
        elif l == 0 and depth > 1:
            wfull[1], = exchange_now("gather_handover_1", gather_wait(1, started, [x3]), 3, handover_plan)
            started = gather_start(2, lands[2]) if depth > 2 else None

    dx, d_final, loss_tile = loss_head(xs, final_norm[None], loss_target[0])
    loss = lax.psum(loss_tile[0, 0], ("x", "y", "c"))

    layer_names = ["ffn1_norm", "mix_norm", "conv_w", "conv_b", "lru_w_a", "lru_b_a", "lru_w_x", "lru_b_x", "lru_lambda",
                   "attn_sink", "lru_out_norm", "attn_out_norm", "ffn2_norm"]
    dbias_total = jnp.zeros(bias.shape, F32)

    def ffn_back(xin, gain, dout, pre, gb, l, which, deps=()):
        dxo, dg, lhs, rhs = ffn_backward_dx(xin, gain, dout, *pre, wfull[l], lay, which, deps=deps)
        return dxo, dg[0], weight_grad_tn(lhs, rhs, gb, lay, 3 * which)

    def pair_start(l, gb, sb):
        lands = [lax.empty((N_CHIPS,) + gb.shape[2:], gb.dtype), lax.empty(sb.shape, sb.dtype)]
        return split_start(f"pair_start_{l}", [gb, sb] + lands, N_CHIPS + 1, pair_plan)

    def reduce_start(l, paired, after):
        gb, sb, p1, sp1 = split_wait(f"pair_wait_{l}", paired[0], paired[1], paired[2], after, pair_plan)
        cs = pair_sum(pos, gb, p1)
        ss = small_pair_sum(sb, sp1)
        lands = [lax.empty((3,) + cs.shape[1:], cs.dtype), lax.empty((N_CHIPS,) + ss.shape, ss.dtype)]
        return split_start(f"reduce_start_{l}", [cs, ss] + lands, 6, reduce_plan)

    def reduce_finish(l, started, after):
        ssem, rsem, bufs, _ = started
        cs, ss, p3, sp3 = split_wait(f"reduce_wait_{l}", ssem, rsem, bufs, after, reduce_plan)
        return chip_sum(pos, cs, p3), small_chip_sum(pos, ss, sp3)

    gf = [None] * depth
    small_sums = [None] * depth
    small_shapes = [None] * depth
    paired = None
    in_flight = None
    finals = {}
    tokens = []
    for l in reversed(range(depth)):
        x0, x1, x2, proj, qt, y_rec, hs, y_att, pre1, pre2 = saved[l]
        cvec, pvec, wblk, sink = layer_small(l)
        gb = lax.empty((N_CHIPS, 2, lay.rows, d), BF)
        part = {}
        dx, part["ffn2_norm"], gb = ffn_back(x2, ffn2_norm[l][None], dx, pre2, gb, l, 1, deps=tuple(tokens))
        deps = ()
        if paired is not None:
            in_flight = (paired[0], reduce_start(paired[0], paired[1], [dx, gb]))
            deps = (in_flight[1][3],)
        dyr, dya, dgr, dga, dwout = mix_output_backward(dx, y_rec, y_att, lru_out_norm[l][None], attn_out_norm[l][:, None],
                                                        wfull[l], lay, deps=deps)
        part["lru_out_norm"] = dgr[0]
        part["attn_out_norm"] = dga[:, 0]
        dq, dkv, dbias, dsink = attention_backward(qt, proj, y_att, dya, bias, sink, kblk)
        dbias_total = dbias_total + dbias
        part["attn_sink"] = jnp.sum(dsink[:, 0, :].reshape(N_HEADS, BLOCK), axis=1)
        dxr, dgt, dcv, dpv, dwb = lru_backward(proj, hs, dyr, cvec, pvec, wblk, lw)
        part["conv_w"] = dcv[:CONV_WIDTH]
        part["conv_b"] = dpv[0]
        part["lru_b_a"] = dpv[1:3]
        part["lru_b_x"] = dpv[3:5]
        part["lru_lambda"] = dpv[5:7]
        part["lru_w_a"] = _diag_blocks(jnp.stack([dwb[0], dwb[2]]))
        part["lru_w_x"] = _diag_blocks(jnp.stack([dwb[1], dwb[3]]))
        dx, dgm, gb = mix_project_backward(x1, mix_norm[l][None], dx, dxr, dgt, dq, dkv, dwout, wfull[l], gb, lay)
        part["mix_norm"] = dgm[0]
        dx, part["ffn1_norm"], gb = ffn_back(x0, ffn1_norm[l][None], dx, pre1, gb, l, 0)
        pieces = [part[n] for n in layer_names]
        if l == 0:
            dbias_heads = dbias_total.reshape(N_KV_HEADS, 3 * BLOCK, KV_GROUP, BLOCK).transpose(0, 2, 3, 1)
            d_rel_bias = jnp.dot(dbias_heads.reshape(N_HEADS, -1), onehot, precision=lax.Precision.HIGHEST).T
            pieces += [d_rel_bias, d_final[0]]
        small_shapes[l] = [p.shape for p in pieces]
        paired = (l, pair_start(l, gb, _pack_rows(pieces, 1024)))
        tokens = [paired[1][3]]
        if in_flight is not None:
            above = in_flight[0]
            half, small_sums[above] = reduce_finish(above, in_flight[1], [dx])
            finals[above] = split_start(f"final_start_{above}", [half], 1, final_plan)
            tokens.append(finals[above][3])
            in_flight = None
    grad_x = dx[None]

    weights = dict(ffn1_norm=ffn1_norm, ffn1_w_gate=ffn1_w_gate, ffn1_w_up=ffn1_w_up, ffn1_w_down=ffn1_w_down, mix_norm=mix_norm, w_in=w_in, conv_w=conv_w, conv_b=conv_b, lru_w_a=lru_w_a, lru_b_a=lru_b_a, lru_w_x=lru_w_x, lru_b_x=lru_b_x, lru_lambda=lru_lambda, attn_sink=attn_sink, rel_bias=rel_bias, lru_out_norm=lru_out_norm, attn_out_norm=attn_out_norm, w_out=w_out, ffn2_norm=ffn2_norm, ffn2_w_gate=ffn2_w_gate, ffn2_w_up=ffn2_w_up, ffn2_w_down=ffn2_w_down, final_norm=final_norm)
    m_in = dict(ffn1_norm=m_ffn1_norm, ffn1_w_gate=m_ffn1_w_gate, ffn1_w_up=m_ffn1_w_up, ffn1_w_down=m_ffn1_w_down, mix_norm=m_mix_norm, w_in=m_w_in, conv_w=m_conv_w, conv_b=m_conv_b, lru_w_a=m_lru_w_a, lru_b_a=m_lru_b_a, lru_w_x=m_lru_w_x, lru_b_x=m_lru_b_x, lru_lambda=m_lru_lambda, attn_sink=m_attn_sink, rel_bias=m_rel_bias, lru_out_norm=m_lru_out_norm, attn_out_norm=m_attn_out_norm, w_out=m_w_out, ffn2_norm=m_ffn2_norm, ffn2_w_gate=m_ffn2_w_gate, ffn2_w_up=m_ffn2_w_up, ffn2_w_down=m_ffn2_w_down, final_norm=m_final_norm)
    v_in = dict(ffn1_norm=v_ffn1_norm, ffn1_w_gate=v_ffn1_w_gate, ffn1_w_up=v_ffn1_w_up, ffn1_w_down=v_ffn1_w_down, mix_norm=v_mix_norm, w_in=v_w_in, conv_w=v_conv_w, conv_b=v_conv_b, lru_w_a=v_lru_w_a, lru_b_a=v_lru_b_a, lru_w_x=v_lru_w_x, lru_b_x=v_lru_b_x, lru_lambda=v_lru_lambda, attn_sink=v_attn_sink, rel_bias=v_rel_bias, lru_out_norm=v_lru_out_norm, attn_out_norm=v_attn_out_norm, w_out=v_w_out, ffn2_norm=v_ffn2_norm, ffn2_w_gate=v_ffn2_w_gate, ffn2_w_up=v_ffn2_w_up, ffn2_w_down=v_ffn2_w_down, final_norm=v_final_norm)
    order = list(weights)
    large = [(name, m, 0, lay.fh, m % 3 != 2) for m, name in
             enumerate(("ffn1_w_gate", "ffn1_w_up", "ffn1_w_down", "ffn2_w_gate", "ffn2_w_up", "ffn2_w_down"))]
    large += [("w_in", lay.MIX_BLK, 0, lay.ih, True), ("w_out", lay.MIX_BLK, lay.ih, lay.oh, False)]
    as_rows = {name: [rows_major(src[name]) if flip else src[name] for src in (weights, m_in, v_in)]
               for name, _, _, _, flip in large}
    stacked = {name: tuple(lax.empty(as_rows[name][0].shape, F32) for _ in range(4)) for name, *_ in large}

    def adamw_large(l, deps=()):
        for i, (name, blk, row_off, n_half, _) in enumerate(large):
            stacked[name] = adamw_layer(gf[l], blk, row_off, n_half, l, *as_rows[name], stacked[name],
                                        deps=deps if i == 0 else ())

    last = paired[0]
    crossing = reduce_start(last, paired[1], [dx])
    for l in sorted(finals):
        gf[l], = split_wait(f"final_wait_{l}", finals[l][0], finals[l][1], finals[l][2], [crossing[3]], final_plan)
        adamw_large(l, deps=(crossing[3],))
    ready = [buf for name, *_ in large for buf in stacked[name]] if depth > 1 else []
    half, small_sums[last] = reduce_finish(last, crossing, [dx] + ready)
    gf[last], = exchange_now(f"final_now_{last}", [half], 1, final_plan)
    adamw_large(last)

    per_layer = [_unpack_rows(small_sums[l], small_shapes[l]) for l in range(depth)]
    grads = {n: jnp.stack([per_layer[l][i] for l in range(depth)]) for i, n in enumerate(layer_names)}
    grads["rel_bias"], grads["final_norm"] = per_layer[0][len(layer_names):]
    for name in ("conv_w", "lru_b_a", "lru_b_x", "lru_lambda"):
        grads[name] = lax.dynamic_slice_in_dim(grads[name], k_chip * LANE, LANE, axis=2)
    delta, new_m, new_v = {}, {}, {}
    for name, _, _, _, flip in large:
        grads[name], delta[name], new_m[name], new_v[name] = [rows_major(a) if flip else a for a in stacked[name]]
    small = [n for n in order if n not in stacked]
    packed = [_pack_rows([src[n] for n in small], 1024) for src in (weights, grads, m_in, v_in)]
    outs = adamw(*packed)
    shapes = [weights[n].shape for n in small]
    for dst, buf in zip((delta, new_m, new_v), outs):
        dst.update(zip(small, _unpack_rows(buf, shapes)))

    return (loss, grad_x, *[grads[n] for n in order], *[delta[n] for n in order],
            *[new_m[n] for n in order], *[new_v[n] for n in order])
```

```python
import functools
import math

import jax
import jax.numpy as jnp
import numpy as np
from jax import lax
from jax.experimental import pallas as pl
from jax.experimental.pallas import tpu as pltpu

BF = jnp.bfloat16
F32 = jnp.float32
SDS = jax.ShapeDtypeStruct
MESH = pl.DeviceIdType.MESH
ANY = pl.BlockSpec(memory_space=pl.ANY)

N_CHIPS = 4
N_HEADS = 8
N_KV_HEADS = 2
KV_GROUP = N_HEADS // N_KV_HEADS
HEAD_DIM = 64
BLOCK = 128
WINDOW = 128
N_BUCKETS = 32
MAX_DISTANCE = 128
LRU_C = 8.0
CONV_WIDTH = 4
LANE = 128
SUBLANES = 8
MXU_WIDTH = 256
SCAN_CHAINS = 8
EPS = 1e-6
FFN_RES = 0.5
NEG_INF = -1e30
ADAM_LR = 0.001
ADAM_B1 = 0.9
ADAM_B2 = 0.999
ADAM_EPS = 1e-08
ADAM_WD = 0.01
ADAM_STEP = 10
VMEM_LIMIT = 60000 * 1024
GELU_C = math.sqrt(2.0 / math.pi)


def dot_nn(a, b):
    return lax.dot_general(a, b, (((1,), (0,)), ((), ())), preferred_element_type=F32)


def dot_nt(a, b):
    return lax.dot_general(a, b, (((1,), (1,)), ((), ())), preferred_element_type=F32)


def dot_tn(a, b):
    return lax.dot_general(a, b, (((0,), (0,)), ((), ())), preferred_element_type=F32)


def _cparams(**kw):
    return pltpu.CompilerParams(vmem_limit_bytes=VMEM_LIMIT, **kw)


class Layout:
    MIX_BLK = 6
    BLOCKS = 7

    def __init__(self, d_model, d_ff, d_in):
        self.fh = d_ff // (2 * N_CHIPS)
        self.ih = d_in // (2 * N_CHIPS)
        self.oh = d_model // (2 * N_CHIPS)
        assert self.ih + self.oh == self.fh, "w_in^T and w_out rows must fill one ffn-sized block"
        self.rows = self.BLOCKS * self.fh


def _row_chunk(rows, target, step=16):
    best = rows
    for c in range(step, min(rows, target) + 1, step):
        if rows % c == 0:
            best = c
    return best


def _mesh_pos():
    return lax.axis_index("x"), lax.axis_index("y"), lax.axis_index("c")


def _rcopy(src, dst, ssem, rsem, dev):
    return pltpu.make_async_remote_copy(src_ref=src, dst_ref=dst, send_sem=ssem, recv_sem=rsem,
                                        device_id=dev, device_id_type=MESH)


HBM = pl.BlockSpec(memory_space=pltpu.HBM)
SEM = pl.BlockSpec(memory_space=pltpu.SEMAPHORE)
DATAFLOW = pltpu.SideEffectType.DATAFLOW_SIDE_EFFECTING


def _chip_peers():
    x, y, c = _mesh_pos()
    peers = [(1 - x, y), (x, 1 - y), (1 - x, 1 - y)]
    return x, y, c, 2 * x + y, [(px, py, 2 * px + py) for px, py in peers]


def split_start(name, bufs, n, plan):
    nb = len(bufs)

    def body(*refs):
        sends, _ = plan(refs[:nb], refs[nb], refs[nb + 1])
        for cp in sends:
            cp.start()
        refs[-1][...] = jnp.zeros_like(refs[-1])

    out = pl.pallas_call(
        body, name=name,
        out_shape=(pltpu.SemaphoreType.DMA((n,)), pltpu.SemaphoreType.DMA((n,)),
                   *[pltpu.HBM(b.shape, b.dtype) for b in bufs], SDS((8, LANE), F32)),
        in_specs=[HBM] * nb, out_specs=(SEM, SEM, *([HBM] * nb), pl.BlockSpec(memory_space=pltpu.VMEM)),
        input_output_aliases={i: 2 + i for i in range(nb)},
        compiler_params=pltpu.CompilerParams(has_side_effects=DATAFLOW),
    )(*[pltpu.with_memory_space_constraint(b, pltpu.HBM) for b in bufs])
    return out[0], out[1], list(out[2:2 + nb]), out[-1]


def split_wait(name, ssem, rsem, bufs, after, plan):
    nb = len(bufs)

    def body(*refs):
        sends, recvs = plan(refs[:nb], refs[nb], refs[nb + 1])
        for cp in recvs:
            cp.wait_recv()
        for cp in sends:
            cp.wait_send()

    out = pl.pallas_call(
        body, name=name, out_shape=tuple(pltpu.HBM(b.shape, b.dtype) for b in bufs),
        in_specs=[HBM] * nb + [SEM, SEM] + [ANY] * len(after), out_specs=tuple([HBM] * nb),
        input_output_aliases={i: i for i in range(nb)},
        compiler_params=pltpu.CompilerParams(has_side_effects=DATAFLOW),
    )(*bufs, ssem, rsem, *after)
    return list(out)


def gather_plan(refs, ssem, rsem, rows=None):
    land_ref, = refs
    _, _, c, k, peers = _chip_peers()
    part = (lambda a: a) if rows is None else (lambda a: a.at[pl.ds(rows[0], rows[1])])
    sends = [_rcopy(part(land_ref.at[k, c]), part(land_ref.at[k, c]), ssem.at[j], rsem.at[j], (px, py, c))
             for j, (px, py, _) in enumerate(peers)]
    recvs = [_rcopy(part(land_ref.at[kp, c]), part(land_ref.at[kp, c]), ssem.at[j], rsem.at[j], (px, py, c))
             for j, (px, py, kp) in enumerate(peers)]
    return sends, recvs


def reduce_plan(refs, ssem, rsem):
    cs_ref, ss_ref, p3_ref, sp3_ref = refs
    _, _, c, k, peers = _chip_peers()
    sends, recvs = [], []
    for j, (px, py, kp) in enumerate(peers):
        sends.append(_rcopy(cs_ref.at[kp], p3_ref.at[j], ssem.at[j], rsem.at[j], (px, py, c)))
        recvs.append(_rcopy(cs_ref.at[kp], p3_ref.at[j], ssem.at[j], rsem.at[j], (px, py, c)))
        sends.append(_rcopy(ss_ref, sp3_ref.at[k], ssem.at[3 + j], rsem.at[3 + j], (px, py, c)))
        recvs.append(_rcopy(ss_ref, sp3_ref.at[kp], ssem.at[3 + j], rsem.at[3 + j], (px, py, c)))
    return sends, recvs


def gather_small(sshard):
    def body(s_ref, sf_ref, lsem, ssem, rsem):
        _, _, c, k, peers = _chip_peers()
        own = pltpu.make_async_copy(s_ref, sf_ref.at[k], lsem)
        own.start()
        sends = [_rcopy(s_ref, sf_ref.at[k], ssem.at[j], rsem.at[j], (px, py, c)) for j, (px, py, _) in enumerate(peers)]
        recvs = [_rcopy(s_ref, sf_ref.at[kp], ssem.at[j], rsem.at[j], (px, py, c)) for j, (px, py, kp) in enumerate(peers)]
        for cp in sends:
            cp.start()
        for cp in recvs:
            cp.wait_recv()
        for cp in sends:
            cp.wait_send()
        own.wait()

    return pl.pallas_call(
        body, name="gather_small", out_shape=SDS((N_CHIPS,) + sshard.shape, sshard.dtype),
        in_specs=[ANY], out_specs=ANY,
        scratch_shapes=[pltpu.SemaphoreType.DMA, pltpu.SemaphoreType.DMA((3,)), pltpu.SemaphoreType.DMA((3,))],
    )(sshard)


def exchange_now(name, bufs, n, plan):
    nb = len(bufs)

    def body(*refs):
        sends, recvs = plan(refs[nb:2 * nb], refs[2 * nb], refs[2 * nb + 1])
        for cp in sends:
            cp.start()
        for cp in recvs:
            cp.wait_recv()
        for cp in sends:
            cp.wait_send()

    return list(pl.pallas_call(
        body, name=name, out_shape=tuple(SDS(b.shape, b.dtype) for b in bufs),
        in_specs=[ANY] * nb, out_specs=tuple([ANY] * nb), input_output_aliases={i: i for i in range(nb)},
        scratch_shapes=[pltpu.SemaphoreType.DMA((n,)), pltpu.SemaphoreType.DMA((n,))],
    )(*bufs))


def handover_plan(refs, ssem, rsem, rows=None):
    land_ref, = refs
    x, y, c, _, peers = _chip_peers()
    sib = (x, y, 1 - c)
    part = (lambda a: a) if rows is None else (lambda a: a.at[pl.ds(rows[0], rows[1])])
    sends = [_rcopy(part(land_ref.at[kp, c]), part(land_ref.at[kp, c]), ssem.at[j], rsem.at[j], sib)
             for j, (_, _, kp) in enumerate(peers)]
    recvs = [_rcopy(part(land_ref.at[kp, 1 - c]), part(land_ref.at[kp, 1 - c]), ssem.at[j], rsem.at[j], sib)
             for j, (_, _, kp) in enumerate(peers)]
    return sends, recvs


def pair_plan(refs, ssem, rsem):
    gb_ref, sb_ref, p_ref, sp_ref = refs
    x, y, c = _mesh_pos()
    sib = (x, y, 1 - c)
    n = gb_ref.shape[0]
    copies = [_rcopy(gb_ref.at[kk, 1 - c], p_ref.at[kk], ssem.at[kk], rsem.at[kk], sib) for kk in range(n)]
    copies.append(_rcopy(sb_ref, sp_ref, ssem.at[n], rsem.at[n], sib))
    return copies, copies


def final_plan(refs, ssem, rsem):
    gf_ref, = refs
    x, y, c = _mesh_pos()
    sib = (x, y, 1 - c)
    return ([_rcopy(gf_ref.at[c], gf_ref.at[c], ssem.at[0], rsem.at[0], sib)],
            [_rcopy(gf_ref.at[1 - c], gf_ref.at[1 - c], ssem.at[0], rsem.at[0], sib)])


def pair_sum(pos, gb, p1):
    n, _, rh, d = gb.shape
    cr = _row_chunk(rh, 1280)

    def body(pos_ref, a_ref, b_ref, o_ref):
        o_ref[...] = (a_ref[...].astype(F32) + b_ref[...].astype(F32)).astype(o_ref.dtype)

    return pl.pallas_call(
        body, name="pair_sum", out_shape=SDS((n, rh, d), gb.dtype),
        grid_spec=pltpu.PrefetchScalarGridSpec(
            num_scalar_prefetch=1, grid=(n, rh // cr),
            in_specs=[pl.BlockSpec((None, None, cr, d), lambda kk, r, pos: (kk, pos[1], r, 0)),
                      pl.BlockSpec((None, cr, d), lambda kk, r, pos: (kk, r, 0))],
            out_specs=pl.BlockSpec((None, cr, d), lambda kk, r, pos: (kk, r, 0))),
        compiler_params=_cparams(),
    )(pos, gb, p1)


def chip_sum(pos, cs, p3):
    n, rh, d = cs.shape
    cr = _row_chunk(rh, 640)

    def body(pos_ref, a_ref, b_ref, o_ref):
        acc = a_ref[...].astype(F32)
        for j in range(3):
            acc = acc + b_ref[j].astype(F32)
        o_ref[...] = acc

    return pl.pallas_call(
        body, name="chip_sum", out_shape=SDS((2, rh, d), F32),
        grid_spec=pltpu.PrefetchScalarGridSpec(
            num_scalar_prefetch=1, grid=(rh // cr,),
            in_specs=[pl.BlockSpec((None, cr, d), lambda r, pos: (pos[0], r, 0)),
                      pl.BlockSpec((3, cr, d), lambda r, pos: (0, r, 0))],
            out_specs=pl.BlockSpec((None, cr, d), lambda r, pos: (pos[1], r, 0))),
        compiler_params=_cparams(),
    )(pos, cs, p3)


def small_pair_sum(a, b):
    def body(a_ref, b_ref, o_ref):
        o_ref[...] = a_ref[...] + b_ref[...]

    return pl.pallas_call(body, name="small_pair_sum", out_shape=SDS(a.shape, a.dtype),
                          compiler_params=_cparams())(a, b)


def small_chip_sum(pos, own, p):
    ns, w = own.shape

    def body(pos_ref, own_ref, p0, p1, p2, p3, o_ref):
        k = pos_ref[0]
        acc = None
        for chip, ref in enumerate((p0, p1, p2, p3)):
            term = jnp.where(k == chip, own_ref[...], ref[...])
            acc = term if acc is None else acc + term
        o_ref[...] = acc

    def slot(chip):
        return pl.BlockSpec((None, ns, w), lambda i, pos: (jnp.where(pos[0] == chip, (chip + 1) % N_CHIPS, chip), 0, 0))

    return pl.pallas_call(
        body, name="small_chip_sum", out_shape=SDS(own.shape, own.dtype),
        grid_spec=pltpu.PrefetchScalarGridSpec(
            num_scalar_prefetch=1, grid=(1,),
            in_specs=[pl.BlockSpec((ns, w), lambda i, pos: (0, 0))] + [slot(chip) for chip in range(N_CHIPS)],
            out_specs=pl.BlockSpec((ns, w), lambda i, pos: (0, 0))),
        compiler_params=_cparams(),
    )(pos, own, p, p, p, p)


def _rms(x, g):
    rs = lax.rsqrt(jnp.mean(x * x, axis=-1, keepdims=True) + EPS)
    xh = x * rs
    return xh, rs, xh * g


def _rms_bwd(dy, xh, rs, g):
    dxh = dy * g
    dx = rs * (dxh - xh * jnp.mean(dxh * xh, axis=-1, keepdims=True))
    return dx, dy * xh


def _gelu(x):
    t = jnp.tanh(GELU_C * (x + 0.044715 * x * x * x))
    return 0.5 * x * (1.0 + t), t


def _gelu_grad(x, t):
    return 0.5 * (1.0 + t) + 0.5 * x * (1.0 - t * t) * GELU_C * (1.0 + 3.0 * 0.044715 * x * x)


def _shift_rows(v, s, n):
    if s == 0:
        return v
    t = lax.broadcasted_iota(jnp.int32, v.shape, 0)
    rolled = pltpu.roll(v, (-s) % n, 0)
    inside = (t < n - s) if s > 0 else (t >= -s)
    return jnp.where(inside, rolled, 0.0)


def _scan_rows(a_ref, u_ref, h_ref, acum_ref, reverse):
    s_len, w = a_ref.shape
    chunk = min(512, s_len)
    last = 0 if reverse else SUBLANES - 1

    def inside_vregs(ci, _):
        rows = pl.ds(pl.multiple_of(ci * chunk, chunk), chunk)
        a = a_ref[rows, :].reshape(chunk // SUBLANES, SUBLANES, w)
        u = u_ref[rows, :].reshape(chunk // SUBLANES, SUBLANES, w)
        pos = lax.broadcasted_iota(jnp.int32, (1, SUBLANES, w), 1)
        for dist in (1, 2, 4):
            ok = (pos < SUBLANES - dist) if reverse else (pos >= dist)
            shift = SUBLANES - dist if reverse else dist
            u = u + a * jnp.where(ok, pltpu.roll(u, shift, 1), 0.0)
            a = a * jnp.where(ok, pltpu.roll(a, shift, 1), 1.0)
        h_ref[rows, :] = u.reshape(chunk, w)
        acum_ref[rows, :] = a.reshape(chunk, w)
        return 0

    lax.fori_loop(0, s_len // chunk, inside_vregs, 0)

    chains = max(1, min(SCAN_CHAINS, s_len // (8 * SUBLANES)))
    seg = s_len // chains
    nvreg = seg // SUBLANES

    def step(j, carry):
        jj = (nvreg - 1 - j) if reverse else j
        out = []
        for c, (hin, ain) in enumerate(carry):
            rows = pl.ds(pl.multiple_of(c * seg + jj * SUBLANES, SUBLANES), SUBLANES)
            acc = acum_ref[rows, :]
            h = h_ref[rows, :] + acc * hin
            acc = acc * ain
            h_ref[rows, :] = h
            acum_ref[rows, :] = acc
            out.append((jnp.broadcast_to(h[last:last + 1, :], h.shape), jnp.broadcast_to(acc[last:last + 1, :], acc.shape)))
        return tuple(out)

    init = tuple((jnp.zeros((SUBLANES, w), F32), jnp.ones((SUBLANES, w), F32)) for _ in range(chains))
    ends = lax.fori_loop(0, nvreg, step, init, unroll=min(2, nvreg))
    order = range(chains - 2, -1, -1) if reverse else range(1, chains)
    inflow = jnp.zeros((1, w), F32)
    for s in order:
        h, acc = ends[s + 1 if reverse else s - 1]
        inflow = h[0:1, :] + acc[0:1, :] * inflow
        rows = pl.ds(s * seg, seg)
        h_ref[rows, :] = h_ref[rows, :] + acum_ref[rows, :] * inflow


def _width_parts(f):
    cut = (f // 2) // MXU_WIDTH * MXU_WIDTH
    return [slice(0, cut), slice(cut, f)] if cut and f % MXU_WIDTH == 0 else [slice(0, f // 2), slice(f // 2, f)]


def _w_spec(rows_half, d, blk):
    return pl.BlockSpec((N_CHIPS, 2, rows_half, d), lambda *_: (0, 0, blk, 0), pipeline_mode=pl.Buffered(1))


def ffn_forward(x, gain, wfull, lay, which, deps=(), tm=512):
    s_len, d = x.shape
    tm = min(tm, s_len)
    f = 8 * lay.fh

    def body(x_ref, g_ref, wg_ref, wu_ref, wd_ref, *rest):
        o_ref, gate_ref, up_ref = rest[len(deps):]
        x = x_ref[...]
        _, _, hn = _rms(x, g_ref[...])
        h = hn.astype(BF)
        y = jnp.zeros((tm, d), F32)
        for cols in _width_parts(f):
            gate = dot_nt(h, wg_ref[...].reshape(f, d)[cols])
            up = dot_nt(h, wu_ref[...].reshape(f, d)[cols])
            act = (gate * jax.nn.sigmoid(gate) * up).astype(BF)
            y = y + dot_nn(act, wd_ref[...].reshape(f, d)[cols])
            gate_ref[:, cols] = gate.astype(BF)
            up_ref[:, cols] = up.astype(BF)
        o_ref[...] = x + FFN_RES * y

    row = pl.BlockSpec((tm, d), lambda i: (i, 0))
    wide = pl.BlockSpec((tm, f), lambda i: (i, 0))
    return pl.pallas_call(
        body, name="ffn_forward", grid=(s_len // tm,),
        out_shape=(SDS((s_len, d), F32), SDS((s_len, f), BF), SDS((s_len, f), BF)),
        in_specs=[row, pl.BlockSpec((1, d), lambda i: (0, 0))]
        + [_w_spec(lay.fh, d, 3 * which + m) for m in range(3)] + [ANY] * len(deps),
        out_specs=(row, wide, wide), compiler_params=_cparams(),
    )(x, gain, wfull, wfull, wfull, *deps)


def ffn_backward_dx(x, gain, dout, gate_bf, up_bf, wfull, lay, which, deps=(), tm=256):
    s_len, d = x.shape
    tm = min(tm, s_len)
    f = 8 * lay.fh
    nt = s_len // tm

    def body(x_ref, g_ref, do_ref, gate_ref, up_ref, wg_ref, wu_ref, wd_ref, *rest):
        dx_ref, dg_ref, lhs_ref, rhs_ref = rest[len(deps):]
        dgate_ref, dup_ref, act_ref = lhs_ref.at[0], lhs_ref.at[1], lhs_ref.at[2]
        h_ref, df_ref = rhs_ref.at[0], rhs_ref.at[1]
        x = x_ref[...]
        g = g_ref[...]
        xh, rs, hn = _rms(x, g)
        h = hn.astype(BF)
        do = do_ref[...]
        df = (FFN_RES * do).astype(BF)
        dh = jnp.zeros((tm, d), F32)
        wd = wd_ref[...].reshape(f, d)
        parts = _width_parts(f)
        dacts = [dot_nt(df, wd[cols]) for cols in parts]
        for part, cols in enumerate(parts):
            wg = wg_ref[...].reshape(f, d)[cols]
            wu = wu_ref[...].reshape(f, d)[cols]
            gate = gate_ref[:, cols].astype(F32)
            up = up_ref[:, cols].astype(F32)
            sg = jax.nn.sigmoid(gate)
            silu = gate * sg
            dact = dacts[part]
            dup = (dact * silu).astype(BF)
            dgate = (dact * up * (sg * (1.0 + gate * (1.0 - sg)))).astype(BF)
            dh = dh + dot_nn(dgate, wg) + dot_nn(dup, wu)
            dgate_ref[:, cols] = dgate
            dup_ref[:, cols] = dup
            act_ref[:, cols] = (silu * up).astype(BF)
        dxn, dgrow = _rms_bwd(dh, xh, rs, g)
        dx_ref[...] = do + dxn

        @pl.when(pl.program_id(0) == 0)
        def _():
            dg_ref[...] = jnp.zeros_like(dg_ref)

        dg_ref[...] += jnp.sum(dgrow, axis=0, keepdims=True)
        h_ref[...] = h
        df_ref[...] = df

    row = pl.BlockSpec((tm, d), lambda i: (i, 0))
    wide = pl.BlockSpec((tm, f), lambda i: (i, 0))
    vec = pl.BlockSpec((1, d), lambda i: (0, 0))
    return pl.pallas_call(
        body, name="ffn_backward_dx", grid=(nt,),
        out_shape=(SDS((s_len, d), F32), SDS((1, d), F32), SDS((3, s_len, f), BF), SDS((2, s_len, d), BF)),
        in_specs=[row, vec, row, wide, wide] + [_w_spec(lay.fh, d, 3 * which + m) for m in range(3)] + [ANY] * len(deps),
        out_specs=(row, vec, pl.BlockSpec((3, tm, f), lambda i: (0, i, 0)), pl.BlockSpec((2, tm, d), lambda i: (0, i, 0))),
        compiler_params=_cparams(),
    )(x, gain, dout, gate_bf, up_bf, wfull, wfull, wfull, *deps)


def weight_grad_tn(lhs, rhs, gb, lay, blk0, tk=4096):
    nmat, s_len, f = lhs.shape
    tk = min(tk, s_len)
    d = rhs.shape[2]
    fc = f // 2
    nk = s_len // tk

    def body(a_ref, b_ref, gb_ref, o_ref, acc):
        kt = pl.program_id(2)

        @pl.when(kt == 0)
        def _():
            acc[...] = jnp.zeros_like(acc)

        acc[...] += dot_tn(a_ref[...], b_ref[...])

        @pl.when(kt == nk - 1)
        def _():
            for p in range(2):
                for q in range(2):
                    o_ref[p, q] = acc[pl.ds((2 * p + q) * lay.fh, lay.fh), :].astype(o_ref.dtype)

    return pl.pallas_call(
        body, name="weight_grad_tn", grid=(nmat, 2, nk), out_shape=SDS(gb.shape, gb.dtype),
        in_specs=[pl.BlockSpec((None, tk, fc), lambda m, j, kt: (m, kt, j)),
                  pl.BlockSpec((None, tk, d), lambda m, j, kt: (jnp.where(m == nmat - 1, 1, 0), kt, 0)), ANY],
        out_specs=pl.BlockSpec((2, 2, lay.fh, d), lambda m, j, kt: (j, 0, blk0 + m, 0)),
        scratch_shapes=[pltpu.VMEM((fc, d), F32)],
        input_output_aliases={2: 0}, compiler_params=_cparams(),
    )(lhs, rhs, gb)


def _lane_blocks(v):
    return [v[:, j * LANE:(j + 1) * LANE] for j in range(v.shape[1] // LANE)]


def _join_lane_blocks(ref):
    return jnp.concatenate([ref[j] for j in range(ref.shape[0])], axis=1)


def _cbm_spec(nblk, rows, first=0):
    return pl.BlockSpec((nblk, rows, LANE), lambda i: (first // nblk, i, 0))


def mix_project(x, gain, wfull, lay, lw, att, tm=512):
    s_len, d = x.shape
    tm = min(tm, s_len)
    d_in = 8 * lay.ih
    kvw = (d_in - 2 * lw - att) // 2
    ncol = (2 * lw + 2 * kvw) // LANE

    def body(x_ref, g_ref, w_ref, o_ref, qt_ref, vt_ref):
        _, _, hn = _rms(x_ref[...], g_ref[...])
        h = hn.astype(BF)
        w = w_ref[:, :, :lay.ih, :].reshape(d_in, d)
        pieces = _lane_blocks(dot_nt(h, w[:2 * lw])) + _lane_blocks(dot_nt(h, w[2 * lw + att:]))
        for j, piece in enumerate(pieces):
            o_ref[j] = piece
        qt_ref[...] = dot_nt(w[2 * lw:2 * lw + att], h)
        vt_ref[...] = dot_nt(w[2 * lw + att + kvw:], h)

    return pl.pallas_call(
        body, name="mix_project", grid=(s_len // tm,),
        out_shape=(SDS((ncol, s_len, LANE), F32), SDS((att, s_len), F32), SDS((kvw, s_len), F32)),
        in_specs=[pl.BlockSpec((tm, d), lambda i: (i, 0)), pl.BlockSpec((1, d), lambda i: (0, 0)),
                  _w_spec(lay.fh, d, lay.MIX_BLK)],
        out_specs=(_cbm_spec(ncol, tm), pl.BlockSpec((att, tm), lambda i: (0, i)), pl.BlockSpec((kvw, tm), lambda i: (0, i))),
        compiler_params=_cparams(),
    )(x, gain, wfull)


def mix_project_backward(x, gain, dout, dxr, dgt, dqt, dkv, dwout, wfull, gb, lay, tm=512):
    s_len, d = x.shape
    tm = min(tm, s_len)
    d_in = 8 * lay.ih
    nt = s_len // tm
    kvw = dkv.shape[1]
    att = dqt.shape[0]
    nlru = (dxr.shape[0] + dgt.shape[0]) * LANE

    def body(x_ref, g_ref, do_ref, dxr_ref, dgt_ref, dqt_ref, dkv_ref, dwo_ref, w_ref, gb_ref, dx_ref, dg_ref, o_ref, acc):
        i = pl.program_id(0)
        g = g_ref[...]
        xh, rs, hn = _rms(x_ref[...], g)
        h = hn.astype(BF)
        w = w_ref[:, :, :lay.ih, :].reshape(d_in, d)
        dlru = jnp.concatenate([_join_lane_blocks(dxr_ref), _join_lane_blocks(dgt_ref)], axis=1).astype(BF)
        dqt = dqt_ref[...].astype(BF)
        dkv = dkv_ref[...].astype(BF)
        dh = dot_nn(dlru, w[:nlru]) + dot_tn(dqt, w[nlru:nlru + att]) + dot_nn(dkv, w[nlru + att:])
        dxn, dgrow = _rms_bwd(dh, xh, rs, g)
        dx_ref[...] = do_ref[...] + dxn

        @pl.when(i == 0)
        def _():
            dg_ref[...] = jnp.zeros_like(dg_ref)
            acc[...] = jnp.zeros_like(acc)

        dg_ref[...] += jnp.sum(dgrow, axis=0, keepdims=True)
        acc[0:nlru, :] += dot_tn(dlru, h)
        acc[nlru:nlru + att, :] += dot_nn(dqt, h)
        acc[nlru + att:, :] += dot_tn(dkv, h)

        @pl.when(i == nt - 1)
        def _():
            for p in range(N_CHIPS):
                for q in range(2):
                    o_ref[p, q, :lay.ih, :] = acc[pl.ds((2 * p + q) * lay.ih, lay.ih), :].astype(o_ref.dtype)
            o_ref[:, :, lay.ih:, :] = dwo_ref[...]

    row = pl.BlockSpec((tm, d), lambda i: (i, 0))
    vec = pl.BlockSpec((1, d), lambda i: (0, 0))
    return pl.pallas_call(
        body, name="mix_project_backward", grid=(nt,),
        out_shape=(SDS((s_len, d), F32), SDS((1, d), F32), SDS(gb.shape, gb.dtype)),
        in_specs=[row, vec, row, _cbm_spec(dxr.shape[0], tm), _cbm_spec(dgt.shape[0], tm),
                  pl.BlockSpec((att, tm), lambda i: (0, i)), pl.BlockSpec((tm, kvw), lambda i: (i, 0)),
                  pl.BlockSpec(dwout.shape, lambda i: (0, 0, 0, 0)), _w_spec(lay.fh, d, lay.MIX_BLK), ANY],
        out_specs=(row, vec, pl.BlockSpec((N_CHIPS, 2, lay.fh, d), lambda i: (0, 0, lay.MIX_BLK, 0))),
        scratch_shapes=[pltpu.VMEM((d_in, d), F32)],
        input_output_aliases={9: 2}, compiler_params=_cparams(),
    )(x, gain, dout, dxr, dgt, dqt, dkv, dwout, wfull, gb)


def _neg_expm1(x):
    series = -x * (1.0 + x * (0.5 + x * (1.0 / 6.0 + x * (1.0 / 24.0))))
    return jnp.where(x > -0.03, series, 1.0 - jnp.exp(x))


def _decay(pv_ref, direction, r):
    lam = pv_ref[5 + direction:6 + direction, :]
    sp = jnp.maximum(-lam, 0.0) + jnp.log1p(jnp.exp(-jnp.abs(lam)))
    log_a = -LRU_C * sp * r
    return jnp.exp(log_a), jnp.sqrt(_neg_expm1(2.0 * log_a)), sp


def _lru_gates(xc, wb_ref, pv_ref, direction):
    xcb = xc.astype(BF)
    sigmoid = lambda z: 0.5 * jnp.tanh(0.5 * z) + 0.5
    r = sigmoid(dot_nn(xcb, wb_ref[2 * direction]) + pv_ref[1 + direction:2 + direction, :])
    i = sigmoid(dot_nn(xcb, wb_ref[2 * direction + 1]) + pv_ref[3 + direction:4 + direction, :])
    a, mult, sp = _decay(pv_ref, direction, r)
    return xcb, r, i, a, mult, sp


def _conv_rows(xr, cv_ref, bias, n):
    acc = bias + cv_ref[0:1, :] * _shift_rows(xr, -2, n)
    for j in range(1, CONV_WIDTH):
        acc = acc + cv_ref[j:j + 1, :] * _shift_rows(xr, j - 2, n)
    return acc


def lru_forward(proj, cvec, pvec, wblk, lw, deps=(), ch=512):
    s_len = proj.shape[1]
    ncb = lw // LANE
    ch = min(ch, s_len)
    nchunk = s_len // ch

    def body(xr_ref, gt_ref, cv_ref, pv_ref, wb_ref, *rest):
        y_ref, hs_ref, xc_s, a_s, u_s, acum_s = rest[len(deps):]
        xc_s[...] = _conv_rows(xr_ref[...], cv_ref, pv_ref[0:1, :], s_len)
        for direction in range(2):
            def fill(ci, _):
                rows = pl.ds(pl.multiple_of(ci * ch, ch), ch)
                xc = xc_s[rows, :]
                _, _, i, a, mult, _ = _lru_gates(xc, wb_ref, pv_ref, direction)
                a_s[rows, :] = a
                u_s[rows, :] = mult * (i * xc)
                return 0

            lax.fori_loop(0, nchunk, fill, 0)
            _scan_rows(a_s, u_s, hs_ref.at[direction], acum_s, reverse=direction == 1)

        def out(ci, _):
            rows = pl.ds(pl.multiple_of(ci * ch, ch), ch)
            gl, _ = _gelu(gt_ref[rows, :])
            y_ref[rows, :] = gl * (hs_ref[0, rows, :] + hs_ref[1, rows, :])
            return 0

        lax.fori_loop(0, nchunk, out, 0)

    col = lambda off: pl.BlockSpec((None, s_len, LANE), lambda cb: (off + cb, 0, 0))
    return pl.pallas_call(
        body, name="lru_forward", grid=(ncb,),
        out_shape=(SDS((ncb, s_len, LANE), F32), SDS((2, ncb, s_len, LANE), F32)),
        in_specs=[col(0), col(ncb), pl.BlockSpec((8, LANE), lambda cb: (0, cb)), pl.BlockSpec((8, LANE), lambda cb: (0, cb)),
                  pl.BlockSpec((4, None, LANE, LANE), lambda cb: (0, cb, 0, 0))] + [ANY] * len(deps),
        out_specs=(col(0), pl.BlockSpec((2, None, s_len, LANE), lambda cb: (0, cb, 0, 0))),
        scratch_shapes=[pltpu.VMEM((s_len, LANE), F32)] * 4, compiler_params=_cparams(),
    )(proj, proj, cvec, pvec, wblk, *deps)


def lru_backward(proj, hs, dy, cvec, pvec, wblk, lw, ch=512):
    s_len = proj.shape[1]
    ncb = lw // LANE
    ch = min(ch, s_len)
    nchunk = s_len // ch

    def body(xr_ref, gt_ref, hs_ref, dy_ref, cv_ref, pv_ref, wb_ref, dxr_ref, dgt_ref, dcv_ref, dpv_ref, dwb_ref,
             xc_s, a_s, r_s, i_s, dh_s, lam_s, hp_s, dxc_s, acum_s):
        xr = xr_ref[...]
        xc_s[...] = _conv_rows(xr, cv_ref, pv_ref[0:1, :], s_len)
        dxc_s[...] = jnp.zeros_like(dxc_s)
        dpv_ref[...] = jnp.zeros_like(dpv_ref)
        dwb_ref[...] = jnp.zeros_like(dwb_ref)

        def head(ci, _):
            rows = pl.ds(pl.multiple_of(ci * ch, ch), ch)
            gt = gt_ref[rows, :]
            gl, t = _gelu(gt)
            dy = dy_ref[rows, :]
            dh_s[rows, :] = dy * gl
            dgt_ref[rows, :] = dy * (hs_ref[0, rows, :] + hs_ref[1, rows, :]) * _gelu_grad(gt, t)
            return 0

        lax.fori_loop(0, nchunk, head, 0)

        for direction in range(2):
            def fill(ci, _):
                rows = pl.ds(pl.multiple_of(ci * ch, ch), ch)
                _, r, i, a, _, _ = _lru_gates(xc_s[rows, :], wb_ref, pv_ref, direction)
                a_s[rows, :] = a
                r_s[rows, :] = r
                i_s[rows, :] = i
                return 0

            lax.fori_loop(0, nchunk, fill, 0)
            toward = 1 if direction == 0 else -1
            hp_s[...] = _shift_rows(a_s[...], toward, s_len)
            _scan_rows(hp_s, dh_s, lam_s, acum_s, reverse=direction == 0)
            hp_s[...] = _shift_rows(hs_ref[direction], -toward, s_len)

            def grads(ci, _):
                rows = pl.ds(pl.multiple_of(ci * ch, ch), ch)
                xc = xc_s[rows, :]
                xcb = xc.astype(BF)
                r, i = r_s[rows, :], i_s[rows, :]
                a, mult, sp = _decay(pv_ref, direction, r)
                du = lam_s[rows, :]
                da = du * hp_s[rows, :]
                dmult = du * i * xc
                di = du * mult * xc
                dlog_a = (da - dmult * a / mult) * a
                dr = dlog_a * (-LRU_C * sp)
                dza = dr * r * (1.0 - r)
                dzx = di * i * (1.0 - i)
                dzab = dza.astype(BF)
                dzxb = dzx.astype(BF)
                dxc_s[rows, :] += (du * mult * i + dot_nt(dzab, wb_ref[2 * direction])
                                   + dot_nt(dzxb, wb_ref[2 * direction + 1]))
                dwb_ref[2 * direction] += dot_tn(xcb, dzab)
                dwb_ref[2 * direction + 1] += dot_tn(xcb, dzxb)
                dpv_ref[1 + direction:2 + direction, :] += jnp.sum(dza, axis=0, keepdims=True)
                dpv_ref[3 + direction:4 + direction, :] += jnp.sum(dzx, axis=0, keepdims=True)
                dpv_ref[5 + direction:6 + direction, :] += jnp.sum(dlog_a * (-LRU_C * r), axis=0, keepdims=True)
                return 0

            lax.fori_loop(0, nchunk, grads, 0)

        for direction in range(2):
            lam = pv_ref[5 + direction:6 + direction, :]
            dpv_ref[5 + direction:6 + direction, :] = dpv_ref[5 + direction:6 + direction, :] * (-jax.nn.sigmoid(-lam))
        dxc = dxc_s[...]
        dpv_ref[0:1, :] = jnp.sum(dxc, axis=0, keepdims=True)
        dxr = cv_ref[0:1, :] * _shift_rows(dxc, 2, s_len)
        for j in range(1, CONV_WIDTH):
            dxr = dxr + cv_ref[j:j + 1, :] * _shift_rows(dxc, 2 - j, s_len)
        dxr_ref[...] = dxr
        dcv_ref[...] = jnp.zeros_like(dcv_ref)
        for j in range(CONV_WIDTH):
            dcv_ref[j:j + 1, :] = jnp.sum(dxc * _shift_rows(xr, j - 2, s_len), axis=0, keepdims=True)

    col = lambda off: pl.BlockSpec((None, s_len, LANE), lambda cb: (off + cb, 0, 0))
    own = col(0)
    small = pl.BlockSpec((8, LANE), lambda cb: (0, cb))
    wspec = pl.BlockSpec((4, None, LANE, LANE), lambda cb: (0, cb, 0, 0))
    return pl.pallas_call(
        body, name="lru_backward", grid=(ncb,),
        out_shape=(SDS((ncb, s_len, LANE), F32), SDS((ncb, s_len, LANE), F32), SDS((8, lw), F32), SDS((8, lw), F32),
                   SDS(wblk.shape, F32)),
        in_specs=[col(0), col(ncb), pl.BlockSpec((2, None, s_len, LANE), lambda cb: (0, cb, 0, 0)), own, small, small, wspec],
        out_specs=(own, own, small, small, wspec),
        scratch_shapes=[pltpu.VMEM((s_len, LANE), F32)] * 9, compiler_params=_cparams(),
    )(proj, proj, hs, dy, cvec, pvec, wblk)


def _window_specs(s_len, first, width=None):
    nb = s_len // BLOCK
    where = (lambda n: jnp.maximum(n - 1, 0), lambda n: n, lambda n: jnp.minimum(n + 1, nb - 1))
    if width is None:
        return [pl.BlockSpec((None, BLOCK, LANE), lambda n, f=f: (first, f(n), 0)) for f in where]
    return [pl.BlockSpec((width, BLOCK), lambda n, f=f: (0, f(n))) for f in where]


def _stack_heads(v, kh):
    return jnp.concatenate([v[(kh * KV_GROUP + g) * HEAD_DIM:(kh * KV_GROUP + g + 1) * HEAD_DIM, :]
                            for g in range(KV_GROUP)], axis=1)


def _unstack_heads(ref, kh, v):
    for g in range(KV_GROUP):
        h = kh * KV_GROUP + g
        ref[h * HEAD_DIM:(h + 1) * HEAD_DIM, :] = v[:, g * BLOCK:(g + 1) * BLOCK]


def _key_exists(n, nb):
    j = lax.broadcasted_iota(jnp.int32, (3 * BLOCK, 1), 0)
    return ((n > 0) | (j >= BLOCK)) & ((n < nb - 1) | (j < 2 * BLOCK))


def _attn_probs(qs, kcat, bias_g, sink_g, key_ok):
    logits = jnp.where(key_ok, dot_nn(kcat, qs) + bias_g, NEG_INF)
    m = jnp.maximum(jnp.max(logits, axis=0, keepdims=True), sink_g)
    p = jnp.exp(logits - m)
    es = jnp.exp(sink_g - m)
    inv = 1.0 / (jnp.sum(p, axis=0, keepdims=True) + es)
    return p * inv, es * inv


def attention_forward(qt, proj, vt, bias, sink, kblk):
    att, s_len = qt.shape
    kvw = vt.shape[0]
    nb = s_len // BLOCK

    def body(q_ref, kp_ref, kc_ref, kn_ref, vp_ref, vc_ref, vn_ref, b_ref, s_ref, o_ref):
        n = pl.program_id(0)
        q = q_ref[...]
        key_ok = _key_exists(n, nb)
        kall = jnp.concatenate([kp_ref[...], kc_ref[...], kn_ref[...]], axis=0).astype(BF)
        vall = jnp.concatenate([vp_ref[...], vc_ref[...], vn_ref[...]], axis=1).astype(BF)
        for kh in range(N_KV_HEADS):
            qs = (_stack_heads(q, kh) * (HEAD_DIM ** -0.5)).astype(BF)
            p, _ = _attn_probs(qs, kall[:, kh * HEAD_DIM:(kh + 1) * HEAD_DIM], b_ref[kh], s_ref[kh, 0:1, :], key_ok)
            _unstack_heads(o_ref, kh, dot_nn(vall[kh * HEAD_DIM:(kh + 1) * HEAD_DIM, :], p.astype(BF)))

    blk = pl.BlockSpec((att, BLOCK), lambda n: (0, n))
    return pl.pallas_call(
        body, name="attention_forward", grid=(nb,), out_shape=SDS((att, s_len), F32),
        in_specs=[blk] + _window_specs(s_len, kblk) + _window_specs(s_len, 0, kvw)
        + [pl.BlockSpec(bias.shape, lambda n: (0, 0, 0)), pl.BlockSpec(sink.shape, lambda n: (0, 0, 0))],
        out_specs=blk, compiler_params=_cparams(),
    )(qt, proj, proj, proj, vt, vt, vt, bias, sink)


def attention_backward(qt, proj, y_att, dy, bias, sink, kblk):
    att, s_len = qt.shape
    nb = s_len // BLOCK
    kvw = N_KV_HEADS * HEAD_DIM

    def body(q_ref, kp_ref, kc_ref, kn_ref, vp_ref, vc_ref, vn_ref, o_ref, do_ref, b_ref, s_ref,
             dq_ref, dkv_ref, db_ref, ds_ref):
        n = pl.program_id(0)

        @pl.when(n == 0)
        def _():
            dkv_ref[...] = jnp.zeros_like(dkv_ref)
            db_ref[...] = jnp.zeros_like(db_ref)
            ds_ref[...] = jnp.zeros_like(ds_ref)

        q = q_ref[...]
        o = o_ref[...]
        do = do_ref[...]
        kall = jnp.concatenate([kp_ref[...], kc_ref[...], kn_ref[...]], axis=0).astype(BF)
        vall = jnp.concatenate([vp_ref[...], vc_ref[...], vn_ref[...]], axis=0).astype(BF)
        key_ok = _key_exists(n, nb)
        dks, dvs = [], []
        for kh in range(N_KV_HEADS):
            kcat = kall[:, kh * HEAD_DIM:(kh + 1) * HEAD_DIM]
            vcat = vall[:, kh * HEAD_DIM:(kh + 1) * HEAD_DIM]
            qs = (_stack_heads(q, kh) * (HEAD_DIM ** -0.5)).astype(BF)
            p, ps = _attn_probs(qs, kcat, b_ref[kh], s_ref[kh, 0:1, :], key_ok)
            dos = _stack_heads(do, kh)
            dosb = dos.astype(BF)
            delta = jnp.sum(dos * _stack_heads(o, kh), axis=0, keepdims=True)
            dlog = p * (dot_nn(vcat, dosb) - delta)
            dlogb = dlog.astype(BF)
            db_ref[kh] += dlog
            ds_ref[kh] += jnp.broadcast_to(-ps * delta, ds_ref.shape[1:])
            _unstack_heads(dq_ref, kh, dot_tn(kcat, dlogb) * (HEAD_DIM ** -0.5))
            dks.append(dot_nt(dlogb, qs))
            dvs.append(dot_nt(p.astype(BF), dosb))
        dkv = jnp.concatenate(dks + dvs, axis=1)
        starts = [jnp.maximum(n - 1, 0), n, jnp.minimum(n + 1, nb - 1)]
        for b, st in enumerate(starts):
            rows = pl.ds(pl.multiple_of(st * BLOCK, BLOCK), BLOCK)
            dkv_ref[rows, :] += dkv[b * BLOCK:(b + 1) * BLOCK, :]

    blk = pl.BlockSpec((att, BLOCK), lambda n: (0, n))
    whole = lambda a: pl.BlockSpec(a.shape, lambda n: (0, 0, 0))
    return pl.pallas_call(
        body, name="attention_backward", grid=(nb,),
        out_shape=(SDS((att, s_len), F32), SDS((s_len, 2 * kvw), F32), SDS(bias.shape, F32), SDS(sink.shape, F32)),
        in_specs=[blk] + _window_specs(s_len, kblk) + _window_specs(s_len, kblk + 1) + [blk, blk, whole(bias), whole(sink)],
        out_specs=(blk, pl.BlockSpec((s_len, 2 * kvw), lambda n: (0, 0)), whole(bias), whole(sink)),
        compiler_params=_cparams(),
    )(qt, proj, proj, proj, proj, proj, proj, y_att, dy, bias, sink)


def _rms_cols(x, g):
    rs = lax.rsqrt(jnp.mean(x * x, axis=0, keepdims=True) + EPS)
    xh = x * rs
    return xh, rs, xh * g


def _rms_cols_bwd(dy, xh, rs, g):
    dxh = dy * g
    dx = rs * (dxh - xh * jnp.mean(dxh * xh, axis=0, keepdims=True))
    return dx, dy * xh


def mix_output(x, y_rec, y_att, g_rec, g_att, wfull, lay, tm=512):
    s_len, d = x.shape
    tm = min(tm, s_len)
    lw = y_rec.shape[0] * LANE
    att = y_att.shape[0]

    def body(x_ref, yr_ref, ya_ref, gr_ref, ga_ref, w_ref, o_ref):
        _, _, nr = _rms(_join_lane_blocks(yr_ref), gr_ref[...])
        _, _, na = _rms_cols(ya_ref[...], ga_ref[...])
        w = w_ref[:, :, lay.ih:, :].reshape(d, d)
        o_ref[...] = x_ref[...] + dot_nn(nr.astype(BF), w[:lw]) + dot_tn(na.astype(BF), w[lw:])

    row = pl.BlockSpec((tm, d), lambda i: (i, 0))
    return pl.pallas_call(
        body, name="mix_output", grid=(s_len // tm,), out_shape=SDS((s_len, d), F32),
        in_specs=[row, _cbm_spec(lw // LANE, tm), pl.BlockSpec((att, tm), lambda i: (0, i)),
                  pl.BlockSpec((1, lw), lambda i: (0, 0)), pl.BlockSpec((att, 1), lambda i: (0, 0)),
                  _w_spec(lay.fh, d, lay.MIX_BLK)],
        out_specs=row, compiler_params=_cparams(),
    )(x, y_rec, y_att, g_rec, g_att, wfull)


def mix_output_backward(dout, y_rec, y_att, g_rec, g_att, wfull, lay, deps=(), tm=1024):
    s_len, d = dout.shape
    tm = min(tm, s_len)
    lw = y_rec.shape[0] * LANE
    att = y_att.shape[0]
    nt = s_len // tm

    def body(do_ref, yr_ref, ya_ref, gr_ref, ga_ref, w_ref, *rest):
        dyr_ref, dya_ref, dgr_ref, dga_ref, o_ref, acc = rest[len(deps):]
        i = pl.program_id(0)
        gr = gr_ref[...]
        ga = ga_ref[...]
        xhr, rsr, nr = _rms(_join_lane_blocks(yr_ref), gr)
        xha, rsa, na = _rms_cols(ya_ref[...], ga)
        dob = do_ref[...].astype(BF)
        w = w_ref[:, :, lay.ih:, :].reshape(d, d)
        dyr, dgr_row = _rms_bwd(dot_nt(dob, w[:lw]), xhr, rsr, gr)
        dya, dga_col = _rms_cols_bwd(dot_nt(w[lw:], dob), xha, rsa, ga)
        for j, piece in enumerate(_lane_blocks(dyr)):
            dyr_ref[j] = piece
        dya_ref[...] = dya

        @pl.when(i == 0)
        def _():
            dgr_ref[...] = jnp.zeros_like(dgr_ref)
            dga_ref[...] = jnp.zeros_like(dga_ref)
            acc[...] = jnp.zeros_like(acc)

        dgr_ref[...] += jnp.sum(dgr_row, axis=0, keepdims=True)
        dga_ref[...] += jnp.sum(dga_col, axis=1, keepdims=True)
        acc[0:lw, :] += dot_tn(nr.astype(BF), dob)
        acc[lw:, :] += dot_nn(na.astype(BF), dob)

        @pl.when(i == nt - 1)
        def _():
            for p in range(N_CHIPS):
                for q in range(2):
                    o_ref[p, q] = acc[pl.ds((2 * p + q) * lay.oh, lay.oh), :].astype(o_ref.dtype)

    row = pl.BlockSpec((tm, d), lambda i: (i, 0))
    return pl.pallas_call(
        body, name="mix_output_backward", grid=(nt,),
        out_shape=(SDS(y_rec.shape, F32), SDS(y_att.shape, F32), SDS((1, lw), F32), SDS((att, 1), F32),
                   SDS((N_CHIPS, 2, lay.oh, d), BF)),
        in_specs=[row, _cbm_spec(lw // LANE, tm), pl.BlockSpec((att, tm), lambda i: (0, i)),
                  pl.BlockSpec((1, lw), lambda i: (0, 0)), pl.BlockSpec((att, 1), lambda i: (0, 0)),
                  _w_spec(lay.fh, d, lay.MIX_BLK)] + [ANY] * len(deps),
        out_specs=(_cbm_spec(lw // LANE, tm), pl.BlockSpec((att, tm), lambda i: (0, i)),
                   pl.BlockSpec((1, lw), lambda i: (0, 0)), pl.BlockSpec((att, 1), lambda i: (0, 0)),
                   pl.BlockSpec((N_CHIPS, 2, lay.oh, d), lambda i: (0, 0, 0, 0))),
        scratch_shapes=[pltpu.VMEM((d, d), F32)], compiler_params=_cparams(),
    )(dout, y_rec, y_att, g_rec, g_att, wfull, *deps)


def loss_head(x, gain, target, tm=512):
    s_len, d = x.shape
    tm = min(tm, s_len)

    def body(x_ref, g_ref, t_ref, dx_ref, dg_ref, loss_ref):
        g = g_ref[...]
        xh, rs, y = _rms(x_ref[...], g)
        err = y - t_ref[...]

        @pl.when(pl.program_id(0) == 0)
        def _():
            dg_ref[...] = jnp.zeros_like(dg_ref)
            loss_ref[...] = jnp.zeros_like(loss_ref)

        part = 0.5 * jnp.sum(jnp.mean(err * err, axis=-1, keepdims=True), axis=0, keepdims=True)
        loss_ref[...] += jnp.broadcast_to(part, loss_ref.shape)
        dx, dgrow = _rms_bwd(err * (1.0 / d), xh, rs, g)
        dx_ref[...] = dx
        dg_ref[...] += jnp.sum(dgrow, axis=0, keepdims=True)

    row = pl.BlockSpec((tm, d), lambda i: (i, 0))
    vec = pl.BlockSpec((1, d), lambda i: (0, 0))
    return pl.pallas_call(
        body, name="loss_head", grid=(s_len // tm,),
        out_shape=(SDS((s_len, d), F32), SDS((1, d), F32), SDS((8, LANE), F32)),
        in_specs=[row, vec, row], out_specs=(row, vec, pl.BlockSpec((8, LANE), lambda i: (0, 0))),
        compiler_params=_cparams(),
    )(x, gain, target)


def _adamw_update(w, g, m, v):
    m = ADAM_B1 * m + (1.0 - ADAM_B1) * g
    v = ADAM_B2 * v + (1.0 - ADAM_B2) * (g * g)
    m_hat = m / (1.0 - ADAM_B1 ** ADAM_STEP)
    v_hat = v / (1.0 - ADAM_B2 ** ADAM_STEP)
    return -ADAM_LR * (m_hat / (jnp.sqrt(v_hat) + ADAM_EPS) + ADAM_WD * w), m, v


def adamw(w, g, m, v, tr=512):
    rows, cols = w.shape
    tr = _row_chunk(rows, tr, 8)

    def body(w_ref, g_ref, m_ref, v_ref, d_ref, nm_ref, nv_ref):
        d_ref[...], nm_ref[...], nv_ref[...] = _adamw_update(w_ref[...], g_ref[...], m_ref[...], v_ref[...])

    blk = pl.BlockSpec((tr, cols), lambda i: (i, 0))
    return pl.pallas_call(
        body, name="adamw", grid=(rows // tr,), out_shape=(SDS(w.shape, F32),) * 3,
        in_specs=[blk] * 4, out_specs=(blk,) * 3, compiler_params=_cparams(),
    )(w, g, m, v)


def adamw_layer(gf, blk, row_off, n_half, l, w, m, v, outs, deps=()):
    fh = gf.shape[1] // Layout.BLOCKS
    d = gf.shape[2]
    nd = len(deps)
    steps = 2 if n_half % (2 * SUBLANES) == 0 else 1
    rows = n_half // steps

    def body(gf_ref, w_ref, m_ref, v_ref, *rest):
        g_ref, d_ref, nm_ref, nv_ref = rest[4 + nd:]
        g = gf_ref[pl.ds(pl.multiple_of(row_off + pl.program_id(1) * rows, SUBLANES), rows), :]
        g_ref[...] = g
        d_ref[...], nm_ref[...], nv_ref[...] = _adamw_update(w_ref[...], g, m_ref[...], v_ref[...])

    gspec = pl.BlockSpec((None, fh, d), lambda h, q: (h, blk, 0))
    wspec = pl.BlockSpec((None, rows, d), lambda h, q: (l, h * steps + q, 0))
    return pl.pallas_call(
        body, name="adamw_layer", grid=(2, steps), out_shape=tuple(SDS(o.shape, o.dtype) for o in outs),
        in_specs=[gspec, wspec, wspec, wspec] + [ANY] * (4 + nd), out_specs=(wspec,) * 4,
        input_output_aliases={4 + i: i for i in range(4)}, compiler_params=_cparams(),
    )(gf, w, m, v, *outs, *deps)


def pack_weight(pos, land, blk, l, w, extra=None, deps=()):
    fh, d = land.shape[2] // Layout.BLOCKS, land.shape[3]
    nd = len(deps)

    def body(pos_ref, w_ref, *rest):
        o_ref = rest[-1]
        a = w_ref[...].astype(BF)
        n = a.shape[0] // 2
        for h in range(2):
            o_ref[h, 0:n, :] = a[h * n:(h + 1) * n]
        if extra is not None:
            b = rest[0][...].astype(BF)
            nb = b.shape[0] // 2
            for h in range(2):
                o_ref[h, n:n + nb, :] = b[h * nb:(h + 1) * nb]

    def whole(a):
        return pl.BlockSpec((None,) + a.shape[1:], lambda i, p: (l, 0, 0))

    ins = [w] + ([extra] if extra is not None else [])
    return pl.pallas_call(
        body, name="pack_weight", out_shape=SDS(land.shape, land.dtype),
        grid_spec=pltpu.PrefetchScalarGridSpec(
            num_scalar_prefetch=1, grid=(1,),
            in_specs=[whole(a) for a in ins] + [ANY] * (1 + nd),
            out_specs=pl.BlockSpec((None, 2, fh, d), lambda i, p: (p[0], 0, blk, 0))),
        input_output_aliases={1 + len(ins): 0}, compiler_params=_cparams(),
    )(pos, *ins, land, *deps)


def _rows_of(shape, width):
    return -(-int(np.prod(shape)) // (SUBLANES * width)) * SUBLANES


def _pack_rows(arrays, width):
    parts = []
    for a in arrays:
        flat = a.reshape(-1).astype(F32)
        r = _rows_of(a.shape, width)
        parts.append(jnp.pad(flat, (0, r * width - flat.shape[0])).reshape(r, width))
    return jnp.concatenate(parts, axis=0)


def _unpack_rows(buf, shapes):
    out, row = [], 0
    for shp in shapes:
        r = _rows_of(shp, buf.shape[1])
        out.append(buf[row:row + r].reshape(-1)[:int(np.prod(shp))].reshape(shp))
        row += r
    return out


def _t5_buckets(rel):
    half = N_BUCKETS // 2
    max_exact = half // 2
    ret = (rel > 0).astype(jnp.int32) * half
    n = jnp.abs(rel)
    n_f = jnp.maximum(n, 1).astype(F32)
    large = max_exact + (jnp.log(n_f / max_exact) / math.log(MAX_DISTANCE / max_exact) * (half - max_exact)).astype(jnp.int32)
    large = jnp.minimum(large, half - 1)
    return ret + jnp.where(n < max_exact, n, large)


def _band_buckets():
    t = jnp.arange(BLOCK)[:, None]
    j = jnp.arange(3 * BLOCK)[None, :]
    rel = j - BLOCK - t
    return _t5_buckets(rel), jnp.abs(rel) <= WINDOW


def _block_diag_pairs(w):
    depth, two, nblk, bw, _ = w.shape
    pairs = w.reshape(depth, two, nblk // 2, 2, bw, bw)
    z = jnp.zeros_like(pairs[:, :, :, 0])
    top = jnp.concatenate([pairs[:, :, :, 0], z], axis=-1)
    bot = jnp.concatenate([z, pairs[:, :, :, 1]], axis=-1)
    return jnp.concatenate([top, bot], axis=-2)


def _diag_blocks(dw):
    bw = dw.shape[-1] // 2
    a = dw[:, :, :bw, :bw]
    b = dw[:, :, bw:, bw:]
    return jnp.stack([a, b], axis=2).reshape(dw.shape[0], 2 * dw.shape[1], bw, bw)


def kernel(x, ffn1_norm, ffn1_w_gate, ffn1_w_up, ffn1_w_down, mix_norm, w_in, conv_w, conv_b, lru_w_a, lru_b_a, lru_w_x, lru_b_x, lru_lambda, attn_sink, rel_bias, lru_out_norm, attn_out_norm, w_out, ffn2_norm, ffn2_w_gate, ffn2_w_up, ffn2_w_down, final_norm, loss_target, m_ffn1_norm, m_ffn1_w_gate, m_ffn1_w_up, m_ffn1_w_down, m_mix_norm, m_w_in, m_conv_w, m_conv_b, m_lru_w_a, m_lru_b_a, m_lru_w_x, m_lru_b_x, m_lru_lambda, m_attn_sink, m_rel_bias, m_lru_out_norm, m_attn_out_norm, m_w_out, m_ffn2_norm, m_ffn2_w_gate, m_ffn2_w_up, m_ffn2_w_down, m_final_norm, v_ffn1_norm, v_ffn1_w_gate, v_ffn1_w_up, v_ffn1_w_down, v_mix_norm, v_w_in, v_conv_w, v_conv_b, v_lru_w_a, v_lru_b_a, v_lru_w_x, v_lru_b_x, v_lru_lambda, v_attn_sink, v_rel_bias, v_lru_out_norm, v_attn_out_norm, v_w_out, v_ffn2_norm, v_ffn2_w_gate, v_ffn2_w_up, v_ffn2_w_down, v_final_norm):
    depth, d = ffn1_norm.shape
    d_ff = N_CHIPS * ffn1_w_gate.shape[2]
    d_in = N_CHIPS * w_in.shape[2]
    lw = conv_b.shape[1]
    att = N_HEADS * HEAD_DIM
    lay = Layout(d, d_ff, d_in)
    k_chip = 2 * lax.axis_index("x") + lax.axis_index("y")
    pos = jnp.stack([k_chip, lax.axis_index("c")]).astype(jnp.int32)

    def rows_major(a):
        return jnp.swapaxes(a, 1, 2)

    mats = (rows_major(ffn1_w_gate), rows_major(ffn1_w_up), ffn1_w_down,
            rows_major(ffn2_w_gate), rows_major(ffn2_w_up), ffn2_w_down)

    def pack_layer(l, deps=()):
        land = lax.empty((N_CHIPS, 2, lay.rows, d), BF)
        for m, a in enumerate(mats):
            land = pack_weight(pos, land, m, l, a, deps=deps if m == 0 else ())
        return pack_weight(pos, land, lay.MIX_BLK, l, rows_major(w_in), extra=w_out)

    def gather_start(l, land):
        return split_start(f"gather_start_{l}", [land], 3, gather_plan)

    def gather_wait(l, started, after):
        ssem, rsem, bufs, _ = started
        return split_wait(f"gather_wait_{l}", ssem, rsem, bufs, after, gather_plan)

    sharded_small = (conv_w, lru_b_a, lru_b_x, lru_lambda)
    sshard = jnp.concatenate([a.reshape(-1, LANE) for a in sharded_small], axis=0)
    sfull = gather_small(sshard)
    small_full, off = [], 0
    for a in sharded_small:
        r = a.shape[0] * a.shape[1]
        piece = sfull[:, off:off + r].reshape((N_CHIPS,) + a.shape)
        small_full.append(jnp.moveaxis(piece, 0, 2).reshape(a.shape[0], a.shape[1], N_CHIPS * LANE))
        off += r
    conv_w_f, b_a_f, b_x_f, lam_f = small_full

    zrow = jnp.zeros((1, lw), F32)
    wblk_a = _block_diag_pairs(lru_w_a)
    wblk_x = _block_diag_pairs(lru_w_x)
    buckets, in_band = _band_buckets()
    onehot = (buckets.reshape(-1)[:, None] == jnp.arange(N_BUCKETS)[None, :]).astype(F32)
    bias = jnp.dot(rel_bias.T, onehot.T, precision=lax.Precision.HIGHEST).reshape(N_HEADS, BLOCK, 3 * BLOCK)
    bias = jnp.where(in_band[None], bias, NEG_INF)
    bias = bias.reshape(N_KV_HEADS, KV_GROUP, BLOCK, 3 * BLOCK).transpose(0, 3, 1, 2).reshape(N_KV_HEADS, 3 * BLOCK, KV_GROUP * BLOCK)
    kblk = 2 * lw // LANE

    def layer_small(l):
        cvec = jnp.concatenate([conv_w_f[l], jnp.zeros((8 - CONV_WIDTH, lw), F32)], axis=0)
        pvec = jnp.concatenate([conv_b[l][None], b_a_f[l], b_x_f[l], lam_f[l], zrow], axis=0)
        wblk = jnp.stack([wblk_a[l, 0], wblk_x[l, 0], wblk_a[l, 1], wblk_x[l, 1]]).astype(BF)
        sink = jnp.broadcast_to(jnp.repeat(attn_sink[l], BLOCK).reshape(N_KV_HEADS, 1, KV_GROUP * BLOCK),
                                (N_KV_HEADS, 8, KV_GROUP * BLOCK))
        return cvec, pvec, wblk, sink

    xs = x[0]
    wfull = [None] * depth
    parts = [(0, 3 * lay.fh), (3 * lay.fh, lay.rows - 3 * lay.fh)]
    plans = [(functools.partial(gather_plan, rows=p), functools.partial(handover_plan, rows=p)) for p in parts]
    land = pack_layer(0, deps=(sfull,))
    first = split_start("gather_start_0a", [land], 3, plans[0][0])
    second = split_start("gather_start_0b", first[2], 3, plans[1][0])
    lands = {l: pack_layer(l, deps=(second[3],)) for l in range(1, depth)}
    land = split_wait("gather_wait_0a", first[0], first[1], second[2], [xs] + list(lands.values()), plans[0][0])
    wfull[0], = exchange_now("gather_handover_0a", land, 3, plans[0][1])
    started = None
    saved = []
    for l in range(depth):
        cvec, pvec, wblk, sink = layer_small(l)
        deps = (started[3],) if started is not None else ()
        x1, gate1, up1 = ffn_forward(xs, ffn1_norm[l][None], wfull[l], lay, 0, deps=deps)
        deps = ()
        if l == 0:
            land = split_wait("gather_wait_0b", second[0], second[1], [wfull[0]], [x1], plans[1][0])
            wfull[0], = exchange_now("gather_handover_0b", land, 3, plans[1][1])
            if depth > 1:
                started = gather_start(1, lands[1])
                deps = (started[3],)
        proj, qt, vt = mix_project(x1, mix_norm[l][None], wfull[l], lay, lw, att)
        y_rec, hs = lru_forward(proj, cvec, pvec, wblk, lw, deps=deps)
        y_att = attention_forward(qt, proj, vt, bias, sink, kblk)
        x2 = mix_output(x1, y_rec, y_att, lru_out_norm[l][None], attn_out_norm[l][:, None], wfull[l], lay)
        deps, handover = (), None
        if 0 < l < depth - 1:
            land, = gather_wait(l + 1, started, [x2])
            started = gather_start(l + 2, lands[l + 2]) if l + 2 < depth else None
            handover = split_start(f"gather_handover_start_{l + 1}", [land], 3, handover_plan)
            deps = (handover[3],) + ((started[3],) if started is not None else ())
        x3, gate2, up2 = ffn_forward(x2, ffn2_norm[l][None], wfull[l], lay, 1, deps=deps)
        saved.append((xs, x1, x2, proj, qt, y_rec, hs, y_att, (gate1, up1), (gate2, up2)))
        xs = x3
        if handover is not None:
            wfull[l + 1], = split_wait(f"gather_handover_wait_{l + 1}", handover[0], handover[1], handover[2], [x3],
                                       handover_plan)
        elif l == 0 and depth > 1:
            wfull[1], = exchange_now("gather_handover_1", gather_wait(1, started, [x3]), 3, handover_plan)
            started = gather_start(2, lands[2]) if depth > 2 else None

    dx, d_final, loss_tile = loss_head(xs, final_norm[None], loss_target[0])
    loss = lax.psum(loss_tile[0, 0], ("x", "y", "c"))

    layer_names = ["ffn1_norm", "mix_norm", "conv_w", "conv_b", "lru_w_a", "lru_b_a", "lru_w_x", "lru_b_x", "lru_lambda",
                   "attn_sink", "lru_out_norm", "attn_out_norm", "ffn2_norm"]
    dbias_total = jnp.zeros(bias.shape, F32)

    def ffn_back(xin, gain, dout, pre, gb, l, which, deps=()):
        dxo, dg, lhs, rhs = ffn_backward_dx(xin, gain, dout, *pre, wfull[l], lay, which, deps=deps)
        return dxo, dg[0], weight_grad_tn(lhs, rhs, gb, lay, 3 * which)

    def pair_start(l, gb, sb):
        lands = [lax.empty((N_CHIPS,) + gb.shape[2:], gb.dtype), lax.empty(sb.shape, sb.dtype)]
        return split_start(f"pair_start_{l}", [gb, sb] + lands, N_CHIPS + 1, pair_plan)

    def reduce_start(l, paired, after):
        gb, sb, p1, sp1 = split_wait(f"pair_wait_{l}", paired[0], paired[1], paired[2], after, pair_plan)
        cs = pair_sum(pos, gb, p1)
        ss = small_pair_sum(sb, sp1)
        lands = [lax.empty((3,) + cs.shape[1:], cs.dtype), lax.empty((N_CHIPS,) + ss.shape, ss.dtype)]
        return split_start(f"reduce_start_{l}", [cs, ss] + lands, 6, reduce_plan)

    def reduce_finish(l, started, after):
        ssem, rsem, bufs, _ = started
        cs, ss, p3, sp3 = split_wait(f"reduce_wait_{l}", ssem, rsem, bufs, after, reduce_plan)
        return chip_sum(pos, cs, p3), small_chip_sum(pos, ss, sp3)

    gf = [None] * depth
    small_sums = [None] * depth
    small_shapes = [None] * depth
    paired = None
    in_flight = None
    finals = {}
    tokens = []
    for l in reversed(range(depth)):
        x0, x1, x2, proj, qt, y_rec, hs, y_att, pre1, pre2 = saved[l]
        cvec, pvec, wblk, sink = layer_small(l)
        gb = lax.empty((N_CHIPS, 2, lay.rows, d), BF)
        part = {}
        dx, part["ffn2_norm"], gb = ffn_back(x2, ffn2_norm[l][None], dx, pre2, gb, l, 1, deps=tuple(tokens))
        deps = ()
        if paired is not None:
            in_flight = (paired[0], reduce_start(paired[0], paired[1], [dx, gb]))
            deps = (in_flight[1][3],)
        dyr, dya, dgr, dga, dwout = mix_output_backward(dx, y_rec, y_att, lru_out_norm[l][None], attn_out_norm[l][:, None],
                                                        wfull[l], lay, deps=deps)
        part["lru_out_norm"] = dgr[0]
        part["attn_out_norm"] = dga[:, 0]
        dq, dkv, dbias, dsink = attention_backward(qt, proj, y_att, dya, bias, sink, kblk)
        dbias_total = dbias_total + dbias
        part["attn_sink"] = jnp.sum(dsink[:, 0, :].reshape(N_HEADS, BLOCK), axis=1)
        dxr, dgt, dcv, dpv, dwb = lru_backward(proj, hs, dyr, cvec, pvec, wblk, lw)
        part["conv_w"] = dcv[:CONV_WIDTH]
        part["conv_b"] = dpv[0]
        part["lru_b_a"] = dpv[1:3]
        part["lru_b_x"] = dpv[3:5]
        part["lru_lambda"] = dpv[5:7]
        part["lru_w_a"] = _diag_blocks(jnp.stack([dwb[0], dwb[2]]))
        part["lru_w_x"] = _diag_blocks(jnp.stack([dwb[1], dwb[3]]))
        dx, dgm, gb = mix_project_backward(x1, mix_norm[l][None], dx, dxr, dgt, dq, dkv, dwout, wfull[l], gb, lay)
        part["mix_norm"] = dgm[0]
        dx, part["ffn1_norm"], gb = ffn_back(x0, ffn1_norm[l][None], dx, pre1, gb, l, 0)
        pieces = [part[n] for n in layer_names]
        if l == 0:
            dbias_heads = dbias_total.reshape(N_KV_HEADS, 3 * BLOCK, KV_GROUP, BLOCK).transpose(0, 2, 3, 1)
            d_rel_bias = jnp.dot(dbias_heads.reshape(N_HEADS, -1), onehot, precision=lax.Precision.HIGHEST).T
            pieces += [d_rel_bias, d_final[0]]
        small_shapes[l] = [p.shape for p in pieces]
        paired = (l, pair_start(l, gb, _pack_rows(pieces, 1024)))
        tokens = [paired[1][3]]
        if in_flight is not None:
            above = in_flight[0]
            half, small_sums[above] = reduce_finish(above, in_flight[1], [dx])
            finals[above] = split_start(f"final_start_{above}", [half], 1, final_plan)
            tokens.append(finals[above][3])
            in_flight = None
    grad_x = dx[None]

    weights = dict(ffn1_norm=ffn1_norm, ffn1_w_gate=ffn1_w_gate, ffn1_w_up=ffn1_w_up, ffn1_w_down=ffn1_w_down, mix_norm=mix_norm, w_in=w_in, conv_w=conv_w, conv_b=conv_b, lru_w_a=lru_w_a, lru_b_a=lru_b_a, lru_w_x=lru_w_x, lru_b_x=lru_b_x, lru_lambda=lru_lambda, attn_sink=attn_sink, rel_bias=rel_bias, lru_out_norm=lru_out_norm, attn_out_norm=attn_out_norm, w_out=w_out, ffn2_norm=ffn2_norm, ffn2_w_gate=ffn2_w_gate, ffn2_w_up=ffn2_w_up, ffn2_w_down=ffn2_w_down, final_norm=final_norm)
    m_in = dict(ffn1_norm=m_ffn1_norm, ffn1_w_gate=m_ffn1_w_gate, ffn1_w_up=m_ffn1_w_up, ffn1_w_down=m_ffn1_w_down, mix_norm=m_mix_norm, w_in=m_w_in, conv_w=m_conv_w, conv_b=m_conv_b, lru_w_a=m_lru_w_a, lru_b_a=m_lru_b_a, lru_w_x=m_lru_w_x, lru_b_x=m_lru_b_x, lru_lambda=m_lru_lambda, attn_sink=m_attn_sink, rel_bias=m_rel_bias, lru_out_norm=m_lru_out_norm, attn_out_norm=m_attn_out_norm, w_out=m_w_out, ffn2_norm=m_ffn2_norm, ffn2_w_gate=m_ffn2_w_gate, ffn2_w_up=m_ffn2_w_up, ffn2_w_down=m_ffn2_w_down, final_norm=m_final_norm)
    v_in = dict(ffn1_norm=v_ffn1_norm, ffn1_w_gate=v_ffn1_w_gate, ffn1_w_up=v_ffn1_w_up, ffn1_w_down=v_ffn1_w_down, mix_norm=v_mix_norm, w_in=v_w_in, conv_w=v_conv_w, conv_b=v_conv_b, lru_w_a=v_lru_w_a, lru_b_a=v_lru_b_a, lru_w_x=v_lru_w_x, lru_b_x=v_lru_b_x, lru_lambda=v_lru_lambda, attn_sink=v_attn_sink, rel_bias=v_rel_bias, lru_out_norm=v_lru_out_norm, attn_out_norm=v_attn_out_norm, w_out=v_w_out, ffn2_norm=v_ffn2_norm, ffn2_w_gate=v_ffn2_w_gate, ffn2_w_up=v_ffn2_w_up, ffn2_w_down=v_ffn2_w_down, final_norm=v_final_norm)
    order = list(weights)
    large = [(name, m, 0, lay.fh, m % 3 != 2) for m, name in
             enumerate(("ffn1_w_gate", "ffn1_w_up", "ffn1_w_down", "ffn2_w_gate", "ffn2_w_up", "ffn2_w_down"))]
    large += [("w_in", lay.MIX_BLK, 0, lay.ih, True), ("w_out", lay.MIX_BLK, lay.ih, lay.oh, False)]
    as_rows = {name: [rows_major(src[name]) if flip else src[name] for src in (weights, m_in, v_in)]
               for name, _, _, _, flip in large}
    stacked = {name: tuple(lax.empty(as_rows[name][0].shape, F32) for _ in range(4)) for name, *_ in large}

    def adamw_large(l, deps=()):
        for i, (name, blk, row_off, n_half, _) in enumerate(large):
            stacked[name] = adamw_layer(gf[l], blk, row_off, n_half, l, *as_rows[name], stacked[name],
                                        deps=deps if i == 0 else ())

    last = paired[0]
    crossing = reduce_start(last, paired[1], [dx])
    for l in sorted(finals):
        gf[l], = split_wait(f"final_wait_{l}", finals[l][0], finals[l][1], finals[l][2], [crossing[3]], final_plan)
        adamw_large(l, deps=(crossing[3],))
    ready = [buf for name, *_ in large for buf in stacked[name]] if depth > 1 else []
    half, small_sums[last] = reduce_finish(last, crossing, [dx] + ready)
    gf[last], = exchange_now(f"final_now_{last}", [half], 1, final_plan)
    adamw_large(last)

    per_layer = [_unpack_rows(small_sums[l], small_shapes[l]) for l in range(depth)]
    grads = {n: jnp.stack([per_layer[l][i] for l in range(depth)]) for i, n in enumerate(layer_names)}
    grads["rel_bias"], grads["final_norm"] = per_layer[0][len(layer_names):]
    for name in ("conv_w", "lru_b_a", "lru_b_x", "lru_lambda"):
        grads[name] = lax.dynamic_slice_in_dim(grads[name], k_chip * LANE, LANE, axis=2)
    delta, new_m, new_v = {}, {}, {}
    for name, _, _, _, flip in large:
        grads[name], delta[name], new_m[name], new_v[name] = [rows_major(a) if flip else a for a in stacked[name]]
    small = [n for n in order if n not in stacked]
    packed = [_pack_rows([src[n] for n in small], 1024) for src in (weights, grads, m_in, v_in)]
    outs = adamw(*packed)
    shapes = [weights[n].shape for n in small]
    for dst, buf in zip((delta, new_m, new_v), outs):
        dst.update(zip(small, _unpack_rows(buf, shapes)))

    return (loss, grad_x, *[grads[n] for n in order], *[delta[n] for n in order],
            *[new_m[n] for n in order], *[new_v[n] for n in order])
```

```python
import functools
import math

import jax
import jax.numpy as jnp
import numpy as np
from jax import lax
from jax.experimental import pallas as pl
from jax.experimental.pallas import tpu as pltpu

BF = jnp.bfloat16
F32 = jnp.float32
SDS = jax.ShapeDtypeStruct
MESH = pl.DeviceIdType.MESH
ANY = pl.BlockSpec(memory_space=pl.ANY)

N_CHIPS = 4
N_HEADS = 8
N_KV_HEADS = 2
KV_GROUP = N_HEADS // N_KV_HEADS
HEAD_DIM = 64
BLOCK = 128
WINDOW = 128
N_BUCKETS = 32
MAX_DISTANCE = 128
LRU_C = 8.0
CONV_WIDTH = 4
LANE = 128
SUBLANES = 8
MXU_WIDTH = 256
SCAN_CHAINS = 8
EPS = 1e-6
FFN_RES = 0.5
NEG_INF = -1e30
ADAM_LR = 0.001
ADAM_B1 = 0.9
ADAM_B2 = 0.999
ADAM_EPS = 1e-08
ADAM_WD = 0.01
ADAM_STEP = 10
VMEM_LIMIT = 60000 * 1024
GELU_C = math.sqrt(2.0 / math.pi)


def dot_nn(a, b):
    return lax.dot_general(a, b, (((1,), (0,)), ((), ())), preferred_element_type=F32)


def dot_nt(a, b):
    return lax.dot_general(a, b, (((1,), (1,)), ((), ())), preferred_element_type=F32)


def dot_tn(a, b):
    return lax.dot_general(a, b, (((0,), (0,)), ((), ())), preferred_element_type=F32)


def _cparams(**kw):
    return pltpu.CompilerParams(vmem_limit_bytes=VMEM_LIMIT, **kw)


class Layout:
    MIX_BLK = 6
    BLOCKS = 7

    def __init__(self, d_model, d_ff, d_in):
        self.fh = d_ff // (2 * N_CHIPS)
        self.ih = d_in // (2 * N_CHIPS)
        self.oh = d_model // (2 * N_CHIPS)
        assert self.ih + self.oh == self.fh, "w_in^T and w_out rows must fill one ffn-sized block"
        self.rows = self.BLOCKS * self.fh


def _row_chunk(rows, target, step=16):
    best = rows
    for c in range(step, min(rows, target) + 1, step):
        if rows % c == 0:
            best = c
    return best


def _mesh_pos():
    return lax.axis_index("x"), lax.axis_index("y"), lax.axis_index("c")


def _rcopy(src, dst, ssem, rsem, dev):
    return pltpu.make_async_remote_copy(src_ref=src, dst_ref=dst, send_sem=ssem, recv_sem=rsem,
                                        device_id=dev, device_id_type=MESH)


HBM = pl.BlockSpec(memory_space=pltpu.HBM)
SEM = pl.BlockSpec(memory_space=pltpu.SEMAPHORE)
DATAFLOW = pltpu.SideEffectType.DATAFLOW_SIDE_EFFECTING


def _chip_peers():
    x, y, c = _mesh_pos()
    peers = [(1 - x, y), (x, 1 - y), (1 - x, 1 - y)]
    return x, y, c, 2 * x + y, [(px, py, 2 * px + py) for px, py in peers]


def split_start(name, bufs, n, plan):
    nb = len(bufs)

    def body(*refs):
        sends, _ = plan(refs[:nb], refs[nb], refs[nb + 1])
        for cp in sends:
            cp.start()
        refs[-1][...] = jnp.zeros_like(refs[-1])

    out = pl.pallas_call(
        body, name=name,
        out_shape=(pltpu.SemaphoreType.DMA((n,)), pltpu.SemaphoreType.DMA((n,)),
                   *[pltpu.HBM(b.shape, b.dtype) for b in bufs], SDS((8, LANE), F32)),
        in_specs=[HBM] * nb, out_specs=(SEM, SEM, *([HBM] * nb), pl.BlockSpec(memory_space=pltpu.VMEM)),
        input_output_aliases={i: 2 + i for i in range(nb)},
        compiler_params=pltpu.CompilerParams(has_side_effects=DATAFLOW),
    )(*[pltpu.with_memory_space_constraint(b, pltpu.HBM) for b in bufs])
    return out[0], out[1], list(out[2:2 + nb]), out[-1]


def split_wait(name, ssem, rsem, bufs, after, plan):
    nb = len(bufs)

    def body(*refs):
        sends, recvs = plan(refs[:nb], refs[nb], refs[nb + 1])
        for cp in recvs:
            cp.wait_recv()
        for cp in sends:
            cp.wait_send()

    out = pl.pallas_call(
        body, name=name, out_shape=tuple(pltpu.HBM(b.shape, b.dtype) for b in bufs),
        in_specs=[HBM] * nb + [SEM, SEM] + [ANY] * len(after), out_specs=tuple([HBM] * nb),
        input_output_aliases={i: i for i in range(nb)},
        compiler_params=pltpu.CompilerParams(has_side_effects=DATAFLOW),
    )(*bufs, ssem, rsem, *after)
    return list(out)


def gather_plan(refs, ssem, rsem, rows=None):
    land_ref, = refs
    _, _, c, k, peers = _chip_peers()
    part = (lambda a: a) if rows is None else (lambda a: a.at[pl.ds(rows[0], rows[1])])
    sends = [_rcopy(part(land_ref.at[k, c]), part(land_ref.at[k, c]), ssem.at[j], rsem.at[j], (px, py, c))
             for j, (px, py, _) in enumerate(peers)]
    recvs = [_rcopy(part(land_ref.at[kp, c]), part(land_ref.at[kp, c]), ssem.at[j], rsem.at[j], (px, py, c))
             for j, (px, py, kp) in enumerate(peers)]
    return sends, recvs


def reduce_plan(refs, ssem, rsem):
    cs_ref, ss_ref, p3_ref, sp3_ref = refs
    _, _, c, k, peers = _chip_peers()
    sends, recvs = [], []
    for j, (px, py, kp) in enumerate(peers):
        sends.append(_rcopy(cs_ref.at[kp], p3_ref.at[j], ssem.at[j], rsem.at[j], (px, py, c)))
        recvs.append(_rcopy(cs_ref.at[kp], p3_ref.at[j], ssem.at[j], rsem.at[j], (px, py, c)))
        sends.append(_rcopy(ss_ref, sp3_ref.at[k], ssem.at[3 + j], rsem.at[3 + j], (px, py, c)))
        recvs.append(_rcopy(ss_ref, sp3_ref.at[kp], ssem.at[3 + j], rsem.at[3 + j], (px, py, c)))
    return sends, recvs


def gather_small(sshard):
    def body(s_ref, sf_ref, lsem, ssem, rsem):
        _, _, c, k, peers = _chip_peers()
        own = pltpu.make_async_copy(s_ref, sf_ref.at[k], lsem)
        own.start()
        sends = [_rcopy(s_ref, sf_ref.at[k], ssem.at[j], rsem.at[j], (px, py, c)) for j, (px, py, _) in enumerate(peers)]
        recvs = [_rcopy(s_ref, sf_ref.at[kp], ssem.at[j], rsem.at[j], (px, py, c)) for j, (px, py, kp) in enumerate(peers)]
        for cp in sends:
            cp.start()
        for cp in recvs:
            cp.wait_recv()
        for cp in sends:
            cp.wait_send()
        own.wait()

    return pl.pallas_call(
        body, name="gather_small", out_shape=SDS((N_CHIPS,) + sshard.shape, sshard.dtype),
        in_specs=[ANY], out_specs=ANY,
        scratch_shapes=[pltpu.SemaphoreType.DMA, pltpu.SemaphoreType.DMA((3,)), pltpu.SemaphoreType.DMA((3,))],
    )(sshard)


def exchange_now(name, bufs, n, plan):
    nb = len(bufs)

    def body(*refs):
        sends, recvs = plan(refs[nb:2 * nb], refs[2 * nb], refs[2 * nb + 1])
        for cp in sends:
            cp.start()
        for cp in recvs:
            cp.wait_recv()
        for cp in sends:
            cp.wait_send()

    return list(pl.pallas_call(
        body, name=name, out_shape=tuple(SDS(b.shape, b.dtype) for b in bufs),
        in_specs=[ANY] * nb, out_specs=tuple([ANY] * nb), input_output_aliases={i: i for i in range(nb)},
        scratch_shapes=[pltpu.SemaphoreType.DMA((n,)), pltpu.SemaphoreType.DMA((n,))],
    )(*bufs))


def handover_plan(refs, ssem, rsem, rows=None):
    land_ref, = refs
    x, y, c, _, peers = _chip_peers()
    sib = (x, y, 1 - c)
    part = (lambda a: a) if rows is None else (lambda a: a.at[pl.ds(rows[0], rows[1])])
    sends = [_rcopy(part(land_ref.at[kp, c]), part(land_ref.at[kp, c]), ssem.at[j], rsem.at[j], sib)
             for j, (_, _, kp) in enumerate(peers)]
    recvs = [_rcopy(part(land_ref.at[kp, 1 - c]), part(land_ref.at[kp, 1 - c]), ssem.at[j], rsem.at[j], sib)
             for j, (_, _, kp) in enumerate(peers)]
    return sends, recvs


def pair_plan(refs, ssem, rsem):
    gb_ref, sb_ref, p_ref, sp_ref = refs
    x, y, c = _mesh_pos()
    sib = (x, y, 1 - c)
    n = gb_ref.shape[0]
    copies = [_rcopy(gb_ref.at[kk, 1 - c], p_ref.at[kk], ssem.at[kk], rsem.at[kk], sib) for kk in range(n)]
    copies.append(_rcopy(sb_ref, sp_ref, ssem.at[n], rsem.at[n], sib))
    return copies, copies


def final_plan(refs, ssem, rsem):
    gf_ref, = refs
    x, y, c = _mesh_pos()
    sib = (x, y, 1 - c)
    return ([_rcopy(gf_ref.at[c], gf_ref.at[c], ssem.at[0], rsem.at[0], sib)],
            [_rcopy(gf_ref.at[1 - c], gf_ref.at[1 - c], ssem.at[0], rsem.at[0], sib)])


def pair_sum(pos, gb, p1):
    n, _, rh, d = gb.shape
    cr = _row_chunk(rh, 1280)

    def body(pos_ref, a_ref, b_ref, o_ref):
        o_ref[...] = a_ref[...] + b_ref[...]

    return pl.pallas_call(
        body, name="pair_sum", out_shape=SDS((n, rh, d), gb.dtype),
        grid_spec=pltpu.PrefetchScalarGridSpec(
            num_scalar_prefetch=1, grid=(n, rh // cr),
            in_specs=[pl.BlockSpec((None, None, cr, d), lambda kk, r, pos: (kk, pos[1], r, 0)),
                      pl.BlockSpec((None, cr, d), lambda kk, r, pos: (kk, r, 0))],
            out_specs=pl.BlockSpec((None, cr, d), lambda kk, r, pos: (kk, r, 0))),
        compiler_params=_cparams(),
    )(pos, gb, p1)


def chip_sum(pos, cs, p3):
    n, rh, d = cs.shape
    cr = _row_chunk(rh, 640)

    def body(pos_ref, a_ref, b_ref, o_ref):
        acc = a_ref[...].astype(F32)
        for j in range(3):
            acc = acc + b_ref[j].astype(F32)
        o_ref[...] = acc

    return pl.pallas_call(
        body, name="chip_sum", out_shape=SDS((2, rh, d), F32),
        grid_spec=pltpu.PrefetchScalarGridSpec(
            num_scalar_prefetch=1, grid=(rh // cr,),
            in_specs=[pl.BlockSpec((None, cr, d), lambda r, pos: (pos[0], r, 0)),
                      pl.BlockSpec((3, cr, d), lambda r, pos: (0, r, 0))],
            out_specs=pl.BlockSpec((None, cr, d), lambda r, pos: (pos[1], r, 0))),
        compiler_params=_cparams(),
    )(pos, cs, p3)


def small_pair_sum(a, b):
    def body(a_ref, b_ref, o_ref):
        o_ref[...] = a_ref[...] + b_ref[...]

    return pl.pallas_call(body, name="small_pair_sum", out_shape=SDS(a.shape, a.dtype),
                          compiler_params=_cparams())(a, b)


def small_chip_sum(pos, own, p):
    ns, w = own.shape

    def body(pos_ref, own_ref, p0, p1, p2, p3, o_ref):
        k = pos_ref[0]
        acc = None
        for chip, ref in enumerate((p0, p1, p2, p3)):
            term = jnp.where(k == chip, own_ref[...], ref[...])
            acc = term if acc is None else acc + term
        o_ref[...] = acc

    def slot(chip):
        return pl.BlockSpec((None, ns, w), lambda i, pos: (jnp.where(pos[0] == chip, (chip + 1) % N_CHIPS, chip), 0, 0))

    return pl.pallas_call(
        body, name="small_chip_sum", out_shape=SDS(own.shape, own.dtype),
        grid_spec=pltpu.PrefetchScalarGridSpec(
            num_scalar_prefetch=1, grid=(1,),
            in_specs=[pl.BlockSpec((ns, w), lambda i, pos: (0, 0))] + [slot(chip) for chip in range(N_CHIPS)],
            out_specs=pl.BlockSpec((ns, w), lambda i, pos: (0, 0))),
        compiler_params=_cparams(),
    )(pos, own, p, p, p, p)


def _rms(x, g):
    rs = lax.rsqrt(jnp.mean(x * x, axis=-1, keepdims=True) + EPS)
    xh = x * rs
    return xh, rs, xh * g


def _rms_bwd(dy, xh, rs, g):
    dxh = dy * g
    dx = rs * (dxh - xh * jnp.mean(dxh * xh, axis=-1, keepdims=True))
    return dx, dy * xh


def _gelu(x):
    t = jnp.tanh(GELU_C * (x + 0.044715 * x * x * x))
    return 0.5 * x * (1.0 + t), t


def _gelu_grad(x, t):
    return 0.5 * (1.0 + t) + 0.5 * x * (1.0 - t * t) * GELU_C * (1.0 + 3.0 * 0.044715 * x * x)


def _shift_rows(v, s, n):
    if s == 0:
        return v
    t = lax.broadcasted_iota(jnp.int32, v.shape, 0)
    rolled = pltpu.roll(v, (-s) % n, 0)
    inside = (t < n - s) if s > 0 else (t >= -s)
    return jnp.where(inside, rolled, 0.0)


def _scan_rows(a_ref, u_ref, h_ref, acum_ref, reverse):
    s_len, w = a_ref.shape
    chunk = min(512, s_len)
    last = 0 if reverse else SUBLANES - 1

    def inside_vregs(ci, _):
        rows = pl.ds(pl.multiple_of(ci * chunk, chunk), chunk)
        a = a_ref[rows, :].reshape(chunk // SUBLANES, SUBLANES, w)
        u = u_ref[rows, :].reshape(chunk // SUBLANES, SUBLANES, w)
        pos = lax.broadcasted_iota(jnp.int32, (1, SUBLANES, w), 1)
        for dist in (1, 2, 4):
            ok = (pos < SUBLANES - dist) if reverse else (pos >= dist)
            shift = SUBLANES - dist if reverse else dist
            u = u + a * jnp.where(ok, pltpu.roll(u, shift, 1), 0.0)
            a = a * jnp.where(ok, pltpu.roll(a, shift, 1), 1.0)
        h_ref[rows, :] = u.reshape(chunk, w)
        acum_ref[rows, :] = a.reshape(chunk, w)
        return 0

    lax.fori_loop(0, s_len // chunk, inside_vregs, 0)

    chains = max(1, min(SCAN_CHAINS, s_len // (8 * SUBLANES)))
    seg = s_len // chains
    nvreg = seg // SUBLANES

    def step(j, carry):
        jj = (nvreg - 1 - j) if reverse else j
        out = []
        for c, (hin, ain) in enumerate(carry):
            rows = pl.ds(pl.multiple_of(c * seg + jj * SUBLANES, SUBLANES), SUBLANES)
            acc = acum_ref[rows, :]
            h = h_ref[rows, :] + acc * hin
            acc = acc * ain
            h_ref[rows, :] = h
            acum_ref[rows, :] = acc
            out.append((jnp.broadcast_to(h[last:last + 1, :], h.shape), jnp.broadcast_to(acc[last:last + 1, :], acc.shape)))
        return tuple(out)

    init = tuple((jnp.zeros((SUBLANES, w), F32), jnp.ones((SUBLANES, w), F32)) for _ in range(chains))
    ends = lax.fori_loop(0, nvreg, step, init, unroll=min(2, nvreg))
    order = range(chains - 2, -1, -1) if reverse else range(1, chains)
    inflow = jnp.zeros((1, w), F32)
    for s in order:
        h, acc = ends[s + 1 if reverse else s - 1]
        inflow = h[0:1, :] + acc[0:1, :] * inflow
        rows = pl.ds(s * seg, seg)
        h_ref[rows, :] = h_ref[rows, :] + acum_ref[rows, :] * inflow


def _width_parts(f):
    cut = (f // 2) // MXU_WIDTH * MXU_WIDTH
    return [slice(0, cut), slice(cut, f)] if cut and f % MXU_WIDTH == 0 else [slice(0, f // 2), slice(f // 2, f)]


def _w_spec(rows_half, d, blk):
    return pl.BlockSpec((N_CHIPS, 2, rows_half, d), lambda *_: (0, 0, blk, 0), pipeline_mode=pl.Buffered(1))


def ffn_forward(x, gain, wfull, lay, which, deps=(), tm=512):
    s_len, d = x.shape
    tm = min(tm, s_len)
    f = 8 * lay.fh

    def body(x_ref, g_ref, wg_ref, wu_ref, wd_ref, *rest):
        o_ref, gate_ref, up_ref = rest[len(deps):]
        x = x_ref[...]
        _, _, hn = _rms(x, g_ref[...])
        h = hn.astype(BF)
        y = jnp.zeros((tm, d), F32)
        for cols in _width_parts(f):
            gate = dot_nt(h, wg_ref[...].reshape(f, d)[cols])
            up = dot_nt(h, wu_ref[...].reshape(f, d)[cols])
            act = (gate * jax.nn.sigmoid(gate) * up).astype(BF)
            y = y + dot_nn(act, wd_ref[...].reshape(f, d)[cols])
            gate_ref[:, cols] = gate.astype(BF)
            up_ref[:, cols] = up.astype(BF)
        o_ref[...] = x + FFN_RES * y

    row = pl.BlockSpec((tm, d), lambda i: (i, 0))
    wide = pl.BlockSpec((tm, f), lambda i: (i, 0))
    return pl.pallas_call(
        body, name="ffn_forward", grid=(s_len // tm,),
        out_shape=(SDS((s_len, d), F32), SDS((s_len, f), BF), SDS((s_len, f), BF)),
        in_specs=[row, pl.BlockSpec((1, d), lambda i: (0, 0))]
        + [_w_spec(lay.fh, d, 3 * which + m) for m in range(3)] + [ANY] * len(deps),
        out_specs=(row, wide, wide), compiler_params=_cparams(),
    )(x, gain, wfull, wfull, wfull, *deps)


def ffn_backward_dx(x, gain, dout, gate_bf, up_bf, wfull, lay, which, deps=(), tm=256):
    s_len, d = x.shape
    tm = min(tm, s_len)
    f = 8 * lay.fh
    nt = s_len // tm

    def body(x_ref, g_ref, do_ref, gate_ref, up_ref, wg_ref, wu_ref, wd_ref, *rest):
        dx_ref, dg_ref, lhs_ref, rhs_ref = rest[len(deps):]
        dgate_ref, dup_ref, act_ref = lhs_ref.at[0], lhs_ref.at[1], lhs_ref.at[2]
        h_ref, df_ref = rhs_ref.at[0], rhs_ref.at[1]
        x = x_ref[...]
        g = g_ref[...]
        xh, rs, hn = _rms(x, g)
        h = hn.astype(BF)
        do = do_ref[...]
        df = (FFN_RES * do).astype(BF)
        dh = jnp.zeros((tm, d), F32)
        wd = wd_ref[...].reshape(f, d)
        parts = _width_parts(f)
        dacts = [dot_nt(df, wd[cols]) for cols in parts]
        for part, cols in enumerate(parts):
            wg = wg_ref[...].reshape(f, d)[cols]
            wu = wu_ref[...].reshape(f, d)[cols]
            gate = gate_ref[:, cols].astype(F32)
            up = up_ref[:, cols].astype(F32)
            sg = jax.nn.sigmoid(gate)
            silu = gate * sg
            dact = dacts[part]
            dup = (dact * silu).astype(BF)
            dgate = (dact * up * (sg * (1.0 + gate * (1.0 - sg)))).astype(BF)
            dh = dh + dot_nn(dgate, wg) + dot_nn(dup, wu)
            dgate_ref[:, cols] = dgate
            dup_ref[:, cols] = dup
            act_ref[:, cols] = (silu * up).astype(BF)
        dxn, dgrow = _rms_bwd(dh, xh, rs, g)
        dx_ref[...] = do + dxn

        @pl.when(pl.program_id(0) == 0)
        def _():
            dg_ref[...] = jnp.zeros_like(dg_ref)

        dg_ref[...] += jnp.sum(dgrow, axis=0, keepdims=True)
        h_ref[...] = h
        df_ref[...] = df

    row = pl.BlockSpec((tm, d), lambda i: (i, 0))
    wide = pl.BlockSpec((tm, f), lambda i: (i, 0))
    vec = pl.BlockSpec((1, d), lambda i: (0, 0))
    return pl.pallas_call(
        body, name="ffn_backward_dx", grid=(nt,),
        out_shape=(SDS((s_len, d), F32), SDS((1, d), F32), SDS((3, s_len, f), BF), SDS((2, s_len, d), BF)),
        in_specs=[row, vec, row, wide, wide] + [_w_spec(lay.fh, d, 3 * which + m) for m in range(3)] + [ANY] * len(deps),
        out_specs=(row, vec, pl.BlockSpec((3, tm, f), lambda i: (0, i, 0)), pl.BlockSpec((2, tm, d), lambda i: (0, i, 0))),
        compiler_params=_cparams(),
    )(x, gain, dout, gate_bf, up_bf, wfull, wfull, wfull, *deps)


def weight_grad_tn(lhs, rhs, gb, lay, blk0, tk=4096):
    nmat, s_len, f = lhs.shape
    tk = min(tk, s_len)
    d = rhs.shape[2]
    fc = f // 2
    nk = s_len // tk

    def body(a_ref, b_ref, gb_ref, o_ref, acc):
        kt = pl.program_id(2)

        @pl.when(kt == 0)
        def _():
            acc[...] = jnp.zeros_like(acc)

        acc[...] += dot_tn(a_ref[...], b_ref[...])

        @pl.when(kt == nk - 1)
        def _():
            for p in range(2):
                for q in range(2):
                    o_ref[p, q] = acc[pl.ds((2 * p + q) * lay.fh, lay.fh), :].astype(o_ref.dtype)

    return pl.pallas_call(
        body, name="weight_grad_tn", grid=(nmat, 2, nk), out_shape=SDS(gb.shape, gb.dtype),
        in_specs=[pl.BlockSpec((None, tk, fc), lambda m, j, kt: (m, kt, j)),
                  pl.BlockSpec((None, tk, d), lambda m, j, kt: (jnp.where(m == nmat - 1, 1, 0), kt, 0)), ANY],
        out_specs=pl.BlockSpec((2, 2, lay.fh, d), lambda m, j, kt: (j, 0, blk0 + m, 0)),
        scratch_shapes=[pltpu.VMEM((fc, d), F32)],
        input_output_aliases={2: 0}, compiler_params=_cparams(),
    )(lhs, rhs, gb)


def _lane_blocks(v):
    return [v[:, j * LANE:(j + 1) * LANE] for j in range(v.shape[1] // LANE)]


def _join_lane_blocks(ref):
    return jnp.concatenate([ref[j] for j in range(ref.shape[0])], axis=1)


def _cbm_spec(nblk, rows, first=0):
    return pl.BlockSpec((nblk, rows, LANE), lambda i: (first // nblk, i, 0))


def mix_project(x, gain, wfull, lay, lw, att, tm=512):
    s_len, d = x.shape
    tm = min(tm, s_len)
    d_in = 8 * lay.ih
    kvw = (d_in - 2 * lw - att) // 2
    ncol = (2 * lw + 2 * kvw) // LANE

    def body(x_ref, g_ref, w_ref, o_ref, qt_ref, vt_ref):
        _, _, hn = _rms(x_ref[...], g_ref[...])
        h = hn.astype(BF)
        w = w_ref[:, :, :lay.ih, :].reshape(d_in, d)
        pieces = _lane_blocks(dot_nt(h, w[:2 * lw])) + _lane_blocks(dot_nt(h, w[2 * lw + att:]))
        for j, piece in enumerate(pieces):
            o_ref[j] = piece
        qt_ref[...] = dot_nt(w[2 * lw:2 * lw + att], h)
        vt_ref[...] = dot_nt(w[2 * lw + att + kvw:], h)

    return pl.pallas_call(
        body, name="mix_project", grid=(s_len // tm,),
        out_shape=(SDS((ncol, s_len, LANE), F32), SDS((att, s_len), F32), SDS((kvw, s_len), F32)),
        in_specs=[pl.BlockSpec((tm, d), lambda i: (i, 0)), pl.BlockSpec((1, d), lambda i: (0, 0)),
                  _w_spec(lay.fh, d, lay.MIX_BLK)],
        out_specs=(_cbm_spec(ncol, tm), pl.BlockSpec((att, tm), lambda i: (0, i)), pl.BlockSpec((kvw, tm), lambda i: (0, i))),
        compiler_params=_cparams(),
    )(x, gain, wfull)


def mix_project_backward(x, gain, dout, dxr, dgt, dqt, dkv, dwout, wfull, gb, lay, tm=512):
    s_len, d = x.shape
    tm = min(tm, s_len)
    d_in = 8 * lay.ih
    nt = s_len // tm
    kvw = dkv.shape[1]
    att = dqt.shape[0]
    nlru = (dxr.shape[0] + dgt.shape[0]) * LANE

    def body(x_ref, g_ref, do_ref, dxr_ref, dgt_ref, dqt_ref, dkv_ref, dwo_ref, w_ref, gb_ref, dx_ref, dg_ref, o_ref, acc):
        i = pl.program_id(0)
        g = g_ref[...]
        xh, rs, hn = _rms(x_ref[...], g)
        h = hn.astype(BF)
        w = w_ref[:, :, :lay.ih, :].reshape(d_in, d)
        dlru = jnp.concatenate([_join_lane_blocks(dxr_ref), _join_lane_blocks(dgt_ref)], axis=1).astype(BF)
        dqt = dqt_ref[...].astype(BF)
        dkv = dkv_ref[...].astype(BF)
        dh = dot_nn(dlru, w[:nlru]) + dot_tn(dqt, w[nlru:nlru + att]) + dot_nn(dkv, w[nlru + att:])
        dxn, dgrow = _rms_bwd(dh, xh, rs, g)
        dx_ref[...] = do_ref[...] + dxn

        @pl.when(i == 0)
        def _():
            dg_ref[...] = jnp.zeros_like(dg_ref)
            acc[...] = jnp.zeros_like(acc)

        dg_ref[...] += jnp.sum(dgrow, axis=0, keepdims=True)
        acc[0:nlru, :] += dot_tn(dlru, h)
        acc[nlru:nlru + att, :] += dot_nn(dqt, h)
        acc[nlru + att:, :] += dot_tn(dkv, h)

        @pl.when(i == nt - 1)
        def _():
            for p in range(N_CHIPS):
                for q in range(2):
                    o_ref[p, q, :lay.ih, :] = acc[pl.ds((2 * p + q) * lay.ih, lay.ih), :].astype(o_ref.dtype)
            o_ref[:, :, lay.ih:, :] = dwo_ref[...]

    row = pl.BlockSpec((tm, d), lambda i: (i, 0))
    vec = pl.BlockSpec((1, d), lambda i: (0, 0))
    return pl.pallas_call(
        body, name="mix_project_backward", grid=(nt,),
        out_shape=(SDS((s_len, d), F32), SDS((1, d), F32), SDS(gb.shape, gb.dtype)),
        in_specs=[row, vec, row, _cbm_spec(dxr.shape[0], tm), _cbm_spec(dgt.shape[0], tm),
                  pl.BlockSpec((att, tm), lambda i: (0, i)), pl.BlockSpec((tm, kvw), lambda i: (i, 0)),
                  pl.BlockSpec(dwout.shape, lambda i: (0, 0, 0, 0)), _w_spec(lay.fh, d, lay.MIX_BLK), ANY],
        out_specs=(row, vec, pl.BlockSpec((N_CHIPS, 2, lay.fh, d), lambda i: (0, 0, lay.MIX_BLK, 0))),
        scratch_shapes=[pltpu.VMEM((d_in, d), F32)],
        input_output_aliases={9: 2}, compiler_params=_cparams(),
    )(x, gain, dout, dxr, dgt, dqt, dkv, dwout, wfull, gb)


def _neg_expm1(x):
    series = -x * (1.0 + x * (0.5 + x * (1.0 / 6.0 + x * (1.0 / 24.0))))
    return jnp.where(x > -0.03, series, 1.0 - jnp.exp(x))


def _decay(pv_ref, direction, r):
    lam = pv_ref[5 + direction:6 + direction, :]
    sp = jnp.maximum(-lam, 0.0) + jnp.log1p(jnp.exp(-jnp.abs(lam)))
    log_a = -LRU_C * sp * r
    return jnp.exp(log_a), jnp.sqrt(_neg_expm1(2.0 * log_a)), sp


def _lru_gates(xc, wb_ref, pv_ref, direction):
    xcb = xc.astype(BF)
    sigmoid = lambda z: 0.5 * jnp.tanh(0.5 * z) + 0.5
    r = sigmoid(dot_nn(xcb, wb_ref[2 * direction]) + pv_ref[1 + direction:2 + direction, :])
    i = sigmoid(dot_nn(xcb, wb_ref[2 * direction + 1]) + pv_ref[3 + direction:4 + direction, :])
    a, mult, sp = _decay(pv_ref, direction, r)
    return xcb, r, i, a, mult, sp


def _conv_rows(xr, cv_ref, bias, n):
    acc = bias + cv_ref[0:1, :] * _shift_rows(xr, -2, n)
    for j in range(1, CONV_WIDTH):
        acc = acc + cv_ref[j:j + 1, :] * _shift_rows(xr, j - 2, n)
    return acc


def lru_forward(proj, cvec, pvec, wblk, lw, deps=(), ch=512):
    s_len = proj.shape[1]
    ncb = lw // LANE
    ch = min(ch, s_len)
    nchunk = s_len // ch

    def body(xr_ref, gt_ref, cv_ref, pv_ref, wb_ref, *rest):
        y_ref, hs_ref, xc_s, a_s, u_s, acum_s = rest[len(deps):]
        xc_s[...] = _conv_rows(xr_ref[...], cv_ref, pv_ref[0:1, :], s_len)
        for direction in range(2):
            def fill(ci, _):
                rows = pl.ds(pl.multiple_of(ci * ch, ch), ch)
                xc = xc_s[rows, :]
                _, _, i, a, mult, _ = _lru_gates(xc, wb_ref, pv_ref, direction)
                a_s[rows, :] = a
                u_s[rows, :] = mult * (i * xc)
                return 0

            lax.fori_loop(0, nchunk, fill, 0)
            _scan_rows(a_s, u_s, hs_ref.at[direction], acum_s, reverse=direction == 1)

        def out(ci, _):
            rows = pl.ds(pl.multiple_of(ci * ch, ch), ch)
            gl, _ = _gelu(gt_ref[rows, :])
            y_ref[rows, :] = gl * (hs_ref[0, rows, :] + hs_ref[1, rows, :])
            return 0

        lax.fori_loop(0, nchunk, out, 0)

    col = lambda off: pl.BlockSpec((None, s_len, LANE), lambda cb: (off + cb, 0, 0))
    return pl.pallas_call(
        body, name="lru_forward", grid=(ncb,),
        out_shape=(SDS((ncb, s_len, LANE), F32), SDS((2, ncb, s_len, LANE), F32)),
        in_specs=[col(0), col(ncb), pl.BlockSpec((8, LANE), lambda cb: (0, cb)), pl.BlockSpec((8, LANE), lambda cb: (0, cb)),
                  pl.BlockSpec((4, None, LANE, LANE), lambda cb: (0, cb, 0, 0))] + [ANY] * len(deps),
        out_specs=(col(0), pl.BlockSpec((2, None, s_len, LANE), lambda cb: (0, cb, 0, 0))),
        scratch_shapes=[pltpu.VMEM((s_len, LANE), F32)] * 4, compiler_params=_cparams(),
    )(proj, proj, cvec, pvec, wblk, *deps)


def lru_backward(proj, hs, dy, cvec, pvec, wblk, lw, ch=512):
    s_len = proj.shape[1]
    ncb = lw // LANE
    ch = min(ch, s_len)
    nchunk = s_len // ch

    def body(xr_ref, gt_ref, hs_ref, dy_ref, cv_ref, pv_ref, wb_ref, dxr_ref, dgt_ref, dcv_ref, dpv_ref, dwb_ref,
             xc_s, a_s, r_s, i_s, dh_s, lam_s, hp_s, dxc_s, acum_s):
        xr = xr_ref[...]
        xc_s[...] = _conv_rows(xr, cv_ref, pv_ref[0:1, :], s_len)
        dxc_s[...] = jnp.zeros_like(dxc_s)
        dpv_ref[...] = jnp.zeros_like(dpv_ref)
        dwb_ref[...] = jnp.zeros_like(dwb_ref)

        def head(ci, _):
            rows = pl.ds(pl.multiple_of(ci * ch, ch), ch)
            gt = gt_ref[rows, :]
            gl, t = _gelu(gt)
            dy = dy_ref[rows, :]
            dh_s[rows, :] = dy * gl
            dgt_ref[rows, :] = dy * (hs_ref[0, rows, :] + hs_ref[1, rows, :]) * _gelu_grad(gt, t)
            return 0

        lax.fori_loop(0, nchunk, head, 0)

        for direction in range(2):
            def fill(ci, _):
                rows = pl.ds(pl.multiple_of(ci * ch, ch), ch)
                _, r, i, a, _, _ = _lru_gates(xc_s[rows, :], wb_ref, pv_ref, direction)
                a_s[rows, :] = a
                r_s[rows, :] = r
                i_s[rows, :] = i
                return 0

            lax.fori_loop(0, nchunk, fill, 0)
            toward = 1 if direction == 0 else -1
            hp_s[...] = _shift_rows(a_s[...], toward, s_len)
            _scan_rows(hp_s, dh_s, lam_s, acum_s, reverse=direction == 0)
            hp_s[...] = _shift_rows(hs_ref[direction], -toward, s_len)

            def grads(ci, _):
                rows = pl.ds(pl.multiple_of(ci * ch, ch), ch)
                xc = xc_s[rows, :]
                xcb = xc.astype(BF)
                r, i = r_s[rows, :], i_s[rows, :]
                a, mult, sp = _decay(pv_ref, direction, r)
                du = lam_s[rows, :]
                da = du * hp_s[rows, :]
                dmult = du * i * xc
                di = du * mult * xc
                dlog_a = (da - dmult * a / mult) * a
                dr = dlog_a * (-LRU_C * sp)
                dza = dr * r * (1.0 - r)
                dzx = di * i * (1.0 - i)
                dzab = dza.astype(BF)
                dzxb = dzx.astype(BF)
                dxc_s[rows, :] += (du * mult * i + dot_nt(dzab, wb_ref[2 * direction])
                                   + dot_nt(dzxb, wb_ref[2 * direction + 1]))
                dwb_ref[2 * direction] += dot_tn(xcb, dzab)
                dwb_ref[2 * direction + 1] += dot_tn(xcb, dzxb)
                dpv_ref[1 + direction:2 + direction, :] += jnp.sum(dza, axis=0, keepdims=True)
                dpv_ref[3 + direction:4 + direction, :] += jnp.sum(dzx, axis=0, keepdims=True)
                dpv_ref[5 + direction:6 + direction, :] += jnp.sum(dlog_a * (-LRU_C * r), axis=0, keepdims=True)
                return 0

            lax.fori_loop(0, nchunk, grads, 0)

        for direction in range(2):
            lam = pv_ref[5 + direction:6 + direction, :]
            dpv_ref[5 + direction:6 + direction, :] = dpv_ref[5 + direction:6 + direction, :] * (-jax.nn.sigmoid(-lam))
        dxc = dxc_s[...]
        dpv_ref[0:1, :] = jnp.sum(dxc, axis=0, keepdims=True)
        dxr = cv_ref[0:1, :] * _shift_rows(dxc, 2, s_len)
        for j in range(1, CONV_WIDTH):
            dxr = dxr + cv_ref[j:j + 1, :] * _shift_rows(dxc, 2 - j, s_len)
        dxr_ref[...] = dxr
        dcv_ref[...] = jnp.zeros_like(dcv_ref)
        for j in range(CONV_WIDTH):
            dcv_ref[j:j + 1, :] = jnp.sum(dxc * _shift_rows(xr, j - 2, s_len), axis=0, keepdims=True)

    col = lambda off: pl.BlockSpec((None, s_len, LANE), lambda cb: (off + cb, 0, 0))
    own = col(0)
    small = pl.BlockSpec((8, LANE), lambda cb: (0, cb))
    wspec = pl.BlockSpec((4, None, LANE, LANE), lambda cb: (0, cb, 0, 0))
    return pl.pallas_call(
        body, name="lru_backward", grid=(ncb,),
        out_shape=(SDS((ncb, s_len, LANE), F32), SDS((ncb, s_len, LANE), F32), SDS((8, lw), F32), SDS((8, lw), F32),
                   SDS(wblk.shape, F32)),
        in_specs=[col(0), col(ncb), pl.BlockSpec((2, None, s_len, LANE), lambda cb: (0, cb, 0, 0)), own, small, small, wspec],
        out_specs=(own, own, small, small, wspec),
        scratch_shapes=[pltpu.VMEM((s_len, LANE), F32)] * 9, compiler_params=_cparams(),
    )(proj, proj, hs, dy, cvec, pvec, wblk)


def _window_specs(s_len, first, width=None):
    nb = s_len // BLOCK
    where = (lambda n: jnp.maximum(n - 1, 0), lambda n: n, lambda n: jnp.minimum(n + 1, nb - 1))
    if width is None:
        return [pl.BlockSpec((None, BLOCK, LANE), lambda n, f=f: (first, f(n), 0)) for f in where]
    return [pl.BlockSpec((width, BLOCK), lambda n, f=f: (0, f(n))) for f in where]


def _stack_heads(v, kh):
    return jnp.concatenate([v[(kh * KV_GROUP + g) * HEAD_DIM:(kh * KV_GROUP + g + 1) * HEAD_DIM, :]
                            for g in range(KV_GROUP)], axis=1)


def _unstack_heads(ref, kh, v):
    for g in range(KV_GROUP):
        h = kh * KV_GROUP + g
        ref[h * HEAD_DIM:(h + 1) * HEAD_DIM, :] = v[:, g * BLOCK:(g + 1) * BLOCK]


def _key_exists(n, nb):
    j = lax.broadcasted_iota(jnp.int32, (3 * BLOCK, 1), 0)
    return ((n > 0) | (j >= BLOCK)) & ((n < nb - 1) | (j < 2 * BLOCK))


def _attn_probs(qs, kcat, bias_g, sink_g, key_ok):
    logits = jnp.where(key_ok, dot_nn(kcat, qs) + bias_g, NEG_INF)
    m = jnp.maximum(jnp.max(logits, axis=0, keepdims=True), sink_g)
    p = jnp.exp(logits - m)
    es = jnp.exp(sink_g - m)
    inv = 1.0 / (jnp.sum(p, axis=0, keepdims=True) + es)
    return p * inv, es * inv


def attention_forward(qt, proj, vt, bias, sink, kblk):
    att, s_len = qt.shape
    kvw = vt.shape[0]
    nb = s_len // BLOCK

    def body(q_ref, kp_ref, kc_ref, kn_ref, vp_ref, vc_ref, vn_ref, b_ref, s_ref, o_ref):
        n = pl.program_id(0)
        q = q_ref[...]
        key_ok = _key_exists(n, nb)
        kall = jnp.concatenate([kp_ref[...], kc_ref[...], kn_ref[...]], axis=0).astype(BF)
        vall = jnp.concatenate([vp_ref[...], vc_ref[...], vn_ref[...]], axis=1).astype(BF)
        for kh in range(N_KV_HEADS):
            qs = (_stack_heads(q, kh) * (HEAD_DIM ** -0.5)).astype(BF)
            p, _ = _attn_probs(qs, kall[:, kh * HEAD_DIM:(kh + 1) * HEAD_DIM], b_ref[kh], s_ref[kh, 0:1, :], key_ok)
            _unstack_heads(o_ref, kh, dot_nn(vall[kh * HEAD_DIM:(kh + 1) * HEAD_DIM, :], p.astype(BF)))

    blk = pl.BlockSpec((att, BLOCK), lambda n: (0, n))
    return pl.pallas_call(
        body, name="attention_forward", grid=(nb,), out_shape=SDS((att, s_len), F32),
        in_specs=[blk] + _window_specs(s_len, kblk) + _window_specs(s_len, 0, kvw)
        + [pl.BlockSpec(bias.shape, lambda n: (0, 0, 0)), pl.BlockSpec(sink.shape, lambda n: (0, 0, 0))],
        out_specs=blk, compiler_params=_cparams(),
    )(qt, proj, proj, proj, vt, vt, vt, bias, sink)


def attention_backward(qt, proj, y_att, dy, bias, sink, kblk):
    att, s_len = qt.shape
    nb = s_len // BLOCK
    kvw = N_KV_HEADS * HEAD_DIM

    def body(q_ref, kp_ref, kc_ref, kn_ref, vp_ref, vc_ref, vn_ref, o_ref, do_ref, b_ref, s_ref,
             dq_ref, dkv_ref, db_ref, ds_ref):
        n = pl.program_id(0)

        @pl.when(n == 0)
        def _():
            dkv_ref[...] = jnp.zeros_like(dkv_ref)
            db_ref[...] = jnp.zeros_like(db_ref)
            ds_ref[...] = jnp.zeros_like(ds_ref)

        q = q_ref[...]
        o = o_ref[...]
        do = do_ref[...]
        kall = jnp.concatenate([kp_ref[...], kc_ref[...], kn_ref[...]], axis=0).astype(BF)
        vall = jnp.concatenate([vp_ref[...], vc_ref[...], vn_ref[...]], axis=0).astype(BF)
        key_ok = _key_exists(n, nb)
        dks, dvs = [], []
        for kh in range(N_KV_HEADS):
            kcat = kall[:, kh * HEAD_DIM:(kh + 1) * HEAD_DIM]
            vcat = vall[:, kh * HEAD_DIM:(kh + 1) * HEAD_DIM]
            qs = (_stack_heads(q, kh) * (HEAD_DIM ** -0.5)).astype(BF)
            p, ps = _attn_probs(qs, kcat, b_ref[kh], s_ref[kh, 0:1, :], key_ok)
            dos = _stack_heads(do, kh)
            dosb = dos.astype(BF)
            delta = jnp.sum(dos * _stack_heads(o, kh), axis=0, keepdims=True)
            dlog = p * (dot_nn(vcat, dosb) - delta)
            dlogb = dlog.astype(BF)
            db_ref[kh] += dlog
            ds_ref[kh] += jnp.broadcast_to(-ps * delta, ds_ref.shape[1:])
            _unstack_heads(dq_ref, kh, dot_tn(kcat, dlogb) * (HEAD_DIM ** -0.5))
            dks.append(dot_nt(dlogb, qs))
            dvs.append(dot_nt(p.astype(BF), dosb))
        dkv = jnp.concatenate(dks + dvs, axis=1)
        starts = [jnp.maximum(n - 1, 0), n, jnp.minimum(n + 1, nb - 1)]
        for b, st in enumerate(starts):
            rows = pl.ds(pl.multiple_of(st * BLOCK, BLOCK), BLOCK)
            dkv_ref[rows, :] += dkv[b * BLOCK:(b + 1) * BLOCK, :]

    blk = pl.BlockSpec((att, BLOCK), lambda n: (0, n))
    whole = lambda a: pl.BlockSpec(a.shape, lambda n: (0, 0, 0))
    return pl.pallas_call(
        body, name="attention_backward", grid=(nb,),
        out_shape=(SDS((att, s_len), F32), SDS((s_len, 2 * kvw), F32), SDS(bias.shape, F32), SDS(sink.shape, F32)),
        in_specs=[blk] + _window_specs(s_len, kblk) + _window_specs(s_len, kblk + 1) + [blk, blk, whole(bias), whole(sink)],
        out_specs=(blk, pl.BlockSpec((s_len, 2 * kvw), lambda n: (0, 0)), whole(bias), whole(sink)),
        compiler_params=_cparams(),
    )(qt, proj, proj, proj, proj, proj, proj, y_att, dy, bias, sink)


def _rms_cols(x, g):
    rs = lax.rsqrt(jnp.mean(x * x, axis=0, keepdims=True) + EPS)
    xh = x * rs
    return xh, rs, xh * g


def _rms_cols_bwd(dy, xh, rs, g):
    dxh = dy * g
    dx = rs * (dxh - xh * jnp.mean(dxh * xh, axis=0, keepdims=True))
    return dx, dy * xh


def mix_output(x, y_rec, y_att, g_rec, g_att, wfull, lay, tm=512):
    s_len, d = x.shape
    tm = min(tm, s_len)
    lw = y_rec.shape[0] * LANE
    att = y_att.shape[0]

    def body(x_ref, yr_ref, ya_ref, gr_ref, ga_ref, w_ref, o_ref):
        _, _, nr = _rms(_join_lane_blocks(yr_ref), gr_ref[...])
        _, _, na = _rms_cols(ya_ref[...], ga_ref[...])
        w = w_ref[:, :, lay.ih:, :].reshape(d, d)
        o_ref[...] = x_ref[...] + dot_nn(nr.astype(BF), w[:lw]) + dot_tn(na.astype(BF), w[lw:])

    row = pl.BlockSpec((tm, d), lambda i: (i, 0))
    return pl.pallas_call(
        body, name="mix_output", grid=(s_len // tm,), out_shape=SDS((s_len, d), F32),
        in_specs=[row, _cbm_spec(lw // LANE, tm), pl.BlockSpec((att, tm), lambda i: (0, i)),
                  pl.BlockSpec((1, lw), lambda i: (0, 0)), pl.BlockSpec((att, 1), lambda i: (0, 0)),
                  _w_spec(lay.fh, d, lay.MIX_BLK)],
        out_specs=row, compiler_params=_cparams(),
    )(x, y_rec, y_att, g_rec, g_att, wfull)


def mix_output_backward(dout, y_rec, y_att, g_rec, g_att, wfull, lay, deps=(), tm=1024):
    s_len, d = dout.shape
    tm = min(tm, s_len)
    lw = y_rec.shape[0] * LANE
    att = y_att.shape[0]
    nt = s_len // tm

    def body(do_ref, yr_ref, ya_ref, gr_ref, ga_ref, w_ref, *rest):
        dyr_ref, dya_ref, dgr_ref, dga_ref, o_ref, acc = rest[len(deps):]
        i = pl.program_id(0)
        gr = gr_ref[...]
        ga = ga_ref[...]
        xhr, rsr, nr = _rms(_join_lane_blocks(yr_ref), gr)
        xha, rsa, na = _rms_cols(ya_ref[...], ga)
        dob = do_ref[...].astype(BF)
        w = w_ref[:, :, lay.ih:, :].reshape(d, d)
        dyr, dgr_row = _rms_bwd(dot_nt(dob, w[:lw]), xhr, rsr, gr)
        dya, dga_col = _rms_cols_bwd(dot_nt(w[lw:], dob), xha, rsa, ga)
        for j, piece in enumerate(_lane_blocks(dyr)):
            dyr_ref[j] = piece
        dya_ref[...] = dya

        @pl.when(i == 0)
        def _():
            dgr_ref[...] = jnp.zeros_like(dgr_ref)
            dga_ref[...] = jnp.zeros_like(dga_ref)
            acc[...] = jnp.zeros_like(acc)

        dgr_ref[...] += jnp.sum(dgr_row, axis=0, keepdims=True)
        dga_ref[...] += jnp.sum(dga_col, axis=1, keepdims=True)
        acc[0:lw, :] += dot_tn(nr.astype(BF), dob)
        acc[lw:, :] += dot_nn(na.astype(BF), dob)

        @pl.when(i == nt - 1)
        def _():
            for p in range(N_CHIPS):
                for q in range(2):
                    o_ref[p, q] = acc[pl.ds((2 * p + q) * lay.oh, lay.oh), :].astype(o_ref.dtype)

    row = pl.BlockSpec((tm, d), lambda i: (i, 0))
    return pl.pallas_call(
        body, name="mix_output_backward", grid=(nt,),
        out_shape=(SDS(y_rec.shape, F32), SDS(y_att.shape, F32), SDS((1, lw), F32), SDS((att, 1), F32),
                   SDS((N_CHIPS, 2, lay.oh, d), BF)),
        in_specs=[row, _cbm_spec(lw // LANE, tm), pl.BlockSpec((att, tm), lambda i: (0, i)),
                  pl.BlockSpec((1, lw), lambda i: (0, 0)), pl.BlockSpec((att, 1), lambda i: (0, 0)),
                  _w_spec(lay.fh, d, lay.MIX_BLK)] + [ANY] * len(deps),
        out_specs=(_cbm_spec(lw // LANE, tm), pl.BlockSpec((att, tm), lambda i: (0, i)),
                   pl.BlockSpec((1, lw), lambda i: (0, 0)), pl.BlockSpec((att, 1), lambda i: (0, 0)),
                   pl.BlockSpec((N_CHIPS, 2, lay.oh, d), lambda i: (0, 0, 0, 0))),
        scratch_shapes=[pltpu.VMEM((d, d), F32)], compiler_params=_cparams(),
    )(dout, y_rec, y_att, g_rec, g_att, wfull, *deps)


def loss_head(x, gain, target, tm=512):
    s_len, d = x.shape
    tm = min(tm, s_len)

    def body(x_ref, g_ref, t_ref, dx_ref, dg_ref, loss_ref):
        g = g_ref[...]
        xh, rs, y = _rms(x_ref[...], g)
        err = y - t_ref[...]

        @pl.when(pl.program_id(0) == 0)
        def _():
            dg_ref[...] = jnp.zeros_like(dg_ref)
            loss_ref[...] = jnp.zeros_like(loss_ref)

        part = 0.5 * jnp.sum(jnp.mean(err * err, axis=-1, keepdims=True), axis=0, keepdims=True)
        loss_ref[...] += jnp.broadcast_to(part, loss_ref.shape)
        dx, dgrow = _rms_bwd(err * (1.0 / d), xh, rs, g)
        dx_ref[...] = dx
        dg_ref[...] += jnp.sum(dgrow, axis=0, keepdims=True)

    row = pl.BlockSpec((tm, d), lambda i: (i, 0))
    vec = pl.BlockSpec((1, d), lambda i: (0, 0))
    return pl.pallas_call(
        body, name="loss_head", grid=(s_len // tm,),
        out_shape=(SDS((s_len, d), F32), SDS((1, d), F32), SDS((8, LANE), F32)),
        in_specs=[row, vec, row], out_specs=(row, vec, pl.BlockSpec((8, LANE), lambda i: (0, 0))),
        compiler_params=_cparams(),
    )(x, gain, target)


def _adamw_update(w, g, m, v):
    m = ADAM_B1 * m + (1.0 - ADAM_B1) * g
    v = ADAM_B2 * v + (1.0 - ADAM_B2) * (g * g)
    m_hat = m / (1.0 - ADAM_B1 ** ADAM_STEP)
    v_hat = v / (1.0 - ADAM_B2 ** ADAM_STEP)
    return -ADAM_LR * (m_hat / (jnp.sqrt(v_hat) + ADAM_EPS) + ADAM_WD * w), m, v


def adamw(w, g, m, v, tr=512):
    rows, cols = w.shape
    tr = _row_chunk(rows, tr, 8)

    def body(w_ref, g_ref, m_ref, v_ref, d_ref, nm_ref, nv_ref):
        d_ref[...], nm_ref[...], nv_ref[...] = _adamw_update(w_ref[...], g_ref[...], m_ref[...], v_ref[...])

    blk = pl.BlockSpec((tr, cols), lambda i: (i, 0))
    return pl.pallas_call(
        body, name="adamw", grid=(rows // tr,), out_shape=(SDS(w.shape, F32),) * 3,
        in_specs=[blk] * 4, out_specs=(blk,) * 3, compiler_params=_cparams(),
    )(w, g, m, v)


def adamw_layer(gf, blk, row_off, n_half, l, w, m, v, outs, deps=()):
    fh = gf.shape[1] // Layout.BLOCKS
    d = gf.shape[2]
    nd = len(deps)

    def body(gf_ref, w_ref, m_ref, v_ref, *rest):
        g_ref, d_ref, nm_ref, nv_ref = rest[4 + nd:]
        g = gf_ref[row_off:row_off + n_half, :]
        g_ref[...] = g
        d_ref[...], nm_ref[...], nv_ref[...] = _adamw_update(w_ref[...], g, m_ref[...], v_ref[...])

    gspec = pl.BlockSpec((None, fh, d), lambda h: (h, blk, 0))
    wspec = pl.BlockSpec((None, n_half, d), lambda h: (l, h, 0))
    return pl.pallas_call(
        body, name="adamw_layer", grid=(2,), out_shape=tuple(SDS(o.shape, o.dtype) for o in outs),
        in_specs=[gspec, wspec, wspec, wspec] + [ANY] * (4 + nd), out_specs=(wspec,) * 4,
        input_output_aliases={4 + i: i for i in range(4)}, compiler_params=_cparams(),
    )(gf, w, m, v, *outs, *deps)


def pack_weight(pos, land, blk, l, w, extra=None, deps=()):
    fh, d = land.shape[2] // Layout.BLOCKS, land.shape[3]
    nd = len(deps)

    def body(pos_ref, w_ref, *rest):
        o_ref = rest[-1]
        a = w_ref[...].astype(BF)
        n = a.shape[0] // 2
        for h in range(2):
            o_ref[h, 0:n, :] = a[h * n:(h + 1) * n]
        if extra is not None:
            b = rest[0][...].astype(BF)
            nb = b.shape[0] // 2
            for h in range(2):
                o_ref[h, n:n + nb, :] = b[h * nb:(h + 1) * nb]

    def whole(a):
        return pl.BlockSpec((None,) + a.shape[1:], lambda i, p: (l, 0, 0))

    ins = [w] + ([extra] if extra is not None else [])
    return pl.pallas_call(
        body, name="pack_weight", out_shape=SDS(land.shape, land.dtype),
        grid_spec=pltpu.PrefetchScalarGridSpec(
            num_scalar_prefetch=1, grid=(1,),
            in_specs=[whole(a) for a in ins] + [ANY] * (1 + nd),
            out_specs=pl.BlockSpec((None, 2, fh, d), lambda i, p: (p[0], 0, blk, 0))),
        input_output_aliases={1 + len(ins): 0}, compiler_params=_cparams(),
    )(pos, *ins, land, *deps)


def _rows_of(shape, width):
    return -(-int(np.prod(shape)) // (SUBLANES * width)) * SUBLANES


def _pack_rows(arrays, width):
    parts = []
    for a in arrays:
        flat = a.reshape(-1).astype(F32)
        r = _rows_of(a.shape, width)
        parts.append(jnp.pad(flat, (0, r * width - flat.shape[0])).reshape(r, width))
    return jnp.concatenate(parts, axis=0)


def _unpack_rows(buf, shapes):
    out, row = [], 0
    for shp in shapes:
        r = _rows_of(shp, buf.shape[1])
        out.append(buf[row:row + r].reshape(-1)[:int(np.prod(shp))].reshape(shp))
        row += r
    return out


def _t5_buckets(rel):
    half = N_BUCKETS // 2
    max_exact = half // 2
    ret = (rel > 0).astype(jnp.int32) * half
    n = jnp.abs(rel)
    n_f = jnp.maximum(n, 1).astype(F32)
    large = max_exact + (jnp.log(n_f / max_exact) / math.log(MAX_DISTANCE / max_exact) * (half - max_exact)).astype(jnp.int32)
    large = jnp.minimum(large, half - 1)
    return ret + jnp.where(n < max_exact, n, large)


def _band_buckets():
    t = jnp.arange(BLOCK)[:, None]
    j = jnp.arange(3 * BLOCK)[None, :]
    rel = j - BLOCK - t
    return _t5_buckets(rel), jnp.abs(rel) <= WINDOW


def _block_diag_pairs(w):
    depth, two, nblk, bw, _ = w.shape
    pairs = w.reshape(depth, two, nblk // 2, 2, bw, bw)
    z = jnp.zeros_like(pairs[:, :, :, 0])
    top = jnp.concatenate([pairs[:, :, :, 0], z], axis=-1)
    bot = jnp.concatenate([z, pairs[:, :, :, 1]], axis=-1)
    return jnp.concatenate([top, bot], axis=-2)


def _diag_blocks(dw):
    bw = dw.shape[-1] // 2
    a = dw[:, :, :bw, :bw]
    b = dw[:, :, bw:, bw:]
    return jnp.stack([a, b], axis=2).reshape(dw.shape[0], 2 * dw.shape[1], bw, bw)


def kernel(x, ffn1_norm, ffn1_w_gate, ffn1_w_up, ffn1_w_down, mix_norm, w_in, conv_w, conv_b, lru_w_a, lru_b_a, lru_w_x, lru_b_x, lru_lambda, attn_sink, rel_bias, lru_out_norm, attn_out_norm, w_out, ffn2_norm, ffn2_w_gate, ffn2_w_up, ffn2_w_down, final_norm, loss_target, m_ffn1_norm, m_ffn1_w_gate, m_ffn1_w_up, m_ffn1_w_down, m_mix_norm, m_w_in, m_conv_w, m_conv_b, m_lru_w_a, m_lru_b_a, m_lru_w_x, m_lru_b_x, m_lru_lambda, m_attn_sink, m_rel_bias, m_lru_out_norm, m_attn_out_norm, m_w_out, m_ffn2_norm, m_ffn2_w_gate, m_ffn2_w_up, m_ffn2_w_down, m_final_norm, v_ffn1_norm, v_ffn1_w_gate, v_ffn1_w_up, v_ffn1_w_down, v_mix_norm, v_w_in, v_conv_w, v_conv_b, v_lru_w_a, v_lru_b_a, v_lru_w_x, v_lru_b_x, v_lru_lambda, v_attn_sink, v_rel_bias, v_lru_out_norm, v_attn_out_norm, v_w_out, v_ffn2_norm, v_ffn2_w_gate, v_ffn2_w_up, v_ffn2_w_down, v_final_norm):
    depth, d = ffn1_norm.shape
    d_ff = N_CHIPS * ffn1_w_gate.shape[2]
    d_in = N_CHIPS * w_in.shape[2]
    lw = conv_b.shape[1]
    att = N_HEADS * HEAD_DIM
    lay = Layout(d, d_ff, d_in)
    k_chip = 2 * lax.axis_index("x") + lax.axis_index("y")
    pos = jnp.stack([k_chip, lax.axis_index("c")]).astype(jnp.int32)

    def rows_major(a):
        return jnp.swapaxes(a, 1, 2)

    mats = (rows_major(ffn1_w_gate), rows_major(ffn1_w_up), ffn1_w_down,
            rows_major(ffn2_w_gate), rows_major(ffn2_w_up), ffn2_w_down)

    def pack_layer(l, deps=()):
        land = lax.empty((N_CHIPS, 2, lay.rows, d), BF)
        for m, a in enumerate(mats):
            land = pack_weight(pos, land, m, l, a, deps=deps if m == 0 else ())
        return pack_weight(pos, land, lay.MIX_BLK, l, rows_major(w_in), extra=w_out)

    def gather_start(l, land):
        return split_start(f"gather_start_{l}", [land], 3, gather_plan)

    def gather_wait(l, started, after):
        ssem, rsem, bufs, _ = started
        return split_wait(f"gather_wait_{l}", ssem, rsem, bufs, after, gather_plan)

    sharded_small = (conv_w, lru_b_a, lru_b_x, lru_lambda)
    sshard = jnp.concatenate([a.reshape(-1, LANE) for a in sharded_small], axis=0)
    sfull = gather_small(sshard)
    small_full, off = [], 0
    for a in sharded_small:
        r = a.shape[0] * a.shape[1]
        piece = sfull[:, off:off + r].reshape((N_CHIPS,) + a.shape)
        small_full.append(jnp.moveaxis(piece, 0, 2).reshape(a.shape[0], a.shape[1], N_CHIPS * LANE))
        off += r
    conv_w_f, b_a_f, b_x_f, lam_f = small_full

    zrow = jnp.zeros((1, lw), F32)
    wblk_a = _block_diag_pairs(lru_w_a)
    wblk_x = _block_diag_pairs(lru_w_x)
    buckets, in_band = _band_buckets()
    onehot = (buckets.reshape(-1)[:, None] == jnp.arange(N_BUCKETS)[None, :]).astype(F32)
    bias = jnp.dot(rel_bias.T, onehot.T, precision=lax.Precision.HIGHEST).reshape(N_HEADS, BLOCK, 3 * BLOCK)
    bias = jnp.where(in_band[None], bias, NEG_INF)
    bias = bias.reshape(N_KV_HEADS, KV_GROUP, BLOCK, 3 * BLOCK).transpose(0, 3, 1, 2).reshape(N_KV_HEADS, 3 * BLOCK, KV_GROUP * BLOCK)
    kblk = 2 * lw // LANE

    def layer_small(l):
        cvec = jnp.concatenate([conv_w_f[l], jnp.zeros((8 - CONV_WIDTH, lw), F32)], axis=0)
        pvec = jnp.concatenate([conv_b[l][None], b_a_f[l], b_x_f[l], lam_f[l], zrow], axis=0)
        wblk = jnp.stack([wblk_a[l, 0], wblk_x[l, 0], wblk_a[l, 1], wblk_x[l, 1]]).astype(BF)
        sink = jnp.broadcast_to(jnp.repeat(attn_sink[l], BLOCK).reshape(N_KV_HEADS, 1, KV_GROUP * BLOCK),
                                (N_KV_HEADS, 8, KV_GROUP * BLOCK))
        return cvec, pvec, wblk, sink

    xs = x[0]
    wfull = [None] * depth
    parts = [(0, 3 * lay.fh), (3 * lay.fh, lay.rows - 3 * lay.fh)]
    plans = [(functools.partial(gather_plan, rows=p), functools.partial(handover_plan, rows=p)) for p in parts]
    land = pack_layer(0, deps=(sfull,))
    first = split_start("gather_start_0a", [land], 3, plans[0][0])
    second = split_start("gather_start_0b", first[2], 3, plans[1][0])
    lands = {l: pack_layer(l, deps=(second[3],)) for l in range(1, depth)}
    land = split_wait("gather_wait_0a", first[0], first[1], second[2], [xs] + list(lands.values()), plans[0][0])
    wfull[0], = exchange_now("gather_handover_0a", land, 3, plans[0][1])
    started = None
    saved = []
    for l in range(depth):
        cvec, pvec, wblk, sink = layer_small(l)
        deps = (started[3],) if started is not None else ()
        x1, gate1, up1 = ffn_forward(xs, ffn1_norm[l][None], wfull[l], lay, 0, deps=deps)
        deps = ()
        if l == 0:
            land = split_wait("gather_wait_0b", second[0], second[1], [wfull[0]], [x1], plans[1][0])
            wfull[0], = exchange_now("gather_handover_0b", land, 3, plans[1][1])
            if depth > 1:
                started = gather_start(1, lands[1])
                deps = (started[3],)
        proj, qt, vt = mix_project(x1, mix_norm[l][None], wfull[l], lay, lw, att)
        y_rec, hs = lru_forward(proj, cvec, pvec, wblk, lw, deps=deps)
        y_att = attention_forward(qt, proj, vt, bias, sink, kblk)
        x2 = mix_output(x1, y_rec, y_att, lru_out_norm[l][None], attn_out_norm[l][:, None], wfull[l], lay)
        deps, handover = (), None
        if 0 < l < depth - 1:
            land, = gather_wait(l + 1, started, [x2])
            started = gather_start(l + 2, lands[l + 2]) if l + 2 < depth else None
            handover = split_start(f"gather_handover_start_{l + 1}", [land], 3, handover_plan)
            deps = (handover[3],) + ((started[3],) if started is not None else ())
        x3, gate2, up2 = ffn_forward(x2, ffn2_norm[l][None], wfull[l], lay, 1, deps=deps)
        saved.append((xs, x1, x2, proj, qt, y_rec, hs, y_att, (gate1, up1), (gate2, up2)))
        xs = x3
        if handover is not None:
            wfull[l + 1], = split_wait(f"gather_handover_wait_{l + 1}", handover[0], handover[1], handover[2], [x3],
                                       handover_plan)
        elif l == 0 and depth > 1:
            wfull[1], = exchange_now("gather_handover_1", gather_wait(1, started, [x3]), 3, handover_plan)
            started = gather_start(2, lands[2]) if depth > 2 else None

    dx, d_final, loss_tile = loss_head(xs, final_norm[None], loss_target[0])
    loss = lax.psum(loss_tile[0, 0], ("x", "y", "c"))

    layer_names = ["ffn1_norm", "mix_norm", "conv_w", "conv_b", "lru_w_a", "lru_b_a", "lru_w_x", "lru_b_x", "lru_lambda",
                   "attn_sink", "lru_out_norm", "attn_out_norm", "ffn2_norm"]
    dbias_total = jnp.zeros(bias.shape, F32)

    def ffn_back(xin, gain, dout, pre, gb, l, which, deps=()):
        dxo, dg, lhs, rhs = ffn_backward_dx(xin, gain, dout, *pre, wfull[l], lay, which, deps=deps)
        return dxo, dg[0], weight_grad_tn(lhs, rhs, gb, lay, 3 * which)

    def pair_start(l, gb, sb):
        lands = [lax.empty((N_CHIPS,) + gb.shape[2:], gb.dtype), lax.empty(sb.shape, sb.dtype)]
        return split_start(f"pair_start_{l}", [gb, sb] + lands, N_CHIPS + 1, pair_plan)

    def reduce_start(l, paired, after):
        gb, sb, p1, sp1 = split_wait(f"pair_wait_{l}", paired[0], paired[1], paired[2], after, pair_plan)
        cs = pair_sum(pos, gb, p1)
        ss = small_pair_sum(sb, sp1)
        lands = [lax.empty((3,) + cs.shape[1:], cs.dtype), lax.empty((N_CHIPS,) + ss.shape, ss.dtype)]
        return split_start(f"reduce_start_{l}", [cs, ss] + lands, 6, reduce_plan)

    def reduce_finish(l, started, after):
        ssem, rsem, bufs, _ = started
        cs, ss, p3, sp3 = split_wait(f"reduce_wait_{l}", ssem, rsem, bufs, after, reduce_plan)
        return chip_sum(pos, cs, p3), small_chip_sum(pos, ss, sp3)

    gf = [None] * depth
    small_sums = [None] * depth
    small_shapes = [None] * depth
    paired = None
    in_flight = None
    finals = {}
    tokens = []
    for l in reversed(range(depth)):
        x0, x1, x2, proj, qt, y_rec, hs, y_att, pre1, pre2 = saved[l]
        cvec, pvec, wblk, sink = layer_small(l)
        gb = lax.empty((N_CHIPS, 2, lay.rows, d), BF)
        part = {}
        dx, part["ffn2_norm"], gb = ffn_back(x2, ffn2_norm[l][None], dx, pre2, gb, l, 1, deps=tuple(tokens))
        deps = ()
        if paired is not None:
            in_flight = (paired[0], reduce_start(paired[0], paired[1], [dx, gb]))
            deps = (in_flight[1][3],)
        dyr, dya, dgr, dga, dwout = mix_output_backward(dx, y_rec, y_att, lru_out_norm[l][None], attn_out_norm[l][:, None],
                                                        wfull[l], lay, deps=deps)
        part["lru_out_norm"] = dgr[0]
        part["attn_out_norm"] = dga[:, 0]
        dq, dkv, dbias, dsink = attention_backward(qt, proj, y_att, dya, bias, sink, kblk)
        dbias_total = dbias_total + dbias
        part["attn_sink"] = jnp.sum(dsink[:, 0, :].reshape(N_HEADS, BLOCK), axis=1)
        dxr, dgt, dcv, dpv, dwb = lru_backward(proj, hs, dyr, cvec, pvec, wblk, lw)
        part["conv_w"] = dcv[:CONV_WIDTH]
        part["conv_b"] = dpv[0]
        part["lru_b_a"] = dpv[1:3]
        part["lru_b_x"] = dpv[3:5]
        part["lru_lambda"] = dpv[5:7]
        part["lru_w_a"] = _diag_blocks(jnp.stack([dwb[0], dwb[2]]))
        part["lru_w_x"] = _diag_blocks(jnp.stack([dwb[1], dwb[3]]))
        dx, dgm, gb = mix_project_backward(x1, mix_norm[l][None], dx, dxr, dgt, dq, dkv, dwout, wfull[l], gb, lay)
        part["mix_norm"] = dgm[0]
        dx, part["ffn1_norm"], gb = ffn_back(x0, ffn1_norm[l][None], dx, pre1, gb, l, 0)
        pieces = [part[n] for n in layer_names]
        if l == 0:
            dbias_heads = dbias_total.reshape(N_KV_HEADS, 3 * BLOCK, KV_GROUP, BLOCK).transpose(0, 2, 3, 1)
            d_rel_bias = jnp.dot(dbias_heads.reshape(N_HEADS, -1), onehot, precision=lax.Precision.HIGHEST).T
            pieces += [d_rel_bias, d_final[0]]
        small_shapes[l] = [p.shape for p in pieces]
        paired = (l, pair_start(l, gb, _pack_rows(pieces, 1024)))
        tokens = [paired[1][3]]
        if in_flight is not None:
            above = in_flight[0]
            half, small_sums[above] = reduce_finish(above, in_flight[1], [dx])
            finals[above] = split_start(f"final_start_{above}", [half], 1, final_plan)
            tokens.append(finals[above][3])
            in_flight = None
    grad_x = dx[None]

    weights = dict(ffn1_norm=ffn1_norm, ffn1_w_gate=ffn1_w_gate, ffn1_w_up=ffn1_w_up, ffn1_w_down=ffn1_w_down, mix_norm=mix_norm, w_in=w_in, conv_w=conv_w, conv_b=conv_b, lru_w_a=lru_w_a, lru_b_a=lru_b_a, lru_w_x=lru_w_x, lru_b_x=lru_b_x, lru_lambda=lru_lambda, attn_sink=attn_sink, rel_bias=rel_bias, lru_out_norm=lru_out_norm, attn_out_norm=attn_out_norm, w_out=w_out, ffn2_norm=ffn2_norm, ffn2_w_gate=ffn2_w_gate, ffn2_w_up=ffn2_w_up, ffn2_w_down=ffn2_w_down, final_norm=final_norm)
    m_in = dict(ffn1_norm=m_ffn1_norm, ffn1_w_gate=m_ffn1_w_gate, ffn1_w_up=m_ffn1_w_up, ffn1_w_down=m_ffn1_w_down, mix_norm=m_mix_norm, w_in=m_w_in, conv_w=m_conv_w, conv_b=m_conv_b, lru_w_a=m_lru_w_a, lru_b_a=m_lru_b_a, lru_w_x=m_lru_w_x, lru_b_x=m_lru_b_x, lru_lambda=m_lru_lambda, attn_sink=m_attn_sink, rel_bias=m_rel_bias, lru_out_norm=m_lru_out_norm, attn_out_norm=m_attn_out_norm, w_out=m_w_out, ffn2_norm=m_ffn2_norm, ffn2_w_gate=m_ffn2_w_gate, ffn2_w_up=m_ffn2_w_up, ffn2_w_down=m_ffn2_w_down, final_norm=m_final_norm)
    v_in = dict(ffn1_norm=v_ffn1_norm, ffn1_w_gate=v_ffn1_w_gate, ffn1_w_up=v_ffn1_w_up, ffn1_w_down=v_ffn1_w_down, mix_norm=v_mix_norm, w_in=v_w_in, conv_w=v_conv_w, conv_b=v_conv_b, lru_w_a=v_lru_w_a, lru_b_a=v_lru_b_a, lru_w_x=v_lru_w_x, lru_b_x=v_lru_b_x, lru_lambda=v_lru_lambda, attn_sink=v_attn_sink, rel_bias=v_rel_bias, lru_out_norm=v_lru_out_norm, attn_out_norm=v_attn_out_norm, w_out=v_w_out, ffn2_norm=v_ffn2_norm, ffn2_w_gate=v_ffn2_w_gate, ffn2_w_up=v_ffn2_w_up, ffn2_w_down=v_ffn2_w_down, final_norm=v_final_norm)
    order = list(weights)
    large = [(name, m, 0, lay.fh, m % 3 != 2) for m, name in
             enumerate(("ffn1_w_gate", "ffn1_w_up", "ffn1_w_down", "ffn2_w_gate", "ffn2_w_up", "ffn2_w_down"))]
    large += [("w_in", lay.MIX_BLK, 0, lay.ih, True), ("w_out", lay.MIX_BLK, lay.ih, lay.oh, False)]
    as_rows = {name: [rows_major(src[name]) if flip else src[name] for src in (weights, m_in, v_in)]
               for name, _, _, _, flip in large}
    stacked = {name: tuple(lax.empty(as_rows[name][0].shape, F32) for _ in range(4)) for name, *_ in large}

    def adamw_large(l, deps=()):
        for i, (name, blk, row_off, n_half, _) in enumerate(large):
            stacked[name] = adamw_layer(gf[l], blk, row_off, n_half, l, *as_rows[name], stacked[name],
                                        deps=deps if i == 0 else ())

    last = paired[0]
    crossing = reduce_start(last, paired[1], [dx])
    for l in sorted(finals):
        gf[l], = split_wait(f"final_wait_{l}", finals[l][0], finals[l][1], finals[l][2], [crossing[3]], final_plan)
        adamw_large(l, deps=(crossing[3],))
    ready = [buf for name, *_ in large for buf in stacked[name]] if depth > 1 else []
    half, small_sums[last] = reduce_finish(last, crossing, [dx] + ready)
    gf[last], = exchange_now(f"final_now_{last}", [half], 1, final_plan)
    adamw_large(last)

    per_layer = [_unpack_rows(small_sums[l], small_shapes[l]) for l in range(depth)]
    grads = {n: jnp.stack([per_layer[l][i] for l in range(depth)]) for i, n in enumerate(layer_names)}
    grads["rel_bias"], grads["final_norm"] = per_layer[0][len(layer_names):]
    for name in ("conv_w", "lru_b_a", "lru_b_x", "lru_lambda"):
        grads[name] = lax.dynamic_slice_in_dim(grads[name], k_chip * LANE, LANE, axis=2)
    delta, new_m, new_v = {}, {}, {}
    for name, _, _, _, flip in large:
        grads[name], delta[name], new_m[name], new_v[name] = [rows_major(a) if flip else a for a in stacked[name]]
    small = [n for n in order if n not in stacked]
    packed = [_pack_rows([src[n] for n in small], 1024) for src in (weights, grads, m_in, v_in)]
    outs = adamw(*packed)
    shapes = [weights[n].shape for n in small]
    for dst, buf in zip((delta, new_m, new_v), outs):
        dst.update(zip(small, _unpack_rows(buf, shapes)))

    return (loss, grad_x, *[grads[n] for n in order], *[delta[n] for n in order],
            *[new_m[n] for n in order], *[new_v[n] for n in order])
```
